```python
import math
import jax, jax.numpy as jnp
from jax import lax
import numpy as np

D_MODEL = 2048
BATCH = 8
SEQ = 8192
DEPTH = 1

D_FF = 5504
SSD_D_INNER = 2048
SSD_HEAD_DIM = 64
SSD_HEADS = SSD_D_INNER // SSD_HEAD_DIM
SSD_GROUPS = 4
SSD_HEADS_PER_GROUP = SSD_HEADS // SSD_GROUPS
SSD_STATE = 128
SSD_CONV = 4
SSD_CHUNK = 128
SSD_CONV_DIM = SSD_D_INNER + 2 * SSD_GROUPS * SSD_STATE
S5_WIDTH = 1024
S5_GROUP_SIZE = 16
S5_GROUPS = S5_WIDTH // S5_GROUP_SIZE
S5_STATE = 64
OFF_XBC = SSD_D_INNER
OFF_DT = OFF_XBC + SSD_CONV_DIM
OFF_U = OFF_DT + SSD_HEADS
OFF_GATES = OFF_U + S5_WIDTH
IN_COLS = OFF_GATES + 2 * D_MODEL
N_ADA = 9
EPS = 1e-6

kernel_name = "hybrid_ssd_s5_gated_macaron_block"


def rms_norm(x, g):
    xf = x.astype(jnp.float32)
    y = xf * lax.rsqrt(jnp.mean(xf * xf, axis=-1, keepdims=True) + EPS)
    return (y * g.astype(jnp.float32)).astype(x.dtype)


def modulate(x, g, shift, scale):
    return rms_norm(x, g) * (1 + scale) + shift


def swiglu(h, w_in, w_out):
    a, b = jnp.split(h @ w_in, 2, axis=-1)
    return (jax.nn.silu(a) * b) @ w_out


def causal_depthwise_conv(x, w, b):
    k_w = w.shape[0]
    s = x.shape[1]
    xp = jnp.pad(x, ((0, 0), (k_w - 1, 0), (0, 0)))
    out = b
    for k in range(k_w):
        out = out + xp[:, k:k + s] * w[k]
    return out


def ssd_mixer(xbc, z, dt_raw, conv_w, conv_b, dt_bias, a_log, d_skip, norm_w):
    f32 = jnp.float32
    bsz, s, _ = xbc.shape
    nc = s // SSD_CHUNK
    G, R, P, N, L = SSD_GROUPS, SSD_HEADS_PER_GROUP, SSD_HEAD_DIM, SSD_STATE, SSD_CHUNK
    xbc = jax.nn.silu(causal_depthwise_conv(xbc, conv_w, conv_b)).astype(f32)
    xs, bm, cm = jnp.split(xbc, [SSD_D_INNER, SSD_D_INNER + G * N], axis=-1)
    xs = xs.reshape(bsz, nc, L, G, R, P)
    bm = bm.reshape(bsz, nc, L, G, N)
    cm = cm.reshape(bsz, nc, L, G, N)
    dt = jax.nn.softplus(dt_raw.astype(f32) + dt_bias.astype(f32)).reshape(bsz, nc, L, G, R)
    a = -jnp.exp(a_log.astype(f32)).reshape(G, R)
    xdt = xs * dt[..., None]
    da_cs = jnp.cumsum(jnp.moveaxis(dt * a, 2, -1), axis=-1)
    causal = jnp.tril(jnp.ones((L, L), dtype=bool))
    seg = da_cs[..., :, None] - da_cs[..., None, :]
    decay_ls = jnp.exp(jnp.where(causal, seg, -jnp.inf))
    cb = jnp.einsum('bclgn,bcsgn->bcgls', cm, bm)
    y_diag = jnp.einsum('bcgls,bcgrls,bcsgrp->bclgrp', cb, decay_ls, xdt)
    decay_to_end = jnp.exp(da_cs[..., -1:] - da_cs)
    states = jnp.einsum('bclgn,bcgrl,bclgrp->bcgrpn', bm, decay_to_end, xdt)
    chunk_decay = jnp.exp(da_cs[..., -1])

    def step(h, inp):
        st, dec = inp
        return dec[..., None, None] * h + st, h

    h0 = jnp.zeros((bsz, G, R, P, N), f32)
    _, prev = lax.scan(step, h0, (jnp.moveaxis(states, 1, 0), jnp.moveaxis(chunk_decay, 1, 0)))
    prev = jnp.moveaxis(prev, 0, 1)
    y_off = jnp.einsum('bclgn,bcgrpn,bcgrl->bclgrp', cm, prev, jnp.exp(da_cs))
    y = y_diag + y_off + d_skip.astype(f32).reshape(G, R)[:, :, None] * xs
    y = y.reshape(bsz, s, G, R * P)
    yz = y * jax.nn.silu(z.astype(f32)).reshape(bsz, s, G, R * P)
    yz = yz * lax.rsqrt(jnp.mean(yz * yz, axis=-1, keepdims=True) + EPS)
    return (yz.reshape(bsz, s, SSD_D_INNER) * norm_w.astype(f32)).astype(z.dtype)


def s5_mixer(u, lambda_re, lambda_im, b_re, b_im, c_re, c_im, d_skip, log_dt):
    f32 = jnp.float32
    bsz, s, _ = u.shape
    uf = u.astype(f32).reshape(bsz, s, S5_GROUPS, S5_GROUP_SIZE)
    dt = jnp.exp(log_dt.astype(f32))[:, None]
    lr = jnp.minimum(lambda_re.astype(f32), -1e-4)
    li = lambda_im.astype(f32)
    mag = jnp.exp(lr * dt)
    ar = mag * jnp.cos(li * dt)
    ai = mag * jnp.sin(li * dt)
    den = lr * lr + li * li
    nr = ar - 1.0
    kr = (nr * lr + ai * li) / den
    ki = (ai * lr - nr * li) / den
    br = b_re.astype(f32)
    bi = b_im.astype(f32)
    bbar_re = kr[..., None] * br - ki[..., None] * bi
    bbar_im = kr[..., None] * bi + ki[..., None] * br
    bu_re = jnp.einsum('bsgi,gpi->bsgp', uf, bbar_re)
    bu_im = jnp.einsum('bsgi,gpi->bsgp', uf, bbar_im)
    a_re = jnp.broadcast_to(ar, bu_re.shape)
    a_im = jnp.broadcast_to(ai, bu_im.shape)

    def combine(e1, e2):
        a1r, a1i, b1r, b1i = e1
        a2r, a2i, b2r, b2i = e2
        return (a2r * a1r - a2i * a1i,
                a2r * a1i + a2i * a1r,
                a2r * b1r - a2i * b1i + b2r,
                a2r * b1i + a2i * b1r + b2i)

    _, _, s_re, s_im = lax.associative_scan(combine, (a_re, a_im, bu_re, bu_im), axis=1)
    y = (jnp.einsum('bsgp,gip->bsgi', s_re, c_re.astype(f32))
         - jnp.einsum('bsgp,gip->bsgi', s_im, c_im.astype(f32))
         + d_skip.astype(f32) * uf)
    return y.reshape(bsz, s, S5_WIDTH).astype(u.dtype)


def _fwd_setup_inputs(seed: int = 0) -> dict:
    key = jax.random.key(seed)
    ks = jax.random.split(key, 40)
    f32 = jnp.float32

    def nrm(k, shape, scale):
        return jax.random.normal(k, shape, f32) * scale

    dt_ssd = jnp.exp(jax.random.uniform(ks[11], (DEPTH, SSD_HEADS), f32, math.log(1e-3), math.log(1e-1)))
    lam_im = (jnp.pi * jnp.arange(S5_STATE, dtype=f32))[None, None, :] + nrm(ks[16], (DEPTH, S5_GROUPS, S5_STATE), 0.01)
    return {
        "x": nrm(ks[0], (BATCH, SEQ, D_MODEL), 1.0),
        "c": nrm(ks[1], (BATCH, D_MODEL), 1.0),
        "w_ada": nrm(ks[2], (DEPTH, D_MODEL, N_ADA * D_MODEL), D_MODEL ** -0.5),
        "b_ada": nrm(ks[3], (DEPTH, N_ADA * D_MODEL), 0.01),
        "norm_ffn1": 1.0 + nrm(ks[4], (DEPTH, D_MODEL), 0.02),
        "w_ffn1_in": nrm(ks[5], (DEPTH, D_MODEL, 2 * D_FF), D_MODEL ** -0.5),
        "w_ffn1_out": nrm(ks[6], (DEPTH, D_FF, D_MODEL), D_FF ** -0.5),
        "norm_mix": 1.0 + nrm(ks[7], (DEPTH, D_MODEL), 0.02),
        "w_in": nrm(ks[8], (DEPTH, D_MODEL, IN_COLS), D_MODEL ** -0.5),
        "conv_w": nrm(ks[9], (DEPTH, SSD_CONV, SSD_CONV_DIM), SSD_CONV ** -0.5),
        "conv_b": nrm(ks[10], (DEPTH, SSD_CONV_DIM), 0.01),
        "dt_bias": dt_ssd + jnp.log(-jnp.expm1(-dt_ssd)),
        "a_log": jnp.log(jax.random.uniform(ks[12], (DEPTH, SSD_HEADS), f32, 1.0, 16.0)),
        "d_ssd": 1.0 + nrm(ks[13], (DEPTH, SSD_HEADS), 0.02),
        "ssd_norm_w": 1.0 + nrm(ks[14], (DEPTH, SSD_D_INNER), 0.02),
        "w_a_proj": nrm(ks[15], (DEPTH, SSD_D_INNER, D_MODEL), SSD_D_INNER ** -0.5),
        "s5_lambda_re": -0.5 + nrm(ks[17], (DEPTH, S5_GROUPS, S5_STATE), 0.01),
        "s5_lambda_im": lam_im,
        "s5_b_re": nrm(ks[18], (DEPTH, S5_GROUPS, S5_STATE, S5_GROUP_SIZE), (2 * S5_GROUP_SIZE) ** -0.5),
        "s5_b_im": nrm(ks[19], (DEPTH, S5_GROUPS, S5_STATE, S5_GROUP_SIZE), (2 * S5_GROUP_SIZE) ** -0.5),
        "s5_c_re": nrm(ks[20], (DEPTH, S5_GROUPS, S5_GROUP_SIZE, S5_STATE), (2 * S5_STATE) ** -0.5),
        "s5_c_im": nrm(ks[21], (DEPTH, S5_GROUPS, S5_GROUP_SIZE, S5_STATE), (2 * S5_STATE) ** -0.5),
        "s5_d": nrm(ks[22], (DEPTH, S5_GROUPS, S5_GROUP_SIZE), 1.0),
        "s5_log_dt": jax.random.uniform(ks[23], (DEPTH, S5_GROUPS), f32, math.log(1e-3), math.log(1e-1)),
        "w_b_glu": nrm(ks[24], (DEPTH, S5_WIDTH, 2 * D_MODEL), S5_WIDTH ** -0.5),
        "w_out": nrm(ks[25], (DEPTH, D_MODEL, D_MODEL), D_MODEL ** -0.5),
        "norm_ffn2": 1.0 + nrm(ks[26], (DEPTH, D_MODEL), 0.02),
        "w_ffn2_in": nrm(ks[27], (DEPTH, D_MODEL, 2 * D_FF), D_MODEL ** -0.5),
        "w_ffn2_out": nrm(ks[28], (DEPTH, D_FF, D_MODEL), D_FF ** -0.5),
        "norm_final": 1.0 + nrm(ks[29], (D_MODEL,), 0.02),
    }


def _fwd_reference(x, c, w_ada, b_ada, norm_ffn1, w_ffn1_in, w_ffn1_out, norm_mix, w_in,
              conv_w, conv_b, dt_bias, a_log, d_ssd, ssd_norm_w, w_a_proj,
              s5_lambda_re, s5_lambda_im, s5_b_re, s5_b_im, s5_c_re, s5_c_im, s5_d, s5_log_dt,
              w_b_glu, w_out, norm_ffn2, w_ffn2_in, w_ffn2_out, norm_final):
    c_act = jax.nn.silu(c)
    for l in range(DEPTH):
        mods = (c_act @ w_ada[l] + b_ada[l])[:, None, :]
        (sh1, sc1, g1, sh2, sc2, g2, sh3, sc3, g3) = jnp.split(mods, N_ADA, axis=-1)

        h = modulate(x, norm_ffn1[l], sh1, sc1)
        x = x + 0.5 * g1 * swiglu(h, w_ffn1_in[l], w_ffn1_out[l])

        h = modulate(x, norm_mix[l], sh2, sc2)
        proj = h @ w_in[l]
        z, xbc, dt_raw, u, gates = jnp.split(proj, [OFF_XBC, OFF_DT, OFF_U, OFF_GATES], axis=-1)
        gate_a, gate_b = jnp.split(gates, 2, axis=-1)

        y_a = ssd_mixer(xbc, z, dt_raw, conv_w[l], conv_b[l], dt_bias[l], a_log[l], d_ssd[l], ssd_norm_w[l])
        p_a = y_a @ w_a_proj[l]

        y_b = s5_mixer(u, s5_lambda_re[l], s5_lambda_im[l], s5_b_re[l], s5_b_im[l],
                       s5_c_re[l], s5_c_im[l], s5_d[l], s5_log_dt[l])
        glu_a, glu_g = jnp.split(jax.nn.gelu(y_b) @ w_b_glu[l], 2, axis=-1)
        p_b = glu_a * jax.nn.sigmoid(glu_g)

        merged = jax.nn.sigmoid(gate_a) * p_a + jax.nn.sigmoid(gate_b) * p_b
        x = x + g2 * (merged @ w_out[l])

        h = modulate(x, norm_ffn2[l], sh3, sc3)
        x = x + 0.5 * g3 * swiglu(h, w_ffn2_in[l], w_ffn2_out[l])
    return rms_norm(x, norm_final)


import jax as _jax
import jax.numpy as _jnp

TWIN_FORMAT = 'train_step'
FWD_PARAMS = ['x', 'c', 'w_ada', 'b_ada', 'norm_ffn1', 'w_ffn1_in', 'w_ffn1_out', 'norm_mix', 'w_in', 'conv_w', 'conv_b', 'dt_bias', 'a_log', 'd_ssd', 'ssd_norm_w', 'w_a_proj', 's5_lambda_re', 's5_lambda_im', 's5_b_re', 's5_b_im', 's5_c_re', 's5_c_im', 's5_d', 's5_log_dt', 'w_b_glu', 'w_out', 'norm_ffn2', 'w_ffn2_in', 'w_ffn2_out', 'norm_final']
TWIN_WEIGHTS = ['w_ada', 'b_ada', 'norm_ffn1', 'w_ffn1_in', 'w_ffn1_out', 'norm_mix', 'w_in', 'conv_w', 'conv_b', 'dt_bias', 'a_log', 'd_ssd', 'ssd_norm_w', 'w_a_proj', 's5_lambda_re', 's5_lambda_im', 's5_b_re', 's5_b_im', 's5_c_re', 's5_c_im', 's5_d', 's5_log_dt', 'w_b_glu', 'w_out', 'norm_ffn2', 'w_ffn2_in', 'w_ffn2_out', 'norm_final']
TWIN_DIFF_INPUT = 'x'
TWIN_INPUTS = ['x', 'c', 'w_ada', 'b_ada', 'norm_ffn1', 'w_ffn1_in', 'w_ffn1_out', 'norm_mix', 'w_in', 'conv_w', 'conv_b', 'dt_bias', 'a_log', 'd_ssd', 'ssd_norm_w', 'w_a_proj', 's5_lambda_re', 's5_lambda_im', 's5_b_re', 's5_b_im', 's5_c_re', 's5_c_im', 's5_d', 's5_log_dt', 'w_b_glu', 'w_out', 'norm_ffn2', 'w_ffn2_in', 'w_ffn2_out', 'norm_final', 'loss_target', 'm_w_ada', 'm_b_ada', 'm_norm_ffn1', 'm_w_ffn1_in', 'm_w_ffn1_out', 'm_norm_mix', 'm_w_in', 'm_conv_w', 'm_conv_b', 'm_dt_bias', 'm_a_log', 'm_d_ssd', 'm_ssd_norm_w', 'm_w_a_proj', 'm_s5_lambda_re', 'm_s5_lambda_im', 'm_s5_b_re', 'm_s5_b_im', 'm_s5_c_re', 'm_s5_c_im', 'm_s5_d', 'm_s5_log_dt', 'm_w_b_glu', 'm_w_out', 'm_norm_ffn2', 'm_w_ffn2_in', 'm_w_ffn2_out', 'm_norm_final', 'v_w_ada', 'v_b_ada', 'v_norm_ffn1', 'v_w_ffn1_in', 'v_w_ffn1_out', 'v_norm_mix', 'v_w_in', 'v_conv_w', 'v_conv_b', 'v_dt_bias', 'v_a_log', 'v_d_ssd', 'v_ssd_norm_w', 'v_w_a_proj', 'v_s5_lambda_re', 'v_s5_lambda_im', 'v_s5_b_re', 'v_s5_b_im', 'v_s5_c_re', 'v_s5_c_im', 'v_s5_d', 'v_s5_log_dt', 'v_w_b_glu', 'v_w_out', 'v_norm_ffn2', 'v_w_ffn2_in', 'v_w_ffn2_out', 'v_norm_final']
TWIN_OUTPUTS = ['loss', 'grad_x', 'grad_w_ada', 'grad_b_ada', 'grad_norm_ffn1', 'grad_w_ffn1_in', 'grad_w_ffn1_out', 'grad_norm_mix', 'grad_w_in', 'grad_conv_w', 'grad_conv_b', 'grad_dt_bias', 'grad_a_log', 'grad_d_ssd', 'grad_ssd_norm_w', 'grad_w_a_proj', 'grad_s5_lambda_re', 'grad_s5_lambda_im', 'grad_s5_b_re', 'grad_s5_b_im', 'grad_s5_c_re', 'grad_s5_c_im', 'grad_s5_d', 'grad_s5_log_dt', 'grad_w_b_glu', 'grad_w_out', 'grad_norm_ffn2', 'grad_w_ffn2_in', 'grad_w_ffn2_out', 'grad_norm_final', 'delta_w_ada', 'delta_b_ada', 'delta_norm_ffn1', 'delta_w_ffn1_in', 'delta_w_ffn1_out', 'delta_norm_mix', 'delta_w_in', 'delta_conv_w', 'delta_conv_b', 'delta_dt_bias', 'delta_a_log', 'delta_d_ssd', 'delta_ssd_norm_w', 'delta_w_a_proj', 'delta_s5_lambda_re', 'delta_s5_lambda_im', 'delta_s5_b_re', 'delta_s5_b_im', 'delta_s5_c_re', 'delta_s5_c_im', 'delta_s5_d', 'delta_s5_log_dt', 'delta_w_b_glu', 'delta_w_out', 'delta_norm_ffn2', 'delta_w_ffn2_in', 'delta_w_ffn2_out', 'delta_norm_final', 'new_m_w_ada', 'new_m_b_ada', 'new_m_norm_ffn1', 'new_m_w_ffn1_in', 'new_m_w_ffn1_out', 'new_m_norm_mix', 'new_m_w_in', 'new_m_conv_w', 'new_m_conv_b', 'new_m_dt_bias', 'new_m_a_log', 'new_m_d_ssd', 'new_m_ssd_norm_w', 'new_m_w_a_proj', 'new_m_s5_lambda_re', 'new_m_s5_lambda_im', 'new_m_s5_b_re', 'new_m_s5_b_im', 'new_m_s5_c_re', 'new_m_s5_c_im', 'new_m_s5_d', 'new_m_s5_log_dt', 'new_m_w_b_glu', 'new_m_w_out', 'new_m_norm_ffn2', 'new_m_w_ffn2_in', 'new_m_w_ffn2_out', 'new_m_norm_final', 'new_v_w_ada', 'new_v_b_ada', 'new_v_norm_ffn1', 'new_v_w_ffn1_in', 'new_v_w_ffn1_out', 'new_v_norm_mix', 'new_v_w_in', 'new_v_conv_w', 'new_v_conv_b', 'new_v_dt_bias', 'new_v_a_log', 'new_v_d_ssd', 'new_v_ssd_norm_w', 'new_v_w_a_proj', 'new_v_s5_lambda_re', 'new_v_s5_lambda_im', 'new_v_s5_b_re', 'new_v_s5_b_im', 'new_v_s5_c_re', 'new_v_s5_c_im', 'new_v_s5_d', 'new_v_s5_log_dt', 'new_v_w_b_glu', 'new_v_w_out', 'new_v_norm_ffn2', 'new_v_w_ffn2_in', 'new_v_w_ffn2_out', 'new_v_norm_final']
TWIN_LEAF_KINDS = {'loss': 'loss', 'grad_x': 'grad_x', 'grad_w_ada': 'grad_w', 'grad_b_ada': 'grad_w', 'grad_norm_ffn1': 'grad_w', 'grad_w_ffn1_in': 'grad_w', 'grad_w_ffn1_out': 'grad_w', 'grad_norm_mix': 'grad_w', 'grad_w_in': 'grad_w', 'grad_conv_w': 'grad_w', 'grad_conv_b': 'grad_w', 'grad_dt_bias': 'grad_w', 'grad_a_log': 'grad_w', 'grad_d_ssd': 'grad_w', 'grad_ssd_norm_w': 'grad_w', 'grad_w_a_proj': 'grad_w', 'grad_s5_lambda_re': 'grad_w', 'grad_s5_lambda_im': 'grad_w', 'grad_s5_b_re': 'grad_w', 'grad_s5_b_im': 'grad_w', 'grad_s5_c_re': 'grad_w', 'grad_s5_c_im': 'grad_w', 'grad_s5_d': 'grad_w', 'grad_s5_log_dt': 'grad_w', 'grad_w_b_glu': 'grad_w', 'grad_w_out': 'grad_w', 'grad_norm_ffn2': 'grad_w', 'grad_w_ffn2_in': 'grad_w', 'grad_w_ffn2_out': 'grad_w', 'grad_norm_final': 'grad_w', 'delta_w_ada': 'delta_w', 'delta_b_ada': 'delta_w', 'delta_norm_ffn1': 'delta_w', 'delta_w_ffn1_in': 'delta_w', 'delta_w_ffn1_out': 'delta_w', 'delta_norm_mix': 'delta_w', 'delta_w_in': 'delta_w', 'delta_conv_w': 'delta_w', 'delta_conv_b': 'delta_w', 'delta_dt_bias': 'delta_w', 'delta_a_log': 'delta_w', 'delta_d_ssd': 'delta_w', 'delta_ssd_norm_w': 'delta_w', 'delta_w_a_proj': 'delta_w', 'delta_s5_lambda_re': 'delta_w', 'delta_s5_lambda_im': 'delta_w', 'delta_s5_b_re': 'delta_w', 'delta_s5_b_im': 'delta_w', 'delta_s5_c_re': 'delta_w', 'delta_s5_c_im': 'delta_w', 'delta_s5_d': 'delta_w', 'delta_s5_log_dt': 'delta_w', 'delta_w_b_glu': 'delta_w', 'delta_w_out': 'delta_w', 'delta_norm_ffn2': 'delta_w', 'delta_w_ffn2_in': 'delta_w', 'delta_w_ffn2_out': 'delta_w', 'delta_norm_final': 'delta_w', 'new_m_w_ada': 'new_m', 'new_m_b_ada': 'new_m', 'new_m_norm_ffn1': 'new_m', 'new_m_w_ffn1_in': 'new_m', 'new_m_w_ffn1_out': 'new_m', 'new_m_norm_mix': 'new_m', 'new_m_w_in': 'new_m', 'new_m_conv_w': 'new_m', 'new_m_conv_b': 'new_m', 'new_m_dt_bias': 'new_m', 'new_m_a_log': 'new_m', 'new_m_d_ssd': 'new_m', 'new_m_ssd_norm_w': 'new_m', 'new_m_w_a_proj': 'new_m', 'new_m_s5_lambda_re': 'new_m', 'new_m_s5_lambda_im': 'new_m', 'new_m_s5_b_re': 'new_m', 'new_m_s5_b_im': 'new_m', 'new_m_s5_c_re': 'new_m', 'new_m_s5_c_im': 'new_m', 'new_m_s5_d': 'new_m', 'new_m_s5_log_dt': 'new_m', 'new_m_w_b_glu': 'new_m', 'new_m_w_out': 'new_m', 'new_m_norm_ffn2': 'new_m', 'new_m_w_ffn2_in': 'new_m', 'new_m_w_ffn2_out': 'new_m', 'new_m_norm_final': 'new_m', 'new_v_w_ada': 'new_v', 'new_v_b_ada': 'new_v', 'new_v_norm_ffn1': 'new_v', 'new_v_w_ffn1_in': 'new_v', 'new_v_w_ffn1_out': 'new_v', 'new_v_norm_mix': 'new_v', 'new_v_w_in': 'new_v', 'new_v_conv_w': 'new_v', 'new_v_conv_b': 'new_v', 'new_v_dt_bias': 'new_v', 'new_v_a_log': 'new_v', 'new_v_d_ssd': 'new_v', 'new_v_ssd_norm_w': 'new_v', 'new_v_w_a_proj': 'new_v', 'new_v_s5_lambda_re': 'new_v', 'new_v_s5_lambda_im': 'new_v', 'new_v_s5_b_re': 'new_v', 'new_v_s5_b_im': 'new_v', 'new_v_s5_c_re': 'new_v', 'new_v_s5_c_im': 'new_v', 'new_v_s5_d': 'new_v', 'new_v_s5_log_dt': 'new_v', 'new_v_w_b_glu': 'new_v', 'new_v_w_out': 'new_v', 'new_v_norm_ffn2': 'new_v', 'new_v_w_ffn2_in': 'new_v', 'new_v_w_ffn2_out': 'new_v', 'new_v_norm_final': 'new_v'}


def _forward(args):
    return _fwd_reference(*[args[k] for k in FWD_PARAMS])


def _output_shape():
    def fwd():
        inp = _fwd_setup_inputs(0)
        return _fwd_reference(*[inp[k] for k in FWD_PARAMS])
    out = _jax.eval_shape(fwd)
    return out.shape, out.dtype

N_MICROBATCH = 1
ADAM_LR = 0.001
ADAM_B1 = 0.9
ADAM_B2 = 0.999
ADAM_EPS = 1e-08
ADAM_WD = 0.01
ADAM_STEP = 10
PER_EXAMPLE_BATCH_AXIS = {'x': 0, 'c': 0, 'loss_target': 0}
SHARED_INPUTS = []
_WEIGHT_DTYPES = {'w_ada': _jnp.float32, 'b_ada': _jnp.float32, 'norm_ffn1': _jnp.float32, 'w_ffn1_in': _jnp.float32, 'w_ffn1_out': _jnp.float32, 'norm_mix': _jnp.float32, 'w_in': _jnp.float32, 'conv_w': _jnp.float32, 'conv_b': _jnp.float32, 'dt_bias': _jnp.float32, 'a_log': _jnp.float32, 'd_ssd': _jnp.float32, 'ssd_norm_w': _jnp.float32, 'w_a_proj': _jnp.float32, 's5_lambda_re': _jnp.float32, 's5_lambda_im': _jnp.float32, 's5_b_re': _jnp.float32, 's5_b_im': _jnp.float32, 's5_c_re': _jnp.float32, 's5_c_im': _jnp.float32, 's5_d': _jnp.float32, 's5_log_dt': _jnp.float32, 'w_b_glu': _jnp.float32, 'w_out': _jnp.float32, 'norm_ffn2': _jnp.float32, 'w_ffn2_in': _jnp.float32, 'w_ffn2_out': _jnp.float32, 'norm_final': _jnp.float32}
MOMENT_SCALE = {'w_ada': 3.373589e-02, 'b_ada': 5.792609e-02, 'norm_ffn1': 5.449071e-02, 'w_ffn1_in': 2.566020e-02, 'w_ffn1_out': 4.145442e-02, 'norm_mix': 5.847965e-02, 'w_in': 2.987774e-02, 'conv_w': 3.399991e-02, 'conv_b': 3.140546e-02, 'dt_bias': 1.201983e-01, 'a_log': 1.770045e-01, 'd_ssd': 1.727411e-01, 'ssd_norm_w': 3.933039e-02, 'w_a_proj': 3.897699e-02, 's5_lambda_re': 4.646916e-03, 's5_lambda_im': 3.199024e-03, 's5_b_re': 1.420659e-03, 's5_b_im': 1.658306e-03, 's5_c_re': 3.264457e-03, 's5_c_im': 2.888380e-03, 's5_d': 3.184844e-02, 's5_log_dt': 6.509155e-01, 'w_b_glu': 1.424623e-02, 'w_out': 4.374507e-02, 'norm_ffn2': 4.515576e-02, 'w_ffn2_in': 2.171700e-02, 'w_ffn2_out': 3.512927e-02, 'norm_final': 3.206585e+01}


def _to_microbatches(a, axis):
    t = _jnp.moveaxis(a, axis, 0)
    t = t.reshape((N_MICROBATCH, t.shape[0] // N_MICROBATCH) + t.shape[1:])
    return _jnp.moveaxis(t, 1, axis + 1)


def setup_inputs(seed: int = 0) -> dict:
    inp = _fwd_setup_inputs(seed)
    key = _jax.random.fold_in(_jax.random.key(seed), 7919)
    shape, _ = _output_shape()
    out = dict(inp)
    out["loss_target"] = _jax.random.normal(_jax.random.fold_in(key, 0), shape, _jnp.float32)
    for i, name in enumerate(TWIN_WEIGHTS):
        w = inp[name].astype(_jnp.float32)
        if MOMENT_SCALE is None:
            s = _jnp.sqrt(_jnp.mean(_jnp.square(w)) + 1e-30)
        else:
            s = MOMENT_SCALE[name]
        km, kv = _jax.random.split(_jax.random.fold_in(key, i + 1))
        out[name] = w
        out["m_" + name] = s * _jax.random.normal(km, w.shape, _jnp.float32)
        out["v_" + name] = (s * s) * _jax.random.uniform(kv, w.shape, _jnp.float32, 0.5, 1.5)
    if N_MICROBATCH > 1:
        for name, axis in PER_EXAMPLE_BATCH_AXIS.items():
            out[name] = _to_microbatches(out[name], axis)
    return {'x': out['x'], 'c': out['c'], 'w_ada': out['w_ada'], 'b_ada': out['b_ada'], 'norm_ffn1': out['norm_ffn1'], 'w_ffn1_in': out['w_ffn1_in'], 'w_ffn1_out': out['w_ffn1_out'], 'norm_mix': out['norm_mix'], 'w_in': out['w_in'], 'conv_w': out['conv_w'], 'conv_b': out['conv_b'], 'dt_bias': out['dt_bias'], 'a_log': out['a_log'], 'd_ssd': out['d_ssd'], 'ssd_norm_w': out['ssd_norm_w'], 'w_a_proj': out['w_a_proj'], 's5_lambda_re': out['s5_lambda_re'], 's5_lambda_im': out['s5_lambda_im'], 's5_b_re': out['s5_b_re'], 's5_b_im': out['s5_b_im'], 's5_c_re': out['s5_c_re'], 's5_c_im': out['s5_c_im'], 's5_d': out['s5_d'], 's5_log_dt': out['s5_log_dt'], 'w_b_glu': out['w_b_glu'], 'w_out': out['w_out'], 'norm_ffn2': out['norm_ffn2'], 'w_ffn2_in': out['w_ffn2_in'], 'w_ffn2_out': out['w_ffn2_out'], 'norm_final': out['norm_final'], 'loss_target': out['loss_target'], 'm_w_ada': out['m_w_ada'], 'm_b_ada': out['m_b_ada'], 'm_norm_ffn1': out['m_norm_ffn1'], 'm_w_ffn1_in': out['m_w_ffn1_in'], 'm_w_ffn1_out': out['m_w_ffn1_out'], 'm_norm_mix': out['m_norm_mix'], 'm_w_in': out['m_w_in'], 'm_conv_w': out['m_conv_w'], 'm_conv_b': out['m_conv_b'], 'm_dt_bias': out['m_dt_bias'], 'm_a_log': out['m_a_log'], 'm_d_ssd': out['m_d_ssd'], 'm_ssd_norm_w': out['m_ssd_norm_w'], 'm_w_a_proj': out['m_w_a_proj'], 'm_s5_lambda_re': out['m_s5_lambda_re'], 'm_s5_lambda_im': out['m_s5_lambda_im'], 'm_s5_b_re': out['m_s5_b_re'], 'm_s5_b_im': out['m_s5_b_im'], 'm_s5_c_re': out['m_s5_c_re'], 'm_s5_c_im': out['m_s5_c_im'], 'm_s5_d': out['m_s5_d'], 'm_s5_log_dt': out['m_s5_log_dt'], 'm_w_b_glu': out['m_w_b_glu'], 'm_w_out': out['m_w_out'], 'm_norm_ffn2': out['m_norm_ffn2'], 'm_w_ffn2_in': out['m_w_ffn2_in'], 'm_w_ffn2_out': out['m_w_ffn2_out'], 'm_norm_final': out['m_norm_final'], 'v_w_ada': out['v_w_ada'], 'v_b_ada': out['v_b_ada'], 'v_norm_ffn1': out['v_norm_ffn1'], 'v_w_ffn1_in': out['v_w_ffn1_in'], 'v_w_ffn1_out': out['v_w_ffn1_out'], 'v_norm_mix': out['v_norm_mix'], 'v_w_in': out['v_w_in'], 'v_conv_w': out['v_conv_w'], 'v_conv_b': out['v_conv_b'], 'v_dt_bias': out['v_dt_bias'], 'v_a_log': out['v_a_log'], 'v_d_ssd': out['v_d_ssd'], 'v_ssd_norm_w': out['v_ssd_norm_w'], 'v_w_a_proj': out['v_w_a_proj'], 'v_s5_lambda_re': out['v_s5_lambda_re'], 'v_s5_lambda_im': out['v_s5_lambda_im'], 'v_s5_b_re': out['v_s5_b_re'], 'v_s5_b_im': out['v_s5_b_im'], 'v_s5_c_re': out['v_s5_c_re'], 'v_s5_c_im': out['v_s5_c_im'], 'v_s5_d': out['v_s5_d'], 'v_s5_log_dt': out['v_s5_log_dt'], 'v_w_b_glu': out['v_w_b_glu'], 'v_w_out': out['v_w_out'], 'v_norm_ffn2': out['v_norm_ffn2'], 'v_w_ffn2_in': out['v_w_ffn2_in'], 'v_w_ffn2_out': out['v_w_ffn2_out'], 'v_norm_final': out['v_norm_final']}


def _loss(weights, diff, rest, loss_target):
    with _jax.named_scope("forward"):
        args = {**rest, TWIN_DIFF_INPUT: diff, **{k: w.astype(_WEIGHT_DTYPES[k]) for k, w in weights.items()}}
        y = _forward(args)
    with _jax.named_scope("loss_head"):
        err = _jnp.square(y.astype(_jnp.float32) - loss_target)
        return 0.5 * _jnp.sum(_jnp.mean(err, axis=-1)) if err.ndim else 0.5 * err


def _adamw(w, g, m, v):
    m = ADAM_B1 * m + (1.0 - ADAM_B1) * g
    v = ADAM_B2 * v + (1.0 - ADAM_B2) * _jnp.square(g)
    m_hat = m / (1.0 - ADAM_B1 ** ADAM_STEP)
    v_hat = v / (1.0 - ADAM_B2 ** ADAM_STEP)
    delta = -ADAM_LR * (m_hat / (_jnp.sqrt(v_hat) + ADAM_EPS) + ADAM_WD * w)
    return delta, m, v


def reference(x, c, w_ada, b_ada, norm_ffn1, w_ffn1_in, w_ffn1_out, norm_mix, w_in, conv_w, conv_b, dt_bias, a_log, d_ssd, ssd_norm_w, w_a_proj, s5_lambda_re, s5_lambda_im, s5_b_re, s5_b_im, s5_c_re, s5_c_im, s5_d, s5_log_dt, w_b_glu, w_out, norm_ffn2, w_ffn2_in, w_ffn2_out, norm_final, loss_target, m_w_ada, m_b_ada, m_norm_ffn1, m_w_ffn1_in, m_w_ffn1_out, m_norm_mix, m_w_in, m_conv_w, m_conv_b, m_dt_bias, m_a_log, m_d_ssd, m_ssd_norm_w, m_w_a_proj, m_s5_lambda_re, m_s5_lambda_im, m_s5_b_re, m_s5_b_im, m_s5_c_re, m_s5_c_im, m_s5_d, m_s5_log_dt, m_w_b_glu, m_w_out, m_norm_ffn2, m_w_ffn2_in, m_w_ffn2_out, m_norm_final, v_w_ada, v_b_ada, v_norm_ffn1, v_w_ffn1_in, v_w_ffn1_out, v_norm_mix, v_w_in, v_conv_w, v_conv_b, v_dt_bias, v_a_log, v_d_ssd, v_ssd_norm_w, v_w_a_proj, v_s5_lambda_re, v_s5_lambda_im, v_s5_b_re, v_s5_b_im, v_s5_c_re, v_s5_c_im, v_s5_d, v_s5_log_dt, v_w_b_glu, v_w_out, v_norm_ffn2, v_w_ffn2_in, v_w_ffn2_out, v_norm_final):
    given = dict(x=x, c=c, w_ada=w_ada, b_ada=b_ada, norm_ffn1=norm_ffn1, w_ffn1_in=w_ffn1_in, w_ffn1_out=w_ffn1_out, norm_mix=norm_mix, w_in=w_in, conv_w=conv_w, conv_b=conv_b, dt_bias=dt_bias, a_log=a_log, d_ssd=d_ssd, ssd_norm_w=ssd_norm_w, w_a_proj=w_a_proj, s5_lambda_re=s5_lambda_re, s5_lambda_im=s5_lambda_im, s5_b_re=s5_b_re, s5_b_im=s5_b_im, s5_c_re=s5_c_re, s5_c_im=s5_c_im, s5_d=s5_d, s5_log_dt=s5_log_dt, w_b_glu=w_b_glu, w_out=w_out, norm_ffn2=norm_ffn2, w_ffn2_in=w_ffn2_in, w_ffn2_out=w_ffn2_out, norm_final=norm_final, loss_target=loss_target, m_w_ada=m_w_ada, m_b_ada=m_b_ada, m_norm_ffn1=m_norm_ffn1, m_w_ffn1_in=m_w_ffn1_in, m_w_ffn1_out=m_w_ffn1_out, m_norm_mix=m_norm_mix, m_w_in=m_w_in, m_conv_w=m_conv_w, m_conv_b=m_conv_b, m_dt_bias=m_dt_bias, m_a_log=m_a_log, m_d_ssd=m_d_ssd, m_ssd_norm_w=m_ssd_norm_w, m_w_a_proj=m_w_a_proj, m_s5_lambda_re=m_s5_lambda_re, m_s5_lambda_im=m_s5_lambda_im, m_s5_b_re=m_s5_b_re, m_s5_b_im=m_s5_b_im, m_s5_c_re=m_s5_c_re, m_s5_c_im=m_s5_c_im, m_s5_d=m_s5_d, m_s5_log_dt=m_s5_log_dt, m_w_b_glu=m_w_b_glu, m_w_out=m_w_out, m_norm_ffn2=m_norm_ffn2, m_w_ffn2_in=m_w_ffn2_in, m_w_ffn2_out=m_w_ffn2_out, m_norm_final=m_norm_final, v_w_ada=v_w_ada, v_b_ada=v_b_ada, v_norm_ffn1=v_norm_ffn1, v_w_ffn1_in=v_w_ffn1_in, v_w_ffn1_out=v_w_ffn1_out, v_norm_mix=v_norm_mix, v_w_in=v_w_in, v_conv_w=v_conv_w, v_conv_b=v_conv_b, v_dt_bias=v_dt_bias, v_a_log=v_a_log, v_d_ssd=v_d_ssd, v_ssd_norm_w=v_ssd_norm_w, v_w_a_proj=v_w_a_proj, v_s5_lambda_re=v_s5_lambda_re, v_s5_lambda_im=v_s5_lambda_im, v_s5_b_re=v_s5_b_re, v_s5_b_im=v_s5_b_im, v_s5_c_re=v_s5_c_re, v_s5_c_im=v_s5_c_im, v_s5_d=v_s5_d, v_s5_log_dt=v_s5_log_dt, v_w_b_glu=v_w_b_glu, v_w_out=v_w_out, v_norm_ffn2=v_norm_ffn2, v_w_ffn2_in=v_w_ffn2_in, v_w_ffn2_out=v_w_ffn2_out, v_norm_final=v_norm_final)
    weights = {n: given[n] for n in TWIN_WEIGHTS}
    shared = {n: given[n] for n in SHARED_INPUTS}
    per_example = {n: given[n] for n in ['x', 'c']}
    grad_fn = _jax.value_and_grad(_loss, argnums=(0, 1))

    def one_microbatch(ex, loss_target):
        ex = dict(ex)
        diff = ex.pop(TWIN_DIFF_INPUT)
        return grad_fn(weights, diff, {**shared, **ex}, loss_target)

    if N_MICROBATCH == 1:
        loss, (grad_w, grad_x) = one_microbatch(per_example, given["loss_target"])
    else:
        def body(carry, xs):
            loss_sum, grad_sum = carry
            l_k, (gw_k, gx_k) = one_microbatch(xs[0], xs[1])
            with _jax.named_scope("update"):
                return (loss_sum + l_k, _jax.tree.map(_jnp.add, grad_sum, gw_k)), gx_k

        init = (_jnp.zeros((), _jnp.float32), _jax.tree.map(_jnp.zeros_like, weights))
        (loss, grad_w), grad_x = _jax.lax.scan(body, init, (per_example, given["loss_target"]))
    with _jax.named_scope("update"):
        delta_w, new_m, new_v = {}, {}, {}
        for n in TWIN_WEIGHTS:
            delta_w[n], new_m[n], new_v[n] = _adamw(weights[n], grad_w[n], given["m_" + n], given["v_" + n])
    return (loss, grad_x, *[grad_w[n] for n in TWIN_WEIGHTS], *[delta_w[n] for n in TWIN_WEIGHTS],
            *[new_m[n] for n in TWIN_WEIGHTS], *[new_v[n] for n in TWIN_WEIGHTS])
```

```python
import functools
import math

import numpy as np
import jax
import jax.numpy as jnp
from jax import lax
from jax.experimental import pallas as pl
from jax.experimental.pallas import tpu as pltpu

f32 = jnp.float32
bf16 = jnp.bfloat16
HI = lax.Precision.HIGHEST
MESH = pl.DeviceIdType.MESH
AXES = ("x", "y", "c")

EPS = 1e-6
SSD_HEADS, SSD_P, SSD_N, SSD_G, SSD_R, SSD_L = 32, 64, 128, 4, 8, 128
SSD_DI = SSD_HEADS * SSD_P
CONV_K = 4
CONV_DIM = SSD_DI + 2 * SSD_G * SSD_N
S5_W, S5_G, S5_I, S5_P = 1024, 64, 16, 64
S5_S = S5_G * S5_P
N_ADA = 9
ADAM_LR, ADAM_B1, ADAM_B2, ADAM_EPS, ADAM_WD, ADAM_STEP = 0.001, 0.9, 0.999, 1e-08, 0.01, 10

LANE = 128
SUBLANE = 8
VMEM_LIMIT = 56 << 20
MM_VMEM_BUDGET = 40 << 20
RW_VMEM_BUDGET = 36 << 20

WEIGHTS = ['w_ada', 'b_ada', 'norm_ffn1', 'w_ffn1_in', 'w_ffn1_out', 'norm_mix', 'w_in', 'conv_w', 'conv_b', 'dt_bias',
           'a_log', 'd_ssd', 'ssd_norm_w', 'w_a_proj', 's5_lambda_re', 's5_lambda_im', 's5_b_re', 's5_b_im', 's5_c_re',
           's5_c_im', 's5_d', 's5_log_dt', 'w_b_glu', 'w_out', 'norm_ffn2', 'w_ffn2_in', 'w_ffn2_out', 'norm_final']
BIG = ['w_ffn1_in', 'w_ffn1_out', 'w_in', 'w_a_proj', 'w_b_glu', 'w_out', 'w_ffn2_in', 'w_ffn2_out']
COL_SHARDED = ('w_ffn1_in', 'w_in', 'w_b_glu', 'w_ffn2_in')
SMALL = [n for n in WEIGHTS if n not in BIG and n != 'w_ada']


def _cp(sem=None):
    return pltpu.CompilerParams(dimension_semantics=sem, vmem_limit_bytes=VMEM_LIMIT)


def _tile(dim, target, align=LANE):
    if dim <= target:
        return dim
    t = (target // align) * align
    while t >= align:
        if dim % t == 0:
            return t
        t -= align
    return dim


def _round_up(n, m):
    return (n + m - 1) // m * m


def _mm(a, b, mode, *, out_dtype, name, a_win=None, b_win=None, add=None):
    a0, aw = a_win or (0, a.shape[1])
    b0, bw = b_win or (0, b.shape[1])
    if mode == 'nn':
        m, k, n = a.shape[0], aw, bw
        assert b.shape[0] == k
    elif mode == 'nt':
        m, k, n = a.shape[0], aw, b.shape[0]
        assert bw == k
    else:
        k, m, n = a.shape[0], aw, bw
        assert b.shape[0] == k
    osz = jnp.dtype(out_dtype).itemsize
    tm, tn, tk = 1024, 1152, 2048
    while True:
        bm = _tile(math.gcd(m, a0) if (mode == 'tn' and a0) else m, tm)
        bn = _tile(math.gcd(n, b0) if (mode != 'nt' and b0) else n, tn)
        kk = k
        if mode != 'tn' and a0:
            kk = math.gcd(kk, a0)
        if mode == 'nt' and b0:
            kk = math.gcd(kk, b0)
        bk = _tile(kk, tk)
        need = 2 * (bm * bk * a.dtype.itemsize + bk * bn * b.dtype.itemsize + bm * bn * osz) + bm * bn * 4
        if add is not None:
            need += 2 * bm * bn * add.dtype.itemsize
        if need <= MM_VMEM_BUDGET or (tm <= 256 and tn <= 256 and tk <= 512):
            break
        if tk > 1024:
            tk //= 2
        elif tm >= tn:
            tm //= 2
        else:
            tn //= 2
    nk = k // bk
    assert m % bm == 0 and n % bn == 0 and k % bk == 0, (name, m, n, k, bm, bn, bk)
    if mode == 'nn':
        ao, bo = a0 // bk, b0 // bn
        a_spec = pl.BlockSpec((bm, bk), lambda i, j, q: (i, q + ao))
        b_spec = pl.BlockSpec((bk, bn), lambda i, j, q: (q, j + bo))
        dims = (((1,), (0,)), ((), ()))
    elif mode == 'nt':
        ao, bo = a0 // bk, b0 // bk
        a_spec = pl.BlockSpec((bm, bk), lambda i, j, q: (i, q + ao))
        b_spec = pl.BlockSpec((bn, bk), lambda i, j, q: (j, q + bo))
        dims = (((1,), (1,)), ((), ()))
    else:
        ao, bo = a0 // bm, b0 // bn
        a_spec = pl.BlockSpec((bk, bm), lambda i, j, q: (q, i + ao))
        b_spec = pl.BlockSpec((bk, bn), lambda i, j, q: (q, j + bo))
        dims = (((0,), (0,)), ((), ()))
    has_add = add is not None

    def body(*refs):
        a_ref, b_ref = refs[0], refs[1]
        add_ref = refs[2] if has_add else None
        o_ref, acc_ref = refs[-2], refs[-1]
        q = pl.program_id(2)

        @pl.when(q == 0)
        def _():
            acc_ref[...] = jnp.zeros_like(acc_ref)

        acc_ref[...] += lax.dot_general(a_ref[...].astype(bf16), b_ref[...].astype(bf16), dims,
                                        preferred_element_type=f32)

        @pl.when(q == nk - 1)
        def _():
            r = acc_ref[...]
            if has_add:
                r = r + add_ref[...].astype(f32)
            o_ref[...] = r.astype(out_dtype)

    in_specs = [a_spec, b_spec]
    ops = [a, b]
    if has_add:
        in_specs.append(pl.BlockSpec((bm, bn), lambda i, j, q: (i, j)))
        ops.append(add)
    return pl.pallas_call(
        body, name=name, grid=(m // bm, n // bn, nk), in_specs=in_specs,
        out_specs=pl.BlockSpec((bm, bn), lambda i, j, q: (i, j)),
        out_shape=jax.ShapeDtypeStruct((m, n), out_dtype),
        scratch_shapes=[pltpu.VMEM((bm, bn), f32)],
        compiler_params=_cp(("parallel", "parallel", "arbitrary")),
    )(*ops)


def _win(r):
    return r if isinstance(r, tuple) else (r, 0, r.shape[1])


def _row_tile(t, widths):
    per_row = 48 * max(widths)
    tm = 512
    while tm > SUBLANE and tm * per_row > RW_VMEM_BUDGET:
        tm //= 2
    return min(tm, t)


def _row_spec(r, tm):
    arr, c0, w = _win(r)
    assert c0 % w == 0, (c0, w)
    cb = c0 // w
    return pl.BlockSpec((tm, w), lambda i: (i, cb))


def _full_spec(p):
    nd = p.ndim
    return pl.BlockSpec(p.shape, lambda i: (0,) * nd)


def _rw(f, rows, params, outs, *, name, accs=(), tm=None):
    t = _win(rows[0])[0].shape[0]
    tm = tm or _row_tile(t, [_win(r)[2] for r in rows] + [w for w, _ in outs])
    nr, npar, no, na = len(rows), len(params), len(outs), len(accs)

    def body(*refs):
        vals = [r[...] for r in refs[:nr + npar]]
        res = f(*vals)
        res = res if isinstance(res, (tuple, list)) else (res,)
        for o_ref, v in zip(refs[nr + npar:nr + npar + no], res[:no]):
            o_ref[...] = v.astype(o_ref.dtype)
        if na:
            first = pl.program_id(0) == 0
            for a_ref, v in zip(refs[nr + npar + no:], res[no:]):
                @pl.when(first)
                def _(a_ref=a_ref):
                    a_ref[...] = jnp.zeros_like(a_ref)
                a_ref[...] += v

    out_shape = [jax.ShapeDtypeStruct((t, w), d) for w, d in outs] + [jax.ShapeDtypeStruct(s, f32) for s in accs]
    out_specs = [pl.BlockSpec((tm, w), lambda i: (i, 0)) for w, _ in outs] + \
                [pl.BlockSpec(s, lambda i: (0, 0)) for s in accs]
    return pl.pallas_call(
        body, name=name, grid=(t // tm,),
        in_specs=[_row_spec(r, tm) for r in rows] + [_full_spec(p) for p in params],
        out_specs=out_specs, out_shape=out_shape,
        compiler_params=_cp(("arbitrary",)),
    )(*[_win(r)[0] for r in rows], *params)


def _rw_vjp(f, rows, params, cots, *, row_grads, param_grads, name, tm=None):
    t = _win(rows[0])[0].shape[0]
    cot_rows = [c for c in cots if c is not None]
    tm = tm or _row_tile(t, [_win(r)[2] for r in rows] + [_win(c)[2] for c in cot_rows])
    nr, npar, ncot = len(rows), len(params), len(cot_rows)
    d_rows = [i for i, d in enumerate(row_grads) if d is not None]
    d_pars = [i for i, d in enumerate(param_grads) if d]

    def body(*refs):
        rv = [r[...] for r in refs[:nr]]
        pv = [r[...] for r in refs[nr:nr + npar]]
        cv = [r[...] for r in refs[nr + npar:nr + npar + ncot]]
        outs_r = refs[nr + npar + ncot:nr + npar + ncot + len(d_rows)]
        outs_p = refs[nr + npar + ncot + len(d_rows):]

        def g(*diff):
            rr, pp = list(rv), list(pv)
            for i, v in zip(d_rows, diff[:len(d_rows)]):
                rr[i] = v
            for i, v in zip(d_pars, diff[len(d_rows):]):
                pp[i] = v
            res = f(*rr, *pp)
            return tuple(res) if isinstance(res, (tuple, list)) else (res,)

        prim, vjp = jax.vjp(g, *[rv[i] for i in d_rows], *[pv[i] for i in d_pars])
        it = iter(cv)
        cts = tuple(next(it).astype(o.dtype) if c is not None else jnp.zeros_like(o) for o, c in zip(prim, cots))
        grads = vjp(cts)
        for o_ref, v in zip(outs_r, grads[:len(d_rows)]):
            o_ref[...] = v.astype(o_ref.dtype)
        first = pl.program_id(0) == 0
        for o_ref, v in zip(outs_p, grads[len(d_rows):]):
            @pl.when(first)
            def _(o_ref=o_ref):
                o_ref[...] = jnp.zeros_like(o_ref)
            o_ref[...] += v.astype(f32)

    out_shape = [jax.ShapeDtypeStruct((t, _win(rows[i])[2]), row_grads[i]) for i in d_rows] + \
                [jax.ShapeDtypeStruct(params[i].shape, f32) for i in d_pars]
    out_specs = [pl.BlockSpec((tm, _win(rows[i])[2]), lambda i_: (i_, 0)) for i in d_rows] + \
                [_full_spec(params[i]) for i in d_pars]
    return pl.pallas_call(
        body, name=name, grid=(t // tm,),
        in_specs=[_row_spec(r, tm) for r in rows] + [_full_spec(p) for p in params] + [_row_spec(c, tm) for c in cot_rows],
        out_specs=out_specs, out_shape=out_shape,
        compiler_params=_cp(("arbitrary",)),
    )(*[_win(r)[0] for r in rows], *params, *[_win(c)[0] for c in cot_rows])


def _rms(x, g):
    return x * lax.rsqrt(jnp.mean(x * x, axis=-1, keepdims=True) + EPS) * g


def _f_mod(x, nw, sh, sc):
    return (_rms(x, nw) * (1.0 + sc) + sh).astype(bf16)


def _f_mod_keep(x, nw, sh, sc):
    return _f_mod(x, nw, sh, sc), x


def _f_res_mod(coef, x, o, g, nw, sh, sc):
    x1 = x + coef * g * o.astype(f32)
    return x1, _f_mod(x1, nw, sh, sc)


def _f_swiglu(ab):
    h = ab.shape[1] // 2
    a = ab[:, :h].astype(f32)
    b = ab[:, h:].astype(f32)
    return (jax.nn.silu(a) * b).astype(bf16)


def _f_ssd_pre(pre, dtraw, bias, sel):
    xc = jax.nn.silu(pre)
    dt = jax.nn.softplus(dtraw + bias)
    dt4 = jnp.dot(dt, sel, precision=HI, preferred_element_type=f32)
    return xc[:, :SSD_DI], xc[:, SSD_DI:SSD_DI + SSD_G * SSD_N], xc[:, SSD_DI + SSD_G * SSD_N:], dt4


def _f_ssd_post(y, z, nw):
    yz = y * jax.nn.silu(z)
    w = SSD_DI // SSD_G
    parts = []
    for g in range(SSD_G):
        s = yz[:, g * w:(g + 1) * w]
        parts.append(s * lax.rsqrt(jnp.mean(s * s, axis=-1, keepdims=True) + EPS))
    return (jnp.concatenate(parts, axis=1) * nw).astype(bf16)


def _f_s5_post(yb, u, d):
    return jax.nn.gelu(yb + d * u).astype(bf16)


def _f_merge(pa, glu_a, glu_g, ga, gb):
    pb = glu_a * jax.nn.sigmoid(glu_g)
    return (jax.nn.sigmoid(ga) * pa + jax.nn.sigmoid(gb) * pb).astype(bf16)


def _f_final(x2, o, tgt, g, nw):
    x3 = x2 + 0.5 * g * o.astype(f32)
    y = _rms(x3, nw)
    return 0.5 * jnp.mean(jnp.square(y - tgt), axis=-1, keepdims=True)


def _f_final_loss(x2, o, tgt, g, nw):
    rows = _f_final(x2, o, tgt, g, nw)
    return jnp.broadcast_to(jnp.sum(rows, axis=0, keepdims=True), (1, LANE))


def _f_s5_prep(lr, li, ldt, br, bi):
    dt = jnp.exp(ldt)
    lr = jnp.minimum(lr, -1e-4)
    mag = jnp.exp(lr * dt)
    ar = mag * jnp.cos(li * dt)
    ai = mag * jnp.sin(li * dt)
    den = lr * lr + li * li
    nr = ar - 1.0
    kr = (nr * lr + ai * li) / den
    ki = (ai * lr - nr * li) / den
    return ar, ai, kr * br - ki * bi, kr * bi + ki * br


def _shift_down(cur, halo8, j):
    if j == 0:
        return cur
    rolled = pltpu.roll(cur, j, 0)
    row8 = lax.broadcasted_iota(jnp.int32, halo8.shape, 0)
    top = jnp.where(row8 < j, pltpu.roll(halo8, j, 0), rolled[:SUBLANE])
    return jnp.concatenate([top, rolled[SUBLANE:]], axis=0)


def _shift_up(cur, halo8, j):
    if j == 0:
        return cur
    n = cur.shape[0]
    rolled = pltpu.roll(cur, n - j, 0)
    row8 = lax.broadcasted_iota(jnp.int32, halo8.shape, 0)
    bot = jnp.where(row8 >= SUBLANE - j, pltpu.roll(halo8, SUBLANE - j, 0), rolled[n - SUBLANE:])
    return jnp.concatenate([rolled[:n - SUBLANE], bot], axis=0)


def _conv_fwd(proj, c0, w8, b, name):
    t = proj.shape[0]
    cw = 1024
    tm = min(512, t)
    cb0 = c0 // cw
    r8 = tm // SUBLANE

    def body(x_ref, h_ref, w_ref, b_ref, o_ref):
        i = pl.program_id(1)
        x = x_ref[...]
        halo = jnp.where(i > 0, h_ref[...], 0.0)
        acc = b_ref[...] + w_ref[CONV_K - 1:CONV_K, :] * x
        for j in range(1, CONV_K):
            acc = acc + w_ref[CONV_K - 1 - j:CONV_K - j, :] * _shift_down(x, halo, j)
        o_ref[...] = acc

    return pl.pallas_call(
        body, name=name, grid=(CONV_DIM // cw, t // tm),
        in_specs=[pl.BlockSpec((tm, cw), lambda c, i: (i, cb0 + c)),
                  pl.BlockSpec((SUBLANE, cw), lambda c, i: (jnp.maximum(i * r8 - 1, 0), cb0 + c)),
                  pl.BlockSpec((SUBLANE, cw), lambda c, i: (0, c)),
                  pl.BlockSpec((1, cw), lambda c, i: (0, c))],
        out_specs=pl.BlockSpec((tm, cw), lambda c, i: (i, c)),
        out_shape=jax.ShapeDtypeStruct((t, CONV_DIM), f32),
        compiler_params=_cp(("parallel", "arbitrary")),
    )(proj, proj, w8, b)


def _conv_bwd(dpre, proj, c0, w8, name):
    t = proj.shape[0]
    cw = 1024
    tm = min(512, t)
    cb0 = c0 // cw
    r8 = tm // SUBLANE
    nb = t // tm

    def body(d_ref, dn_ref, x_ref, xh_ref, w_ref, dx_ref, dw_ref, db_ref):
        i = pl.program_id(1)
        d = d_ref[...]
        dn = jnp.where(i < nb - 1, dn_ref[...], 0.0)
        x = x_ref[...]
        xh = jnp.where(i > 0, xh_ref[...], 0.0)

        @pl.when(i == 0)
        def _():
            dw_ref[...] = jnp.zeros_like(dw_ref)
            db_ref[...] = jnp.zeros_like(db_ref)

        dx = w_ref[CONV_K - 1:CONV_K, :] * d
        rows = [jnp.sum(d * x, axis=0, keepdims=True)]
        for j in range(1, CONV_K):
            dx = dx + w_ref[CONV_K - 1 - j:CONV_K - j, :] * _shift_up(d, dn, j)
            rows.append(jnp.sum(d * _shift_down(x, xh, j), axis=0, keepdims=True))
        dx_ref[...] = dx.astype(dx_ref.dtype)
        dw = jnp.concatenate([rows[CONV_K - 1 - k] for k in range(CONV_K)] +
                             [jnp.zeros((SUBLANE - CONV_K, cw), f32)], axis=0)
        dw_ref[...] += dw
        db_ref[...] += jnp.sum(d, axis=0, keepdims=True)

    return pl.pallas_call(
        body, name=name, grid=(CONV_DIM // cw, nb),
        in_specs=[pl.BlockSpec((tm, cw), lambda c, i: (i, c)),
                  pl.BlockSpec((SUBLANE, cw), lambda c, i: (jnp.minimum((i + 1) * r8, nb * r8 - 1), c)),
                  pl.BlockSpec((tm, cw), lambda c, i: (i, cb0 + c)),
                  pl.BlockSpec((SUBLANE, cw), lambda c, i: (jnp.maximum(i * r8 - 1, 0), cb0 + c)),
                  pl.BlockSpec((SUBLANE, cw), lambda c, i: (0, c))],
        out_specs=[pl.BlockSpec((tm, cw), lambda c, i: (i, c)),
                   pl.BlockSpec((SUBLANE, cw), lambda c, i: (0, c)),
                   pl.BlockSpec((1, cw), lambda c, i: (0, c))],
        out_shape=[jax.ShapeDtypeStruct((t, CONV_DIM), bf16), jax.ShapeDtypeStruct((SUBLANE, CONV_DIM), f32),
                   jax.ShapeDtypeStruct((1, CONV_DIM), f32)],
        compiler_params=_cp(("parallel", "arbitrary")),
    )(dpre, dpre, proj, proj, w8)


def _ssd_chunk(xs, bm, cm, dt, a, dsk, h):
    n = SSD_L
    row = lax.broadcasted_iota(jnp.int32, (n, n), 0)
    col = lax.broadcasted_iota(jnp.int32, (n, n), 1)
    causal = row >= col
    cs = jnp.dot(causal.astype(f32), dt * a, precision=HI, preferred_element_type=f32)
    cs_t = cs.T
    nt = (((1,), (1,)), ((), ()))
    cb = lax.dot_general(cm.astype(bf16), bm.astype(bf16), nt, preferred_element_type=f32)
    ys, hn = [], []
    for r in range(SSD_R):
        xr = xs[:, r * SSD_P:(r + 1) * SSD_P]
        xdt = xr * dt[:, r:r + 1]
        c_col = cs[:, r:r + 1]
        decay = jnp.exp(jnp.where(causal, c_col - cs_t[r:r + 1, :], -1e30))
        y_diag = jnp.dot((cb * decay).astype(bf16), xdt.astype(bf16), preferred_element_type=f32)
        hr = h[r * SSD_P:(r + 1) * SSD_P, :]
        y_off = lax.dot_general(cm.astype(bf16), hr.astype(bf16), nt, preferred_element_type=f32) * jnp.exp(c_col)
        last = cs[n - 1:n, r:r + 1]
        st = lax.dot_general((xdt * jnp.exp(last - c_col)).astype(bf16), bm.astype(bf16), (((0,), (0,)), ((), ())),
                             preferred_element_type=f32)
        hn.append(jnp.exp(last) * hr + st)
        ys.append(y_diag + y_off + dsk[:, r:r + 1] * xr)
    return jnp.concatenate(ys, axis=1), jnp.concatenate(hn, axis=0)


def _ssd_specs(nc, rev):
    ch = (lambda c: nc - 1 - c) if rev else (lambda c: c)
    gw = SSD_R * SSD_P
    return [pl.BlockSpec((SSD_L, gw), lambda g, c: (ch(c), g)),
            pl.BlockSpec((SSD_L, SSD_N), lambda g, c: (ch(c), g)),
            pl.BlockSpec((SSD_L, SSD_N), lambda g, c: (ch(c), g)),
            pl.BlockSpec((SSD_L, LANE), lambda g, c: (ch(c), g)),
            pl.BlockSpec((1, LANE), lambda g, c: (0, g)),
            pl.BlockSpec((1, LANE), lambda g, c: (0, g))]


def _ssd_fwd(xs, bm, cm, dt4, a4, dsk4, name):
    t = xs.shape[0]
    nc = t // SSD_L
    gw = SSD_R * SSD_P

    def body(xs_ref, bm_ref, cm_ref, dt_ref, a_ref, dsk_ref, y_ref, hs_ref, h_ref):
        @pl.when(pl.program_id(1) == 0)
        def _():
            h_ref[...] = jnp.zeros_like(h_ref)

        hs_ref[...] = h_ref[...]
        y, hn = _ssd_chunk(xs_ref[...], bm_ref[...], cm_ref[...], dt_ref[...], a_ref[...], dsk_ref[...], h_ref[...])
        y_ref[...] = y
        h_ref[...] = hn

    return pl.pallas_call(
        body, name=name, grid=(SSD_G, nc), in_specs=_ssd_specs(nc, False),
        out_specs=[pl.BlockSpec((SSD_L, gw), lambda g, c: (c, g)),
                   pl.BlockSpec((None, None, gw, SSD_N), lambda g, c: (g, c, 0, 0))],
        out_shape=[jax.ShapeDtypeStruct((t, SSD_DI), f32), jax.ShapeDtypeStruct((SSD_G, nc, gw, SSD_N), f32)],
        scratch_shapes=[pltpu.VMEM((gw, SSD_N), f32)],
        compiler_params=_cp(("parallel", "arbitrary")),
    )(xs, bm, cm, dt4, a4, dsk4)


def _ssd_bwd(xs, bm, cm, dt4, a4, dsk4, hs, dy, name):
    t = xs.shape[0]
    nc = t // SSD_L
    gw = SSD_R * SSD_P
    rc = lambda c: nc - 1 - c

    def body(xs_ref, bm_ref, cm_ref, dt_ref, a_ref, dsk_ref, hs_ref, dy_ref,
             dxs_ref, dbm_ref, dcm_ref, ddt_ref, da_ref, ddsk_ref, dh_ref):
        @pl.when(pl.program_id(1) == 0)
        def _():
            dh_ref[...] = jnp.zeros_like(dh_ref)
            da_ref[...] = jnp.zeros_like(da_ref)
            ddsk_ref[...] = jnp.zeros_like(ddsk_ref)

        _, vjp = jax.vjp(_ssd_chunk, xs_ref[...], bm_ref[...], cm_ref[...], dt_ref[...], a_ref[...], dsk_ref[...],
                         hs_ref[...])
        dxs, dbm, dcm, ddt, da, ddsk, dh = vjp((dy_ref[...], dh_ref[...]))
        dxs_ref[...] = dxs
        dbm_ref[...] = dbm
        dcm_ref[...] = dcm
        ddt_ref[...] = ddt
        da_ref[...] += da
        ddsk_ref[...] += ddsk
        dh_ref[...] = dh

    return pl.pallas_call(
        body, name=name, grid=(SSD_G, nc),
        in_specs=_ssd_specs(nc, True) + [pl.BlockSpec((None, None, gw, SSD_N), lambda g, c: (g, rc(c), 0, 0)),
                                         pl.BlockSpec((SSD_L, gw), lambda g, c: (rc(c), g))],
        out_specs=[pl.BlockSpec((SSD_L, gw), lambda g, c: (rc(c), g)),
                   pl.BlockSpec((SSD_L, SSD_N), lambda g, c: (rc(c), g)),
                   pl.BlockSpec((SSD_L, SSD_N), lambda g, c: (rc(c), g)),
                   pl.BlockSpec((SSD_L, LANE), lambda g, c: (rc(c), g)),
                   pl.BlockSpec((1, LANE), lambda g, c: (0, g)),
                   pl.BlockSpec((1, LANE), lambda g, c: (0, g))],
        out_shape=[jax.ShapeDtypeStruct((t, SSD_DI), f32), jax.ShapeDtypeStruct((t, SSD_G * SSD_N), f32),
                   jax.ShapeDtypeStruct((t, SSD_G * SSD_N), f32), jax.ShapeDtypeStruct((t, SSD_G * LANE), f32),
                   jax.ShapeDtypeStruct((1, SSD_G * LANE), f32), jax.ShapeDtypeStruct((1, SSD_G * LANE), f32)],
        scratch_shapes=[pltpu.VMEM((gw, SSD_N), f32)],
        compiler_params=_cp(("parallel", "arbitrary")),
    )(xs, bm, cm, dt4, a4, dsk4, hs, dy)


S5_CH = 1024


def _s5_scan(bu, ar, ai, name):
    t = bu.shape[0]
    tb = min(128, t)

    def body(bu_ref, ar_ref, ai_ref, s_ref, carry):
        @pl.when(pl.program_id(0) == 0)
        def _():
            carry[...] = jnp.zeros_like(carry)

        for c0 in range(0, S5_S, S5_CH):
            re = pl.ds(c0, S5_CH)
            im = pl.ds(S5_S + c0, S5_CH)
            a_r = ar_ref[:, re]
            a_i = ai_ref[:, re]

            def step(k, st, re=re, im=im, a_r=a_r, a_i=a_i):
                sr, si = st
                row = pl.ds(k, 1)
                nr = a_r * sr - a_i * si + bu_ref[row, re]
                ni = a_r * si + a_i * sr + bu_ref[row, im]
                s_ref[row, re] = nr
                s_ref[row, im] = ni
                return nr, ni

            sr, si = lax.fori_loop(0, tb, step, (carry[:, re], carry[:, im]))
            carry[:, re] = sr
            carry[:, im] = si

    return pl.pallas_call(
        body, name=name, grid=(t // tb,),
        in_specs=[pl.BlockSpec((tb, 2 * S5_S), lambda i: (i, 0)),
                  pl.BlockSpec((1, S5_S), lambda i: (0, 0)), pl.BlockSpec((1, S5_S), lambda i: (0, 0))],
        out_specs=pl.BlockSpec((tb, 2 * S5_S), lambda i: (i, 0)),
        out_shape=jax.ShapeDtypeStruct((t, 2 * S5_S), f32),
        scratch_shapes=[pltpu.VMEM((1, 2 * S5_S), f32)],
        compiler_params=_cp(("arbitrary",)),
    )(bu, ar, ai)


def _s5_scan_bwd(ds, s, ar, ai, name):
    t = ds.shape[0]
    tb = min(128, t)
    nb = t // tb
    r8 = tb // SUBLANE
    rb = lambda i: nb - 1 - i

    def body(ds_ref, s_ref, sh_ref, ar_ref, ai_ref, g_ref, dar_ref, dai_ref, carry):
        i = pl.program_id(0)

        @pl.when(i == 0)
        def _():
            carry[...] = jnp.zeros_like(carry)
            dar_ref[...] = jnp.zeros_like(dar_ref)
            dai_ref[...] = jnp.zeros_like(dai_ref)

        has_prev = (i < nb - 1).astype(f32)
        for c0 in range(0, S5_S, S5_CH):
            re = pl.ds(c0, S5_CH)
            im = pl.ds(S5_S + c0, S5_CH)
            a_r = ar_ref[:, re]
            a_i = ai_ref[:, re]

            def upd(st, row, sp_r, sp_i, re=re, im=im, a_r=a_r, a_i=a_i):
                gr, gi, acr, aci = st
                ngr = ds_ref[row, re] + a_r * gr + a_i * gi
                ngi = ds_ref[row, im] + a_r * gi - a_i * gr
                g_ref[row, re] = ngr
                g_ref[row, im] = ngi
                return ngr, ngi, acr + ngr * sp_r + ngi * sp_i, aci + ngi * sp_r - ngr * sp_i

            def step(k, st, re=re, im=im, upd=upd):
                tt = tb - 1 - k
                prev = pl.ds(tt - 1, 1)
                return upd(st, pl.ds(tt, 1), s_ref[prev, re], s_ref[prev, im])

            zero = jnp.zeros((1, S5_CH), f32)
            st = lax.fori_loop(0, tb - 1, step, (carry[:, re], carry[:, im], zero, zero))
            last = pl.ds(SUBLANE - 1, 1)
            gr, gi, acr, aci = upd(st, pl.ds(0, 1), sh_ref[last, re] * has_prev, sh_ref[last, im] * has_prev)
            carry[:, re] = gr
            carry[:, im] = gi
            dar_ref[:, re] += acr
            dai_ref[:, re] += aci

    return pl.pallas_call(
        body, name=name, grid=(nb,),
        in_specs=[pl.BlockSpec((tb, 2 * S5_S), lambda i: (rb(i), 0)),
                  pl.BlockSpec((tb, 2 * S5_S), lambda i: (rb(i), 0)),
                  pl.BlockSpec((SUBLANE, 2 * S5_S), lambda i: (jnp.maximum(rb(i) * r8 - 1, 0), 0)),
                  pl.BlockSpec((1, S5_S), lambda i: (0, 0)), pl.BlockSpec((1, S5_S), lambda i: (0, 0))],
        out_specs=[pl.BlockSpec((tb, 2 * S5_S), lambda i: (rb(i), 0)),
                   pl.BlockSpec((1, S5_S), lambda i: (0, 0)), pl.BlockSpec((1, S5_S), lambda i: (0, 0))],
        out_shape=[jax.ShapeDtypeStruct((t, 2 * S5_S), f32), jax.ShapeDtypeStruct((1, S5_S), f32),
                   jax.ShapeDtypeStruct((1, S5_S), f32)],
        scratch_shapes=[pltpu.VMEM((1, 2 * S5_S), f32)],
        compiler_params=_cp(("arbitrary",)),
    )(ds, s, s, ar, ai)


def _ada_fwd(c_all, w, b, name):
    d, n = w.shape
    tn = _tile(n, 1536)

    def body(c_ref, w_ref, b_ref, o_ref):
        a = jax.nn.silu(c_ref[...]).astype(bf16)
        o_ref[...] = jnp.dot(a, w_ref[...].astype(bf16), preferred_element_type=f32) + b_ref[...]

    return pl.pallas_call(
        body, name=name, grid=(n // tn,),
        in_specs=[pl.BlockSpec(c_all.shape, lambda j: (0, 0)), pl.BlockSpec((d, tn), lambda j: (0, j)),
                  pl.BlockSpec((1, tn), lambda j: (0, j))],
        out_specs=pl.BlockSpec((c_all.shape[0], tn), lambda j: (0, j)),
        out_shape=jax.ShapeDtypeStruct((c_all.shape[0], n), f32),
        compiler_params=_cp(("parallel",)),
    )(c_all, w, b)


def _ada_bwd(c_all, dm, name):
    d = c_all.shape[1]
    n = dm.shape[1]
    tn = _tile(n, 1536)

    def body(c_ref, dm_ref, o_ref):
        a = jax.nn.silu(c_ref[...]).astype(bf16)
        o_ref[...] = lax.dot_general(a, dm_ref[...].astype(bf16), (((0,), (0,)), ((), ())), preferred_element_type=f32)

    return pl.pallas_call(
        body, name=name, grid=(n // tn,),
        in_specs=[pl.BlockSpec(c_all.shape, lambda j: (0, 0)), pl.BlockSpec((dm.shape[0], tn), lambda j: (0, j))],
        out_specs=pl.BlockSpec((d, tn), lambda j: (0, j)),
        out_shape=jax.ShapeDtypeStruct((d, n), f32),
        compiler_params=_cp(("parallel",)),
    )(c_all, dm)


def _blk_rows(r, c, nbuf, itemsize=4):
    tr = _tile(r, max(SUBLANE, (RW_VMEM_BUDGET // (2 * nbuf * c * itemsize)) // 16 * 16), 16)
    return tr if r % tr == 0 else r


def _cast_bf16(w, name):
    r, c = w.shape
    tr = _blk_rows(r, c, 2)

    def body(w_ref, o_ref):
        o_ref[...] = w_ref[...].astype(bf16)

    return pl.pallas_call(
        body, name=name, grid=(r // tr,), in_specs=[pl.BlockSpec((tr, c), lambda i: (i, 0))],
        out_specs=pl.BlockSpec((tr, c), lambda i: (i, 0)), out_shape=jax.ShapeDtypeStruct((r, c), bf16),
        compiler_params=_cp(("parallel",)),
    )(w)


def _sum_lead(parts, name):
    n, r, c = parts.shape
    tr = _blk_rows(r, c, n + 2)

    def body(p_ref, o_ref):
        acc = p_ref[0].astype(f32)
        for q in range(1, n):
            acc = acc + p_ref[q].astype(f32)
        o_ref[...] = acc

    return pl.pallas_call(
        body, name=name, grid=(r // tr,), in_specs=[pl.BlockSpec((n, tr, c), lambda i: (0, i, 0))],
        out_specs=pl.BlockSpec((tr, c), lambda i: (i, 0)), out_shape=jax.ShapeDtypeStruct((r, c), f32),
        compiler_params=_cp(("parallel",)),
    )(parts)


def _adamw(w, m, v, parts, name):
    r, c = w.shape
    npart = len(parts)
    tr = _blk_rows(r, c, 7 + npart)
    c1 = 1.0 - ADAM_B1 ** ADAM_STEP
    c2 = 1.0 - ADAM_B2 ** ADAM_STEP

    def body(*refs):
        w_ref, m_ref, v_ref = refs[:3]
        g_ref, d_ref, nm_ref, nv_ref = refs[3 + npart:]
        g = refs[3][...].astype(f32)
        for p in refs[4:3 + npart]:
            g = g + p[...].astype(f32)
        nm = ADAM_B1 * m_ref[...] + (1.0 - ADAM_B1) * g
        nv = ADAM_B2 * v_ref[...] + (1.0 - ADAM_B2) * jnp.square(g)
        g_ref[...] = g
        nm_ref[...] = nm
        nv_ref[...] = nv
        d_ref[...] = -ADAM_LR * ((nm / c1) / (jnp.sqrt(nv / c2) + ADAM_EPS) + ADAM_WD * w_ref[...])

    spec = pl.BlockSpec((tr, c), lambda i: (i, 0))
    return pl.pallas_call(
        body, name=name, grid=(r // tr,), in_specs=[spec] * (3 + npart), out_specs=[spec] * 4,
        out_shape=[jax.ShapeDtypeStruct((r, c), f32)] * 4, compiler_params=_cp(("parallel",)),
    )(w, m, v, *parts)


def _ag_small(x_shard, name):
    m_per, n = x_shard.shape

    def body(x_ref, out_ref, send_sems, recv_sems, local_sem):
        x, y, c = lax.axis_index("x"), lax.axis_index("y"), lax.axis_index("c")
        me, sibling = (x, y, c), (x, y, 1 - c)
        chips = [(1 - x, y), (x, 1 - y), (1 - x, 1 - y)]

        def rows(px, py, pc):
            return out_ref.at[pl.ds((4 * px + 2 * py + pc) * m_per, m_per), :]

        def copy(k, block, to, src=None):
            return pltpu.make_async_remote_copy(
                src_ref=rows(*block) if src is None else src, dst_ref=rows(*block),
                send_sem=send_sems.at[k], recv_sem=recv_sems.at[k], device_id=to, device_id_type=MESH)

        mine = pltpu.make_async_copy(x_ref, rows(*me), local_sem)
        mine.start()
        first = [copy(0, me, sibling, src=x_ref)]
        first += [copy(1 + j, me, (*chip, c), src=x_ref) for j, chip in enumerate(chips)]
        for cp in first:
            cp.start()
        passed = [copy(4 + j, (*chip, c), sibling) for j, chip in enumerate(chips)]
        for j, chip in enumerate(chips):
            copy(1 + j, (*chip, c), me).wait_recv()
            passed[j].start()
        copy(0, sibling, me).wait_recv()
        for j, chip in enumerate(chips):
            copy(4 + j, (*chip, 1 - c), me).wait_recv()
        for cp in first + passed:
            cp.wait_send()
        mine.wait()

    return pl.pallas_call(
        body, name=name, out_shape=jax.ShapeDtypeStruct((8 * m_per, n), x_shard.dtype),
        in_specs=[pl.BlockSpec(memory_space=pltpu.VMEM)], out_specs=pl.BlockSpec(memory_space=pltpu.VMEM),
        scratch_shapes=[pltpu.SemaphoreType.DMA((7,)), pltpu.SemaphoreType.DMA((7,)), pltpu.SemaphoreType.DMA],
        compiler_params=pltpu.CompilerParams(vmem_limit_bytes=VMEM_LIMIT),
    )(x_shard)


def _xchg(srcs, scatter, name):
    n = len(srcs)

    def body(*refs):
        ins, outs = refs[:n], refs[n:2 * n]
        send_sems, recv_sems, local_sems = refs[2 * n:]
        x, y, c = lax.axis_index("x"), lax.axis_index("y"), lax.axis_index("c")
        my_k = 2 * x + y
        peers = [(1 - x, y), (x, 1 - y), (1 - x, 1 - y)]
        sends, locals_ = [], []
        for a in range(n):
            own = ins[a].at[my_k] if scatter else ins[a]
            lc = pltpu.make_async_copy(own, outs[a].at[my_k], local_sems.at[a])
            lc.start()
            locals_.append(lc)
            for j, (px, py) in enumerate(peers):
                src = ins[a].at[2 * px + py] if scatter else ins[a]
                cp = pltpu.make_async_remote_copy(
                    src_ref=src, dst_ref=outs[a].at[my_k], send_sem=send_sems.at[3 * a + j],
                    recv_sem=recv_sems.at[3 * a + j], device_id=(px, py, c), device_id_type=MESH)
                cp.start()
                sends.append(cp)
        for a in range(n):
            for j, (px, py) in enumerate(peers):
                landed = outs[a].at[2 * px + py]
                pltpu.make_async_remote_copy(
                    src_ref=landed, dst_ref=landed, send_sem=send_sems.at[3 * a + j],
                    recv_sem=recv_sems.at[3 * a + j], device_id=(px, py, c), device_id_type=MESH).wait_recv()
        for cp in sends:
            cp.wait_send()
        for lc in locals_:
            lc.wait()

    out_shape = [jax.ShapeDtypeStruct(s.shape if scatter else (4,) + s.shape, s.dtype) for s in srcs]
    anyspec = pl.BlockSpec(memory_space=pl.ANY)
    return pl.pallas_call(
        body, name=name, out_shape=out_shape, in_specs=[anyspec] * n, out_specs=[anyspec] * n,
        scratch_shapes=[pltpu.SemaphoreType.DMA((3 * n,)), pltpu.SemaphoreType.DMA((3 * n,)),
                        pltpu.SemaphoreType.DMA((n,))],
    )(*srcs)


def _swap_sibling(srcs, name):
    n = len(srcs)

    def body(*refs):
        ins, outs = refs[:n], refs[n:2 * n]
        send_sems, recv_sems = refs[2 * n:]
        sib = (lax.axis_index("x"), lax.axis_index("y"), 1 - lax.axis_index("c"))
        cps = [pltpu.make_async_remote_copy(src_ref=ins[a], dst_ref=outs[a], send_sem=send_sems.at[a],
                                            recv_sem=recv_sems.at[a], device_id=sib, device_id_type=MESH)
               for a in range(n)]
        for cp in cps:
            cp.start()
        for cp in cps:
            cp.wait_recv()
        for cp in cps:
            cp.wait_send()

    anyspec = pl.BlockSpec(memory_space=pl.ANY)
    return pl.pallas_call(
        body, name=name, out_shape=[jax.ShapeDtypeStruct(s.shape, s.dtype) for s in srcs],
        in_specs=[anyspec] * n, out_specs=[anyspec] * n,
        scratch_shapes=[pltpu.SemaphoreType.DMA((n,)), pltpu.SemaphoreType.DMA((n,))],
    )(*srcs)


def _gather8(vec, name):
    size = vec.shape[0]
    n = _round_up(size, SUBLANE * LANE)
    blk = jnp.concatenate([vec, jnp.zeros((n - size,), f32)]).reshape(SUBLANE, n // SUBLANE)
    out = _ag_small(blk, name)
    return out.reshape(8, n)[:, :size]


def _sel_matrix():
    sel = np.zeros((LANE, SSD_G * LANE), np.float32)
    for h in range(SSD_HEADS):
        sel[h, (h // SSD_R) * LANE + h % SSD_R] = 1.0
    return jnp.asarray(sel)


def _heads_to_lanes(v):
    z = jnp.zeros((SSD_G, LANE), f32).at[:, :SSD_R].set(v.reshape(SSD_G, SSD_R))
    return z.reshape(1, SSD_G * LANE)


def _lanes_to_heads(v):
    return v.reshape(SSD_G, LANE)[:, :SSD_R].reshape(SSD_HEADS)


def _block_diag(blocks):
    g, r, c = blocks.shape
    eye = jnp.eye(g, dtype=bool)
    return jnp.where(eye[:, None, :, None], blocks[:, :, None, :], jnp.zeros((), blocks.dtype)).reshape(g * r, g * c)


def _diag_blocks(mat, g):
    r, c = mat.shape[0] // g, mat.shape[1] // g
    idx = jnp.arange(g)
    return mat.reshape(g, r, g, c)[idx, :, idx, :]


class _Layout:
    def __init__(self, d):
        self.d = d
        self.z, self.xbc, self.u = 0, SSD_DI, SSD_DI + CONV_DIM
        self.ga = self.u + S5_W
        self.gb = self.ga + d
        self.dt = self.gb + d
        self.np_ = self.dt + LANE
        self.in_cols = SSD_DI + CONV_DIM + SSD_HEADS + S5_W + 2 * d
        off_dt = SSD_DI + CONV_DIM
        off_u = off_dt + SSD_HEADS
        off_g = off_u + S5_W
        self.src = [(0, off_dt), (off_u, off_u + S5_W + 2 * d), (off_dt, off_u)]

    def arrange(self, w):
        (a0, a1), (b0, b1), (c0, c1) = self.src
        pad = jnp.zeros((w.shape[0], LANE - SSD_HEADS), w.dtype)
        return jnp.concatenate([w[:, a0:a1], w[:, b0:b1], w[:, c0:c1], pad], axis=1)

    def restore(self, w):
        n_a = self.src[0][1]
        n_b = self.src[1][1] - self.src[1][0]
        return jnp.concatenate([w[:, :n_a], w[:, n_a + n_b:n_a + n_b + SSD_HEADS], w[:, n_a:n_a + n_b]], axis=1)


def _unshard_cols(g):
    return g.transpose(1, 0, 2).reshape(g.shape[1], 4 * g.shape[2])


def _shard_cols(w):
    r, c4 = w.shape
    return w.reshape(r, 4, c4 // 4).transpose(1, 0, 2)


def kernel(x, c, w_ada, b_ada, norm_ffn1, w_ffn1_in, w_ffn1_out, norm_mix, w_in, conv_w, conv_b, dt_bias, a_log, d_ssd, ssd_norm_w, w_a_proj, s5_lambda_re, s5_lambda_im, s5_b_re, s5_b_im, s5_c_re, s5_c_im, s5_d, s5_log_dt, w_b_glu, w_out, norm_ffn2, w_ffn2_in, w_ffn2_out, norm_final, loss_target, m_w_ada, m_b_ada, m_norm_ffn1, m_w_ffn1_in, m_w_ffn1_out, m_norm_mix, m_w_in, m_conv_w, m_conv_b, m_dt_bias, m_a_log, m_d_ssd, m_ssd_norm_w, m_w_a_proj, m_s5_lambda_re, m_s5_lambda_im, m_s5_b_re, m_s5_b_im, m_s5_c_re, m_s5_c_im, m_s5_d, m_s5_log_dt, m_w_b_glu, m_w_out, m_norm_ffn2, m_w_ffn2_in, m_w_ffn2_out, m_norm_final, v_w_ada, v_b_ada, v_norm_ffn1, v_w_ffn1_in, v_w_ffn1_out, v_norm_mix, v_w_in, v_conv_w, v_conv_b, v_dt_bias, v_a_log, v_d_ssd, v_ssd_norm_w, v_w_a_proj, v_s5_lambda_re, v_s5_lambda_im, v_s5_b_re, v_s5_b_im, v_s5_c_re, v_s5_c_im, v_s5_d, v_s5_log_dt, v_w_b_glu, v_w_out, v_norm_ffn2, v_w_ffn2_in, v_w_ffn2_out, v_norm_final):
    W = dict(w_ada=w_ada, b_ada=b_ada, norm_ffn1=norm_ffn1, w_ffn1_in=w_ffn1_in, w_ffn1_out=w_ffn1_out, norm_mix=norm_mix, w_in=w_in, conv_w=conv_w, conv_b=conv_b, dt_bias=dt_bias, a_log=a_log, d_ssd=d_ssd, ssd_norm_w=ssd_norm_w, w_a_proj=w_a_proj, s5_lambda_re=s5_lambda_re, s5_lambda_im=s5_lambda_im, s5_b_re=s5_b_re, s5_b_im=s5_b_im, s5_c_re=s5_c_re, s5_c_im=s5_c_im, s5_d=s5_d, s5_log_dt=s5_log_dt, w_b_glu=w_b_glu, w_out=w_out, norm_ffn2=norm_ffn2, w_ffn2_in=w_ffn2_in, w_ffn2_out=w_ffn2_out, norm_final=norm_final)
    Mo = dict(w_ada=m_w_ada, b_ada=m_b_ada, norm_ffn1=m_norm_ffn1, w_ffn1_in=m_w_ffn1_in, w_ffn1_out=m_w_ffn1_out, norm_mix=m_norm_mix, w_in=m_w_in, conv_w=m_conv_w, conv_b=m_conv_b, dt_bias=m_dt_bias, a_log=m_a_log, d_ssd=m_d_ssd, ssd_norm_w=m_ssd_norm_w, w_a_proj=m_w_a_proj, s5_lambda_re=m_s5_lambda_re, s5_lambda_im=m_s5_lambda_im, s5_b_re=m_s5_b_re, s5_b_im=m_s5_b_im, s5_c_re=m_s5_c_re, s5_c_im=m_s5_c_im, s5_d=m_s5_d, s5_log_dt=m_s5_log_dt, w_b_glu=m_w_b_glu, w_out=m_w_out, norm_ffn2=m_norm_ffn2, w_ffn2_in=m_w_ffn2_in, w_ffn2_out=m_w_ffn2_out, norm_final=m_norm_final)
    Vo = dict(w_ada=v_w_ada, b_ada=v_b_ada, norm_ffn1=v_norm_ffn1, w_ffn1_in=v_w_ffn1_in, w_ffn1_out=v_w_ffn1_out, norm_mix=v_norm_mix, w_in=v_w_in, conv_w=v_conv_w, conv_b=v_conv_b, dt_bias=v_dt_bias, a_log=v_a_log, d_ssd=v_d_ssd, ssd_norm_w=v_ssd_norm_w, w_a_proj=v_w_a_proj, s5_lambda_re=v_s5_lambda_re, s5_lambda_im=v_s5_lambda_im, s5_b_re=v_s5_b_re, s5_b_im=v_s5_b_im, s5_c_re=v_s5_c_re, s5_c_im=v_s5_c_im, s5_d=v_s5_d, s5_log_dt=v_s5_log_dt, w_b_glu=v_w_b_glu, w_out=v_w_out, norm_ffn2=v_norm_ffn2, w_ffn2_in=v_w_ffn2_in, w_ffn2_out=v_w_ffn2_out, norm_final=v_norm_final)

    t, d = x.shape[1], x.shape[2]
    ff = 4 * w_ffn1_out.shape[1]
    ffp = _round_up(ff, 512)
    lay = _Layout(d)
    xi, yi, ci = lax.axis_index("x"), lax.axis_index("y"), lax.axis_index("c")
    k_me = 2 * xi + yi
    e_me = 4 * xi + 2 * yi + ci
    x2d = x[0]
    tgt = loss_target[0]

    cw_cols = conv_w.shape[2]
    g1 = _gather8(jnp.concatenate([c[0], conv_w[0].reshape(-1)]), "gather_c_convw")
    c_all = g1[:, :d]
    conv_full = g1[::2, d:].reshape(4, CONV_K, cw_cols).transpose(1, 0, 2).reshape(CONV_K, CONV_DIM)
    conv_w8 = jnp.zeros((SUBLANE, CONV_DIM), f32).at[:CONV_K].set(conv_full)

    n_ada_loc = w_ada.shape[2]
    b_loc = lax.dynamic_slice(b_ada, (0, k_me * n_ada_loc), (1, n_ada_loc))
    mods_part = _ada_fwd(c_all, w_ada[0], b_loc, "ada_fwd")
    g2 = _gather8(mods_part.reshape(-1), "gather_mods").reshape(8, 8, n_ada_loc)
    mods = lax.dynamic_index_in_dim(g2[::2], e_me, axis=1, keepdims=False).reshape(N_ADA, d)
    sh1, sc1, gt1, sh2, sc2, gt2, sh3, sc3, gt3 = [mods[i:i + 1] for i in range(N_ADA)]

    gathered = _xchg([_cast_bf16(W[n][0], "cast_" + n) for n in BIG], False, "gather_weights")
    full = {}
    for n, g in zip(BIG, gathered):
        full[n] = _unshard_cols(g) if n in COL_SHARDED else g.reshape(4 * g.shape[1], g.shape[2])

    def ffn_in(w):
        z = jnp.zeros((w.shape[0], ffp - ff), w.dtype)
        return jnp.concatenate([w[:, :ff], z, w[:, ff:], z], axis=1)

    def ffn_out(w):
        return jnp.concatenate([w, jnp.zeros((ffp - ff, w.shape[1]), w.dtype)], axis=0)

    w1i, w1o = ffn_in(full['w_ffn1_in']), ffn_out(full['w_ffn1_out'])
    w2i, w2o = ffn_in(full['w_ffn2_in']), ffn_out(full['w_ffn2_out'])
    w_inr = lay.arrange(full['w_in'])
    w_a, w_glu, w_o = full['w_a_proj'], full['w_b_glu'], full['w_out']

    nf1, nmx, nf2 = norm_ffn1, norm_mix, norm_ffn2
    nfin = norm_final.reshape(1, d)

    (h1,) = _rw(_f_mod, [x2d], [nf1, sh1, sc1], [(d, bf16)], name="mod1")
    ab1 = _mm(h1, w1i, 'nn', out_dtype=bf16, name="ffn1_in")
    (act1,) = _rw(_f_swiglu, [ab1], [], [(ffp, bf16)], name="ffn1_act")
    f1 = _mm(act1, w1o, 'nn', out_dtype=bf16, name="ffn1_out")
    res1 = functools.partial(_f_res_mod, 0.5)
    x1, h2 = _rw(res1, [x2d, f1], [gt1, nmx, sh2, sc2], [(d, f32), (d, bf16)], name="res1_mod2")
    proj = _mm(h2, w_inr, 'nn', out_dtype=f32, name="in_proj")

    pre = _conv_fwd(proj, lay.xbc, conv_w8, conv_b, "conv_fwd")
    sel = _sel_matrix()
    bias128 = jnp.zeros((1, LANE), f32).at[:, :SSD_HEADS].set(dt_bias)
    xs, bm, cm, dt4 = _rw(_f_ssd_pre, [pre, (proj, lay.dt, LANE)], [bias128, sel],
                          [(SSD_DI, f32), (SSD_G * SSD_N, f32), (SSD_G * SSD_N, f32), (SSD_G * LANE, f32)],
                          name="ssd_pre")

    def head_params(a_log_, d_ssd_):
        return _heads_to_lanes(-jnp.exp(a_log_[0])), _heads_to_lanes(d_ssd_[0])

    (a4, dsk4), head_vjp = jax.vjp(head_params, a_log, d_ssd)
    y_ssd, hs = _ssd_fwd(xs, bm, cm, dt4, a4, dsk4, "ssd_fwd")
    (y_a,) = _rw(_f_ssd_post, [y_ssd, (proj, lay.z, SSD_DI)], [ssd_norm_w], [(SSD_DI, bf16)], name="ssd_post")
    p_a = _mm(y_a, w_a, 'nn', out_dtype=f32, name="a_proj")

    col = lambda v: v.reshape(S5_S, 1)
    ldt_col = jnp.repeat(s5_log_dt[0], S5_P).reshape(S5_S, 1)
    prep_rows = [col(s5_lambda_re[0]), col(s5_lambda_im[0]), ldt_col,
                 s5_b_re[0].reshape(S5_S, S5_I), s5_b_im[0].reshape(S5_S, S5_I)]
    ar, ai, bbr, bbi = _rw(_f_s5_prep, prep_rows, [], [(1, f32), (1, f32), (S5_I, f32), (S5_I, f32)],
                           name="s5_prep", tm=512)
    to_bd = lambda bb: _block_diag(bb.reshape(S5_G, S5_P, S5_I).transpose(0, 2, 1).astype(bf16))
    bd = jnp.concatenate([to_bd(bbr), to_bd(bbi)], axis=1)
    c_bd = jnp.concatenate([_block_diag(s5_c_re[0].transpose(0, 2, 1).astype(bf16)),
                            _block_diag((-s5_c_im[0]).transpose(0, 2, 1).astype(bf16))], axis=0)
    ar_row, ai_row = ar.reshape(1, S5_S), ai.reshape(1, S5_S)
    u_win = (lay.u, S5_W)
    bu = _mm(proj, bd, 'nn', out_dtype=f32, name="s5_bu", a_win=u_win)
    s5s = _s5_scan(bu, ar_row, ai_row, "s5_scan")
    yb = _mm(s5s, c_bd, 'nn', out_dtype=f32, name="s5_out")
    d_row = s5_d[0].reshape(1, S5_W)
    (gl,) = _rw(_f_s5_post, [yb, (proj, lay.u, S5_W)], [d_row], [(S5_W, bf16)], name="s5_post")
    glu = _mm(gl, w_glu, 'nn', out_dtype=f32, name="glu_proj")

    merge_rows = [p_a, (glu, 0, d), (glu, d, d), (proj, lay.ga, d), (proj, lay.gb, d)]
    (merged,) = _rw(_f_merge, merge_rows, [], [(d, bf16)], name="merge")
    o_mix = _mm(merged, w_o, 'nn', out_dtype=bf16, name="out_proj")
    res2 = functools.partial(_f_res_mod, 1.0)
    x2, h3 = _rw(res2, [x1, o_mix], [gt2, nf2, sh3, sc3], [(d, f32), (d, bf16)], name="res2_mod3")
    ab2 = _mm(h3, w2i, 'nn', out_dtype=bf16, name="ffn2_in")
    (act2,) = _rw(_f_swiglu, [ab2], [], [(ffp, bf16)], name="ffn2_act")
    f2 = _mm(act2, w2o, 'nn', out_dtype=bf16, name="ffn2_out")
    (loss_acc,) = _rw(_f_final_loss, [x2, f2, tgt], [gt3, nfin], [], accs=[(1, LANE)], name="loss")
    loss = lax.psum(loss_acc[0, 0], AXES)

    ones = jnp.ones((t, 1), f32)
    dx2, df2, dgt3, dnfin = _rw_vjp(_f_final, [x2, f2, tgt], [gt3, nfin], [ones],
                                    row_grads=[f32, bf16, None], param_grads=[True, True], name="loss_bwd")
    dact2 = _mm(df2, w2o, 'nt', out_dtype=bf16, name="ffn2_out_dx")
    dw2o = _mm(act2, df2, 'tn', out_dtype=f32, name="ffn2_out_dw")
    (dab2,) = _rw_vjp(_f_swiglu, [ab2], [], [dact2], row_grads=[bf16], param_grads=[], name="ffn2_act_bwd")
    dh3 = _mm(dab2, w2i, 'nt', out_dtype=bf16, name="ffn2_in_dx")
    dw2i = _mm(h3, dab2, 'tn', out_dtype=f32, name="ffn2_in_dw")
    dx1, do_mix, dgt2, dnf2, dsh3, dsc3 = _rw_vjp(
        res2, [x1, o_mix], [gt2, nf2, sh3, sc3], [dx2, dh3], row_grads=[f32, bf16], param_grads=[True] * 4,
        name="res2_mod3_bwd")
    dmerged = _mm(do_mix, w_o, 'nt', out_dtype=bf16, name="out_proj_dx")
    dw_o = _mm(merged, do_mix, 'tn', out_dtype=f32, name="out_proj_dw")
    dp_a, dglu_a, dglu_g, dga, dgb = _rw_vjp(_f_merge, merge_rows, [], [dmerged], row_grads=[bf16] * 5,
                                             param_grads=[], name="merge_bwd")
    dglu = jnp.concatenate([dglu_a, dglu_g], axis=1)

    dgl = _mm(dglu, w_glu, 'nt', out_dtype=bf16, name="glu_proj_dx")
    dw_glu = _mm(gl, dglu, 'tn', out_dtype=f32, name="glu_proj_dw")
    dyb, du_skip, dd_row = _rw_vjp(_f_s5_post, [yb, (proj, lay.u, S5_W)], [d_row], [dgl],
                                   row_grads=[bf16, f32], param_grads=[True], name="s5_post_bwd")
    ds5 = _mm(dyb, c_bd, 'nt', out_dtype=f32, name="s5_out_dx")
    dc_bd = _mm(s5s, dyb, 'tn', out_dtype=f32, name="s5_out_dw")
    g5, dar, dai = _s5_scan_bwd(ds5, s5s, ar_row, ai_row, "s5_scan_bwd")
    du = _mm(g5, bd, 'nt', out_dtype=bf16, name="s5_bu_dx", add=du_skip)
    dbd = _mm(proj, g5, 'tn', out_dtype=f32, name="s5_bu_dw", a_win=u_win)
    from_bd = lambda m_: _diag_blocks(m_, S5_G).transpose(0, 2, 1).reshape(S5_S, S5_I)
    dprep = _rw_vjp(_f_s5_prep, prep_rows, [], [dar.reshape(S5_S, 1), dai.reshape(S5_S, 1),
                                                from_bd(dbd[:, :S5_S]), from_bd(dbd[:, S5_S:])],
                    row_grads=[f32] * 5, param_grads=[], name="s5_prep_bwd", tm=512)
    dlr, dli, dldt, dbr, dbi = dprep
    g_s5 = dict(
        s5_lambda_re=dlr.reshape(S5_G, S5_P), s5_lambda_im=dli.reshape(S5_G, S5_P),
        s5_log_dt=dldt.reshape(S5_G, S5_P).sum(axis=1),
        s5_b_re=dbr.reshape(S5_G, S5_P, S5_I), s5_b_im=dbi.reshape(S5_G, S5_P, S5_I),
        s5_c_re=_diag_blocks(dc_bd[:S5_S], S5_G).transpose(0, 2, 1),
        s5_c_im=-_diag_blocks(dc_bd[S5_S:], S5_G).transpose(0, 2, 1),
        s5_d=dd_row.reshape(S5_G, S5_I))

    dy_a = _mm(dp_a, w_a, 'nt', out_dtype=bf16, name="a_proj_dx")
    dw_a = _mm(y_a, dp_a, 'tn', out_dtype=f32, name="a_proj_dw")
    dy_ssd, dz, dssd_nw = _rw_vjp(_f_ssd_post, [y_ssd, (proj, lay.z, SSD_DI)], [ssd_norm_w], [dy_a],
                                  row_grads=[f32, bf16], param_grads=[True], name="ssd_post_bwd")
    dxs, dbm, dcm, ddt4, da4, ddsk4 = _ssd_bwd(xs, bm, cm, dt4, a4, dsk4, hs, dy_ssd, "ssd_bwd")
    da_log, dd_ssd = head_vjp((da4, ddsk4))
    dpre, ddt_raw, dbias128 = _rw_vjp(_f_ssd_pre, [pre, (proj, lay.dt, LANE)], [bias128, sel], [dxs, dbm, dcm, ddt4],
                                      row_grads=[f32, bf16], param_grads=[True, False], name="ssd_pre_bwd")
    dxbc, dconv_w8, dconv_b = _conv_bwd(dpre, proj, lay.xbc, conv_w8, "conv_bwd")

    dproj = jnp.concatenate([dz, dxbc, du, dga, dgb, ddt_raw], axis=1)
    dh2 = _mm(dproj, w_inr, 'nt', out_dtype=bf16, name="in_proj_dx")
    dw_inr = _mm(h2, dproj, 'tn', out_dtype=f32, name="in_proj_dw")
    dx0, df1, dgt1, dnmx, dsh2, dsc2 = _rw_vjp(
        res1, [x2d, f1], [gt1, nmx, sh2, sc2], [dx1, dh2], row_grads=[f32, bf16], param_grads=[True] * 4,
        name="res1_mod2_bwd")
    dact1 = _mm(df1, w1o, 'nt', out_dtype=bf16, name="ffn1_out_dx")
    dw1o = _mm(act1, df1, 'tn', out_dtype=f32, name="ffn1_out_dw")
    (dab1,) = _rw_vjp(_f_swiglu, [ab1], [], [dact1], row_grads=[bf16], param_grads=[], name="ffn1_act_bwd")
    dh1 = _mm(dab1, w1i, 'nt', out_dtype=bf16, name="ffn1_in_dx")
    dw1i = _mm(h1, dab1, 'tn', out_dtype=f32, name="ffn1_in_dw")
    grad_x, dnf1, dsh1, dsc1 = _rw_vjp(_f_mod_keep, [x2d], [nf1, sh1, sc1], [dh1, dx0],
                                       row_grads=[f32], param_grads=[True] * 3, name="mod1_bwd")
    d_mods = jnp.concatenate([dsh1, dsc1, dgt1, dsh2, dsc2, dgt2, dsh3, dsc3, dgt3], axis=1).reshape(-1)

    def ffn_in_back(g):
        return jnp.concatenate([g[:, :ff], g[:, ffp:ffp + ff]], axis=1)

    gfull = dict(w_ffn1_in=ffn_in_back(dw1i), w_ffn1_out=dw1o[:ff], w_in=lay.restore(dw_inr), w_a_proj=dw_a,
                 w_b_glu=dw_glu, w_out=dw_o, w_ffn2_in=ffn_in_back(dw2i), w_ffn2_out=dw2o[:ff])
    slabs = []
    for n in BIG:
        g = gfull[n]
        g = _shard_cols(g) if n in COL_SHARDED else g.reshape(4, g.shape[0] // 4, g.shape[1])
        slabs.append(g.astype(bf16))
    terms = _xchg(slabs, True, "scatter_grads")
    sums = [_sum_lead(p, "sum_" + n) for n, p in zip(BIG, terms)]
    others = _swap_sibling(sums, "swap_sums")

    out_g, out_d, out_m, out_v = {}, {}, {}, {}
    for n, s_own, s_sib in zip(BIG, sums, others):
        r = _adamw(W[n][0], Mo[n][0], Vo[n][0], [s_own, s_sib], "adamw_" + n)
        out_g[n], out_d[n], out_m[n], out_v[n] = [o[None] for o in r]

    local = dict(
        b_ada=d_mods, norm_ffn1=dnf1, norm_mix=dnmx, conv_w=dconv_w8[:CONV_K], conv_b=dconv_b,
        dt_bias=dbias128[:, :SSD_HEADS], a_log=da_log, d_ssd=dd_ssd, ssd_norm_w=dssd_nw,
        norm_ffn2=dnf2, norm_final=dnfin, **g_s5)
    flat = jnp.concatenate([local[n].reshape(-1) for n in SMALL])
    g3 = _gather8(flat, "gather_small_grads")
    n_small = flat.shape[0]
    npad = _round_up(n_small, SUBLANE * LANE)
    g3p = jnp.zeros((8, npad), f32).at[:, :n_small].set(g3).reshape(8, npad // LANE, LANE)
    gsum = _sum_lead(g3p, "sum_small").reshape(-1)

    def local_shard(n, a):
        if n == 'conv_w':
            return lax.dynamic_slice(a.reshape(CONV_K, CONV_DIM), (0, k_me * cw_cols), (CONV_K, cw_cols))
        return a

    pieces, off = {}, 0
    for n in SMALL:
        sz = local[n].size
        pieces[n] = local_shard(n, gsum[off:off + sz]).reshape(W[n].shape)
        off += sz

    def pack(dct):
        v_ = jnp.concatenate([dct[n].reshape(-1) for n in SMALL])
        pad = _round_up(v_.shape[0], SUBLANE * LANE) - v_.shape[0]
        return jnp.concatenate([v_, jnp.ones((pad,), f32)]).reshape(-1, LANE)

    rs = _adamw(pack(W), pack(Mo), pack(Vo), [pack(pieces)], "adamw_small")
    off = 0
    for n in SMALL:
        sz = W[n].size
        out_g[n], out_d[n], out_m[n], out_v[n] = [o.reshape(-1)[off:off + sz].reshape(W[n].shape) for o in rs]
        off += sz

    dm_loc = lax.dynamic_slice(g3[:, :N_ADA * d], (0, k_me * n_ada_loc), (SUBLANE, n_ada_loc))
    g_ada = _ada_bwd(c_all, dm_loc, "ada_bwd")
    r = _adamw(w_ada[0], m_w_ada[0], v_w_ada[0], [g_ada], "adamw_w_ada")
    out_g['w_ada'], out_d['w_ada'], out_m['w_ada'], out_v['w_ada'] = [o[None] for o in r]

    return (loss, grad_x[None], *[out_g[n] for n in WEIGHTS], *[out_d[n] for n in WEIGHTS],
            *[out_m[n] for n in WEIGHTS], *[out_v[n] for n in WEIGHTS])
```

```python
import functools
import math

import numpy as np
import jax
import jax.numpy as jnp
from jax import lax
from jax.experimental import pallas as pl
from jax.experimental.pallas import tpu as pltpu

f32 = jnp.float32
bf16 = jnp.bfloat16
HI = lax.Precision.HIGHEST
MESH = pl.DeviceIdType.MESH
AXES = ("x", "y", "c")

EPS = 1e-6
SSD_HEADS, SSD_P, SSD_N, SSD_G, SSD_R, SSD_L = 32, 64, 128, 4, 8, 128
SSD_DI = SSD_HEADS * SSD_P
CONV_K = 4
CONV_DIM = SSD_DI + 2 * SSD_G * SSD_N
S5_W, S5_G, S5_I, S5_P = 1024, 64, 16, 64
S5_S = S5_G * S5_P
N_ADA = 9
ADAM_LR, ADAM_B1, ADAM_B2, ADAM_EPS, ADAM_WD, ADAM_STEP = 0.001, 0.9, 0.999, 1e-08, 0.01, 10

LANE = 128
SUBLANE = 8
VMEM_LIMIT = 56 << 20
MM_VMEM_BUDGET = 40 << 20
RW_VMEM_BUDGET = 36 << 20

WEIGHTS = ['w_ada', 'b_ada', 'norm_ffn1', 'w_ffn1_in', 'w_ffn1_out', 'norm_mix', 'w_in', 'conv_w', 'conv_b', 'dt_bias',
           'a_log', 'd_ssd', 'ssd_norm_w', 'w_a_proj', 's5_lambda_re', 's5_lambda_im', 's5_b_re', 's5_b_im', 's5_c_re',
           's5_c_im', 's5_d', 's5_log_dt', 'w_b_glu', 'w_out', 'norm_ffn2', 'w_ffn2_in', 'w_ffn2_out', 'norm_final']
BIG = ['w_ffn1_in', 'w_ffn1_out', 'w_in', 'w_a_proj', 'w_b_glu', 'w_out', 'w_ffn2_in', 'w_ffn2_out']
COL_SHARDED = ('w_ffn1_in', 'w_in', 'w_b_glu', 'w_ffn2_in')
SMALL = [n for n in WEIGHTS if n not in BIG and n != 'w_ada']


def _cp(sem=None):
    return pltpu.CompilerParams(dimension_semantics=sem, vmem_limit_bytes=VMEM_LIMIT)


def _tile(dim, target, align=LANE):
    if dim <= target:
        return dim
    t = (target // align) * align
    while t >= align:
        if dim % t == 0:
            return t
        t -= align
    return dim


def _round_up(n, m):
    return (n + m - 1) // m * m


def _mm(a, b, mode, *, out_dtype, name, a_win=None, b_win=None, add=None, ride=None):
    a0, aw = a_win or (0, a.shape[1])
    b0, bw = b_win or (0, b.shape[1])
    if mode == 'nn':
        m, k, n = a.shape[0], aw, bw
        assert b.shape[0] == k
    elif mode == 'nt':
        m, k, n = a.shape[0], aw, b.shape[0]
        assert bw == k
    else:
        k, m, n = a.shape[0], aw, bw
        assert b.shape[0] == k
    osz = jnp.dtype(out_dtype).itemsize
    tm, tn, tk = 1024, 1152, 2048
    while True:
        bm = _tile(math.gcd(m, a0) if (mode == 'tn' and a0) else m, tm)
        bn = _tile(math.gcd(n, b0) if (mode != 'nt' and b0) else n, tn)
        kk = k
        if mode != 'tn' and a0:
            kk = math.gcd(kk, a0)
        if mode == 'nt' and b0:
            kk = math.gcd(kk, b0)
        bk = _tile(kk, tk)
        need = 2 * (bm * bk * a.dtype.itemsize + bk * bn * b.dtype.itemsize + bm * bn * osz) + bm * bn * 4
        if add is not None:
            need += 2 * bm * bn * add.dtype.itemsize
        if need <= MM_VMEM_BUDGET or (tm <= 256 and tn <= 256 and tk <= 512):
            break
        if tk > 1024:
            tk //= 2
        elif tm >= tn:
            tm //= 2
        else:
            tn //= 2
    nk = k // bk
    assert m % bm == 0 and n % bn == 0 and k % bk == 0, (name, m, n, k, bm, bn, bk)
    if mode == 'nn':
        ao, bo = a0 // bk, b0 // bn
        a_blk, a_map = (bm, bk), lambda i, j, q: (i, q + ao)
        b_blk, b_map = (bk, bn), lambda i, j, q: (q, j + bo)
    elif mode == 'nt':
        ao, bo = a0 // bk, b0 // bk
        a_blk, a_map = (bm, bk), lambda i, j, q: (i, q + ao)
        b_blk, b_map = (bn, bk), lambda i, j, q: (j, q + bo)
    else:
        ao, bo = a0 // bm, b0 // bn
        a_blk, a_map = (bk, bm), lambda i, j, q: (q, i + ao)
        b_blk, b_map = (bk, bn), lambda i, j, q: (q, j + bo)
    return _mm_core(a, b, mode, grid=(m // bm, n // bn, nk), a_blk=a_blk, a_map=a_map, b_blk=b_blk, b_map=b_map,
                    o_blk=(bm, bn), o_map=lambda i, j, q: (i, j), out_shape=(m, n), out_dtype=out_dtype, name=name,
                    add=add, ride=ride)


def _mm_core(a, b, mode, *, grid, a_blk, a_map, b_blk, b_map, o_blk, o_map, out_shape, out_dtype, name,
             add=None, ride=None):
    dims = {'nn': (((1,), (0,)), ((), ())), 'nt': (((1,), (1,)), ((), ())), 'tn': (((0,), (0,)), ((), ()))}[mode]
    nk = grid[-1]
    has_add = add is not None
    nr = ride.n if ride is not None else 0

    def body(*refs):
        a_ref, b_ref = refs[0], refs[1]
        pos = 2
        add_ref = refs[pos] if has_add else None
        pos += int(has_add)
        r_ins = refs[pos:pos + nr]
        o_ref = refs[pos + nr]
        r_outs = refs[pos + nr + 1:pos + 2 * nr + 1]
        acc_ref = refs[pos + 2 * nr + 1]
        r_sems = refs[pos + 2 * nr + 2:]
        ids = [pl.program_id(ax) for ax in range(len(grid))]
        q = ids[-1]
        if nr:
            @pl.when(functools.reduce(lambda u, v: u & v, [i == 0 for i in ids]))
            def _():
                ride.start(r_ins, r_outs, r_sems)

        @pl.when(q == 0)
        def _():
            acc_ref[...] = jnp.zeros_like(acc_ref)

        acc_ref[...] += lax.dot_general(a_ref[...].astype(bf16), b_ref[...].astype(bf16), dims,
                                        preferred_element_type=f32)

        @pl.when(q == nk - 1)
        def _():
            r = acc_ref[...]
            if has_add:
                r = r + add_ref[...].astype(f32)
            o_ref[...] = r.astype(out_dtype)

        if nr:
            @pl.when(functools.reduce(lambda u, v: u & v, [i == g - 1 for i, g in zip(ids, grid)]))
            def _():
                ride.wait(r_ins, r_outs, r_sems)

    in_specs = [pl.BlockSpec(a_blk, a_map), pl.BlockSpec(b_blk, b_map)]
    ops = [a, b]
    if has_add:
        in_specs.append(pl.BlockSpec(o_blk, o_map))
        ops.append(add)
    out_specs = [pl.BlockSpec(o_blk, o_map)]
    out_shapes = [jax.ShapeDtypeStruct(out_shape, out_dtype)]
    scratch = [pltpu.VMEM(o_blk, f32)]
    if nr:
        in_specs += ride.specs
        ops += ride.srcs
        out_specs += ride.specs
        out_shapes += ride.out_shape
        scratch += ride.scratch
    sem = ("arbitrary",) * len(grid) if nr else ("parallel",) * (len(grid) - 1) + ("arbitrary",)
    res = pl.pallas_call(
        body, name=name, grid=grid, in_specs=in_specs, out_specs=out_specs, out_shape=out_shapes,
        scratch_shapes=scratch, compiler_params=_cp(sem),
    )(*ops)
    return (res[0], list(res[1:])) if nr else res[0]


S5_NB = 8
S5_UB = 128
S5_SB = 512


def _s5_bu(proj, u0, bd_c, name):
    t = proj.shape[0]
    bm = _tile(t, 1024)
    ub = u0 // S5_UB
    return _mm_core(proj, bd_c, 'nn', grid=(t // bm, 2 * S5_NB, 1),
                    a_blk=(bm, S5_UB), a_map=lambda i, j, q: (i, ub + j % S5_NB),
                    b_blk=(S5_UB, S5_SB), b_map=lambda i, j, q: (j % S5_NB, j // S5_NB),
                    o_blk=(bm, S5_SB), o_map=lambda i, j, q: (i, j),
                    out_shape=(t, 2 * S5_NB * S5_SB), out_dtype=f32, name=name)


def _s5_out(s, c_c, name):
    t = s.shape[0]
    bm = _tile(t, 1024)
    return _mm_core(s, c_c, 'nn', grid=(t // bm, S5_NB, 2),
                    a_blk=(bm, S5_SB), a_map=lambda i, j, q: (i, j + S5_NB * q),
                    b_blk=(S5_SB, S5_UB), b_map=lambda i, j, q: (j + S5_NB * q, 0),
                    o_blk=(bm, S5_UB), o_map=lambda i, j, q: (i, j),
                    out_shape=(t, S5_NB * S5_UB), out_dtype=f32, name=name)


def _s5_out_dx(dyb, c_c, name):
    t = dyb.shape[0]
    bm = _tile(t, 1024)
    return _mm_core(dyb, c_c, 'nt', grid=(t // bm, 2 * S5_NB, 1),
                    a_blk=(bm, S5_UB), a_map=lambda i, j, q: (i, j % S5_NB),
                    b_blk=(S5_SB, S5_UB), b_map=lambda i, j, q: (j, 0),
                    o_blk=(bm, S5_SB), o_map=lambda i, j, q: (i, j),
                    out_shape=(t, 2 * S5_NB * S5_SB), out_dtype=f32, name=name)


def _s5_out_dw(s, dyb, name):
    t = s.shape[0]
    bk = _tile(t, 2048)
    return _mm_core(s, dyb, 'tn', grid=(2 * S5_NB, t // bk),
                    a_blk=(bk, S5_SB), a_map=lambda j, q: (q, j),
                    b_blk=(bk, S5_UB), b_map=lambda j, q: (q, j % S5_NB),
                    o_blk=(S5_SB, S5_UB), o_map=lambda j, q: (j, 0),
                    out_shape=(2 * S5_NB * S5_SB, S5_UB), out_dtype=f32, name=name)


def _s5_bu_dx(g, bd_c, add, name):
    t = g.shape[0]
    bm = _tile(t, 1024)
    return _mm_core(g, bd_c, 'nt', grid=(t // bm, S5_NB, 2),
                    a_blk=(bm, S5_SB), a_map=lambda i, j, q: (i, j + S5_NB * q),
                    b_blk=(S5_UB, S5_SB), b_map=lambda i, j, q: (j, q),
                    o_blk=(bm, S5_UB), o_map=lambda i, j, q: (i, j),
                    out_shape=(t, S5_NB * S5_UB), out_dtype=bf16, name=name, add=add)


def _s5_bu_dw(proj, u0, g, name):
    t = proj.shape[0]
    bk = _tile(t, 2048)
    ub = u0 // S5_UB
    return _mm_core(proj, g, 'tn', grid=(S5_NB, 2, t // bk),
                    a_blk=(bk, S5_UB), a_map=lambda j, r, q: (q, ub + j),
                    b_blk=(bk, S5_SB), b_map=lambda j, r, q: (q, j + S5_NB * r),
                    o_blk=(S5_UB, S5_SB), o_map=lambda j, r, q: (j, r),
                    out_shape=(S5_NB * S5_UB, 2 * S5_SB), out_dtype=f32, name=name)


def _win(r):
    return r if isinstance(r, tuple) else (r, 0, r.shape[1])


def _row_tile(t, widths):
    per_row = 48 * max(widths)
    tm = 512
    while tm > SUBLANE and tm * per_row > RW_VMEM_BUDGET:
        tm //= 2
    return min(tm, t)


def _row_spec(r, tm):
    arr, c0, w = _win(r)
    assert c0 % w == 0, (c0, w)
    cb = c0 // w
    return pl.BlockSpec((tm, w), lambda i: (i, cb))


def _full_spec(p):
    nd = p.ndim
    return pl.BlockSpec(p.shape, lambda i: (0,) * nd)


def _rw(f, rows, params, outs, *, name, accs=(), tm=None):
    t = _win(rows[0])[0].shape[0]
    tm = tm or _row_tile(t, [_win(r)[2] for r in rows] + [w for w, _ in outs])
    nr, npar, no, na = len(rows), len(params), len(outs), len(accs)

    def body(*refs):
        vals = [r[...] for r in refs[:nr + npar]]
        res = f(*vals)
        res = res if isinstance(res, (tuple, list)) else (res,)
        for o_ref, v in zip(refs[nr + npar:nr + npar + no], res[:no]):
            o_ref[...] = v.astype(o_ref.dtype)
        if na:
            first = pl.program_id(0) == 0
            for a_ref, v in zip(refs[nr + npar + no:], res[no:]):
                @pl.when(first)
                def _(a_ref=a_ref):
                    a_ref[...] = jnp.zeros_like(a_ref)
                a_ref[...] += v

    out_shape = [jax.ShapeDtypeStruct((t, w), d) for w, d in outs] + [jax.ShapeDtypeStruct(s, f32) for s in accs]
    out_specs = [pl.BlockSpec((tm, w), lambda i: (i, 0)) for w, _ in outs] + \
                [pl.BlockSpec(s, lambda i: (0, 0)) for s in accs]
    return pl.pallas_call(
        body, name=name, grid=(t // tm,),
        in_specs=[_row_spec(r, tm) for r in rows] + [_full_spec(p) for p in params],
        out_specs=out_specs, out_shape=out_shape,
        compiler_params=_cp(("arbitrary",)),
    )(*[_win(r)[0] for r in rows], *params)


def _rw_vjp(f, rows, params, cots, *, row_grads, param_grads, name, tm=None):
    t = _win(rows[0])[0].shape[0]
    cot_rows = [c for c in cots if c is not None]
    tm = tm or _row_tile(t, [_win(r)[2] for r in rows] + [_win(c)[2] for c in cot_rows])
    nr, npar, ncot = len(rows), len(params), len(cot_rows)
    d_rows = [i for i, d in enumerate(row_grads) if d is not None]
    d_pars = [i for i, d in enumerate(param_grads) if d]

    def body(*refs):
        rv = [r[...] for r in refs[:nr]]
        pv = [r[...] for r in refs[nr:nr + npar]]
        cv = [r[...] for r in refs[nr + npar:nr + npar + ncot]]
        outs_r = refs[nr + npar + ncot:nr + npar + ncot + len(d_rows)]
        outs_p = refs[nr + npar + ncot + len(d_rows):]

        def g(*diff):
            rr, pp = list(rv), list(pv)
            for i, v in zip(d_rows, diff[:len(d_rows)]):
                rr[i] = v
            for i, v in zip(d_pars, diff[len(d_rows):]):
                pp[i] = v
            res = f(*rr, *pp)
            return tuple(res) if isinstance(res, (tuple, list)) else (res,)

        prim, vjp = jax.vjp(g, *[rv[i] for i in d_rows], *[pv[i] for i in d_pars])
        it = iter(cv)
        cts = tuple(next(it).astype(o.dtype) if c is not None else jnp.zeros_like(o) for o, c in zip(prim, cots))
        grads = vjp(cts)
        for o_ref, v in zip(outs_r, grads[:len(d_rows)]):
            o_ref[...] = v.astype(o_ref.dtype)
        first = pl.program_id(0) == 0
        for o_ref, v in zip(outs_p, grads[len(d_rows):]):
            @pl.when(first)
            def _(o_ref=o_ref):
                o_ref[...] = jnp.zeros_like(o_ref)
            o_ref[...] += v.astype(f32)

    out_shape = [jax.ShapeDtypeStruct((t, _win(rows[i])[2]), row_grads[i]) for i in d_rows] + \
                [jax.ShapeDtypeStruct(params[i].shape, f32) for i in d_pars]
    out_specs = [pl.BlockSpec((tm, _win(rows[i])[2]), lambda i_: (i_, 0)) for i in d_rows] + \
                [_full_spec(params[i]) for i in d_pars]
    return pl.pallas_call(
        body, name=name, grid=(t // tm,),
        in_specs=[_row_spec(r, tm) for r in rows] + [_full_spec(p) for p in params] + [_row_spec(c, tm) for c in cot_rows],
        out_specs=out_specs, out_shape=out_shape,
        compiler_params=_cp(("arbitrary",)),
    )(*[_win(r)[0] for r in rows], *params, *[_win(c)[0] for c in cot_rows])


def _rms(x, g):
    return x * lax.rsqrt(jnp.mean(x * x, axis=-1, keepdims=True) + EPS) * g


def _f_mod(x, nw, sh, sc):
    return (_rms(x, nw) * (1.0 + sc) + sh).astype(bf16)


def _f_mod_keep(x, nw, sh, sc):
    return _f_mod(x, nw, sh, sc), x


def _f_res_mod(coef, x, o, g, nw, sh, sc):
    x1 = x + coef * g * o.astype(f32)
    return x1, _f_mod(x1, nw, sh, sc)


def _f_swiglu(ab):
    h = ab.shape[1] // 2
    a = ab[:, :h].astype(f32)
    b = ab[:, h:].astype(f32)
    return (jax.nn.silu(a) * b).astype(bf16)


def _f_ssd_pre(pre, dtraw, bias, sel):
    xc = jax.nn.silu(pre)
    dt = jax.nn.softplus(dtraw + bias)
    dt4 = jnp.dot(dt, sel, precision=HI, preferred_element_type=f32)
    return xc[:, :SSD_DI], xc[:, SSD_DI:SSD_DI + SSD_G * SSD_N], xc[:, SSD_DI + SSD_G * SSD_N:], dt4


def _f_ssd_post(y, z, nw):
    yz = y * jax.nn.silu(z)
    w = SSD_DI // SSD_G
    parts = []
    for g in range(SSD_G):
        s = yz[:, g * w:(g + 1) * w]
        parts.append(s * lax.rsqrt(jnp.mean(s * s, axis=-1, keepdims=True) + EPS))
    return (jnp.concatenate(parts, axis=1) * nw).astype(bf16)


def _f_s5_post(yb, u, d):
    return jax.nn.gelu(yb + d * u).astype(bf16)


def _f_merge(pa, glu_a, glu_g, ga, gb):
    pb = glu_a * jax.nn.sigmoid(glu_g)
    return (jax.nn.sigmoid(ga) * pa + jax.nn.sigmoid(gb) * pb).astype(bf16)


def _f_final(x2, o, tgt, g, nw):
    x3 = x2 + 0.5 * g * o.astype(f32)
    y = _rms(x3, nw)
    return 0.5 * jnp.mean(jnp.square(y - tgt), axis=-1, keepdims=True)


def _f_final_loss(x2, o, tgt, g, nw):
    rows = _f_final(x2, o, tgt, g, nw)
    return jnp.broadcast_to(jnp.sum(rows, axis=0, keepdims=True), (1, LANE))


def _f_s5_prep(lr, li, ldt, br, bi):
    dt = jnp.exp(ldt)
    lr = jnp.minimum(lr, -1e-4)
    mag = jnp.exp(lr * dt)
    ar = mag * jnp.cos(li * dt)
    ai = mag * jnp.sin(li * dt)
    den = lr * lr + li * li
    nr = ar - 1.0
    kr = (nr * lr + ai * li) / den
    ki = (ai * lr - nr * li) / den
    return ar, ai, kr * br - ki * bi, kr * bi + ki * br


def _shift_down(cur, halo8, j):
    if j == 0:
        return cur
    rolled = pltpu.roll(cur, j, 0)
    row8 = lax.broadcasted_iota(jnp.int32, halo8.shape, 0)
    top = jnp.where(row8 < j, pltpu.roll(halo8, j, 0), rolled[:SUBLANE])
    return jnp.concatenate([top, rolled[SUBLANE:]], axis=0)


def _shift_up(cur, halo8, j):
    if j == 0:
        return cur
    n = cur.shape[0]
    rolled = pltpu.roll(cur, n - j, 0)
    row8 = lax.broadcasted_iota(jnp.int32, halo8.shape, 0)
    bot = jnp.where(row8 >= SUBLANE - j, pltpu.roll(halo8, SUBLANE - j, 0), rolled[n - SUBLANE:])
    return jnp.concatenate([rolled[:n - SUBLANE], bot], axis=0)


def _conv_fwd(proj, c0, w8, b, name):
    t = proj.shape[0]
    cw = 1024
    tm = min(512, t)
    cb0 = c0 // cw
    r8 = tm // SUBLANE

    def body(x_ref, h_ref, w_ref, b_ref, o_ref):
        i = pl.program_id(1)
        x = x_ref[...]
        halo = jnp.where(i > 0, h_ref[...], 0.0)
        acc = b_ref[...] + w_ref[CONV_K - 1:CONV_K, :] * x
        for j in range(1, CONV_K):
            acc = acc + w_ref[CONV_K - 1 - j:CONV_K - j, :] * _shift_down(x, halo, j)
        o_ref[...] = acc

    return pl.pallas_call(
        body, name=name, grid=(CONV_DIM // cw, t // tm),
        in_specs=[pl.BlockSpec((tm, cw), lambda c, i: (i, cb0 + c)),
                  pl.BlockSpec((SUBLANE, cw), lambda c, i: (jnp.maximum(i * r8 - 1, 0), cb0 + c)),
                  pl.BlockSpec((SUBLANE, cw), lambda c, i: (0, c)),
                  pl.BlockSpec((1, cw), lambda c, i: (0, c))],
        out_specs=pl.BlockSpec((tm, cw), lambda c, i: (i, c)),
        out_shape=jax.ShapeDtypeStruct((t, CONV_DIM), f32),
        compiler_params=_cp(("parallel", "arbitrary")),
    )(proj, proj, w8, b)


def _conv_bwd(dpre, proj, c0, w8, name):
    t = proj.shape[0]
    cw = 1024
    tm = min(512, t)
    cb0 = c0 // cw
    r8 = tm // SUBLANE
    nb = t // tm

    def body(d_ref, dn_ref, x_ref, xh_ref, w_ref, dx_ref, dw_ref, db_ref):
        i = pl.program_id(1)
        d = d_ref[...]
        dn = jnp.where(i < nb - 1, dn_ref[...], 0.0)
        x = x_ref[...]
        xh = jnp.where(i > 0, xh_ref[...], 0.0)

        @pl.when(i == 0)
        def _():
            dw_ref[...] = jnp.zeros_like(dw_ref)
            db_ref[...] = jnp.zeros_like(db_ref)

        dx = w_ref[CONV_K - 1:CONV_K, :] * d
        rows = [jnp.sum(d * x, axis=0, keepdims=True)]
        for j in range(1, CONV_K):
            dx = dx + w_ref[CONV_K - 1 - j:CONV_K - j, :] * _shift_up(d, dn, j)
            rows.append(jnp.sum(d * _shift_down(x, xh, j), axis=0, keepdims=True))
        dx_ref[...] = dx.astype(dx_ref.dtype)
        dw = jnp.concatenate([rows[CONV_K - 1 - k] for k in range(CONV_K)] +
                             [jnp.zeros((SUBLANE - CONV_K, cw), f32)], axis=0)
        dw_ref[...] += dw
        db_ref[...] += jnp.sum(d, axis=0, keepdims=True)

    return pl.pallas_call(
        body, name=name, grid=(CONV_DIM // cw, nb),
        in_specs=[pl.BlockSpec((tm, cw), lambda c, i: (i, c)),
                  pl.BlockSpec((SUBLANE, cw), lambda c, i: (jnp.minimum((i + 1) * r8, nb * r8 - 1), c)),
                  pl.BlockSpec((tm, cw), lambda c, i: (i, cb0 + c)),
                  pl.BlockSpec((SUBLANE, cw), lambda c, i: (jnp.maximum(i * r8 - 1, 0), cb0 + c)),
                  pl.BlockSpec((SUBLANE, cw), lambda c, i: (0, c))],
        out_specs=[pl.BlockSpec((tm, cw), lambda c, i: (i, c)),
                   pl.BlockSpec((SUBLANE, cw), lambda c, i: (0, c)),
                   pl.BlockSpec((1, cw), lambda c, i: (0, c))],
        out_shape=[jax.ShapeDtypeStruct((t, CONV_DIM), bf16), jax.ShapeDtypeStruct((SUBLANE, CONV_DIM), f32),
                   jax.ShapeDtypeStruct((1, CONV_DIM), f32)],
        compiler_params=_cp(("parallel", "arbitrary")),
    )(dpre, dpre, proj, proj, w8)


def _ssd_chunk(xs, bm, cm, dt, a, dsk, h):
    n = SSD_L
    row = lax.broadcasted_iota(jnp.int32, (n, n), 0)
    col = lax.broadcasted_iota(jnp.int32, (n, n), 1)
    causal = row >= col
    cs = jnp.dot(causal.astype(f32), dt * a, precision=HI, preferred_element_type=f32)
    cs_t = cs.T
    nt = (((1,), (1,)), ((), ()))
    cb = lax.dot_general(cm.astype(bf16), bm.astype(bf16), nt, preferred_element_type=f32)
    ys, hn = [], []
    for r in range(SSD_R):
        xr = xs[:, r * SSD_P:(r + 1) * SSD_P]
        xdt = xr * dt[:, r:r + 1]
        c_col = cs[:, r:r + 1]
        decay = jnp.exp(jnp.where(causal, c_col - cs_t[r:r + 1, :], -1e30))
        y_diag = jnp.dot((cb * decay).astype(bf16), xdt.astype(bf16), preferred_element_type=f32)
        hr = h[r * SSD_P:(r + 1) * SSD_P, :]
        y_off = lax.dot_general(cm.astype(bf16), hr.astype(bf16), nt, preferred_element_type=f32) * jnp.exp(c_col)
        last = cs[n - 1:n, r:r + 1]
        st = lax.dot_general((xdt * jnp.exp(last - c_col)).astype(bf16), bm.astype(bf16), (((0,), (0,)), ((), ())),
                             preferred_element_type=f32)
        hn.append(jnp.exp(last) * hr + st)
        ys.append(y_diag + y_off + dsk[:, r:r + 1] * xr)
    return jnp.concatenate(ys, axis=1), jnp.concatenate(hn, axis=0)


SSD_GB = 1


def _ssd_specs(nc, rev):
    ch = (lambda c: nc - 1 - c) if rev else (lambda c: c)
    gw = SSD_GB * SSD_R * SSD_P
    return [pl.BlockSpec((SSD_L, gw), lambda g, c: (ch(c), g)),
            pl.BlockSpec((SSD_L, SSD_GB * SSD_N), lambda g, c: (ch(c), g)),
            pl.BlockSpec((SSD_L, SSD_GB * SSD_N), lambda g, c: (ch(c), g)),
            pl.BlockSpec((SSD_L, SSD_GB * LANE), lambda g, c: (ch(c), g)),
            pl.BlockSpec((1, SSD_GB * LANE), lambda g, c: (0, g)),
            pl.BlockSpec((1, SSD_GB * LANE), lambda g, c: (0, g))]


def _ssd_group(refs, q):
    gw = SSD_R * SSD_P
    xs_ref, bm_ref, cm_ref, dt_ref, a_ref, dsk_ref = refs
    ln = slice(q * LANE, (q + 1) * LANE)
    return (xs_ref[:, q * gw:(q + 1) * gw], bm_ref[:, ln], cm_ref[:, ln], dt_ref[:, ln], a_ref[:, ln], dsk_ref[:, ln])


def _ssd_fwd(xs, bm, cm, dt4, a4, dsk4, name, ride=None):
    t = xs.shape[0]
    nc = t // SSD_L
    gw = SSD_R * SSD_P

    nr = ride.n if ride is not None else 0
    ng = SSD_G // SSD_GB

    def body(*refs):
        xs_ref, bm_ref, cm_ref, dt_ref, a_ref, dsk_ref = refs[:6]
        r_ins = refs[6:6 + nr]
        y_ref, hs_ref = refs[6 + nr:8 + nr]
        r_outs = refs[8 + nr:8 + 2 * nr]
        h_ref = refs[8 + 2 * nr]
        r_sems = refs[9 + 2 * nr:]
        g, c = pl.program_id(0), pl.program_id(1)
        if nr:
            @pl.when((g == 0) & (c == 0))
            def _():
                ride.start(r_ins, r_outs, r_sems)

        @pl.when(c == 0)
        def _():
            h_ref[...] = jnp.zeros_like(h_ref)

        hs_ref[...] = h_ref[...]
        grp = (xs_ref, bm_ref, cm_ref, dt_ref, a_ref, dsk_ref)
        ops = [_ssd_group(grp, q) + (h_ref[q * gw:(q + 1) * gw, :],) for q in range(SSD_GB)]
        res = [_ssd_chunk(*o) for o in ops]
        for q, (y, hn) in enumerate(res):
            y_ref[:, q * gw:(q + 1) * gw] = y
            h_ref[q * gw:(q + 1) * gw, :] = hn

        if nr:
            @pl.when((g == ng - 1) & (c == nc - 1))
            def _():
                ride.wait(r_ins, r_outs, r_sems)

    res = pl.pallas_call(
        body, name=name, grid=(ng, nc), in_specs=_ssd_specs(nc, False) + (ride.specs if nr else []),
        out_specs=[pl.BlockSpec((SSD_L, SSD_GB * gw), lambda g, c: (c, g)),
                   pl.BlockSpec((None, None, SSD_GB * gw, SSD_N), lambda g, c: (g, c, 0, 0))] +
                  (ride.specs if nr else []),
        out_shape=[jax.ShapeDtypeStruct((t, SSD_DI), f32),
                   jax.ShapeDtypeStruct((ng, nc, SSD_GB * gw, SSD_N), f32)] + (ride.out_shape if nr else []),
        scratch_shapes=[pltpu.VMEM((SSD_GB * gw, SSD_N), f32)] + (ride.scratch if nr else []),
        compiler_params=_cp(("arbitrary", "arbitrary")),
    )(xs, bm, cm, dt4, a4, dsk4, *(ride.srcs if nr else []))
    return res[0], res[1], list(res[2:])


def _ssd_bwd(xs, bm, cm, dt4, a4, dsk4, hs, dy, name, ride=None):
    t = xs.shape[0]
    nc = t // SSD_L
    gw = SSD_R * SSD_P
    rc = lambda c: nc - 1 - c
    nr = ride.n if ride is not None else 0
    ng = SSD_G // SSD_GB

    def body(*refs):
        xs_ref, bm_ref, cm_ref, dt_ref, a_ref, dsk_ref, hs_ref, dy_ref = refs[:8]
        r_ins = refs[8:8 + nr]
        dxs_ref, dbm_ref, dcm_ref, ddt_ref, da_ref, ddsk_ref = refs[8 + nr:14 + nr]
        r_outs = refs[14 + nr:14 + 2 * nr]
        dh_ref = refs[14 + 2 * nr]
        r_sems = refs[15 + 2 * nr:]
        if nr:
            @pl.when((pl.program_id(0) == 0) & (pl.program_id(1) == 0))
            def _():
                ride.start(r_ins, r_outs, r_sems)

        @pl.when(pl.program_id(1) == 0)
        def _():
            dh_ref[...] = jnp.zeros_like(dh_ref)
            da_ref[...] = jnp.zeros_like(da_ref)
            ddsk_ref[...] = jnp.zeros_like(ddsk_ref)

        grp = (xs_ref, bm_ref, cm_ref, dt_ref, a_ref, dsk_ref)
        ops = [_ssd_group(grp, q) + (hs_ref[q * gw:(q + 1) * gw, :],) for q in range(SSD_GB)]
        cts = [(dy_ref[:, q * gw:(q + 1) * gw], dh_ref[q * gw:(q + 1) * gw, :]) for q in range(SSD_GB)]
        grads = [jax.vjp(_ssd_chunk, *o)[1](ct) for o, ct in zip(ops, cts)]
        for q, (dxs, dbm, dcm, ddt, da, ddsk, dh) in enumerate(grads):
            rows = slice(q * gw, (q + 1) * gw)
            ln = slice(q * LANE, (q + 1) * LANE)
            dxs_ref[:, rows] = dxs
            dbm_ref[:, ln] = dbm
            dcm_ref[:, ln] = dcm
            ddt_ref[:, ln] = ddt
            da_ref[:, ln] += da
            ddsk_ref[:, ln] += ddsk
            dh_ref[rows, :] = dh

        if nr:
            @pl.when((pl.program_id(0) == ng - 1) & (pl.program_id(1) == nc - 1))
            def _():
                ride.wait(r_ins, r_outs, r_sems)

    res = pl.pallas_call(
        body, name=name, grid=(ng, nc),
        in_specs=_ssd_specs(nc, True) + [
            pl.BlockSpec((None, None, SSD_GB * gw, SSD_N), lambda g, c: (g, rc(c), 0, 0)),
            pl.BlockSpec((SSD_L, SSD_GB * gw), lambda g, c: (rc(c), g))] + (ride.specs if nr else []),
        out_specs=[pl.BlockSpec((SSD_L, SSD_GB * gw), lambda g, c: (rc(c), g)),
                   pl.BlockSpec((SSD_L, SSD_GB * SSD_N), lambda g, c: (rc(c), g)),
                   pl.BlockSpec((SSD_L, SSD_GB * SSD_N), lambda g, c: (rc(c), g)),
                   pl.BlockSpec((SSD_L, SSD_GB * LANE), lambda g, c: (rc(c), g)),
                   pl.BlockSpec((1, SSD_GB * LANE), lambda g, c: (0, g)),
                   pl.BlockSpec((1, SSD_GB * LANE), lambda g, c: (0, g))] + (ride.specs if nr else []),
        out_shape=[jax.ShapeDtypeStruct((t, SSD_DI), f32), jax.ShapeDtypeStruct((t, SSD_G * SSD_N), f32),
                   jax.ShapeDtypeStruct((t, SSD_G * SSD_N), f32), jax.ShapeDtypeStruct((t, SSD_G * LANE), f32),
                   jax.ShapeDtypeStruct((1, SSD_G * LANE), f32), jax.ShapeDtypeStruct((1, SSD_G * LANE), f32)] +
                  (ride.out_shape if nr else []),
        scratch_shapes=[pltpu.VMEM((SSD_GB * gw, SSD_N), f32)] + (ride.scratch if nr else []),
        compiler_params=_cp(("arbitrary", "arbitrary")),
    )(xs, bm, cm, dt4, a4, dsk4, hs, dy, *(ride.srcs if nr else []))
    return list(res[:6]), list(res[6:])


S5_CH = 1024


def _s5_scan(bu, ar, ai, name):
    t = bu.shape[0]
    tb = min(128, t)

    def body(bu_ref, ar_ref, ai_ref, s_ref, carry):
        @pl.when(pl.program_id(0) == 0)
        def _():
            carry[...] = jnp.zeros_like(carry)

        for c0 in range(0, S5_S, S5_CH):
            re = pl.ds(c0, S5_CH)
            im = pl.ds(S5_S + c0, S5_CH)
            a_r = ar_ref[:, re]
            a_i = ai_ref[:, re]

            def step(k, st, re=re, im=im, a_r=a_r, a_i=a_i):
                sr, si = st
                row = pl.ds(k, 1)
                nr = a_r * sr - a_i * si + bu_ref[row, re]
                ni = a_r * si + a_i * sr + bu_ref[row, im]
                s_ref[row, re] = nr
                s_ref[row, im] = ni
                return nr, ni

            sr, si = lax.fori_loop(0, tb, step, (carry[:, re], carry[:, im]))
            carry[:, re] = sr
            carry[:, im] = si

    return pl.pallas_call(
        body, name=name, grid=(t // tb,),
        in_specs=[pl.BlockSpec((tb, 2 * S5_S), lambda i: (i, 0)),
                  pl.BlockSpec((1, S5_S), lambda i: (0, 0)), pl.BlockSpec((1, S5_S), lambda i: (0, 0))],
        out_specs=pl.BlockSpec((tb, 2 * S5_S), lambda i: (i, 0)),
        out_shape=jax.ShapeDtypeStruct((t, 2 * S5_S), f32),
        scratch_shapes=[pltpu.VMEM((1, 2 * S5_S), f32)],
        compiler_params=_cp(("arbitrary",)),
    )(bu, ar, ai)


def _s5_scan_bwd(ds, s, ar, ai, name):
    t = ds.shape[0]
    tb = min(128, t)
    nb = t // tb
    r8 = tb // SUBLANE
    rb = lambda i: nb - 1 - i

    def body(ds_ref, s_ref, sh_ref, ar_ref, ai_ref, g_ref, dar_ref, dai_ref, carry):
        i = pl.program_id(0)

        @pl.when(i == 0)
        def _():
            carry[...] = jnp.zeros_like(carry)
            dar_ref[...] = jnp.zeros_like(dar_ref)
            dai_ref[...] = jnp.zeros_like(dai_ref)

        has_prev = (i < nb - 1).astype(f32)
        for c0 in range(0, S5_S, S5_CH):
            re = pl.ds(c0, S5_CH)
            im = pl.ds(S5_S + c0, S5_CH)
            a_r = ar_ref[:, re]
            a_i = ai_ref[:, re]

            def upd(st, row, sp_r, sp_i, re=re, im=im, a_r=a_r, a_i=a_i):
                gr, gi, acr, aci = st
                ngr = ds_ref[row, re] + a_r * gr + a_i * gi
                ngi = ds_ref[row, im] + a_r * gi - a_i * gr
                g_ref[row, re] = ngr
                g_ref[row, im] = ngi
                return ngr, ngi, acr + ngr * sp_r + ngi * sp_i, aci + ngi * sp_r - ngr * sp_i

            def step(k, st, re=re, im=im, upd=upd):
                tt = tb - 1 - k
                prev = pl.ds(tt - 1, 1)
                return upd(st, pl.ds(tt, 1), s_ref[prev, re], s_ref[prev, im])

            zero = jnp.zeros((1, S5_CH), f32)
            st = lax.fori_loop(0, tb - 1, step, (carry[:, re], carry[:, im], zero, zero))
            last = pl.ds(SUBLANE - 1, 1)
            gr, gi, acr, aci = upd(st, pl.ds(0, 1), sh_ref[last, re] * has_prev, sh_ref[last, im] * has_prev)
            carry[:, re] = gr
            carry[:, im] = gi
            dar_ref[:, re] += acr
            dai_ref[:, re] += aci

    return pl.pallas_call(
        body, name=name, grid=(nb,),
        in_specs=[pl.BlockSpec((tb, 2 * S5_S), lambda i: (rb(i), 0)),
                  pl.BlockSpec((tb, 2 * S5_S), lambda i: (rb(i), 0)),
                  pl.BlockSpec((SUBLANE, 2 * S5_S), lambda i: (jnp.maximum(rb(i) * r8 - 1, 0), 0)),
                  pl.BlockSpec((1, S5_S), lambda i: (0, 0)), pl.BlockSpec((1, S5_S), lambda i: (0, 0))],
        out_specs=[pl.BlockSpec((tb, 2 * S5_S), lambda i: (rb(i), 0)),
                   pl.BlockSpec((1, S5_S), lambda i: (0, 0)), pl.BlockSpec((1, S5_S), lambda i: (0, 0))],
        out_shape=[jax.ShapeDtypeStruct((t, 2 * S5_S), f32), jax.ShapeDtypeStruct((1, S5_S), f32),
                   jax.ShapeDtypeStruct((1, S5_S), f32)],
        scratch_shapes=[pltpu.VMEM((1, 2 * S5_S), f32)],
        compiler_params=_cp(("arbitrary",)),
    )(ds, s, s, ar, ai)


def _ada_fwd(c_all, w, b, name):
    d, n = w.shape
    tn = _tile(n, 1536)

    def body(c_ref, w_ref, b_ref, o_ref):
        a = jax.nn.silu(c_ref[...]).astype(bf16)
        o_ref[...] = jnp.dot(a, w_ref[...].astype(bf16), preferred_element_type=f32) + b_ref[...]

    return pl.pallas_call(
        body, name=name, grid=(n // tn,),
        in_specs=[pl.BlockSpec(c_all.shape, lambda j: (0, 0)), pl.BlockSpec((d, tn), lambda j: (0, j)),
                  pl.BlockSpec((1, tn), lambda j: (0, j))],
        out_specs=pl.BlockSpec((c_all.shape[0], tn), lambda j: (0, j)),
        out_shape=jax.ShapeDtypeStruct((c_all.shape[0], n), f32),
        compiler_params=_cp(("parallel",)),
    )(c_all, w, b)


def _ada_bwd(c_all, dm, name):
    d = c_all.shape[1]
    n = dm.shape[1]
    tn = _tile(n, 1536)

    def body(c_ref, dm_ref, o_ref):
        a = jax.nn.silu(c_ref[...]).astype(bf16)
        o_ref[...] = lax.dot_general(a, dm_ref[...].astype(bf16), (((0,), (0,)), ((), ())), preferred_element_type=f32)

    return pl.pallas_call(
        body, name=name, grid=(n // tn,),
        in_specs=[pl.BlockSpec(c_all.shape, lambda j: (0, 0)), pl.BlockSpec((dm.shape[0], tn), lambda j: (0, j))],
        out_specs=pl.BlockSpec((d, tn), lambda j: (0, j)),
        out_shape=jax.ShapeDtypeStruct((d, n), f32),
        compiler_params=_cp(("parallel",)),
    )(c_all, dm)


def _blk_rows(r, c, nbuf, itemsize=4):
    tr = _tile(r, max(SUBLANE, (RW_VMEM_BUDGET // (2 * nbuf * c * itemsize)) // 16 * 16), 16)
    return tr if r % tr == 0 else r


def _cast_bf16(w, name):
    r, c = w.shape
    tr = _blk_rows(r, c, 2)

    def body(w_ref, o_ref):
        o_ref[...] = w_ref[...].astype(bf16)

    return pl.pallas_call(
        body, name=name, grid=(r // tr,), in_specs=[pl.BlockSpec((tr, c), lambda i: (i, 0))],
        out_specs=pl.BlockSpec((tr, c), lambda i: (i, 0)), out_shape=jax.ShapeDtypeStruct((r, c), bf16),
        compiler_params=_cp(("parallel",)),
    )(w)


def _sum_lead(parts, name):
    n, r, c = parts.shape
    tr = _blk_rows(r, c, n + 2)

    def body(p_ref, o_ref):
        acc = p_ref[0].astype(f32)
        for q in range(1, n):
            acc = acc + p_ref[q].astype(f32)
        o_ref[...] = acc

    return pl.pallas_call(
        body, name=name, grid=(r // tr,), in_specs=[pl.BlockSpec((n, tr, c), lambda i: (0, i, 0))],
        out_specs=pl.BlockSpec((tr, c), lambda i: (i, 0)), out_shape=jax.ShapeDtypeStruct((r, c), f32),
        compiler_params=_cp(("parallel",)),
    )(parts)


def _adamw(w, m, v, parts, name):
    r, c = w.shape
    npart = len(parts)
    tr = _blk_rows(r, c, 7 + npart)
    c1 = 1.0 - ADAM_B1 ** ADAM_STEP
    c2 = 1.0 - ADAM_B2 ** ADAM_STEP

    def body(*refs):
        w_ref, m_ref, v_ref = refs[:3]
        g_ref, d_ref, nm_ref, nv_ref = refs[3 + npart:]
        g = refs[3][...].astype(f32)
        for p in refs[4:3 + npart]:
            g = g + p[...].astype(f32)
        nm = ADAM_B1 * m_ref[...] + (1.0 - ADAM_B1) * g
        nv = ADAM_B2 * v_ref[...] + (1.0 - ADAM_B2) * jnp.square(g)
        g_ref[...] = g
        nm_ref[...] = nm
        nv_ref[...] = nv
        d_ref[...] = -ADAM_LR * ((nm / c1) / (jnp.sqrt(nv / c2) + ADAM_EPS) + ADAM_WD * w_ref[...])

    spec = pl.BlockSpec((tr, c), lambda i: (i, 0))
    return pl.pallas_call(
        body, name=name, grid=(r // tr,), in_specs=[spec] * (3 + npart), out_specs=[spec] * 4,
        out_shape=[jax.ShapeDtypeStruct((r, c), f32)] * 4, compiler_params=_cp(("parallel",)),
    )(w, m, v, *parts)


def _ag_small(x_shard, name):
    m_per, n = x_shard.shape

    def body(x_ref, out_ref, send_sems, recv_sems, local_sem):
        x, y, c = lax.axis_index("x"), lax.axis_index("y"), lax.axis_index("c")
        me, sibling = (x, y, c), (x, y, 1 - c)
        chips = [(1 - x, y), (x, 1 - y), (1 - x, 1 - y)]

        def rows(px, py, pc):
            return out_ref.at[pl.ds((4 * px + 2 * py + pc) * m_per, m_per), :]

        def copy(k, block, to, src=None):
            return pltpu.make_async_remote_copy(
                src_ref=rows(*block) if src is None else src, dst_ref=rows(*block),
                send_sem=send_sems.at[k], recv_sem=recv_sems.at[k], device_id=to, device_id_type=MESH)

        mine = pltpu.make_async_copy(x_ref, rows(*me), local_sem)
        mine.start()
        first = [copy(0, me, sibling, src=x_ref)]
        first += [copy(1 + j, me, (*chip, c), src=x_ref) for j, chip in enumerate(chips)]
        for cp in first:
            cp.start()
        passed = [copy(4 + j, (*chip, c), sibling) for j, chip in enumerate(chips)]
        for j, chip in enumerate(chips):
            copy(1 + j, (*chip, c), me).wait_recv()
            passed[j].start()
        copy(0, sibling, me).wait_recv()
        for j, chip in enumerate(chips):
            copy(4 + j, (*chip, 1 - c), me).wait_recv()
        for cp in first + passed:
            cp.wait_send()
        mine.wait()

    return pl.pallas_call(
        body, name=name, out_shape=jax.ShapeDtypeStruct((8 * m_per, n), x_shard.dtype),
        in_specs=[pl.BlockSpec(memory_space=pltpu.VMEM)], out_specs=pl.BlockSpec(memory_space=pltpu.VMEM),
        scratch_shapes=[pltpu.SemaphoreType.DMA((7,)), pltpu.SemaphoreType.DMA((7,)), pltpu.SemaphoreType.DMA],
        compiler_params=pltpu.CompilerParams(vmem_limit_bytes=VMEM_LIMIT),
    )(x_shard)


def _xchg(srcs, scatter, name):
    ride = _Ride(srcs, scatter)
    n = ride.n

    def body(*refs):
        ride.start(refs[:n], refs[n:2 * n], refs[2 * n:])
        ride.wait(refs[:n], refs[n:2 * n], refs[2 * n:])

    return pl.pallas_call(
        body, name=name, out_shape=ride.out_shape, in_specs=ride.specs, out_specs=ride.specs,
        scratch_shapes=ride.scratch,
    )(*srcs)


class _Ride:
    def __init__(self, srcs, scatter):
        self.srcs, self.scatter, self.n = list(srcs), scatter, len(srcs)
        n = self.n
        self.out_shape = [jax.ShapeDtypeStruct(s.shape if scatter else (4,) + s.shape, s.dtype) for s in srcs]
        self.specs = [pl.BlockSpec(memory_space=pl.ANY)] * n
        self.scratch = [pltpu.SemaphoreType.DMA((3 * n,)), pltpu.SemaphoreType.DMA((3 * n,)),
                        pltpu.SemaphoreType.DMA((n,))]

    def _copies(self, ins, outs, sems):
        send_sems, recv_sems, local_sems = sems
        x, y, c = lax.axis_index("x"), lax.axis_index("y"), lax.axis_index("c")
        my_k = 2 * x + y
        peers = [(1 - x, y), (x, 1 - y), (1 - x, 1 - y)]
        local, sends, recvs = [], [], []
        for a in range(self.n):
            own = ins[a].at[my_k] if self.scatter else ins[a]
            local.append(pltpu.make_async_copy(own, outs[a].at[my_k], local_sems.at[a]))
            for j, (px, py) in enumerate(peers):
                sems_j = dict(send_sem=send_sems.at[3 * a + j], recv_sem=recv_sems.at[3 * a + j],
                              device_id=(px, py, c), device_id_type=MESH)
                src = ins[a].at[2 * px + py] if self.scatter else ins[a]
                sends.append(pltpu.make_async_remote_copy(src_ref=src, dst_ref=outs[a].at[my_k], **sems_j))
                landed = outs[a].at[2 * px + py]
                recvs.append(pltpu.make_async_remote_copy(src_ref=landed, dst_ref=landed, **sems_j))
        return local, sends, recvs

    def start(self, ins, outs, sems):
        local, sends, _ = self._copies(ins, outs, sems)
        for cp in local + sends:
            cp.start()

    def wait(self, ins, outs, sems):
        local, sends, recvs = self._copies(ins, outs, sems)
        for cp in recvs:
            cp.wait_recv()
        for cp in sends:
            cp.wait_send()
        for cp in local:
            cp.wait()


def _swap_sibling(srcs, name):
    n = len(srcs)

    def body(*refs):
        ins, outs = refs[:n], refs[n:2 * n]
        send_sems, recv_sems = refs[2 * n:]
        sib = (lax.axis_index("x"), lax.axis_index("y"), 1 - lax.axis_index("c"))
        cps = [pltpu.make_async_remote_copy(src_ref=ins[a], dst_ref=outs[a], send_sem=send_sems.at[a],
                                            recv_sem=recv_sems.at[a], device_id=sib, device_id_type=MESH)
               for a in range(n)]
        for cp in cps:
            cp.start()
        for cp in cps:
            cp.wait_recv()
        for cp in cps:
            cp.wait_send()

    anyspec = pl.BlockSpec(memory_space=pl.ANY)
    return pl.pallas_call(
        body, name=name, out_shape=[jax.ShapeDtypeStruct(s.shape, s.dtype) for s in srcs],
        in_specs=[anyspec] * n, out_specs=[anyspec] * n,
        scratch_shapes=[pltpu.SemaphoreType.DMA((n,)), pltpu.SemaphoreType.DMA((n,))],
    )(*srcs)


def _gather8(vec, name):
    size = vec.shape[0]
    n = _round_up(size, SUBLANE * LANE)
    blk = jnp.concatenate([vec, jnp.zeros((n - size,), f32)]).reshape(SUBLANE, n // SUBLANE)
    out = _ag_small(blk, name)
    return out.reshape(8, n)[:, :size]


def _sel_matrix():
    sel = np.zeros((LANE, SSD_G * LANE), np.float32)
    for h in range(SSD_HEADS):
        sel[h, (h // SSD_R) * LANE + h % SSD_R] = 1.0
    return jnp.asarray(sel)


def _heads_to_lanes(v):
    z = jnp.zeros((SSD_G, LANE), f32).at[:, :SSD_R].set(v.reshape(SSD_G, SSD_R))
    return z.reshape(1, SSD_G * LANE)


def _lanes_to_heads(v):
    return v.reshape(SSD_G, LANE)[:, :SSD_R].reshape(SSD_HEADS)


def _block_diag8(blocks):
    g, r, c = blocks.shape
    b = blocks.reshape(g // S5_NB, S5_NB, r, c)
    eye = jnp.eye(S5_NB, dtype=bool)[None, :, None, :, None]
    return jnp.where(eye, b[:, :, :, None, :], jnp.zeros((), blocks.dtype)).reshape(g * r, S5_NB * c)


def _diag8(mat, r, c):
    g = mat.shape[0] // r
    m = mat.reshape(g // S5_NB, S5_NB, r, S5_NB, c)
    eye = jnp.eye(S5_NB, dtype=bool)[None, :, None, :, None]
    return jnp.where(eye, m, 0.0).sum(axis=3).reshape(g, r, c)


class _Layout:
    def __init__(self, d):
        self.d = d
        self.z, self.xbc, self.u = 0, SSD_DI, SSD_DI + CONV_DIM
        self.ga = self.u + S5_W
        self.gb = self.ga + d
        self.dt = self.gb + d
        self.np_ = self.dt + LANE
        self.in_cols = SSD_DI + CONV_DIM + SSD_HEADS + S5_W + 2 * d
        off_dt = SSD_DI + CONV_DIM
        off_u = off_dt + SSD_HEADS
        off_g = off_u + S5_W
        self.src = [(0, off_dt), (off_u, off_u + S5_W + 2 * d), (off_dt, off_u)]

    def arrange(self, w):
        (a0, a1), (b0, b1), (c0, c1) = self.src
        pad = jnp.zeros((w.shape[0], LANE - SSD_HEADS), w.dtype)
        return jnp.concatenate([w[:, a0:a1], w[:, b0:b1], w[:, c0:c1], pad], axis=1)

    def restore(self, w):
        n_a = self.src[0][1]
        n_b = self.src[1][1] - self.src[1][0]
        return jnp.concatenate([w[:, :n_a], w[:, n_a + n_b:n_a + n_b + SSD_HEADS], w[:, n_a:n_a + n_b]], axis=1)


def _unshard_cols(g):
    return g.transpose(1, 0, 2).reshape(g.shape[1], 4 * g.shape[2])


def _shard_cols(w):
    r, c4 = w.shape
    return w.reshape(r, 4, c4 // 4).transpose(1, 0, 2)


def kernel(x, c, w_ada, b_ada, norm_ffn1, w_ffn1_in, w_ffn1_out, norm_mix, w_in, conv_w, conv_b, dt_bias, a_log, d_ssd, ssd_norm_w, w_a_proj, s5_lambda_re, s5_lambda_im, s5_b_re, s5_b_im, s5_c_re, s5_c_im, s5_d, s5_log_dt, w_b_glu, w_out, norm_ffn2, w_ffn2_in, w_ffn2_out, norm_final, loss_target, m_w_ada, m_b_ada, m_norm_ffn1, m_w_ffn1_in, m_w_ffn1_out, m_norm_mix, m_w_in, m_conv_w, m_conv_b, m_dt_bias, m_a_log, m_d_ssd, m_ssd_norm_w, m_w_a_proj, m_s5_lambda_re, m_s5_lambda_im, m_s5_b_re, m_s5_b_im, m_s5_c_re, m_s5_c_im, m_s5_d, m_s5_log_dt, m_w_b_glu, m_w_out, m_norm_ffn2, m_w_ffn2_in, m_w_ffn2_out, m_norm_final, v_w_ada, v_b_ada, v_norm_ffn1, v_w_ffn1_in, v_w_ffn1_out, v_norm_mix, v_w_in, v_conv_w, v_conv_b, v_dt_bias, v_a_log, v_d_ssd, v_ssd_norm_w, v_w_a_proj, v_s5_lambda_re, v_s5_lambda_im, v_s5_b_re, v_s5_b_im, v_s5_c_re, v_s5_c_im, v_s5_d, v_s5_log_dt, v_w_b_glu, v_w_out, v_norm_ffn2, v_w_ffn2_in, v_w_ffn2_out, v_norm_final):
    W = dict(w_ada=w_ada, b_ada=b_ada, norm_ffn1=norm_ffn1, w_ffn1_in=w_ffn1_in, w_ffn1_out=w_ffn1_out, norm_mix=norm_mix, w_in=w_in, conv_w=conv_w, conv_b=conv_b, dt_bias=dt_bias, a_log=a_log, d_ssd=d_ssd, ssd_norm_w=ssd_norm_w, w_a_proj=w_a_proj, s5_lambda_re=s5_lambda_re, s5_lambda_im=s5_lambda_im, s5_b_re=s5_b_re, s5_b_im=s5_b_im, s5_c_re=s5_c_re, s5_c_im=s5_c_im, s5_d=s5_d, s5_log_dt=s5_log_dt, w_b_glu=w_b_glu, w_out=w_out, norm_ffn2=norm_ffn2, w_ffn2_in=w_ffn2_in, w_ffn2_out=w_ffn2_out, norm_final=norm_final)
    Mo = dict(w_ada=m_w_ada, b_ada=m_b_ada, norm_ffn1=m_norm_ffn1, w_ffn1_in=m_w_ffn1_in, w_ffn1_out=m_w_ffn1_out, norm_mix=m_norm_mix, w_in=m_w_in, conv_w=m_conv_w, conv_b=m_conv_b, dt_bias=m_dt_bias, a_log=m_a_log, d_ssd=m_d_ssd, ssd_norm_w=m_ssd_norm_w, w_a_proj=m_w_a_proj, s5_lambda_re=m_s5_lambda_re, s5_lambda_im=m_s5_lambda_im, s5_b_re=m_s5_b_re, s5_b_im=m_s5_b_im, s5_c_re=m_s5_c_re, s5_c_im=m_s5_c_im, s5_d=m_s5_d, s5_log_dt=m_s5_log_dt, w_b_glu=m_w_b_glu, w_out=m_w_out, norm_ffn2=m_norm_ffn2, w_ffn2_in=m_w_ffn2_in, w_ffn2_out=m_w_ffn2_out, norm_final=m_norm_final)
    Vo = dict(w_ada=v_w_ada, b_ada=v_b_ada, norm_ffn1=v_norm_ffn1, w_ffn1_in=v_w_ffn1_in, w_ffn1_out=v_w_ffn1_out, norm_mix=v_norm_mix, w_in=v_w_in, conv_w=v_conv_w, conv_b=v_conv_b, dt_bias=v_dt_bias, a_log=v_a_log, d_ssd=v_d_ssd, ssd_norm_w=v_ssd_norm_w, w_a_proj=v_w_a_proj, s5_lambda_re=v_s5_lambda_re, s5_lambda_im=v_s5_lambda_im, s5_b_re=v_s5_b_re, s5_b_im=v_s5_b_im, s5_c_re=v_s5_c_re, s5_c_im=v_s5_c_im, s5_d=v_s5_d, s5_log_dt=v_s5_log_dt, w_b_glu=v_w_b_glu, w_out=v_w_out, norm_ffn2=v_norm_ffn2, w_ffn2_in=v_w_ffn2_in, w_ffn2_out=v_w_ffn2_out, norm_final=v_norm_final)

    t, d = x.shape[1], x.shape[2]
    ff = 4 * w_ffn1_out.shape[1]
    ffp = _round_up(ff, 512)
    lay = _Layout(d)
    xi, yi, ci = lax.axis_index("x"), lax.axis_index("y"), lax.axis_index("c")
    k_me = 2 * xi + yi
    e_me = 4 * xi + 2 * yi + ci
    x2d = x[0]
    tgt = loss_target[0]

    cw_cols = conv_w.shape[2]
    g1 = _gather8(jnp.concatenate([c[0], conv_w[0].reshape(-1)]), "gather_c_convw")
    c_all = g1[:, :d]
    conv_full = g1[::2, d:].reshape(4, CONV_K, cw_cols).transpose(1, 0, 2).reshape(CONV_K, CONV_DIM)
    conv_w8 = jnp.zeros((SUBLANE, CONV_DIM), f32).at[:CONV_K].set(conv_full)

    n_ada_loc = w_ada.shape[2]
    b_loc = lax.dynamic_slice(b_ada, (0, k_me * n_ada_loc), (1, n_ada_loc))
    mods_part = _ada_fwd(c_all, w_ada[0], b_loc, "ada_fwd")
    g2 = _gather8(mods_part.reshape(-1), "gather_mods").reshape(8, 8, n_ada_loc)
    mods = lax.dynamic_index_in_dim(g2[::2], e_me, axis=1, keepdims=False).reshape(N_ADA, d)
    sh1, sc1, gt1, sh2, sc2, gt2, sh3, sc3, gt3 = [mods[i:i + 1] for i in range(N_ADA)]

    cast = {n: _cast_bf16(W[n][0], "cast_" + n) for n in BIG}

    def gather_of(names):
        return _Ride([cast[n] for n in names], False)

    def unshard(n, g):
        return _unshard_cols(g) if n in COL_SHARDED else g.reshape(4 * g.shape[1], g.shape[2])

    def ffn_in(w):
        z = jnp.zeros((w.shape[0], ffp - ff), w.dtype)
        return jnp.concatenate([w[:, :ff], z, w[:, ff:], z], axis=1)

    def ffn_out(w):
        return jnp.concatenate([w, jnp.zeros((ffp - ff, w.shape[1]), w.dtype)], axis=0)

    nf1, nmx, nf2 = norm_ffn1, norm_mix, norm_ffn2
    nfin = norm_final.reshape(1, d)

    (g_w1i,) = _xchg([cast['w_ffn1_in']], False, "gather_w_ffn1_in")
    w1i = ffn_in(unshard('w_ffn1_in', g_w1i))
    (h1,) = _rw(_f_mod, [x2d], [nf1, sh1, sc1], [(d, bf16)], name="mod1")
    ab1, (g_w1o, g_wa) = _mm(h1, w1i, 'nn', out_dtype=bf16, name="ffn1_in",
                             ride=gather_of(['w_ffn1_out', 'w_a_proj']))
    w1o = ffn_out(unshard('w_ffn1_out', g_w1o))
    w_a = unshard('w_a_proj', g_wa)
    (act1,) = _rw(_f_swiglu, [ab1], [], [(ffp, bf16)], name="ffn1_act")
    f1, (g_win,) = _mm(act1, w1o, 'nn', out_dtype=bf16, name="ffn1_out", ride=gather_of(['w_in']))
    w_inr = lay.arrange(unshard('w_in', g_win))
    res1 = functools.partial(_f_res_mod, 0.5)
    x1, h2 = _rw(res1, [x2d, f1], [gt1, nmx, sh2, sc2], [(d, f32), (d, bf16)], name="res1_mod2")
    proj, (g_wglu, g_wo, g_w2o) = _mm(h2, w_inr, 'nn', out_dtype=f32, name="in_proj",
                                      ride=gather_of(['w_b_glu', 'w_out', 'w_ffn2_out']))
    w_glu, w_o = unshard('w_b_glu', g_wglu), unshard('w_out', g_wo)
    w2o = ffn_out(unshard('w_ffn2_out', g_w2o))

    pre = _conv_fwd(proj, lay.xbc, conv_w8, conv_b, "conv_fwd")
    sel = _sel_matrix()
    bias128 = jnp.zeros((1, LANE), f32).at[:, :SSD_HEADS].set(dt_bias)
    xs, bm, cm, dt4 = _rw(_f_ssd_pre, [pre, (proj, lay.dt, LANE)], [bias128, sel],
                          [(SSD_DI, f32), (SSD_G * SSD_N, f32), (SSD_G * SSD_N, f32), (SSD_G * LANE, f32)],
                          name="ssd_pre")

    def head_params(a_log_, d_ssd_):
        return _heads_to_lanes(-jnp.exp(a_log_[0])), _heads_to_lanes(d_ssd_[0])

    (a4, dsk4), head_vjp = jax.vjp(head_params, a_log, d_ssd)
    y_ssd, hs, (g_w2i,) = _ssd_fwd(xs, bm, cm, dt4, a4, dsk4, "ssd_fwd", ride=gather_of(['w_ffn2_in']))
    w2i = ffn_in(unshard('w_ffn2_in', g_w2i))
    (y_a,) = _rw(_f_ssd_post, [y_ssd, (proj, lay.z, SSD_DI)], [ssd_norm_w], [(SSD_DI, bf16)], name="ssd_post")
    p_a = _mm(y_a, w_a, 'nn', out_dtype=f32, name="a_proj")

    col = lambda v: v.reshape(S5_S, 1)
    ldt_col = jnp.repeat(s5_log_dt[0], S5_P).reshape(S5_S, 1)
    prep_rows = [col(s5_lambda_re[0]), col(s5_lambda_im[0]), ldt_col,
                 s5_b_re[0].reshape(S5_S, S5_I), s5_b_im[0].reshape(S5_S, S5_I)]
    ar, ai, bbr, bbi = _rw(_f_s5_prep, prep_rows, [], [(1, f32), (1, f32), (S5_I, f32), (S5_I, f32)],
                           name="s5_prep", tm=512)
    to_bd = lambda bb: _block_diag8(bb.reshape(S5_G, S5_P, S5_I).transpose(0, 2, 1).astype(bf16))
    bd_c = jnp.concatenate([to_bd(bbr), to_bd(bbi)], axis=1)
    c_c = jnp.concatenate([_block_diag8(s5_c_re[0].transpose(0, 2, 1).astype(bf16)),
                           _block_diag8((-s5_c_im[0]).transpose(0, 2, 1).astype(bf16))], axis=0)
    ar_row, ai_row = ar.reshape(1, S5_S), ai.reshape(1, S5_S)
    bu = _s5_bu(proj, lay.u, bd_c, "s5_bu")
    s5s = _s5_scan(bu, ar_row, ai_row, "s5_scan")
    yb = _s5_out(s5s, c_c, "s5_out")
    d_row = s5_d[0].reshape(1, S5_W)
    (gl,) = _rw(_f_s5_post, [yb, (proj, lay.u, S5_W)], [d_row], [(S5_W, bf16)], name="s5_post")
    glu = _mm(gl, w_glu, 'nn', out_dtype=f32, name="glu_proj")

    merge_rows = [p_a, (glu, 0, d), (glu, d, d), (proj, lay.ga, d), (proj, lay.gb, d)]
    (merged,) = _rw(_f_merge, merge_rows, [], [(d, bf16)], name="merge")
    o_mix = _mm(merged, w_o, 'nn', out_dtype=bf16, name="out_proj")
    res2 = functools.partial(_f_res_mod, 1.0)
    x2, h3 = _rw(res2, [x1, o_mix], [gt2, nf2, sh3, sc3], [(d, f32), (d, bf16)], name="res2_mod3")
    ab2 = _mm(h3, w2i, 'nn', out_dtype=bf16, name="ffn2_in")
    (act2,) = _rw(_f_swiglu, [ab2], [], [(ffp, bf16)], name="ffn2_act")
    f2 = _mm(act2, w2o, 'nn', out_dtype=bf16, name="ffn2_out")
    (loss_acc,) = _rw(_f_final_loss, [x2, f2, tgt], [gt3, nfin], [], accs=[(1, LANE)], name="loss")
    loss = lax.psum(loss_acc[0, 0], AXES)

    ones = jnp.ones((t, 1), f32)
    dx2, df2, dgt3, dnfin = _rw_vjp(_f_final, [x2, f2, tgt], [gt3, nfin], [ones],
                                    row_grads=[f32, bf16, None], param_grads=[True, True], name="loss_bwd")
    dact2 = _mm(df2, w2o, 'nt', out_dtype=bf16, name="ffn2_out_dx")
    dw2o = _mm(act2, df2, 'tn', out_dtype=f32, name="ffn2_out_dw")
    (dab2,) = _rw_vjp(_f_swiglu, [ab2], [], [dact2], row_grads=[bf16], param_grads=[], name="ffn2_act_bwd")
    dh3 = _mm(dab2, w2i, 'nt', out_dtype=bf16, name="ffn2_in_dx")
    dw2i = _mm(h3, dab2, 'tn', out_dtype=f32, name="ffn2_in_dw")
    dx1, do_mix, dgt2, dnf2, dsh3, dsc3 = _rw_vjp(
        res2, [x1, o_mix], [gt2, nf2, sh3, sc3], [dx2, dh3], row_grads=[f32, bf16], param_grads=[True] * 4,
        name="res2_mod3_bwd")
    dmerged = _mm(do_mix, w_o, 'nt', out_dtype=bf16, name="out_proj_dx")
    dw_o = _mm(merged, do_mix, 'tn', out_dtype=f32, name="out_proj_dw")
    dp_a, dglu_a, dglu_g, dga, dgb = _rw_vjp(_f_merge, merge_rows, [], [dmerged], row_grads=[bf16] * 5,
                                             param_grads=[], name="merge_bwd")
    dglu = jnp.concatenate([dglu_a, dglu_g], axis=1)

    dgl = _mm(dglu, w_glu, 'nt', out_dtype=bf16, name="glu_proj_dx")
    dw_glu = _mm(gl, dglu, 'tn', out_dtype=f32, name="glu_proj_dw")
    dyb, du_skip, dd_row = _rw_vjp(_f_s5_post, [yb, (proj, lay.u, S5_W)], [d_row], [dgl],
                                   row_grads=[bf16, f32], param_grads=[True], name="s5_post_bwd")
    ds5 = _s5_out_dx(dyb, c_c, "s5_out_dx")
    dc_c = _s5_out_dw(s5s, dyb, "s5_out_dw")
    g5, dar, dai = _s5_scan_bwd(ds5, s5s, ar_row, ai_row, "s5_scan_bwd")
    du = _s5_bu_dx(g5, bd_c, du_skip, "s5_bu_dx")
    dbd_c = _s5_bu_dw(proj, lay.u, g5, "s5_bu_dw")
    from_bd = lambda m_: _diag8(m_, S5_I, S5_P).transpose(0, 2, 1).reshape(S5_S, S5_I)
    dprep = _rw_vjp(_f_s5_prep, prep_rows, [], [dar.reshape(S5_S, 1), dai.reshape(S5_S, 1),
                                                from_bd(dbd_c[:, :S5_SB]), from_bd(dbd_c[:, S5_SB:])],
                    row_grads=[f32] * 5, param_grads=[], name="s5_prep_bwd", tm=512)
    dlr, dli, dldt, dbr, dbi = dprep
    g_s5 = dict(
        s5_lambda_re=dlr.reshape(S5_G, S5_P), s5_lambda_im=dli.reshape(S5_G, S5_P),
        s5_log_dt=dldt.reshape(S5_G, S5_P).sum(axis=1),
        s5_b_re=dbr.reshape(S5_G, S5_P, S5_I), s5_b_im=dbi.reshape(S5_G, S5_P, S5_I),
        s5_c_re=_diag8(dc_c[:S5_S], S5_P, S5_I).transpose(0, 2, 1),
        s5_c_im=-_diag8(dc_c[S5_S:], S5_P, S5_I).transpose(0, 2, 1),
        s5_d=dd_row.reshape(S5_G, S5_I))

    dy_a = _mm(dp_a, w_a, 'nt', out_dtype=bf16, name="a_proj_dx")
    dw_a = _mm(y_a, dp_a, 'tn', out_dtype=f32, name="a_proj_dw")
    dy_ssd, dz, dssd_nw = _rw_vjp(_f_ssd_post, [y_ssd, (proj, lay.z, SSD_DI)], [ssd_norm_w], [dy_a],
                                  row_grads=[f32, bf16], param_grads=[True], name="ssd_post_bwd")
    def ffn_in_back(g):
        return jnp.concatenate([g[:, :ff], g[:, ffp:ffp + ff]], axis=1)

    def scatter_of(pairs):
        slabs = []
        for n, g in pairs:
            g = _shard_cols(g) if n in COL_SHARDED else g.reshape(4, g.shape[0] // 4, g.shape[1])
            slabs.append(g.astype(bf16))
        return _Ride(slabs, True)

    terms = {}
    early = [('w_ffn2_out', dw2o[:ff]), ('w_ffn2_in', ffn_in_back(dw2i)), ('w_out', dw_o), ('w_b_glu', dw_glu),
             ('w_a_proj', dw_a)]
    (dxs, dbm, dcm, ddt4, da4, ddsk4), landed = _ssd_bwd(xs, bm, cm, dt4, a4, dsk4, hs, dy_ssd, "ssd_bwd",
                                                         ride=scatter_of(early))
    terms.update({n: p for (n, _), p in zip(early, landed)})
    da_log, dd_ssd = head_vjp((da4, ddsk4))
    dpre, ddt_raw, dbias128 = _rw_vjp(_f_ssd_pre, [pre, (proj, lay.dt, LANE)], [bias128, sel], [dxs, dbm, dcm, ddt4],
                                      row_grads=[f32, bf16], param_grads=[True, False], name="ssd_pre_bwd")
    dxbc, dconv_w8, dconv_b = _conv_bwd(dpre, proj, lay.xbc, conv_w8, "conv_bwd")

    dproj = jnp.concatenate([dz, dxbc, du, dga, dgb, ddt_raw], axis=1)
    dw_inr = _mm(h2, dproj, 'tn', out_dtype=f32, name="in_proj_dw")
    dh2, (terms['w_in'],) = _mm(dproj, w_inr, 'nt', out_dtype=bf16, name="in_proj_dx",
                                ride=scatter_of([('w_in', lay.restore(dw_inr))]))
    dx0, df1, dgt1, dnmx, dsh2, dsc2 = _rw_vjp(
        res1, [x2d, f1], [gt1, nmx, sh2, sc2], [dx1, dh2], row_grads=[f32, bf16], param_grads=[True] * 4,
        name="res1_mod2_bwd")
    dw1o = _mm(act1, df1, 'tn', out_dtype=f32, name="ffn1_out_dw")
    dact1, (terms['w_ffn1_out'],) = _mm(df1, w1o, 'nt', out_dtype=bf16, name="ffn1_out_dx",
                                        ride=scatter_of([('w_ffn1_out', dw1o[:ff])]))
    (dab1,) = _rw_vjp(_f_swiglu, [ab1], [], [dact1], row_grads=[bf16], param_grads=[], name="ffn1_act_bwd")
    dw1i = _mm(h1, dab1, 'tn', out_dtype=f32, name="ffn1_in_dw")
    dh1, (terms['w_ffn1_in'],) = _mm(dab1, w1i, 'nt', out_dtype=bf16, name="ffn1_in_dx",
                                     ride=scatter_of([('w_ffn1_in', ffn_in_back(dw1i))]))
    grad_x, dnf1, dsh1, dsc1 = _rw_vjp(_f_mod_keep, [x2d], [nf1, sh1, sc1], [dh1, dx0],
                                       row_grads=[f32], param_grads=[True] * 3, name="mod1_bwd")
    d_mods = jnp.concatenate([dsh1, dsc1, dgt1, dsh2, dsc2, dgt2, dsh3, dsc3, dgt3], axis=1).reshape(-1)

    sums = [_sum_lead(terms[n], "sum_" + n) for n in BIG]
    others = _swap_sibling(sums, "swap_sums")

    out_g, out_d, out_m, out_v = {}, {}, {}, {}
    for n, s_own, s_sib in zip(BIG, sums, others):
        r = _adamw(W[n][0], Mo[n][0], Vo[n][0], [s_own, s_sib], "adamw_" + n)
        out_g[n], out_d[n], out_m[n], out_v[n] = [o[None] for o in r]

    local = dict(
        b_ada=d_mods, norm_ffn1=dnf1, norm_mix=dnmx, conv_w=dconv_w8[:CONV_K], conv_b=dconv_b,
        dt_bias=dbias128[:, :SSD_HEADS], a_log=da_log, d_ssd=dd_ssd, ssd_norm_w=dssd_nw,
        norm_ffn2=dnf2, norm_final=dnfin, **g_s5)
    flat = jnp.concatenate([local[n].reshape(-1) for n in SMALL])
    g3 = _gather8(flat, "gather_small_grads")
    n_small = flat.shape[0]
    npad = _round_up(n_small, SUBLANE * LANE)
    g3p = jnp.zeros((8, npad), f32).at[:, :n_small].set(g3).reshape(8, npad // LANE, LANE)
    gsum = _sum_lead(g3p, "sum_small").reshape(-1)

    def local_shard(n, a):
        if n == 'conv_w':
            return lax.dynamic_slice(a.reshape(CONV_K, CONV_DIM), (0, k_me * cw_cols), (CONV_K, cw_cols))
        return a

    pieces, off = {}, 0
    for n in SMALL:
        sz = local[n].size
        pieces[n] = local_shard(n, gsum[off:off + sz]).reshape(W[n].shape)
        off += sz

    def pack(dct):
        v_ = jnp.concatenate([dct[n].reshape(-1) for n in SMALL])
        pad = _round_up(v_.shape[0], SUBLANE * LANE) - v_.shape[0]
        return jnp.concatenate([v_, jnp.ones((pad,), f32)]).reshape(-1, LANE)

    rs = _adamw(pack(W), pack(Mo), pack(Vo), [pack(pieces)], "adamw_small")
    off = 0
    for n in SMALL:
        sz = W[n].size
        out_g[n], out_d[n], out_m[n], out_v[n] = [o.reshape(-1)[off:off + sz].reshape(W[n].shape) for o in rs]
        off += sz

    dm_loc = lax.dynamic_slice(g3[:, :N_ADA * d], (0, k_me * n_ada_loc), (SUBLANE, n_ada_loc))
    g_ada = _ada_bwd(c_all, dm_loc, "ada_bwd")
    r = _adamw(w_ada[0], m_w_ada[0], v_w_ada[0], [g_ada], "adamw_w_ada")
    out_g['w_ada'], out_d['w_ada'], out_m['w_ada'], out_v['w_ada'] = [o[None] for o in r]

    return (loss, grad_x[None], *[out_g[n] for n in WEIGHTS], *[out_d[n] for n in WEIGHTS],
            *[out_m[n] for n in WEIGHTS], *[out_v[n] for n in WEIGHTS])
```

```python
import functools
import math

import numpy as np
import jax
import jax.numpy as jnp
from jax import lax
from jax.experimental import pallas as pl
from jax.experimental.pallas import tpu as pltpu

f32 = jnp.float32
bf16 = jnp.bfloat16
HI = lax.Precision.HIGHEST
MESH = pl.DeviceIdType.MESH
AXES = ("x", "y", "c")

EPS = 1e-6
SSD_HEADS, SSD_P, SSD_N, SSD_G, SSD_R, SSD_L = 32, 64, 128, 4, 8, 128
SSD_DI = SSD_HEADS * SSD_P
CONV_K = 4
CONV_DIM = SSD_DI + 2 * SSD_G * SSD_N
S5_W, S5_G, S5_I, S5_P = 1024, 64, 16, 64
S5_S = S5_G * S5_P
N_ADA = 9
ADAM_LR, ADAM_B1, ADAM_B2, ADAM_EPS, ADAM_WD, ADAM_STEP = 0.001, 0.9, 0.999, 1e-08, 0.01, 10

LANE = 128
SUBLANE = 8
VMEM_LIMIT = 56 << 20
MM_VMEM_BUDGET = 40 << 20
RW_VMEM_BUDGET = 36 << 20

WEIGHTS = ['w_ada', 'b_ada', 'norm_ffn1', 'w_ffn1_in', 'w_ffn1_out', 'norm_mix', 'w_in', 'conv_w', 'conv_b', 'dt_bias',
           'a_log', 'd_ssd', 'ssd_norm_w', 'w_a_proj', 's5_lambda_re', 's5_lambda_im', 's5_b_re', 's5_b_im', 's5_c_re',
           's5_c_im', 's5_d', 's5_log_dt', 'w_b_glu', 'w_out', 'norm_ffn2', 'w_ffn2_in', 'w_ffn2_out', 'norm_final']
BIG = ['w_ffn1_in', 'w_ffn1_out', 'w_in', 'w_a_proj', 'w_b_glu', 'w_out', 'w_ffn2_in', 'w_ffn2_out']
COL_SHARDED = ('w_ffn1_in', 'w_in', 'w_b_glu', 'w_ffn2_in')
SMALL = [n for n in WEIGHTS if n not in BIG and n != 'w_ada']


def _cp(sem=None):
    return pltpu.CompilerParams(dimension_semantics=sem, vmem_limit_bytes=VMEM_LIMIT)


def _tile(dim, target, align=LANE):
    if dim <= target:
        return dim
    t = (target // align) * align
    while t >= align:
        if dim % t == 0:
            return t
        t -= align
    return dim


def _round_up(n, m):
    return (n + m - 1) // m * m


def _mm(a, b, mode, *, out_dtype, name, a_win=None, b_win=None, add=None, ride=None):
    a0, aw = a_win or (0, a.shape[1])
    b0, bw = b_win or (0, b.shape[1])
    if mode == 'nn':
        m, k, n = a.shape[0], aw, bw
        assert b.shape[0] == k
    elif mode == 'nt':
        m, k, n = a.shape[0], aw, b.shape[0]
        assert bw == k
    else:
        k, m, n = a.shape[0], aw, bw
        assert b.shape[0] == k
    osz = jnp.dtype(out_dtype).itemsize
    tm, tn, tk = 1024, 1152, 2048
    while True:
        bm = _tile(math.gcd(m, a0) if (mode == 'tn' and a0) else m, tm)
        bn = _tile(math.gcd(n, b0) if (mode != 'nt' and b0) else n, tn)
        kk = k
        if mode != 'tn' and a0:
            kk = math.gcd(kk, a0)
        if mode == 'nt' and b0:
            kk = math.gcd(kk, b0)
        bk = _tile(kk, tk)
        need = 2 * (bm * bk * a.dtype.itemsize + bk * bn * b.dtype.itemsize + bm * bn * osz) + bm * bn * 4
        if add is not None:
            need += 2 * bm * bn * add.dtype.itemsize
        if need <= MM_VMEM_BUDGET or (tm <= 256 and tn <= 256 and tk <= 512):
            break
        if tk > 1024:
            tk //= 2
        elif tm >= tn:
            tm //= 2
        else:
            tn //= 2
    nk = k // bk
    assert m % bm == 0 and n % bn == 0 and k % bk == 0, (name, m, n, k, bm, bn, bk)
    if mode == 'nn':
        ao, bo = a0 // bk, b0 // bn
        a_blk, a_map = (bm, bk), lambda i, j, q: (i, q + ao)
        b_blk, b_map = (bk, bn), lambda i, j, q: (q, j + bo)
    elif mode == 'nt':
        ao, bo = a0 // bk, b0 // bk
        a_blk, a_map = (bm, bk), lambda i, j, q: (i, q + ao)
        b_blk, b_map = (bn, bk), lambda i, j, q: (j, q + bo)
    else:
        ao, bo = a0 // bm, b0 // bn
        a_blk, a_map = (bk, bm), lambda i, j, q: (q, i + ao)
        b_blk, b_map = (bk, bn), lambda i, j, q: (q, j + bo)
    return _mm_core(a, b, mode, grid=(m // bm, n // bn, nk), a_blk=a_blk, a_map=a_map, b_blk=b_blk, b_map=b_map,
                    o_blk=(bm, bn), o_map=lambda i, j, q: (i, j), out_shape=(m, n), out_dtype=out_dtype, name=name,
                    add=add, ride=ride)


def _mm_core(a, b, mode, *, grid, a_blk, a_map, b_blk, b_map, o_blk, o_map, out_shape, out_dtype, name,
             add=None, ride=None):
    dims = {'nn': (((1,), (0,)), ((), ())), 'nt': (((1,), (1,)), ((), ())), 'tn': (((0,), (0,)), ((), ()))}[mode]
    nk = grid[-1]
    has_add = add is not None
    nr = ride.n if ride is not None else 0

    def body(*refs):
        a_ref, b_ref = refs[0], refs[1]
        pos = 2
        add_ref = refs[pos] if has_add else None
        pos += int(has_add)
        r_ins = refs[pos:pos + nr]
        o_ref = refs[pos + nr]
        r_outs = refs[pos + nr + 1:pos + 2 * nr + 1]
        acc_ref = refs[pos + 2 * nr + 1]
        r_sems = refs[pos + 2 * nr + 2:]
        ids = [pl.program_id(ax) for ax in range(len(grid))]
        q = ids[-1]
        if nr:
            @pl.when(functools.reduce(lambda u, v: u & v, [i == 0 for i in ids]))
            def _():
                ride.start(r_ins, r_outs, r_sems)

        @pl.when(q == 0)
        def _():
            acc_ref[...] = jnp.zeros_like(acc_ref)

        acc_ref[...] += lax.dot_general(a_ref[...].astype(bf16), b_ref[...].astype(bf16), dims,
                                        preferred_element_type=f32)

        @pl.when(q == nk - 1)
        def _():
            r = acc_ref[...]
            if has_add:
                r = r + add_ref[...].astype(f32)
            o_ref[...] = r.astype(out_dtype)

        if nr:
            @pl.when(functools.reduce(lambda u, v: u & v, [i == g - 1 for i, g in zip(ids, grid)]))
            def _():
                ride.wait(r_ins, r_outs, r_sems)

    in_specs = [pl.BlockSpec(a_blk, a_map), pl.BlockSpec(b_blk, b_map)]
    ops = [a, b]
    if has_add:
        in_specs.append(pl.BlockSpec(o_blk, o_map))
        ops.append(add)
    out_specs = [pl.BlockSpec(o_blk, o_map)]
    out_shapes = [jax.ShapeDtypeStruct(out_shape, out_dtype)]
    scratch = [pltpu.VMEM(o_blk, f32)]
    if nr:
        in_specs += ride.specs
        ops += ride.srcs
        out_specs += ride.specs
        out_shapes += ride.out_shape
        scratch += ride.scratch
    sem = ("arbitrary",) * len(grid) if nr else ("parallel",) * (len(grid) - 1) + ("arbitrary",)
    res = pl.pallas_call(
        body, name=name, grid=grid, in_specs=in_specs, out_specs=out_specs, out_shape=out_shapes,
        scratch_shapes=scratch, compiler_params=_cp(sem),
    )(*ops)
    return (res[0], list(res[1:])) if nr else res[0]


S5_NB = 8
S5_UB = 128
S5_SB = 512


def _s5_bu(proj, u0, bd_c, name):
    t = proj.shape[0]
    bm = _tile(t, 1024)
    ub = u0 // S5_UB
    return _mm_core(proj, bd_c, 'nn', grid=(t // bm, 2 * S5_NB, 1),
                    a_blk=(bm, S5_UB), a_map=lambda i, j, q: (i, ub + j % S5_NB),
                    b_blk=(S5_UB, S5_SB), b_map=lambda i, j, q: (j % S5_NB, j // S5_NB),
                    o_blk=(bm, S5_SB), o_map=lambda i, j, q: (i, j),
                    out_shape=(t, 2 * S5_NB * S5_SB), out_dtype=f32, name=name)


def _s5_out(s, c_c, name):
    t = s.shape[0]
    bm = _tile(t, 1024)
    return _mm_core(s, c_c, 'nn', grid=(t // bm, S5_NB, 2),
                    a_blk=(bm, S5_SB), a_map=lambda i, j, q: (i, j + S5_NB * q),
                    b_blk=(S5_SB, S5_UB), b_map=lambda i, j, q: (j + S5_NB * q, 0),
                    o_blk=(bm, S5_UB), o_map=lambda i, j, q: (i, j),
                    out_shape=(t, S5_NB * S5_UB), out_dtype=f32, name=name)


def _s5_out_dx(dyb, c_c, name):
    t = dyb.shape[0]
    bm = _tile(t, 1024)
    return _mm_core(dyb, c_c, 'nt', grid=(t // bm, 2 * S5_NB, 1),
                    a_blk=(bm, S5_UB), a_map=lambda i, j, q: (i, j % S5_NB),
                    b_blk=(S5_SB, S5_UB), b_map=lambda i, j, q: (j, 0),
                    o_blk=(bm, S5_SB), o_map=lambda i, j, q: (i, j),
                    out_shape=(t, 2 * S5_NB * S5_SB), out_dtype=f32, name=name)


def _s5_out_dw(s, dyb, name):
    t = s.shape[0]
    bk = _tile(t, 2048)
    return _mm_core(s, dyb, 'tn', grid=(2 * S5_NB, t // bk),
                    a_blk=(bk, S5_SB), a_map=lambda j, q: (q, j),
                    b_blk=(bk, S5_UB), b_map=lambda j, q: (q, j % S5_NB),
                    o_blk=(S5_SB, S5_UB), o_map=lambda j, q: (j, 0),
                    out_shape=(2 * S5_NB * S5_SB, S5_UB), out_dtype=f32, name=name)


def _s5_bu_dx(g, bd_c, add, name):
    t = g.shape[0]
    bm = _tile(t, 1024)
    return _mm_core(g, bd_c, 'nt', grid=(t // bm, S5_NB, 2),
                    a_blk=(bm, S5_SB), a_map=lambda i, j, q: (i, j + S5_NB * q),
                    b_blk=(S5_UB, S5_SB), b_map=lambda i, j, q: (j, q),
                    o_blk=(bm, S5_UB), o_map=lambda i, j, q: (i, j),
                    out_shape=(t, S5_NB * S5_UB), out_dtype=bf16, name=name, add=add)


def _s5_bu_dw(proj, u0, g, name):
    t = proj.shape[0]
    bk = _tile(t, 2048)
    ub = u0 // S5_UB
    return _mm_core(proj, g, 'tn', grid=(S5_NB, 2, t // bk),
                    a_blk=(bk, S5_UB), a_map=lambda j, r, q: (q, ub + j),
                    b_blk=(bk, S5_SB), b_map=lambda j, r, q: (q, j + S5_NB * r),
                    o_blk=(S5_UB, S5_SB), o_map=lambda j, r, q: (j, r),
                    out_shape=(S5_NB * S5_UB, 2 * S5_SB), out_dtype=f32, name=name)


def _win(r):
    return r if isinstance(r, tuple) else (r, 0, r.shape[1])


def _row_tile(t, widths):
    per_row = 48 * max(widths)
    tm = 512
    while tm > SUBLANE and tm * per_row > RW_VMEM_BUDGET:
        tm //= 2
    return min(tm, t)


def _row_spec(r, tm):
    arr, c0, w = _win(r)
    assert c0 % w == 0, (c0, w)
    cb = c0 // w
    return pl.BlockSpec((tm, w), lambda i: (i, cb))


def _full_spec(p):
    nd = p.ndim
    return pl.BlockSpec(p.shape, lambda i: (0,) * nd)


def _rw(f, rows, params, outs, *, name, accs=(), tm=None):
    t = _win(rows[0])[0].shape[0]
    tm = tm or _row_tile(t, [_win(r)[2] for r in rows] + [w for w, _ in outs])
    nr, npar, no, na = len(rows), len(params), len(outs), len(accs)

    def body(*refs):
        vals = [r[...] for r in refs[:nr + npar]]
        res = f(*vals)
        res = res if isinstance(res, (tuple, list)) else (res,)
        for o_ref, v in zip(refs[nr + npar:nr + npar + no], res[:no]):
            o_ref[...] = v.astype(o_ref.dtype)
        if na:
            first = pl.program_id(0) == 0
            for a_ref, v in zip(refs[nr + npar + no:], res[no:]):
                @pl.when(first)
                def _(a_ref=a_ref):
                    a_ref[...] = jnp.zeros_like(a_ref)
                a_ref[...] += v

    out_shape = [jax.ShapeDtypeStruct((t, w), d) for w, d in outs] + [jax.ShapeDtypeStruct(s, f32) for s in accs]
    out_specs = [pl.BlockSpec((tm, w), lambda i: (i, 0)) for w, _ in outs] + \
                [pl.BlockSpec(s, lambda i: (0, 0)) for s in accs]
    return pl.pallas_call(
        body, name=name, grid=(t // tm,),
        in_specs=[_row_spec(r, tm) for r in rows] + [_full_spec(p) for p in params],
        out_specs=out_specs, out_shape=out_shape,
        compiler_params=_cp(("arbitrary",)),
    )(*[_win(r)[0] for r in rows], *params)


def _rw_vjp(f, rows, params, cots, *, row_grads, param_grads, name, tm=None):
    t = _win(rows[0])[0].shape[0]
    cot_rows = [c for c in cots if c is not None]
    tm = tm or _row_tile(t, [_win(r)[2] for r in rows] + [_win(c)[2] for c in cot_rows])
    nr, npar, ncot = len(rows), len(params), len(cot_rows)
    d_rows = [i for i, d in enumerate(row_grads) if d is not None]
    d_pars = [i for i, d in enumerate(param_grads) if d]

    def body(*refs):
        rv = [r[...] for r in refs[:nr]]
        pv = [r[...] for r in refs[nr:nr + npar]]
        cv = [r[...] for r in refs[nr + npar:nr + npar + ncot]]
        outs_r = refs[nr + npar + ncot:nr + npar + ncot + len(d_rows)]
        outs_p = refs[nr + npar + ncot + len(d_rows):]

        def g(*diff):
            rr, pp = list(rv), list(pv)
            for i, v in zip(d_rows, diff[:len(d_rows)]):
                rr[i] = v
            for i, v in zip(d_pars, diff[len(d_rows):]):
                pp[i] = v
            res = f(*rr, *pp)
            return tuple(res) if isinstance(res, (tuple, list)) else (res,)

        prim, vjp = jax.vjp(g, *[rv[i] for i in d_rows], *[pv[i] for i in d_pars])
        it = iter(cv)
        cts = tuple(next(it).astype(o.dtype) if c is not None else jnp.zeros_like(o) for o, c in zip(prim, cots))
        grads = vjp(cts)
        for o_ref, v in zip(outs_r, grads[:len(d_rows)]):
            o_ref[...] = v.astype(o_ref.dtype)
        first = pl.program_id(0) == 0
        for o_ref, v in zip(outs_p, grads[len(d_rows):]):
            @pl.when(first)
            def _(o_ref=o_ref):
                o_ref[...] = jnp.zeros_like(o_ref)
            o_ref[...] += v.astype(f32)

    out_shape = [jax.ShapeDtypeStruct((t, _win(rows[i])[2]), row_grads[i]) for i in d_rows] + \
                [jax.ShapeDtypeStruct(params[i].shape, f32) for i in d_pars]
    out_specs = [pl.BlockSpec((tm, _win(rows[i])[2]), lambda i_: (i_, 0)) for i in d_rows] + \
                [_full_spec(params[i]) for i in d_pars]
    return pl.pallas_call(
        body, name=name, grid=(t // tm,),
        in_specs=[_row_spec(r, tm) for r in rows] + [_full_spec(p) for p in params] + [_row_spec(c, tm) for c in cot_rows],
        out_specs=out_specs, out_shape=out_shape,
        compiler_params=_cp(("arbitrary",)),
    )(*[_win(r)[0] for r in rows], *params, *[_win(c)[0] for c in cot_rows])


def _rms(x, g):
    return x * lax.rsqrt(jnp.mean(x * x, axis=-1, keepdims=True) + EPS) * g


def _f_mod(x, nw, sh, sc):
    return (_rms(x, nw) * (1.0 + sc) + sh).astype(bf16)


def _f_mod_keep(x, nw, sh, sc):
    return _f_mod(x, nw, sh, sc), x


def _f_res_mod(coef, x, o, g, nw, sh, sc):
    x1 = x + coef * g * o.astype(f32)
    return x1, _f_mod(x1, nw, sh, sc)


def _f_swiglu(ab):
    h = ab.shape[1] // 2
    a = ab[:, :h].astype(f32)
    b = ab[:, h:].astype(f32)
    return (jax.nn.silu(a) * b).astype(bf16)


def _f_ssd_pre(pre, dtraw, bias, sel):
    xc = jax.nn.silu(pre)
    dt = jax.nn.softplus(dtraw + bias)
    dt4 = jnp.dot(dt, sel, precision=HI, preferred_element_type=f32)
    return xc[:, :SSD_DI], xc[:, SSD_DI:SSD_DI + SSD_G * SSD_N], xc[:, SSD_DI + SSD_G * SSD_N:], dt4


def _f_ssd_post(y, z, nw):
    yz = y * jax.nn.silu(z)
    w = SSD_DI // SSD_G
    parts = []
    for g in range(SSD_G):
        s = yz[:, g * w:(g + 1) * w]
        parts.append(s * lax.rsqrt(jnp.mean(s * s, axis=-1, keepdims=True) + EPS))
    return (jnp.concatenate(parts, axis=1) * nw).astype(bf16)


def _f_s5_post(yb, u, d):
    return jax.nn.gelu(yb + d * u).astype(bf16)


def _f_merge(pa, glu, ga, gb):
    d = pa.shape[1]
    pb = glu[:, :d] * jax.nn.sigmoid(glu[:, d:])
    return (jax.nn.sigmoid(ga) * pa + jax.nn.sigmoid(gb) * pb).astype(bf16)


def _f_final(x2, o, tgt, g, nw):
    x3 = x2 + 0.5 * g * o.astype(f32)
    y = _rms(x3, nw)
    return 0.5 * jnp.mean(jnp.square(y - tgt), axis=-1, keepdims=True)


def _f_final_loss(x2, o, tgt, g, nw):
    rows = _f_final(x2, o, tgt, g, nw)
    return jnp.broadcast_to(jnp.sum(rows, axis=0, keepdims=True), (1, LANE))


def _f_s5_prep(lr, li, ldt, br, bi):
    dt = jnp.exp(ldt)
    lr = jnp.minimum(lr, -1e-4)
    mag = jnp.exp(lr * dt)
    ar = mag * jnp.cos(li * dt)
    ai = mag * jnp.sin(li * dt)
    den = lr * lr + li * li
    nr = ar - 1.0
    kr = (nr * lr + ai * li) / den
    ki = (ai * lr - nr * li) / den
    return ar, ai, kr * br - ki * bi, kr * bi + ki * br


def _shift_down(cur, halo8, j):
    if j == 0:
        return cur
    rolled = pltpu.roll(cur, j, 0)
    row8 = lax.broadcasted_iota(jnp.int32, halo8.shape, 0)
    top = jnp.where(row8 < j, pltpu.roll(halo8, j, 0), rolled[:SUBLANE])
    return jnp.concatenate([top, rolled[SUBLANE:]], axis=0)


def _shift_up(cur, halo8, j):
    if j == 0:
        return cur
    n = cur.shape[0]
    rolled = pltpu.roll(cur, n - j, 0)
    row8 = lax.broadcasted_iota(jnp.int32, halo8.shape, 0)
    bot = jnp.where(row8 >= SUBLANE - j, pltpu.roll(halo8, SUBLANE - j, 0), rolled[n - SUBLANE:])
    return jnp.concatenate([rolled[:n - SUBLANE], bot], axis=0)


def _conv_fwd(proj, c0, w8, b, name):
    t = proj.shape[0]
    cw = 1024
    tm = min(512, t)
    cb0 = c0 // cw
    r8 = tm // SUBLANE

    def body(x_ref, h_ref, w_ref, b_ref, o_ref):
        i = pl.program_id(1)
        x = x_ref[...]
        halo = jnp.where(i > 0, h_ref[...], 0.0)
        acc = b_ref[...] + w_ref[CONV_K - 1:CONV_K, :] * x
        for j in range(1, CONV_K):
            acc = acc + w_ref[CONV_K - 1 - j:CONV_K - j, :] * _shift_down(x, halo, j)
        o_ref[...] = acc

    return pl.pallas_call(
        body, name=name, grid=(CONV_DIM // cw, t // tm),
        in_specs=[pl.BlockSpec((tm, cw), lambda c, i: (i, cb0 + c)),
                  pl.BlockSpec((SUBLANE, cw), lambda c, i: (jnp.maximum(i * r8 - 1, 0), cb0 + c)),
                  pl.BlockSpec((SUBLANE, cw), lambda c, i: (0, c)),
                  pl.BlockSpec((1, cw), lambda c, i: (0, c))],
        out_specs=pl.BlockSpec((tm, cw), lambda c, i: (i, c)),
        out_shape=jax.ShapeDtypeStruct((t, CONV_DIM), f32),
        compiler_params=_cp(("parallel", "arbitrary")),
    )(proj, proj, w8, b)


def _conv_bwd(dpre, proj, c0, w8, name):
    t = proj.shape[0]
    cw = 1024
    tm = min(512, t)
    cb0 = c0 // cw
    r8 = tm // SUBLANE
    nb = t // tm

    def body(d_ref, dn_ref, x_ref, xh_ref, w_ref, dx_ref, dw_ref, db_ref):
        i = pl.program_id(1)
        d = d_ref[...]
        dn = jnp.where(i < nb - 1, dn_ref[...], 0.0)
        x = x_ref[...]
        xh = jnp.where(i > 0, xh_ref[...], 0.0)

        @pl.when(i == 0)
        def _():
            dw_ref[...] = jnp.zeros_like(dw_ref)
            db_ref[...] = jnp.zeros_like(db_ref)

        dx = w_ref[CONV_K - 1:CONV_K, :] * d
        rows = [jnp.sum(d * x, axis=0, keepdims=True)]
        for j in range(1, CONV_K):
            dx = dx + w_ref[CONV_K - 1 - j:CONV_K - j, :] * _shift_up(d, dn, j)
            rows.append(jnp.sum(d * _shift_down(x, xh, j), axis=0, keepdims=True))
        dx_ref[...] = dx.astype(dx_ref.dtype)
        dw = jnp.concatenate([rows[CONV_K - 1 - k] for k in range(CONV_K)] +
                             [jnp.zeros((SUBLANE - CONV_K, cw), f32)], axis=0)
        dw_ref[...] += dw
        db_ref[...] += jnp.sum(d, axis=0, keepdims=True)

    return pl.pallas_call(
        body, name=name, grid=(CONV_DIM // cw, nb),
        in_specs=[pl.BlockSpec((tm, cw), lambda c, i: (i, c)),
                  pl.BlockSpec((SUBLANE, cw), lambda c, i: (jnp.minimum((i + 1) * r8, nb * r8 - 1), c)),
                  pl.BlockSpec((tm, cw), lambda c, i: (i, cb0 + c)),
                  pl.BlockSpec((SUBLANE, cw), lambda c, i: (jnp.maximum(i * r8 - 1, 0), cb0 + c)),
                  pl.BlockSpec((SUBLANE, cw), lambda c, i: (0, c))],
        out_specs=[pl.BlockSpec((tm, cw), lambda c, i: (i, c)),
                   pl.BlockSpec((SUBLANE, cw), lambda c, i: (0, c)),
                   pl.BlockSpec((1, cw), lambda c, i: (0, c))],
        out_shape=[jax.ShapeDtypeStruct((t, CONV_DIM), bf16), jax.ShapeDtypeStruct((SUBLANE, CONV_DIM), f32),
                   jax.ShapeDtypeStruct((1, CONV_DIM), f32)],
        compiler_params=_cp(("parallel", "arbitrary")),
    )(dpre, dpre, proj, proj, w8)


def _ssd_chunk(xs, bm, cm, dt, a, dsk, h):
    n = SSD_L
    row = lax.broadcasted_iota(jnp.int32, (n, n), 0)
    col = lax.broadcasted_iota(jnp.int32, (n, n), 1)
    causal = row >= col
    cs = jnp.dot(causal.astype(f32), dt * a, precision=HI, preferred_element_type=f32)
    cs_t = cs.T
    nt = (((1,), (1,)), ((), ()))
    cb = lax.dot_general(cm.astype(bf16), bm.astype(bf16), nt, preferred_element_type=f32)
    ys, hn = [], []
    for r in range(SSD_R):
        xr = xs[:, r * SSD_P:(r + 1) * SSD_P]
        xdt = xr * dt[:, r:r + 1]
        c_col = cs[:, r:r + 1]
        decay = jnp.exp(jnp.where(causal, c_col - cs_t[r:r + 1, :], -1e30))
        y_diag = jnp.dot((cb * decay).astype(bf16), xdt.astype(bf16), preferred_element_type=f32)
        hr = h[r * SSD_P:(r + 1) * SSD_P, :]
        y_off = lax.dot_general(cm.astype(bf16), hr.astype(bf16), nt, preferred_element_type=f32) * jnp.exp(c_col)
        last = cs[n - 1:n, r:r + 1]
        st = lax.dot_general((xdt * jnp.exp(last - c_col)).astype(bf16), bm.astype(bf16), (((0,), (0,)), ((), ())),
                             preferred_element_type=f32)
        hn.append(jnp.exp(last) * hr + st)
        ys.append(y_diag + y_off + dsk[:, r:r + 1] * xr)
    return jnp.concatenate(ys, axis=1), jnp.concatenate(hn, axis=0)


SSD_GB = 1


def _ssd_specs(nc, rev):
    ch = (lambda c: nc - 1 - c) if rev else (lambda c: c)
    gw = SSD_GB * SSD_R * SSD_P
    return [pl.BlockSpec((SSD_L, gw), lambda g, c: (ch(c), g)),
            pl.BlockSpec((SSD_L, SSD_GB * SSD_N), lambda g, c: (ch(c), g)),
            pl.BlockSpec((SSD_L, SSD_GB * SSD_N), lambda g, c: (ch(c), g)),
            pl.BlockSpec((SSD_L, SSD_GB * LANE), lambda g, c: (ch(c), g)),
            pl.BlockSpec((1, SSD_GB * LANE), lambda g, c: (0, g)),
            pl.BlockSpec((1, SSD_GB * LANE), lambda g, c: (0, g))]


def _ssd_group(refs, q):
    gw = SSD_R * SSD_P
    xs_ref, bm_ref, cm_ref, dt_ref, a_ref, dsk_ref = refs
    ln = slice(q * LANE, (q + 1) * LANE)
    return (xs_ref[:, q * gw:(q + 1) * gw], bm_ref[:, ln], cm_ref[:, ln], dt_ref[:, ln], a_ref[:, ln], dsk_ref[:, ln])


def _ssd_fwd(xs, bm, cm, dt4, a4, dsk4, name, ride=None):
    t = xs.shape[0]
    nc = t // SSD_L
    gw = SSD_R * SSD_P

    nr = ride.n if ride is not None else 0
    ng = SSD_G // SSD_GB

    def body(*refs):
        xs_ref, bm_ref, cm_ref, dt_ref, a_ref, dsk_ref = refs[:6]
        r_ins = refs[6:6 + nr]
        y_ref, hs_ref = refs[6 + nr:8 + nr]
        r_outs = refs[8 + nr:8 + 2 * nr]
        h_ref = refs[8 + 2 * nr]
        r_sems = refs[9 + 2 * nr:]
        g, c = pl.program_id(0), pl.program_id(1)
        if nr:
            @pl.when((g == 0) & (c == 0))
            def _():
                ride.start(r_ins, r_outs, r_sems)

        @pl.when(c == 0)
        def _():
            h_ref[...] = jnp.zeros_like(h_ref)

        hs_ref[...] = h_ref[...]
        grp = (xs_ref, bm_ref, cm_ref, dt_ref, a_ref, dsk_ref)
        ops = [_ssd_group(grp, q) + (h_ref[q * gw:(q + 1) * gw, :],) for q in range(SSD_GB)]
        res = [_ssd_chunk(*o) for o in ops]
        for q, (y, hn) in enumerate(res):
            y_ref[:, q * gw:(q + 1) * gw] = y
            h_ref[q * gw:(q + 1) * gw, :] = hn

        if nr:
            @pl.when((g == ng - 1) & (c == nc - 1))
            def _():
                ride.wait(r_ins, r_outs, r_sems)

    res = pl.pallas_call(
        body, name=name, grid=(ng, nc), in_specs=_ssd_specs(nc, False) + (ride.specs if nr else []),
        out_specs=[pl.BlockSpec((SSD_L, SSD_GB * gw), lambda g, c: (c, g)),
                   pl.BlockSpec((None, None, SSD_GB * gw, SSD_N), lambda g, c: (g, c, 0, 0))] +
                  (ride.specs if nr else []),
        out_shape=[jax.ShapeDtypeStruct((t, SSD_DI), f32),
                   jax.ShapeDtypeStruct((ng, nc, SSD_GB * gw, SSD_N), f32)] + (ride.out_shape if nr else []),
        scratch_shapes=[pltpu.VMEM((SSD_GB * gw, SSD_N), f32)] + (ride.scratch if nr else []),
        compiler_params=_cp(("arbitrary", "arbitrary")),
    )(xs, bm, cm, dt4, a4, dsk4, *(ride.srcs if nr else []))
    return res[0], res[1], list(res[2:])


def _ssd_bwd(xs, bm, cm, dt4, a4, dsk4, hs, dy, name, ride=None):
    t = xs.shape[0]
    nc = t // SSD_L
    gw = SSD_R * SSD_P
    rc = lambda c: nc - 1 - c
    nr = ride.n if ride is not None else 0
    ng = SSD_G // SSD_GB

    def body(*refs):
        xs_ref, bm_ref, cm_ref, dt_ref, a_ref, dsk_ref, hs_ref, dy_ref = refs[:8]
        r_ins = refs[8:8 + nr]
        dxs_ref, dbm_ref, dcm_ref, ddt_ref, da_ref, ddsk_ref = refs[8 + nr:14 + nr]
        r_outs = refs[14 + nr:14 + 2 * nr]
        dh_ref = refs[14 + 2 * nr]
        r_sems = refs[15 + 2 * nr:]
        if nr:
            @pl.when((pl.program_id(0) == 0) & (pl.program_id(1) == 0))
            def _():
                ride.start(r_ins, r_outs, r_sems)

        @pl.when(pl.program_id(1) == 0)
        def _():
            dh_ref[...] = jnp.zeros_like(dh_ref)
            da_ref[...] = jnp.zeros_like(da_ref)
            ddsk_ref[...] = jnp.zeros_like(ddsk_ref)

        grp = (xs_ref, bm_ref, cm_ref, dt_ref, a_ref, dsk_ref)
        ops = [_ssd_group(grp, q) + (hs_ref[q * gw:(q + 1) * gw, :],) for q in range(SSD_GB)]
        cts = [(dy_ref[:, q * gw:(q + 1) * gw], dh_ref[q * gw:(q + 1) * gw, :]) for q in range(SSD_GB)]
        grads = [jax.vjp(_ssd_chunk, *o)[1](ct) for o, ct in zip(ops, cts)]
        for q, (dxs, dbm, dcm, ddt, da, ddsk, dh) in enumerate(grads):
            rows = slice(q * gw, (q + 1) * gw)
            ln = slice(q * LANE, (q + 1) * LANE)
            dxs_ref[:, rows] = dxs
            dbm_ref[:, ln] = dbm
            dcm_ref[:, ln] = dcm
            ddt_ref[:, ln] = ddt
            da_ref[:, ln] += da
            ddsk_ref[:, ln] += ddsk
            dh_ref[rows, :] = dh

        if nr:
            @pl.when((pl.program_id(0) == ng - 1) & (pl.program_id(1) == nc - 1))
            def _():
                ride.wait(r_ins, r_outs, r_sems)

    res = pl.pallas_call(
        body, name=name, grid=(ng, nc),
        in_specs=_ssd_specs(nc, True) + [
            pl.BlockSpec((None, None, SSD_GB * gw, SSD_N), lambda g, c: (g, rc(c), 0, 0)),
            pl.BlockSpec((SSD_L, SSD_GB * gw), lambda g, c: (rc(c), g))] + (ride.specs if nr else []),
        out_specs=[pl.BlockSpec((SSD_L, SSD_GB * gw), lambda g, c: (rc(c), g)),
                   pl.BlockSpec((SSD_L, SSD_GB * SSD_N), lambda g, c: (rc(c), g)),
                   pl.BlockSpec((SSD_L, SSD_GB * SSD_N), lambda g, c: (rc(c), g)),
                   pl.BlockSpec((SSD_L, SSD_GB * LANE), lambda g, c: (rc(c), g)),
                   pl.BlockSpec((1, SSD_GB * LANE), lambda g, c: (0, g)),
                   pl.BlockSpec((1, SSD_GB * LANE), lambda g, c: (0, g))] + (ride.specs if nr else []),
        out_shape=[jax.ShapeDtypeStruct((t, SSD_DI), f32), jax.ShapeDtypeStruct((t, SSD_G * SSD_N), f32),
                   jax.ShapeDtypeStruct((t, SSD_G * SSD_N), f32), jax.ShapeDtypeStruct((t, SSD_G * LANE), f32),
                   jax.ShapeDtypeStruct((1, SSD_G * LANE), f32), jax.ShapeDtypeStruct((1, SSD_G * LANE), f32)] +
                  (ride.out_shape if nr else []),
        scratch_shapes=[pltpu.VMEM((SSD_GB * gw, SSD_N), f32)] + (ride.scratch if nr else []),
        compiler_params=_cp(("arbitrary", "arbitrary")),
    )(xs, bm, cm, dt4, a4, dsk4, hs, dy, *(ride.srcs if nr else []))
    return list(res[:6]), list(res[6:])


S5_CH = 1024


def _s5_scan(bu, ar, ai, name):
    t = bu.shape[0]
    tb = min(128, t)

    def body(bu_ref, ar_ref, ai_ref, s_ref, carry):
        @pl.when(pl.program_id(0) == 0)
        def _():
            carry[...] = jnp.zeros_like(carry)

        for c0 in range(0, S5_S, S5_CH):
            re = pl.ds(c0, S5_CH)
            im = pl.ds(S5_S + c0, S5_CH)
            a_r = ar_ref[:, re]
            a_i = ai_ref[:, re]

            def step(k, st, re=re, im=im, a_r=a_r, a_i=a_i):
                sr, si = st
                row = pl.ds(k, 1)
                nr = a_r * sr - a_i * si + bu_ref[row, re]
                ni = a_r * si + a_i * sr + bu_ref[row, im]
                s_ref[row, re] = nr
                s_ref[row, im] = ni
                return nr, ni

            sr, si = lax.fori_loop(0, tb, step, (carry[:, re], carry[:, im]))
            carry[:, re] = sr
            carry[:, im] = si

    return pl.pallas_call(
        body, name=name, grid=(t // tb,),
        in_specs=[pl.BlockSpec((tb, 2 * S5_S), lambda i: (i, 0)),
                  pl.BlockSpec((1, S5_S), lambda i: (0, 0)), pl.BlockSpec((1, S5_S), lambda i: (0, 0))],
        out_specs=pl.BlockSpec((tb, 2 * S5_S), lambda i: (i, 0)),
        out_shape=jax.ShapeDtypeStruct((t, 2 * S5_S), f32),
        scratch_shapes=[pltpu.VMEM((1, 2 * S5_S), f32)],
        compiler_params=_cp(("arbitrary",)),
    )(bu, ar, ai)


def _s5_scan_bwd(ds, s, ar, ai, name):
    t = ds.shape[0]
    tb = min(128, t)
    nb = t // tb
    r8 = tb // SUBLANE
    rb = lambda i: nb - 1 - i

    def body(ds_ref, s_ref, sh_ref, ar_ref, ai_ref, g_ref, dar_ref, dai_ref, carry):
        i = pl.program_id(0)

        @pl.when(i == 0)
        def _():
            carry[...] = jnp.zeros_like(carry)
            dar_ref[...] = jnp.zeros_like(dar_ref)
            dai_ref[...] = jnp.zeros_like(dai_ref)

        has_prev = (i < nb - 1).astype(f32)
        for c0 in range(0, S5_S, S5_CH):
            re = pl.ds(c0, S5_CH)
            im = pl.ds(S5_S + c0, S5_CH)
            a_r = ar_ref[:, re]
            a_i = ai_ref[:, re]

            def upd(st, row, sp_r, sp_i, re=re, im=im, a_r=a_r, a_i=a_i):
                gr, gi, acr, aci = st
                ngr = ds_ref[row, re] + a_r * gr + a_i * gi
                ngi = ds_ref[row, im] + a_r * gi - a_i * gr
                g_ref[row, re] = ngr
                g_ref[row, im] = ngi
                return ngr, ngi, acr + ngr * sp_r + ngi * sp_i, aci + ngi * sp_r - ngr * sp_i

            def step(k, st, re=re, im=im, upd=upd):
                tt = tb - 1 - k
                prev = pl.ds(tt - 1, 1)
                return upd(st, pl.ds(tt, 1), s_ref[prev, re], s_ref[prev, im])

            zero = jnp.zeros((1, S5_CH), f32)
            st = lax.fori_loop(0, tb - 1, step, (carry[:, re], carry[:, im], zero, zero))
            last = pl.ds(SUBLANE - 1, 1)
            gr, gi, acr, aci = upd(st, pl.ds(0, 1), sh_ref[last, re] * has_prev, sh_ref[last, im] * has_prev)
            carry[:, re] = gr
            carry[:, im] = gi
            dar_ref[:, re] += acr
            dai_ref[:, re] += aci

    return pl.pallas_call(
        body, name=name, grid=(nb,),
        in_specs=[pl.BlockSpec((tb, 2 * S5_S), lambda i: (rb(i), 0)),
                  pl.BlockSpec((tb, 2 * S5_S), lambda i: (rb(i), 0)),
                  pl.BlockSpec((SUBLANE, 2 * S5_S), lambda i: (jnp.maximum(rb(i) * r8 - 1, 0), 0)),
                  pl.BlockSpec((1, S5_S), lambda i: (0, 0)), pl.BlockSpec((1, S5_S), lambda i: (0, 0))],
        out_specs=[pl.BlockSpec((tb, 2 * S5_S), lambda i: (rb(i), 0)),
                   pl.BlockSpec((1, S5_S), lambda i: (0, 0)), pl.BlockSpec((1, S5_S), lambda i: (0, 0))],
        out_shape=[jax.ShapeDtypeStruct((t, 2 * S5_S), f32), jax.ShapeDtypeStruct((1, S5_S), f32),
                   jax.ShapeDtypeStruct((1, S5_S), f32)],
        scratch_shapes=[pltpu.VMEM((1, 2 * S5_S), f32)],
        compiler_params=_cp(("arbitrary",)),
    )(ds, s, s, ar, ai)


def _ada_fwd(c_all, w, b, name):
    d, n = w.shape
    tn = _tile(n, 1536)

    def body(c_ref, w_ref, b_ref, o_ref):
        a = jax.nn.silu(c_ref[...]).astype(bf16)
        o_ref[...] = jnp.dot(a, w_ref[...].astype(bf16), preferred_element_type=f32) + b_ref[...]

    return pl.pallas_call(
        body, name=name, grid=(n // tn,),
        in_specs=[pl.BlockSpec(c_all.shape, lambda j: (0, 0)), pl.BlockSpec((d, tn), lambda j: (0, j)),
                  pl.BlockSpec((1, tn), lambda j: (0, j))],
        out_specs=pl.BlockSpec((c_all.shape[0], tn), lambda j: (0, j)),
        out_shape=jax.ShapeDtypeStruct((c_all.shape[0], n), f32),
        compiler_params=_cp(("parallel",)),
    )(c_all, w, b)


def _ada_bwd(c_all, dm, name):
    d = c_all.shape[1]
    n = dm.shape[1]
    tn = _tile(n, 1536)

    def body(c_ref, dm_ref, o_ref):
        a = jax.nn.silu(c_ref[...]).astype(bf16)
        o_ref[...] = lax.dot_general(a, dm_ref[...].astype(bf16), (((0,), (0,)), ((), ())), preferred_element_type=f32)

    return pl.pallas_call(
        body, name=name, grid=(n // tn,),
        in_specs=[pl.BlockSpec(c_all.shape, lambda j: (0, 0)), pl.BlockSpec((dm.shape[0], tn), lambda j: (0, j))],
        out_specs=pl.BlockSpec((d, tn), lambda j: (0, j)),
        out_shape=jax.ShapeDtypeStruct((d, n), f32),
        compiler_params=_cp(("parallel",)),
    )(c_all, dm)


def _blk_rows(r, c, nbuf, itemsize=4):
    tr = _tile(r, max(SUBLANE, (RW_VMEM_BUDGET // (2 * nbuf * c * itemsize)) // 16 * 16), 16)
    return tr if r % tr == 0 else r


def _cast_bf16(w, name):
    r, c = w.shape
    tr = _blk_rows(r, c, 2)

    def body(w_ref, o_ref):
        o_ref[...] = w_ref[...].astype(bf16)

    return pl.pallas_call(
        body, name=name, grid=(r // tr,), in_specs=[pl.BlockSpec((tr, c), lambda i: (i, 0))],
        out_specs=pl.BlockSpec((tr, c), lambda i: (i, 0)), out_shape=jax.ShapeDtypeStruct((r, c), bf16),
        compiler_params=_cp(("parallel",)),
    )(w)


def _sum_lead(parts, name):
    n, r, c = parts.shape
    tr = _blk_rows(r, c, n + 2)

    def body(p_ref, o_ref):
        acc = p_ref[0].astype(f32)
        for q in range(1, n):
            acc = acc + p_ref[q].astype(f32)
        o_ref[...] = acc

    return pl.pallas_call(
        body, name=name, grid=(r // tr,), in_specs=[pl.BlockSpec((n, tr, c), lambda i: (0, i, 0))],
        out_specs=pl.BlockSpec((tr, c), lambda i: (i, 0)), out_shape=jax.ShapeDtypeStruct((r, c), f32),
        compiler_params=_cp(("parallel",)),
    )(parts)


def _adamw(w, m, v, parts, name):
    r, c = w.shape
    npart = len(parts)
    tr = _blk_rows(r, c, 7 + npart)
    c1 = 1.0 - ADAM_B1 ** ADAM_STEP
    c2 = 1.0 - ADAM_B2 ** ADAM_STEP

    def body(*refs):
        w_ref, m_ref, v_ref = refs[:3]
        g_ref, d_ref, nm_ref, nv_ref = refs[3 + npart:]
        g = refs[3][...].astype(f32)
        for p in refs[4:3 + npart]:
            g = g + p[...].astype(f32)
        nm = ADAM_B1 * m_ref[...] + (1.0 - ADAM_B1) * g
        nv = ADAM_B2 * v_ref[...] + (1.0 - ADAM_B2) * jnp.square(g)
        g_ref[...] = g
        nm_ref[...] = nm
        nv_ref[...] = nv
        d_ref[...] = -ADAM_LR * ((nm / c1) / (jnp.sqrt(nv / c2) + ADAM_EPS) + ADAM_WD * w_ref[...])

    spec = pl.BlockSpec((tr, c), lambda i: (i, 0))
    return pl.pallas_call(
        body, name=name, grid=(r // tr,), in_specs=[spec] * (3 + npart), out_specs=[spec] * 4,
        out_shape=[jax.ShapeDtypeStruct((r, c), f32)] * 4, compiler_params=_cp(("parallel",)),
    )(w, m, v, *parts)


def _ag_small(x_shard, name):
    m_per, n = x_shard.shape

    def body(x_ref, out_ref, send_sems, recv_sems, local_sem):
        x, y, c = lax.axis_index("x"), lax.axis_index("y"), lax.axis_index("c")
        me, sibling = (x, y, c), (x, y, 1 - c)
        chips = [(1 - x, y), (x, 1 - y), (1 - x, 1 - y)]

        def rows(px, py, pc):
            return out_ref.at[pl.ds((4 * px + 2 * py + pc) * m_per, m_per), :]

        def copy(k, block, to, src=None):
            return pltpu.make_async_remote_copy(
                src_ref=rows(*block) if src is None else src, dst_ref=rows(*block),
                send_sem=send_sems.at[k], recv_sem=recv_sems.at[k], device_id=to, device_id_type=MESH)

        mine = pltpu.make_async_copy(x_ref, rows(*me), local_sem)
        mine.start()
        first = [copy(0, me, sibling, src=x_ref)]
        first += [copy(1 + j, me, (*chip, c), src=x_ref) for j, chip in enumerate(chips)]
        for cp in first:
            cp.start()
        passed = [copy(4 + j, (*chip, c), sibling) for j, chip in enumerate(chips)]
        for j, chip in enumerate(chips):
            copy(1 + j, (*chip, c), me).wait_recv()
            passed[j].start()
        copy(0, sibling, me).wait_recv()
        for j, chip in enumerate(chips):
            copy(4 + j, (*chip, 1 - c), me).wait_recv()
        for cp in first + passed:
            cp.wait_send()
        mine.wait()

    return pl.pallas_call(
        body, name=name, out_shape=jax.ShapeDtypeStruct((8 * m_per, n), x_shard.dtype),
        in_specs=[pl.BlockSpec(memory_space=pltpu.VMEM)], out_specs=pl.BlockSpec(memory_space=pltpu.VMEM),
        scratch_shapes=[pltpu.SemaphoreType.DMA((7,)), pltpu.SemaphoreType.DMA((7,)), pltpu.SemaphoreType.DMA],
        compiler_params=pltpu.CompilerParams(vmem_limit_bytes=VMEM_LIMIT),
    )(x_shard)


def _xchg(srcs, scatter, name):
    return _run_ride(_Ride(srcs, scatter), name)


def _run_ride(ride, name):
    n = ride.n

    def body(*refs):
        ride.start(refs[:n], refs[n:2 * n], refs[2 * n:])
        ride.wait(refs[:n], refs[n:2 * n], refs[2 * n:])

    return pl.pallas_call(
        body, name=name, out_shape=ride.out_shape, in_specs=ride.specs, out_specs=ride.specs,
        scratch_shapes=ride.scratch,
    )(*ride.srcs)


class _Ride:
    def __init__(self, srcs, scatter):
        self.srcs, self.scatter, self.n = list(srcs), scatter, len(srcs)
        n = self.n
        self.out_shape = [jax.ShapeDtypeStruct(s.shape if scatter else (4,) + s.shape, s.dtype) for s in srcs]
        self.specs = [pl.BlockSpec(memory_space=pl.ANY)] * n
        self.scratch = [pltpu.SemaphoreType.DMA((3 * n,)), pltpu.SemaphoreType.DMA((3 * n,)),
                        pltpu.SemaphoreType.DMA((n,))]

    def _copies(self, ins, outs, sems):
        send_sems, recv_sems, local_sems = sems
        x, y, c = lax.axis_index("x"), lax.axis_index("y"), lax.axis_index("c")
        my_k = 2 * x + y
        peers = [(1 - x, y), (x, 1 - y), (1 - x, 1 - y)]
        local, sends, recvs = [], [], []
        for a in range(self.n):
            own = ins[a].at[my_k] if self.scatter else ins[a]
            local.append(pltpu.make_async_copy(own, outs[a].at[my_k], local_sems.at[a]))
            for j, (px, py) in enumerate(peers):
                sems_j = dict(send_sem=send_sems.at[3 * a + j], recv_sem=recv_sems.at[3 * a + j],
                              device_id=(px, py, c), device_id_type=MESH)
                src = ins[a].at[2 * px + py] if self.scatter else ins[a]
                sends.append(pltpu.make_async_remote_copy(src_ref=src, dst_ref=outs[a].at[my_k], **sems_j))
                landed = outs[a].at[2 * px + py]
                recvs.append(pltpu.make_async_remote_copy(src_ref=landed, dst_ref=landed, **sems_j))
        return local, sends, recvs

    def start(self, ins, outs, sems):
        local, sends, _ = self._copies(ins, outs, sems)
        for cp in local + sends:
            cp.start()

    def wait(self, ins, outs, sems):
        local, sends, recvs = self._copies(ins, outs, sems)
        for cp in recvs:
            cp.wait_recv()
        for cp in sends:
            cp.wait_send()
        for cp in local:
            cp.wait()


class _RideGather:
    def __init__(self, srcs):
        self.srcs, self.n = list(srcs), len(srcs)
        n = self.n
        assert all(s.shape[0] % 32 == 0 for s in srcs)
        self.out_shape = [jax.ShapeDtypeStruct((4,) + s.shape, s.dtype) for s in srcs]
        self.specs = [pl.BlockSpec(memory_space=pl.ANY)] * n
        dma = pltpu.SemaphoreType.DMA
        self.scratch = [dma((3 * n,)), dma((3 * n,)), dma((3 * n,)), dma((3 * n,)), dma((n,))]

    def _copies(self, ins, outs, sems):
        send_sems, recv_sems, pass_send, pass_recv, local_sems = sems
        x, y, c = lax.axis_index("x"), lax.axis_index("y"), lax.axis_index("c")
        my_k = 2 * x + y
        peers = [(1 - x, y), (x, 1 - y), (1 - x, 1 - y)]
        local, sends, recvs, passes, pass_recvs = [], [], [], [], []
        for a in range(self.n):
            half = self.srcs[a].shape[0] // 2
            mine = pl.ds(pl.multiple_of(c * half, 16), half)
            other = pl.ds(pl.multiple_of((1 - c) * half, 16), half)
            local.append(pltpu.make_async_copy(ins[a], outs[a].at[my_k], local_sems.at[a]))
            for j, (px, py) in enumerate(peers):
                q = 3 * a + j
                over_ici = dict(send_sem=send_sems.at[q], recv_sem=recv_sems.at[q], device_id=(px, py, c),
                                device_id_type=MESH)
                to_sibling = dict(send_sem=pass_send.at[q], recv_sem=pass_recv.at[q], device_id=(x, y, 1 - c),
                                  device_id_type=MESH)
                sends.append(pltpu.make_async_remote_copy(src_ref=ins[a].at[mine], dst_ref=outs[a].at[my_k, mine],
                                                          **over_ici))
                landed = outs[a].at[2 * px + py, mine]
                recvs.append(pltpu.make_async_remote_copy(src_ref=landed, dst_ref=landed, **over_ici))
                passes.append(pltpu.make_async_remote_copy(src_ref=landed, dst_ref=landed, **to_sibling))
                from_sibling = outs[a].at[2 * px + py, other]
                pass_recvs.append(pltpu.make_async_remote_copy(src_ref=from_sibling, dst_ref=from_sibling, **to_sibling))
        return local, sends, recvs, passes, pass_recvs

    def start(self, ins, outs, sems):
        local, sends = self._copies(ins, outs, sems)[:2]
        for cp in local + sends:
            cp.start()

    def wait(self, ins, outs, sems):
        local, sends, recvs, passes, pass_recvs = self._copies(ins, outs, sems)
        for rc, ps in zip(recvs, passes):
            rc.wait_recv()
            ps.start()
        for cp in pass_recvs:
            cp.wait_recv()
        for cp in sends + passes:
            cp.wait_send()
        for cp in local:
            cp.wait()


def _swap_sibling(srcs, name):
    n = len(srcs)

    def body(*refs):
        ins, outs = refs[:n], refs[n:2 * n]
        send_sems, recv_sems = refs[2 * n:]
        sib = (lax.axis_index("x"), lax.axis_index("y"), 1 - lax.axis_index("c"))
        cps = [pltpu.make_async_remote_copy(src_ref=ins[a], dst_ref=outs[a], send_sem=send_sems.at[a],
                                            recv_sem=recv_sems.at[a], device_id=sib, device_id_type=MESH)
               for a in range(n)]
        for cp in cps:
            cp.start()
        for cp in cps:
            cp.wait_recv()
        for cp in cps:
            cp.wait_send()

    anyspec = pl.BlockSpec(memory_space=pl.ANY)
    return pl.pallas_call(
        body, name=name, out_shape=[jax.ShapeDtypeStruct(s.shape, s.dtype) for s in srcs],
        in_specs=[anyspec] * n, out_specs=[anyspec] * n,
        scratch_shapes=[pltpu.SemaphoreType.DMA((n,)), pltpu.SemaphoreType.DMA((n,))],
    )(*srcs)


def _gather8(vec, name):
    size = vec.shape[0]
    n = _round_up(size, SUBLANE * LANE)
    blk = jnp.concatenate([vec, jnp.zeros((n - size,), f32)]).reshape(SUBLANE, n // SUBLANE)
    out = _ag_small(blk, name)
    return out.reshape(8, n)[:, :size]


def _sel_matrix():
    sel = np.zeros((LANE, SSD_G * LANE), np.float32)
    for h in range(SSD_HEADS):
        sel[h, (h // SSD_R) * LANE + h % SSD_R] = 1.0
    return jnp.asarray(sel)


def _heads_to_lanes(v):
    z = jnp.zeros((SSD_G, LANE), f32).at[:, :SSD_R].set(v.reshape(SSD_G, SSD_R))
    return z.reshape(1, SSD_G * LANE)


def _lanes_to_heads(v):
    return v.reshape(SSD_G, LANE)[:, :SSD_R].reshape(SSD_HEADS)


def _block_diag8(blocks):
    g, r, c = blocks.shape
    b = blocks.reshape(g // S5_NB, S5_NB, r, c)
    eye = jnp.eye(S5_NB, dtype=bool)[None, :, None, :, None]
    return jnp.where(eye, b[:, :, :, None, :], jnp.zeros((), blocks.dtype)).reshape(g * r, S5_NB * c)


def _diag8(mat, r, c):
    g = mat.shape[0] // r
    m = mat.reshape(g // S5_NB, S5_NB, r, S5_NB, c)
    eye = jnp.eye(S5_NB, dtype=bool)[None, :, None, :, None]
    return jnp.where(eye, m, 0.0).sum(axis=3).reshape(g, r, c)


class _Layout:
    def __init__(self, d):
        self.d = d
        self.z, self.xbc, self.u = 0, SSD_DI, SSD_DI + CONV_DIM
        self.ga = self.u + S5_W
        self.gb = self.ga + d
        self.dt = self.gb + d
        self.np_ = self.dt + LANE
        self.in_cols = SSD_DI + CONV_DIM + SSD_HEADS + S5_W + 2 * d
        off_dt = SSD_DI + CONV_DIM
        off_u = off_dt + SSD_HEADS
        off_g = off_u + S5_W
        self.src = [(0, off_dt), (off_u, off_u + S5_W + 2 * d), (off_dt, off_u)]

    def arrange(self, w):
        (a0, a1), (b0, b1), (c0, c1) = self.src
        pad = jnp.zeros((w.shape[0], LANE - SSD_HEADS), w.dtype)
        return jnp.concatenate([w[:, a0:a1], w[:, b0:b1], w[:, c0:c1], pad], axis=1)

    def arrange_slabs(self, g):
        pieces = [p for lo, hi in self.src for p in _cols_from_slabs(g, lo, hi)]
        pieces.append(jnp.zeros((g.shape[1], LANE - SSD_HEADS), g.dtype))
        return jnp.concatenate(pieces, axis=1)

    def restore_slabs(self, w):
        (a0, a1), (b0, b1), (c0, c1) = self.src
        n_a, n_b = a1 - a0, b1 - b0
        segs = [(a0, a1, 0), (c0, c1, n_a + n_b), (b0, b1, n_a)]
        cs = self.in_cols // 4
        slabs = []
        for k in range(4):
            lo, hi = k * cs, (k + 1) * cs
            parts = [w[:, pos + max(lo, s0) - s0:pos + min(hi, s1) - s0] for s0, s1, pos in segs
                     if max(lo, s0) < min(hi, s1)]
            slabs.append(jnp.concatenate(parts, axis=1))
        return jnp.stack(slabs)


def _cols_from_slabs(g, start, stop):
    c = g.shape[2]
    return [g[k][:, max(start, k * c) - k * c:min(stop, (k + 1) * c) - k * c] for k in range(4)
            if max(start, k * c) < min(stop, (k + 1) * c)]


def _unshard_cols(g):
    return jnp.concatenate([g[k] for k in range(4)], axis=1)


def _shard_cols(w):
    r, c4 = w.shape
    return w.reshape(r, 4, c4 // 4).transpose(1, 0, 2)


def kernel(x, c, w_ada, b_ada, norm_ffn1, w_ffn1_in, w_ffn1_out, norm_mix, w_in, conv_w, conv_b, dt_bias, a_log, d_ssd, ssd_norm_w, w_a_proj, s5_lambda_re, s5_lambda_im, s5_b_re, s5_b_im, s5_c_re, s5_c_im, s5_d, s5_log_dt, w_b_glu, w_out, norm_ffn2, w_ffn2_in, w_ffn2_out, norm_final, loss_target, m_w_ada, m_b_ada, m_norm_ffn1, m_w_ffn1_in, m_w_ffn1_out, m_norm_mix, m_w_in, m_conv_w, m_conv_b, m_dt_bias, m_a_log, m_d_ssd, m_ssd_norm_w, m_w_a_proj, m_s5_lambda_re, m_s5_lambda_im, m_s5_b_re, m_s5_b_im, m_s5_c_re, m_s5_c_im, m_s5_d, m_s5_log_dt, m_w_b_glu, m_w_out, m_norm_ffn2, m_w_ffn2_in, m_w_ffn2_out, m_norm_final, v_w_ada, v_b_ada, v_norm_ffn1, v_w_ffn1_in, v_w_ffn1_out, v_norm_mix, v_w_in, v_conv_w, v_conv_b, v_dt_bias, v_a_log, v_d_ssd, v_ssd_norm_w, v_w_a_proj, v_s5_lambda_re, v_s5_lambda_im, v_s5_b_re, v_s5_b_im, v_s5_c_re, v_s5_c_im, v_s5_d, v_s5_log_dt, v_w_b_glu, v_w_out, v_norm_ffn2, v_w_ffn2_in, v_w_ffn2_out, v_norm_final):
    W = dict(w_ada=w_ada, b_ada=b_ada, norm_ffn1=norm_ffn1, w_ffn1_in=w_ffn1_in, w_ffn1_out=w_ffn1_out, norm_mix=norm_mix, w_in=w_in, conv_w=conv_w, conv_b=conv_b, dt_bias=dt_bias, a_log=a_log, d_ssd=d_ssd, ssd_norm_w=ssd_norm_w, w_a_proj=w_a_proj, s5_lambda_re=s5_lambda_re, s5_lambda_im=s5_lambda_im, s5_b_re=s5_b_re, s5_b_im=s5_b_im, s5_c_re=s5_c_re, s5_c_im=s5_c_im, s5_d=s5_d, s5_log_dt=s5_log_dt, w_b_glu=w_b_glu, w_out=w_out, norm_ffn2=norm_ffn2, w_ffn2_in=w_ffn2_in, w_ffn2_out=w_ffn2_out, norm_final=norm_final)
    Mo = dict(w_ada=m_w_ada, b_ada=m_b_ada, norm_ffn1=m_norm_ffn1, w_ffn1_in=m_w_ffn1_in, w_ffn1_out=m_w_ffn1_out, norm_mix=m_norm_mix, w_in=m_w_in, conv_w=m_conv_w, conv_b=m_conv_b, dt_bias=m_dt_bias, a_log=m_a_log, d_ssd=m_d_ssd, ssd_norm_w=m_ssd_norm_w, w_a_proj=m_w_a_proj, s5_lambda_re=m_s5_lambda_re, s5_lambda_im=m_s5_lambda_im, s5_b_re=m_s5_b_re, s5_b_im=m_s5_b_im, s5_c_re=m_s5_c_re, s5_c_im=m_s5_c_im, s5_d=m_s5_d, s5_log_dt=m_s5_log_dt, w_b_glu=m_w_b_glu, w_out=m_w_out, norm_ffn2=m_norm_ffn2, w_ffn2_in=m_w_ffn2_in, w_ffn2_out=m_w_ffn2_out, norm_final=m_norm_final)
    Vo = dict(w_ada=v_w_ada, b_ada=v_b_ada, norm_ffn1=v_norm_ffn1, w_ffn1_in=v_w_ffn1_in, w_ffn1_out=v_w_ffn1_out, norm_mix=v_norm_mix, w_in=v_w_in, conv_w=v_conv_w, conv_b=v_conv_b, dt_bias=v_dt_bias, a_log=v_a_log, d_ssd=v_d_ssd, ssd_norm_w=v_ssd_norm_w, w_a_proj=v_w_a_proj, s5_lambda_re=v_s5_lambda_re, s5_lambda_im=v_s5_lambda_im, s5_b_re=v_s5_b_re, s5_b_im=v_s5_b_im, s5_c_re=v_s5_c_re, s5_c_im=v_s5_c_im, s5_d=v_s5_d, s5_log_dt=v_s5_log_dt, w_b_glu=v_w_b_glu, w_out=v_w_out, norm_ffn2=v_norm_ffn2, w_ffn2_in=v_w_ffn2_in, w_ffn2_out=v_w_ffn2_out, norm_final=v_norm_final)

    t, d = x.shape[1], x.shape[2]
    ff = 4 * w_ffn1_out.shape[1]
    ffp = _round_up(ff, 512)
    lay = _Layout(d)
    xi, yi, ci = lax.axis_index("x"), lax.axis_index("y"), lax.axis_index("c")
    k_me = 2 * xi + yi
    e_me = 4 * xi + 2 * yi + ci
    x2d = x[0]
    tgt = loss_target[0]

    cw_cols = conv_w.shape[2]
    g1 = _gather8(jnp.concatenate([c[0], conv_w[0].reshape(-1)]), "gather_c_convw")
    c_all = g1[:, :d]
    conv_full = g1[::2, d:].reshape(4, CONV_K, cw_cols).transpose(1, 0, 2).reshape(CONV_K, CONV_DIM)
    conv_w8 = jnp.zeros((SUBLANE, CONV_DIM), f32).at[:CONV_K].set(conv_full)

    n_ada_loc = w_ada.shape[2]
    b_loc = lax.dynamic_slice(b_ada, (0, k_me * n_ada_loc), (1, n_ada_loc))
    mods_part = _ada_fwd(c_all, w_ada[0], b_loc, "ada_fwd")
    g2 = _gather8(mods_part.reshape(-1), "gather_mods").reshape(8, 8, n_ada_loc)
    mods = lax.dynamic_index_in_dim(g2[::2], e_me, axis=1, keepdims=False).reshape(N_ADA, d)
    sh1, sc1, gt1, sh2, sc2, gt2, sh3, sc3, gt3 = [mods[i:i + 1] for i in range(N_ADA)]

    cast = {n: _cast_bf16(W[n][0], "cast_" + n) for n in BIG}

    def gather_of(names):
        return _RideGather([cast[n] for n in names])

    def rows_of(g):
        return g.reshape(4 * g.shape[1], g.shape[2])

    def ffn_in(g):
        z = jnp.zeros((g.shape[1], ffp - ff), g.dtype)
        return jnp.concatenate([g[0], g[1], z, g[2], g[3], z], axis=1)

    def ffn_out(g):
        return jnp.concatenate([rows_of(g), jnp.zeros((ffp - ff, g.shape[2]), g.dtype)], axis=0)

    nf1, nmx, nf2 = norm_ffn1, norm_mix, norm_ffn2
    nfin = norm_final.reshape(1, d)

    (g_w1i,) = _run_ride(gather_of(['w_ffn1_in']), "gather_w_ffn1_in")
    w1i = ffn_in(g_w1i)
    (h1,) = _rw(_f_mod, [x2d], [nf1, sh1, sc1], [(d, bf16)], name="mod1")
    ab1, (g_w1o, g_wa, g_win) = _mm(h1, w1i, 'nn', out_dtype=bf16, name="ffn1_in",
                                    ride=gather_of(['w_ffn1_out', 'w_a_proj', 'w_in']))
    w1o = ffn_out(g_w1o)
    w_a = rows_of(g_wa)
    w_inr = lay.arrange_slabs(g_win)
    (act1,) = _rw(_f_swiglu, [ab1], [], [(ffp, bf16)], name="ffn1_act")
    f1, (g_wglu, g_wo) = _mm(act1, w1o, 'nn', out_dtype=bf16, name="ffn1_out", ride=gather_of(['w_b_glu', 'w_out']))
    w_glu, w_o = _unshard_cols(g_wglu), rows_of(g_wo)
    res1 = functools.partial(_f_res_mod, 0.5)
    x1, h2 = _rw(res1, [x2d, f1], [gt1, nmx, sh2, sc2], [(d, f32), (d, bf16)], name="res1_mod2")
    proj, (g_w2i,) = _mm(h2, w_inr, 'nn', out_dtype=f32, name="in_proj", ride=gather_of(['w_ffn2_in']))
    w2i = ffn_in(g_w2i)

    pre = _conv_fwd(proj, lay.xbc, conv_w8, conv_b, "conv_fwd")
    sel = _sel_matrix()
    bias128 = jnp.zeros((1, LANE), f32).at[:, :SSD_HEADS].set(dt_bias)
    xs, bm, cm, dt4 = _rw(_f_ssd_pre, [pre, (proj, lay.dt, LANE)], [bias128, sel],
                          [(SSD_DI, f32), (SSD_G * SSD_N, f32), (SSD_G * SSD_N, f32), (SSD_G * LANE, f32)],
                          name="ssd_pre")

    def head_params(a_log_, d_ssd_):
        return _heads_to_lanes(-jnp.exp(a_log_[0])), _heads_to_lanes(d_ssd_[0])

    (a4, dsk4), head_vjp = jax.vjp(head_params, a_log, d_ssd)
    y_ssd, hs, (g_w2o,) = _ssd_fwd(xs, bm, cm, dt4, a4, dsk4, "ssd_fwd", ride=gather_of(['w_ffn2_out']))
    w2o = ffn_out(g_w2o)
    (y_a,) = _rw(_f_ssd_post, [y_ssd, (proj, lay.z, SSD_DI)], [ssd_norm_w], [(SSD_DI, bf16)], name="ssd_post")
    p_a = _mm(y_a, w_a, 'nn', out_dtype=f32, name="a_proj")

    col = lambda v: v.reshape(S5_S, 1)
    ldt_col = jnp.repeat(s5_log_dt[0], S5_P).reshape(S5_S, 1)
    prep_rows = [col(s5_lambda_re[0]), col(s5_lambda_im[0]), ldt_col,
                 s5_b_re[0].reshape(S5_S, S5_I), s5_b_im[0].reshape(S5_S, S5_I)]
    ar, ai, bbr, bbi = _rw(_f_s5_prep, prep_rows, [], [(1, f32), (1, f32), (S5_I, f32), (S5_I, f32)],
                           name="s5_prep", tm=512)
    to_bd = lambda bb: _block_diag8(bb.reshape(S5_G, S5_P, S5_I).transpose(0, 2, 1).astype(bf16))
    bd_c = jnp.concatenate([to_bd(bbr), to_bd(bbi)], axis=1)
    c_c = jnp.concatenate([_block_diag8(s5_c_re[0].transpose(0, 2, 1).astype(bf16)),
                           _block_diag8((-s5_c_im[0]).transpose(0, 2, 1).astype(bf16))], axis=0)
    ar_row, ai_row = ar.reshape(1, S5_S), ai.reshape(1, S5_S)
    bu = _s5_bu(proj, lay.u, bd_c, "s5_bu")
    s5s = _s5_scan(bu, ar_row, ai_row, "s5_scan")
    yb = _s5_out(s5s, c_c, "s5_out")
    d_row = s5_d[0].reshape(1, S5_W)
    (gl,) = _rw(_f_s5_post, [yb, (proj, lay.u, S5_W)], [d_row], [(S5_W, bf16)], name="s5_post")
    glu = _mm(gl, w_glu, 'nn', out_dtype=f32, name="glu_proj")

    merge_rows = [p_a, glu, (proj, lay.ga, d), (proj, lay.gb, d)]
    (merged,) = _rw(_f_merge, merge_rows, [], [(d, bf16)], name="merge")
    o_mix = _mm(merged, w_o, 'nn', out_dtype=bf16, name="out_proj")
    res2 = functools.partial(_f_res_mod, 1.0)
    x2, h3 = _rw(res2, [x1, o_mix], [gt2, nf2, sh3, sc3], [(d, f32), (d, bf16)], name="res2_mod3")
    ab2 = _mm(h3, w2i, 'nn', out_dtype=bf16, name="ffn2_in")
    (act2,) = _rw(_f_swiglu, [ab2], [], [(ffp, bf16)], name="ffn2_act")
    f2 = _mm(act2, w2o, 'nn', out_dtype=bf16, name="ffn2_out")
    (loss_acc,) = _rw(_f_final_loss, [x2, f2, tgt], [gt3, nfin], [], accs=[(1, LANE)], name="loss")
    loss = lax.psum(loss_acc[0, 0], AXES)

    ones = jnp.ones((t, 1), f32)
    dx2, df2, dgt3, dnfin = _rw_vjp(_f_final, [x2, f2, tgt], [gt3, nfin], [ones],
                                    row_grads=[f32, bf16, None], param_grads=[True, True], name="loss_bwd")
    dact2 = _mm(df2, w2o, 'nt', out_dtype=bf16, name="ffn2_out_dx")
    dw2o = _mm(act2, df2, 'tn', out_dtype=bf16, name="ffn2_out_dw")
    (dab2,) = _rw_vjp(_f_swiglu, [ab2], [], [dact2], row_grads=[bf16], param_grads=[], name="ffn2_act_bwd")
    dh3 = _mm(dab2, w2i, 'nt', out_dtype=bf16, name="ffn2_in_dx")
    dw2i = _mm(h3, dab2, 'tn', out_dtype=bf16, name="ffn2_in_dw")
    dx1, do_mix, dgt2, dnf2, dsh3, dsc3 = _rw_vjp(
        res2, [x1, o_mix], [gt2, nf2, sh3, sc3], [dx2, dh3], row_grads=[f32, bf16], param_grads=[True] * 4,
        name="res2_mod3_bwd")
    dmerged = _mm(do_mix, w_o, 'nt', out_dtype=bf16, name="out_proj_dx")
    dw_o = _mm(merged, do_mix, 'tn', out_dtype=bf16, name="out_proj_dw")
    dp_a, dglu, dga, dgb = _rw_vjp(_f_merge, merge_rows, [], [dmerged], row_grads=[bf16] * 4,
                                   param_grads=[], name="merge_bwd")

    dgl = _mm(dglu, w_glu, 'nt', out_dtype=bf16, name="glu_proj_dx")
    dw_glu = _mm(gl, dglu, 'tn', out_dtype=bf16, name="glu_proj_dw")
    dyb, du_skip, dd_row = _rw_vjp(_f_s5_post, [yb, (proj, lay.u, S5_W)], [d_row], [dgl],
                                   row_grads=[bf16, f32], param_grads=[True], name="s5_post_bwd")
    ds5 = _s5_out_dx(dyb, c_c, "s5_out_dx")
    dc_c = _s5_out_dw(s5s, dyb, "s5_out_dw")
    g5, dar, dai = _s5_scan_bwd(ds5, s5s, ar_row, ai_row, "s5_scan_bwd")
    du = _s5_bu_dx(g5, bd_c, du_skip, "s5_bu_dx")
    dbd_c = _s5_bu_dw(proj, lay.u, g5, "s5_bu_dw")
    from_bd = lambda m_: _diag8(m_, S5_I, S5_P).transpose(0, 2, 1).reshape(S5_S, S5_I)
    dprep = _rw_vjp(_f_s5_prep, prep_rows, [], [dar.reshape(S5_S, 1), dai.reshape(S5_S, 1),
                                                from_bd(dbd_c[:, :S5_SB]), from_bd(dbd_c[:, S5_SB:])],
                    row_grads=[f32] * 5, param_grads=[], name="s5_prep_bwd", tm=512)
    dlr, dli, dldt, dbr, dbi = dprep
    g_s5 = dict(
        s5_lambda_re=dlr.reshape(S5_G, S5_P), s5_lambda_im=dli.reshape(S5_G, S5_P),
        s5_log_dt=dldt.reshape(S5_G, S5_P).sum(axis=1),
        s5_b_re=dbr.reshape(S5_G, S5_P, S5_I), s5_b_im=dbi.reshape(S5_G, S5_P, S5_I),
        s5_c_re=_diag8(dc_c[:S5_S], S5_P, S5_I).transpose(0, 2, 1),
        s5_c_im=-_diag8(dc_c[S5_S:], S5_P, S5_I).transpose(0, 2, 1),
        s5_d=dd_row.reshape(S5_G, S5_I))

    dy_a = _mm(dp_a, w_a, 'nt', out_dtype=bf16, name="a_proj_dx")
    dw_a = _mm(y_a, dp_a, 'tn', out_dtype=bf16, name="a_proj_dw")
    dy_ssd, dz, dssd_nw = _rw_vjp(_f_ssd_post, [y_ssd, (proj, lay.z, SSD_DI)], [ssd_norm_w], [dy_a],
                                  row_grads=[f32, bf16], param_grads=[True], name="ssd_post_bwd")
    def ffn_in_back(g):
        hf = ff // 2
        return jnp.stack([g[:, :hf], g[:, hf:ff], g[:, ffp:ffp + hf], g[:, ffp + hf:ffp + ff]])

    def rows_back(g, rows):
        return g[:rows].reshape(4, rows // 4, g.shape[1])

    def scatter_of(pairs):
        return _Ride([g for _, g in pairs], True)

    terms = {}
    early = [('w_ffn2_out', rows_back(dw2o, ff)), ('w_ffn2_in', ffn_in_back(dw2i)), ('w_out', rows_back(dw_o, d)),
             ('w_b_glu', _shard_cols(dw_glu)), ('w_a_proj', rows_back(dw_a, SSD_DI))]
    (dxs, dbm, dcm, ddt4, da4, ddsk4), landed = _ssd_bwd(xs, bm, cm, dt4, a4, dsk4, hs, dy_ssd, "ssd_bwd",
                                                         ride=scatter_of(early))
    terms.update({n: p for (n, _), p in zip(early, landed)})
    da_log, dd_ssd = head_vjp((da4, ddsk4))
    dpre, ddt_raw, dbias128 = _rw_vjp(_f_ssd_pre, [pre, (proj, lay.dt, LANE)], [bias128, sel], [dxs, dbm, dcm, ddt4],
                                      row_grads=[f32, bf16], param_grads=[True, False], name="ssd_pre_bwd")
    dxbc, dconv_w8, dconv_b = _conv_bwd(dpre, proj, lay.xbc, conv_w8, "conv_bwd")

    dproj = jnp.concatenate([dz, dxbc, du, dga, dgb, ddt_raw], axis=1)
    dw_inr = _mm(h2, dproj, 'tn', out_dtype=bf16, name="in_proj_dw")
    dh2, (terms['w_in'],) = _mm(dproj, w_inr, 'nt', out_dtype=bf16, name="in_proj_dx",
                                ride=scatter_of([('w_in', lay.restore_slabs(dw_inr))]))
    dx0, df1, dgt1, dnmx, dsh2, dsc2 = _rw_vjp(
        res1, [x2d, f1], [gt1, nmx, sh2, sc2], [dx1, dh2], row_grads=[f32, bf16], param_grads=[True] * 4,
        name="res1_mod2_bwd")
    dw1o = _mm(act1, df1, 'tn', out_dtype=bf16, name="ffn1_out_dw")
    dact1, (terms['w_ffn1_out'],) = _mm(df1, w1o, 'nt', out_dtype=bf16, name="ffn1_out_dx",
                                        ride=scatter_of([('w_ffn1_out', rows_back(dw1o, ff))]))
    (dab1,) = _rw_vjp(_f_swiglu, [ab1], [], [dact1], row_grads=[bf16], param_grads=[], name="ffn1_act_bwd")
    dw1i = _mm(h1, dab1, 'tn', out_dtype=bf16, name="ffn1_in_dw")
    dh1, (terms['w_ffn1_in'],) = _mm(dab1, w1i, 'nt', out_dtype=bf16, name="ffn1_in_dx",
                                     ride=scatter_of([('w_ffn1_in', ffn_in_back(dw1i))]))
    grad_x, dnf1, dsh1, dsc1 = _rw_vjp(_f_mod_keep, [x2d], [nf1, sh1, sc1], [dh1, dx0],
                                       row_grads=[f32], param_grads=[True] * 3, name="mod1_bwd")
    d_mods = jnp.concatenate([dsh1, dsc1, dgt1, dsh2, dsc2, dgt2, dsh3, dsc3, dgt3], axis=1).reshape(-1)

    sums = [_sum_lead(terms[n], "sum_" + n) for n in BIG]
    others = _swap_sibling(sums, "swap_sums")

    out_g, out_d, out_m, out_v = {}, {}, {}, {}
    for n, s_own, s_sib in zip(BIG, sums, others):
        r = _adamw(W[n][0], Mo[n][0], Vo[n][0], [s_own, s_sib], "adamw_" + n)
        out_g[n], out_d[n], out_m[n], out_v[n] = [o[None] for o in r]

    local = dict(
        b_ada=d_mods, norm_ffn1=dnf1, norm_mix=dnmx, conv_w=dconv_w8[:CONV_K], conv_b=dconv_b,
        dt_bias=dbias128[:, :SSD_HEADS], a_log=da_log, d_ssd=dd_ssd, ssd_norm_w=dssd_nw,
        norm_ffn2=dnf2, norm_final=dnfin, **g_s5)
    flat = jnp.concatenate([local[n].reshape(-1) for n in SMALL])
    g3 = _gather8(flat, "gather_small_grads")
    n_small = flat.shape[0]
    npad = _round_up(n_small, SUBLANE * LANE)
    g3p = jnp.zeros((8, npad), f32).at[:, :n_small].set(g3).reshape(8, npad // LANE, LANE)
    gsum = _sum_lead(g3p, "sum_small").reshape(-1)

    def local_shard(n, a):
        if n == 'conv_w':
            return lax.dynamic_slice(a.reshape(CONV_K, CONV_DIM), (0, k_me * cw_cols), (CONV_K, cw_cols))
        return a

    pieces, off = {}, 0
    for n in SMALL:
        sz = local[n].size
        pieces[n] = local_shard(n, gsum[off:off + sz]).reshape(W[n].shape)
        off += sz

    def pack(dct):
        v_ = jnp.concatenate([dct[n].reshape(-1) for n in SMALL])
        pad = _round_up(v_.shape[0], SUBLANE * LANE) - v_.shape[0]
        return jnp.concatenate([v_, jnp.ones((pad,), f32)]).reshape(-1, LANE)

    rs = _adamw(pack(W), pack(Mo), pack(Vo), [pack(pieces)], "adamw_small")
    off = 0
    for n in SMALL:
        sz = W[n].size
        out_g[n], out_d[n], out_m[n], out_v[n] = [o.reshape(-1)[off:off + sz].reshape(W[n].shape) for o in rs]
        off += sz

    dm_loc = lax.dynamic_slice(g3[:, :N_ADA * d], (0, k_me * n_ada_loc), (SUBLANE, n_ada_loc))
    g_ada = _ada_bwd(c_all, dm_loc, "ada_bwd")
    r = _adamw(w_ada[0], m_w_ada[0], v_w_ada[0], [g_ada], "adamw_w_ada")
    out_g['w_ada'], out_d['w_ada'], out_m['w_ada'], out_v['w_ada'] = [o[None] for o in r]

    return (loss, grad_x[None], *[out_g[n] for n in WEIGHTS], *[out_d[n] for n in WEIGHTS],
            *[out_m[n] for n in WEIGHTS], *[out_v[n] for n in WEIGHTS])
```

```python
import functools
import math

import numpy as np
import jax
import jax.numpy as jnp
from jax import lax
from jax.experimental import pallas as pl
from jax.experimental.pallas import tpu as pltpu

f32 = jnp.float32
bf16 = jnp.bfloat16
HI = lax.Precision.HIGHEST
MESH = pl.DeviceIdType.MESH
AXES = ("x", "y", "c")

EPS = 1e-6
SSD_HEADS, SSD_P, SSD_N, SSD_G, SSD_R, SSD_L = 32, 64, 128, 4, 8, 128
SSD_DI = SSD_HEADS * SSD_P
CONV_K = 4
CONV_DIM = SSD_DI + 2 * SSD_G * SSD_N
S5_W, S5_G, S5_I, S5_P = 1024, 64, 16, 64
S5_S = S5_G * S5_P
N_ADA = 9
ADAM_LR, ADAM_B1, ADAM_B2, ADAM_EPS, ADAM_WD, ADAM_STEP = 0.001, 0.9, 0.999, 1e-08, 0.01, 10

LANE = 128
SUBLANE = 8
VMEM_LIMIT = 56 << 20
MM_VMEM_BUDGET = 40 << 20
RW_VMEM_BUDGET = 36 << 20

WEIGHTS = ['w_ada', 'b_ada', 'norm_ffn1', 'w_ffn1_in', 'w_ffn1_out', 'norm_mix', 'w_in', 'conv_w', 'conv_b', 'dt_bias',
           'a_log', 'd_ssd', 'ssd_norm_w', 'w_a_proj', 's5_lambda_re', 's5_lambda_im', 's5_b_re', 's5_b_im', 's5_c_re',
           's5_c_im', 's5_d', 's5_log_dt', 'w_b_glu', 'w_out', 'norm_ffn2', 'w_ffn2_in', 'w_ffn2_out', 'norm_final']
BIG = ['w_ffn1_in', 'w_ffn1_out', 'w_in', 'w_a_proj', 'w_b_glu', 'w_out', 'w_ffn2_in', 'w_ffn2_out']
COL_SHARDED = ('w_ffn1_in', 'w_in', 'w_b_glu', 'w_ffn2_in')
SMALL = [n for n in WEIGHTS if n not in BIG and n != 'w_ada']


def _cp(sem=None):
    return pltpu.CompilerParams(dimension_semantics=sem, vmem_limit_bytes=VMEM_LIMIT)


def _tile(dim, target, align=LANE):
    if dim <= target:
        return dim
    t = (target // align) * align
    while t >= align:
        if dim % t == 0:
            return t
        t -= align
    return dim


def _round_up(n, m):
    return (n + m - 1) // m * m


def _mm(a, b, mode, *, out_dtype, name, a_win=None, b_win=None, add=None, ride=None):
    a0, aw = a_win or (0, a.shape[1])
    b0, bw = b_win or (0, b.shape[1])
    if mode == 'nn':
        m, k, n = a.shape[0], aw, bw
        assert b.shape[0] == k
    elif mode == 'nt':
        m, k, n = a.shape[0], aw, b.shape[0]
        assert bw == k
    else:
        k, m, n = a.shape[0], aw, bw
        assert b.shape[0] == k
    osz = jnp.dtype(out_dtype).itemsize
    tm, tn, tk = 1024, 1152, 2048
    while True:
        bm = _tile(math.gcd(m, a0) if (mode == 'tn' and a0) else m, tm)
        bn = _tile(math.gcd(n, b0) if (mode != 'nt' and b0) else n, tn)
        kk = k
        if mode != 'tn' and a0:
            kk = math.gcd(kk, a0)
        if mode == 'nt' and b0:
            kk = math.gcd(kk, b0)
        bk = _tile(kk, tk)
        need = 2 * (bm * bk * a.dtype.itemsize + bk * bn * b.dtype.itemsize + bm * bn * osz) + bm * bn * 4
        if add is not None:
            need += 2 * bm * bn * add.dtype.itemsize
        if need <= MM_VMEM_BUDGET or (tm <= 256 and tn <= 256 and tk <= 512):
            break
        if tk > 1024:
            tk //= 2
        elif tm >= tn:
            tm //= 2
        else:
            tn //= 2
    nk = k // bk
    assert m % bm == 0 and n % bn == 0 and k % bk == 0, (name, m, n, k, bm, bn, bk)
    if mode == 'nn':
        ao, bo = a0 // bk, b0 // bn
        a_blk, a_map = (bm, bk), lambda i, j, q: (i, q + ao)
        b_blk, b_map = (bk, bn), lambda i, j, q: (q, j + bo)
    elif mode == 'nt':
        ao, bo = a0 // bk, b0 // bk
        a_blk, a_map = (bm, bk), lambda i, j, q: (i, q + ao)
        b_blk, b_map = (bn, bk), lambda i, j, q: (j, q + bo)
    else:
        ao, bo = a0 // bm, b0 // bn
        a_blk, a_map = (bk, bm), lambda i, j, q: (q, i + ao)
        b_blk, b_map = (bk, bn), lambda i, j, q: (q, j + bo)
    return _mm_core(a, b, mode, grid=(m // bm, n // bn, nk), a_blk=a_blk, a_map=a_map, b_blk=b_blk, b_map=b_map,
                    o_blk=(bm, bn), o_map=lambda i, j, q: (i, j), out_shape=(m, n), out_dtype=out_dtype, name=name,
                    add=add, ride=ride)


def _mm_core(a, b, mode, *, grid, a_blk, a_map, b_blk, b_map, o_blk, o_map, out_shape, out_dtype, name,
             add=None, ride=None):
    dims = {'nn': (((1,), (0,)), ((), ())), 'nt': (((1,), (1,)), ((), ())), 'tn': (((0,), (0,)), ((), ()))}[mode]
    nk = grid[-1]
    has_add = add is not None
    nr = ride.n if ride is not None else 0

    def body(*refs):
        a_ref, b_ref = refs[0], refs[1]
        pos = 2
        add_ref = refs[pos] if has_add else None
        pos += int(has_add)
        r_ins = refs[pos:pos + nr]
        o_ref = refs[pos + nr]
        r_outs = refs[pos + nr + 1:pos + 2 * nr + 1]
        acc_ref = refs[pos + 2 * nr + 1]
        r_sems = refs[pos + 2 * nr + 2:]
        ids = [pl.program_id(ax) for ax in range(len(grid))]
        q = ids[-1]
        if nr:
            @pl.when(functools.reduce(lambda u, v: u & v, [i == 0 for i in ids]))
            def _():
                ride.start(r_ins, r_outs, r_sems)

        @pl.when(q == 0)
        def _():
            acc_ref[...] = jnp.zeros_like(acc_ref)

        acc_ref[...] += lax.dot_general(a_ref[...].astype(bf16), b_ref[...].astype(bf16), dims,
                                        preferred_element_type=f32)

        @pl.when(q == nk - 1)
        def _():
            r = acc_ref[...]
            if has_add:
                r = r + add_ref[...].astype(f32)
            o_ref[...] = r.astype(out_dtype)

        if nr:
            @pl.when(functools.reduce(lambda u, v: u & v, [i == g - 1 for i, g in zip(ids, grid)]))
            def _():
                ride.wait(r_ins, r_outs, r_sems)

    in_specs = [pl.BlockSpec(a_blk, a_map), pl.BlockSpec(b_blk, b_map)]
    ops = [a, b]
    if has_add:
        in_specs.append(pl.BlockSpec(o_blk, o_map))
        ops.append(add)
    out_specs = [pl.BlockSpec(o_blk, o_map)]
    out_shapes = [jax.ShapeDtypeStruct(out_shape, out_dtype)]
    scratch = [pltpu.VMEM(o_blk, f32)]
    if nr:
        in_specs += ride.specs
        ops += ride.srcs
        out_specs += ride.specs
        out_shapes += ride.out_shape
        scratch += ride.scratch
    sem = ("arbitrary",) * len(grid) if nr else ("parallel",) * (len(grid) - 1) + ("arbitrary",)
    res = pl.pallas_call(
        body, name=name, grid=grid, in_specs=in_specs, out_specs=out_specs, out_shape=out_shapes,
        scratch_shapes=scratch, compiler_params=_cp(sem),
    )(*ops)
    return (res[0], list(res[1:])) if nr else res[0]


S5_NB = 8
S5_UB = 128
S5_SB = 512


def _s5_bu(proj, u0, bd_c, name):
    t = proj.shape[0]
    bm = _tile(t, 1024)
    ub = u0 // S5_UB
    return _mm_core(proj, bd_c, 'nn', grid=(t // bm, 2 * S5_NB, 1),
                    a_blk=(bm, S5_UB), a_map=lambda i, j, q: (i, ub + j % S5_NB),
                    b_blk=(S5_UB, S5_SB), b_map=lambda i, j, q: (j % S5_NB, j // S5_NB),
                    o_blk=(bm, S5_SB), o_map=lambda i, j, q: (i, j),
                    out_shape=(t, 2 * S5_NB * S5_SB), out_dtype=f32, name=name)


def _s5_out(s, c_c, name):
    t = s.shape[0]
    bm = _tile(t, 1024)
    return _mm_core(s, c_c, 'nn', grid=(t // bm, S5_NB, 2),
                    a_blk=(bm, S5_SB), a_map=lambda i, j, q: (i, j + S5_NB * q),
                    b_blk=(S5_SB, S5_UB), b_map=lambda i, j, q: (j + S5_NB * q, 0),
                    o_blk=(bm, S5_UB), o_map=lambda i, j, q: (i, j),
                    out_shape=(t, S5_NB * S5_UB), out_dtype=f32, name=name)


def _s5_out_dx(dyb, c_c, name):
    t = dyb.shape[0]
    bm = _tile(t, 1024)
    return _mm_core(dyb, c_c, 'nt', grid=(t // bm, 2 * S5_NB, 1),
                    a_blk=(bm, S5_UB), a_map=lambda i, j, q: (i, j % S5_NB),
                    b_blk=(S5_SB, S5_UB), b_map=lambda i, j, q: (j, 0),
                    o_blk=(bm, S5_SB), o_map=lambda i, j, q: (i, j),
                    out_shape=(t, 2 * S5_NB * S5_SB), out_dtype=f32, name=name)


def _s5_out_dw(s, dyb, name):
    t = s.shape[0]
    bk = _tile(t, 2048)
    return _mm_core(s, dyb, 'tn', grid=(2 * S5_NB, t // bk),
                    a_blk=(bk, S5_SB), a_map=lambda j, q: (q, j),
                    b_blk=(bk, S5_UB), b_map=lambda j, q: (q, j % S5_NB),
                    o_blk=(S5_SB, S5_UB), o_map=lambda j, q: (j, 0),
                    out_shape=(2 * S5_NB * S5_SB, S5_UB), out_dtype=f32, name=name)


def _s5_bu_dx(g, bd_c, add, name):
    t = g.shape[0]
    bm = _tile(t, 1024)
    return _mm_core(g, bd_c, 'nt', grid=(t // bm, S5_NB, 2),
                    a_blk=(bm, S5_SB), a_map=lambda i, j, q: (i, j + S5_NB * q),
                    b_blk=(S5_UB, S5_SB), b_map=lambda i, j, q: (j, q),
                    o_blk=(bm, S5_UB), o_map=lambda i, j, q: (i, j),
                    out_shape=(t, S5_NB * S5_UB), out_dtype=bf16, name=name, add=add)


def _s5_bu_dw(proj, u0, g, name):
    t = proj.shape[0]
    bk = _tile(t, 2048)
    ub = u0 // S5_UB
    return _mm_core(proj, g, 'tn', grid=(S5_NB, 2, t // bk),
                    a_blk=(bk, S5_UB), a_map=lambda j, r, q: (q, ub + j),
                    b_blk=(bk, S5_SB), b_map=lambda j, r, q: (q, j + S5_NB * r),
                    o_blk=(S5_UB, S5_SB), o_map=lambda j, r, q: (j, r),
                    out_shape=(S5_NB * S5_UB, 2 * S5_SB), out_dtype=f32, name=name)


def _swiglu(a, b):
    return jax.nn.silu(a) * b


def _ride_parts(refs, n_in, n_out, ride):
    nr = ride.n if ride is not None else 0
    ins = refs[:n_in]
    r_ins = refs[n_in:n_in + nr]
    outs = refs[n_in + nr:n_in + nr + n_out]
    r_outs = refs[n_in + nr + n_out:n_in + 2 * nr + n_out]
    return ins, r_ins, outs, r_outs, refs[n_in + 2 * nr + n_out:]


def _call_with_ride(body_core, grid, in_specs, ops, out_specs, out_shape, name, ride):
    nr = ride.n if ride is not None else 0
    n_in, n_out = len(in_specs), len(out_specs)

    def body(*refs):
        ins, r_ins, outs, r_outs, r_sems = _ride_parts(refs, n_in, n_out, ride)
        ids = [pl.program_id(ax) for ax in range(len(grid))]
        if nr:
            @pl.when(functools.reduce(lambda u, v: u & v, [i == 0 for i in ids]))
            def _():
                ride.start(r_ins, r_outs, r_sems)
        body_core(ins, outs)
        if nr:
            @pl.when(functools.reduce(lambda u, v: u & v, [i == g - 1 for i, g in zip(ids, grid)]))
            def _():
                ride.wait(r_ins, r_outs, r_sems)

    res = pl.pallas_call(
        body, name=name, grid=grid, in_specs=in_specs + (ride.specs if nr else []),
        out_specs=out_specs + (ride.specs if nr else []), out_shape=out_shape + (ride.out_shape if nr else []),
        scratch_shapes=(ride.scratch if nr else []),
        compiler_params=_cp(("arbitrary",) * len(grid) if nr else ("parallel",) * len(grid)),
    )(*ops, *(ride.srcs if nr else []))
    return list(res[:n_out]), list(res[n_out:])


def _ffn_in(h, w, name, ride=None):
    t, d = h.shape
    ffp = w.shape[1] // 2
    bm, bn = _tile(t, 1024), _tile(ffp, 512)
    nb = ffp // bn
    nn = (((1,), (0,)), ((), ()))

    def core(ins, outs):
        h_ref, wa_ref, wb_ref = ins
        ab_ref, act_ref = outs
        hv = h_ref[...].astype(bf16)
        a = lax.dot_general(hv, wa_ref[...].astype(bf16), nn, preferred_element_type=f32)
        b = lax.dot_general(hv, wb_ref[...].astype(bf16), nn, preferred_element_type=f32)
        ab_ref[0] = a.astype(bf16)
        ab_ref[1] = b.astype(bf16)
        act_ref[...] = _swiglu(a, b).astype(bf16)

    (ab, act), landed = _call_with_ride(
        core, (t // bm, nb),
        [pl.BlockSpec((bm, d), lambda i, j: (i, 0)), pl.BlockSpec((d, bn), lambda i, j: (0, j)),
         pl.BlockSpec((d, bn), lambda i, j: (0, j + nb))], [h, w, w],
        [pl.BlockSpec((2, bm, bn), lambda i, j: (0, i, j)), pl.BlockSpec((bm, bn), lambda i, j: (i, j))],
        [jax.ShapeDtypeStruct((2, t, ffp), bf16), jax.ShapeDtypeStruct((t, ffp), bf16)], name, ride)
    return ab, act, landed


def _ffn_out_dx(df, wo, ab, name, ride=None):
    t, d = df.shape
    ffp = wo.shape[0]
    bm, bn = _tile(t, 1024), _tile(ffp, 512)
    nt = (((1,), (1,)), ((), ()))

    def core(ins, outs):
        df_ref, wo_ref, ab_ref = ins
        dact = lax.dot_general(df_ref[...].astype(bf16), wo_ref[...].astype(bf16), nt, preferred_element_type=f32)
        _, vjp = jax.vjp(_swiglu, ab_ref[0].astype(f32), ab_ref[1].astype(f32))
        da, db = vjp(dact)
        outs[0][0] = da.astype(bf16)
        outs[0][1] = db.astype(bf16)

    (dab,), landed = _call_with_ride(
        core, (t // bm, ffp // bn),
        [pl.BlockSpec((bm, d), lambda i, j: (i, 0)), pl.BlockSpec((bn, d), lambda i, j: (j, 0)),
         pl.BlockSpec((2, bm, bn), lambda i, j: (0, i, j))], [df, wo, ab],
        [pl.BlockSpec((2, bm, bn), lambda i, j: (0, i, j))], [jax.ShapeDtypeStruct((2, t, ffp), bf16)], name, ride)
    return dab, landed


def _ffn_in_dx(dab, w, name, ride=None):
    _, t, ffp = dab.shape
    d = w.shape[0]
    bm, bn, bk = _tile(t, 1024), _tile(d, 1024), _tile(ffp, 1408)
    nkh = ffp // bk
    return _mm_core(dab, w, 'nt', grid=(t // bm, d // bn, 2 * nkh),
                    a_blk=(None, bm, bk), a_map=lambda i, j, q: (q // nkh, i, q % nkh),
                    b_blk=(bn, bk), b_map=lambda i, j, q: (j, q),
                    o_blk=(bm, bn), o_map=lambda i, j, q: (i, j),
                    out_shape=(t, d), out_dtype=bf16, name=name, ride=ride)


def _ffn_in_dw(h, dab, name):
    _, t, ffp = dab.shape
    d = h.shape[1]
    bm, bn, bk = _tile(d, 1024), _tile(ffp, 1408), _tile(t, 2048)
    nbh = ffp // bn
    return _mm_core(h, dab, 'tn', grid=(d // bm, 2 * nbh, t // bk),
                    a_blk=(bk, bm), a_map=lambda i, j, q: (q, i),
                    b_blk=(None, bk, bn), b_map=lambda i, j, q: (j // nbh, q, j % nbh),
                    o_blk=(bm, bn), o_map=lambda i, j, q: (i, j),
                    out_shape=(d, 2 * ffp), out_dtype=bf16, name=name)


def _win(r):
    return r if isinstance(r, tuple) else (r, 0, r.shape[1])


def _row_tile(t, widths):
    per_row = 48 * max(widths)
    tm = 512
    while tm > SUBLANE and tm * per_row > RW_VMEM_BUDGET:
        tm //= 2
    return min(tm, t)


def _row_spec(r, tm):
    arr, c0, w = _win(r)
    assert c0 % w == 0, (c0, w)
    cb = c0 // w
    return pl.BlockSpec((tm, w), lambda i: (i, cb))


def _full_spec(p):
    nd = p.ndim
    return pl.BlockSpec(p.shape, lambda i: (0,) * nd)


def _rw(f, rows, params, outs, *, name, accs=(), tm=None):
    t = _win(rows[0])[0].shape[0]
    tm = tm or _row_tile(t, [_win(r)[2] for r in rows] + [w for w, _ in outs])
    nr, npar, no, na = len(rows), len(params), len(outs), len(accs)

    def body(*refs):
        vals = [r[...] for r in refs[:nr + npar]]
        res = f(*vals)
        res = res if isinstance(res, (tuple, list)) else (res,)
        for o_ref, v in zip(refs[nr + npar:nr + npar + no], res[:no]):
            o_ref[...] = v.astype(o_ref.dtype)
        if na:
            first = pl.program_id(0) == 0
            for a_ref, v in zip(refs[nr + npar + no:], res[no:]):
                @pl.when(first)
                def _(a_ref=a_ref):
                    a_ref[...] = jnp.zeros_like(a_ref)
                a_ref[...] += v

    out_shape = [jax.ShapeDtypeStruct((t, w), d) for w, d in outs] + [jax.ShapeDtypeStruct(s, f32) for s in accs]
    out_specs = [pl.BlockSpec((tm, w), lambda i: (i, 0)) for w, _ in outs] + \
                [pl.BlockSpec(s, lambda i: (0, 0)) for s in accs]
    return pl.pallas_call(
        body, name=name, grid=(t // tm,),
        in_specs=[_row_spec(r, tm) for r in rows] + [_full_spec(p) for p in params],
        out_specs=out_specs, out_shape=out_shape,
        compiler_params=_cp(("arbitrary",)),
    )(*[_win(r)[0] for r in rows], *params)


def _rw_vjp(f, rows, params, cots, *, row_grads, param_grads, name, tm=None):
    t = _win(rows[0])[0].shape[0]
    cot_rows = [c for c in cots if c is not None]
    tm = tm or _row_tile(t, [_win(r)[2] for r in rows] + [_win(c)[2] for c in cot_rows])
    nr, npar, ncot = len(rows), len(params), len(cot_rows)
    d_rows = [i for i, d in enumerate(row_grads) if d is not None]
    d_pars = [i for i, d in enumerate(param_grads) if d]

    def body(*refs):
        rv = [r[...] for r in refs[:nr]]
        pv = [r[...] for r in refs[nr:nr + npar]]
        cv = [r[...] for r in refs[nr + npar:nr + npar + ncot]]
        outs_r = refs[nr + npar + ncot:nr + npar + ncot + len(d_rows)]
        outs_p = refs[nr + npar + ncot + len(d_rows):]

        def g(*diff):
            rr, pp = list(rv), list(pv)
            for i, v in zip(d_rows, diff[:len(d_rows)]):
                rr[i] = v
            for i, v in zip(d_pars, diff[len(d_rows):]):
                pp[i] = v
            res = f(*rr, *pp)
            return tuple(res) if isinstance(res, (tuple, list)) else (res,)

        prim, vjp = jax.vjp(g, *[rv[i] for i in d_rows], *[pv[i] for i in d_pars])
        it = iter(cv)
        cts = tuple(next(it).astype(o.dtype) if c is not None else jnp.zeros_like(o) for o, c in zip(prim, cots))
        grads = vjp(cts)
        for o_ref, v in zip(outs_r, grads[:len(d_rows)]):
            o_ref[...] = v.astype(o_ref.dtype)
        first = pl.program_id(0) == 0
        for o_ref, v in zip(outs_p, grads[len(d_rows):]):
            @pl.when(first)
            def _(o_ref=o_ref):
                o_ref[...] = jnp.zeros_like(o_ref)
            o_ref[...] += v.astype(f32)

    out_shape = [jax.ShapeDtypeStruct((t, _win(rows[i])[2]), row_grads[i]) for i in d_rows] + \
                [jax.ShapeDtypeStruct(params[i].shape, f32) for i in d_pars]
    out_specs = [pl.BlockSpec((tm, _win(rows[i])[2]), lambda i_: (i_, 0)) for i in d_rows] + \
                [_full_spec(params[i]) for i in d_pars]
    return pl.pallas_call(
        body, name=name, grid=(t // tm,),
        in_specs=[_row_spec(r, tm) for r in rows] + [_full_spec(p) for p in params] + [_row_spec(c, tm) for c in cot_rows],
        out_specs=out_specs, out_shape=out_shape,
        compiler_params=_cp(("arbitrary",)),
    )(*[_win(r)[0] for r in rows], *params, *[_win(c)[0] for c in cot_rows])


def _rms(x, g):
    return x * lax.rsqrt(jnp.mean(x * x, axis=-1, keepdims=True) + EPS) * g


def _f_mod(x, nw, sh, sc):
    return (_rms(x, nw) * (1.0 + sc) + sh).astype(bf16)


def _f_mod_keep(x, nw, sh, sc):
    return _f_mod(x, nw, sh, sc), x


def _f_res_mod(coef, x, o, g, nw, sh, sc):
    x1 = x + coef * g * o.astype(f32)
    return x1, _f_mod(x1, nw, sh, sc)


def _f_ssd_pre(pre, dtraw, bias, sel):
    xc = jax.nn.silu(pre)
    dt = jax.nn.softplus(dtraw + bias)
    dt4 = jnp.dot(dt, sel, precision=HI, preferred_element_type=f32)
    return xc[:, :SSD_DI], xc[:, SSD_DI:SSD_DI + SSD_G * SSD_N], xc[:, SSD_DI + SSD_G * SSD_N:], dt4


def _f_ssd_post(y, z, nw):
    yz = y * jax.nn.silu(z)
    w = SSD_DI // SSD_G
    parts = []
    for g in range(SSD_G):
        s = yz[:, g * w:(g + 1) * w]
        parts.append(s * lax.rsqrt(jnp.mean(s * s, axis=-1, keepdims=True) + EPS))
    return (jnp.concatenate(parts, axis=1) * nw).astype(bf16)


def _f_s5_post(yb, u, d):
    return jax.nn.gelu(yb + d * u).astype(bf16)


def _f_merge(pa, glu, ga, gb):
    d = pa.shape[1]
    pb = glu[:, :d] * jax.nn.sigmoid(glu[:, d:])
    return (jax.nn.sigmoid(ga) * pa + jax.nn.sigmoid(gb) * pb).astype(bf16)


def _f_final(x2, o, tgt, g, nw):
    x3 = x2 + 0.5 * g * o.astype(f32)
    y = _rms(x3, nw)
    return 0.5 * jnp.mean(jnp.square(y - tgt), axis=-1, keepdims=True)


def _f_final_loss(x2, o, tgt, g, nw):
    rows = _f_final(x2, o, tgt, g, nw)
    return jnp.broadcast_to(jnp.sum(rows, axis=0, keepdims=True), (1, LANE))


def _f_s5_prep(lr, li, ldt, br, bi):
    dt = jnp.exp(ldt)
    lr = jnp.minimum(lr, -1e-4)
    mag = jnp.exp(lr * dt)
    ar = mag * jnp.cos(li * dt)
    ai = mag * jnp.sin(li * dt)
    den = lr * lr + li * li
    nr = ar - 1.0
    kr = (nr * lr + ai * li) / den
    ki = (ai * lr - nr * li) / den
    return ar, ai, kr * br - ki * bi, kr * bi + ki * br


def _shift_down(cur, halo8, j):
    if j == 0:
        return cur
    rolled = pltpu.roll(cur, j, 0)
    row8 = lax.broadcasted_iota(jnp.int32, halo8.shape, 0)
    top = jnp.where(row8 < j, pltpu.roll(halo8, j, 0), rolled[:SUBLANE])
    return jnp.concatenate([top, rolled[SUBLANE:]], axis=0)


def _shift_up(cur, halo8, j):
    if j == 0:
        return cur
    n = cur.shape[0]
    rolled = pltpu.roll(cur, n - j, 0)
    row8 = lax.broadcasted_iota(jnp.int32, halo8.shape, 0)
    bot = jnp.where(row8 >= SUBLANE - j, pltpu.roll(halo8, SUBLANE - j, 0), rolled[n - SUBLANE:])
    return jnp.concatenate([rolled[:n - SUBLANE], bot], axis=0)


def _conv_fwd(proj, c0, w8, b, name):
    t = proj.shape[0]
    cw = 1024
    tm = min(512, t)
    cb0 = c0 // cw
    r8 = tm // SUBLANE

    def body(x_ref, h_ref, w_ref, b_ref, o_ref):
        i = pl.program_id(1)
        x = x_ref[...]
        halo = jnp.where(i > 0, h_ref[...], 0.0)
        acc = b_ref[...] + w_ref[CONV_K - 1:CONV_K, :] * x
        for j in range(1, CONV_K):
            acc = acc + w_ref[CONV_K - 1 - j:CONV_K - j, :] * _shift_down(x, halo, j)
        o_ref[...] = acc

    return pl.pallas_call(
        body, name=name, grid=(CONV_DIM // cw, t // tm),
        in_specs=[pl.BlockSpec((tm, cw), lambda c, i: (i, cb0 + c)),
                  pl.BlockSpec((SUBLANE, cw), lambda c, i: (jnp.maximum(i * r8 - 1, 0), cb0 + c)),
                  pl.BlockSpec((SUBLANE, cw), lambda c, i: (0, c)),
                  pl.BlockSpec((1, cw), lambda c, i: (0, c))],
        out_specs=pl.BlockSpec((tm, cw), lambda c, i: (i, c)),
        out_shape=jax.ShapeDtypeStruct((t, CONV_DIM), f32),
        compiler_params=_cp(("parallel", "arbitrary")),
    )(proj, proj, w8, b)


def _conv_bwd(dpre, proj, c0, w8, name):
    t = proj.shape[0]
    cw = 1024
    tm = min(512, t)
    cb0 = c0 // cw
    r8 = tm // SUBLANE
    nb = t // tm

    def body(d_ref, dn_ref, x_ref, xh_ref, w_ref, dx_ref, dw_ref, db_ref):
        i = pl.program_id(1)
        d = d_ref[...]
        dn = jnp.where(i < nb - 1, dn_ref[...], 0.0)
        x = x_ref[...]
        xh = jnp.where(i > 0, xh_ref[...], 0.0)

        @pl.when(i == 0)
        def _():
            dw_ref[...] = jnp.zeros_like(dw_ref)
            db_ref[...] = jnp.zeros_like(db_ref)

        dx = w_ref[CONV_K - 1:CONV_K, :] * d
        rows = [jnp.sum(d * x, axis=0, keepdims=True)]
        for j in range(1, CONV_K):
            dx = dx + w_ref[CONV_K - 1 - j:CONV_K - j, :] * _shift_up(d, dn, j)
            rows.append(jnp.sum(d * _shift_down(x, xh, j), axis=0, keepdims=True))
        dx_ref[...] = dx.astype(dx_ref.dtype)
        dw = jnp.concatenate([rows[CONV_K - 1 - k] for k in range(CONV_K)] +
                             [jnp.zeros((SUBLANE - CONV_K, cw), f32)], axis=0)
        dw_ref[...] += dw
        db_ref[...] += jnp.sum(d, axis=0, keepdims=True)

    return pl.pallas_call(
        body, name=name, grid=(CONV_DIM // cw, nb),
        in_specs=[pl.BlockSpec((tm, cw), lambda c, i: (i, c)),
                  pl.BlockSpec((SUBLANE, cw), lambda c, i: (jnp.minimum((i + 1) * r8, nb * r8 - 1), c)),
                  pl.BlockSpec((tm, cw), lambda c, i: (i, cb0 + c)),
                  pl.BlockSpec((SUBLANE, cw), lambda c, i: (jnp.maximum(i * r8 - 1, 0), cb0 + c)),
                  pl.BlockSpec((SUBLANE, cw), lambda c, i: (0, c))],
        out_specs=[pl.BlockSpec((tm, cw), lambda c, i: (i, c)),
                   pl.BlockSpec((SUBLANE, cw), lambda c, i: (0, c)),
                   pl.BlockSpec((1, cw), lambda c, i: (0, c))],
        out_shape=[jax.ShapeDtypeStruct((t, CONV_DIM), bf16), jax.ShapeDtypeStruct((SUBLANE, CONV_DIM), f32),
                   jax.ShapeDtypeStruct((1, CONV_DIM), f32)],
        compiler_params=_cp(("parallel", "arbitrary")),
    )(dpre, dpre, proj, proj, w8)


def _ssd_chunk(xs, bm, cm, dt, a, dsk, h):
    n = SSD_L
    row = lax.broadcasted_iota(jnp.int32, (n, n), 0)
    col = lax.broadcasted_iota(jnp.int32, (n, n), 1)
    causal = row >= col
    cs = jnp.dot(causal.astype(f32), dt * a, precision=HI, preferred_element_type=f32)
    cs_t = cs.T
    nt = (((1,), (1,)), ((), ()))
    cb = lax.dot_general(cm.astype(bf16), bm.astype(bf16), nt, preferred_element_type=f32)
    ys, hn = [], []
    for r in range(SSD_R):
        xr = xs[:, r * SSD_P:(r + 1) * SSD_P]
        xdt = xr * dt[:, r:r + 1]
        c_col = cs[:, r:r + 1]
        decay = jnp.exp(jnp.where(causal, c_col - cs_t[r:r + 1, :], -1e30))
        y_diag = jnp.dot((cb * decay).astype(bf16), xdt.astype(bf16), preferred_element_type=f32)
        hr = h[r * SSD_P:(r + 1) * SSD_P, :]
        y_off = lax.dot_general(cm.astype(bf16), hr.astype(bf16), nt, preferred_element_type=f32) * jnp.exp(c_col)
        last = cs[n - 1:n, r:r + 1]
        st = lax.dot_general((xdt * jnp.exp(last - c_col)).astype(bf16), bm.astype(bf16), (((0,), (0,)), ((), ())),
                             preferred_element_type=f32)
        hn.append(jnp.exp(last) * hr + st)
        ys.append(y_diag + y_off + dsk[:, r:r + 1] * xr)
    return jnp.concatenate(ys, axis=1), jnp.concatenate(hn, axis=0)


SSD_GB = 1


def _ssd_specs(nc, rev):
    ch = (lambda c: nc - 1 - c) if rev else (lambda c: c)
    gw = SSD_GB * SSD_R * SSD_P
    return [pl.BlockSpec((SSD_L, gw), lambda g, c: (ch(c), g)),
            pl.BlockSpec((SSD_L, SSD_GB * SSD_N), lambda g, c: (ch(c), g)),
            pl.BlockSpec((SSD_L, SSD_GB * SSD_N), lambda g, c: (ch(c), g)),
            pl.BlockSpec((SSD_L, SSD_GB * LANE), lambda g, c: (ch(c), g)),
            pl.BlockSpec((1, SSD_GB * LANE), lambda g, c: (0, g)),
            pl.BlockSpec((1, SSD_GB * LANE), lambda g, c: (0, g))]


def _ssd_group(refs, q):
    gw = SSD_R * SSD_P
    xs_ref, bm_ref, cm_ref, dt_ref, a_ref, dsk_ref = refs
    ln = slice(q * LANE, (q + 1) * LANE)
    return (xs_ref[:, q * gw:(q + 1) * gw], bm_ref[:, ln], cm_ref[:, ln], dt_ref[:, ln], a_ref[:, ln], dsk_ref[:, ln])


def _ssd_fwd(xs, bm, cm, dt4, a4, dsk4, name, ride=None):
    t = xs.shape[0]
    nc = t // SSD_L
    gw = SSD_R * SSD_P

    nr = ride.n if ride is not None else 0
    ng = SSD_G // SSD_GB

    def body(*refs):
        xs_ref, bm_ref, cm_ref, dt_ref, a_ref, dsk_ref = refs[:6]
        r_ins = refs[6:6 + nr]
        y_ref, hs_ref = refs[6 + nr:8 + nr]
        r_outs = refs[8 + nr:8 + 2 * nr]
        h_ref = refs[8 + 2 * nr]
        r_sems = refs[9 + 2 * nr:]
        g, c = pl.program_id(0), pl.program_id(1)
        if nr:
            @pl.when((g == 0) & (c == 0))
            def _():
                ride.start(r_ins, r_outs, r_sems)

        @pl.when(c == 0)
        def _():
            h_ref[...] = jnp.zeros_like(h_ref)

        hs_ref[...] = h_ref[...]
        grp = (xs_ref, bm_ref, cm_ref, dt_ref, a_ref, dsk_ref)
        ops = [_ssd_group(grp, q) + (h_ref[q * gw:(q + 1) * gw, :],) for q in range(SSD_GB)]
        res = [_ssd_chunk(*o) for o in ops]
        for q, (y, hn) in enumerate(res):
            y_ref[:, q * gw:(q + 1) * gw] = y
            h_ref[q * gw:(q + 1) * gw, :] = hn

        if nr:
            @pl.when((g == ng - 1) & (c == nc - 1))
            def _():
                ride.wait(r_ins, r_outs, r_sems)

    res = pl.pallas_call(
        body, name=name, grid=(ng, nc), in_specs=_ssd_specs(nc, False) + (ride.specs if nr else []),
        out_specs=[pl.BlockSpec((SSD_L, SSD_GB * gw), lambda g, c: (c, g)),
                   pl.BlockSpec((None, None, SSD_GB * gw, SSD_N), lambda g, c: (g, c, 0, 0))] +
                  (ride.specs if nr else []),
        out_shape=[jax.ShapeDtypeStruct((t, SSD_DI), f32),
                   jax.ShapeDtypeStruct((ng, nc, SSD_GB * gw, SSD_N), f32)] + (ride.out_shape if nr else []),
        scratch_shapes=[pltpu.VMEM((SSD_GB * gw, SSD_N), f32)] + (ride.scratch if nr else []),
        compiler_params=_cp(("arbitrary", "arbitrary")),
    )(xs, bm, cm, dt4, a4, dsk4, *(ride.srcs if nr else []))
    return res[0], res[1], list(res[2:])


def _ssd_bwd(xs, bm, cm, dt4, a4, dsk4, hs, dy, name, ride=None):
    t = xs.shape[0]
    nc = t // SSD_L
    gw = SSD_R * SSD_P
    rc = lambda c: nc - 1 - c
    nr = ride.n if ride is not None else 0
    ng = SSD_G // SSD_GB

    def body(*refs):
        xs_ref, bm_ref, cm_ref, dt_ref, a_ref, dsk_ref, hs_ref, dy_ref = refs[:8]
        r_ins = refs[8:8 + nr]
        dxs_ref, dbm_ref, dcm_ref, ddt_ref, da_ref, ddsk_ref = refs[8 + nr:14 + nr]
        r_outs = refs[14 + nr:14 + 2 * nr]
        dh_ref = refs[14 + 2 * nr]
        r_sems = refs[15 + 2 * nr:]
        if nr:
            @pl.when((pl.program_id(0) == 0) & (pl.program_id(1) == 0))
            def _():
                ride.start(r_ins, r_outs, r_sems)

        @pl.when(pl.program_id(1) == 0)
        def _():
            dh_ref[...] = jnp.zeros_like(dh_ref)
            da_ref[...] = jnp.zeros_like(da_ref)
            ddsk_ref[...] = jnp.zeros_like(ddsk_ref)

        grp = (xs_ref, bm_ref, cm_ref, dt_ref, a_ref, dsk_ref)
        ops = [_ssd_group(grp, q) + (hs_ref[q * gw:(q + 1) * gw, :],) for q in range(SSD_GB)]
        cts = [(dy_ref[:, q * gw:(q + 1) * gw], dh_ref[q * gw:(q + 1) * gw, :]) for q in range(SSD_GB)]
        grads = [jax.vjp(_ssd_chunk, *o)[1](ct) for o, ct in zip(ops, cts)]
        for q, (dxs, dbm, dcm, ddt, da, ddsk, dh) in enumerate(grads):
            rows = slice(q * gw, (q + 1) * gw)
            ln = slice(q * LANE, (q + 1) * LANE)
            dxs_ref[:, rows] = dxs
            dbm_ref[:, ln] = dbm
            dcm_ref[:, ln] = dcm
            ddt_ref[:, ln] = ddt
            da_ref[:, ln] += da
            ddsk_ref[:, ln] += ddsk
            dh_ref[rows, :] = dh

        if nr:
            @pl.when((pl.program_id(0) == ng - 1) & (pl.program_id(1) == nc - 1))
            def _():
                ride.wait(r_ins, r_outs, r_sems)

    res = pl.pallas_call(
        body, name=name, grid=(ng, nc),
        in_specs=_ssd_specs(nc, True) + [
            pl.BlockSpec((None, None, SSD_GB * gw, SSD_N), lambda g, c: (g, rc(c), 0, 0)),
            pl.BlockSpec((SSD_L, SSD_GB * gw), lambda g, c: (rc(c), g))] + (ride.specs if nr else []),
        out_specs=[pl.BlockSpec((SSD_L, SSD_GB * gw), lambda g, c: (rc(c), g)),
                   pl.BlockSpec((SSD_L, SSD_GB * SSD_N), lambda g, c: (rc(c), g)),
                   pl.BlockSpec((SSD_L, SSD_GB * SSD_N), lambda g, c: (rc(c), g)),
                   pl.BlockSpec((SSD_L, SSD_GB * LANE), lambda g, c: (rc(c), g)),
                   pl.BlockSpec((1, SSD_GB * LANE), lambda g, c: (0, g)),
                   pl.BlockSpec((1, SSD_GB * LANE), lambda g, c: (0, g))] + (ride.specs if nr else []),
        out_shape=[jax.ShapeDtypeStruct((t, SSD_DI), f32), jax.ShapeDtypeStruct((t, SSD_G * SSD_N), f32),
                   jax.ShapeDtypeStruct((t, SSD_G * SSD_N), f32), jax.ShapeDtypeStruct((t, SSD_G * LANE), f32),
                   jax.ShapeDtypeStruct((1, SSD_G * LANE), f32), jax.ShapeDtypeStruct((1, SSD_G * LANE), f32)] +
                  (ride.out_shape if nr else []),
        scratch_shapes=[pltpu.VMEM((SSD_GB * gw, SSD_N), f32)] + (ride.scratch if nr else []),
        compiler_params=_cp(("arbitrary", "arbitrary")),
    )(xs, bm, cm, dt4, a4, dsk4, hs, dy, *(ride.srcs if nr else []))
    return list(res[:6]), list(res[6:])


S5_CH = 1024


def _s5_scan(bu, ar, ai, name):
    t = bu.shape[0]
    tb = min(128, t)

    def body(bu_ref, ar_ref, ai_ref, s_ref, carry):
        @pl.when(pl.program_id(0) == 0)
        def _():
            carry[...] = jnp.zeros_like(carry)

        for c0 in range(0, S5_S, S5_CH):
            re = pl.ds(c0, S5_CH)
            im = pl.ds(S5_S + c0, S5_CH)
            a_r = ar_ref[:, re]
            a_i = ai_ref[:, re]

            def step(k, st, re=re, im=im, a_r=a_r, a_i=a_i):
                sr, si = st
                row = pl.ds(k, 1)
                nr = a_r * sr - a_i * si + bu_ref[row, re]
                ni = a_r * si + a_i * sr + bu_ref[row, im]
                s_ref[row, re] = nr
                s_ref[row, im] = ni
                return nr, ni

            sr, si = lax.fori_loop(0, tb, step, (carry[:, re], carry[:, im]))
            carry[:, re] = sr
            carry[:, im] = si

    return pl.pallas_call(
        body, name=name, grid=(t // tb,),
        in_specs=[pl.BlockSpec((tb, 2 * S5_S), lambda i: (i, 0)),
                  pl.BlockSpec((1, S5_S), lambda i: (0, 0)), pl.BlockSpec((1, S5_S), lambda i: (0, 0))],
        out_specs=pl.BlockSpec((tb, 2 * S5_S), lambda i: (i, 0)),
        out_shape=jax.ShapeDtypeStruct((t, 2 * S5_S), f32),
        scratch_shapes=[pltpu.VMEM((1, 2 * S5_S), f32)],
        compiler_params=_cp(("arbitrary",)),
    )(bu, ar, ai)


def _s5_scan_bwd(ds, s, ar, ai, name):
    t = ds.shape[0]
    tb = min(128, t)
    nb = t // tb
    r8 = tb // SUBLANE
    rb = lambda i: nb - 1 - i

    def body(ds_ref, s_ref, sh_ref, ar_ref, ai_ref, g_ref, dar_ref, dai_ref, carry):
        i = pl.program_id(0)

        @pl.when(i == 0)
        def _():
            carry[...] = jnp.zeros_like(carry)
            dar_ref[...] = jnp.zeros_like(dar_ref)
            dai_ref[...] = jnp.zeros_like(dai_ref)

        has_prev = (i < nb - 1).astype(f32)
        for c0 in range(0, S5_S, S5_CH):
            re = pl.ds(c0, S5_CH)
            im = pl.ds(S5_S + c0, S5_CH)
            a_r = ar_ref[:, re]
            a_i = ai_ref[:, re]

            def upd(st, row, sp_r, sp_i, re=re, im=im, a_r=a_r, a_i=a_i):
                gr, gi, acr, aci = st
                ngr = ds_ref[row, re] + a_r * gr + a_i * gi
                ngi = ds_ref[row, im] + a_r * gi - a_i * gr
                g_ref[row, re] = ngr
                g_ref[row, im] = ngi
                return ngr, ngi, acr + ngr * sp_r + ngi * sp_i, aci + ngi * sp_r - ngr * sp_i

            def step(k, st, re=re, im=im, upd=upd):
                tt = tb - 1 - k
                prev = pl.ds(tt - 1, 1)
                return upd(st, pl.ds(tt, 1), s_ref[prev, re], s_ref[prev, im])

            zero = jnp.zeros((1, S5_CH), f32)
            st = lax.fori_loop(0, tb - 1, step, (carry[:, re], carry[:, im], zero, zero))
            last = pl.ds(SUBLANE - 1, 1)
            gr, gi, acr, aci = upd(st, pl.ds(0, 1), sh_ref[last, re] * has_prev, sh_ref[last, im] * has_prev)
            carry[:, re] = gr
            carry[:, im] = gi
            dar_ref[:, re] += acr
            dai_ref[:, re] += aci

    return pl.pallas_call(
        body, name=name, grid=(nb,),
        in_specs=[pl.BlockSpec((tb, 2 * S5_S), lambda i: (rb(i), 0)),
                  pl.BlockSpec((tb, 2 * S5_S), lambda i: (rb(i), 0)),
                  pl.BlockSpec((SUBLANE, 2 * S5_S), lambda i: (jnp.maximum(rb(i) * r8 - 1, 0), 0)),
                  pl.BlockSpec((1, S5_S), lambda i: (0, 0)), pl.BlockSpec((1, S5_S), lambda i: (0, 0))],
        out_specs=[pl.BlockSpec((tb, 2 * S5_S), lambda i: (rb(i), 0)),
                   pl.BlockSpec((1, S5_S), lambda i: (0, 0)), pl.BlockSpec((1, S5_S), lambda i: (0, 0))],
        out_shape=[jax.ShapeDtypeStruct((t, 2 * S5_S), f32), jax.ShapeDtypeStruct((1, S5_S), f32),
                   jax.ShapeDtypeStruct((1, S5_S), f32)],
        scratch_shapes=[pltpu.VMEM((1, 2 * S5_S), f32)],
        compiler_params=_cp(("arbitrary",)),
    )(ds, s, s, ar, ai)


def _ada_fwd(c_all, w, b, name):
    d, n = w.shape
    tn = _tile(n, 1536)

    def body(c_ref, w_ref, b_ref, o_ref):
        a = jax.nn.silu(c_ref[...]).astype(bf16)
        o_ref[...] = jnp.dot(a, w_ref[...].astype(bf16), preferred_element_type=f32) + b_ref[...]

    return pl.pallas_call(
        body, name=name, grid=(n // tn,),
        in_specs=[pl.BlockSpec(c_all.shape, lambda j: (0, 0)), pl.BlockSpec((d, tn), lambda j: (0, j)),
                  pl.BlockSpec((1, tn), lambda j: (0, j))],
        out_specs=pl.BlockSpec((c_all.shape[0], tn), lambda j: (0, j)),
        out_shape=jax.ShapeDtypeStruct((c_all.shape[0], n), f32),
        compiler_params=_cp(("parallel",)),
    )(c_all, w, b)


def _ada_bwd(c_all, dm, name):
    d = c_all.shape[1]
    n = dm.shape[1]
    tn = _tile(n, 1536)

    def body(c_ref, dm_ref, o_ref):
        a = jax.nn.silu(c_ref[...]).astype(bf16)
        o_ref[...] = lax.dot_general(a, dm_ref[...].astype(bf16), (((0,), (0,)), ((), ())), preferred_element_type=f32)

    return pl.pallas_call(
        body, name=name, grid=(n // tn,),
        in_specs=[pl.BlockSpec(c_all.shape, lambda j: (0, 0)), pl.BlockSpec((dm.shape[0], tn), lambda j: (0, j))],
        out_specs=pl.BlockSpec((d, tn), lambda j: (0, j)),
        out_shape=jax.ShapeDtypeStruct((d, n), f32),
        compiler_params=_cp(("parallel",)),
    )(c_all, dm)


def _blk_rows(r, c, nbuf, itemsize=4):
    tr = _tile(r, max(SUBLANE, (RW_VMEM_BUDGET // (2 * nbuf * c * itemsize)) // 16 * 16), 16)
    return tr if r % tr == 0 else r


def _cast_bf16(w, name):
    r, c = w.shape
    tr = _blk_rows(r, c, 2)

    def body(w_ref, o_ref):
        o_ref[...] = w_ref[...].astype(bf16)

    return pl.pallas_call(
        body, name=name, grid=(r // tr,), in_specs=[pl.BlockSpec((tr, c), lambda i: (i, 0))],
        out_specs=pl.BlockSpec((tr, c), lambda i: (i, 0)), out_shape=jax.ShapeDtypeStruct((r, c), bf16),
        compiler_params=_cp(("parallel",)),
    )(w)


def _sum_lead(parts, name):
    n, r, c = parts.shape
    tr = _blk_rows(r, c, n + 2)

    def body(p_ref, o_ref):
        acc = p_ref[0].astype(f32)
        for q in range(1, n):
            acc = acc + p_ref[q].astype(f32)
        o_ref[...] = acc

    return pl.pallas_call(
        body, name=name, grid=(r // tr,), in_specs=[pl.BlockSpec((n, tr, c), lambda i: (0, i, 0))],
        out_specs=pl.BlockSpec((tr, c), lambda i: (i, 0)), out_shape=jax.ShapeDtypeStruct((r, c), f32),
        compiler_params=_cp(("parallel",)),
    )(parts)


def _adamw(w, m, v, parts, name):
    r, c = w.shape
    npart = len(parts)
    tr = _blk_rows(r, c, 7 + npart)
    c1 = 1.0 - ADAM_B1 ** ADAM_STEP
    c2 = 1.0 - ADAM_B2 ** ADAM_STEP

    def body(*refs):
        w_ref, m_ref, v_ref = refs[:3]
        g_ref, d_ref, nm_ref, nv_ref = refs[3 + npart:]
        g = refs[3][...].astype(f32)
        for p in refs[4:3 + npart]:
            g = g + p[...].astype(f32)
        nm = ADAM_B1 * m_ref[...] + (1.0 - ADAM_B1) * g
        nv = ADAM_B2 * v_ref[...] + (1.0 - ADAM_B2) * jnp.square(g)
        g_ref[...] = g
        nm_ref[...] = nm
        nv_ref[...] = nv
        d_ref[...] = -ADAM_LR * ((nm / c1) / (jnp.sqrt(nv / c2) + ADAM_EPS) + ADAM_WD * w_ref[...])

    spec = pl.BlockSpec((tr, c), lambda i: (i, 0))
    return pl.pallas_call(
        body, name=name, grid=(r // tr,), in_specs=[spec] * (3 + npart), out_specs=[spec] * 4,
        out_shape=[jax.ShapeDtypeStruct((r, c), f32)] * 4, compiler_params=_cp(("parallel",)),
    )(w, m, v, *parts)


def _ag_small(x_shard, name):
    m_per, n = x_shard.shape

    def body(x_ref, out_ref, send_sems, recv_sems, local_sem):
        x, y, c = lax.axis_index("x"), lax.axis_index("y"), lax.axis_index("c")
        me, sibling = (x, y, c), (x, y, 1 - c)
        chips = [(1 - x, y), (x, 1 - y), (1 - x, 1 - y)]

        def rows(px, py, pc):
            return out_ref.at[pl.ds((4 * px + 2 * py + pc) * m_per, m_per), :]

        def copy(k, block, to, src=None):
            return pltpu.make_async_remote_copy(
                src_ref=rows(*block) if src is None else src, dst_ref=rows(*block),
                send_sem=send_sems.at[k], recv_sem=recv_sems.at[k], device_id=to, device_id_type=MESH)

        mine = pltpu.make_async_copy(x_ref, rows(*me), local_sem)
        mine.start()
        first = [copy(0, me, sibling, src=x_ref)]
        first += [copy(1 + j, me, (*chip, c), src=x_ref) for j, chip in enumerate(chips)]
        for cp in first:
            cp.start()
        passed = [copy(4 + j, (*chip, c), sibling) for j, chip in enumerate(chips)]
        for j, chip in enumerate(chips):
            copy(1 + j, (*chip, c), me).wait_recv()
            passed[j].start()
        copy(0, sibling, me).wait_recv()
        for j, chip in enumerate(chips):
            copy(4 + j, (*chip, 1 - c), me).wait_recv()
        for cp in first + passed:
            cp.wait_send()
        mine.wait()

    return pl.pallas_call(
        body, name=name, out_shape=jax.ShapeDtypeStruct((8 * m_per, n), x_shard.dtype),
        in_specs=[pl.BlockSpec(memory_space=pltpu.VMEM)], out_specs=pl.BlockSpec(memory_space=pltpu.VMEM),
        scratch_shapes=[pltpu.SemaphoreType.DMA((7,)), pltpu.SemaphoreType.DMA((7,)), pltpu.SemaphoreType.DMA],
        compiler_params=pltpu.CompilerParams(vmem_limit_bytes=VMEM_LIMIT),
    )(x_shard)


def _xchg(srcs, scatter, name):
    return _run_ride(_Ride(srcs, scatter), name)


def _run_ride(ride, name):
    n = ride.n

    def body(*refs):
        ride.start(refs[:n], refs[n:2 * n], refs[2 * n:])
        ride.wait(refs[:n], refs[n:2 * n], refs[2 * n:])

    return pl.pallas_call(
        body, name=name, out_shape=ride.out_shape, in_specs=ride.specs, out_specs=ride.specs,
        scratch_shapes=ride.scratch,
    )(*ride.srcs)


class _Ride:
    def __init__(self, srcs, scatter):
        self.srcs, self.scatter, self.n = list(srcs), scatter, len(srcs)
        n = self.n
        self.out_shape = [jax.ShapeDtypeStruct(s.shape if scatter else (4,) + s.shape, s.dtype) for s in srcs]
        self.specs = [pl.BlockSpec(memory_space=pl.ANY)] * n
        self.scratch = [pltpu.SemaphoreType.DMA((3 * n,)), pltpu.SemaphoreType.DMA((3 * n,)),
                        pltpu.SemaphoreType.DMA((n,))]

    def _copies(self, ins, outs, sems):
        send_sems, recv_sems, local_sems = sems
        x, y, c = lax.axis_index("x"), lax.axis_index("y"), lax.axis_index("c")
        my_k = 2 * x + y
        peers = [(1 - x, y), (x, 1 - y), (1 - x, 1 - y)]
        local, sends, recvs = [], [], []
        for a in range(self.n):
            own = ins[a].at[my_k] if self.scatter else ins[a]
            local.append(pltpu.make_async_copy(own, outs[a].at[my_k], local_sems.at[a]))
            for j, (px, py) in enumerate(peers):
                sems_j = dict(send_sem=send_sems.at[3 * a + j], recv_sem=recv_sems.at[3 * a + j],
                              device_id=(px, py, c), device_id_type=MESH)
                src = ins[a].at[2 * px + py] if self.scatter else ins[a]
                sends.append(pltpu.make_async_remote_copy(src_ref=src, dst_ref=outs[a].at[my_k], **sems_j))
                landed = outs[a].at[2 * px + py]
                recvs.append(pltpu.make_async_remote_copy(src_ref=landed, dst_ref=landed, **sems_j))
        return local, sends, recvs

    def start(self, ins, outs, sems):
        local, sends, _ = self._copies(ins, outs, sems)
        for cp in local + sends:
            cp.start()

    def wait(self, ins, outs, sems):
        local, sends, recvs = self._copies(ins, outs, sems)
        for cp in recvs:
            cp.wait_recv()
        for cp in sends:
            cp.wait_send()
        for cp in local:
            cp.wait()


class _RideGather:
    def __init__(self, srcs):
        self.srcs, self.n = list(srcs), len(srcs)
        n = self.n
        assert all(s.shape[0] % 32 == 0 for s in srcs)
        self.out_shape = [jax.ShapeDtypeStruct((4,) + s.shape, s.dtype) for s in srcs]
        self.specs = [pl.BlockSpec(memory_space=pl.ANY)] * n
        dma = pltpu.SemaphoreType.DMA
        self.scratch = [dma((3 * n,)), dma((3 * n,)), dma((3 * n,)), dma((3 * n,)), dma((n,))]

    def _copies(self, ins, outs, sems):
        send_sems, recv_sems, pass_send, pass_recv, local_sems = sems
        x, y, c = lax.axis_index("x"), lax.axis_index("y"), lax.axis_index("c")
        my_k = 2 * x + y
        peers = [(1 - x, y), (x, 1 - y), (1 - x, 1 - y)]
        local, sends, recvs, passes, pass_recvs = [], [], [], [], []
        for a in range(self.n):
            half = self.srcs[a].shape[0] // 2
            mine = pl.ds(pl.multiple_of(c * half, 16), half)
            other = pl.ds(pl.multiple_of((1 - c) * half, 16), half)
            local.append(pltpu.make_async_copy(ins[a], outs[a].at[my_k], local_sems.at[a]))
            for j, (px, py) in enumerate(peers):
                q = 3 * a + j
                over_ici = dict(send_sem=send_sems.at[q], recv_sem=recv_sems.at[q], device_id=(px, py, c),
                                device_id_type=MESH)
                to_sibling = dict(send_sem=pass_send.at[q], recv_sem=pass_recv.at[q], device_id=(x, y, 1 - c),
                                  device_id_type=MESH)
                sends.append(pltpu.make_async_remote_copy(src_ref=ins[a].at[mine], dst_ref=outs[a].at[my_k, mine],
                                                          **over_ici))
                landed = outs[a].at[2 * px + py, mine]
                recvs.append(pltpu.make_async_remote_copy(src_ref=landed, dst_ref=landed, **over_ici))
                passes.append(pltpu.make_async_remote_copy(src_ref=landed, dst_ref=landed, **to_sibling))
                from_sibling = outs[a].at[2 * px + py, other]
                pass_recvs.append(pltpu.make_async_remote_copy(src_ref=from_sibling, dst_ref=from_sibling, **to_sibling))
        return local, sends, recvs, passes, pass_recvs

    def start(self, ins, outs, sems):
        local, sends = self._copies(ins, outs, sems)[:2]
        for cp in local + sends:
            cp.start()

    def wait(self, ins, outs, sems):
        local, sends, recvs, passes, pass_recvs = self._copies(ins, outs, sems)
        for rc, ps in zip(recvs, passes):
            rc.wait_recv()
            ps.start()
        for cp in pass_recvs:
            cp.wait_recv()
        for cp in sends + passes:
            cp.wait_send()
        for cp in local:
            cp.wait()


def _swap_sibling(srcs, name):
    n = len(srcs)

    def body(*refs):
        ins, outs = refs[:n], refs[n:2 * n]
        send_sems, recv_sems = refs[2 * n:]
        sib = (lax.axis_index("x"), lax.axis_index("y"), 1 - lax.axis_index("c"))
        cps = [pltpu.make_async_remote_copy(src_ref=ins[a], dst_ref=outs[a], send_sem=send_sems.at[a],
                                            recv_sem=recv_sems.at[a], device_id=sib, device_id_type=MESH)
               for a in range(n)]
        for cp in cps:
            cp.start()
        for cp in cps:
            cp.wait_recv()
        for cp in cps:
            cp.wait_send()

    anyspec = pl.BlockSpec(memory_space=pl.ANY)
    return pl.pallas_call(
        body, name=name, out_shape=[jax.ShapeDtypeStruct(s.shape, s.dtype) for s in srcs],
        in_specs=[anyspec] * n, out_specs=[anyspec] * n,
        scratch_shapes=[pltpu.SemaphoreType.DMA((n,)), pltpu.SemaphoreType.DMA((n,))],
    )(*srcs)


def _gather8(vec, name):
    size = vec.shape[0]
    n = _round_up(size, SUBLANE * LANE)
    blk = jnp.concatenate([vec, jnp.zeros((n - size,), f32)]).reshape(SUBLANE, n // SUBLANE)
    out = _ag_small(blk, name)
    return out.reshape(8, n)[:, :size]


def _sel_matrix():
    sel = np.zeros((LANE, SSD_G * LANE), np.float32)
    for h in range(SSD_HEADS):
        sel[h, (h // SSD_R) * LANE + h % SSD_R] = 1.0
    return jnp.asarray(sel)


def _heads_to_lanes(v):
    z = jnp.zeros((SSD_G, LANE), f32).at[:, :SSD_R].set(v.reshape(SSD_G, SSD_R))
    return z.reshape(1, SSD_G * LANE)


def _lanes_to_heads(v):
    return v.reshape(SSD_G, LANE)[:, :SSD_R].reshape(SSD_HEADS)


def _block_diag8(blocks):
    g, r, c = blocks.shape
    b = blocks.reshape(g // S5_NB, S5_NB, r, c)
    eye = jnp.eye(S5_NB, dtype=bool)[None, :, None, :, None]
    return jnp.where(eye, b[:, :, :, None, :], jnp.zeros((), blocks.dtype)).reshape(g * r, S5_NB * c)


def _diag8(mat, r, c):
    g = mat.shape[0] // r
    m = mat.reshape(g // S5_NB, S5_NB, r, S5_NB, c)
    eye = jnp.eye(S5_NB, dtype=bool)[None, :, None, :, None]
    return jnp.where(eye, m, 0.0).sum(axis=3).reshape(g, r, c)


class _Layout:
    def __init__(self, d):
        self.d = d
        self.z, self.xbc, self.u = 0, SSD_DI, SSD_DI + CONV_DIM
        self.ga = self.u + S5_W
        self.gb = self.ga + d
        self.dt = self.gb + d
        self.np_ = self.dt + LANE
        self.in_cols = SSD_DI + CONV_DIM + SSD_HEADS + S5_W + 2 * d
        off_dt = SSD_DI + CONV_DIM
        off_u = off_dt + SSD_HEADS
        off_g = off_u + S5_W
        self.src = [(0, off_dt), (off_u, off_u + S5_W + 2 * d), (off_dt, off_u)]

    def arrange(self, w):
        (a0, a1), (b0, b1), (c0, c1) = self.src
        pad = jnp.zeros((w.shape[0], LANE - SSD_HEADS), w.dtype)
        return jnp.concatenate([w[:, a0:a1], w[:, b0:b1], w[:, c0:c1], pad], axis=1)

    def arrange_slabs(self, g):
        pieces = [p for lo, hi in self.src for p in _cols_from_slabs(g, lo, hi)]
        pieces.append(jnp.zeros((g.shape[1], LANE - SSD_HEADS), g.dtype))
        return jnp.concatenate(pieces, axis=1)

    def restore_slabs(self, w):
        (a0, a1), (b0, b1), (c0, c1) = self.src
        n_a, n_b = a1 - a0, b1 - b0
        segs = [(a0, a1, 0), (c0, c1, n_a + n_b), (b0, b1, n_a)]
        cs = self.in_cols // 4
        slabs = []
        for k in range(4):
            lo, hi = k * cs, (k + 1) * cs
            parts = [w[:, pos + max(lo, s0) - s0:pos + min(hi, s1) - s0] for s0, s1, pos in segs
                     if max(lo, s0) < min(hi, s1)]
            slabs.append(jnp.concatenate(parts, axis=1))
        return jnp.stack(slabs)


def _cols_from_slabs(g, start, stop):
    c = g.shape[2]
    return [g[k][:, max(start, k * c) - k * c:min(stop, (k + 1) * c) - k * c] for k in range(4)
            if max(start, k * c) < min(stop, (k + 1) * c)]


def _unshard_cols(g):
    return jnp.concatenate([g[k] for k in range(4)], axis=1)


def _shard_cols(w):
    r, c4 = w.shape
    return w.reshape(r, 4, c4 // 4).transpose(1, 0, 2)


def kernel(x, c, w_ada, b_ada, norm_ffn1, w_ffn1_in, w_ffn1_out, norm_mix, w_in, conv_w, conv_b, dt_bias, a_log, d_ssd, ssd_norm_w, w_a_proj, s5_lambda_re, s5_lambda_im, s5_b_re, s5_b_im, s5_c_re, s5_c_im, s5_d, s5_log_dt, w_b_glu, w_out, norm_ffn2, w_ffn2_in, w_ffn2_out, norm_final, loss_target, m_w_ada, m_b_ada, m_norm_ffn1, m_w_ffn1_in, m_w_ffn1_out, m_norm_mix, m_w_in, m_conv_w, m_conv_b, m_dt_bias, m_a_log, m_d_ssd, m_ssd_norm_w, m_w_a_proj, m_s5_lambda_re, m_s5_lambda_im, m_s5_b_re, m_s5_b_im, m_s5_c_re, m_s5_c_im, m_s5_d, m_s5_log_dt, m_w_b_glu, m_w_out, m_norm_ffn2, m_w_ffn2_in, m_w_ffn2_out, m_norm_final, v_w_ada, v_b_ada, v_norm_ffn1, v_w_ffn1_in, v_w_ffn1_out, v_norm_mix, v_w_in, v_conv_w, v_conv_b, v_dt_bias, v_a_log, v_d_ssd, v_ssd_norm_w, v_w_a_proj, v_s5_lambda_re, v_s5_lambda_im, v_s5_b_re, v_s5_b_im, v_s5_c_re, v_s5_c_im, v_s5_d, v_s5_log_dt, v_w_b_glu, v_w_out, v_norm_ffn2, v_w_ffn2_in, v_w_ffn2_out, v_norm_final):
    W = dict(w_ada=w_ada, b_ada=b_ada, norm_ffn1=norm_ffn1, w_ffn1_in=w_ffn1_in, w_ffn1_out=w_ffn1_out, norm_mix=norm_mix, w_in=w_in, conv_w=conv_w, conv_b=conv_b, dt_bias=dt_bias, a_log=a_log, d_ssd=d_ssd, ssd_norm_w=ssd_norm_w, w_a_proj=w_a_proj, s5_lambda_re=s5_lambda_re, s5_lambda_im=s5_lambda_im, s5_b_re=s5_b_re, s5_b_im=s5_b_im, s5_c_re=s5_c_re, s5_c_im=s5_c_im, s5_d=s5_d, s5_log_dt=s5_log_dt, w_b_glu=w_b_glu, w_out=w_out, norm_ffn2=norm_ffn2, w_ffn2_in=w_ffn2_in, w_ffn2_out=w_ffn2_out, norm_final=norm_final)
    Mo = dict(w_ada=m_w_ada, b_ada=m_b_ada, norm_ffn1=m_norm_ffn1, w_ffn1_in=m_w_ffn1_in, w_ffn1_out=m_w_ffn1_out, norm_mix=m_norm_mix, w_in=m_w_in, conv_w=m_conv_w, conv_b=m_conv_b, dt_bias=m_dt_bias, a_log=m_a_log, d_ssd=m_d_ssd, ssd_norm_w=m_ssd_norm_w, w_a_proj=m_w_a_proj, s5_lambda_re=m_s5_lambda_re, s5_lambda_im=m_s5_lambda_im, s5_b_re=m_s5_b_re, s5_b_im=m_s5_b_im, s5_c_re=m_s5_c_re, s5_c_im=m_s5_c_im, s5_d=m_s5_d, s5_log_dt=m_s5_log_dt, w_b_glu=m_w_b_glu, w_out=m_w_out, norm_ffn2=m_norm_ffn2, w_ffn2_in=m_w_ffn2_in, w_ffn2_out=m_w_ffn2_out, norm_final=m_norm_final)
    Vo = dict(w_ada=v_w_ada, b_ada=v_b_ada, norm_ffn1=v_norm_ffn1, w_ffn1_in=v_w_ffn1_in, w_ffn1_out=v_w_ffn1_out, norm_mix=v_norm_mix, w_in=v_w_in, conv_w=v_conv_w, conv_b=v_conv_b, dt_bias=v_dt_bias, a_log=v_a_log, d_ssd=v_d_ssd, ssd_norm_w=v_ssd_norm_w, w_a_proj=v_w_a_proj, s5_lambda_re=v_s5_lambda_re, s5_lambda_im=v_s5_lambda_im, s5_b_re=v_s5_b_re, s5_b_im=v_s5_b_im, s5_c_re=v_s5_c_re, s5_c_im=v_s5_c_im, s5_d=v_s5_d, s5_log_dt=v_s5_log_dt, w_b_glu=v_w_b_glu, w_out=v_w_out, norm_ffn2=v_norm_ffn2, w_ffn2_in=v_w_ffn2_in, w_ffn2_out=v_w_ffn2_out, norm_final=v_norm_final)

    t, d = x.shape[1], x.shape[2]
    ff = 4 * w_ffn1_out.shape[1]
    ffp = _round_up(ff, 512)
    lay = _Layout(d)
    xi, yi, ci = lax.axis_index("x"), lax.axis_index("y"), lax.axis_index("c")
    k_me = 2 * xi + yi
    e_me = 4 * xi + 2 * yi + ci
    x2d = x[0]
    tgt = loss_target[0]

    cw_cols = conv_w.shape[2]
    g1 = _gather8(jnp.concatenate([c[0], conv_w[0].reshape(-1)]), "gather_c_convw")
    c_all = g1[:, :d]
    conv_full = g1[::2, d:].reshape(4, CONV_K, cw_cols).transpose(1, 0, 2).reshape(CONV_K, CONV_DIM)
    conv_w8 = jnp.zeros((SUBLANE, CONV_DIM), f32).at[:CONV_K].set(conv_full)

    n_ada_loc = w_ada.shape[2]
    b_loc = lax.dynamic_slice(b_ada, (0, k_me * n_ada_loc), (1, n_ada_loc))
    mods_part = _ada_fwd(c_all, w_ada[0], b_loc, "ada_fwd")
    g2 = _gather8(mods_part.reshape(-1), "gather_mods").reshape(8, 8, n_ada_loc)
    mods = lax.dynamic_index_in_dim(g2[::2], e_me, axis=1, keepdims=False).reshape(N_ADA, d)
    sh1, sc1, gt1, sh2, sc2, gt2, sh3, sc3, gt3 = [mods[i:i + 1] for i in range(N_ADA)]

    cast = {n: _cast_bf16(W[n][0], "cast_" + n) for n in BIG}

    def gather_of(names):
        return _RideGather([cast[n] for n in names])

    def rows_of(g):
        return g.reshape(4 * g.shape[1], g.shape[2])

    def ffn_in(g):
        z = jnp.zeros((g.shape[1], ffp - ff), g.dtype)
        return jnp.concatenate([g[0], g[1], z, g[2], g[3], z], axis=1)

    def ffn_out(g):
        return jnp.concatenate([rows_of(g), jnp.zeros((ffp - ff, g.shape[2]), g.dtype)], axis=0)

    nf1, nmx, nf2 = norm_ffn1, norm_mix, norm_ffn2
    nfin = norm_final.reshape(1, d)

    (g_w1i,) = _run_ride(gather_of(['w_ffn1_in']), "gather_w_ffn1_in")
    w1i = ffn_in(g_w1i)
    (h1,) = _rw(_f_mod, [x2d], [nf1, sh1, sc1], [(d, bf16)], name="mod1")
    ab1, act1, (g_w1o, g_wa, g_wglu, g_wo) = _ffn_in(
        h1, w1i, "ffn1_in", ride=gather_of(['w_ffn1_out', 'w_a_proj', 'w_b_glu', 'w_out']))
    w1o = ffn_out(g_w1o)
    w_a = rows_of(g_wa)
    w_glu, w_o = _unshard_cols(g_wglu), rows_of(g_wo)
    f1, (g_win,) = _mm(act1, w1o, 'nn', out_dtype=bf16, name="ffn1_out", ride=gather_of(['w_in']))
    w_inr = lay.arrange_slabs(g_win)
    res1 = functools.partial(_f_res_mod, 0.5)
    x1, h2 = _rw(res1, [x2d, f1], [gt1, nmx, sh2, sc2], [(d, f32), (d, bf16)], name="res1_mod2")
    proj, (g_w2i,) = _mm(h2, w_inr, 'nn', out_dtype=f32, name="in_proj", ride=gather_of(['w_ffn2_in']))
    w2i = ffn_in(g_w2i)

    pre = _conv_fwd(proj, lay.xbc, conv_w8, conv_b, "conv_fwd")
    sel = _sel_matrix()
    bias128 = jnp.zeros((1, LANE), f32).at[:, :SSD_HEADS].set(dt_bias)
    xs, bm, cm, dt4 = _rw(_f_ssd_pre, [pre, (proj, lay.dt, LANE)], [bias128, sel],
                          [(SSD_DI, f32), (SSD_G * SSD_N, f32), (SSD_G * SSD_N, f32), (SSD_G * LANE, f32)],
                          name="ssd_pre")

    def head_params(a_log_, d_ssd_):
        return _heads_to_lanes(-jnp.exp(a_log_[0])), _heads_to_lanes(d_ssd_[0])

    (a4, dsk4), head_vjp = jax.vjp(head_params, a_log, d_ssd)
    y_ssd, hs, (g_w2o,) = _ssd_fwd(xs, bm, cm, dt4, a4, dsk4, "ssd_fwd", ride=gather_of(['w_ffn2_out']))
    w2o = ffn_out(g_w2o)
    (y_a,) = _rw(_f_ssd_post, [y_ssd, (proj, lay.z, SSD_DI)], [ssd_norm_w], [(SSD_DI, bf16)], name="ssd_post")
    p_a = _mm(y_a, w_a, 'nn', out_dtype=f32, name="a_proj")

    col = lambda v: v.reshape(S5_S, 1)
    ldt_col = jnp.repeat(s5_log_dt[0], S5_P).reshape(S5_S, 1)
    prep_rows = [col(s5_lambda_re[0]), col(s5_lambda_im[0]), ldt_col,
                 s5_b_re[0].reshape(S5_S, S5_I), s5_b_im[0].reshape(S5_S, S5_I)]
    ar, ai, bbr, bbi = _rw(_f_s5_prep, prep_rows, [], [(1, f32), (1, f32), (S5_I, f32), (S5_I, f32)],
                           name="s5_prep", tm=512)
    to_bd = lambda bb: _block_diag8(bb.reshape(S5_G, S5_P, S5_I).transpose(0, 2, 1).astype(bf16))
    bd_c = jnp.concatenate([to_bd(bbr), to_bd(bbi)], axis=1)
    c_c = jnp.concatenate([_block_diag8(s5_c_re[0].transpose(0, 2, 1).astype(bf16)),
                           _block_diag8((-s5_c_im[0]).transpose(0, 2, 1).astype(bf16))], axis=0)
    ar_row, ai_row = ar.reshape(1, S5_S), ai.reshape(1, S5_S)
    bu = _s5_bu(proj, lay.u, bd_c, "s5_bu")
    s5s = _s5_scan(bu, ar_row, ai_row, "s5_scan")
    yb = _s5_out(s5s, c_c, "s5_out")
    d_row = s5_d[0].reshape(1, S5_W)
    (gl,) = _rw(_f_s5_post, [yb, (proj, lay.u, S5_W)], [d_row], [(S5_W, bf16)], name="s5_post")
    glu = _mm(gl, w_glu, 'nn', out_dtype=f32, name="glu_proj")

    merge_rows = [p_a, glu, (proj, lay.ga, d), (proj, lay.gb, d)]
    (merged,) = _rw(_f_merge, merge_rows, [], [(d, bf16)], name="merge")
    o_mix = _mm(merged, w_o, 'nn', out_dtype=bf16, name="out_proj")
    res2 = functools.partial(_f_res_mod, 1.0)
    x2, h3 = _rw(res2, [x1, o_mix], [gt2, nf2, sh3, sc3], [(d, f32), (d, bf16)], name="res2_mod3")
    ab2, act2, _ = _ffn_in(h3, w2i, "ffn2_in")
    f2 = _mm(act2, w2o, 'nn', out_dtype=bf16, name="ffn2_out")
    (loss_acc,) = _rw(_f_final_loss, [x2, f2, tgt], [gt3, nfin], [], accs=[(1, LANE)], name="loss")
    loss = lax.psum(loss_acc[0, 0], AXES)

    ones = jnp.ones((t, 1), f32)
    dx2, df2, dgt3, dnfin = _rw_vjp(_f_final, [x2, f2, tgt], [gt3, nfin], [ones],
                                    row_grads=[f32, bf16, None], param_grads=[True, True], name="loss_bwd")
    dab2, _ = _ffn_out_dx(df2, w2o, ab2, "ffn2_out_dx")
    dw2o = _mm(act2, df2, 'tn', out_dtype=bf16, name="ffn2_out_dw")
    dh3 = _ffn_in_dx(dab2, w2i, "ffn2_in_dx")
    dw2i = _ffn_in_dw(h3, dab2, "ffn2_in_dw")
    dx1, do_mix, dgt2, dnf2, dsh3, dsc3 = _rw_vjp(
        res2, [x1, o_mix], [gt2, nf2, sh3, sc3], [dx2, dh3], row_grads=[f32, bf16], param_grads=[True] * 4,
        name="res2_mod3_bwd")
    dmerged = _mm(do_mix, w_o, 'nt', out_dtype=bf16, name="out_proj_dx")
    dw_o = _mm(merged, do_mix, 'tn', out_dtype=bf16, name="out_proj_dw")
    dp_a, dglu, dga, dgb = _rw_vjp(_f_merge, merge_rows, [], [dmerged], row_grads=[bf16] * 4,
                                   param_grads=[], name="merge_bwd")

    dgl = _mm(dglu, w_glu, 'nt', out_dtype=bf16, name="glu_proj_dx")
    dw_glu = _mm(gl, dglu, 'tn', out_dtype=bf16, name="glu_proj_dw")
    dyb, du_skip, dd_row = _rw_vjp(_f_s5_post, [yb, (proj, lay.u, S5_W)], [d_row], [dgl],
                                   row_grads=[bf16, f32], param_grads=[True], name="s5_post_bwd")
    ds5 = _s5_out_dx(dyb, c_c, "s5_out_dx")
    dc_c = _s5_out_dw(s5s, dyb, "s5_out_dw")
    g5, dar, dai = _s5_scan_bwd(ds5, s5s, ar_row, ai_row, "s5_scan_bwd")
    du = _s5_bu_dx(g5, bd_c, du_skip, "s5_bu_dx")
    dbd_c = _s5_bu_dw(proj, lay.u, g5, "s5_bu_dw")
    from_bd = lambda m_: _diag8(m_, S5_I, S5_P).transpose(0, 2, 1).reshape(S5_S, S5_I)
    dprep = _rw_vjp(_f_s5_prep, prep_rows, [], [dar.reshape(S5_S, 1), dai.reshape(S5_S, 1),
                                                from_bd(dbd_c[:, :S5_SB]), from_bd(dbd_c[:, S5_SB:])],
                    row_grads=[f32] * 5, param_grads=[], name="s5_prep_bwd", tm=512)
    dlr, dli, dldt, dbr, dbi = dprep
    g_s5 = dict(
        s5_lambda_re=dlr.reshape(S5_G, S5_P), s5_lambda_im=dli.reshape(S5_G, S5_P),
        s5_log_dt=dldt.reshape(S5_G, S5_P).sum(axis=1),
        s5_b_re=dbr.reshape(S5_G, S5_P, S5_I), s5_b_im=dbi.reshape(S5_G, S5_P, S5_I),
        s5_c_re=_diag8(dc_c[:S5_S], S5_P, S5_I).transpose(0, 2, 1),
        s5_c_im=-_diag8(dc_c[S5_S:], S5_P, S5_I).transpose(0, 2, 1),
        s5_d=dd_row.reshape(S5_G, S5_I))

    dy_a = _mm(dp_a, w_a, 'nt', out_dtype=bf16, name="a_proj_dx")
    dw_a = _mm(y_a, dp_a, 'tn', out_dtype=bf16, name="a_proj_dw")
    dy_ssd, dz, dssd_nw = _rw_vjp(_f_ssd_post, [y_ssd, (proj, lay.z, SSD_DI)], [ssd_norm_w], [dy_a],
                                  row_grads=[f32, bf16], param_grads=[True], name="ssd_post_bwd")
    def ffn_in_back(g):
        hf = ff // 2
        return jnp.stack([g[:, :hf], g[:, hf:ff], g[:, ffp:ffp + hf], g[:, ffp + hf:ffp + ff]])

    def rows_back(g, rows):
        return g[:rows].reshape(4, rows // 4, g.shape[1])

    def scatter_of(pairs):
        return _Ride([g for _, g in pairs], True)

    terms = {}
    early = [('w_ffn2_out', rows_back(dw2o, ff)), ('w_ffn2_in', ffn_in_back(dw2i)), ('w_out', rows_back(dw_o, d)),
             ('w_b_glu', _shard_cols(dw_glu)), ('w_a_proj', rows_back(dw_a, SSD_DI))]
    (dxs, dbm, dcm, ddt4, da4, ddsk4), landed = _ssd_bwd(xs, bm, cm, dt4, a4, dsk4, hs, dy_ssd, "ssd_bwd",
                                                         ride=scatter_of(early))
    terms.update({n: p for (n, _), p in zip(early, landed)})
    da_log, dd_ssd = head_vjp((da4, ddsk4))
    dpre, ddt_raw, dbias128 = _rw_vjp(_f_ssd_pre, [pre, (proj, lay.dt, LANE)], [bias128, sel], [dxs, dbm, dcm, ddt4],
                                      row_grads=[f32, bf16], param_grads=[True, False], name="ssd_pre_bwd")
    dxbc, dconv_w8, dconv_b = _conv_bwd(dpre, proj, lay.xbc, conv_w8, "conv_bwd")

    dproj = jnp.concatenate([dz, dxbc, du, dga, dgb, ddt_raw], axis=1)
    dw_inr = _mm(h2, dproj, 'tn', out_dtype=bf16, name="in_proj_dw")
    dh2, (terms['w_in'],) = _mm(dproj, w_inr, 'nt', out_dtype=bf16, name="in_proj_dx",
                                ride=scatter_of([('w_in', lay.restore_slabs(dw_inr))]))
    dx0, df1, dgt1, dnmx, dsh2, dsc2 = _rw_vjp(
        res1, [x2d, f1], [gt1, nmx, sh2, sc2], [dx1, dh2], row_grads=[f32, bf16], param_grads=[True] * 4,
        name="res1_mod2_bwd")
    dw1o = _mm(act1, df1, 'tn', out_dtype=bf16, name="ffn1_out_dw")
    dab1, (terms['w_ffn1_out'],) = _ffn_out_dx(df1, w1o, ab1, "ffn1_out_dx",
                                               ride=scatter_of([('w_ffn1_out', rows_back(dw1o, ff))]))
    dw1i = _ffn_in_dw(h1, dab1, "ffn1_in_dw")
    dh1, (terms['w_ffn1_in'],) = _ffn_in_dx(dab1, w1i, "ffn1_in_dx",
                                            ride=scatter_of([('w_ffn1_in', ffn_in_back(dw1i))]))
    grad_x, dnf1, dsh1, dsc1 = _rw_vjp(_f_mod_keep, [x2d], [nf1, sh1, sc1], [dh1, dx0],
                                       row_grads=[f32], param_grads=[True] * 3, name="mod1_bwd")
    d_mods = jnp.concatenate([dsh1, dsc1, dgt1, dsh2, dsc2, dgt2, dsh3, dsc3, dgt3], axis=1).reshape(-1)

    sums = [_sum_lead(terms[n], "sum_" + n) for n in BIG]
    others = _swap_sibling(sums, "swap_sums")

    out_g, out_d, out_m, out_v = {}, {}, {}, {}
    for n, s_own, s_sib in zip(BIG, sums, others):
        r = _adamw(W[n][0], Mo[n][0], Vo[n][0], [s_own, s_sib], "adamw_" + n)
        out_g[n], out_d[n], out_m[n], out_v[n] = [o[None] for o in r]

    local = dict(
        b_ada=d_mods, norm_ffn1=dnf1, norm_mix=dnmx, conv_w=dconv_w8[:CONV_K], conv_b=dconv_b,
        dt_bias=dbias128[:, :SSD_HEADS], a_log=da_log, d_ssd=dd_ssd, ssd_norm_w=dssd_nw,
        norm_ffn2=dnf2, norm_final=dnfin, **g_s5)
    flat = jnp.concatenate([local[n].reshape(-1) for n in SMALL])
    g3 = _gather8(flat, "gather_small_grads")
    n_small = flat.shape[0]
    npad = _round_up(n_small, SUBLANE * LANE)
    g3p = jnp.zeros((8, npad), f32).at[:, :n_small].set(g3).reshape(8, npad // LANE, LANE)
    gsum = _sum_lead(g3p, "sum_small").reshape(-1)

    def local_shard(n, a):
        if n == 'conv_w':
            return lax.dynamic_slice(a.reshape(CONV_K, CONV_DIM), (0, k_me * cw_cols), (CONV_K, cw_cols))
        return a

    pieces, off = {}, 0
    for n in SMALL:
        sz = local[n].size
        pieces[n] = local_shard(n, gsum[off:off + sz]).reshape(W[n].shape)
        off += sz

    def pack(dct):
        v_ = jnp.concatenate([dct[n].reshape(-1) for n in SMALL])
        pad = _round_up(v_.shape[0], SUBLANE * LANE) - v_.shape[0]
        return jnp.concatenate([v_, jnp.ones((pad,), f32)]).reshape(-1, LANE)

    rs = _adamw(pack(W), pack(Mo), pack(Vo), [pack(pieces)], "adamw_small")
    off = 0
    for n in SMALL:
        sz = W[n].size
        out_g[n], out_d[n], out_m[n], out_v[n] = [o.reshape(-1)[off:off + sz].reshape(W[n].shape) for o in rs]
        off += sz

    dm_loc = lax.dynamic_slice(g3[:, :N_ADA * d], (0, k_me * n_ada_loc), (SUBLANE, n_ada_loc))
    g_ada = _ada_bwd(c_all, dm_loc, "ada_bwd")
    r = _adamw(w_ada[0], m_w_ada[0], v_w_ada[0], [g_ada], "adamw_w_ada")
    out_g['w_ada'], out_d['w_ada'], out_m['w_ada'], out_v['w_ada'] = [o[None] for o in r]

    return (loss, grad_x[None], *[out_g[n] for n in WEIGHTS], *[out_d[n] for n in WEIGHTS],
            *[out_m[n] for n in WEIGHTS], *[out_v[n] for n in WEIGHTS])
```

```python
import functools
import math

import numpy as np
import jax
import jax.numpy as jnp
from jax import lax
from jax.experimental import pallas as pl
from jax.experimental.pallas import tpu as pltpu

f32 = jnp.float32
bf16 = jnp.bfloat16
HI = lax.Precision.HIGHEST
MESH = pl.DeviceIdType.MESH
AXES = ("x", "y", "c")

EPS = 1e-6
SSD_HEADS, SSD_P, SSD_N, SSD_G, SSD_R, SSD_L = 32, 64, 128, 4, 8, 128
SSD_DI = SSD_HEADS * SSD_P
CONV_K = 4
CONV_DIM = SSD_DI + 2 * SSD_G * SSD_N
S5_W, S5_G, S5_I, S5_P = 1024, 64, 16, 64
S5_S = S5_G * S5_P
N_ADA = 9
ADAM_LR, ADAM_B1, ADAM_B2, ADAM_EPS, ADAM_WD, ADAM_STEP = 0.001, 0.9, 0.999, 1e-08, 0.01, 10

LANE = 128
SUBLANE = 8
VMEM_LIMIT = 56 << 20
MM_VMEM_BUDGET = 40 << 20
RW_VMEM_BUDGET = 36 << 20

WEIGHTS = ['w_ada', 'b_ada', 'norm_ffn1', 'w_ffn1_in', 'w_ffn1_out', 'norm_mix', 'w_in', 'conv_w', 'conv_b', 'dt_bias',
           'a_log', 'd_ssd', 'ssd_norm_w', 'w_a_proj', 's5_lambda_re', 's5_lambda_im', 's5_b_re', 's5_b_im', 's5_c_re',
           's5_c_im', 's5_d', 's5_log_dt', 'w_b_glu', 'w_out', 'norm_ffn2', 'w_ffn2_in', 'w_ffn2_out', 'norm_final']
BIG = ['w_ffn1_in', 'w_ffn1_out', 'w_in', 'w_a_proj', 'w_b_glu', 'w_out', 'w_ffn2_in', 'w_ffn2_out']
COL_SHARDED = ('w_ffn1_in', 'w_in', 'w_b_glu', 'w_ffn2_in')
SMALL = [n for n in WEIGHTS if n not in BIG and n != 'w_ada']


def _cp(sem=None):
    return pltpu.CompilerParams(dimension_semantics=sem, vmem_limit_bytes=VMEM_LIMIT)


def _tile(dim, target, align=LANE):
    if dim <= target:
        return dim
    t = (target // align) * align
    while t >= align:
        if dim % t == 0:
            return t
        t -= align
    return dim


def _round_up(n, m):
    return (n + m - 1) // m * m


def _mm(a, b, mode, *, out_dtype, name, a_win=None, b_win=None, add=None, ride=None):
    a0, aw = a_win or (0, a.shape[1])
    b0, bw = b_win or (0, b.shape[1])
    if mode == 'nn':
        m, k, n = a.shape[0], aw, bw
        assert b.shape[0] == k
    elif mode == 'nt':
        m, k, n = a.shape[0], aw, b.shape[0]
        assert bw == k
    else:
        k, m, n = a.shape[0], aw, bw
        assert b.shape[0] == k
    osz = jnp.dtype(out_dtype).itemsize
    tm, tn, tk = 1024, 1152, 2048
    while True:
        bm = _tile(math.gcd(m, a0) if (mode == 'tn' and a0) else m, tm)
        bn = _tile(math.gcd(n, b0) if (mode != 'nt' and b0) else n, tn)
        kk = k
        if mode != 'tn' and a0:
            kk = math.gcd(kk, a0)
        if mode == 'nt' and b0:
            kk = math.gcd(kk, b0)
        bk = _tile(kk, tk)
        need = 2 * (bm * bk * a.dtype.itemsize + bk * bn * b.dtype.itemsize + bm * bn * osz) + bm * bn * 4
        if add is not None:
            need += 2 * bm * bn * add.dtype.itemsize
        if need <= MM_VMEM_BUDGET or (tm <= 256 and tn <= 256 and tk <= 512):
            break
        if tk > 1024:
            tk //= 2
        elif tm >= tn:
            tm //= 2
        else:
            tn //= 2
    nk = k // bk
    assert m % bm == 0 and n % bn == 0 and k % bk == 0, (name, m, n, k, bm, bn, bk)
    if mode == 'nn':
        ao, bo = a0 // bk, b0 // bn
        a_blk, a_map = (bm, bk), lambda i, j, q: (i, q + ao)
        b_blk, b_map = (bk, bn), lambda i, j, q: (q, j + bo)
    elif mode == 'nt':
        ao, bo = a0 // bk, b0 // bk
        a_blk, a_map = (bm, bk), lambda i, j, q: (i, q + ao)
        b_blk, b_map = (bn, bk), lambda i, j, q: (j, q + bo)
    else:
        ao, bo = a0 // bm, b0 // bn
        a_blk, a_map = (bk, bm), lambda i, j, q: (q, i + ao)
        b_blk, b_map = (bk, bn), lambda i, j, q: (q, j + bo)
    return _mm_core(a, b, mode, grid=(m // bm, n // bn, nk), a_blk=a_blk, a_map=a_map, b_blk=b_blk, b_map=b_map,
                    o_blk=(bm, bn), o_map=lambda i, j, q: (i, j), out_shape=(m, n), out_dtype=out_dtype, name=name,
                    add=add, ride=ride)


def _mm_core(a, b, mode, *, grid, a_blk, a_map, b_blk, b_map, o_blk, o_map, out_shape, out_dtype, name,
             add=None, ride=None):
    dims = {'nn': (((1,), (0,)), ((), ())), 'nt': (((1,), (1,)), ((), ())), 'tn': (((0,), (0,)), ((), ()))}[mode]
    nk = grid[-1]
    has_add = add is not None
    nr = ride.n if ride is not None else 0

    def body(*refs):
        a_ref, b_ref = refs[0], refs[1]
        pos = 2
        add_ref = refs[pos] if has_add else None
        pos += int(has_add)
        r_ins = refs[pos:pos + nr]
        o_ref = refs[pos + nr]
        r_outs = refs[pos + nr + 1:pos + 2 * nr + 1]
        acc_ref = refs[pos + 2 * nr + 1]
        r_sems = refs[pos + 2 * nr + 2:]
        ids = [pl.program_id(ax) for ax in range(len(grid))]
        q = ids[-1]
        if nr:
            @pl.when(functools.reduce(lambda u, v: u & v, [i == 0 for i in ids]))
            def _():
                ride.start(r_ins, r_outs, r_sems)

        @pl.when(q == 0)
        def _():
            acc_ref[...] = jnp.zeros_like(acc_ref)

        acc_ref[...] += lax.dot_general(a_ref[...].astype(bf16), b_ref[...].astype(bf16), dims,
                                        preferred_element_type=f32)

        @pl.when(q == nk - 1)
        def _():
            r = acc_ref[...]
            if has_add:
                r = r + add_ref[...].astype(f32)
            o_ref[...] = r.astype(out_dtype)

        if nr:
            @pl.when(functools.reduce(lambda u, v: u & v, [i == g - 1 for i, g in zip(ids, grid)]))
            def _():
                ride.wait(r_ins, r_outs, r_sems)

    in_specs = [pl.BlockSpec(a_blk, a_map), pl.BlockSpec(b_blk, b_map)]
    ops = [a, b]
    if has_add:
        in_specs.append(pl.BlockSpec(o_blk, o_map))
        ops.append(add)
    out_specs = [pl.BlockSpec(o_blk, o_map)]
    out_shapes = [jax.ShapeDtypeStruct(out_shape, out_dtype)]
    scratch = [pltpu.VMEM(o_blk, f32)]
    if nr:
        in_specs += ride.specs
        ops += ride.srcs
        out_specs += ride.specs
        out_shapes += ride.out_shape
        scratch += ride.scratch
    sem = ("arbitrary",) * len(grid) if nr else ("parallel",) * (len(grid) - 1) + ("arbitrary",)
    res = pl.pallas_call(
        body, name=name, grid=grid, in_specs=in_specs, out_specs=out_specs, out_shape=out_shapes,
        scratch_shapes=scratch, compiler_params=_cp(sem),
    )(*ops)
    return (res[0], list(res[1:])) if nr else res[0]


S5_NB = 8
S5_UB = 128
S5_SB = 512


def _s5_bu(proj, u0, bd_c, name):
    t = proj.shape[0]
    bm = _tile(t, 1024)
    ub = u0 // S5_UB
    return _mm_core(proj, bd_c, 'nn', grid=(t // bm, 2 * S5_NB, 1),
                    a_blk=(bm, S5_UB), a_map=lambda i, j, q: (i, ub + j % S5_NB),
                    b_blk=(S5_UB, S5_SB), b_map=lambda i, j, q: (j % S5_NB, j // S5_NB),
                    o_blk=(bm, S5_SB), o_map=lambda i, j, q: (i, j),
                    out_shape=(t, 2 * S5_NB * S5_SB), out_dtype=f32, name=name)


def _s5_out(s, c_c, name):
    t = s.shape[0]
    bm = _tile(t, 1024)
    return _mm_core(s, c_c, 'nn', grid=(t // bm, S5_NB, 2),
                    a_blk=(bm, S5_SB), a_map=lambda i, j, q: (i, j + S5_NB * q),
                    b_blk=(S5_SB, S5_UB), b_map=lambda i, j, q: (j + S5_NB * q, 0),
                    o_blk=(bm, S5_UB), o_map=lambda i, j, q: (i, j),
                    out_shape=(t, S5_NB * S5_UB), out_dtype=f32, name=name)


def _s5_out_dx(dyb, c_c, name):
    t = dyb.shape[0]
    bm = _tile(t, 1024)
    return _mm_core(dyb, c_c, 'nt', grid=(t // bm, 2 * S5_NB, 1),
                    a_blk=(bm, S5_UB), a_map=lambda i, j, q: (i, j % S5_NB),
                    b_blk=(S5_SB, S5_UB), b_map=lambda i, j, q: (j, 0),
                    o_blk=(bm, S5_SB), o_map=lambda i, j, q: (i, j),
                    out_shape=(t, 2 * S5_NB * S5_SB), out_dtype=f32, name=name)


def _s5_out_dw(s, dyb, name):
    t = s.shape[0]
    bk = _tile(t, 2048)
    return _mm_core(s, dyb, 'tn', grid=(2 * S5_NB, t // bk),
                    a_blk=(bk, S5_SB), a_map=lambda j, q: (q, j),
                    b_blk=(bk, S5_UB), b_map=lambda j, q: (q, j % S5_NB),
                    o_blk=(S5_SB, S5_UB), o_map=lambda j, q: (j, 0),
                    out_shape=(2 * S5_NB * S5_SB, S5_UB), out_dtype=f32, name=name)


def _s5_bu_dx(g, bd_c, add, name):
    t = g.shape[0]
    bm = _tile(t, 1024)
    return _mm_core(g, bd_c, 'nt', grid=(t // bm, S5_NB, 2),
                    a_blk=(bm, S5_SB), a_map=lambda i, j, q: (i, j + S5_NB * q),
                    b_blk=(S5_UB, S5_SB), b_map=lambda i, j, q: (j, q),
                    o_blk=(bm, S5_UB), o_map=lambda i, j, q: (i, j),
                    out_shape=(t, S5_NB * S5_UB), out_dtype=bf16, name=name, add=add)


def _s5_bu_dw(proj, u0, g, name):
    t = proj.shape[0]
    bk = _tile(t, 2048)
    ub = u0 // S5_UB
    return _mm_core(proj, g, 'tn', grid=(S5_NB, 2, t // bk),
                    a_blk=(bk, S5_UB), a_map=lambda j, r, q: (q, ub + j),
                    b_blk=(bk, S5_SB), b_map=lambda j, r, q: (q, j + S5_NB * r),
                    o_blk=(S5_UB, S5_SB), o_map=lambda j, r, q: (j, r),
                    out_shape=(S5_NB * S5_UB, 2 * S5_SB), out_dtype=f32, name=name)


def _swiglu(a, b):
    return jax.nn.silu(a) * b


def _ride_parts(refs, n_in, n_out, ride):
    nr = ride.n if ride is not None else 0
    ins = refs[:n_in]
    r_ins = refs[n_in:n_in + nr]
    outs = refs[n_in + nr:n_in + nr + n_out]
    r_outs = refs[n_in + nr + n_out:n_in + 2 * nr + n_out]
    return ins, r_ins, outs, r_outs, refs[n_in + 2 * nr + n_out:]


def _call_with_ride(body_core, grid, in_specs, ops, out_specs, out_shape, name, ride, scratch=(), sem=None):
    nr = ride.n if ride is not None else 0
    n_in, n_out, n_scr = len(in_specs), len(out_specs), len(scratch)

    def body(*refs):
        ins, r_ins, outs, r_outs, rest = _ride_parts(refs, n_in, n_out, ride)
        ids = [pl.program_id(ax) for ax in range(len(grid))]
        if nr:
            @pl.when(functools.reduce(lambda u, v: u & v, [i == 0 for i in ids]))
            def _():
                ride.start(r_ins, r_outs, rest[n_scr:])
        body_core(ins, outs, *rest[:n_scr])
        if nr:
            @pl.when(functools.reduce(lambda u, v: u & v, [i == g - 1 for i, g in zip(ids, grid)]))
            def _():
                ride.wait(r_ins, r_outs, rest[n_scr:])

    res = pl.pallas_call(
        body, name=name, grid=grid, in_specs=in_specs + (ride.specs if nr else []),
        out_specs=out_specs + (ride.specs if nr else []), out_shape=out_shape + (ride.out_shape if nr else []),
        scratch_shapes=list(scratch) + (ride.scratch if nr else []),
        compiler_params=_cp(("arbitrary",) * len(grid) if nr else (sem or ("parallel",) * len(grid))),
    )(*ops, *(ride.srcs if nr else []))
    return list(res[:n_out]), list(res[n_out:])


def _ffn_in(h, w, name, ride=None):
    t, d = h.shape
    ffp = w.shape[1] // 2
    bm, bn = _tile(t, 1024), _tile(ffp, 512)
    nb = ffp // bn
    nn = (((1,), (0,)), ((), ()))

    def core(ins, outs):
        h_ref, wa_ref, wb_ref = ins
        ab_ref, act_ref = outs
        hv = h_ref[...].astype(bf16)
        a = lax.dot_general(hv, wa_ref[...].astype(bf16), nn, preferred_element_type=f32)
        b = lax.dot_general(hv, wb_ref[...].astype(bf16), nn, preferred_element_type=f32)
        ab_ref[0] = a.astype(bf16)
        ab_ref[1] = b.astype(bf16)
        act_ref[...] = _swiglu(a, b).astype(bf16)

    (ab, act), landed = _call_with_ride(
        core, (t // bm, nb),
        [pl.BlockSpec((bm, d), lambda i, j: (i, 0)), pl.BlockSpec((d, bn), lambda i, j: (0, j)),
         pl.BlockSpec((d, bn), lambda i, j: (0, j + nb))], [h, w, w],
        [pl.BlockSpec((2, bm, bn), lambda i, j: (0, i, j)), pl.BlockSpec((bm, bn), lambda i, j: (i, j))],
        [jax.ShapeDtypeStruct((2, t, ffp), bf16), jax.ShapeDtypeStruct((t, ffp), bf16)], name, ride)
    return ab, act, landed


def _ffn_out_dx(df, wo, ab, name, ride=None):
    t, d = df.shape
    ffp = wo.shape[0]
    bm, bn = _tile(t, 1024), _tile(ffp, 512)
    nt = (((1,), (1,)), ((), ()))

    def core(ins, outs):
        df_ref, wo_ref, ab_ref = ins
        dact = lax.dot_general(df_ref[...].astype(bf16), wo_ref[...].astype(bf16), nt, preferred_element_type=f32)
        _, vjp = jax.vjp(_swiglu, ab_ref[0].astype(f32), ab_ref[1].astype(f32))
        da, db = vjp(dact)
        outs[0][0] = da.astype(bf16)
        outs[0][1] = db.astype(bf16)

    (dab,), landed = _call_with_ride(
        core, (t // bm, ffp // bn),
        [pl.BlockSpec((bm, d), lambda i, j: (i, 0)), pl.BlockSpec((bn, d), lambda i, j: (j, 0)),
         pl.BlockSpec((2, bm, bn), lambda i, j: (0, i, j))], [df, wo, ab],
        [pl.BlockSpec((2, bm, bn), lambda i, j: (0, i, j))], [jax.ShapeDtypeStruct((2, t, ffp), bf16)], name, ride)
    return dab, landed


def _ffn_in_dx(dab, w, name, ride=None):
    _, t, ffp = dab.shape
    d = w.shape[0]
    bm, bn, bk = _tile(t, 1024), _tile(d, 1024), _tile(ffp, 1408)
    nkh = ffp // bk
    return _mm_core(dab, w, 'nt', grid=(t // bm, d // bn, 2 * nkh),
                    a_blk=(None, bm, bk), a_map=lambda i, j, q: (q // nkh, i, q % nkh),
                    b_blk=(bn, bk), b_map=lambda i, j, q: (j, q),
                    o_blk=(bm, bn), o_map=lambda i, j, q: (i, j),
                    out_shape=(t, d), out_dtype=bf16, name=name, ride=ride)


def _ffn_in_dw(h, dab, name):
    _, t, ffp = dab.shape
    d = h.shape[1]
    bm, bn, bk = _tile(d, 1024), _tile(ffp, 1408), _tile(t, 2048)
    nbh = ffp // bn
    return _mm_core(h, dab, 'tn', grid=(d // bm, 2 * nbh, t // bk),
                    a_blk=(bk, bm), a_map=lambda i, j, q: (q, i),
                    b_blk=(None, bk, bn), b_map=lambda i, j, q: (j // nbh, q, j % nbh),
                    o_blk=(bm, bn), o_map=lambda i, j, q: (i, j),
                    out_shape=(d, 2 * ffp), out_dtype=bf16, name=name)


def _win(r):
    return r if isinstance(r, tuple) else (r, 0, r.shape[1])


def _row_tile(t, widths):
    per_row = 48 * max(widths)
    tm = 512
    while tm > SUBLANE and tm * per_row > RW_VMEM_BUDGET:
        tm //= 2
    return min(tm, t)


def _row_spec(r, tm):
    arr, c0, w = _win(r)
    assert c0 % w == 0, (c0, w)
    cb = c0 // w
    return pl.BlockSpec((tm, w), lambda i: (i, cb))


def _full_spec(p):
    nd = p.ndim
    return pl.BlockSpec(p.shape, lambda i: (0,) * nd)


def _rw(f, rows, params, outs, *, name, accs=(), tm=None):
    t = _win(rows[0])[0].shape[0]
    tm = tm or _row_tile(t, [_win(r)[2] for r in rows] + [w for w, _ in outs])
    nr, npar, no, na = len(rows), len(params), len(outs), len(accs)

    def body(*refs):
        vals = [r[...] for r in refs[:nr + npar]]
        res = f(*vals)
        res = res if isinstance(res, (tuple, list)) else (res,)
        for o_ref, v in zip(refs[nr + npar:nr + npar + no], res[:no]):
            o_ref[...] = v.astype(o_ref.dtype)
        if na:
            first = pl.program_id(0) == 0
            for a_ref, v in zip(refs[nr + npar + no:], res[no:]):
                @pl.when(first)
                def _(a_ref=a_ref):
                    a_ref[...] = jnp.zeros_like(a_ref)
                a_ref[...] += v

    out_shape = [jax.ShapeDtypeStruct((t, w), d) for w, d in outs] + [jax.ShapeDtypeStruct(s, f32) for s in accs]
    out_specs = [pl.BlockSpec((tm, w), lambda i: (i, 0)) for w, _ in outs] + \
                [pl.BlockSpec(s, lambda i: (0, 0)) for s in accs]
    return pl.pallas_call(
        body, name=name, grid=(t // tm,),
        in_specs=[_row_spec(r, tm) for r in rows] + [_full_spec(p) for p in params],
        out_specs=out_specs, out_shape=out_shape,
        compiler_params=_cp(("arbitrary",)),
    )(*[_win(r)[0] for r in rows], *params)


def _rw_vjp(f, rows, params, cots, *, row_grads, param_grads, name, tm=None):
    t = _win(rows[0])[0].shape[0]
    cot_rows = [c for c in cots if c is not None]
    tm = tm or _row_tile(t, [_win(r)[2] for r in rows] + [_win(c)[2] for c in cot_rows])
    nr, npar, ncot = len(rows), len(params), len(cot_rows)
    d_rows = [i for i, d in enumerate(row_grads) if d is not None]
    d_pars = [i for i, d in enumerate(param_grads) if d]

    def body(*refs):
        rv = [r[...] for r in refs[:nr]]
        pv = [r[...] for r in refs[nr:nr + npar]]
        cv = [r[...] for r in refs[nr + npar:nr + npar + ncot]]
        outs_r = refs[nr + npar + ncot:nr + npar + ncot + len(d_rows)]
        outs_p = refs[nr + npar + ncot + len(d_rows):]

        def g(*diff):
            rr, pp = list(rv), list(pv)
            for i, v in zip(d_rows, diff[:len(d_rows)]):
                rr[i] = v
            for i, v in zip(d_pars, diff[len(d_rows):]):
                pp[i] = v
            res = f(*rr, *pp)
            return tuple(res) if isinstance(res, (tuple, list)) else (res,)

        prim, vjp = jax.vjp(g, *[rv[i] for i in d_rows], *[pv[i] for i in d_pars])
        it = iter(cv)
        cts = tuple(next(it).astype(o.dtype) if c is not None else jnp.zeros_like(o) for o, c in zip(prim, cots))
        grads = vjp(cts)
        for o_ref, v in zip(outs_r, grads[:len(d_rows)]):
            o_ref[...] = v.astype(o_ref.dtype)
        first = pl.program_id(0) == 0
        for o_ref, v in zip(outs_p, grads[len(d_rows):]):
            @pl.when(first)
            def _(o_ref=o_ref):
                o_ref[...] = jnp.zeros_like(o_ref)
            o_ref[...] += v.astype(f32)

    out_shape = [jax.ShapeDtypeStruct((t, _win(rows[i])[2]), row_grads[i]) for i in d_rows] + \
                [jax.ShapeDtypeStruct(params[i].shape, f32) for i in d_pars]
    out_specs = [pl.BlockSpec((tm, _win(rows[i])[2]), lambda i_: (i_, 0)) for i in d_rows] + \
                [_full_spec(params[i]) for i in d_pars]
    return pl.pallas_call(
        body, name=name, grid=(t // tm,),
        in_specs=[_row_spec(r, tm) for r in rows] + [_full_spec(p) for p in params] + [_row_spec(c, tm) for c in cot_rows],
        out_specs=out_specs, out_shape=out_shape,
        compiler_params=_cp(("arbitrary",)),
    )(*[_win(r)[0] for r in rows], *params, *[_win(c)[0] for c in cot_rows])


def _rms(x, g):
    return x * lax.rsqrt(jnp.mean(x * x, axis=-1, keepdims=True) + EPS) * g


def _f_mod(x, nw, sh, sc):
    return (_rms(x, nw) * (1.0 + sc) + sh).astype(bf16)


def _f_mod_keep(x, nw, sh, sc):
    return _f_mod(x, nw, sh, sc), x


def _f_res_mod(coef, x, o, g, nw, sh, sc):
    x1 = x + coef * g * o.astype(f32)
    return x1, _f_mod(x1, nw, sh, sc)


def _times01(x, e):
    hi = x.astype(bf16)
    r1 = x - hi.astype(f32)
    mid = r1.astype(bf16)
    lo = (r1 - mid.astype(f32)).astype(bf16)
    return (jnp.dot(hi, e, preferred_element_type=f32) + jnp.dot(mid, e, preferred_element_type=f32) +
            jnp.dot(lo, e, preferred_element_type=f32))


@jax.custom_vjp
def _spread_heads(x, e, et):
    return _times01(x, e)


_spread_heads.defvjp(lambda x, e, et: (_times01(x, e), (e, et)),
                     lambda res, g: (_times01(g, res[1]), None, None))


def _f_ssd_pre(pre, dtraw, bias, e, et):
    xc = jax.nn.silu(pre)
    dtx = _spread_heads(jax.nn.softplus(dtraw + bias), e, et)
    return xc[:, :SSD_DI], xc[:, SSD_DI:SSD_DI + SSD_G * SSD_N], xc[:, SSD_DI + SSD_G * SSD_N:], dtx


def _f_ssd_post(y, z, nw):
    yz = y * jax.nn.silu(z)
    w = SSD_DI // SSD_G
    parts = []
    for g in range(SSD_G):
        s = yz[:, g * w:(g + 1) * w]
        parts.append(s * lax.rsqrt(jnp.mean(s * s, axis=-1, keepdims=True) + EPS))
    return (jnp.concatenate(parts, axis=1) * nw).astype(bf16)


def _f_s5_post(yb, u, d):
    return jax.nn.gelu(yb + d * u).astype(bf16)


def _f_merge(pa, glu, ga, gb):
    d = pa.shape[1]
    pb = glu[:, :d] * jax.nn.sigmoid(glu[:, d:])
    return (jax.nn.sigmoid(ga) * pa + jax.nn.sigmoid(gb) * pb).astype(bf16)


def _f_final(x2, o, tgt, g, nw):
    x3 = x2 + 0.5 * g * o.astype(f32)
    y = _rms(x3, nw)
    return 0.5 * jnp.mean(jnp.square(y - tgt), axis=-1, keepdims=True)


def _f_final_loss(x2, o, tgt, g, nw):
    rows = _f_final(x2, o, tgt, g, nw)
    return jnp.broadcast_to(jnp.sum(rows, axis=0, keepdims=True), (1, LANE))


def _f_s5_prep(lr, li, ldt, br, bi):
    dt = jnp.exp(ldt)
    lr = jnp.minimum(lr, -1e-4)
    mag = jnp.exp(lr * dt)
    ar = mag * jnp.cos(li * dt)
    ai = mag * jnp.sin(li * dt)
    den = lr * lr + li * li
    nr = ar - 1.0
    kr = (nr * lr + ai * li) / den
    ki = (ai * lr - nr * li) / den
    return ar, ai, kr * br - ki * bi, kr * bi + ki * br


def _shift_down(cur, halo8, j):
    if j == 0:
        return cur
    rolled = pltpu.roll(cur, j, 0)
    row8 = lax.broadcasted_iota(jnp.int32, halo8.shape, 0)
    top = jnp.where(row8 < j, pltpu.roll(halo8, j, 0), rolled[:SUBLANE])
    return jnp.concatenate([top, rolled[SUBLANE:]], axis=0)


def _shift_up(cur, halo8, j):
    if j == 0:
        return cur
    n = cur.shape[0]
    rolled = pltpu.roll(cur, n - j, 0)
    row8 = lax.broadcasted_iota(jnp.int32, halo8.shape, 0)
    bot = jnp.where(row8 >= SUBLANE - j, pltpu.roll(halo8, SUBLANE - j, 0), rolled[n - SUBLANE:])
    return jnp.concatenate([rolled[:n - SUBLANE], bot], axis=0)


def _conv_fwd(proj, c0, w8, b, name):
    t = proj.shape[0]
    cw = 1024
    tm = min(512, t)
    cb0 = c0 // cw
    r8 = tm // SUBLANE

    def body(x_ref, h_ref, w_ref, b_ref, o_ref):
        i = pl.program_id(1)
        x = x_ref[...]
        halo = jnp.where(i > 0, h_ref[...], 0.0)
        acc = b_ref[...] + w_ref[CONV_K - 1:CONV_K, :] * x
        for j in range(1, CONV_K):
            acc = acc + w_ref[CONV_K - 1 - j:CONV_K - j, :] * _shift_down(x, halo, j)
        o_ref[...] = acc

    return pl.pallas_call(
        body, name=name, grid=(CONV_DIM // cw, t // tm),
        in_specs=[pl.BlockSpec((tm, cw), lambda c, i: (i, cb0 + c)),
                  pl.BlockSpec((SUBLANE, cw), lambda c, i: (jnp.maximum(i * r8 - 1, 0), cb0 + c)),
                  pl.BlockSpec((SUBLANE, cw), lambda c, i: (0, c)),
                  pl.BlockSpec((1, cw), lambda c, i: (0, c))],
        out_specs=pl.BlockSpec((tm, cw), lambda c, i: (i, c)),
        out_shape=jax.ShapeDtypeStruct((t, CONV_DIM), f32),
        compiler_params=_cp(("parallel", "arbitrary")),
    )(proj, proj, w8, b)


def _conv_bwd(dpre, proj, c0, w8, name):
    t = proj.shape[0]
    cw = 1024
    tm = min(512, t)
    cb0 = c0 // cw
    r8 = tm // SUBLANE
    nb = t // tm

    def body(d_ref, dn_ref, x_ref, xh_ref, w_ref, dx_ref, dw_ref, db_ref):
        i = pl.program_id(1)
        d = d_ref[...]
        dn = jnp.where(i < nb - 1, dn_ref[...], 0.0)
        x = x_ref[...]
        xh = jnp.where(i > 0, xh_ref[...], 0.0)

        @pl.when(i == 0)
        def _():
            dw_ref[...] = jnp.zeros_like(dw_ref)
            db_ref[...] = jnp.zeros_like(db_ref)

        dx = w_ref[CONV_K - 1:CONV_K, :] * d
        rows = [jnp.sum(d * x, axis=0, keepdims=True)]
        for j in range(1, CONV_K):
            dx = dx + w_ref[CONV_K - 1 - j:CONV_K - j, :] * _shift_up(d, dn, j)
            rows.append(jnp.sum(d * _shift_down(x, xh, j), axis=0, keepdims=True))
        dx_ref[...] = dx.astype(dx_ref.dtype)
        dw = jnp.concatenate([rows[CONV_K - 1 - k] for k in range(CONV_K)] +
                             [jnp.zeros((SUBLANE - CONV_K, cw), f32)], axis=0)
        dw_ref[...] += dw
        db_ref[...] += jnp.sum(d, axis=0, keepdims=True)

    return pl.pallas_call(
        body, name=name, grid=(CONV_DIM // cw, nb),
        in_specs=[pl.BlockSpec((tm, cw), lambda c, i: (i, c)),
                  pl.BlockSpec((SUBLANE, cw), lambda c, i: (jnp.minimum((i + 1) * r8, nb * r8 - 1), c)),
                  pl.BlockSpec((tm, cw), lambda c, i: (i, cb0 + c)),
                  pl.BlockSpec((SUBLANE, cw), lambda c, i: (jnp.maximum(i * r8 - 1, 0), cb0 + c)),
                  pl.BlockSpec((SUBLANE, cw), lambda c, i: (0, c))],
        out_specs=[pl.BlockSpec((tm, cw), lambda c, i: (i, c)),
                   pl.BlockSpec((SUBLANE, cw), lambda c, i: (0, c)),
                   pl.BlockSpec((1, cw), lambda c, i: (0, c))],
        out_shape=[jax.ShapeDtypeStruct((t, CONV_DIM), bf16), jax.ShapeDtypeStruct((SUBLANE, CONV_DIM), f32),
                   jax.ShapeDtypeStruct((1, CONV_DIM), f32)],
        compiler_params=_cp(("parallel", "arbitrary")),
    )(dpre, dpre, proj, proj, w8)


def _cumsum_rows_impl(x):
    n = x.shape[0]
    row = lax.broadcasted_iota(jnp.int32, x.shape, 0)
    s = 1
    while s < n:
        x = x + jnp.where(row >= s, pltpu.roll(x, s, 0), 0.0)
        s *= 2
    return x


@jax.custom_vjp
def _cumsum_rows(x):
    return _cumsum_rows_impl(x)


def _cumsum_rows_bwd(_, g):
    c = _cumsum_rows_impl(g)
    return (c[c.shape[0] - 1:, :] - c + g,)


_cumsum_rows.defvjp(lambda x: (_cumsum_rows_impl(x), None), _cumsum_rows_bwd)


@jax.custom_vjp
def _swap_halves(t):
    return pltpu.roll(t, LANE // 2, 1)


_swap_halves.defvjp(lambda t: (pltpu.roll(t, LANE // 2, 1), None), lambda _, g: (pltpu.roll(g, LANE // 2, 1),))


def _ssd_chunk(xs, bm, cm, dtx, ax, dskx, ht):
    n = SSD_L
    assert n == LANE and SSD_P * 2 == LANE
    row = lax.broadcasted_iota(jnp.int32, (n, n), 0)
    col = lax.broadcasted_iota(jnp.int32, (n, n), 1)
    causal = row >= col
    lo = col < SSD_P
    cs = _cumsum_rows(dtx * ax)
    xdt = xs * dtx
    last = cs[n - 1:n, :]
    cb = lax.dot_general(cm.astype(bf16), bm.astype(bf16), (((1,), (1,)), ((), ())), preferred_element_type=f32)
    y_off = jnp.dot(cm.astype(bf16), ht.astype(bf16), preferred_element_type=f32) * jnp.exp(cs)
    st = lax.dot_general(bm.astype(bf16), (xdt * jnp.exp(last - cs)).astype(bf16), (((0,), (0,)), ((), ())),
                         preferred_element_type=f32)
    ht_new = jnp.exp(last) * ht + st
    ys = []
    for q in range(SSD_R // 2):
        tq = cs[:, q * LANE:(q + 1) * LANE]
        sw = _swap_halves(tq)
        tqt = tq.T
        xq = xdt[:, q * LANE:(q + 1) * LANE].astype(bf16)
        pair = []
        for c_col, r_row in ((jnp.where(lo, tq, sw), tqt[0:1, :]), (jnp.where(lo, sw, tq), tqt[SSD_P:SSD_P + 1, :])):
            decay = jnp.exp(jnp.where(causal, c_col - r_row, -1e30))
            pair.append(jnp.dot((cb * decay).astype(bf16), xq, preferred_element_type=f32))
        ys.append(jnp.where(lo, pair[0], pair[1]))
    return jnp.concatenate(ys, axis=1) + y_off + dskx * xs, ht_new


SSD_GB = 1


def _ssd_specs(nc, rev):
    ch = (lambda c: nc - 1 - c) if rev else (lambda c: c)
    gw = SSD_GB * SSD_R * SSD_P
    return [pl.BlockSpec((SSD_L, gw), lambda g, c: (ch(c), g)),
            pl.BlockSpec((SSD_L, SSD_GB * SSD_N), lambda g, c: (ch(c), g)),
            pl.BlockSpec((SSD_L, SSD_GB * SSD_N), lambda g, c: (ch(c), g)),
            pl.BlockSpec((SSD_L, gw), lambda g, c: (ch(c), g)),
            pl.BlockSpec((1, gw), lambda g, c: (0, g)),
            pl.BlockSpec((1, gw), lambda g, c: (0, g))]


def _ssd_group(refs, q):
    gw = SSD_R * SSD_P
    xs_ref, bm_ref, cm_ref, dt_ref, a_ref, dsk_ref = refs
    ln = slice(q * LANE, (q + 1) * LANE)
    wd = slice(q * gw, (q + 1) * gw)
    return (xs_ref[:, wd], bm_ref[:, ln], cm_ref[:, ln], dt_ref[:, wd], a_ref[:, wd], dsk_ref[:, wd])


def _ssd_fwd(xs, bm, cm, dt4, a4, dsk4, name, ride=None):
    t = xs.shape[0]
    nc = t // SSD_L
    gw = SSD_R * SSD_P

    nr = ride.n if ride is not None else 0
    ng = SSD_G // SSD_GB

    def body(*refs):
        xs_ref, bm_ref, cm_ref, dt_ref, a_ref, dsk_ref = refs[:6]
        r_ins = refs[6:6 + nr]
        y_ref, hs_ref = refs[6 + nr:8 + nr]
        r_outs = refs[8 + nr:8 + 2 * nr]
        h_ref = refs[8 + 2 * nr]
        r_sems = refs[9 + 2 * nr:]
        g, c = pl.program_id(0), pl.program_id(1)
        if nr:
            @pl.when((g == 0) & (c == 0))
            def _():
                ride.start(r_ins, r_outs, r_sems)

        @pl.when(c == 0)
        def _():
            h_ref[...] = jnp.zeros_like(h_ref)

        hs_ref[...] = h_ref[...]
        grp = (xs_ref, bm_ref, cm_ref, dt_ref, a_ref, dsk_ref)
        ops = [_ssd_group(grp, q) + (h_ref[:, q * gw:(q + 1) * gw],) for q in range(SSD_GB)]
        res = [_ssd_chunk(*o) for o in ops]
        for q, (y, hn) in enumerate(res):
            y_ref[:, q * gw:(q + 1) * gw] = y
            h_ref[:, q * gw:(q + 1) * gw] = hn

        if nr:
            @pl.when((g == ng - 1) & (c == nc - 1))
            def _():
                ride.wait(r_ins, r_outs, r_sems)

    res = pl.pallas_call(
        body, name=name, grid=(ng, nc), in_specs=_ssd_specs(nc, False) + (ride.specs if nr else []),
        out_specs=[pl.BlockSpec((SSD_L, SSD_GB * gw), lambda g, c: (c, g)),
                   pl.BlockSpec((None, None, SSD_N, SSD_GB * gw), lambda g, c: (g, c, 0, 0))] +
                  (ride.specs if nr else []),
        out_shape=[jax.ShapeDtypeStruct((t, SSD_DI), f32),
                   jax.ShapeDtypeStruct((ng, nc, SSD_N, SSD_GB * gw), f32)] + (ride.out_shape if nr else []),
        scratch_shapes=[pltpu.VMEM((SSD_N, SSD_GB * gw), f32)] + (ride.scratch if nr else []),
        compiler_params=_cp(("arbitrary", "arbitrary")),
    )(xs, bm, cm, dt4, a4, dsk4, *(ride.srcs if nr else []))
    return res[0], res[1], list(res[2:])


def _ssd_bwd(xs, bm, cm, dt4, a4, dsk4, hs, dy, name, ride=None):
    t = xs.shape[0]
    nc = t // SSD_L
    gw = SSD_R * SSD_P
    rc = lambda c: nc - 1 - c
    nr = ride.n if ride is not None else 0
    ng = SSD_G // SSD_GB

    def body(*refs):
        xs_ref, bm_ref, cm_ref, dt_ref, a_ref, dsk_ref, hs_ref, dy_ref = refs[:8]
        r_ins = refs[8:8 + nr]
        dxs_ref, dbm_ref, dcm_ref, ddt_ref, da_ref, ddsk_ref = refs[8 + nr:14 + nr]
        r_outs = refs[14 + nr:14 + 2 * nr]
        dh_ref = refs[14 + 2 * nr]
        r_sems = refs[15 + 2 * nr:]
        if nr:
            @pl.when((pl.program_id(0) == 0) & (pl.program_id(1) == 0))
            def _():
                ride.start(r_ins, r_outs, r_sems)

        @pl.when(pl.program_id(1) == 0)
        def _():
            dh_ref[...] = jnp.zeros_like(dh_ref)
            da_ref[...] = jnp.zeros_like(da_ref)
            ddsk_ref[...] = jnp.zeros_like(ddsk_ref)

        grp = (xs_ref, bm_ref, cm_ref, dt_ref, a_ref, dsk_ref)
        ops = [_ssd_group(grp, q) + (hs_ref[:, q * gw:(q + 1) * gw],) for q in range(SSD_GB)]
        cts = [(dy_ref[:, q * gw:(q + 1) * gw], dh_ref[:, q * gw:(q + 1) * gw]) for q in range(SSD_GB)]
        grads = [jax.vjp(_ssd_chunk, *o)[1](ct) for o, ct in zip(ops, cts)]
        for q, (dxs, dbm, dcm, ddt, da, ddsk, dh) in enumerate(grads):
            wd = slice(q * gw, (q + 1) * gw)
            ln = slice(q * LANE, (q + 1) * LANE)
            dxs_ref[:, wd] = dxs
            dbm_ref[:, ln] = dbm
            dcm_ref[:, ln] = dcm
            ddt_ref[:, wd] = ddt
            da_ref[:, wd] += da
            ddsk_ref[:, wd] += ddsk
            dh_ref[:, wd] = dh

        if nr:
            @pl.when((pl.program_id(0) == ng - 1) & (pl.program_id(1) == nc - 1))
            def _():
                ride.wait(r_ins, r_outs, r_sems)

    res = pl.pallas_call(
        body, name=name, grid=(ng, nc),
        in_specs=_ssd_specs(nc, True) + [
            pl.BlockSpec((None, None, SSD_N, SSD_GB * gw), lambda g, c: (g, rc(c), 0, 0)),
            pl.BlockSpec((SSD_L, SSD_GB * gw), lambda g, c: (rc(c), g))] + (ride.specs if nr else []),
        out_specs=[pl.BlockSpec((SSD_L, SSD_GB * gw), lambda g, c: (rc(c), g)),
                   pl.BlockSpec((SSD_L, SSD_GB * SSD_N), lambda g, c: (rc(c), g)),
                   pl.BlockSpec((SSD_L, SSD_GB * SSD_N), lambda g, c: (rc(c), g)),
                   pl.BlockSpec((SSD_L, SSD_GB * gw), lambda g, c: (rc(c), g)),
                   pl.BlockSpec((1, SSD_GB * gw), lambda g, c: (0, g)),
                   pl.BlockSpec((1, SSD_GB * gw), lambda g, c: (0, g))] + (ride.specs if nr else []),
        out_shape=[jax.ShapeDtypeStruct((t, SSD_DI), f32), jax.ShapeDtypeStruct((t, SSD_G * SSD_N), f32),
                   jax.ShapeDtypeStruct((t, SSD_G * SSD_N), f32), jax.ShapeDtypeStruct((t, SSD_DI), f32),
                   jax.ShapeDtypeStruct((1, SSD_DI), f32), jax.ShapeDtypeStruct((1, SSD_DI), f32)] +
                  (ride.out_shape if nr else []),
        scratch_shapes=[pltpu.VMEM((SSD_N, SSD_GB * gw), f32)] + (ride.scratch if nr else []),
        compiler_params=_cp(("arbitrary", "arbitrary")),
    )(xs, bm, cm, dt4, a4, dsk4, hs, dy, *(ride.srcs if nr else []))
    return list(res[:6]), list(res[6:])


S5_CH = 1024


def _s5_scan(bu, ar, ai, name):
    t = bu.shape[0]
    tb = min(128, t)

    def body(bu_ref, ar_ref, ai_ref, s_ref, carry):
        @pl.when(pl.program_id(0) == 0)
        def _():
            carry[...] = jnp.zeros_like(carry)

        for c0 in range(0, S5_S, S5_CH):
            re = pl.ds(c0, S5_CH)
            im = pl.ds(S5_S + c0, S5_CH)
            a_r = ar_ref[:, re]
            a_i = ai_ref[:, re]

            def step(k, st, re=re, im=im, a_r=a_r, a_i=a_i):
                sr, si = st
                row = pl.ds(k, 1)
                nr = a_r * sr - a_i * si + bu_ref[row, re]
                ni = a_r * si + a_i * sr + bu_ref[row, im]
                s_ref[row, re] = nr
                s_ref[row, im] = ni
                return nr, ni

            sr, si = lax.fori_loop(0, tb, step, (carry[:, re], carry[:, im]))
            carry[:, re] = sr
            carry[:, im] = si

    return pl.pallas_call(
        body, name=name, grid=(t // tb,),
        in_specs=[pl.BlockSpec((tb, 2 * S5_S), lambda i: (i, 0)),
                  pl.BlockSpec((1, S5_S), lambda i: (0, 0)), pl.BlockSpec((1, S5_S), lambda i: (0, 0))],
        out_specs=pl.BlockSpec((tb, 2 * S5_S), lambda i: (i, 0)),
        out_shape=jax.ShapeDtypeStruct((t, 2 * S5_S), f32),
        scratch_shapes=[pltpu.VMEM((1, 2 * S5_S), f32)],
        compiler_params=_cp(("arbitrary",)),
    )(bu, ar, ai)


def _s5_scan_bwd(ds, s, ar, ai, name, ride=None):
    t = ds.shape[0]
    tb = min(128, t)
    nb = t // tb
    r8 = tb // SUBLANE
    rb = lambda i: nb - 1 - i

    def body(ins, outs, carry):
        ds_ref, s_ref, sh_ref, ar_ref, ai_ref = ins
        g_ref, dar_ref, dai_ref = outs
        i = pl.program_id(0)

        @pl.when(i == 0)
        def _():
            carry[...] = jnp.zeros_like(carry)
            dar_ref[...] = jnp.zeros_like(dar_ref)
            dai_ref[...] = jnp.zeros_like(dai_ref)

        has_prev = (i < nb - 1).astype(f32)
        for c0 in range(0, S5_S, S5_CH):
            re = pl.ds(c0, S5_CH)
            im = pl.ds(S5_S + c0, S5_CH)
            a_r = ar_ref[:, re]
            a_i = ai_ref[:, re]

            def upd(st, row, sp_r, sp_i, re=re, im=im, a_r=a_r, a_i=a_i):
                gr, gi, acr, aci = st
                ngr = ds_ref[row, re] + a_r * gr + a_i * gi
                ngi = ds_ref[row, im] + a_r * gi - a_i * gr
                g_ref[row, re] = ngr
                g_ref[row, im] = ngi
                return ngr, ngi, acr + ngr * sp_r + ngi * sp_i, aci + ngi * sp_r - ngr * sp_i

            def step(k, st, re=re, im=im, upd=upd):
                tt = tb - 1 - k
                prev = pl.ds(tt - 1, 1)
                return upd(st, pl.ds(tt, 1), s_ref[prev, re], s_ref[prev, im])

            zero = jnp.zeros((1, S5_CH), f32)
            st = lax.fori_loop(0, tb - 1, step, (carry[:, re], carry[:, im], zero, zero))
            last = pl.ds(SUBLANE - 1, 1)
            gr, gi, acr, aci = upd(st, pl.ds(0, 1), sh_ref[last, re] * has_prev, sh_ref[last, im] * has_prev)
            carry[:, re] = gr
            carry[:, im] = gi
            dar_ref[:, re] += acr
            dai_ref[:, re] += aci

    return _call_with_ride(
        body, (nb,),
        [pl.BlockSpec((tb, 2 * S5_S), lambda i: (rb(i), 0)),
         pl.BlockSpec((tb, 2 * S5_S), lambda i: (rb(i), 0)),
         pl.BlockSpec((SUBLANE, 2 * S5_S), lambda i: (jnp.maximum(rb(i) * r8 - 1, 0), 0)),
         pl.BlockSpec((1, S5_S), lambda i: (0, 0)), pl.BlockSpec((1, S5_S), lambda i: (0, 0))],
        [ds, s, s, ar, ai],
        [pl.BlockSpec((tb, 2 * S5_S), lambda i: (rb(i), 0)),
         pl.BlockSpec((1, S5_S), lambda i: (0, 0)), pl.BlockSpec((1, S5_S), lambda i: (0, 0))],
        [jax.ShapeDtypeStruct((t, 2 * S5_S), f32), jax.ShapeDtypeStruct((1, S5_S), f32),
         jax.ShapeDtypeStruct((1, S5_S), f32)],
        name, ride, scratch=[pltpu.VMEM((1, 2 * S5_S), f32)], sem=("arbitrary",))


def _ada_fwd(c_all, w, b, name):
    d, n = w.shape
    tn = _tile(n, 1536)

    def body(c_ref, w_ref, b_ref, o_ref):
        a = jax.nn.silu(c_ref[...]).astype(bf16)
        o_ref[...] = jnp.dot(a, w_ref[...].astype(bf16), preferred_element_type=f32) + b_ref[...]

    return pl.pallas_call(
        body, name=name, grid=(n // tn,),
        in_specs=[pl.BlockSpec(c_all.shape, lambda j: (0, 0)), pl.BlockSpec((d, tn), lambda j: (0, j)),
                  pl.BlockSpec((1, tn), lambda j: (0, j))],
        out_specs=pl.BlockSpec((c_all.shape[0], tn), lambda j: (0, j)),
        out_shape=jax.ShapeDtypeStruct((c_all.shape[0], n), f32),
        compiler_params=_cp(("parallel",)),
    )(c_all, w, b)


def _ada_bwd(c_all, dm, name):
    d = c_all.shape[1]
    n = dm.shape[1]
    tn = _tile(n, 1536)

    def body(c_ref, dm_ref, o_ref):
        a = jax.nn.silu(c_ref[...]).astype(bf16)
        o_ref[...] = lax.dot_general(a, dm_ref[...].astype(bf16), (((0,), (0,)), ((), ())), preferred_element_type=f32)

    return pl.pallas_call(
        body, name=name, grid=(n // tn,),
        in_specs=[pl.BlockSpec(c_all.shape, lambda j: (0, 0)), pl.BlockSpec((dm.shape[0], tn), lambda j: (0, j))],
        out_specs=pl.BlockSpec((d, tn), lambda j: (0, j)),
        out_shape=jax.ShapeDtypeStruct((d, n), f32),
        compiler_params=_cp(("parallel",)),
    )(c_all, dm)


def _blk_rows(r, c, nbuf, itemsize=4):
    tr = _tile(r, max(SUBLANE, (RW_VMEM_BUDGET // (2 * nbuf * c * itemsize)) // 16 * 16), 16)
    return tr if r % tr == 0 else r


def _cast_bf16(w, name):
    r, c = w.shape
    tr = _blk_rows(r, c, 2)

    def body(w_ref, o_ref):
        o_ref[...] = w_ref[...].astype(bf16)

    return pl.pallas_call(
        body, name=name, grid=(r // tr,), in_specs=[pl.BlockSpec((tr, c), lambda i: (i, 0))],
        out_specs=pl.BlockSpec((tr, c), lambda i: (i, 0)), out_shape=jax.ShapeDtypeStruct((r, c), bf16),
        compiler_params=_cp(("parallel",)),
    )(w)


def _sum_lead(parts, name):
    n, r, c = parts.shape
    tr = _blk_rows(r, c, n + 2)

    def body(p_ref, o_ref):
        acc = p_ref[0].astype(f32)
        for q in range(1, n):
            acc = acc + p_ref[q].astype(f32)
        o_ref[...] = acc

    return pl.pallas_call(
        body, name=name, grid=(r // tr,), in_specs=[pl.BlockSpec((n, tr, c), lambda i: (0, i, 0))],
        out_specs=pl.BlockSpec((tr, c), lambda i: (i, 0)), out_shape=jax.ShapeDtypeStruct((r, c), f32),
        compiler_params=_cp(("parallel",)),
    )(parts)


def _adamw(w, m, v, parts, name):
    r, c = w.shape
    npart = len(parts)
    tr = _blk_rows(r, c, 7 + npart)
    c1 = 1.0 - ADAM_B1 ** ADAM_STEP
    c2 = 1.0 - ADAM_B2 ** ADAM_STEP

    def body(*refs):
        w_ref, m_ref, v_ref = refs[:3]
        g_ref, d_ref, nm_ref, nv_ref = refs[3 + npart:]
        g = refs[3][...].astype(f32)
        for p in refs[4:3 + npart]:
            g = g + p[...].astype(f32)
        nm = ADAM_B1 * m_ref[...] + (1.0 - ADAM_B1) * g
        nv = ADAM_B2 * v_ref[...] + (1.0 - ADAM_B2) * jnp.square(g)
        g_ref[...] = g
        nm_ref[...] = nm
        nv_ref[...] = nv
        d_ref[...] = -ADAM_LR * ((nm / c1) / (jnp.sqrt(nv / c2) + ADAM_EPS) + ADAM_WD * w_ref[...])

    spec = pl.BlockSpec((tr, c), lambda i: (i, 0))
    return pl.pallas_call(
        body, name=name, grid=(r // tr,), in_specs=[spec] * (3 + npart), out_specs=[spec] * 4,
        out_shape=[jax.ShapeDtypeStruct((r, c), f32)] * 4, compiler_params=_cp(("parallel",)),
    )(w, m, v, *parts)


def _ag_small(x_shard, name):
    m_per, n = x_shard.shape

    def body(x_ref, out_ref, send_sems, recv_sems, local_sem):
        x, y, c = lax.axis_index("x"), lax.axis_index("y"), lax.axis_index("c")
        me, sibling = (x, y, c), (x, y, 1 - c)
        chips = [(1 - x, y), (x, 1 - y), (1 - x, 1 - y)]

        def rows(px, py, pc):
            return out_ref.at[pl.ds((4 * px + 2 * py + pc) * m_per, m_per), :]

        def copy(k, block, to, src=None):
            return pltpu.make_async_remote_copy(
                src_ref=rows(*block) if src is None else src, dst_ref=rows(*block),
                send_sem=send_sems.at[k], recv_sem=recv_sems.at[k], device_id=to, device_id_type=MESH)

        mine = pltpu.make_async_copy(x_ref, rows(*me), local_sem)
        mine.start()
        first = [copy(0, me, sibling, src=x_ref)]
        first += [copy(1 + j, me, (*chip, c), src=x_ref) for j, chip in enumerate(chips)]
        for cp in first:
            cp.start()
        passed = [copy(4 + j, (*chip, c), sibling) for j, chip in enumerate(chips)]
        for j, chip in enumerate(chips):
            copy(1 + j, (*chip, c), me).wait_recv()
            passed[j].start()
        copy(0, sibling, me).wait_recv()
        for j, chip in enumerate(chips):
            copy(4 + j, (*chip, 1 - c), me).wait_recv()
        for cp in first + passed:
            cp.wait_send()
        mine.wait()

    return pl.pallas_call(
        body, name=name, out_shape=jax.ShapeDtypeStruct((8 * m_per, n), x_shard.dtype),
        in_specs=[pl.BlockSpec(memory_space=pltpu.VMEM)], out_specs=pl.BlockSpec(memory_space=pltpu.VMEM),
        scratch_shapes=[pltpu.SemaphoreType.DMA((7,)), pltpu.SemaphoreType.DMA((7,)), pltpu.SemaphoreType.DMA],
        compiler_params=pltpu.CompilerParams(vmem_limit_bytes=VMEM_LIMIT),
    )(x_shard)


def _xchg(srcs, scatter, name):
    return _run_ride(_Ride(srcs, scatter), name)


def _run_ride(ride, name):
    n = ride.n

    def body(*refs):
        ride.start(refs[:n], refs[n:2 * n], refs[2 * n:])
        ride.wait(refs[:n], refs[n:2 * n], refs[2 * n:])

    return pl.pallas_call(
        body, name=name, out_shape=ride.out_shape, in_specs=ride.specs, out_specs=ride.specs,
        scratch_shapes=ride.scratch,
    )(*ride.srcs)


class _Ride:
    def __init__(self, srcs, scatter):
        self.srcs, self.scatter, self.n = list(srcs), scatter, len(srcs)
        n = self.n
        self.out_shape = [jax.ShapeDtypeStruct(s.shape if scatter else (4,) + s.shape, s.dtype) for s in srcs]
        self.specs = [pl.BlockSpec(memory_space=pl.ANY)] * n
        self.scratch = [pltpu.SemaphoreType.DMA((3 * n,)), pltpu.SemaphoreType.DMA((3 * n,)),
                        pltpu.SemaphoreType.DMA((n,))]

    def _copies(self, ins, outs, sems):
        send_sems, recv_sems, local_sems = sems
        x, y, c = lax.axis_index("x"), lax.axis_index("y"), lax.axis_index("c")
        my_k = 2 * x + y
        peers = [(1 - x, y), (x, 1 - y), (1 - x, 1 - y)]
        local, sends, recvs = [], [], []
        for a in range(self.n):
            own = ins[a].at[my_k] if self.scatter else ins[a]
            local.append(pltpu.make_async_copy(own, outs[a].at[my_k], local_sems.at[a]))
            for j, (px, py) in enumerate(peers):
                sems_j = dict(send_sem=send_sems.at[3 * a + j], recv_sem=recv_sems.at[3 * a + j],
                              device_id=(px, py, c), device_id_type=MESH)
                src = ins[a].at[2 * px + py] if self.scatter else ins[a]
                sends.append(pltpu.make_async_remote_copy(src_ref=src, dst_ref=outs[a].at[my_k], **sems_j))
                landed = outs[a].at[2 * px + py]
                recvs.append(pltpu.make_async_remote_copy(src_ref=landed, dst_ref=landed, **sems_j))
        return local, sends, recvs

    def start(self, ins, outs, sems):
        local, sends, _ = self._copies(ins, outs, sems)
        for cp in local + sends:
            cp.start()

    def wait(self, ins, outs, sems):
        local, sends, recvs = self._copies(ins, outs, sems)
        for cp in recvs:
            cp.wait_recv()
        for cp in sends:
            cp.wait_send()
        for cp in local:
            cp.wait()


class _RideGather:
    def __init__(self, srcs):
        self.srcs, self.n = list(srcs), len(srcs)
        n = self.n
        assert all(s.shape[0] % 32 == 0 for s in srcs)
        self.out_shape = [jax.ShapeDtypeStruct((4,) + s.shape, s.dtype) for s in srcs]
        self.specs = [pl.BlockSpec(memory_space=pl.ANY)] * n
        dma = pltpu.SemaphoreType.DMA
        self.scratch = [dma((3 * n,)), dma((3 * n,)), dma((3 * n,)), dma((3 * n,)), dma((n,))]

    def _copies(self, ins, outs, sems):
        send_sems, recv_sems, pass_send, pass_recv, local_sems = sems
        x, y, c = lax.axis_index("x"), lax.axis_index("y"), lax.axis_index("c")
        my_k = 2 * x + y
        peers = [(1 - x, y), (x, 1 - y), (1 - x, 1 - y)]
        local, sends, recvs, passes, pass_recvs = [], [], [], [], []
        for a in range(self.n):
            half = self.srcs[a].shape[0] // 2
            mine = pl.ds(pl.multiple_of(c * half, 16), half)
            other = pl.ds(pl.multiple_of((1 - c) * half, 16), half)
            local.append(pltpu.make_async_copy(ins[a], outs[a].at[my_k], local_sems.at[a]))
            for j, (px, py) in enumerate(peers):
                q = 3 * a + j
                over_ici = dict(send_sem=send_sems.at[q], recv_sem=recv_sems.at[q], device_id=(px, py, c),
                                device_id_type=MESH)
                to_sibling = dict(send_sem=pass_send.at[q], recv_sem=pass_recv.at[q], device_id=(x, y, 1 - c),
                                  device_id_type=MESH)
                sends.append(pltpu.make_async_remote_copy(src_ref=ins[a].at[mine], dst_ref=outs[a].at[my_k, mine],
                                                          **over_ici))
                landed = outs[a].at[2 * px + py, mine]
                recvs.append(pltpu.make_async_remote_copy(src_ref=landed, dst_ref=landed, **over_ici))
                passes.append(pltpu.make_async_remote_copy(src_ref=landed, dst_ref=landed, **to_sibling))
                from_sibling = outs[a].at[2 * px + py, other]
                pass_recvs.append(pltpu.make_async_remote_copy(src_ref=from_sibling, dst_ref=from_sibling, **to_sibling))
        return local, sends, recvs, passes, pass_recvs

    def start(self, ins, outs, sems):
        local, sends = self._copies(ins, outs, sems)[:2]
        for cp in local + sends:
            cp.start()

    def wait(self, ins, outs, sems):
        local, sends, recvs, passes, pass_recvs = self._copies(ins, outs, sems)
        for rc, ps in zip(recvs, passes):
            rc.wait_recv()
            ps.start()
        for cp in pass_recvs:
            cp.wait_recv()
        for cp in sends + passes:
            cp.wait_send()
        for cp in local:
            cp.wait()


def _swap_sibling(srcs, name):
    n = len(srcs)

    def body(*refs):
        ins, outs = refs[:n], refs[n:2 * n]
        send_sems, recv_sems = refs[2 * n:]
        sib = (lax.axis_index("x"), lax.axis_index("y"), 1 - lax.axis_index("c"))
        cps = [pltpu.make_async_remote_copy(src_ref=ins[a], dst_ref=outs[a], send_sem=send_sems.at[a],
                                            recv_sem=recv_sems.at[a], device_id=sib, device_id_type=MESH)
               for a in range(n)]
        for cp in cps:
            cp.start()
        for cp in cps:
            cp.wait_recv()
        for cp in cps:
            cp.wait_send()

    anyspec = pl.BlockSpec(memory_space=pl.ANY)
    return pl.pallas_call(
        body, name=name, out_shape=[jax.ShapeDtypeStruct(s.shape, s.dtype) for s in srcs],
        in_specs=[anyspec] * n, out_specs=[anyspec] * n,
        scratch_shapes=[pltpu.SemaphoreType.DMA((n,)), pltpu.SemaphoreType.DMA((n,))],
    )(*srcs)


def _gather8(vec, name):
    size = vec.shape[0]
    n = _round_up(size, SUBLANE * LANE)
    blk = jnp.concatenate([vec, jnp.zeros((n - size,), f32)]).reshape(SUBLANE, n // SUBLANE)
    out = _ag_small(blk, name)
    return out.reshape(8, n)[:, :size]


def _head_spread_matrices():
    e = np.zeros((LANE, SSD_DI), np.float32)
    for h in range(SSD_HEADS):
        e[h, h * SSD_P:(h + 1) * SSD_P] = 1.0
    return jnp.asarray(e, bf16), jnp.asarray(e.T, bf16)


def _heads_to_lanes(v):
    return jnp.repeat(v, SSD_P).reshape(1, SSD_DI)


def _block_diag8(blocks):
    g, r, c = blocks.shape
    b = blocks.reshape(g // S5_NB, S5_NB, r, c)
    eye = jnp.eye(S5_NB, dtype=bool)[None, :, None, :, None]
    return jnp.where(eye, b[:, :, :, None, :], jnp.zeros((), blocks.dtype)).reshape(g * r, S5_NB * c)


def _diag8(mat, r, c):
    g = mat.shape[0] // r
    m = mat.reshape(g // S5_NB, S5_NB, r, S5_NB, c)
    eye = jnp.eye(S5_NB, dtype=bool)[None, :, None, :, None]
    return jnp.where(eye, m, 0.0).sum(axis=3).reshape(g, r, c)


class _Layout:
    def __init__(self, d):
        self.d = d
        self.z, self.xbc, self.u = 0, SSD_DI, SSD_DI + CONV_DIM
        self.ga = self.u + S5_W
        self.gb = self.ga + d
        self.dt = self.gb + d
        self.np_ = self.dt + LANE
        self.in_cols = SSD_DI + CONV_DIM + SSD_HEADS + S5_W + 2 * d
        off_dt = SSD_DI + CONV_DIM
        off_u = off_dt + SSD_HEADS
        off_g = off_u + S5_W
        self.src = [(0, off_dt), (off_u, off_u + S5_W + 2 * d), (off_dt, off_u)]

    def arrange(self, w):
        (a0, a1), (b0, b1), (c0, c1) = self.src
        pad = jnp.zeros((w.shape[0], LANE - SSD_HEADS), w.dtype)
        return jnp.concatenate([w[:, a0:a1], w[:, b0:b1], w[:, c0:c1], pad], axis=1)

    def arrange_slabs(self, g):
        pieces = [p for lo, hi in self.src for p in _cols_from_slabs(g, lo, hi)]
        pieces.append(jnp.zeros((g.shape[1], LANE - SSD_HEADS), g.dtype))
        return jnp.concatenate(pieces, axis=1)

    def restore_slabs(self, w):
        (a0, a1), (b0, b1), (c0, c1) = self.src
        n_a, n_b = a1 - a0, b1 - b0
        segs = [(a0, a1, 0), (c0, c1, n_a + n_b), (b0, b1, n_a)]
        cs = self.in_cols // 4
        slabs = []
        for k in range(4):
            lo, hi = k * cs, (k + 1) * cs
            parts = [w[:, pos + max(lo, s0) - s0:pos + min(hi, s1) - s0] for s0, s1, pos in segs
                     if max(lo, s0) < min(hi, s1)]
            slabs.append(jnp.concatenate(parts, axis=1))
        return jnp.stack(slabs)


def _cols_from_slabs(g, start, stop):
    c = g.shape[2]
    return [g[k][:, max(start, k * c) - k * c:min(stop, (k + 1) * c) - k * c] for k in range(4)
            if max(start, k * c) < min(stop, (k + 1) * c)]


def _unshard_cols(g):
    return jnp.concatenate([g[k] for k in range(4)], axis=1)


def _shard_cols(w):
    r, c4 = w.shape
    return w.reshape(r, 4, c4 // 4).transpose(1, 0, 2)


def kernel(x, c, w_ada, b_ada, norm_ffn1, w_ffn1_in, w_ffn1_out, norm_mix, w_in, conv_w, conv_b, dt_bias, a_log, d_ssd, ssd_norm_w, w_a_proj, s5_lambda_re, s5_lambda_im, s5_b_re, s5_b_im, s5_c_re, s5_c_im, s5_d, s5_log_dt, w_b_glu, w_out, norm_ffn2, w_ffn2_in, w_ffn2_out, norm_final, loss_target, m_w_ada, m_b_ada, m_norm_ffn1, m_w_ffn1_in, m_w_ffn1_out, m_norm_mix, m_w_in, m_conv_w, m_conv_b, m_dt_bias, m_a_log, m_d_ssd, m_ssd_norm_w, m_w_a_proj, m_s5_lambda_re, m_s5_lambda_im, m_s5_b_re, m_s5_b_im, m_s5_c_re, m_s5_c_im, m_s5_d, m_s5_log_dt, m_w_b_glu, m_w_out, m_norm_ffn2, m_w_ffn2_in, m_w_ffn2_out, m_norm_final, v_w_ada, v_b_ada, v_norm_ffn1, v_w_ffn1_in, v_w_ffn1_out, v_norm_mix, v_w_in, v_conv_w, v_conv_b, v_dt_bias, v_a_log, v_d_ssd, v_ssd_norm_w, v_w_a_proj, v_s5_lambda_re, v_s5_lambda_im, v_s5_b_re, v_s5_b_im, v_s5_c_re, v_s5_c_im, v_s5_d, v_s5_log_dt, v_w_b_glu, v_w_out, v_norm_ffn2, v_w_ffn2_in, v_w_ffn2_out, v_norm_final):
    W = dict(w_ada=w_ada, b_ada=b_ada, norm_ffn1=norm_ffn1, w_ffn1_in=w_ffn1_in, w_ffn1_out=w_ffn1_out, norm_mix=norm_mix, w_in=w_in, conv_w=conv_w, conv_b=conv_b, dt_bias=dt_bias, a_log=a_log, d_ssd=d_ssd, ssd_norm_w=ssd_norm_w, w_a_proj=w_a_proj, s5_lambda_re=s5_lambda_re, s5_lambda_im=s5_lambda_im, s5_b_re=s5_b_re, s5_b_im=s5_b_im, s5_c_re=s5_c_re, s5_c_im=s5_c_im, s5_d=s5_d, s5_log_dt=s5_log_dt, w_b_glu=w_b_glu, w_out=w_out, norm_ffn2=norm_ffn2, w_ffn2_in=w_ffn2_in, w_ffn2_out=w_ffn2_out, norm_final=norm_final)
    Mo = dict(w_ada=m_w_ada, b_ada=m_b_ada, norm_ffn1=m_norm_ffn1, w_ffn1_in=m_w_ffn1_in, w_ffn1_out=m_w_ffn1_out, norm_mix=m_norm_mix, w_in=m_w_in, conv_w=m_conv_w, conv_b=m_conv_b, dt_bias=m_dt_bias, a_log=m_a_log, d_ssd=m_d_ssd, ssd_norm_w=m_ssd_norm_w, w_a_proj=m_w_a_proj, s5_lambda_re=m_s5_lambda_re, s5_lambda_im=m_s5_lambda_im, s5_b_re=m_s5_b_re, s5_b_im=m_s5_b_im, s5_c_re=m_s5_c_re, s5_c_im=m_s5_c_im, s5_d=m_s5_d, s5_log_dt=m_s5_log_dt, w_b_glu=m_w_b_glu, w_out=m_w_out, norm_ffn2=m_norm_ffn2, w_ffn2_in=m_w_ffn2_in, w_ffn2_out=m_w_ffn2_out, norm_final=m_norm_final)
    Vo = dict(w_ada=v_w_ada, b_ada=v_b_ada, norm_ffn1=v_norm_ffn1, w_ffn1_in=v_w_ffn1_in, w_ffn1_out=v_w_ffn1_out, norm_mix=v_norm_mix, w_in=v_w_in, conv_w=v_conv_w, conv_b=v_conv_b, dt_bias=v_dt_bias, a_log=v_a_log, d_ssd=v_d_ssd, ssd_norm_w=v_ssd_norm_w, w_a_proj=v_w_a_proj, s5_lambda_re=v_s5_lambda_re, s5_lambda_im=v_s5_lambda_im, s5_b_re=v_s5_b_re, s5_b_im=v_s5_b_im, s5_c_re=v_s5_c_re, s5_c_im=v_s5_c_im, s5_d=v_s5_d, s5_log_dt=v_s5_log_dt, w_b_glu=v_w_b_glu, w_out=v_w_out, norm_ffn2=v_norm_ffn2, w_ffn2_in=v_w_ffn2_in, w_ffn2_out=v_w_ffn2_out, norm_final=v_norm_final)

    t, d = x.shape[1], x.shape[2]
    ff = 4 * w_ffn1_out.shape[1]
    ffp = _round_up(ff, 512)
    lay = _Layout(d)
    xi, yi, ci = lax.axis_index("x"), lax.axis_index("y"), lax.axis_index("c")
    k_me = 2 * xi + yi
    e_me = 4 * xi + 2 * yi + ci
    x2d = x[0]
    tgt = loss_target[0]

    cw_cols = conv_w.shape[2]
    g1 = _gather8(jnp.concatenate([c[0], conv_w[0].reshape(-1)]), "gather_c_convw")
    c_all = g1[:, :d]
    conv_full = g1[::2, d:].reshape(4, CONV_K, cw_cols).transpose(1, 0, 2).reshape(CONV_K, CONV_DIM)
    conv_w8 = jnp.zeros((SUBLANE, CONV_DIM), f32).at[:CONV_K].set(conv_full)

    n_ada_loc = w_ada.shape[2]
    b_loc = lax.dynamic_slice(b_ada, (0, k_me * n_ada_loc), (1, n_ada_loc))
    mods_part = _ada_fwd(c_all, w_ada[0], b_loc, "ada_fwd")
    g2 = _gather8(mods_part.reshape(-1), "gather_mods").reshape(8, 8, n_ada_loc)
    mods = lax.dynamic_index_in_dim(g2[::2], e_me, axis=1, keepdims=False).reshape(N_ADA, d)
    sh1, sc1, gt1, sh2, sc2, gt2, sh3, sc3, gt3 = [mods[i:i + 1] for i in range(N_ADA)]

    cast = {n: _cast_bf16(W[n][0], "cast_" + n) for n in BIG}

    def gather_of(names):
        return _RideGather([cast[n] for n in names])

    def rows_of(g):
        return g.reshape(4 * g.shape[1], g.shape[2])

    def ffn_in(g):
        z = jnp.zeros((g.shape[1], ffp - ff), g.dtype)
        return jnp.concatenate([g[0], g[1], z, g[2], g[3], z], axis=1)

    def ffn_out(g):
        return jnp.concatenate([rows_of(g), jnp.zeros((ffp - ff, g.shape[2]), g.dtype)], axis=0)

    nf1, nmx, nf2 = norm_ffn1, norm_mix, norm_ffn2
    nfin = norm_final.reshape(1, d)

    (g_w1i,) = _run_ride(gather_of(['w_ffn1_in']), "gather_w_ffn1_in")
    w1i = ffn_in(g_w1i)
    (h1,) = _rw(_f_mod, [x2d], [nf1, sh1, sc1], [(d, bf16)], name="mod1")
    ab1, act1, (g_w1o, g_wa, g_wglu, g_wo) = _ffn_in(
        h1, w1i, "ffn1_in", ride=gather_of(['w_ffn1_out', 'w_a_proj', 'w_b_glu', 'w_out']))
    w1o = ffn_out(g_w1o)
    w_a = rows_of(g_wa)
    w_glu, w_o = _unshard_cols(g_wglu), rows_of(g_wo)
    f1, (g_win,) = _mm(act1, w1o, 'nn', out_dtype=bf16, name="ffn1_out", ride=gather_of(['w_in']))
    w_inr = lay.arrange_slabs(g_win)
    res1 = functools.partial(_f_res_mod, 0.5)
    x1, h2 = _rw(res1, [x2d, f1], [gt1, nmx, sh2, sc2], [(d, f32), (d, bf16)], name="res1_mod2")
    proj, (g_w2i,) = _mm(h2, w_inr, 'nn', out_dtype=f32, name="in_proj", ride=gather_of(['w_ffn2_in']))
    w2i = ffn_in(g_w2i)

    pre = _conv_fwd(proj, lay.xbc, conv_w8, conv_b, "conv_fwd")
    spread, spread_t = _head_spread_matrices()
    bias128 = jnp.zeros((1, LANE), f32).at[:, :SSD_HEADS].set(dt_bias)
    xs, bm, cm, dt4 = _rw(_f_ssd_pre, [pre, (proj, lay.dt, LANE)], [bias128, spread, spread_t],
                          [(SSD_DI, f32), (SSD_G * SSD_N, f32), (SSD_G * SSD_N, f32), (SSD_DI, f32)],
                          name="ssd_pre")

    def head_params(a_log_, d_ssd_):
        return _heads_to_lanes(-jnp.exp(a_log_[0])), _heads_to_lanes(d_ssd_[0])

    (a4, dsk4), head_vjp = jax.vjp(head_params, a_log, d_ssd)
    y_ssd, hs, (g_w2o,) = _ssd_fwd(xs, bm, cm, dt4, a4, dsk4, "ssd_fwd", ride=gather_of(['w_ffn2_out']))
    w2o = ffn_out(g_w2o)
    (y_a,) = _rw(_f_ssd_post, [y_ssd, (proj, lay.z, SSD_DI)], [ssd_norm_w], [(SSD_DI, bf16)], name="ssd_post")
    p_a = _mm(y_a, w_a, 'nn', out_dtype=f32, name="a_proj")

    col = lambda v: v.reshape(S5_S, 1)
    ldt_col = jnp.repeat(s5_log_dt[0], S5_P).reshape(S5_S, 1)
    prep_rows = [col(s5_lambda_re[0]), col(s5_lambda_im[0]), ldt_col,
                 s5_b_re[0].reshape(S5_S, S5_I), s5_b_im[0].reshape(S5_S, S5_I)]
    ar, ai, bbr, bbi = _rw(_f_s5_prep, prep_rows, [], [(1, f32), (1, f32), (S5_I, f32), (S5_I, f32)],
                           name="s5_prep", tm=512)
    to_bd = lambda bb: _block_diag8(bb.reshape(S5_G, S5_P, S5_I).transpose(0, 2, 1).astype(bf16))
    bd_c = jnp.concatenate([to_bd(bbr), to_bd(bbi)], axis=1)
    c_c = jnp.concatenate([_block_diag8(s5_c_re[0].transpose(0, 2, 1).astype(bf16)),
                           _block_diag8((-s5_c_im[0]).transpose(0, 2, 1).astype(bf16))], axis=0)
    ar_row, ai_row = ar.reshape(1, S5_S), ai.reshape(1, S5_S)
    bu = _s5_bu(proj, lay.u, bd_c, "s5_bu")
    s5s = _s5_scan(bu, ar_row, ai_row, "s5_scan")
    yb = _s5_out(s5s, c_c, "s5_out")
    d_row = s5_d[0].reshape(1, S5_W)
    (gl,) = _rw(_f_s5_post, [yb, (proj, lay.u, S5_W)], [d_row], [(S5_W, bf16)], name="s5_post")
    glu = _mm(gl, w_glu, 'nn', out_dtype=f32, name="glu_proj")

    merge_rows = [p_a, glu, (proj, lay.ga, d), (proj, lay.gb, d)]
    (merged,) = _rw(_f_merge, merge_rows, [], [(d, bf16)], name="merge")
    o_mix = _mm(merged, w_o, 'nn', out_dtype=bf16, name="out_proj")
    res2 = functools.partial(_f_res_mod, 1.0)
    x2, h3 = _rw(res2, [x1, o_mix], [gt2, nf2, sh3, sc3], [(d, f32), (d, bf16)], name="res2_mod3")
    ab2, act2, _ = _ffn_in(h3, w2i, "ffn2_in")
    f2 = _mm(act2, w2o, 'nn', out_dtype=bf16, name="ffn2_out")
    (loss_acc,) = _rw(_f_final_loss, [x2, f2, tgt], [gt3, nfin], [], accs=[(1, LANE)], name="loss")
    loss = lax.psum(loss_acc[0, 0], AXES)

    ones = jnp.ones((t, 1), f32)
    dx2, df2, dgt3, dnfin = _rw_vjp(_f_final, [x2, f2, tgt], [gt3, nfin], [ones],
                                    row_grads=[f32, bf16, None], param_grads=[True, True], name="loss_bwd")
    def ffn_in_back(g):
        hf = ff // 2
        return jnp.stack([g[:, :hf], g[:, hf:ff], g[:, ffp:ffp + hf], g[:, ffp + hf:ffp + ff]])

    def rows_back(g, rows):
        return g[:rows].reshape(4, rows // 4, g.shape[1])

    def scatter_of(pairs):
        return _Ride([g for _, g in pairs], True)

    terms = {}
    dab2, _ = _ffn_out_dx(df2, w2o, ab2, "ffn2_out_dx")
    dw2o = _mm(act2, df2, 'tn', out_dtype=bf16, name="ffn2_out_dw")
    dh3, (terms['w_ffn2_out'],) = _ffn_in_dx(dab2, w2i, "ffn2_in_dx",
                                             ride=scatter_of([('w_ffn2_out', rows_back(dw2o, ff))]))
    dw2i = _ffn_in_dw(h3, dab2, "ffn2_in_dw")
    dx1, do_mix, dgt2, dnf2, dsh3, dsc3 = _rw_vjp(
        res2, [x1, o_mix], [gt2, nf2, sh3, sc3], [dx2, dh3], row_grads=[f32, bf16], param_grads=[True] * 4,
        name="res2_mod3_bwd")
    dmerged = _mm(do_mix, w_o, 'nt', out_dtype=bf16, name="out_proj_dx")
    dw_o = _mm(merged, do_mix, 'tn', out_dtype=bf16, name="out_proj_dw")
    dp_a, dglu, dga, dgb = _rw_vjp(_f_merge, merge_rows, [], [dmerged], row_grads=[bf16] * 4,
                                   param_grads=[], name="merge_bwd")

    dgl = _mm(dglu, w_glu, 'nt', out_dtype=bf16, name="glu_proj_dx")
    dw_glu = _mm(gl, dglu, 'tn', out_dtype=bf16, name="glu_proj_dw")
    dyb, du_skip, dd_row = _rw_vjp(_f_s5_post, [yb, (proj, lay.u, S5_W)], [d_row], [dgl],
                                   row_grads=[bf16, f32], param_grads=[True], name="s5_post_bwd")
    ds5 = _s5_out_dx(dyb, c_c, "s5_out_dx")
    dc_c = _s5_out_dw(s5s, dyb, "s5_out_dw")
    (g5, dar, dai), (terms['w_ffn2_in'],) = _s5_scan_bwd(ds5, s5s, ar_row, ai_row, "s5_scan_bwd",
                                                         ride=scatter_of([('w_ffn2_in', ffn_in_back(dw2i))]))
    du = _s5_bu_dx(g5, bd_c, du_skip, "s5_bu_dx")
    dbd_c = _s5_bu_dw(proj, lay.u, g5, "s5_bu_dw")
    from_bd = lambda m_: _diag8(m_, S5_I, S5_P).transpose(0, 2, 1).reshape(S5_S, S5_I)
    dprep = _rw_vjp(_f_s5_prep, prep_rows, [], [dar.reshape(S5_S, 1), dai.reshape(S5_S, 1),
                                                from_bd(dbd_c[:, :S5_SB]), from_bd(dbd_c[:, S5_SB:])],
                    row_grads=[f32] * 5, param_grads=[], name="s5_prep_bwd", tm=512)
    dlr, dli, dldt, dbr, dbi = dprep
    g_s5 = dict(
        s5_lambda_re=dlr.reshape(S5_G, S5_P), s5_lambda_im=dli.reshape(S5_G, S5_P),
        s5_log_dt=dldt.reshape(S5_G, S5_P).sum(axis=1),
        s5_b_re=dbr.reshape(S5_G, S5_P, S5_I), s5_b_im=dbi.reshape(S5_G, S5_P, S5_I),
        s5_c_re=_diag8(dc_c[:S5_S], S5_P, S5_I).transpose(0, 2, 1),
        s5_c_im=-_diag8(dc_c[S5_S:], S5_P, S5_I).transpose(0, 2, 1),
        s5_d=dd_row.reshape(S5_G, S5_I))

    dy_a = _mm(dp_a, w_a, 'nt', out_dtype=bf16, name="a_proj_dx")
    dw_a = _mm(y_a, dp_a, 'tn', out_dtype=bf16, name="a_proj_dw")
    dy_ssd, dz, dssd_nw = _rw_vjp(_f_ssd_post, [y_ssd, (proj, lay.z, SSD_DI)], [ssd_norm_w], [dy_a],
                                  row_grads=[f32, bf16], param_grads=[True], name="ssd_post_bwd")
    early = [('w_out', rows_back(dw_o, d)), ('w_b_glu', _shard_cols(dw_glu)), ('w_a_proj', rows_back(dw_a, SSD_DI))]
    (dxs, dbm, dcm, ddt4, da4, ddsk4), landed = _ssd_bwd(xs, bm, cm, dt4, a4, dsk4, hs, dy_ssd, "ssd_bwd",
                                                         ride=scatter_of(early))
    terms.update({n: p for (n, _), p in zip(early, landed)})
    da_log, dd_ssd = head_vjp((da4, ddsk4))
    dpre, ddt_raw, dbias128 = _rw_vjp(_f_ssd_pre, [pre, (proj, lay.dt, LANE)], [bias128, spread, spread_t],
                                      [dxs, dbm, dcm, ddt4], row_grads=[f32, bf16],
                                      param_grads=[True, False, False], name="ssd_pre_bwd")
    dxbc, dconv_w8, dconv_b = _conv_bwd(dpre, proj, lay.xbc, conv_w8, "conv_bwd")

    dproj = jnp.concatenate([dz, dxbc, du, dga, dgb, ddt_raw], axis=1)
    dw_inr = _mm(h2, dproj, 'tn', out_dtype=bf16, name="in_proj_dw")
    dh2, (terms['w_in'],) = _mm(dproj, w_inr, 'nt', out_dtype=bf16, name="in_proj_dx",
                                ride=scatter_of([('w_in', lay.restore_slabs(dw_inr))]))
    dx0, df1, dgt1, dnmx, dsh2, dsc2 = _rw_vjp(
        res1, [x2d, f1], [gt1, nmx, sh2, sc2], [dx1, dh2], row_grads=[f32, bf16], param_grads=[True] * 4,
        name="res1_mod2_bwd")
    dw1o = _mm(act1, df1, 'tn', out_dtype=bf16, name="ffn1_out_dw")
    dab1, (terms['w_ffn1_out'],) = _ffn_out_dx(df1, w1o, ab1, "ffn1_out_dx",
                                               ride=scatter_of([('w_ffn1_out', rows_back(dw1o, ff))]))
    dw1i = _ffn_in_dw(h1, dab1, "ffn1_in_dw")
    dh1, (terms['w_ffn1_in'],) = _ffn_in_dx(dab1, w1i, "ffn1_in_dx",
                                            ride=scatter_of([('w_ffn1_in', ffn_in_back(dw1i))]))
    grad_x, dnf1, dsh1, dsc1 = _rw_vjp(_f_mod_keep, [x2d], [nf1, sh1, sc1], [dh1, dx0],
                                       row_grads=[f32], param_grads=[True] * 3, name="mod1_bwd")
    d_mods = jnp.concatenate([dsh1, dsc1, dgt1, dsh2, dsc2, dgt2, dsh3, dsc3, dgt3], axis=1).reshape(-1)

    sums = [_sum_lead(terms[n], "sum_" + n) for n in BIG]
    others = _swap_sibling(sums, "swap_sums")

    out_g, out_d, out_m, out_v = {}, {}, {}, {}
    for n, s_own, s_sib in zip(BIG, sums, others):
        r = _adamw(W[n][0], Mo[n][0], Vo[n][0], [s_own, s_sib], "adamw_" + n)
        out_g[n], out_d[n], out_m[n], out_v[n] = [o[None] for o in r]

    local = dict(
        b_ada=d_mods, norm_ffn1=dnf1, norm_mix=dnmx, conv_w=dconv_w8[:CONV_K], conv_b=dconv_b,
        dt_bias=dbias128[:, :SSD_HEADS], a_log=da_log, d_ssd=dd_ssd, ssd_norm_w=dssd_nw,
        norm_ffn2=dnf2, norm_final=dnfin, **g_s5)
    flat = jnp.concatenate([local[n].reshape(-1) for n in SMALL])
    g3 = _gather8(flat, "gather_small_grads")
    n_small = flat.shape[0]
    npad = _round_up(n_small, SUBLANE * LANE)
    g3p = jnp.zeros((8, npad), f32).at[:, :n_small].set(g3).reshape(8, npad // LANE, LANE)
    gsum = _sum_lead(g3p, "sum_small").reshape(-1)

    def local_shard(n, a):
        if n == 'conv_w':
            return lax.dynamic_slice(a.reshape(CONV_K, CONV_DIM), (0, k_me * cw_cols), (CONV_K, cw_cols))
        return a

    pieces, off = {}, 0
    for n in SMALL:
        sz = local[n].size
        pieces[n] = local_shard(n, gsum[off:off + sz]).reshape(W[n].shape)
        off += sz

    def pack(dct):
        v_ = jnp.concatenate([dct[n].reshape(-1) for n in SMALL])
        pad = _round_up(v_.shape[0], SUBLANE * LANE) - v_.shape[0]
        return jnp.concatenate([v_, jnp.ones((pad,), f32)]).reshape(-1, LANE)

    rs = _adamw(pack(W), pack(Mo), pack(Vo), [pack(pieces)], "adamw_small")
    off = 0
    for n in SMALL:
        sz = W[n].size
        out_g[n], out_d[n], out_m[n], out_v[n] = [o.reshape(-1)[off:off + sz].reshape(W[n].shape) for o in rs]
        off += sz

    dm_loc = lax.dynamic_slice(g3[:, :N_ADA * d], (0, k_me * n_ada_loc), (SUBLANE, n_ada_loc))
    g_ada = _ada_bwd(c_all, dm_loc, "ada_bwd")
    r = _adamw(w_ada[0], m_w_ada[0], v_w_ada[0], [g_ada], "adamw_w_ada")
    out_g['w_ada'], out_d['w_ada'], out_m['w_ada'], out_v['w_ada'] = [o[None] for o in r]

    return (loss, grad_x[None], *[out_g[n] for n in WEIGHTS], *[out_d[n] for n in WEIGHTS],
            *[out_m[n] for n in WEIGHTS], *[out_v[n] for n in WEIGHTS])
```

```python
import functools
import math

import numpy as np
import jax
import jax.numpy as jnp
from jax import lax
from jax.experimental import pallas as pl
from jax.experimental.pallas import tpu as pltpu

f32 = jnp.float32
bf16 = jnp.bfloat16
HI = lax.Precision.HIGHEST
MESH = pl.DeviceIdType.MESH
AXES = ("x", "y", "c")

EPS = 1e-6
SSD_HEADS, SSD_P, SSD_N, SSD_G, SSD_R, SSD_L = 32, 64, 128, 4, 8, 128
SSD_DI = SSD_HEADS * SSD_P
CONV_K = 4
CONV_DIM = SSD_DI + 2 * SSD_G * SSD_N
S5_W, S5_G, S5_I, S5_P = 1024, 64, 16, 64
S5_S = S5_G * S5_P
N_ADA = 9
ADAM_LR, ADAM_B1, ADAM_B2, ADAM_EPS, ADAM_WD, ADAM_STEP = 0.001, 0.9, 0.999, 1e-08, 0.01, 10

LANE = 128
SUBLANE = 8
VMEM_LIMIT = 56 << 20
MM_VMEM_BUDGET = 40 << 20
RW_VMEM_BUDGET = 36 << 20

WEIGHTS = ['w_ada', 'b_ada', 'norm_ffn1', 'w_ffn1_in', 'w_ffn1_out', 'norm_mix', 'w_in', 'conv_w', 'conv_b', 'dt_bias',
           'a_log', 'd_ssd', 'ssd_norm_w', 'w_a_proj', 's5_lambda_re', 's5_lambda_im', 's5_b_re', 's5_b_im', 's5_c_re',
           's5_c_im', 's5_d', 's5_log_dt', 'w_b_glu', 'w_out', 'norm_ffn2', 'w_ffn2_in', 'w_ffn2_out', 'norm_final']
BIG = ['w_ffn1_in', 'w_ffn1_out', 'w_in', 'w_a_proj', 'w_b_glu', 'w_out', 'w_ffn2_in', 'w_ffn2_out']
COL_SHARDED = ('w_ffn1_in', 'w_in', 'w_b_glu', 'w_ffn2_in')
SMALL = [n for n in WEIGHTS if n not in BIG and n != 'w_ada']


def _cp(sem=None):
    return pltpu.CompilerParams(dimension_semantics=sem, vmem_limit_bytes=VMEM_LIMIT)


def _tile(dim, target, align=LANE):
    if dim <= target:
        return dim
    t = (target // align) * align
    while t >= align:
        if dim % t == 0:
            return t
        t -= align
    return dim


def _round_up(n, m):
    return (n + m - 1) // m * m


def _mm(a, b, mode, *, out_dtype, name, a_win=None, b_win=None, add=None, ride=None):
    a0, aw = a_win or (0, a.shape[1])
    b0, bw = b_win or (0, b.shape[1])
    if mode == 'nn':
        m, k, n = a.shape[0], aw, bw
        assert b.shape[0] == k
    elif mode == 'nt':
        m, k, n = a.shape[0], aw, b.shape[0]
        assert bw == k
    else:
        k, m, n = a.shape[0], aw, bw
        assert b.shape[0] == k
    osz = jnp.dtype(out_dtype).itemsize
    tm, tn, tk = 1024, 1152, 3456
    while True:
        bm = _tile(math.gcd(m, a0) if (mode == 'tn' and a0) else m, tm)
        bn = _tile(math.gcd(n, b0) if (mode != 'nt' and b0) else n, tn)
        kk = k
        if mode != 'tn' and a0:
            kk = math.gcd(kk, a0)
        if mode == 'nt' and b0:
            kk = math.gcd(kk, b0)
        bk = _tile(kk, tk)
        need = 2 * (bm * bk * a.dtype.itemsize + bk * bn * b.dtype.itemsize + bm * bn * osz) + bm * bn * 4
        if add is not None:
            need += 2 * bm * bn * add.dtype.itemsize
        if need <= MM_VMEM_BUDGET or (tm <= 256 and tn <= 256 and tk <= 512):
            break
        if tk > 1024:
            tk //= 2
        elif tm >= tn:
            tm //= 2
        else:
            tn //= 2
    nk = k // bk
    assert m % bm == 0 and n % bn == 0 and k % bk == 0, (name, m, n, k, bm, bn, bk)
    if mode == 'nn':
        ao, bo = a0 // bk, b0 // bn
        a_blk, a_map = (bm, bk), lambda i, j, q: (i, q + ao)
        b_blk, b_map = (bk, bn), lambda i, j, q: (q, j + bo)
    elif mode == 'nt':
        ao, bo = a0 // bk, b0 // bk
        a_blk, a_map = (bm, bk), lambda i, j, q: (i, q + ao)
        b_blk, b_map = (bn, bk), lambda i, j, q: (j, q + bo)
    else:
        ao, bo = a0 // bm, b0 // bn
        a_blk, a_map = (bk, bm), lambda i, j, q: (q, i + ao)
        b_blk, b_map = (bk, bn), lambda i, j, q: (q, j + bo)
    return _mm_core(a, b, mode, grid=(m // bm, n // bn, nk), a_blk=a_blk, a_map=a_map, b_blk=b_blk, b_map=b_map,
                    o_blk=(bm, bn), o_map=lambda i, j, q: (i, j), out_shape=(m, n), out_dtype=out_dtype, name=name,
                    add=add, ride=ride)


def _mm_core(a, b, mode, *, grid, a_blk, a_map, b_blk, b_map, o_blk, o_map, out_shape, out_dtype, name,
             add=None, ride=None):
    dims = {'nn': (((1,), (0,)), ((), ())), 'nt': (((1,), (1,)), ((), ())), 'tn': (((0,), (0,)), ((), ()))}[mode]
    nk = grid[-1]
    has_add = add is not None
    nr = ride.n if ride is not None else 0

    def body(*refs):
        a_ref, b_ref = refs[0], refs[1]
        pos = 2
        add_ref = refs[pos] if has_add else None
        pos += int(has_add)
        r_ins = refs[pos:pos + nr]
        o_ref = refs[pos + nr]
        r_outs = refs[pos + nr + 1:pos + 2 * nr + 1]
        acc_ref = refs[pos + 2 * nr + 1]
        r_sems = refs[pos + 2 * nr + 2:]
        ids = [pl.program_id(ax) for ax in range(len(grid))]
        q = ids[-1]
        if nr:
            @pl.when(functools.reduce(lambda u, v: u & v, [i == 0 for i in ids]))
            def _():
                ride.start(r_ins, r_outs, r_sems)

        @pl.when(q == 0)
        def _():
            acc_ref[...] = jnp.zeros_like(acc_ref)

        acc_ref[...] += lax.dot_general(a_ref[...].astype(bf16), b_ref[...].astype(bf16), dims,
                                        preferred_element_type=f32)

        @pl.when(q == nk - 1)
        def _():
            r = acc_ref[...]
            if has_add:
                r = r + add_ref[...].astype(f32)
            o_ref[...] = r.astype(out_dtype)

        if nr:
            @pl.when(functools.reduce(lambda u, v: u & v, [i == g - 1 for i, g in zip(ids, grid)]))
            def _():
                ride.wait(r_ins, r_outs, r_sems)

    in_specs = [pl.BlockSpec(a_blk, a_map), pl.BlockSpec(b_blk, b_map)]
    ops = [a, b]
    if has_add:
        in_specs.append(pl.BlockSpec(o_blk, o_map))
        ops.append(add)
    out_specs = [pl.BlockSpec(o_blk, o_map)]
    out_shapes = [jax.ShapeDtypeStruct(out_shape, out_dtype)]
    scratch = [pltpu.VMEM(o_blk, f32)]
    if nr:
        in_specs += ride.specs
        ops += ride.srcs
        out_specs += ride.specs
        out_shapes += ride.out_shape
        scratch += ride.scratch
    sem = ("arbitrary",) * len(grid) if nr else ("parallel",) * (len(grid) - 1) + ("arbitrary",)
    res = pl.pallas_call(
        body, name=name, grid=grid, in_specs=in_specs, out_specs=out_specs, out_shape=out_shapes,
        scratch_shapes=scratch, compiler_params=_cp(sem),
    )(*ops)
    return (res[0], list(res[1:])) if nr else res[0]


def _swiglu(a, b):
    return jax.nn.silu(a) * b


def _ride_parts(refs, n_in, n_out, ride):
    nr = ride.n if ride is not None else 0
    ins = refs[:n_in]
    r_ins = refs[n_in:n_in + nr]
    outs = refs[n_in + nr:n_in + nr + n_out]
    r_outs = refs[n_in + nr + n_out:n_in + 2 * nr + n_out]
    return ins, r_ins, outs, r_outs, refs[n_in + 2 * nr + n_out:]


def _call_with_ride(body_core, grid, in_specs, ops, out_specs, out_shape, name, ride, scratch=(), sem=None):
    nr = ride.n if ride is not None else 0
    n_in, n_out, n_scr = len(in_specs), len(out_specs), len(scratch)

    def body(*refs):
        ins, r_ins, outs, r_outs, rest = _ride_parts(refs, n_in, n_out, ride)
        ids = [pl.program_id(ax) for ax in range(len(grid))]
        if nr:
            @pl.when(functools.reduce(lambda u, v: u & v, [i == 0 for i in ids]))
            def _():
                ride.start(r_ins, r_outs, rest[n_scr:])
        body_core(ins, outs, *rest[:n_scr])
        if nr:
            @pl.when(functools.reduce(lambda u, v: u & v, [i == g - 1 for i, g in zip(ids, grid)]))
            def _():
                ride.wait(r_ins, r_outs, rest[n_scr:])

    res = pl.pallas_call(
        body, name=name, grid=grid, in_specs=in_specs + (ride.specs if nr else []),
        out_specs=out_specs + (ride.specs if nr else []), out_shape=out_shape + (ride.out_shape if nr else []),
        scratch_shapes=list(scratch) + (ride.scratch if nr else []),
        compiler_params=_cp(("arbitrary",) * len(grid) if nr else (sem or ("parallel",) * len(grid))),
    )(*ops, *(ride.srcs if nr else []))
    return list(res[:n_out]), list(res[n_out:])


def _ffn_in(h, w, name, ride=None):
    t, d = h.shape
    ffp = w.shape[1] // 2
    bm, bn = _tile(t, 1024), _tile(ffp, 512)
    nb = ffp // bn
    nn = (((1,), (0,)), ((), ()))

    def core(ins, outs):
        h_ref, wa_ref, wb_ref = ins
        ab_ref, act_ref = outs
        hv = h_ref[...].astype(bf16)
        a = lax.dot_general(hv, wa_ref[...].astype(bf16), nn, preferred_element_type=f32)
        b = lax.dot_general(hv, wb_ref[...].astype(bf16), nn, preferred_element_type=f32)
        ab_ref[0] = a.astype(bf16)
        ab_ref[1] = b.astype(bf16)
        act_ref[...] = _swiglu(a, b).astype(bf16)

    (ab, act), landed = _call_with_ride(
        core, (t // bm, nb),
        [pl.BlockSpec((bm, d), lambda i, j: (i, 0)), pl.BlockSpec((d, bn), lambda i, j: (0, j)),
         pl.BlockSpec((d, bn), lambda i, j: (0, j + nb))], [h, w, w],
        [pl.BlockSpec((2, bm, bn), lambda i, j: (0, i, j)), pl.BlockSpec((bm, bn), lambda i, j: (i, j))],
        [jax.ShapeDtypeStruct((2, t, ffp), bf16), jax.ShapeDtypeStruct((t, ffp), bf16)], name, ride)
    return ab, act, landed


def _ffn_out_dx(df, wo, ab, name, ride=None):
    t, d = df.shape
    ffp = wo.shape[0]
    bm, bn = _tile(t, 1024), _tile(ffp, 512)
    nt = (((1,), (1,)), ((), ()))

    def core(ins, outs):
        df_ref, wo_ref, ab_ref = ins
        dact = lax.dot_general(df_ref[...].astype(bf16), wo_ref[...].astype(bf16), nt, preferred_element_type=f32)
        _, vjp = jax.vjp(_swiglu, ab_ref[0].astype(f32), ab_ref[1].astype(f32))
        da, db = vjp(dact)
        outs[0][0] = da.astype(bf16)
        outs[0][1] = db.astype(bf16)

    (dab,), landed = _call_with_ride(
        core, (t // bm, ffp // bn),
        [pl.BlockSpec((bm, d), lambda i, j: (i, 0)), pl.BlockSpec((bn, d), lambda i, j: (j, 0)),
         pl.BlockSpec((2, bm, bn), lambda i, j: (0, i, j))], [df, wo, ab],
        [pl.BlockSpec((2, bm, bn), lambda i, j: (0, i, j))], [jax.ShapeDtypeStruct((2, t, ffp), bf16)], name, ride)
    return dab, landed


def _ffn_in_dx(dab, w, name, ride=None):
    _, t, ffp = dab.shape
    d = w.shape[0]
    bm, bn, bk = _tile(t, 1024), _tile(d, 1024), _tile(ffp, 2816)
    nkh = ffp // bk
    return _mm_core(dab, w, 'nt', grid=(t // bm, d // bn, 2 * nkh),
                    a_blk=(None, bm, bk), a_map=lambda i, j, q: (q // nkh, i, q % nkh),
                    b_blk=(bn, bk), b_map=lambda i, j, q: (j, q),
                    o_blk=(bm, bn), o_map=lambda i, j, q: (i, j),
                    out_shape=(t, d), out_dtype=bf16, name=name, ride=ride)


def _ffn_in_dw(h, dab, name):
    _, t, ffp = dab.shape
    d = h.shape[1]
    bm, bn, bk = _tile(d, 1024), _tile(ffp, 1408), _tile(t, 2048)
    nbh = ffp // bn
    return _mm_core(h, dab, 'tn', grid=(d // bm, 2 * nbh, t // bk),
                    a_blk=(bk, bm), a_map=lambda i, j, q: (q, i),
                    b_blk=(None, bk, bn), b_map=lambda i, j, q: (j // nbh, q, j % nbh),
                    o_blk=(bm, bn), o_map=lambda i, j, q: (i, j),
                    out_shape=(d, 2 * ffp), out_dtype=bf16, name=name)


def _win(r):
    return r if isinstance(r, tuple) else (r, 0, r.shape[1])


def _row_tile(t, widths):
    per_row = 48 * max(widths)
    tm = 512
    while tm > SUBLANE and tm * per_row > RW_VMEM_BUDGET:
        tm //= 2
    return min(tm, t)


def _row_spec(r, tm):
    arr, c0, w = _win(r)
    assert c0 % w == 0, (c0, w)
    cb = c0 // w
    return pl.BlockSpec((tm, w), lambda i: (i, cb))


def _full_spec(p):
    nd = p.ndim
    return pl.BlockSpec(p.shape, lambda i: (0,) * nd)


def _rw(f, rows, params, outs, *, name, accs=(), tm=None):
    t = _win(rows[0])[0].shape[0]
    tm = tm or _row_tile(t, [_win(r)[2] for r in rows] + [w for w, _ in outs])
    nr, npar, no, na = len(rows), len(params), len(outs), len(accs)

    def body(*refs):
        vals = [r[...] for r in refs[:nr + npar]]
        res = f(*vals)
        res = res if isinstance(res, (tuple, list)) else (res,)
        for o_ref, v in zip(refs[nr + npar:nr + npar + no], res[:no]):
            o_ref[...] = v.astype(o_ref.dtype)
        if na:
            first = pl.program_id(0) == 0
            for a_ref, v in zip(refs[nr + npar + no:], res[no:]):
                @pl.when(first)
                def _(a_ref=a_ref):
                    a_ref[...] = jnp.zeros_like(a_ref)
                a_ref[...] += v

    out_shape = [jax.ShapeDtypeStruct((t, w), d) for w, d in outs] + [jax.ShapeDtypeStruct(s, f32) for s in accs]
    out_specs = [pl.BlockSpec((tm, w), lambda i: (i, 0)) for w, _ in outs] + \
                [pl.BlockSpec(s, lambda i: (0, 0)) for s in accs]
    return pl.pallas_call(
        body, name=name, grid=(t // tm,),
        in_specs=[_row_spec(r, tm) for r in rows] + [_full_spec(p) for p in params],
        out_specs=out_specs, out_shape=out_shape,
        compiler_params=_cp(("arbitrary",)),
    )(*[_win(r)[0] for r in rows], *params)


def _rw_vjp(f, rows, params, cots, *, row_grads, param_grads, name, tm=None):
    t = _win(rows[0])[0].shape[0]
    cot_rows = [c for c in cots if c is not None]
    tm = tm or _row_tile(t, [_win(r)[2] for r in rows] + [_win(c)[2] for c in cot_rows])
    nr, npar, ncot = len(rows), len(params), len(cot_rows)
    d_rows = [i for i, d in enumerate(row_grads) if d is not None]
    d_pars = [i for i, d in enumerate(param_grads) if d]

    def body(*refs):
        rv = [r[...] for r in refs[:nr]]
        pv = [r[...] for r in refs[nr:nr + npar]]
        cv = [r[...] for r in refs[nr + npar:nr + npar + ncot]]
        outs_r = refs[nr + npar + ncot:nr + npar + ncot + len(d_rows)]
        outs_p = refs[nr + npar + ncot + len(d_rows):]

        def g(*diff):
            rr, pp = list(rv), list(pv)
            for i, v in zip(d_rows, diff[:len(d_rows)]):
                rr[i] = v
            for i, v in zip(d_pars, diff[len(d_rows):]):
                pp[i] = v
            res = f(*rr, *pp)
            return tuple(res) if isinstance(res, (tuple, list)) else (res,)

        prim, vjp = jax.vjp(g, *[rv[i] for i in d_rows], *[pv[i] for i in d_pars])
        it = iter(cv)
        cts = tuple(next(it).astype(o.dtype) if c is not None else jnp.zeros_like(o) for o, c in zip(prim, cots))
        grads = vjp(cts)
        for o_ref, v in zip(outs_r, grads[:len(d_rows)]):
            o_ref[...] = v.astype(o_ref.dtype)
        first = pl.program_id(0) == 0
        for o_ref, v in zip(outs_p, grads[len(d_rows):]):
            @pl.when(first)
            def _(o_ref=o_ref):
                o_ref[...] = jnp.zeros_like(o_ref)
            o_ref[...] += v.astype(f32)

    out_shape = [jax.ShapeDtypeStruct((t, _win(rows[i])[2]), row_grads[i]) for i in d_rows] + \
                [jax.ShapeDtypeStruct(params[i].shape, f32) for i in d_pars]
    out_specs = [pl.BlockSpec((tm, _win(rows[i])[2]), lambda i_: (i_, 0)) for i in d_rows] + \
                [_full_spec(params[i]) for i in d_pars]
    return pl.pallas_call(
        body, name=name, grid=(t // tm,),
        in_specs=[_row_spec(r, tm) for r in rows] + [_full_spec(p) for p in params] + [_row_spec(c, tm) for c in cot_rows],
        out_specs=out_specs, out_shape=out_shape,
        compiler_params=_cp(("arbitrary",)),
    )(*[_win(r)[0] for r in rows], *params, *[_win(c)[0] for c in cot_rows])


def _rms(x, g):
    return x * lax.rsqrt(jnp.mean(x * x, axis=-1, keepdims=True) + EPS) * g


def _f_mod(x, nw, sh, sc):
    return (_rms(x, nw) * (1.0 + sc) + sh).astype(bf16)


def _f_mod_keep(x, nw, sh, sc):
    return _f_mod(x, nw, sh, sc), x


def _f_res_mod(coef, x, o, g, nw, sh, sc):
    x1 = x + coef * g * o.astype(f32)
    return x1, _f_mod(x1, nw, sh, sc)


def _times01(x, e):
    hi = x.astype(bf16)
    r1 = x - hi.astype(f32)
    mid = r1.astype(bf16)
    lo = (r1 - mid.astype(f32)).astype(bf16)
    return (jnp.dot(hi, e, preferred_element_type=f32) + jnp.dot(mid, e, preferred_element_type=f32) +
            jnp.dot(lo, e, preferred_element_type=f32))


@jax.custom_vjp
def _spread_heads(x, e, et):
    return _times01(x, e)


_spread_heads.defvjp(lambda x, e, et: (_times01(x, e), (e, et)),
                     lambda res, g: (_times01(g, res[1]), None, None))


def _f_ssd_pre(pre, dtraw, bias, e, et):
    xc = jax.nn.silu(pre)
    dtx = _spread_heads(jax.nn.softplus(dtraw + bias), e, et)
    return xc[:, :SSD_DI], xc[:, SSD_DI:SSD_DI + SSD_G * SSD_N], xc[:, SSD_DI + SSD_G * SSD_N:], dtx


def _f_ssd_post(y, z, nw):
    yz = y * jax.nn.silu(z)
    w = SSD_DI // SSD_G
    parts = []
    for g in range(SSD_G):
        s = yz[:, g * w:(g + 1) * w]
        parts.append(s * lax.rsqrt(jnp.mean(s * s, axis=-1, keepdims=True) + EPS))
    return (jnp.concatenate(parts, axis=1) * nw).astype(bf16)


def _f_s5_post(yb, u, d):
    return jax.nn.gelu(yb + d * u).astype(bf16)


def _f_merge(pa, glu, ga, gb):
    d = pa.shape[1]
    pb = glu[:, :d] * jax.nn.sigmoid(glu[:, d:])
    return (jax.nn.sigmoid(ga) * pa + jax.nn.sigmoid(gb) * pb).astype(bf16)


def _f_final(x2, o, tgt, g, nw):
    x3 = x2 + 0.5 * g * o.astype(f32)
    y = _rms(x3, nw)
    return 0.5 * jnp.mean(jnp.square(y - tgt), axis=-1, keepdims=True)


def _f_final_loss(x2, o, tgt, g, nw):
    rows = _f_final(x2, o, tgt, g, nw)
    return jnp.broadcast_to(jnp.sum(rows, axis=0, keepdims=True), (1, LANE))


def _f_s5_prep(lr, li, ldt, br, bi):
    dt = jnp.exp(ldt)
    lr = jnp.minimum(lr, -1e-4)
    mag = jnp.exp(lr * dt)
    ar = mag * jnp.cos(li * dt)
    ai = mag * jnp.sin(li * dt)
    den = lr * lr + li * li
    nr = ar - 1.0
    kr = (nr * lr + ai * li) / den
    ki = (ai * lr - nr * li) / den
    return ar, ai, kr * br - ki * bi, kr * bi + ki * br


def _shift_down(cur, halo8, j):
    if j == 0:
        return cur
    rolled = pltpu.roll(cur, j, 0)
    row8 = lax.broadcasted_iota(jnp.int32, halo8.shape, 0)
    top = jnp.where(row8 < j, pltpu.roll(halo8, j, 0), rolled[:SUBLANE])
    return jnp.concatenate([top, rolled[SUBLANE:]], axis=0)


def _shift_up(cur, halo8, j):
    if j == 0:
        return cur
    n = cur.shape[0]
    rolled = pltpu.roll(cur, n - j, 0)
    row8 = lax.broadcasted_iota(jnp.int32, halo8.shape, 0)
    bot = jnp.where(row8 >= SUBLANE - j, pltpu.roll(halo8, SUBLANE - j, 0), rolled[n - SUBLANE:])
    return jnp.concatenate([rolled[:n - SUBLANE], bot], axis=0)


def _conv_fwd(proj, c0, w8, b, name):
    t = proj.shape[0]
    cw = 1024
    tm = min(512, t)
    cb0 = c0 // cw
    r8 = tm // SUBLANE

    def body(x_ref, h_ref, w_ref, b_ref, o_ref):
        i = pl.program_id(1)
        x = x_ref[...]
        halo = jnp.where(i > 0, h_ref[...], 0.0)
        acc = b_ref[...] + w_ref[CONV_K - 1:CONV_K, :] * x
        for j in range(1, CONV_K):
            acc = acc + w_ref[CONV_K - 1 - j:CONV_K - j, :] * _shift_down(x, halo, j)
        o_ref[...] = acc

    return pl.pallas_call(
        body, name=name, grid=(CONV_DIM // cw, t // tm),
        in_specs=[pl.BlockSpec((tm, cw), lambda c, i: (i, cb0 + c)),
                  pl.BlockSpec((SUBLANE, cw), lambda c, i: (jnp.maximum(i * r8 - 1, 0), cb0 + c)),
                  pl.BlockSpec((SUBLANE, cw), lambda c, i: (0, c)),
                  pl.BlockSpec((1, cw), lambda c, i: (0, c))],
        out_specs=pl.BlockSpec((tm, cw), lambda c, i: (i, c)),
        out_shape=jax.ShapeDtypeStruct((t, CONV_DIM), f32),
        compiler_params=_cp(("parallel", "arbitrary")),
    )(proj, proj, w8, b)


def _conv_bwd(dpre, proj, c0, w8, name):
    t = proj.shape[0]
    cw = 1024
    tm = min(512, t)
    cb0 = c0 // cw
    r8 = tm // SUBLANE
    nb = t // tm

    def body(d_ref, dn_ref, x_ref, xh_ref, w_ref, dx_ref, dw_ref, db_ref):
        i = pl.program_id(1)
        d = d_ref[...]
        dn = jnp.where(i < nb - 1, dn_ref[...], 0.0)
        x = x_ref[...]
        xh = jnp.where(i > 0, xh_ref[...], 0.0)

        @pl.when(i == 0)
        def _():
            dw_ref[...] = jnp.zeros_like(dw_ref)
            db_ref[...] = jnp.zeros_like(db_ref)

        dx = w_ref[CONV_K - 1:CONV_K, :] * d
        rows = [jnp.sum(d * x, axis=0, keepdims=True)]
        for j in range(1, CONV_K):
            dx = dx + w_ref[CONV_K - 1 - j:CONV_K - j, :] * _shift_up(d, dn, j)
            rows.append(jnp.sum(d * _shift_down(x, xh, j), axis=0, keepdims=True))
        dx_ref[...] = dx.astype(dx_ref.dtype)
        dw = jnp.concatenate([rows[CONV_K - 1 - k] for k in range(CONV_K)] +
                             [jnp.zeros((SUBLANE - CONV_K, cw), f32)], axis=0)
        dw_ref[...] += dw
        db_ref[...] += jnp.sum(d, axis=0, keepdims=True)

    return pl.pallas_call(
        body, name=name, grid=(CONV_DIM // cw, nb),
        in_specs=[pl.BlockSpec((tm, cw), lambda c, i: (i, c)),
                  pl.BlockSpec((SUBLANE, cw), lambda c, i: (jnp.minimum((i + 1) * r8, nb * r8 - 1), c)),
                  pl.BlockSpec((tm, cw), lambda c, i: (i, cb0 + c)),
                  pl.BlockSpec((SUBLANE, cw), lambda c, i: (jnp.maximum(i * r8 - 1, 0), cb0 + c)),
                  pl.BlockSpec((SUBLANE, cw), lambda c, i: (0, c))],
        out_specs=[pl.BlockSpec((tm, cw), lambda c, i: (i, c)),
                   pl.BlockSpec((SUBLANE, cw), lambda c, i: (0, c)),
                   pl.BlockSpec((1, cw), lambda c, i: (0, c))],
        out_shape=[jax.ShapeDtypeStruct((t, CONV_DIM), bf16), jax.ShapeDtypeStruct((SUBLANE, CONV_DIM), f32),
                   jax.ShapeDtypeStruct((1, CONV_DIM), f32)],
        compiler_params=_cp(("parallel", "arbitrary")),
    )(dpre, dpre, proj, proj, w8)


def _cumsum_rows_impl(x):
    n = x.shape[0]
    row = lax.broadcasted_iota(jnp.int32, x.shape, 0)
    s = 1
    while s < n:
        x = x + jnp.where(row >= s, pltpu.roll(x, s, 0), 0.0)
        s *= 2
    return x


@jax.custom_vjp
def _cumsum_rows(x):
    return _cumsum_rows_impl(x)


def _cumsum_rows_bwd(_, g):
    c = _cumsum_rows_impl(g)
    return (c[c.shape[0] - 1:, :] - c + g,)


_cumsum_rows.defvjp(lambda x: (_cumsum_rows_impl(x), None), _cumsum_rows_bwd)


@jax.custom_vjp
def _swap_halves(t):
    return pltpu.roll(t, LANE // 2, 1)


_swap_halves.defvjp(lambda t: (pltpu.roll(t, LANE // 2, 1), None), lambda _, g: (pltpu.roll(g, LANE // 2, 1),))


def _ssd_chunk(xs, bm, cm, dtx, ax, dskx, ht):
    n = SSD_L
    assert n == LANE and SSD_P * 2 == LANE
    row = lax.broadcasted_iota(jnp.int32, (n, n), 0)
    col = lax.broadcasted_iota(jnp.int32, (n, n), 1)
    causal = row >= col
    lo = col < SSD_P
    cs = _cumsum_rows(dtx * ax)
    xdt = xs * dtx
    last = cs[n - 1:n, :]
    cb = lax.dot_general(cm.astype(bf16), bm.astype(bf16), (((1,), (1,)), ((), ())), preferred_element_type=f32)
    y_off = jnp.dot(cm.astype(bf16), ht.astype(bf16), preferred_element_type=f32) * jnp.exp(cs)
    st = lax.dot_general(bm.astype(bf16), (xdt * jnp.exp(last - cs)).astype(bf16), (((0,), (0,)), ((), ())),
                         preferred_element_type=f32)
    ht_new = jnp.exp(last) * ht + st
    ys = []
    for q in range(SSD_R // 2):
        tq = cs[:, q * LANE:(q + 1) * LANE]
        sw = _swap_halves(tq)
        tqt = tq.T
        xq = xdt[:, q * LANE:(q + 1) * LANE].astype(bf16)
        pair = []
        for c_col, r_row in ((jnp.where(lo, tq, sw), tqt[0:1, :]), (jnp.where(lo, sw, tq), tqt[SSD_P:SSD_P + 1, :])):
            decay = jnp.exp(jnp.where(causal, c_col - r_row, -1e30))
            pair.append(jnp.dot((cb * decay).astype(bf16), xq, preferred_element_type=f32))
        ys.append(jnp.where(lo, pair[0], pair[1]))
    return jnp.concatenate(ys, axis=1) + y_off + dskx * xs, ht_new


SSD_GB = 1


def _ssd_specs(nc, rev):
    ch = (lambda c: nc - 1 - c) if rev else (lambda c: c)
    gw = SSD_GB * SSD_R * SSD_P
    return [pl.BlockSpec((SSD_L, gw), lambda g, c: (ch(c), g)),
            pl.BlockSpec((SSD_L, SSD_GB * SSD_N), lambda g, c: (ch(c), g)),
            pl.BlockSpec((SSD_L, SSD_GB * SSD_N), lambda g, c: (ch(c), g)),
            pl.BlockSpec((SSD_L, gw), lambda g, c: (ch(c), g)),
            pl.BlockSpec((1, gw), lambda g, c: (0, g)),
            pl.BlockSpec((1, gw), lambda g, c: (0, g))]


def _ssd_group(refs, q):
    gw = SSD_R * SSD_P
    xs_ref, bm_ref, cm_ref, dt_ref, a_ref, dsk_ref = refs
    ln = slice(q * LANE, (q + 1) * LANE)
    wd = slice(q * gw, (q + 1) * gw)
    return (xs_ref[:, wd], bm_ref[:, ln], cm_ref[:, ln], dt_ref[:, wd], a_ref[:, wd], dsk_ref[:, wd])


def _ssd_fwd(xs, bm, cm, dt4, a4, dsk4, name, ride=None):
    t = xs.shape[0]
    nc = t // SSD_L
    gw = SSD_R * SSD_P

    nr = ride.n if ride is not None else 0
    ng = SSD_G // SSD_GB

    def body(*refs):
        xs_ref, bm_ref, cm_ref, dt_ref, a_ref, dsk_ref = refs[:6]
        r_ins = refs[6:6 + nr]
        y_ref, hs_ref = refs[6 + nr:8 + nr]
        r_outs = refs[8 + nr:8 + 2 * nr]
        h_ref = refs[8 + 2 * nr]
        r_sems = refs[9 + 2 * nr:]
        g, c = pl.program_id(0), pl.program_id(1)
        if nr:
            @pl.when((g == 0) & (c == 0))
            def _():
                ride.start(r_ins, r_outs, r_sems)

        @pl.when(c == 0)
        def _():
            h_ref[...] = jnp.zeros_like(h_ref)

        hs_ref[...] = h_ref[...]
        grp = (xs_ref, bm_ref, cm_ref, dt_ref, a_ref, dsk_ref)
        ops = [_ssd_group(grp, q) + (h_ref[:, q * gw:(q + 1) * gw],) for q in range(SSD_GB)]
        res = [_ssd_chunk(*o) for o in ops]
        for q, (y, hn) in enumerate(res):
            y_ref[:, q * gw:(q + 1) * gw] = y
            h_ref[:, q * gw:(q + 1) * gw] = hn

        if nr:
            @pl.when((g == ng - 1) & (c == nc - 1))
            def _():
                ride.wait(r_ins, r_outs, r_sems)

    res = pl.pallas_call(
        body, name=name, grid=(ng, nc), in_specs=_ssd_specs(nc, False) + (ride.specs if nr else []),
        out_specs=[pl.BlockSpec((SSD_L, SSD_GB * gw), lambda g, c: (c, g)),
                   pl.BlockSpec((None, None, SSD_N, SSD_GB * gw), lambda g, c: (g, c, 0, 0))] +
                  (ride.specs if nr else []),
        out_shape=[jax.ShapeDtypeStruct((t, SSD_DI), f32),
                   jax.ShapeDtypeStruct((ng, nc, SSD_N, SSD_GB * gw), f32)] + (ride.out_shape if nr else []),
        scratch_shapes=[pltpu.VMEM((SSD_N, SSD_GB * gw), f32)] + (ride.scratch if nr else []),
        compiler_params=_cp(("arbitrary", "arbitrary")),
    )(xs, bm, cm, dt4, a4, dsk4, *(ride.srcs if nr else []))
    return res[0], res[1], list(res[2:])


def _ssd_bwd(xs, bm, cm, dt4, a4, dsk4, hs, dy, name, ride=None):
    t = xs.shape[0]
    nc = t // SSD_L
    gw = SSD_R * SSD_P
    rc = lambda c: nc - 1 - c
    nr = ride.n if ride is not None else 0
    ng = SSD_G // SSD_GB

    def body(*refs):
        xs_ref, bm_ref, cm_ref, dt_ref, a_ref, dsk_ref, hs_ref, dy_ref = refs[:8]
        r_ins = refs[8:8 + nr]
        dxs_ref, dbm_ref, dcm_ref, ddt_ref, da_ref, ddsk_ref = refs[8 + nr:14 + nr]
        r_outs = refs[14 + nr:14 + 2 * nr]
        dh_ref = refs[14 + 2 * nr]
        r_sems = refs[15 + 2 * nr:]
        if nr:
            @pl.when((pl.program_id(0) == 0) & (pl.program_id(1) == 0))
            def _():
                ride.start(r_ins, r_outs, r_sems)

        @pl.when(pl.program_id(1) == 0)
        def _():
            dh_ref[...] = jnp.zeros_like(dh_ref)
            da_ref[...] = jnp.zeros_like(da_ref)
            ddsk_ref[...] = jnp.zeros_like(ddsk_ref)

        grp = (xs_ref, bm_ref, cm_ref, dt_ref, a_ref, dsk_ref)
        ops = [_ssd_group(grp, q) + (hs_ref[:, q * gw:(q + 1) * gw],) for q in range(SSD_GB)]
        cts = [(dy_ref[:, q * gw:(q + 1) * gw], dh_ref[:, q * gw:(q + 1) * gw]) for q in range(SSD_GB)]
        grads = [jax.vjp(_ssd_chunk, *o)[1](ct) for o, ct in zip(ops, cts)]
        for q, (dxs, dbm, dcm, ddt, da, ddsk, dh) in enumerate(grads):
            wd = slice(q * gw, (q + 1) * gw)
            ln = slice(q * LANE, (q + 1) * LANE)
            dxs_ref[:, wd] = dxs
            dbm_ref[:, ln] = dbm
            dcm_ref[:, ln] = dcm
            ddt_ref[:, wd] = ddt
            da_ref[:, wd] += da
            ddsk_ref[:, wd] += ddsk
            dh_ref[:, wd] = dh

        if nr:
            @pl.when((pl.program_id(0) == ng - 1) & (pl.program_id(1) == nc - 1))
            def _():
                ride.wait(r_ins, r_outs, r_sems)

    res = pl.pallas_call(
        body, name=name, grid=(ng, nc),
        in_specs=_ssd_specs(nc, True) + [
            pl.BlockSpec((None, None, SSD_N, SSD_GB * gw), lambda g, c: (g, rc(c), 0, 0)),
            pl.BlockSpec((SSD_L, SSD_GB * gw), lambda g, c: (rc(c), g))] + (ride.specs if nr else []),
        out_specs=[pl.BlockSpec((SSD_L, SSD_GB * gw), lambda g, c: (rc(c), g)),
                   pl.BlockSpec((SSD_L, SSD_GB * SSD_N), lambda g, c: (rc(c), g)),
                   pl.BlockSpec((SSD_L, SSD_GB * SSD_N), lambda g, c: (rc(c), g)),
                   pl.BlockSpec((SSD_L, SSD_GB * gw), lambda g, c: (rc(c), g)),
                   pl.BlockSpec((1, SSD_GB * gw), lambda g, c: (0, g)),
                   pl.BlockSpec((1, SSD_GB * gw), lambda g, c: (0, g))] + (ride.specs if nr else []),
        out_shape=[jax.ShapeDtypeStruct((t, SSD_DI), f32), jax.ShapeDtypeStruct((t, SSD_G * SSD_N), f32),
                   jax.ShapeDtypeStruct((t, SSD_G * SSD_N), f32), jax.ShapeDtypeStruct((t, SSD_DI), f32),
                   jax.ShapeDtypeStruct((1, SSD_DI), f32), jax.ShapeDtypeStruct((1, SSD_DI), f32)] +
                  (ride.out_shape if nr else []),
        scratch_shapes=[pltpu.VMEM((SSD_N, SSD_GB * gw), f32)] + (ride.scratch if nr else []),
        compiler_params=_cp(("arbitrary", "arbitrary")),
    )(xs, bm, cm, dt4, a4, dsk4, hs, dy, *(ride.srcs if nr else []))
    return list(res[:6]), list(res[6:])


S5_CH = 1024


S5_NB = 8
S5_UB = 128
S5_SB = 512
_NT = (((1,), (1,)), ((), ()))
_TN = (((0,), (0,)), ((), ()))


def _s5_fwd(proj, u0, bd_c, c_c, ar, ai, name):
    t = proj.shape[0]
    tb = min(128, t)
    ub = u0 // S5_W

    def body(u_ref, bd_ref, cc_ref, ar_ref, ai_ref, s_ref, yb_ref, bu_ref, carry):
        @pl.when(pl.program_id(0) == 0)
        def _():
            carry[...] = jnp.zeros_like(carry)

        u = u_ref[...].astype(bf16)
        for j in range(S5_NB):
            uj = u[:, j * S5_UB:(j + 1) * S5_UB]
            for half in range(2):
                bu_ref[:, half * S5_S + j * S5_SB:half * S5_S + (j + 1) * S5_SB] = jnp.dot(
                    uj, bd_ref[j * S5_UB:(j + 1) * S5_UB, half * S5_SB:(half + 1) * S5_SB], preferred_element_type=f32)

        for c0 in range(0, S5_S, S5_CH):
            re = pl.ds(c0, S5_CH)
            im = pl.ds(S5_S + c0, S5_CH)
            a_r = ar_ref[:, re]
            a_i = ai_ref[:, re]

            def step(k, st, re=re, im=im, a_r=a_r, a_i=a_i):
                sr, si = st
                row = pl.ds(k, 1)
                nr = a_r * sr - a_i * si + bu_ref[row, re]
                ni = a_r * si + a_i * sr + bu_ref[row, im]
                s_ref[row, re] = nr
                s_ref[row, im] = ni
                return nr, ni

            sr, si = lax.fori_loop(0, tb, step, (carry[:, re], carry[:, im]))
            carry[:, re] = sr
            carry[:, im] = si

        for j in range(S5_NB):
            lo, hi = j * S5_SB, (j + 1) * S5_SB
            yb_ref[:, j * S5_UB:(j + 1) * S5_UB] = (
                jnp.dot(s_ref[:, lo:hi].astype(bf16), cc_ref[lo:hi, :], preferred_element_type=f32) +
                jnp.dot(s_ref[:, S5_S + lo:S5_S + hi].astype(bf16), cc_ref[S5_S + lo:S5_S + hi, :],
                        preferred_element_type=f32))

    return pl.pallas_call(
        body, name=name, grid=(t // tb,),
        in_specs=[pl.BlockSpec((tb, S5_W), lambda i: (i, ub)), _full_spec(bd_c), _full_spec(c_c),
                  pl.BlockSpec((1, S5_S), lambda i: (0, 0)), pl.BlockSpec((1, S5_S), lambda i: (0, 0))],
        out_specs=[pl.BlockSpec((tb, 2 * S5_S), lambda i: (i, 0)), pl.BlockSpec((tb, S5_W), lambda i: (i, 0))],
        out_shape=[jax.ShapeDtypeStruct((t, 2 * S5_S), f32), jax.ShapeDtypeStruct((t, S5_W), f32)],
        scratch_shapes=[pltpu.VMEM((tb, 2 * S5_S), f32), pltpu.VMEM((1, 2 * S5_S), f32)],
        compiler_params=_cp(("arbitrary",)),
    )(proj, bd_c, c_c, ar, ai)


def _s5_bwd(dyb, s, proj, u0, bd_c, c_c, ar, ai, du_skip, name, ride=None):
    t = dyb.shape[0]
    tb = min(128, t)
    nb = t // tb
    r8 = tb // SUBLANE
    rb = lambda i: nb - 1 - i
    ub = u0 // S5_W

    def body(ins, outs, g_ref, carry):
        dyb_ref, s_ref, sh_ref, u_ref, skip_ref, bd_ref, cc_ref, ar_ref, ai_ref = ins
        du_ref, dar_ref, dai_ref, dbd_ref, dcc_ref = outs
        ds_ref = g_ref
        i = pl.program_id(0)

        @pl.when(i == 0)
        def _():
            carry[...] = jnp.zeros_like(carry)
            dar_ref[...] = jnp.zeros_like(dar_ref)
            dai_ref[...] = jnp.zeros_like(dai_ref)
            dbd_ref[...] = jnp.zeros_like(dbd_ref)
            dcc_ref[...] = jnp.zeros_like(dcc_ref)

        dyb = dyb_ref[...].astype(bf16)
        for jj in range(2 * S5_NB):
            blk = jj % S5_NB
            g_ref[:, jj * S5_SB:(jj + 1) * S5_SB] = lax.dot_general(
                dyb[:, blk * S5_UB:(blk + 1) * S5_UB], cc_ref[jj * S5_SB:(jj + 1) * S5_SB, :], _NT,
                preferred_element_type=f32)

        has_prev = (i < nb - 1).astype(f32)
        for c0 in range(0, S5_S, S5_CH):
            re = pl.ds(c0, S5_CH)
            im = pl.ds(S5_S + c0, S5_CH)
            a_r = ar_ref[:, re]
            a_i = ai_ref[:, re]

            def upd(st, row, sp_r, sp_i, re=re, im=im, a_r=a_r, a_i=a_i):
                gr, gi, acr, aci = st
                ngr = ds_ref[row, re] + a_r * gr + a_i * gi
                ngi = ds_ref[row, im] + a_r * gi - a_i * gr
                g_ref[row, re] = ngr
                g_ref[row, im] = ngi
                return ngr, ngi, acr + ngr * sp_r + ngi * sp_i, aci + ngi * sp_r - ngr * sp_i

            def step(k, st, re=re, im=im, upd=upd):
                tt = tb - 1 - k
                prev = pl.ds(tt - 1, 1)
                return upd(st, pl.ds(tt, 1), s_ref[prev, re], s_ref[prev, im])

            zero = jnp.zeros((1, S5_CH), f32)
            st = lax.fori_loop(0, tb - 1, step, (carry[:, re], carry[:, im], zero, zero))
            last = pl.ds(SUBLANE - 1, 1)
            gr, gi, acr, aci = upd(st, pl.ds(0, 1), sh_ref[last, re] * has_prev, sh_ref[last, im] * has_prev)
            carry[:, re] = gr
            carry[:, im] = gi
            dar_ref[:, re] += acr
            dai_ref[:, re] += aci

        u = u_ref[...].astype(bf16)
        for j in range(S5_NB):
            lo, hi = j * S5_SB, (j + 1) * S5_SB
            blk = slice(j * S5_UB, (j + 1) * S5_UB)
            g_re = g_ref[:, lo:hi].astype(bf16)
            g_im = g_ref[:, S5_S + lo:S5_S + hi].astype(bf16)
            du = (lax.dot_general(g_re, bd_ref[blk, :S5_SB], _NT, preferred_element_type=f32) +
                  lax.dot_general(g_im, bd_ref[blk, S5_SB:], _NT, preferred_element_type=f32) + skip_ref[:, blk])
            du_ref[:, blk] = du.astype(du_ref.dtype)
            dbd_ref[blk, :S5_SB] += lax.dot_general(u[:, blk], g_re, _TN, preferred_element_type=f32)
            dbd_ref[blk, S5_SB:] += lax.dot_general(u[:, blk], g_im, _TN, preferred_element_type=f32)
            dcc_ref[lo:hi, :] += lax.dot_general(s_ref[:, lo:hi].astype(bf16), dyb[:, blk], _TN,
                                                 preferred_element_type=f32)
            dcc_ref[S5_S + lo:S5_S + hi, :] += lax.dot_general(s_ref[:, S5_S + lo:S5_S + hi].astype(bf16), dyb[:, blk],
                                                               _TN, preferred_element_type=f32)

    row_blk = lambda w: pl.BlockSpec((tb, w), lambda i: (rb(i), 0))
    const = lambda shape: pl.BlockSpec(shape, lambda i: (0, 0))
    return _call_with_ride(
        body, (nb,),
        [row_blk(S5_W), row_blk(2 * S5_S),
         pl.BlockSpec((SUBLANE, 2 * S5_S), lambda i: (jnp.maximum(rb(i) * r8 - 1, 0), 0)),
         pl.BlockSpec((tb, S5_W), lambda i: (rb(i), ub)), row_blk(S5_W), const(bd_c.shape), const(c_c.shape),
         const((1, S5_S)), const((1, S5_S))],
        [dyb, s, s, proj, du_skip, bd_c, c_c, ar, ai],
        [row_blk(S5_W), const((1, S5_S)), const((1, S5_S)), const(bd_c.shape), const(c_c.shape)],
        [jax.ShapeDtypeStruct((t, S5_W), bf16), jax.ShapeDtypeStruct((1, S5_S), f32),
         jax.ShapeDtypeStruct((1, S5_S), f32), jax.ShapeDtypeStruct(bd_c.shape, f32),
         jax.ShapeDtypeStruct(c_c.shape, f32)],
        name, ride, scratch=[pltpu.VMEM((tb, 2 * S5_S), f32), pltpu.VMEM((1, 2 * S5_S), f32)], sem=("arbitrary",))


def _ada_fwd(c_all, w, b, name):
    d, n = w.shape
    tn = _tile(n, 1536)

    def body(c_ref, w_ref, b_ref, o_ref):
        a = jax.nn.silu(c_ref[...]).astype(bf16)
        o_ref[...] = jnp.dot(a, w_ref[...].astype(bf16), preferred_element_type=f32) + b_ref[...]

    return pl.pallas_call(
        body, name=name, grid=(n // tn,),
        in_specs=[pl.BlockSpec(c_all.shape, lambda j: (0, 0)), pl.BlockSpec((d, tn), lambda j: (0, j)),
                  pl.BlockSpec((1, tn), lambda j: (0, j))],
        out_specs=pl.BlockSpec((c_all.shape[0], tn), lambda j: (0, j)),
        out_shape=jax.ShapeDtypeStruct((c_all.shape[0], n), f32),
        compiler_params=_cp(("parallel",)),
    )(c_all, w, b)


def _ada_bwd(c_all, dm, name):
    d = c_all.shape[1]
    n = dm.shape[1]
    tn = _tile(n, 1536)

    def body(c_ref, dm_ref, o_ref):
        a = jax.nn.silu(c_ref[...]).astype(bf16)
        o_ref[...] = lax.dot_general(a, dm_ref[...].astype(bf16), (((0,), (0,)), ((), ())), preferred_element_type=f32)

    return pl.pallas_call(
        body, name=name, grid=(n // tn,),
        in_specs=[pl.BlockSpec(c_all.shape, lambda j: (0, 0)), pl.BlockSpec((dm.shape[0], tn), lambda j: (0, j))],
        out_specs=pl.BlockSpec((d, tn), lambda j: (0, j)),
        out_shape=jax.ShapeDtypeStruct((d, n), f32),
        compiler_params=_cp(("parallel",)),
    )(c_all, dm)


def _blk_rows(r, c, nbuf, itemsize=4):
    tr = _tile(r, max(SUBLANE, (RW_VMEM_BUDGET // (2 * nbuf * c * itemsize)) // 16 * 16), 16)
    return tr if r % tr == 0 else r


def _cast_bf16(w, name):
    r, c = w.shape
    tr = _blk_rows(r, c, 2)

    def body(w_ref, o_ref):
        o_ref[...] = w_ref[...].astype(bf16)

    return pl.pallas_call(
        body, name=name, grid=(r // tr,), in_specs=[pl.BlockSpec((tr, c), lambda i: (i, 0))],
        out_specs=pl.BlockSpec((tr, c), lambda i: (i, 0)), out_shape=jax.ShapeDtypeStruct((r, c), bf16),
        compiler_params=_cp(("parallel",)),
    )(w)


def _sum_lead(parts, name):
    n, r, c = parts.shape
    tr = _blk_rows(r, c, n + 2)

    def body(p_ref, o_ref):
        acc = p_ref[0].astype(f32)
        for q in range(1, n):
            acc = acc + p_ref[q].astype(f32)
        o_ref[...] = acc

    return pl.pallas_call(
        body, name=name, grid=(r // tr,), in_specs=[pl.BlockSpec((n, tr, c), lambda i: (0, i, 0))],
        out_specs=pl.BlockSpec((tr, c), lambda i: (i, 0)), out_shape=jax.ShapeDtypeStruct((r, c), f32),
        compiler_params=_cp(("parallel",)),
    )(parts)


def _adamw(w, m, v, parts, name):
    r, c = w.shape
    npart = len(parts)
    tr = _blk_rows(r, c, 7 + npart)
    c1 = 1.0 - ADAM_B1 ** ADAM_STEP
    c2 = 1.0 - ADAM_B2 ** ADAM_STEP

    def body(*refs):
        w_ref, m_ref, v_ref = refs[:3]
        g_ref, d_ref, nm_ref, nv_ref = refs[3 + npart:]
        g = refs[3][...].astype(f32)
        for p in refs[4:3 + npart]:
            g = g + p[...].astype(f32)
        nm = ADAM_B1 * m_ref[...] + (1.0 - ADAM_B1) * g
        nv = ADAM_B2 * v_ref[...] + (1.0 - ADAM_B2) * jnp.square(g)
        g_ref[...] = g
        nm_ref[...] = nm
        nv_ref[...] = nv
        d_ref[...] = -ADAM_LR * ((nm / c1) / (jnp.sqrt(nv / c2) + ADAM_EPS) + ADAM_WD * w_ref[...])

    spec = pl.BlockSpec((tr, c), lambda i: (i, 0))
    return pl.pallas_call(
        body, name=name, grid=(r // tr,), in_specs=[spec] * (3 + npart), out_specs=[spec] * 4,
        out_shape=[jax.ShapeDtypeStruct((r, c), f32)] * 4, compiler_params=_cp(("parallel",)),
    )(w, m, v, *parts)


def _ag_small(x_shard, name):
    m_per, n = x_shard.shape

    def body(x_ref, out_ref, send_sems, recv_sems, local_sem):
        x, y, c = lax.axis_index("x"), lax.axis_index("y"), lax.axis_index("c")
        me, sibling = (x, y, c), (x, y, 1 - c)
        chips = [(1 - x, y), (x, 1 - y), (1 - x, 1 - y)]

        def rows(px, py, pc):
            return out_ref.at[pl.ds((4 * px + 2 * py + pc) * m_per, m_per), :]

        def copy(k, block, to, src=None):
            return pltpu.make_async_remote_copy(
                src_ref=rows(*block) if src is None else src, dst_ref=rows(*block),
                send_sem=send_sems.at[k], recv_sem=recv_sems.at[k], device_id=to, device_id_type=MESH)

        mine = pltpu.make_async_copy(x_ref, rows(*me), local_sem)
        mine.start()
        first = [copy(0, me, sibling, src=x_ref)]
        first += [copy(1 + j, me, (*chip, c), src=x_ref) for j, chip in enumerate(chips)]
        for cp in first:
            cp.start()
        passed = [copy(4 + j, (*chip, c), sibling) for j, chip in enumerate(chips)]
        for j, chip in enumerate(chips):
            copy(1 + j, (*chip, c), me).wait_recv()
            passed[j].start()
        copy(0, sibling, me).wait_recv()
        for j, chip in enumerate(chips):
            copy(4 + j, (*chip, 1 - c), me).wait_recv()
        for cp in first + passed:
            cp.wait_send()
        mine.wait()

    return pl.pallas_call(
        body, name=name, out_shape=jax.ShapeDtypeStruct((8 * m_per, n), x_shard.dtype),
        in_specs=[pl.BlockSpec(memory_space=pltpu.VMEM)], out_specs=pl.BlockSpec(memory_space=pltpu.VMEM),
        scratch_shapes=[pltpu.SemaphoreType.DMA((7,)), pltpu.SemaphoreType.DMA((7,)), pltpu.SemaphoreType.DMA],
        compiler_params=pltpu.CompilerParams(vmem_limit_bytes=VMEM_LIMIT),
    )(x_shard)


def _xchg(srcs, scatter, name):
    return _run_ride(_Ride(srcs, scatter), name)


def _run_ride(ride, name):
    n = ride.n

    def body(*refs):
        ride.start(refs[:n], refs[n:2 * n], refs[2 * n:])
        ride.wait(refs[:n], refs[n:2 * n], refs[2 * n:])

    return pl.pallas_call(
        body, name=name, out_shape=ride.out_shape, in_specs=ride.specs, out_specs=ride.specs,
        scratch_shapes=ride.scratch,
    )(*ride.srcs)


class _Ride:
    def __init__(self, srcs, scatter):
        self.srcs, self.scatter, self.n = list(srcs), scatter, len(srcs)
        n = self.n
        self.out_shape = [jax.ShapeDtypeStruct(s.shape if scatter else (4,) + s.shape, s.dtype) for s in srcs]
        self.specs = [pl.BlockSpec(memory_space=pl.ANY)] * n
        self.scratch = [pltpu.SemaphoreType.DMA((3 * n,)), pltpu.SemaphoreType.DMA((3 * n,)),
                        pltpu.SemaphoreType.DMA((n,))]

    def _copies(self, ins, outs, sems):
        send_sems, recv_sems, local_sems = sems
        x, y, c = lax.axis_index("x"), lax.axis_index("y"), lax.axis_index("c")
        my_k = 2 * x + y
        peers = [(1 - x, y), (x, 1 - y), (1 - x, 1 - y)]
        local, sends, recvs = [], [], []
        for a in range(self.n):
            own = ins[a].at[my_k] if self.scatter else ins[a]
            local.append(pltpu.make_async_copy(own, outs[a].at[my_k], local_sems.at[a]))
            for j, (px, py) in enumerate(peers):
                sems_j = dict(send_sem=send_sems.at[3 * a + j], recv_sem=recv_sems.at[3 * a + j],
                              device_id=(px, py, c), device_id_type=MESH)
                src = ins[a].at[2 * px + py] if self.scatter else ins[a]
                sends.append(pltpu.make_async_remote_copy(src_ref=src, dst_ref=outs[a].at[my_k], **sems_j))
                landed = outs[a].at[2 * px + py]
                recvs.append(pltpu.make_async_remote_copy(src_ref=landed, dst_ref=landed, **sems_j))
        return local, sends, recvs

    def start(self, ins, outs, sems):
        local, sends, _ = self._copies(ins, outs, sems)
        for cp in local + sends:
            cp.start()

    def wait(self, ins, outs, sems):
        local, sends, recvs = self._copies(ins, outs, sems)
        for cp in recvs:
            cp.wait_recv()
        for cp in sends:
            cp.wait_send()
        for cp in local:
            cp.wait()


class _RideGather:
    def __init__(self, srcs):
        self.srcs, self.n = list(srcs), len(srcs)
        n = self.n
        assert all(s.shape[0] % 32 == 0 for s in srcs)
        self.out_shape = [jax.ShapeDtypeStruct((4,) + s.shape, s.dtype) for s in srcs]
        self.specs = [pl.BlockSpec(memory_space=pl.ANY)] * n
        dma = pltpu.SemaphoreType.DMA
        self.scratch = [dma((3 * n,)), dma((3 * n,)), dma((3 * n,)), dma((3 * n,)), dma((n,))]

    def _copies(self, ins, outs, sems):
        send_sems, recv_sems, pass_send, pass_recv, local_sems = sems
        x, y, c = lax.axis_index("x"), lax.axis_index("y"), lax.axis_index("c")
        my_k = 2 * x + y
        peers = [(1 - x, y), (x, 1 - y), (1 - x, 1 - y)]
        local, sends, recvs, passes, pass_recvs = [], [], [], [], []
        for a in range(self.n):
            half = self.srcs[a].shape[0] // 2
            mine = pl.ds(pl.multiple_of(c * half, 16), half)
            other = pl.ds(pl.multiple_of((1 - c) * half, 16), half)
            local.append(pltpu.make_async_copy(ins[a], outs[a].at[my_k], local_sems.at[a]))
            for j, (px, py) in enumerate(peers):
                q = 3 * a + j
                over_ici = dict(send_sem=send_sems.at[q], recv_sem=recv_sems.at[q], device_id=(px, py, c),
                                device_id_type=MESH)
                to_sibling = dict(send_sem=pass_send.at[q], recv_sem=pass_recv.at[q], device_id=(x, y, 1 - c),
                                  device_id_type=MESH)
                sends.append(pltpu.make_async_remote_copy(src_ref=ins[a].at[mine], dst_ref=outs[a].at[my_k, mine],
                                                          **over_ici))
                landed = outs[a].at[2 * px + py, mine]
                recvs.append(pltpu.make_async_remote_copy(src_ref=landed, dst_ref=landed, **over_ici))
                passes.append(pltpu.make_async_remote_copy(src_ref=landed, dst_ref=landed, **to_sibling))
                from_sibling = outs[a].at[2 * px + py, other]
                pass_recvs.append(pltpu.make_async_remote_copy(src_ref=from_sibling, dst_ref=from_sibling, **to_sibling))
        return local, sends, recvs, passes, pass_recvs

    def start(self, ins, outs, sems):
        local, sends = self._copies(ins, outs, sems)[:2]
        for cp in local + sends:
            cp.start()

    def wait(self, ins, outs, sems):
        local, sends, recvs, passes, pass_recvs = self._copies(ins, outs, sems)
        for rc, ps in zip(recvs, passes):
            rc.wait_recv()
            ps.start()
        for cp in pass_recvs:
            cp.wait_recv()
        for cp in sends + passes:
            cp.wait_send()
        for cp in local:
            cp.wait()


def _swap_sibling(srcs, name):
    n = len(srcs)

    def body(*refs):
        ins, outs = refs[:n], refs[n:2 * n]
        send_sems, recv_sems = refs[2 * n:]
        sib = (lax.axis_index("x"), lax.axis_index("y"), 1 - lax.axis_index("c"))
        cps = [pltpu.make_async_remote_copy(src_ref=ins[a], dst_ref=outs[a], send_sem=send_sems.at[a],
                                            recv_sem=recv_sems.at[a], device_id=sib, device_id_type=MESH)
               for a in range(n)]
        for cp in cps:
            cp.start()
        for cp in cps:
            cp.wait_recv()
        for cp in cps:
            cp.wait_send()

    anyspec = pl.BlockSpec(memory_space=pl.ANY)
    return pl.pallas_call(
        body, name=name, out_shape=[jax.ShapeDtypeStruct(s.shape, s.dtype) for s in srcs],
        in_specs=[anyspec] * n, out_specs=[anyspec] * n,
        scratch_shapes=[pltpu.SemaphoreType.DMA((n,)), pltpu.SemaphoreType.DMA((n,))],
    )(*srcs)


def _gather8(vec, name):
    size = vec.shape[0]
    n = _round_up(size, SUBLANE * LANE)
    blk = jnp.concatenate([vec, jnp.zeros((n - size,), f32)]).reshape(SUBLANE, n // SUBLANE)
    out = _ag_small(blk, name)
    return out.reshape(8, n)[:, :size]


def _head_spread_matrices():
    e = np.zeros((LANE, SSD_DI), np.float32)
    for h in range(SSD_HEADS):
        e[h, h * SSD_P:(h + 1) * SSD_P] = 1.0
    return jnp.asarray(e, bf16), jnp.asarray(e.T, bf16)


def _heads_to_lanes(v):
    return jnp.repeat(v, SSD_P).reshape(1, SSD_DI)


def _block_diag8(blocks):
    g, r, c = blocks.shape
    b = blocks.reshape(g // S5_NB, S5_NB, r, c)
    eye = jnp.eye(S5_NB, dtype=bool)[None, :, None, :, None]
    return jnp.where(eye, b[:, :, :, None, :], jnp.zeros((), blocks.dtype)).reshape(g * r, S5_NB * c)


def _diag8(mat, r, c):
    g = mat.shape[0] // r
    m = mat.reshape(g // S5_NB, S5_NB, r, S5_NB, c)
    eye = jnp.eye(S5_NB, dtype=bool)[None, :, None, :, None]
    return jnp.where(eye, m, 0.0).sum(axis=3).reshape(g, r, c)


class _Layout:
    def __init__(self, d):
        self.d = d
        self.z, self.xbc, self.u = 0, SSD_DI, SSD_DI + CONV_DIM
        self.ga = self.u + S5_W
        self.gb = self.ga + d
        self.dt = self.gb + d
        self.np_ = self.dt + LANE
        self.in_cols = SSD_DI + CONV_DIM + SSD_HEADS + S5_W + 2 * d
        off_dt = SSD_DI + CONV_DIM
        off_u = off_dt + SSD_HEADS
        off_g = off_u + S5_W
        self.src = [(0, off_dt), (off_u, off_u + S5_W + 2 * d), (off_dt, off_u)]

    def arrange(self, w):
        (a0, a1), (b0, b1), (c0, c1) = self.src
        pad = jnp.zeros((w.shape[0], LANE - SSD_HEADS), w.dtype)
        return jnp.concatenate([w[:, a0:a1], w[:, b0:b1], w[:, c0:c1], pad], axis=1)

    def arrange_slabs(self, g):
        pieces = [p for lo, hi in self.src for p in _cols_from_slabs(g, lo, hi)]
        pieces.append(jnp.zeros((g.shape[1], LANE - SSD_HEADS), g.dtype))
        return jnp.concatenate(pieces, axis=1)

    def restore_slabs(self, w):
        (a0, a1), (b0, b1), (c0, c1) = self.src
        n_a, n_b = a1 - a0, b1 - b0
        segs = [(a0, a1, 0), (c0, c1, n_a + n_b), (b0, b1, n_a)]
        cs = self.in_cols // 4
        slabs = []
        for k in range(4):
            lo, hi = k * cs, (k + 1) * cs
            parts = [w[:, pos + max(lo, s0) - s0:pos + min(hi, s1) - s0] for s0, s1, pos in segs
                     if max(lo, s0) < min(hi, s1)]
            slabs.append(jnp.concatenate(parts, axis=1))
        return jnp.stack(slabs)


def _cols_from_slabs(g, start, stop):
    c = g.shape[2]
    return [g[k][:, max(start, k * c) - k * c:min(stop, (k + 1) * c) - k * c] for k in range(4)
            if max(start, k * c) < min(stop, (k + 1) * c)]


def _unshard_cols(g):
    return jnp.concatenate([g[k] for k in range(4)], axis=1)


def _shard_cols(w):
    r, c4 = w.shape
    return w.reshape(r, 4, c4 // 4).transpose(1, 0, 2)


def kernel(x, c, w_ada, b_ada, norm_ffn1, w_ffn1_in, w_ffn1_out, norm_mix, w_in, conv_w, conv_b, dt_bias, a_log, d_ssd, ssd_norm_w, w_a_proj, s5_lambda_re, s5_lambda_im, s5_b_re, s5_b_im, s5_c_re, s5_c_im, s5_d, s5_log_dt, w_b_glu, w_out, norm_ffn2, w_ffn2_in, w_ffn2_out, norm_final, loss_target, m_w_ada, m_b_ada, m_norm_ffn1, m_w_ffn1_in, m_w_ffn1_out, m_norm_mix, m_w_in, m_conv_w, m_conv_b, m_dt_bias, m_a_log, m_d_ssd, m_ssd_norm_w, m_w_a_proj, m_s5_lambda_re, m_s5_lambda_im, m_s5_b_re, m_s5_b_im, m_s5_c_re, m_s5_c_im, m_s5_d, m_s5_log_dt, m_w_b_glu, m_w_out, m_norm_ffn2, m_w_ffn2_in, m_w_ffn2_out, m_norm_final, v_w_ada, v_b_ada, v_norm_ffn1, v_w_ffn1_in, v_w_ffn1_out, v_norm_mix, v_w_in, v_conv_w, v_conv_b, v_dt_bias, v_a_log, v_d_ssd, v_ssd_norm_w, v_w_a_proj, v_s5_lambda_re, v_s5_lambda_im, v_s5_b_re, v_s5_b_im, v_s5_c_re, v_s5_c_im, v_s5_d, v_s5_log_dt, v_w_b_glu, v_w_out, v_norm_ffn2, v_w_ffn2_in, v_w_ffn2_out, v_norm_final):
    W = dict(w_ada=w_ada, b_ada=b_ada, norm_ffn1=norm_ffn1, w_ffn1_in=w_ffn1_in, w_ffn1_out=w_ffn1_out, norm_mix=norm_mix, w_in=w_in, conv_w=conv_w, conv_b=conv_b, dt_bias=dt_bias, a_log=a_log, d_ssd=d_ssd, ssd_norm_w=ssd_norm_w, w_a_proj=w_a_proj, s5_lambda_re=s5_lambda_re, s5_lambda_im=s5_lambda_im, s5_b_re=s5_b_re, s5_b_im=s5_b_im, s5_c_re=s5_c_re, s5_c_im=s5_c_im, s5_d=s5_d, s5_log_dt=s5_log_dt, w_b_glu=w_b_glu, w_out=w_out, norm_ffn2=norm_ffn2, w_ffn2_in=w_ffn2_in, w_ffn2_out=w_ffn2_out, norm_final=norm_final)
    Mo = dict(w_ada=m_w_ada, b_ada=m_b_ada, norm_ffn1=m_norm_ffn1, w_ffn1_in=m_w_ffn1_in, w_ffn1_out=m_w_ffn1_out, norm_mix=m_norm_mix, w_in=m_w_in, conv_w=m_conv_w, conv_b=m_conv_b, dt_bias=m_dt_bias, a_log=m_a_log, d_ssd=m_d_ssd, ssd_norm_w=m_ssd_norm_w, w_a_proj=m_w_a_proj, s5_lambda_re=m_s5_lambda_re, s5_lambda_im=m_s5_lambda_im, s5_b_re=m_s5_b_re, s5_b_im=m_s5_b_im, s5_c_re=m_s5_c_re, s5_c_im=m_s5_c_im, s5_d=m_s5_d, s5_log_dt=m_s5_log_dt, w_b_glu=m_w_b_glu, w_out=m_w_out, norm_ffn2=m_norm_ffn2, w_ffn2_in=m_w_ffn2_in, w_ffn2_out=m_w_ffn2_out, norm_final=m_norm_final)
    Vo = dict(w_ada=v_w_ada, b_ada=v_b_ada, norm_ffn1=v_norm_ffn1, w_ffn1_in=v_w_ffn1_in, w_ffn1_out=v_w_ffn1_out, norm_mix=v_norm_mix, w_in=v_w_in, conv_w=v_conv_w, conv_b=v_conv_b, dt_bias=v_dt_bias, a_log=v_a_log, d_ssd=v_d_ssd, ssd_norm_w=v_ssd_norm_w, w_a_proj=v_w_a_proj, s5_lambda_re=v_s5_lambda_re, s5_lambda_im=v_s5_lambda_im, s5_b_re=v_s5_b_re, s5_b_im=v_s5_b_im, s5_c_re=v_s5_c_re, s5_c_im=v_s5_c_im, s5_d=v_s5_d, s5_log_dt=v_s5_log_dt, w_b_glu=v_w_b_glu, w_out=v_w_out, norm_ffn2=v_norm_ffn2, w_ffn2_in=v_w_ffn2_in, w_ffn2_out=v_w_ffn2_out, norm_final=v_norm_final)

    t, d = x.shape[1], x.shape[2]
    ff = 4 * w_ffn1_out.shape[1]
    ffp = _round_up(ff, 512)
    lay = _Layout(d)
    xi, yi, ci = lax.axis_index("x"), lax.axis_index("y"), lax.axis_index("c")
    k_me = 2 * xi + yi
    e_me = 4 * xi + 2 * yi + ci
    x2d = x[0]
    tgt = loss_target[0]

    cw_cols = conv_w.shape[2]
    g1 = _gather8(jnp.concatenate([c[0], conv_w[0].reshape(-1)]), "gather_c_convw")
    c_all = g1[:, :d]
    conv_full = g1[::2, d:].reshape(4, CONV_K, cw_cols).transpose(1, 0, 2).reshape(CONV_K, CONV_DIM)
    conv_w8 = jnp.zeros((SUBLANE, CONV_DIM), f32).at[:CONV_K].set(conv_full)

    n_ada_loc = w_ada.shape[2]
    b_loc = lax.dynamic_slice(b_ada, (0, k_me * n_ada_loc), (1, n_ada_loc))
    mods_part = _ada_fwd(c_all, w_ada[0], b_loc, "ada_fwd")
    g2 = _gather8(mods_part.reshape(-1), "gather_mods").reshape(8, 8, n_ada_loc)
    mods = lax.dynamic_index_in_dim(g2[::2], e_me, axis=1, keepdims=False).reshape(N_ADA, d)
    sh1, sc1, gt1, sh2, sc2, gt2, sh3, sc3, gt3 = [mods[i:i + 1] for i in range(N_ADA)]

    cast = {n: _cast_bf16(W[n][0], "cast_" + n) for n in BIG}

    def gather_of(names):
        return _RideGather([cast[n] for n in names])

    def rows_of(g):
        return g.reshape(4 * g.shape[1], g.shape[2])

    def ffn_in(g):
        z = jnp.zeros((g.shape[1], ffp - ff), g.dtype)
        return jnp.concatenate([g[0], g[1], z, g[2], g[3], z], axis=1)

    def ffn_out(g):
        return jnp.concatenate([rows_of(g), jnp.zeros((ffp - ff, g.shape[2]), g.dtype)], axis=0)

    nf1, nmx, nf2 = norm_ffn1, norm_mix, norm_ffn2
    nfin = norm_final.reshape(1, d)

    (g_w1i,) = _run_ride(gather_of(['w_ffn1_in']), "gather_w_ffn1_in")
    w1i = ffn_in(g_w1i)
    (h1,) = _rw(_f_mod, [x2d], [nf1, sh1, sc1], [(d, bf16)], name="mod1")
    ab1, act1, (g_w1o, g_wa, g_wglu, g_wo) = _ffn_in(
        h1, w1i, "ffn1_in", ride=gather_of(['w_ffn1_out', 'w_a_proj', 'w_b_glu', 'w_out']))
    w1o = ffn_out(g_w1o)
    w_a = rows_of(g_wa)
    w_glu, w_o = _unshard_cols(g_wglu), rows_of(g_wo)
    f1, (g_win,) = _mm(act1, w1o, 'nn', out_dtype=bf16, name="ffn1_out", ride=gather_of(['w_in']))
    w_inr = lay.arrange_slabs(g_win)
    res1 = functools.partial(_f_res_mod, 0.5)
    x1, h2 = _rw(res1, [x2d, f1], [gt1, nmx, sh2, sc2], [(d, f32), (d, bf16)], name="res1_mod2")
    proj, (g_w2i,) = _mm(h2, w_inr, 'nn', out_dtype=f32, name="in_proj", ride=gather_of(['w_ffn2_in']))
    w2i = ffn_in(g_w2i)

    pre = _conv_fwd(proj, lay.xbc, conv_w8, conv_b, "conv_fwd")
    spread, spread_t = _head_spread_matrices()
    bias128 = jnp.zeros((1, LANE), f32).at[:, :SSD_HEADS].set(dt_bias)
    xs, bm, cm, dt4 = _rw(_f_ssd_pre, [pre, (proj, lay.dt, LANE)], [bias128, spread, spread_t],
                          [(SSD_DI, f32), (SSD_G * SSD_N, f32), (SSD_G * SSD_N, f32), (SSD_DI, f32)],
                          name="ssd_pre")

    def head_params(a_log_, d_ssd_):
        return _heads_to_lanes(-jnp.exp(a_log_[0])), _heads_to_lanes(d_ssd_[0])

    (a4, dsk4), head_vjp = jax.vjp(head_params, a_log, d_ssd)
    y_ssd, hs, (g_w2o,) = _ssd_fwd(xs, bm, cm, dt4, a4, dsk4, "ssd_fwd", ride=gather_of(['w_ffn2_out']))
    w2o = ffn_out(g_w2o)
    (y_a,) = _rw(_f_ssd_post, [y_ssd, (proj, lay.z, SSD_DI)], [ssd_norm_w], [(SSD_DI, bf16)], name="ssd_post")
    p_a = _mm(y_a, w_a, 'nn', out_dtype=f32, name="a_proj")

    col = lambda v: v.reshape(S5_S, 1)
    ldt_col = jnp.repeat(s5_log_dt[0], S5_P).reshape(S5_S, 1)
    prep_rows = [col(s5_lambda_re[0]), col(s5_lambda_im[0]), ldt_col,
                 s5_b_re[0].reshape(S5_S, S5_I), s5_b_im[0].reshape(S5_S, S5_I)]
    ar, ai, bbr, bbi = _rw(_f_s5_prep, prep_rows, [], [(1, f32), (1, f32), (S5_I, f32), (S5_I, f32)],
                           name="s5_prep", tm=512)
    to_bd = lambda bb: _block_diag8(bb.reshape(S5_G, S5_P, S5_I).transpose(0, 2, 1).astype(bf16))
    bd_c = jnp.concatenate([to_bd(bbr), to_bd(bbi)], axis=1)
    c_c = jnp.concatenate([_block_diag8(s5_c_re[0].transpose(0, 2, 1).astype(bf16)),
                           _block_diag8((-s5_c_im[0]).transpose(0, 2, 1).astype(bf16))], axis=0)
    ar_row, ai_row = ar.reshape(1, S5_S), ai.reshape(1, S5_S)
    s5s, yb = _s5_fwd(proj, lay.u, bd_c, c_c, ar_row, ai_row, "s5_fwd")
    d_row = s5_d[0].reshape(1, S5_W)
    (gl,) = _rw(_f_s5_post, [yb, (proj, lay.u, S5_W)], [d_row], [(S5_W, bf16)], name="s5_post")
    glu = _mm(gl, w_glu, 'nn', out_dtype=f32, name="glu_proj")

    merge_rows = [p_a, glu, (proj, lay.ga, d), (proj, lay.gb, d)]
    (merged,) = _rw(_f_merge, merge_rows, [], [(d, bf16)], name="merge")
    o_mix = _mm(merged, w_o, 'nn', out_dtype=bf16, name="out_proj")
    res2 = functools.partial(_f_res_mod, 1.0)
    x2, h3 = _rw(res2, [x1, o_mix], [gt2, nf2, sh3, sc3], [(d, f32), (d, bf16)], name="res2_mod3")
    ab2, act2, _ = _ffn_in(h3, w2i, "ffn2_in")
    f2 = _mm(act2, w2o, 'nn', out_dtype=bf16, name="ffn2_out")
    (loss_acc,) = _rw(_f_final_loss, [x2, f2, tgt], [gt3, nfin], [], accs=[(1, LANE)], name="loss")
    loss = lax.psum(loss_acc[0, 0], AXES)

    ones = jnp.ones((t, 1), f32)
    dx2, df2, dgt3, dnfin = _rw_vjp(_f_final, [x2, f2, tgt], [gt3, nfin], [ones],
                                    row_grads=[f32, bf16, None], param_grads=[True, True], name="loss_bwd")
    def ffn_in_back(g):
        hf = ff // 2
        return jnp.stack([g[:, :hf], g[:, hf:ff], g[:, ffp:ffp + hf], g[:, ffp + hf:ffp + ff]])

    def rows_back(g, rows):
        return g[:rows].reshape(4, rows // 4, g.shape[1])

    def scatter_of(pairs):
        return _Ride([g for _, g in pairs], True)

    terms = {}
    dab2, _ = _ffn_out_dx(df2, w2o, ab2, "ffn2_out_dx")
    dw2o = _mm(act2, df2, 'tn', out_dtype=bf16, name="ffn2_out_dw")
    dh3, (terms['w_ffn2_out'],) = _ffn_in_dx(dab2, w2i, "ffn2_in_dx",
                                             ride=scatter_of([('w_ffn2_out', rows_back(dw2o, ff))]))
    dw2i = _ffn_in_dw(h3, dab2, "ffn2_in_dw")
    dx1, do_mix, dgt2, dnf2, dsh3, dsc3 = _rw_vjp(
        res2, [x1, o_mix], [gt2, nf2, sh3, sc3], [dx2, dh3], row_grads=[f32, bf16], param_grads=[True] * 4,
        name="res2_mod3_bwd")
    dmerged = _mm(do_mix, w_o, 'nt', out_dtype=bf16, name="out_proj_dx")
    dw_o = _mm(merged, do_mix, 'tn', out_dtype=bf16, name="out_proj_dw")
    dp_a, dglu, dga, dgb = _rw_vjp(_f_merge, merge_rows, [], [dmerged], row_grads=[bf16] * 4,
                                   param_grads=[], name="merge_bwd")

    dgl = _mm(dglu, w_glu, 'nt', out_dtype=bf16, name="glu_proj_dx")
    dw_glu = _mm(gl, dglu, 'tn', out_dtype=bf16, name="glu_proj_dw")
    dyb, du_skip, dd_row = _rw_vjp(_f_s5_post, [yb, (proj, lay.u, S5_W)], [d_row], [dgl],
                                   row_grads=[bf16, f32], param_grads=[True], name="s5_post_bwd")
    (du, dar, dai, dbd_c, dc_c), (terms['w_ffn2_in'],) = _s5_bwd(
        dyb, s5s, proj, lay.u, bd_c, c_c, ar_row, ai_row, du_skip, "s5_bwd",
        ride=scatter_of([('w_ffn2_in', ffn_in_back(dw2i))]))
    from_bd = lambda m_: _diag8(m_, S5_I, S5_P).transpose(0, 2, 1).reshape(S5_S, S5_I)
    dprep = _rw_vjp(_f_s5_prep, prep_rows, [], [dar.reshape(S5_S, 1), dai.reshape(S5_S, 1),
                                                from_bd(dbd_c[:, :S5_SB]), from_bd(dbd_c[:, S5_SB:])],
                    row_grads=[f32] * 5, param_grads=[], name="s5_prep_bwd", tm=512)
    dlr, dli, dldt, dbr, dbi = dprep
    g_s5 = dict(
        s5_lambda_re=dlr.reshape(S5_G, S5_P), s5_lambda_im=dli.reshape(S5_G, S5_P),
        s5_log_dt=dldt.reshape(S5_G, S5_P).sum(axis=1),
        s5_b_re=dbr.reshape(S5_G, S5_P, S5_I), s5_b_im=dbi.reshape(S5_G, S5_P, S5_I),
        s5_c_re=_diag8(dc_c[:S5_S], S5_P, S5_I).transpose(0, 2, 1),
        s5_c_im=-_diag8(dc_c[S5_S:], S5_P, S5_I).transpose(0, 2, 1),
        s5_d=dd_row.reshape(S5_G, S5_I))

    dy_a = _mm(dp_a, w_a, 'nt', out_dtype=bf16, name="a_proj_dx")
    dw_a = _mm(y_a, dp_a, 'tn', out_dtype=bf16, name="a_proj_dw")
    dy_ssd, dz, dssd_nw = _rw_vjp(_f_ssd_post, [y_ssd, (proj, lay.z, SSD_DI)], [ssd_norm_w], [dy_a],
                                  row_grads=[f32, bf16], param_grads=[True], name="ssd_post_bwd")
    early = [('w_out', rows_back(dw_o, d)), ('w_b_glu', _shard_cols(dw_glu)), ('w_a_proj', rows_back(dw_a, SSD_DI))]
    (dxs, dbm, dcm, ddt4, da4, ddsk4), landed = _ssd_bwd(xs, bm, cm, dt4, a4, dsk4, hs, dy_ssd, "ssd_bwd",
                                                         ride=scatter_of(early))
    terms.update({n: p for (n, _), p in zip(early, landed)})
    da_log, dd_ssd = head_vjp((da4, ddsk4))
    dpre, ddt_raw, dbias128 = _rw_vjp(_f_ssd_pre, [pre, (proj, lay.dt, LANE)], [bias128, spread, spread_t],
                                      [dxs, dbm, dcm, ddt4], row_grads=[f32, bf16],
                                      param_grads=[True, False, False], name="ssd_pre_bwd")
    dxbc, dconv_w8, dconv_b = _conv_bwd(dpre, proj, lay.xbc, conv_w8, "conv_bwd")

    dproj = jnp.concatenate([dz, dxbc, du, dga, dgb, ddt_raw], axis=1)
    dw_inr = _mm(h2, dproj, 'tn', out_dtype=bf16, name="in_proj_dw")
    dh2, (terms['w_in'],) = _mm(dproj, w_inr, 'nt', out_dtype=bf16, name="in_proj_dx",
                                ride=scatter_of([('w_in', lay.restore_slabs(dw_inr))]))
    dx0, df1, dgt1, dnmx, dsh2, dsc2 = _rw_vjp(
        res1, [x2d, f1], [gt1, nmx, sh2, sc2], [dx1, dh2], row_grads=[f32, bf16], param_grads=[True] * 4,
        name="res1_mod2_bwd")
    dw1o = _mm(act1, df1, 'tn', out_dtype=bf16, name="ffn1_out_dw")
    dab1, (terms['w_ffn1_out'],) = _ffn_out_dx(df1, w1o, ab1, "ffn1_out_dx",
                                               ride=scatter_of([('w_ffn1_out', rows_back(dw1o, ff))]))
    dw1i = _ffn_in_dw(h1, dab1, "ffn1_in_dw")
    dh1, (terms['w_ffn1_in'],) = _ffn_in_dx(dab1, w1i, "ffn1_in_dx",
                                            ride=scatter_of([('w_ffn1_in', ffn_in_back(dw1i))]))
    grad_x, dnf1, dsh1, dsc1 = _rw_vjp(_f_mod_keep, [x2d], [nf1, sh1, sc1], [dh1, dx0],
                                       row_grads=[f32], param_grads=[True] * 3, name="mod1_bwd")
    d_mods = jnp.concatenate([dsh1, dsc1, dgt1, dsh2, dsc2, dgt2, dsh3, dsc3, dgt3], axis=1).reshape(-1)

    sums = [_sum_lead(terms[n], "sum_" + n) for n in BIG]
    others = _swap_sibling(sums, "swap_sums")

    out_g, out_d, out_m, out_v = {}, {}, {}, {}
    for n, s_own, s_sib in zip(BIG, sums, others):
        r = _adamw(W[n][0], Mo[n][0], Vo[n][0], [s_own, s_sib], "adamw_" + n)
        out_g[n], out_d[n], out_m[n], out_v[n] = [o[None] for o in r]

    local = dict(
        b_ada=d_mods, norm_ffn1=dnf1, norm_mix=dnmx, conv_w=dconv_w8[:CONV_K], conv_b=dconv_b,
        dt_bias=dbias128[:, :SSD_HEADS], a_log=da_log, d_ssd=dd_ssd, ssd_norm_w=dssd_nw,
        norm_ffn2=dnf2, norm_final=dnfin, **g_s5)
    flat = jnp.concatenate([local[n].reshape(-1) for n in SMALL])
    g3 = _gather8(flat, "gather_small_grads")
    n_small = flat.shape[0]
    npad = _round_up(n_small, SUBLANE * LANE)
    g3p = jnp.zeros((8, npad), f32).at[:, :n_small].set(g3).reshape(8, npad // LANE, LANE)
    gsum = _sum_lead(g3p, "sum_small").reshape(-1)

    def local_shard(n, a):
        if n == 'conv_w':
            return lax.dynamic_slice(a.reshape(CONV_K, CONV_DIM), (0, k_me * cw_cols), (CONV_K, cw_cols))
        return a

    pieces, off = {}, 0
    for n in SMALL:
        sz = local[n].size
        pieces[n] = local_shard(n, gsum[off:off + sz]).reshape(W[n].shape)
        off += sz

    def pack(dct):
        v_ = jnp.concatenate([dct[n].reshape(-1) for n in SMALL])
        pad = _round_up(v_.shape[0], SUBLANE * LANE) - v_.shape[0]
        return jnp.concatenate([v_, jnp.ones((pad,), f32)]).reshape(-1, LANE)

    rs = _adamw(pack(W), pack(Mo), pack(Vo), [pack(pieces)], "adamw_small")
    off = 0
    for n in SMALL:
        sz = W[n].size
        out_g[n], out_d[n], out_m[n], out_v[n] = [o.reshape(-1)[off:off + sz].reshape(W[n].shape) for o in rs]
        off += sz

    dm_loc = lax.dynamic_slice(g3[:, :N_ADA * d], (0, k_me * n_ada_loc), (SUBLANE, n_ada_loc))
    g_ada = _ada_bwd(c_all, dm_loc, "ada_bwd")
    r = _adamw(w_ada[0], m_w_ada[0], v_w_ada[0], [g_ada], "adamw_w_ada")
    out_g['w_ada'], out_d['w_ada'], out_m['w_ada'], out_v['w_ada'] = [o[None] for o in r]

    return (loss, grad_x[None], *[out_g[n] for n in WEIGHTS], *[out_d[n] for n in WEIGHTS],
            *[out_m[n] for n in WEIGHTS], *[out_v[n] for n in WEIGHTS])
```

```python
import functools
import math

import numpy as np
import jax
import jax.numpy as jnp
from jax import lax
from jax.experimental import pallas as pl
from jax.experimental.pallas import tpu as pltpu

f32 = jnp.float32
bf16 = jnp.bfloat16
HI = lax.Precision.HIGHEST
MESH = pl.DeviceIdType.MESH
AXES = ("x", "y", "c")

EPS = 1e-6
SSD_HEADS, SSD_P, SSD_N, SSD_G, SSD_R, SSD_L = 32, 64, 128, 4, 8, 128
SSD_DI = SSD_HEADS * SSD_P
CONV_K = 4
CONV_DIM = SSD_DI + 2 * SSD_G * SSD_N
S5_W, S5_G, S5_I, S5_P = 1024, 64, 16, 64
S5_S = S5_G * S5_P
N_ADA = 9
ADAM_LR, ADAM_B1, ADAM_B2, ADAM_EPS, ADAM_WD, ADAM_STEP = 0.001, 0.9, 0.999, 1e-08, 0.01, 10

LANE = 128
SUBLANE = 8
VMEM_LIMIT = 56 << 20
MM_VMEM_BUDGET = 40 << 20
RW_VMEM_BUDGET = 36 << 20

WEIGHTS = ['w_ada', 'b_ada', 'norm_ffn1', 'w_ffn1_in', 'w_ffn1_out', 'norm_mix', 'w_in', 'conv_w', 'conv_b', 'dt_bias',
           'a_log', 'd_ssd', 'ssd_norm_w', 'w_a_proj', 's5_lambda_re', 's5_lambda_im', 's5_b_re', 's5_b_im', 's5_c_re',
           's5_c_im', 's5_d', 's5_log_dt', 'w_b_glu', 'w_out', 'norm_ffn2', 'w_ffn2_in', 'w_ffn2_out', 'norm_final']
BIG = ['w_ffn1_in', 'w_ffn1_out', 'w_in', 'w_a_proj', 'w_b_glu', 'w_out', 'w_ffn2_in', 'w_ffn2_out']
COL_SHARDED = ('w_ffn1_in', 'w_in', 'w_b_glu', 'w_ffn2_in')
SMALL = [n for n in WEIGHTS if n not in BIG and n != 'w_ada']


def _cp(sem=None):
    return pltpu.CompilerParams(dimension_semantics=sem, vmem_limit_bytes=VMEM_LIMIT)


def _tile(dim, target, align=LANE):
    if dim <= target:
        return dim
    t = (target // align) * align
    while t >= align:
        if dim % t == 0:
            return t
        t -= align
    return dim


def _round_up(n, m):
    return (n + m - 1) // m * m


def _mm(a, b, mode, *, out_dtype, name, a_win=None, b_win=None, add=None, ride=None):
    a0, aw = a_win or (0, a.shape[1])
    b0, bw = b_win or (0, b.shape[1])
    if mode == 'nn':
        m, k, n = a.shape[0], aw, bw
        assert b.shape[0] == k
    elif mode == 'nt':
        m, k, n = a.shape[0], aw, b.shape[0]
        assert bw == k
    else:
        k, m, n = a.shape[0], aw, bw
        assert b.shape[0] == k
    osz = jnp.dtype(out_dtype).itemsize
    tm, tn, tk = 1024, 1152, 3456
    while True:
        bm = _tile(math.gcd(m, a0) if (mode == 'tn' and a0) else m, tm)
        bn = _tile(math.gcd(n, b0) if (mode != 'nt' and b0) else n, tn)
        kk = k
        if mode != 'tn' and a0:
            kk = math.gcd(kk, a0)
        if mode == 'nt' and b0:
            kk = math.gcd(kk, b0)
        bk = _tile(kk, tk)
        need = 2 * (bm * bk * a.dtype.itemsize + bk * bn * b.dtype.itemsize + bm * bn * osz) + bm * bn * 4
        if add is not None:
            need += 2 * bm * bn * add.dtype.itemsize
        if need <= MM_VMEM_BUDGET or (tm <= 256 and tn <= 256 and tk <= 512):
            break
        if tk > 1024:
            tk //= 2
        elif tm >= tn:
            tm //= 2
        else:
            tn //= 2
    nk = k // bk
    assert m % bm == 0 and n % bn == 0 and k % bk == 0, (name, m, n, k, bm, bn, bk)
    if mode == 'nn':
        ao, bo = a0 // bk, b0 // bn
        a_blk, a_map = (bm, bk), lambda i, j, q: (i, q + ao)
        b_blk, b_map = (bk, bn), lambda i, j, q: (q, j + bo)
    elif mode == 'nt':
        ao, bo = a0 // bk, b0 // bk
        a_blk, a_map = (bm, bk), lambda i, j, q: (i, q + ao)
        b_blk, b_map = (bn, bk), lambda i, j, q: (j, q + bo)
    else:
        ao, bo = a0 // bm, b0 // bn
        a_blk, a_map = (bk, bm), lambda i, j, q: (q, i + ao)
        b_blk, b_map = (bk, bn), lambda i, j, q: (q, j + bo)
    return _mm_core(a, b, mode, grid=(m // bm, n // bn, nk), a_blk=a_blk, a_map=a_map, b_blk=b_blk, b_map=b_map,
                    o_blk=(bm, bn), o_map=lambda i, j, q: (i, j), out_shape=(m, n), out_dtype=out_dtype, name=name,
                    add=add, ride=ride)


def _mm_core(a, b, mode, *, grid, a_blk, a_map, b_blk, b_map, o_blk, o_map, out_shape, out_dtype, name,
             add=None, ride=None):
    dims = {'nn': (((1,), (0,)), ((), ())), 'nt': (((1,), (1,)), ((), ())), 'tn': (((0,), (0,)), ((), ()))}[mode]
    nk = grid[-1]
    has_add = add is not None
    nr = ride.n if ride is not None else 0

    def body(*refs):
        a_ref, b_ref = refs[0], refs[1]
        pos = 2
        add_ref = refs[pos] if has_add else None
        pos += int(has_add)
        r_ins = refs[pos:pos + nr]
        o_ref = refs[pos + nr]
        r_outs = refs[pos + nr + 1:pos + 2 * nr + 1]
        acc_ref = refs[pos + 2 * nr + 1]
        r_sems = refs[pos + 2 * nr + 2:]
        ids = [pl.program_id(ax) for ax in range(len(grid))]
        q = ids[-1]
        if nr:
            @pl.when(functools.reduce(lambda u, v: u & v, [i == 0 for i in ids]))
            def _():
                ride.start(r_ins, r_outs, r_sems)

        @pl.when(q == 0)
        def _():
            acc_ref[...] = jnp.zeros_like(acc_ref)

        acc_ref[...] += lax.dot_general(a_ref[...].astype(bf16), b_ref[...].astype(bf16), dims,
                                        preferred_element_type=f32)

        @pl.when(q == nk - 1)
        def _():
            r = acc_ref[...]
            if has_add:
                r = r + add_ref[...].astype(f32)
            o_ref[...] = r.astype(out_dtype)

        if nr:
            @pl.when(functools.reduce(lambda u, v: u & v, [i == g - 1 for i, g in zip(ids, grid)]))
            def _():
                ride.wait(r_ins, r_outs, r_sems)

    in_specs = [pl.BlockSpec(a_blk, a_map), pl.BlockSpec(b_blk, b_map)]
    ops = [a, b]
    if has_add:
        in_specs.append(pl.BlockSpec(o_blk, o_map))
        ops.append(add)
    out_specs = [pl.BlockSpec(o_blk, o_map)]
    out_shapes = [jax.ShapeDtypeStruct(out_shape, out_dtype)]
    scratch = [pltpu.VMEM(o_blk, f32)]
    if nr:
        in_specs += ride.specs
        ops += ride.srcs
        out_specs += ride.specs
        out_shapes += ride.out_shape
        scratch += ride.scratch
    sem = ("arbitrary",) * len(grid) if nr else ("parallel",) * (len(grid) - 1) + ("arbitrary",)
    res = pl.pallas_call(
        body, name=name, grid=grid, in_specs=in_specs, out_specs=out_specs, out_shape=out_shapes,
        scratch_shapes=scratch, compiler_params=_cp(sem),
    )(*ops)
    return (res[0], list(res[1:])) if nr else res[0]


def _swiglu(a, b):
    return jax.nn.silu(a) * b


def _ride_parts(refs, n_in, n_out, ride):
    nr = ride.n if ride is not None else 0
    ins = refs[:n_in]
    r_ins = refs[n_in:n_in + nr]
    outs = refs[n_in + nr:n_in + nr + n_out]
    r_outs = refs[n_in + nr + n_out:n_in + 2 * nr + n_out]
    return ins, r_ins, outs, r_outs, refs[n_in + 2 * nr + n_out:]


def _call_with_ride(body_core, grid, in_specs, ops, out_specs, out_shape, name, ride, scratch=(), sem=None):
    nr = ride.n if ride is not None else 0
    n_in, n_out, n_scr = len(in_specs), len(out_specs), len(scratch)

    def body(*refs):
        ins, r_ins, outs, r_outs, rest = _ride_parts(refs, n_in, n_out, ride)
        ids = [pl.program_id(ax) for ax in range(len(grid))]
        if nr:
            @pl.when(functools.reduce(lambda u, v: u & v, [i == 0 for i in ids]))
            def _():
                ride.start(r_ins, r_outs, rest[n_scr:])
        body_core(ins, outs, *rest[:n_scr])
        if nr:
            @pl.when(functools.reduce(lambda u, v: u & v, [i == g - 1 for i, g in zip(ids, grid)]))
            def _():
                ride.wait(r_ins, r_outs, rest[n_scr:])

    res = pl.pallas_call(
        body, name=name, grid=grid, in_specs=in_specs + (ride.specs if nr else []),
        out_specs=out_specs + (ride.specs if nr else []), out_shape=out_shape + (ride.out_shape if nr else []),
        scratch_shapes=list(scratch) + (ride.scratch if nr else []),
        compiler_params=_cp(("arbitrary",) * len(grid) if nr else (sem or ("parallel",) * len(grid))),
    )(*ops, *(ride.srcs if nr else []))
    return list(res[:n_out]), list(res[n_out:])


def _ffn_in(h, w, name, ride=None):
    t, d = h.shape
    ffp = w.shape[1] // 2
    bm, bn = _tile(t, 1024), _tile(ffp, 512)
    nb = ffp // bn
    nn = (((1,), (0,)), ((), ()))

    def core(ins, outs):
        h_ref, wa_ref, wb_ref = ins
        ab_ref, act_ref = outs
        hv = h_ref[...].astype(bf16)
        a = lax.dot_general(hv, wa_ref[...].astype(bf16), nn, preferred_element_type=f32)
        b = lax.dot_general(hv, wb_ref[...].astype(bf16), nn, preferred_element_type=f32)
        ab_ref[0] = a.astype(bf16)
        ab_ref[1] = b.astype(bf16)
        act_ref[...] = _swiglu(a, b).astype(bf16)

    (ab, act), landed = _call_with_ride(
        core, (t // bm, nb),
        [pl.BlockSpec((bm, d), lambda i, j: (i, 0)), pl.BlockSpec((d, bn), lambda i, j: (0, j)),
         pl.BlockSpec((d, bn), lambda i, j: (0, j + nb))], [h, w, w],
        [pl.BlockSpec((2, bm, bn), lambda i, j: (0, i, j)), pl.BlockSpec((bm, bn), lambda i, j: (i, j))],
        [jax.ShapeDtypeStruct((2, t, ffp), bf16), jax.ShapeDtypeStruct((t, ffp), bf16)], name, ride)
    return ab, act, landed


def _ffn_out_dx(df, wo, ab, name, ride=None):
    t, d = df.shape
    ffp = wo.shape[0]
    bm, bn = _tile(t, 1024), _tile(ffp, 512)
    nt = (((1,), (1,)), ((), ()))

    def core(ins, outs):
        df_ref, wo_ref, ab_ref = ins
        dact = lax.dot_general(df_ref[...].astype(bf16), wo_ref[...].astype(bf16), nt, preferred_element_type=f32)
        _, vjp = jax.vjp(_swiglu, ab_ref[0].astype(f32), ab_ref[1].astype(f32))
        da, db = vjp(dact)
        outs[0][0] = da.astype(bf16)
        outs[0][1] = db.astype(bf16)

    (dab,), landed = _call_with_ride(
        core, (t // bm, ffp // bn),
        [pl.BlockSpec((bm, d), lambda i, j: (i, 0)), pl.BlockSpec((bn, d), lambda i, j: (j, 0)),
         pl.BlockSpec((2, bm, bn), lambda i, j: (0, i, j))], [df, wo, ab],
        [pl.BlockSpec((2, bm, bn), lambda i, j: (0, i, j))], [jax.ShapeDtypeStruct((2, t, ffp), bf16)], name, ride)
    return dab, landed


def _ffn_in_dx(dab, w, name, ride=None):
    _, t, ffp = dab.shape
    d = w.shape[0]
    bm, bn, bk = _tile(t, 1024), _tile(d, 1024), _tile(ffp, 2816)
    nkh = ffp // bk
    return _mm_core(dab, w, 'nt', grid=(t // bm, d // bn, 2 * nkh),
                    a_blk=(None, bm, bk), a_map=lambda i, j, q: (q // nkh, i, q % nkh),
                    b_blk=(bn, bk), b_map=lambda i, j, q: (j, q),
                    o_blk=(bm, bn), o_map=lambda i, j, q: (i, j),
                    out_shape=(t, d), out_dtype=bf16, name=name, ride=ride)


def _ffn_in_dw(h, dab, name):
    _, t, ffp = dab.shape
    d = h.shape[1]
    bm, bn, bk = _tile(d, 1024), _tile(ffp, 1408), _tile(t, 2048)
    nbh = ffp // bn
    return _mm_core(h, dab, 'tn', grid=(d // bm, 2 * nbh, t // bk),
                    a_blk=(bk, bm), a_map=lambda i, j, q: (q, i),
                    b_blk=(None, bk, bn), b_map=lambda i, j, q: (j // nbh, q, j % nbh),
                    o_blk=(bm, bn), o_map=lambda i, j, q: (i, j),
                    out_shape=(d, 2 * ffp), out_dtype=bf16, name=name)


def _piece_blocks(pieces):
    bk = min(1024, functools.reduce(math.gcd, [p.shape[1] for p in pieces]))
    starts, n = [], 0
    for p in pieces:
        starts.append(n)
        n += p.shape[1] // bk
    return bk, starts, n


def _in_proj_dx(pieces, tail, tail_col, w, name, ride=None):
    t, d = pieces[0].shape[0], w.shape[0]
    bk, starts, nq = _piece_blocks(pieces)
    assert tail_col == nq * bk and tail.shape[1] == LANE
    bm, bn = _tile(t, 1024), _tile(d, 1024)
    npc = len(pieces)

    def core(ins, outs, acc_ref):
        a_refs, tail_ref, b_ref, bt_ref = ins[:npc], ins[npc], ins[npc + 1], ins[npc + 2]
        q = pl.program_id(2)

        @pl.when(q == 0)
        def _():
            acc_ref[...] = jnp.zeros_like(acc_ref)

        for a_ref, s0, p in zip(a_refs, starts, pieces):
            @pl.when((q >= s0) & (q < s0 + p.shape[1] // bk))
            def _(a_ref=a_ref):
                acc_ref[...] += lax.dot_general(a_ref[...].astype(bf16), b_ref[...], _NT, preferred_element_type=f32)

        @pl.when(q == nq)
        def _():
            r = acc_ref[...] + lax.dot_general(tail_ref[...].astype(bf16), bt_ref[...], _NT,
                                               preferred_element_type=f32)
            outs[0][...] = r.astype(bf16)

    def a_spec(s0, p):
        last = p.shape[1] // bk - 1
        return pl.BlockSpec((bm, bk), lambda i, j, q: (i, jnp.clip(q - s0, 0, last)))

    in_specs = [a_spec(s0, p) for s0, p in zip(starts, pieces)] + [
        pl.BlockSpec((bm, LANE), lambda i, j, q: (i, 0)),
        pl.BlockSpec((bn, bk), lambda i, j, q: (j, jnp.minimum(q, nq - 1))),
        pl.BlockSpec((bn, LANE), lambda i, j, q: (j, tail_col // LANE))]
    (dh,), landed = _call_with_ride(
        core, (t // bm, d // bn, nq + 1), in_specs, list(pieces) + [tail, w, w],
        [pl.BlockSpec((bm, bn), lambda i, j, q: (i, j))], [jax.ShapeDtypeStruct((t, d), bf16)], name, ride,
        scratch=[pltpu.VMEM((bm, bn), f32)], sem=("parallel", "parallel", "arbitrary"))
    return dh, landed


def _in_proj_dw(h, pieces, name):
    t, d = h.shape
    bn, starts, nj = _piece_blocks(pieces)
    bm, bk = _tile(d, 1024), _tile(t, 1024)
    nk = t // bk
    npc = len(pieces)

    def core(ins, outs, acc_ref):
        h_ref, b_refs = ins[0], ins[1:]
        j, q = pl.program_id(1), pl.program_id(2)

        @pl.when(q == 0)
        def _():
            acc_ref[...] = jnp.zeros_like(acc_ref)

        for b_ref, s0, p in zip(b_refs, starts, pieces):
            @pl.when((j >= s0) & (j < s0 + p.shape[1] // bn))
            def _(b_ref=b_ref):
                acc_ref[...] += lax.dot_general(h_ref[...], b_ref[...].astype(bf16), _TN, preferred_element_type=f32)

        @pl.when(q == nk - 1)
        def _():
            outs[0][...] = acc_ref[...].astype(bf16)

    def b_spec(s0, p):
        last = p.shape[1] // bn - 1
        return pl.BlockSpec((bk, bn), lambda i, j, q: (q, jnp.clip(j - s0, 0, last)))

    (dw,), _ = _call_with_ride(
        core, (d // bm, nj, nk), [pl.BlockSpec((bk, bm), lambda i, j, q: (q, i))] + [
            b_spec(s0, p) for s0, p in zip(starts, pieces)], [h] + list(pieces),
        [pl.BlockSpec((bm, bn), lambda i, j, q: (i, j))], [jax.ShapeDtypeStruct((d, nj * bn), bf16)], name, None,
        scratch=[pltpu.VMEM((bm, bn), f32)], sem=("parallel", "parallel", "arbitrary"))
    return dw


def _win(r):
    return r if isinstance(r, tuple) else (r, 0, r.shape[1])


def _row_tile(t, widths):
    per_row = 48 * max(widths)
    tm = 512
    while tm > SUBLANE and tm * per_row > RW_VMEM_BUDGET:
        tm //= 2
    return min(tm, t)


def _row_spec(r, tm):
    arr, c0, w = _win(r)
    assert c0 % w == 0, (c0, w)
    cb = c0 // w
    return pl.BlockSpec((tm, w), lambda i: (i, cb))


def _full_spec(p):
    nd = p.ndim
    return pl.BlockSpec(p.shape, lambda i: (0,) * nd)


def _rw(f, rows, params, outs, *, name, accs=(), tm=None):
    t = _win(rows[0])[0].shape[0]
    tm = tm or _row_tile(t, [_win(r)[2] for r in rows] + [w for w, _ in outs])
    nr, npar, no, na = len(rows), len(params), len(outs), len(accs)

    def body(*refs):
        vals = [r[...] for r in refs[:nr + npar]]
        res = f(*vals)
        res = res if isinstance(res, (tuple, list)) else (res,)
        for o_ref, v in zip(refs[nr + npar:nr + npar + no], res[:no]):
            o_ref[...] = v.astype(o_ref.dtype)
        if na:
            first = pl.program_id(0) == 0
            for a_ref, v in zip(refs[nr + npar + no:], res[no:]):
                @pl.when(first)
                def _(a_ref=a_ref):
                    a_ref[...] = jnp.zeros_like(a_ref)
                a_ref[...] += v

    out_shape = [jax.ShapeDtypeStruct((t, w), d) for w, d in outs] + [jax.ShapeDtypeStruct(s, f32) for s in accs]
    out_specs = [pl.BlockSpec((tm, w), lambda i: (i, 0)) for w, _ in outs] + \
                [pl.BlockSpec(s, lambda i: (0, 0)) for s in accs]
    return pl.pallas_call(
        body, name=name, grid=(t // tm,),
        in_specs=[_row_spec(r, tm) for r in rows] + [_full_spec(p) for p in params],
        out_specs=out_specs, out_shape=out_shape,
        compiler_params=_cp(("arbitrary",)),
    )(*[_win(r)[0] for r in rows], *params)


def _rw_vjp(f, rows, params, cots, *, row_grads, param_grads, name, tm=None):
    t = _win(rows[0])[0].shape[0]
    cot_rows = [c for c in cots if c is not None]
    tm = tm or _row_tile(t, [_win(r)[2] for r in rows] + [_win(c)[2] for c in cot_rows])
    nr, npar, ncot = len(rows), len(params), len(cot_rows)
    d_rows = [i for i, d in enumerate(row_grads) if d is not None]
    d_pars = [i for i, d in enumerate(param_grads) if d]

    def body(*refs):
        rv = [r[...] for r in refs[:nr]]
        pv = [r[...] for r in refs[nr:nr + npar]]
        cv = [r[...] for r in refs[nr + npar:nr + npar + ncot]]
        outs_r = refs[nr + npar + ncot:nr + npar + ncot + len(d_rows)]
        outs_p = refs[nr + npar + ncot + len(d_rows):]

        def g(*diff):
            rr, pp = list(rv), list(pv)
            for i, v in zip(d_rows, diff[:len(d_rows)]):
                rr[i] = v
            for i, v in zip(d_pars, diff[len(d_rows):]):
                pp[i] = v
            res = f(*rr, *pp)
            return tuple(res) if isinstance(res, (tuple, list)) else (res,)

        prim, vjp = jax.vjp(g, *[rv[i] for i in d_rows], *[pv[i] for i in d_pars])
        it = iter(cv)
        cts = tuple(next(it).astype(o.dtype) if c is not None else jnp.zeros_like(o) for o, c in zip(prim, cots))
        grads = vjp(cts)
        for o_ref, v in zip(outs_r, grads[:len(d_rows)]):
            o_ref[...] = v.astype(o_ref.dtype)
        first = pl.program_id(0) == 0
        for o_ref, v in zip(outs_p, grads[len(d_rows):]):
            @pl.when(first)
            def _(o_ref=o_ref):
                o_ref[...] = jnp.zeros_like(o_ref)
            o_ref[...] += v.astype(f32)

    out_shape = [jax.ShapeDtypeStruct((t, _win(rows[i])[2]), row_grads[i]) for i in d_rows] + \
                [jax.ShapeDtypeStruct(params[i].shape, f32) for i in d_pars]
    out_specs = [pl.BlockSpec((tm, _win(rows[i])[2]), lambda i_: (i_, 0)) for i in d_rows] + \
                [_full_spec(params[i]) for i in d_pars]
    return pl.pallas_call(
        body, name=name, grid=(t // tm,),
        in_specs=[_row_spec(r, tm) for r in rows] + [_full_spec(p) for p in params] + [_row_spec(c, tm) for c in cot_rows],
        out_specs=out_specs, out_shape=out_shape,
        compiler_params=_cp(("arbitrary",)),
    )(*[_win(r)[0] for r in rows], *params, *[_win(c)[0] for c in cot_rows])


def _rms(x, g):
    return x * lax.rsqrt(jnp.mean(x * x, axis=-1, keepdims=True) + EPS) * g


def _f_mod(x, nw, sh, sc):
    return (_rms(x, nw) * (1.0 + sc) + sh).astype(bf16)


def _f_mod_keep(x, nw, sh, sc):
    return _f_mod(x, nw, sh, sc), x


def _f_res_mod(coef, x, o, g, nw, sh, sc):
    x1 = x + coef * g * o.astype(f32)
    return x1, _f_mod(x1, nw, sh, sc)


def _times01(x, e):
    hi = x.astype(bf16)
    r1 = x - hi.astype(f32)
    mid = r1.astype(bf16)
    lo = (r1 - mid.astype(f32)).astype(bf16)
    return (jnp.dot(hi, e, preferred_element_type=f32) + jnp.dot(mid, e, preferred_element_type=f32) +
            jnp.dot(lo, e, preferred_element_type=f32))


@jax.custom_vjp
def _spread_heads(x, e, et):
    return _times01(x, e)


_spread_heads.defvjp(lambda x, e, et: (_times01(x, e), (e, et)),
                     lambda res, g: (_times01(g, res[1]), None, None))


def _f_ssd_pre(pre, dtraw, bias, e, et):
    xc = jax.nn.silu(pre)
    dtx = _spread_heads(jax.nn.softplus(dtraw + bias), e, et)
    return xc[:, :SSD_DI], xc[:, SSD_DI:SSD_DI + SSD_G * SSD_N], xc[:, SSD_DI + SSD_G * SSD_N:], dtx


def _f_ssd_post(y, z, nw):
    yz = y * jax.nn.silu(z)
    w = SSD_DI // SSD_G
    parts = []
    for g in range(SSD_G):
        s = yz[:, g * w:(g + 1) * w]
        parts.append(s * lax.rsqrt(jnp.mean(s * s, axis=-1, keepdims=True) + EPS))
    return (jnp.concatenate(parts, axis=1) * nw).astype(bf16)


def _f_s5_post(yb, u, d):
    return jax.nn.gelu(yb + d * u).astype(bf16)


def _f_merge(pa, glu, ga, gb):
    d = pa.shape[1]
    glu = glu.astype(f32)
    pb = glu[:, :d] * jax.nn.sigmoid(glu[:, d:])
    return (jax.nn.sigmoid(ga) * pa.astype(f32) + jax.nn.sigmoid(gb) * pb).astype(bf16)


def _f_final(x2, o, tgt, g, nw):
    x3 = x2 + 0.5 * g * o.astype(f32)
    y = _rms(x3, nw)
    return 0.5 * jnp.mean(jnp.square(y - tgt), axis=-1, keepdims=True)


def _f_final_loss(x2, o, tgt, g, nw):
    rows = _f_final(x2, o, tgt, g, nw)
    return jnp.broadcast_to(jnp.sum(rows, axis=0, keepdims=True), (1, LANE))


def _f_s5_prep(lr, li, ldt, br, bi):
    dt = jnp.exp(ldt)
    lr = jnp.minimum(lr, -1e-4)
    mag = jnp.exp(lr * dt)
    ar = mag * jnp.cos(li * dt)
    ai = mag * jnp.sin(li * dt)
    den = lr * lr + li * li
    nr = ar - 1.0
    kr = (nr * lr + ai * li) / den
    ki = (ai * lr - nr * li) / den
    return ar, ai, kr * br - ki * bi, kr * bi + ki * br


def _shift_down(cur, halo8, j):
    if j == 0:
        return cur
    rolled = pltpu.roll(cur, j, 0)
    row8 = lax.broadcasted_iota(jnp.int32, halo8.shape, 0)
    top = jnp.where(row8 < j, pltpu.roll(halo8, j, 0), rolled[:SUBLANE])
    return jnp.concatenate([top, rolled[SUBLANE:]], axis=0)


def _shift_up(cur, halo8, j):
    if j == 0:
        return cur
    n = cur.shape[0]
    rolled = pltpu.roll(cur, n - j, 0)
    row8 = lax.broadcasted_iota(jnp.int32, halo8.shape, 0)
    bot = jnp.where(row8 >= SUBLANE - j, pltpu.roll(halo8, SUBLANE - j, 0), rolled[n - SUBLANE:])
    return jnp.concatenate([rolled[:n - SUBLANE], bot], axis=0)


def _conv_fwd(proj, c0, w8, b, name):
    t = proj.shape[0]
    cw = 1024
    tm = min(512, t)
    cb0 = c0 // cw
    r8 = tm // SUBLANE

    def body(x_ref, h_ref, w_ref, b_ref, o_ref):
        i = pl.program_id(1)
        x = x_ref[...]
        halo = jnp.where(i > 0, h_ref[...], 0.0)
        acc = b_ref[...] + w_ref[CONV_K - 1:CONV_K, :] * x
        for j in range(1, CONV_K):
            acc = acc + w_ref[CONV_K - 1 - j:CONV_K - j, :] * _shift_down(x, halo, j)
        o_ref[...] = acc

    return pl.pallas_call(
        body, name=name, grid=(CONV_DIM // cw, t // tm),
        in_specs=[pl.BlockSpec((tm, cw), lambda c, i: (i, cb0 + c)),
                  pl.BlockSpec((SUBLANE, cw), lambda c, i: (jnp.maximum(i * r8 - 1, 0), cb0 + c)),
                  pl.BlockSpec((SUBLANE, cw), lambda c, i: (0, c)),
                  pl.BlockSpec((1, cw), lambda c, i: (0, c))],
        out_specs=pl.BlockSpec((tm, cw), lambda c, i: (i, c)),
        out_shape=jax.ShapeDtypeStruct((t, CONV_DIM), f32),
        compiler_params=_cp(("parallel", "arbitrary")),
    )(proj, proj, w8, b)


def _conv_bwd(dpre, proj, c0, w8, name):
    t = proj.shape[0]
    cw = 1024
    tm = min(512, t)
    cb0 = c0 // cw
    r8 = tm // SUBLANE
    nb = t // tm

    def body(d_ref, dn_ref, x_ref, xh_ref, w_ref, dx_ref, dw_ref, db_ref):
        i = pl.program_id(1)
        d = d_ref[...]
        dn = jnp.where(i < nb - 1, dn_ref[...], 0.0)
        x = x_ref[...]
        xh = jnp.where(i > 0, xh_ref[...], 0.0)

        @pl.when(i == 0)
        def _():
            dw_ref[...] = jnp.zeros_like(dw_ref)
            db_ref[...] = jnp.zeros_like(db_ref)

        dx = w_ref[CONV_K - 1:CONV_K, :] * d
        rows = [jnp.sum(d * x, axis=0, keepdims=True)]
        for j in range(1, CONV_K):
            dx = dx + w_ref[CONV_K - 1 - j:CONV_K - j, :] * _shift_up(d, dn, j)
            rows.append(jnp.sum(d * _shift_down(x, xh, j), axis=0, keepdims=True))
        dx_ref[...] = dx.astype(dx_ref.dtype)
        dw = jnp.concatenate([rows[CONV_K - 1 - k] for k in range(CONV_K)] +
                             [jnp.zeros((SUBLANE - CONV_K, cw), f32)], axis=0)
        dw_ref[...] += dw
        db_ref[...] += jnp.sum(d, axis=0, keepdims=True)

    return pl.pallas_call(
        body, name=name, grid=(CONV_DIM // cw, nb),
        in_specs=[pl.BlockSpec((tm, cw), lambda c, i: (i, c)),
                  pl.BlockSpec((SUBLANE, cw), lambda c, i: (jnp.minimum((i + 1) * r8, nb * r8 - 1), c)),
                  pl.BlockSpec((tm, cw), lambda c, i: (i, cb0 + c)),
                  pl.BlockSpec((SUBLANE, cw), lambda c, i: (jnp.maximum(i * r8 - 1, 0), cb0 + c)),
                  pl.BlockSpec((SUBLANE, cw), lambda c, i: (0, c))],
        out_specs=[pl.BlockSpec((tm, cw), lambda c, i: (i, c)),
                   pl.BlockSpec((SUBLANE, cw), lambda c, i: (0, c)),
                   pl.BlockSpec((1, cw), lambda c, i: (0, c))],
        out_shape=[jax.ShapeDtypeStruct((t, CONV_DIM), bf16), jax.ShapeDtypeStruct((SUBLANE, CONV_DIM), f32),
                   jax.ShapeDtypeStruct((1, CONV_DIM), f32)],
        compiler_params=_cp(("parallel", "arbitrary")),
    )(dpre, dpre, proj, proj, w8)


def _cumsum_rows_impl(x):
    n = x.shape[0]
    row = lax.broadcasted_iota(jnp.int32, x.shape, 0)
    s = 1
    while s < n:
        x = x + jnp.where(row >= s, pltpu.roll(x, s, 0), 0.0)
        s *= 2
    return x


@jax.custom_vjp
def _cumsum_rows(x):
    return _cumsum_rows_impl(x)


def _cumsum_rows_bwd(_, g):
    c = _cumsum_rows_impl(g)
    return (c[c.shape[0] - 1:, :] - c + g,)


_cumsum_rows.defvjp(lambda x: (_cumsum_rows_impl(x), None), _cumsum_rows_bwd)


@jax.custom_vjp
def _swap_halves(t):
    return pltpu.roll(t, LANE // 2, 1)


_swap_halves.defvjp(lambda t: (pltpu.roll(t, LANE // 2, 1), None), lambda _, g: (pltpu.roll(g, LANE // 2, 1),))


def _ssd_chunk(xs, bm, cm, dtx, ax, dskx, ht):
    n = SSD_L
    assert n == LANE and SSD_P * 2 == LANE
    row = lax.broadcasted_iota(jnp.int32, (n, n), 0)
    col = lax.broadcasted_iota(jnp.int32, (n, n), 1)
    causal = row >= col
    lo = col < SSD_P
    cs = _cumsum_rows(dtx * ax)
    xdt = xs * dtx
    last = cs[n - 1:n, :]
    cb = lax.dot_general(cm.astype(bf16), bm.astype(bf16), (((1,), (1,)), ((), ())), preferred_element_type=f32)
    y_off = jnp.dot(cm.astype(bf16), ht.astype(bf16), preferred_element_type=f32) * jnp.exp(cs)
    st = lax.dot_general(bm.astype(bf16), (xdt * jnp.exp(last - cs)).astype(bf16), (((0,), (0,)), ((), ())),
                         preferred_element_type=f32)
    ht_new = jnp.exp(last) * ht + st
    ys = []
    for q in range(SSD_R // 2):
        tq = cs[:, q * LANE:(q + 1) * LANE]
        sw = _swap_halves(tq)
        tqt = tq.T
        xq = xdt[:, q * LANE:(q + 1) * LANE].astype(bf16)
        pair = []
        for c_col, r_row in ((jnp.where(lo, tq, sw), tqt[0:1, :]), (jnp.where(lo, sw, tq), tqt[SSD_P:SSD_P + 1, :])):
            decay = jnp.exp(jnp.where(causal, c_col - r_row, -1e30))
            pair.append(jnp.dot((cb * decay).astype(bf16), xq, preferred_element_type=f32))
        ys.append(jnp.where(lo, pair[0], pair[1]))
    return jnp.concatenate(ys, axis=1) + y_off + dskx * xs, ht_new


SSD_GB = 1


def _ssd_specs(nc, rev):
    ch = (lambda c: nc - 1 - c) if rev else (lambda c: c)
    gw = SSD_GB * SSD_R * SSD_P
    return [pl.BlockSpec((SSD_L, gw), lambda g, c: (ch(c), g)),
            pl.BlockSpec((SSD_L, SSD_GB * SSD_N), lambda g, c: (ch(c), g)),
            pl.BlockSpec((SSD_L, SSD_GB * SSD_N), lambda g, c: (ch(c), g)),
            pl.BlockSpec((SSD_L, gw), lambda g, c: (ch(c), g)),
            pl.BlockSpec((1, gw), lambda g, c: (0, g)),
            pl.BlockSpec((1, gw), lambda g, c: (0, g))]


def _ssd_group(refs, q):
    gw = SSD_R * SSD_P
    xs_ref, bm_ref, cm_ref, dt_ref, a_ref, dsk_ref = refs
    ln = slice(q * LANE, (q + 1) * LANE)
    wd = slice(q * gw, (q + 1) * gw)
    return (xs_ref[:, wd], bm_ref[:, ln], cm_ref[:, ln], dt_ref[:, wd], a_ref[:, wd], dsk_ref[:, wd])


def _ssd_fwd(xs, bm, cm, dt4, a4, dsk4, name, ride=None):
    t = xs.shape[0]
    nc = t // SSD_L
    gw = SSD_R * SSD_P

    nr = ride.n if ride is not None else 0
    ng = SSD_G // SSD_GB

    def body(*refs):
        xs_ref, bm_ref, cm_ref, dt_ref, a_ref, dsk_ref = refs[:6]
        r_ins = refs[6:6 + nr]
        y_ref, hs_ref = refs[6 + nr:8 + nr]
        r_outs = refs[8 + nr:8 + 2 * nr]
        h_ref = refs[8 + 2 * nr]
        r_sems = refs[9 + 2 * nr:]
        g, c = pl.program_id(0), pl.program_id(1)
        if nr:
            @pl.when((g == 0) & (c == 0))
            def _():
                ride.start(r_ins, r_outs, r_sems)

        @pl.when(c == 0)
        def _():
            h_ref[...] = jnp.zeros_like(h_ref)

        hs_ref[...] = h_ref[...]
        grp = (xs_ref, bm_ref, cm_ref, dt_ref, a_ref, dsk_ref)
        ops = [_ssd_group(grp, q) + (h_ref[:, q * gw:(q + 1) * gw],) for q in range(SSD_GB)]
        res = [_ssd_chunk(*o) for o in ops]
        for q, (y, hn) in enumerate(res):
            y_ref[:, q * gw:(q + 1) * gw] = y
            h_ref[:, q * gw:(q + 1) * gw] = hn

        if nr:
            @pl.when((g == ng - 1) & (c == nc - 1))
            def _():
                ride.wait(r_ins, r_outs, r_sems)

    res = pl.pallas_call(
        body, name=name, grid=(ng, nc), in_specs=_ssd_specs(nc, False) + (ride.specs if nr else []),
        out_specs=[pl.BlockSpec((SSD_L, SSD_GB * gw), lambda g, c: (c, g)),
                   pl.BlockSpec((None, None, SSD_N, SSD_GB * gw), lambda g, c: (g, c, 0, 0))] +
                  (ride.specs if nr else []),
        out_shape=[jax.ShapeDtypeStruct((t, SSD_DI), f32),
                   jax.ShapeDtypeStruct((ng, nc, SSD_N, SSD_GB * gw), f32)] + (ride.out_shape if nr else []),
        scratch_shapes=[pltpu.VMEM((SSD_N, SSD_GB * gw), f32)] + (ride.scratch if nr else []),
        compiler_params=_cp(("arbitrary", "arbitrary")),
    )(xs, bm, cm, dt4, a4, dsk4, *(ride.srcs if nr else []))
    return res[0], res[1], list(res[2:])


def _ssd_bwd(xs, bm, cm, dt4, a4, dsk4, hs, dy, name, ride=None):
    t = xs.shape[0]
    nc = t // SSD_L
    gw = SSD_R * SSD_P
    rc = lambda c: nc - 1 - c
    nr = ride.n if ride is not None else 0
    ng = SSD_G // SSD_GB

    def body(*refs):
        xs_ref, bm_ref, cm_ref, dt_ref, a_ref, dsk_ref, hs_ref, dy_ref = refs[:8]
        r_ins = refs[8:8 + nr]
        dxs_ref, dbm_ref, dcm_ref, ddt_ref, da_ref, ddsk_ref = refs[8 + nr:14 + nr]
        r_outs = refs[14 + nr:14 + 2 * nr]
        dh_ref = refs[14 + 2 * nr]
        r_sems = refs[15 + 2 * nr:]
        if nr:
            @pl.when((pl.program_id(0) == 0) & (pl.program_id(1) == 0))
            def _():
                ride.start(r_ins, r_outs, r_sems)

        @pl.when(pl.program_id(1) == 0)
        def _():
            dh_ref[...] = jnp.zeros_like(dh_ref)
            da_ref[...] = jnp.zeros_like(da_ref)
            ddsk_ref[...] = jnp.zeros_like(ddsk_ref)

        grp = (xs_ref, bm_ref, cm_ref, dt_ref, a_ref, dsk_ref)
        ops = [_ssd_group(grp, q) + (hs_ref[:, q * gw:(q + 1) * gw],) for q in range(SSD_GB)]
        cts = [(dy_ref[:, q * gw:(q + 1) * gw], dh_ref[:, q * gw:(q + 1) * gw]) for q in range(SSD_GB)]
        grads = [jax.vjp(_ssd_chunk, *o)[1](ct) for o, ct in zip(ops, cts)]
        for q, (dxs, dbm, dcm, ddt, da, ddsk, dh) in enumerate(grads):
            wd = slice(q * gw, (q + 1) * gw)
            ln = slice(q * LANE, (q + 1) * LANE)
            dxs_ref[:, wd] = dxs
            dbm_ref[:, ln] = dbm
            dcm_ref[:, ln] = dcm
            ddt_ref[:, wd] = ddt
            da_ref[:, wd] += da
            ddsk_ref[:, wd] += ddsk
            dh_ref[:, wd] = dh

        if nr:
            @pl.when((pl.program_id(0) == ng - 1) & (pl.program_id(1) == nc - 1))
            def _():
                ride.wait(r_ins, r_outs, r_sems)

    res = pl.pallas_call(
        body, name=name, grid=(ng, nc),
        in_specs=_ssd_specs(nc, True) + [
            pl.BlockSpec((None, None, SSD_N, SSD_GB * gw), lambda g, c: (g, rc(c), 0, 0)),
            pl.BlockSpec((SSD_L, SSD_GB * gw), lambda g, c: (rc(c), g))] + (ride.specs if nr else []),
        out_specs=[pl.BlockSpec((SSD_L, SSD_GB * gw), lambda g, c: (rc(c), g)),
                   pl.BlockSpec((SSD_L, SSD_GB * SSD_N), lambda g, c: (rc(c), g)),
                   pl.BlockSpec((SSD_L, SSD_GB * SSD_N), lambda g, c: (rc(c), g)),
                   pl.BlockSpec((SSD_L, SSD_GB * gw), lambda g, c: (rc(c), g)),
                   pl.BlockSpec((1, SSD_GB * gw), lambda g, c: (0, g)),
                   pl.BlockSpec((1, SSD_GB * gw), lambda g, c: (0, g))] + (ride.specs if nr else []),
        out_shape=[jax.ShapeDtypeStruct((t, SSD_DI), f32), jax.ShapeDtypeStruct((t, SSD_G * SSD_N), f32),
                   jax.ShapeDtypeStruct((t, SSD_G * SSD_N), f32), jax.ShapeDtypeStruct((t, SSD_DI), f32),
                   jax.ShapeDtypeStruct((1, SSD_DI), f32), jax.ShapeDtypeStruct((1, SSD_DI), f32)] +
                  (ride.out_shape if nr else []),
        scratch_shapes=[pltpu.VMEM((SSD_N, SSD_GB * gw), f32)] + (ride.scratch if nr else []),
        compiler_params=_cp(("arbitrary", "arbitrary")),
    )(xs, bm, cm, dt4, a4, dsk4, hs, dy, *(ride.srcs if nr else []))
    return list(res[:6]), list(res[6:])


S5_CH = 1024


S5_NB = 8
S5_UB = 128
S5_SB = 512
_NT = (((1,), (1,)), ((), ()))
_TN = (((0,), (0,)), ((), ()))


def _s5_fwd(proj, u0, bd_c, c_c, ar, ai, name):
    t = proj.shape[0]
    tb = min(128, t)
    ub = u0 // S5_W

    def body(u_ref, bd_ref, cc_ref, ar_ref, ai_ref, s_ref, yb_ref, bu_ref, carry):
        @pl.when(pl.program_id(0) == 0)
        def _():
            carry[...] = jnp.zeros_like(carry)

        u = u_ref[...].astype(bf16)
        for j in range(S5_NB):
            uj = u[:, j * S5_UB:(j + 1) * S5_UB]
            for half in range(2):
                bu_ref[:, half * S5_S + j * S5_SB:half * S5_S + (j + 1) * S5_SB] = jnp.dot(
                    uj, bd_ref[j * S5_UB:(j + 1) * S5_UB, half * S5_SB:(half + 1) * S5_SB], preferred_element_type=f32)

        for c0 in range(0, S5_S, S5_CH):
            re = pl.ds(c0, S5_CH)
            im = pl.ds(S5_S + c0, S5_CH)
            a_r = ar_ref[:, re]
            a_i = ai_ref[:, re]

            def step(k, st, re=re, im=im, a_r=a_r, a_i=a_i):
                sr, si = st
                row = pl.ds(k, 1)
                nr = a_r * sr - a_i * si + bu_ref[row, re]
                ni = a_r * si + a_i * sr + bu_ref[row, im]
                s_ref[row, re] = nr
                s_ref[row, im] = ni
                return nr, ni

            sr, si = lax.fori_loop(0, tb, step, (carry[:, re], carry[:, im]))
            carry[:, re] = sr
            carry[:, im] = si

        for j in range(S5_NB):
            lo, hi = j * S5_SB, (j + 1) * S5_SB
            yb_ref[:, j * S5_UB:(j + 1) * S5_UB] = (
                jnp.dot(s_ref[:, lo:hi].astype(bf16), cc_ref[lo:hi, :], preferred_element_type=f32) +
                jnp.dot(s_ref[:, S5_S + lo:S5_S + hi].astype(bf16), cc_ref[S5_S + lo:S5_S + hi, :],
                        preferred_element_type=f32))

    return pl.pallas_call(
        body, name=name, grid=(t // tb,),
        in_specs=[pl.BlockSpec((tb, S5_W), lambda i: (i, ub)), _full_spec(bd_c), _full_spec(c_c),
                  pl.BlockSpec((1, S5_S), lambda i: (0, 0)), pl.BlockSpec((1, S5_S), lambda i: (0, 0))],
        out_specs=[pl.BlockSpec((tb, 2 * S5_S), lambda i: (i, 0)), pl.BlockSpec((tb, S5_W), lambda i: (i, 0))],
        out_shape=[jax.ShapeDtypeStruct((t, 2 * S5_S), f32), jax.ShapeDtypeStruct((t, S5_W), f32)],
        scratch_shapes=[pltpu.VMEM((tb, 2 * S5_S), f32), pltpu.VMEM((1, 2 * S5_S), f32)],
        compiler_params=_cp(("arbitrary",)),
    )(proj, bd_c, c_c, ar, ai)


def _s5_bwd(dyb, s, proj, u0, bd_c, c_c, ar, ai, du_skip, name, ride=None):
    t = dyb.shape[0]
    tb = min(128, t)
    nb = t // tb
    r8 = tb // SUBLANE
    rb = lambda i: nb - 1 - i
    ub = u0 // S5_W

    def body(ins, outs, g_ref, carry):
        dyb_ref, s_ref, sh_ref, u_ref, skip_ref, bd_ref, cc_ref, ar_ref, ai_ref = ins
        du_ref, dar_ref, dai_ref, dbd_ref, dcc_ref = outs
        ds_ref = g_ref
        i = pl.program_id(0)

        @pl.when(i == 0)
        def _():
            carry[...] = jnp.zeros_like(carry)
            dar_ref[...] = jnp.zeros_like(dar_ref)
            dai_ref[...] = jnp.zeros_like(dai_ref)
            dbd_ref[...] = jnp.zeros_like(dbd_ref)
            dcc_ref[...] = jnp.zeros_like(dcc_ref)

        dyb = dyb_ref[...].astype(bf16)
        for jj in range(2 * S5_NB):
            blk = jj % S5_NB
            g_ref[:, jj * S5_SB:(jj + 1) * S5_SB] = lax.dot_general(
                dyb[:, blk * S5_UB:(blk + 1) * S5_UB], cc_ref[jj * S5_SB:(jj + 1) * S5_SB, :], _NT,
                preferred_element_type=f32)

        has_prev = (i < nb - 1).astype(f32)
        for c0 in range(0, S5_S, S5_CH):
            re = pl.ds(c0, S5_CH)
            im = pl.ds(S5_S + c0, S5_CH)
            a_r = ar_ref[:, re]
            a_i = ai_ref[:, re]

            def upd(st, row, sp_r, sp_i, re=re, im=im, a_r=a_r, a_i=a_i):
                gr, gi, acr, aci = st
                ngr = ds_ref[row, re] + a_r * gr + a_i * gi
                ngi = ds_ref[row, im] + a_r * gi - a_i * gr
                g_ref[row, re] = ngr
                g_ref[row, im] = ngi
                return ngr, ngi, acr + ngr * sp_r + ngi * sp_i, aci + ngi * sp_r - ngr * sp_i

            def step(k, st, re=re, im=im, upd=upd):
                tt = tb - 1 - k
                prev = pl.ds(tt - 1, 1)
                return upd(st, pl.ds(tt, 1), s_ref[prev, re], s_ref[prev, im])

            zero = jnp.zeros((1, S5_CH), f32)
            st = lax.fori_loop(0, tb - 1, step, (carry[:, re], carry[:, im], zero, zero))
            last = pl.ds(SUBLANE - 1, 1)
            gr, gi, acr, aci = upd(st, pl.ds(0, 1), sh_ref[last, re] * has_prev, sh_ref[last, im] * has_prev)
            carry[:, re] = gr
            carry[:, im] = gi
            dar_ref[:, re] += acr
            dai_ref[:, re] += aci

        u = u_ref[...].astype(bf16)
        for j in range(S5_NB):
            lo, hi = j * S5_SB, (j + 1) * S5_SB
            blk = slice(j * S5_UB, (j + 1) * S5_UB)
            g_re = g_ref[:, lo:hi].astype(bf16)
            g_im = g_ref[:, S5_S + lo:S5_S + hi].astype(bf16)
            du = (lax.dot_general(g_re, bd_ref[blk, :S5_SB], _NT, preferred_element_type=f32) +
                  lax.dot_general(g_im, bd_ref[blk, S5_SB:], _NT, preferred_element_type=f32) + skip_ref[:, blk])
            du_ref[:, blk] = du.astype(du_ref.dtype)
            dbd_ref[blk, :S5_SB] += lax.dot_general(u[:, blk], g_re, _TN, preferred_element_type=f32)
            dbd_ref[blk, S5_SB:] += lax.dot_general(u[:, blk], g_im, _TN, preferred_element_type=f32)
            dcc_ref[lo:hi, :] += lax.dot_general(s_ref[:, lo:hi].astype(bf16), dyb[:, blk], _TN,
                                                 preferred_element_type=f32)
            dcc_ref[S5_S + lo:S5_S + hi, :] += lax.dot_general(s_ref[:, S5_S + lo:S5_S + hi].astype(bf16), dyb[:, blk],
                                                               _TN, preferred_element_type=f32)

    row_blk = lambda w: pl.BlockSpec((tb, w), lambda i: (rb(i), 0))
    const = lambda shape: pl.BlockSpec(shape, lambda i: (0, 0))
    return _call_with_ride(
        body, (nb,),
        [row_blk(S5_W), row_blk(2 * S5_S),
         pl.BlockSpec((SUBLANE, 2 * S5_S), lambda i: (jnp.maximum(rb(i) * r8 - 1, 0), 0)),
         pl.BlockSpec((tb, S5_W), lambda i: (rb(i), ub)), row_blk(S5_W), const(bd_c.shape), const(c_c.shape),
         const((1, S5_S)), const((1, S5_S))],
        [dyb, s, s, proj, du_skip, bd_c, c_c, ar, ai],
        [row_blk(S5_W), const((1, S5_S)), const((1, S5_S)), const(bd_c.shape), const(c_c.shape)],
        [jax.ShapeDtypeStruct((t, S5_W), bf16), jax.ShapeDtypeStruct((1, S5_S), f32),
         jax.ShapeDtypeStruct((1, S5_S), f32), jax.ShapeDtypeStruct(bd_c.shape, f32),
         jax.ShapeDtypeStruct(c_c.shape, f32)],
        name, ride, scratch=[pltpu.VMEM((tb, 2 * S5_S), f32), pltpu.VMEM((1, 2 * S5_S), f32)], sem=("arbitrary",))


def _ada_fwd(c_all, w, b, name):
    d, n = w.shape
    tn = _tile(n, 1536)

    def body(c_ref, w_ref, b_ref, o_ref):
        a = jax.nn.silu(c_ref[...]).astype(bf16)
        o_ref[...] = jnp.dot(a, w_ref[...].astype(bf16), preferred_element_type=f32) + b_ref[...]

    return pl.pallas_call(
        body, name=name, grid=(n // tn,),
        in_specs=[pl.BlockSpec(c_all.shape, lambda j: (0, 0)), pl.BlockSpec((d, tn), lambda j: (0, j)),
                  pl.BlockSpec((1, tn), lambda j: (0, j))],
        out_specs=pl.BlockSpec((c_all.shape[0], tn), lambda j: (0, j)),
        out_shape=jax.ShapeDtypeStruct((c_all.shape[0], n), f32),
        compiler_params=_cp(("parallel",)),
    )(c_all, w, b)


def _ada_bwd(c_all, dm, name):
    d = c_all.shape[1]
    n = dm.shape[1]
    tn = _tile(n, 1536)

    def body(c_ref, dm_ref, o_ref):
        a = jax.nn.silu(c_ref[...]).astype(bf16)
        o_ref[...] = lax.dot_general(a, dm_ref[...].astype(bf16), (((0,), (0,)), ((), ())), preferred_element_type=f32)

    return pl.pallas_call(
        body, name=name, grid=(n // tn,),
        in_specs=[pl.BlockSpec(c_all.shape, lambda j: (0, 0)), pl.BlockSpec((dm.shape[0], tn), lambda j: (0, j))],
        out_specs=pl.BlockSpec((d, tn), lambda j: (0, j)),
        out_shape=jax.ShapeDtypeStruct((d, n), f32),
        compiler_params=_cp(("parallel",)),
    )(c_all, dm)


def _blk_rows(r, c, nbuf, itemsize=4):
    tr = _tile(r, max(SUBLANE, (RW_VMEM_BUDGET // (2 * nbuf * c * itemsize)) // 16 * 16), 16)
    return tr if r % tr == 0 else r


def _cast_bf16(w, name):
    r, c = w.shape
    tr = _blk_rows(r, c, 2)

    def body(w_ref, o_ref):
        o_ref[...] = w_ref[...].astype(bf16)

    return pl.pallas_call(
        body, name=name, grid=(r // tr,), in_specs=[pl.BlockSpec((tr, c), lambda i: (i, 0))],
        out_specs=pl.BlockSpec((tr, c), lambda i: (i, 0)), out_shape=jax.ShapeDtypeStruct((r, c), bf16),
        compiler_params=_cp(("parallel",)),
    )(w)


def _sum_lead(parts, name):
    n, r, c = parts.shape
    tr = _blk_rows(r, c, n + 2)

    def body(p_ref, o_ref):
        acc = p_ref[0].astype(f32)
        for q in range(1, n):
            acc = acc + p_ref[q].astype(f32)
        o_ref[...] = acc

    return pl.pallas_call(
        body, name=name, grid=(r // tr,), in_specs=[pl.BlockSpec((n, tr, c), lambda i: (0, i, 0))],
        out_specs=pl.BlockSpec((tr, c), lambda i: (i, 0)), out_shape=jax.ShapeDtypeStruct((r, c), f32),
        compiler_params=_cp(("parallel",)),
    )(parts)


def _adamw(w, m, v, parts, name):
    r, c = w.shape
    npart = len(parts)
    tr = _blk_rows(r, c, 7 + npart)
    c1 = 1.0 - ADAM_B1 ** ADAM_STEP
    c2 = 1.0 - ADAM_B2 ** ADAM_STEP

    def body(*refs):
        w_ref, m_ref, v_ref = refs[:3]
        g_ref, d_ref, nm_ref, nv_ref = refs[3 + npart:]
        g = refs[3][...].astype(f32)
        for p in refs[4:3 + npart]:
            g = g + p[...].astype(f32)
        nm = ADAM_B1 * m_ref[...] + (1.0 - ADAM_B1) * g
        nv = ADAM_B2 * v_ref[...] + (1.0 - ADAM_B2) * jnp.square(g)
        g_ref[...] = g
        nm_ref[...] = nm
        nv_ref[...] = nv
        d_ref[...] = -ADAM_LR * ((nm / c1) / (jnp.sqrt(nv / c2) + ADAM_EPS) + ADAM_WD * w_ref[...])

    spec = pl.BlockSpec((tr, c), lambda i: (i, 0))
    return pl.pallas_call(
        body, name=name, grid=(r // tr,), in_specs=[spec] * (3 + npart), out_specs=[spec] * 4,
        out_shape=[jax.ShapeDtypeStruct((r, c), f32)] * 4, compiler_params=_cp(("parallel",)),
    )(w, m, v, *parts)


def _ag_small(x_shard, name):
    m_per, n = x_shard.shape

    def body(x_ref, out_ref, send_sems, recv_sems, local_sem):
        x, y, c = lax.axis_index("x"), lax.axis_index("y"), lax.axis_index("c")
        me, sibling = (x, y, c), (x, y, 1 - c)
        chips = [(1 - x, y), (x, 1 - y), (1 - x, 1 - y)]

        def rows(px, py, pc):
            return out_ref.at[pl.ds((4 * px + 2 * py + pc) * m_per, m_per), :]

        def copy(k, block, to, src=None):
            return pltpu.make_async_remote_copy(
                src_ref=rows(*block) if src is None else src, dst_ref=rows(*block),
                send_sem=send_sems.at[k], recv_sem=recv_sems.at[k], device_id=to, device_id_type=MESH)

        mine = pltpu.make_async_copy(x_ref, rows(*me), local_sem)
        mine.start()
        first = [copy(0, me, sibling, src=x_ref)]
        first += [copy(1 + j, me, (*chip, c), src=x_ref) for j, chip in enumerate(chips)]
        for cp in first:
            cp.start()
        passed = [copy(4 + j, (*chip, c), sibling) for j, chip in enumerate(chips)]
        for j, chip in enumerate(chips):
            copy(1 + j, (*chip, c), me).wait_recv()
            passed[j].start()
        copy(0, sibling, me).wait_recv()
        for j, chip in enumerate(chips):
            copy(4 + j, (*chip, 1 - c), me).wait_recv()
        for cp in first + passed:
            cp.wait_send()
        mine.wait()

    return pl.pallas_call(
        body, name=name, out_shape=jax.ShapeDtypeStruct((8 * m_per, n), x_shard.dtype),
        in_specs=[pl.BlockSpec(memory_space=pltpu.VMEM)], out_specs=pl.BlockSpec(memory_space=pltpu.VMEM),
        scratch_shapes=[pltpu.SemaphoreType.DMA((7,)), pltpu.SemaphoreType.DMA((7,)), pltpu.SemaphoreType.DMA],
        compiler_params=pltpu.CompilerParams(vmem_limit_bytes=VMEM_LIMIT),
    )(x_shard)


def _run_ride(ride, name):
    n = ride.n

    def body(*refs):
        ride.start(refs[:n], refs[n:2 * n], refs[2 * n:])
        ride.wait(refs[:n], refs[n:2 * n], refs[2 * n:])

    return pl.pallas_call(
        body, name=name, out_shape=ride.out_shape, in_specs=ride.specs, out_specs=ride.specs,
        scratch_shapes=ride.scratch,
    )(*ride.srcs)


class _Ride:
    def __init__(self, srcs, scatter):
        self.srcs, self.scatter, self.n = list(srcs), scatter, len(srcs)
        n = self.n
        self.out_shape = [jax.ShapeDtypeStruct(s.shape if scatter else (4,) + s.shape, s.dtype) for s in srcs]
        self.specs = [pl.BlockSpec(memory_space=pl.ANY)] * n
        self.scratch = [pltpu.SemaphoreType.DMA((3 * n,)), pltpu.SemaphoreType.DMA((3 * n,)),
                        pltpu.SemaphoreType.DMA((n,))]

    def _copies(self, ins, outs, sems):
        send_sems, recv_sems, local_sems = sems
        x, y, c = lax.axis_index("x"), lax.axis_index("y"), lax.axis_index("c")
        my_k = 2 * x + y
        peers = [(1 - x, y), (x, 1 - y), (1 - x, 1 - y)]
        local, sends, recvs = [], [], []
        for a in range(self.n):
            own = ins[a].at[my_k] if self.scatter else ins[a]
            local.append(pltpu.make_async_copy(own, outs[a].at[my_k], local_sems.at[a]))
            for j, (px, py) in enumerate(peers):
                sems_j = dict(send_sem=send_sems.at[3 * a + j], recv_sem=recv_sems.at[3 * a + j],
                              device_id=(px, py, c), device_id_type=MESH)
                src = ins[a].at[2 * px + py] if self.scatter else ins[a]
                sends.append(pltpu.make_async_remote_copy(src_ref=src, dst_ref=outs[a].at[my_k], **sems_j))
                landed = outs[a].at[2 * px + py]
                recvs.append(pltpu.make_async_remote_copy(src_ref=landed, dst_ref=landed, **sems_j))
        return local, sends, recvs

    def start(self, ins, outs, sems):
        local, sends, _ = self._copies(ins, outs, sems)
        for cp in local + sends:
            cp.start()

    def wait(self, ins, outs, sems):
        local, sends, recvs = self._copies(ins, outs, sems)
        for cp in recvs:
            cp.wait_recv()
        for cp in sends:
            cp.wait_send()
        for cp in local:
            cp.wait()


class _RideGather:
    def __init__(self, srcs):
        self.srcs, self.n = list(srcs), len(srcs)
        n = self.n
        assert all(s.shape[0] % 32 == 0 for s in srcs)
        self.out_shape = [jax.ShapeDtypeStruct((4,) + s.shape, s.dtype) for s in srcs]
        self.specs = [pl.BlockSpec(memory_space=pl.ANY)] * n
        dma = pltpu.SemaphoreType.DMA
        self.scratch = [dma((3 * n,)), dma((3 * n,)), dma((3 * n,)), dma((3 * n,)), dma((n,))]

    def _copies(self, ins, outs, sems):
        send_sems, recv_sems, pass_send, pass_recv, local_sems = sems
        x, y, c = lax.axis_index("x"), lax.axis_index("y"), lax.axis_index("c")
        my_k = 2 * x + y
        peers = [(1 - x, y), (x, 1 - y), (1 - x, 1 - y)]
        local, sends, recvs, passes, pass_recvs = [], [], [], [], []
        for a in range(self.n):
            half = self.srcs[a].shape[0] // 2
            mine = pl.ds(pl.multiple_of(c * half, 16), half)
            other = pl.ds(pl.multiple_of((1 - c) * half, 16), half)
            local.append(pltpu.make_async_copy(ins[a], outs[a].at[my_k], local_sems.at[a]))
            for j, (px, py) in enumerate(peers):
                q = 3 * a + j
                over_ici = dict(send_sem=send_sems.at[q], recv_sem=recv_sems.at[q], device_id=(px, py, c),
                                device_id_type=MESH)
                to_sibling = dict(send_sem=pass_send.at[q], recv_sem=pass_recv.at[q], device_id=(x, y, 1 - c),
                                  device_id_type=MESH)
                sends.append(pltpu.make_async_remote_copy(src_ref=ins[a].at[mine], dst_ref=outs[a].at[my_k, mine],
                                                          **over_ici))
                landed = outs[a].at[2 * px + py, mine]
                recvs.append(pltpu.make_async_remote_copy(src_ref=landed, dst_ref=landed, **over_ici))
                passes.append(pltpu.make_async_remote_copy(src_ref=landed, dst_ref=landed, **to_sibling))
                from_sibling = outs[a].at[2 * px + py, other]
                pass_recvs.append(pltpu.make_async_remote_copy(src_ref=from_sibling, dst_ref=from_sibling, **to_sibling))
        return local, sends, recvs, passes, pass_recvs

    def start(self, ins, outs, sems):
        local, sends = self._copies(ins, outs, sems)[:2]
        for cp in local + sends:
            cp.start()

    def wait(self, ins, outs, sems):
        local, sends, recvs, passes, pass_recvs = self._copies(ins, outs, sems)
        for rc, ps in zip(recvs, passes):
            rc.wait_recv()
            ps.start()
        for cp in pass_recvs:
            cp.wait_recv()
        for cp in sends + passes:
            cp.wait_send()
        for cp in local:
            cp.wait()


class _RideSwap:
    def __init__(self, srcs):
        self.srcs, self.n = list(srcs), len(srcs)
        self.out_shape = [jax.ShapeDtypeStruct(s.shape, s.dtype) for s in srcs]
        self.specs = [pl.BlockSpec(memory_space=pl.ANY)] * self.n
        self.scratch = [pltpu.SemaphoreType.DMA((self.n,)), pltpu.SemaphoreType.DMA((self.n,))]

    def _copies(self, ins, outs, sems):
        send_sems, recv_sems = sems
        sib = (lax.axis_index("x"), lax.axis_index("y"), 1 - lax.axis_index("c"))
        return [pltpu.make_async_remote_copy(src_ref=ins[a], dst_ref=outs[a], send_sem=send_sems.at[a],
                                             recv_sem=recv_sems.at[a], device_id=sib, device_id_type=MESH)
                for a in range(self.n)]

    def start(self, ins, outs, sems):
        for cp in self._copies(ins, outs, sems):
            cp.start()

    def wait(self, ins, outs, sems):
        cps = self._copies(ins, outs, sems)
        for cp in cps:
            cp.wait_recv()
        for cp in cps:
            cp.wait_send()


class _Rides:
    def __init__(self, rides):
        self.rides = list(rides)
        self.n = sum(r.n for r in self.rides)
        self.srcs = [s for r in self.rides for s in r.srcs]
        self.out_shape = [s for r in self.rides for s in r.out_shape]
        self.specs = [s for r in self.rides for s in r.specs]
        self.scratch = [s for r in self.rides for s in r.scratch]

    def _each(self, ins, outs, sems):
        i = k = 0
        for r in self.rides:
            yield r, ins[i:i + r.n], outs[i:i + r.n], sems[k:k + len(r.scratch)]
            i += r.n
            k += len(r.scratch)

    def start(self, ins, outs, sems):
        for r, a, b, c in self._each(ins, outs, sems):
            r.start(a, b, c)

    def wait(self, ins, outs, sems):
        for r, a, b, c in self._each(ins, outs, sems):
            r.wait(a, b, c)

    def split(self, results):
        out, i = [], 0
        for r in self.rides:
            out.append(results[i:i + r.n])
            i += r.n
        return out


def _gather8(vec, name):
    size = vec.shape[0]
    n = _round_up(size, SUBLANE * LANE)
    blk = jnp.concatenate([vec, jnp.zeros((n - size,), f32)]).reshape(SUBLANE, n // SUBLANE)
    out = _ag_small(blk, name)
    return out.reshape(8, n)[:, :size]


def _head_spread_matrices():
    e = np.zeros((LANE, SSD_DI), np.float32)
    for h in range(SSD_HEADS):
        e[h, h * SSD_P:(h + 1) * SSD_P] = 1.0
    return jnp.asarray(e, bf16), jnp.asarray(e.T, bf16)


def _heads_to_lanes(v):
    return jnp.repeat(v, SSD_P).reshape(1, SSD_DI)


def _block_diag8(blocks):
    g, r, c = blocks.shape
    b = blocks.reshape(g // S5_NB, S5_NB, r, c)
    eye = jnp.eye(S5_NB, dtype=bool)[None, :, None, :, None]
    return jnp.where(eye, b[:, :, :, None, :], jnp.zeros((), blocks.dtype)).reshape(g * r, S5_NB * c)


def _diag8(mat, r, c):
    g = mat.shape[0] // r
    m = mat.reshape(g // S5_NB, S5_NB, r, S5_NB, c)
    eye = jnp.eye(S5_NB, dtype=bool)[None, :, None, :, None]
    return jnp.where(eye, m, 0.0).sum(axis=3).reshape(g, r, c)


class _Layout:
    def __init__(self, d):
        self.d = d
        self.z, self.xbc, self.u = 0, SSD_DI, SSD_DI + CONV_DIM
        self.ga = self.u + S5_W
        self.gb = self.ga + d
        self.dt = self.gb + d
        self.np_ = self.dt + LANE
        self.in_cols = SSD_DI + CONV_DIM + SSD_HEADS + S5_W + 2 * d
        off_dt = SSD_DI + CONV_DIM
        off_u = off_dt + SSD_HEADS
        off_g = off_u + S5_W
        self.src = [(0, off_dt), (off_u, off_u + S5_W + 2 * d), (off_dt, off_u)]

    def arrange_slabs(self, g):
        pieces = [p for lo, hi in self.src for p in _cols_from_slabs(g, lo, hi)]
        pieces.append(jnp.zeros((g.shape[1], LANE - SSD_HEADS), g.dtype))
        return jnp.concatenate(pieces, axis=1)

    def restore_slabs(self, main, tail):
        (a0, a1), (b0, b1), (c0, c1) = self.src
        n_a = a1 - a0
        segs = [(a0, a1, main, 0), (c0, c1, tail, 0), (b0, b1, main, n_a)]
        cs = self.in_cols // 4
        slabs = []
        for k in range(4):
            lo, hi = k * cs, (k + 1) * cs
            parts = [w[:, pos + max(lo, s0) - s0:pos + min(hi, s1) - s0] for s0, s1, w, pos in segs
                     if max(lo, s0) < min(hi, s1)]
            slabs.append(jnp.concatenate(parts, axis=1))
        return jnp.stack(slabs)


def _cols_from_slabs(g, start, stop):
    c = g.shape[2]
    return [g[k][:, max(start, k * c) - k * c:min(stop, (k + 1) * c) - k * c] for k in range(4)
            if max(start, k * c) < min(stop, (k + 1) * c)]


def _unshard_cols(g):
    return jnp.concatenate([g[k] for k in range(4)], axis=1)


def _shard_cols(w):
    r, c4 = w.shape
    return w.reshape(r, 4, c4 // 4).transpose(1, 0, 2)


def kernel(x, c, w_ada, b_ada, norm_ffn1, w_ffn1_in, w_ffn1_out, norm_mix, w_in, conv_w, conv_b, dt_bias, a_log, d_ssd, ssd_norm_w, w_a_proj, s5_lambda_re, s5_lambda_im, s5_b_re, s5_b_im, s5_c_re, s5_c_im, s5_d, s5_log_dt, w_b_glu, w_out, norm_ffn2, w_ffn2_in, w_ffn2_out, norm_final, loss_target, m_w_ada, m_b_ada, m_norm_ffn1, m_w_ffn1_in, m_w_ffn1_out, m_norm_mix, m_w_in, m_conv_w, m_conv_b, m_dt_bias, m_a_log, m_d_ssd, m_ssd_norm_w, m_w_a_proj, m_s5_lambda_re, m_s5_lambda_im, m_s5_b_re, m_s5_b_im, m_s5_c_re, m_s5_c_im, m_s5_d, m_s5_log_dt, m_w_b_glu, m_w_out, m_norm_ffn2, m_w_ffn2_in, m_w_ffn2_out, m_norm_final, v_w_ada, v_b_ada, v_norm_ffn1, v_w_ffn1_in, v_w_ffn1_out, v_norm_mix, v_w_in, v_conv_w, v_conv_b, v_dt_bias, v_a_log, v_d_ssd, v_ssd_norm_w, v_w_a_proj, v_s5_lambda_re, v_s5_lambda_im, v_s5_b_re, v_s5_b_im, v_s5_c_re, v_s5_c_im, v_s5_d, v_s5_log_dt, v_w_b_glu, v_w_out, v_norm_ffn2, v_w_ffn2_in, v_w_ffn2_out, v_norm_final):
    W = dict(w_ada=w_ada, b_ada=b_ada, norm_ffn1=norm_ffn1, w_ffn1_in=w_ffn1_in, w_ffn1_out=w_ffn1_out, norm_mix=norm_mix, w_in=w_in, conv_w=conv_w, conv_b=conv_b, dt_bias=dt_bias, a_log=a_log, d_ssd=d_ssd, ssd_norm_w=ssd_norm_w, w_a_proj=w_a_proj, s5_lambda_re=s5_lambda_re, s5_lambda_im=s5_lambda_im, s5_b_re=s5_b_re, s5_b_im=s5_b_im, s5_c_re=s5_c_re, s5_c_im=s5_c_im, s5_d=s5_d, s5_log_dt=s5_log_dt, w_b_glu=w_b_glu, w_out=w_out, norm_ffn2=norm_ffn2, w_ffn2_in=w_ffn2_in, w_ffn2_out=w_ffn2_out, norm_final=norm_final)
    Mo = dict(w_ada=m_w_ada, b_ada=m_b_ada, norm_ffn1=m_norm_ffn1, w_ffn1_in=m_w_ffn1_in, w_ffn1_out=m_w_ffn1_out, norm_mix=m_norm_mix, w_in=m_w_in, conv_w=m_conv_w, conv_b=m_conv_b, dt_bias=m_dt_bias, a_log=m_a_log, d_ssd=m_d_ssd, ssd_norm_w=m_ssd_norm_w, w_a_proj=m_w_a_proj, s5_lambda_re=m_s5_lambda_re, s5_lambda_im=m_s5_lambda_im, s5_b_re=m_s5_b_re, s5_b_im=m_s5_b_im, s5_c_re=m_s5_c_re, s5_c_im=m_s5_c_im, s5_d=m_s5_d, s5_log_dt=m_s5_log_dt, w_b_glu=m_w_b_glu, w_out=m_w_out, norm_ffn2=m_norm_ffn2, w_ffn2_in=m_w_ffn2_in, w_ffn2_out=m_w_ffn2_out, norm_final=m_norm_final)
    Vo = dict(w_ada=v_w_ada, b_ada=v_b_ada, norm_ffn1=v_norm_ffn1, w_ffn1_in=v_w_ffn1_in, w_ffn1_out=v_w_ffn1_out, norm_mix=v_norm_mix, w_in=v_w_in, conv_w=v_conv_w, conv_b=v_conv_b, dt_bias=v_dt_bias, a_log=v_a_log, d_ssd=v_d_ssd, ssd_norm_w=v_ssd_norm_w, w_a_proj=v_w_a_proj, s5_lambda_re=v_s5_lambda_re, s5_lambda_im=v_s5_lambda_im, s5_b_re=v_s5_b_re, s5_b_im=v_s5_b_im, s5_c_re=v_s5_c_re, s5_c_im=v_s5_c_im, s5_d=v_s5_d, s5_log_dt=v_s5_log_dt, w_b_glu=v_w_b_glu, w_out=v_w_out, norm_ffn2=v_norm_ffn2, w_ffn2_in=v_w_ffn2_in, w_ffn2_out=v_w_ffn2_out, norm_final=v_norm_final)

    t, d = x.shape[1], x.shape[2]
    ff = 4 * w_ffn1_out.shape[1]
    ffp = _round_up(ff, 512)
    lay = _Layout(d)
    xi, yi, ci = lax.axis_index("x"), lax.axis_index("y"), lax.axis_index("c")
    k_me = 2 * xi + yi
    e_me = 4 * xi + 2 * yi + ci
    x2d = x[0]
    tgt = loss_target[0]

    cw_cols = conv_w.shape[2]
    g1 = _gather8(jnp.concatenate([c[0], conv_w[0].reshape(-1)]), "gather_c_convw")
    c_all = g1[:, :d]
    conv_full = g1[::2, d:].reshape(4, CONV_K, cw_cols).transpose(1, 0, 2).reshape(CONV_K, CONV_DIM)
    conv_w8 = jnp.zeros((SUBLANE, CONV_DIM), f32).at[:CONV_K].set(conv_full)

    n_ada_loc = w_ada.shape[2]
    b_loc = lax.dynamic_slice(b_ada, (0, k_me * n_ada_loc), (1, n_ada_loc))
    mods_part = _ada_fwd(c_all, w_ada[0], b_loc, "ada_fwd")
    g2 = _gather8(mods_part.reshape(-1), "gather_mods").reshape(8, 8, n_ada_loc)
    mods = lax.dynamic_index_in_dim(g2[::2], e_me, axis=1, keepdims=False).reshape(N_ADA, d)
    sh1, sc1, gt1, sh2, sc2, gt2, sh3, sc3, gt3 = [mods[i:i + 1] for i in range(N_ADA)]

    cast = {n: _cast_bf16(W[n][0], "cast_" + n) for n in BIG}

    def gather_of(names):
        return _RideGather([cast[n] for n in names])

    def rows_of(g):
        return g.reshape(4 * g.shape[1], g.shape[2])

    def ffn_in(g):
        z = jnp.zeros((g.shape[1], ffp - ff), g.dtype)
        return jnp.concatenate([g[0], g[1], z, g[2], g[3], z], axis=1)

    def ffn_out(g):
        return jnp.concatenate([rows_of(g), jnp.zeros((ffp - ff, g.shape[2]), g.dtype)], axis=0)

    nf1, nmx, nf2 = norm_ffn1, norm_mix, norm_ffn2
    nfin = norm_final.reshape(1, d)

    (g_w1i,) = _run_ride(gather_of(['w_ffn1_in']), "gather_w_ffn1_in")
    w1i = ffn_in(g_w1i)
    (h1,) = _rw(_f_mod, [x2d], [nf1, sh1, sc1], [(d, bf16)], name="mod1")
    ab1, act1, (g_w1o, g_wa, g_wglu, g_wo) = _ffn_in(
        h1, w1i, "ffn1_in", ride=gather_of(['w_ffn1_out', 'w_a_proj', 'w_b_glu', 'w_out']))
    w1o = ffn_out(g_w1o)
    w_a = rows_of(g_wa)
    w_glu, w_o = _unshard_cols(g_wglu), rows_of(g_wo)
    f1, (g_win,) = _mm(act1, w1o, 'nn', out_dtype=bf16, name="ffn1_out", ride=gather_of(['w_in']))
    w_inr = lay.arrange_slabs(g_win)
    res1 = functools.partial(_f_res_mod, 0.5)
    x1, h2 = _rw(res1, [x2d, f1], [gt1, nmx, sh2, sc2], [(d, f32), (d, bf16)], name="res1_mod2")
    proj, (g_w2i,) = _mm(h2, w_inr, 'nn', out_dtype=f32, name="in_proj", ride=gather_of(['w_ffn2_in']))
    w2i = ffn_in(g_w2i)

    pre = _conv_fwd(proj, lay.xbc, conv_w8, conv_b, "conv_fwd")
    spread, spread_t = _head_spread_matrices()
    bias128 = jnp.zeros((1, LANE), f32).at[:, :SSD_HEADS].set(dt_bias)
    xs, bm, cm, dt4 = _rw(_f_ssd_pre, [pre, (proj, lay.dt, LANE)], [bias128, spread, spread_t],
                          [(SSD_DI, f32), (SSD_G * SSD_N, f32), (SSD_G * SSD_N, f32), (SSD_DI, f32)],
                          name="ssd_pre")

    def head_params(a_log_, d_ssd_):
        return _heads_to_lanes(-jnp.exp(a_log_[0])), _heads_to_lanes(d_ssd_[0])

    (a4, dsk4), head_vjp = jax.vjp(head_params, a_log, d_ssd)
    y_ssd, hs, (g_w2o,) = _ssd_fwd(xs, bm, cm, dt4, a4, dsk4, "ssd_fwd", ride=gather_of(['w_ffn2_out']))
    w2o = ffn_out(g_w2o)
    (y_a,) = _rw(_f_ssd_post, [y_ssd, (proj, lay.z, SSD_DI)], [ssd_norm_w], [(SSD_DI, bf16)], name="ssd_post")
    p_a = _mm(y_a, w_a, 'nn', out_dtype=bf16, name="a_proj")

    col = lambda v: v.reshape(S5_S, 1)
    ldt_col = jnp.repeat(s5_log_dt[0], S5_P).reshape(S5_S, 1)
    prep_rows = [col(s5_lambda_re[0]), col(s5_lambda_im[0]), ldt_col,
                 s5_b_re[0].reshape(S5_S, S5_I), s5_b_im[0].reshape(S5_S, S5_I)]
    ar, ai, bbr, bbi = _rw(_f_s5_prep, prep_rows, [], [(1, f32), (1, f32), (S5_I, f32), (S5_I, f32)],
                           name="s5_prep", tm=512)
    to_bd = lambda bb: _block_diag8(bb.reshape(S5_G, S5_P, S5_I).transpose(0, 2, 1).astype(bf16))
    bd_c = jnp.concatenate([to_bd(bbr), to_bd(bbi)], axis=1)
    c_c = jnp.concatenate([_block_diag8(s5_c_re[0].transpose(0, 2, 1).astype(bf16)),
                           _block_diag8((-s5_c_im[0]).transpose(0, 2, 1).astype(bf16))], axis=0)
    ar_row, ai_row = ar.reshape(1, S5_S), ai.reshape(1, S5_S)
    s5s, yb = _s5_fwd(proj, lay.u, bd_c, c_c, ar_row, ai_row, "s5_fwd")
    d_row = s5_d[0].reshape(1, S5_W)
    (gl,) = _rw(_f_s5_post, [yb, (proj, lay.u, S5_W)], [d_row], [(S5_W, bf16)], name="s5_post")
    glu = _mm(gl, w_glu, 'nn', out_dtype=bf16, name="glu_proj")

    merge_rows = [p_a, glu, (proj, lay.ga, d), (proj, lay.gb, d)]
    (merged,) = _rw(_f_merge, merge_rows, [], [(d, bf16)], name="merge")
    o_mix = _mm(merged, w_o, 'nn', out_dtype=bf16, name="out_proj")
    res2 = functools.partial(_f_res_mod, 1.0)
    x2, h3 = _rw(res2, [x1, o_mix], [gt2, nf2, sh3, sc3], [(d, f32), (d, bf16)], name="res2_mod3")
    ab2, act2, _ = _ffn_in(h3, w2i, "ffn2_in")
    f2 = _mm(act2, w2o, 'nn', out_dtype=bf16, name="ffn2_out")
    (loss_acc,) = _rw(_f_final_loss, [x2, f2, tgt], [gt3, nfin], [], accs=[(1, LANE)], name="loss")
    loss = lax.psum(loss_acc[0, 0], AXES)

    ones = jnp.ones((t, 1), f32)
    dx2, df2, dgt3, dnfin = _rw_vjp(_f_final, [x2, f2, tgt], [gt3, nfin], [ones],
                                    row_grads=[f32, bf16, None], param_grads=[True, True], name="loss_bwd")
    def ffn_in_back(g):
        hf = ff // 2
        return jnp.stack([g[:, :hf], g[:, hf:ff], g[:, ffp:ffp + hf], g[:, ffp + hf:ffp + ff]])

    def rows_back(g, rows):
        return g[:rows].reshape(4, rows // 4, g.shape[1])

    def scatter_of(pairs):
        return _Ride([g for _, g in pairs], True)

    terms = {}
    dab2, _ = _ffn_out_dx(df2, w2o, ab2, "ffn2_out_dx")
    dw2o = _mm(act2, df2, 'tn', out_dtype=bf16, name="ffn2_out_dw")
    dh3, (terms['w_ffn2_out'],) = _ffn_in_dx(dab2, w2i, "ffn2_in_dx",
                                             ride=scatter_of([('w_ffn2_out', rows_back(dw2o, ff))]))
    dw2i = _ffn_in_dw(h3, dab2, "ffn2_in_dw")
    dx1, do_mix, dgt2, dnf2, dsh3, dsc3 = _rw_vjp(
        res2, [x1, o_mix], [gt2, nf2, sh3, sc3], [dx2, dh3], row_grads=[f32, bf16], param_grads=[True] * 4,
        name="res2_mod3_bwd")
    dmerged = _mm(do_mix, w_o, 'nt', out_dtype=bf16, name="out_proj_dx")
    dw_o = _mm(merged, do_mix, 'tn', out_dtype=bf16, name="out_proj_dw")
    dp_a, dglu, dga, dgb = _rw_vjp(_f_merge, merge_rows, [], [dmerged], row_grads=[bf16] * 4,
                                   param_grads=[], name="merge_bwd")

    dgl = _mm(dglu, w_glu, 'nt', out_dtype=bf16, name="glu_proj_dx")
    dw_glu = _mm(gl, dglu, 'tn', out_dtype=bf16, name="glu_proj_dw")
    dyb, du_skip, dd_row = _rw_vjp(_f_s5_post, [yb, (proj, lay.u, S5_W)], [d_row], [dgl],
                                   row_grads=[bf16, f32], param_grads=[True], name="s5_post_bwd")
    (du, dar, dai, dbd_c, dc_c), (terms['w_ffn2_in'],) = _s5_bwd(
        dyb, s5s, proj, lay.u, bd_c, c_c, ar_row, ai_row, du_skip, "s5_bwd",
        ride=scatter_of([('w_ffn2_in', ffn_in_back(dw2i))]))
    from_bd = lambda m_: _diag8(m_, S5_I, S5_P).transpose(0, 2, 1).reshape(S5_S, S5_I)
    dprep = _rw_vjp(_f_s5_prep, prep_rows, [], [dar.reshape(S5_S, 1), dai.reshape(S5_S, 1),
                                                from_bd(dbd_c[:, :S5_SB]), from_bd(dbd_c[:, S5_SB:])],
                    row_grads=[f32] * 5, param_grads=[], name="s5_prep_bwd", tm=512)
    dlr, dli, dldt, dbr, dbi = dprep
    g_s5 = dict(
        s5_lambda_re=dlr.reshape(S5_G, S5_P), s5_lambda_im=dli.reshape(S5_G, S5_P),
        s5_log_dt=dldt.reshape(S5_G, S5_P).sum(axis=1),
        s5_b_re=dbr.reshape(S5_G, S5_P, S5_I), s5_b_im=dbi.reshape(S5_G, S5_P, S5_I),
        s5_c_re=_diag8(dc_c[:S5_S], S5_P, S5_I).transpose(0, 2, 1),
        s5_c_im=-_diag8(dc_c[S5_S:], S5_P, S5_I).transpose(0, 2, 1),
        s5_d=dd_row.reshape(S5_G, S5_I))

    dy_a = _mm(dp_a, w_a, 'nt', out_dtype=bf16, name="a_proj_dx")
    dw_a = _mm(y_a, dp_a, 'tn', out_dtype=bf16, name="a_proj_dw")
    dy_ssd, dz, dssd_nw = _rw_vjp(_f_ssd_post, [y_ssd, (proj, lay.z, SSD_DI)], [ssd_norm_w], [dy_a],
                                  row_grads=[f32, bf16], param_grads=[True], name="ssd_post_bwd")
    early = [('w_out', rows_back(dw_o, d)), ('w_b_glu', _shard_cols(dw_glu)), ('w_a_proj', rows_back(dw_a, SSD_DI))]
    (dxs, dbm, dcm, ddt4, da4, ddsk4), landed = _ssd_bwd(xs, bm, cm, dt4, a4, dsk4, hs, dy_ssd, "ssd_bwd",
                                                         ride=scatter_of(early))
    terms.update({n: p for (n, _), p in zip(early, landed)})
    da_log, dd_ssd = head_vjp((da4, ddsk4))
    dpre, ddt_raw, dbias128 = _rw_vjp(_f_ssd_pre, [pre, (proj, lay.dt, LANE)], [bias128, spread, spread_t],
                                      [dxs, dbm, dcm, ddt4], row_grads=[f32, bf16],
                                      param_grads=[True, False, False], name="ssd_pre_bwd")
    dxbc, dconv_w8, dconv_b = _conv_bwd(dpre, proj, lay.xbc, conv_w8, "conv_bwd")

    dproj = [dz, dxbc, du, dga, dgb]
    dw_main = _in_proj_dw(h2, dproj, "in_proj_dw")
    dw_dt = _mm(h2, ddt_raw, 'tn', out_dtype=bf16, name="in_proj_dw_dt")
    dh2, (terms['w_in'],) = _in_proj_dx(dproj, ddt_raw, lay.dt, w_inr, "in_proj_dx",
                                        ride=scatter_of([('w_in', lay.restore_slabs(dw_main, dw_dt))]))
    dx0, df1, dgt1, dnmx, dsh2, dsc2 = _rw_vjp(
        res1, [x2d, f1], [gt1, nmx, sh2, sc2], [dx1, dh2], row_grads=[f32, bf16], param_grads=[True] * 4,
        name="res1_mod2_bwd")
    dw1o = _mm(act1, df1, 'tn', out_dtype=bf16, name="ffn1_out_dw")
    dab1, (terms['w_ffn1_out'],) = _ffn_out_dx(df1, w1o, ab1, "ffn1_out_dx",
                                               ride=scatter_of([('w_ffn1_out', rows_back(dw1o, ff))]))
    dw1i = _ffn_in_dw(h1, dab1, "ffn1_in_dw")

    last = 'w_ffn1_in'
    sums = {n: _sum_lead(terms[n], "sum_" + n) for n in BIG if n != last}
    swap = _RideSwap([sums[n] for n in BIG if n != last])
    rides = _Rides([scatter_of([(last, ffn_in_back(dw1i))]), swap])
    dh1, landed = _ffn_in_dx(dab1, w1i, "ffn1_in_dx", ride=rides)
    (terms[last],), swapped = rides.split(landed)
    others = dict(zip([n for n in BIG if n != last], swapped))
    grad_x, dnf1, dsh1, dsc1 = _rw_vjp(_f_mod_keep, [x2d], [nf1, sh1, sc1], [dh1, dx0],
                                       row_grads=[f32], param_grads=[True] * 3, name="mod1_bwd")
    d_mods = jnp.concatenate([dsh1, dsc1, dgt1, dsh2, dsc2, dgt2, dsh3, dsc3, dgt3], axis=1).reshape(-1)
    sums[last] = _sum_lead(terms[last], "sum_" + last)
    (others[last],) = _run_ride(_RideSwap([sums[last]]), "swap_sum_" + last)

    out_g, out_d, out_m, out_v = {}, {}, {}, {}
    for n in BIG:
        r = _adamw(W[n][0], Mo[n][0], Vo[n][0], [sums[n], others[n]], "adamw_" + n)
        out_g[n], out_d[n], out_m[n], out_v[n] = [o[None] for o in r]

    local = dict(
        b_ada=d_mods, norm_ffn1=dnf1, norm_mix=dnmx, conv_w=dconv_w8[:CONV_K], conv_b=dconv_b,
        dt_bias=dbias128[:, :SSD_HEADS], a_log=da_log, d_ssd=dd_ssd, ssd_norm_w=dssd_nw,
        norm_ffn2=dnf2, norm_final=dnfin, **g_s5)
    flat = jnp.concatenate([local[n].reshape(-1) for n in SMALL])
    g3 = _gather8(flat, "gather_small_grads")
    n_small = flat.shape[0]
    npad = _round_up(n_small, SUBLANE * LANE)
    g3p = jnp.zeros((8, npad), f32).at[:, :n_small].set(g3).reshape(8, npad // LANE, LANE)
    gsum = _sum_lead(g3p, "sum_small").reshape(-1)

    def local_shard(n, a):
        if n == 'conv_w':
            return lax.dynamic_slice(a.reshape(CONV_K, CONV_DIM), (0, k_me * cw_cols), (CONV_K, cw_cols))
        return a

    pieces, off = {}, 0
    for n in SMALL:
        sz = local[n].size
        pieces[n] = local_shard(n, gsum[off:off + sz]).reshape(W[n].shape)
        off += sz

    def pack(dct):
        v_ = jnp.concatenate([dct[n].reshape(-1) for n in SMALL])
        pad = _round_up(v_.shape[0], SUBLANE * LANE) - v_.shape[0]
        return jnp.concatenate([v_, jnp.ones((pad,), f32)]).reshape(-1, LANE)

    rs = _adamw(pack(W), pack(Mo), pack(Vo), [pack(pieces)], "adamw_small")
    off = 0
    for n in SMALL:
        sz = W[n].size
        out_g[n], out_d[n], out_m[n], out_v[n] = [o.reshape(-1)[off:off + sz].reshape(W[n].shape) for o in rs]
        off += sz

    dm_loc = lax.dynamic_slice(g3[:, :N_ADA * d], (0, k_me * n_ada_loc), (SUBLANE, n_ada_loc))
    g_ada = _ada_bwd(c_all, dm_loc, "ada_bwd")
    r = _adamw(w_ada[0], m_w_ada[0], v_w_ada[0], [g_ada], "adamw_w_ada")
    out_g['w_ada'], out_d['w_ada'], out_m['w_ada'], out_v['w_ada'] = [o[None] for o in r]

    return (loss, grad_x[None], *[out_g[n] for n in WEIGHTS], *[out_d[n] for n in WEIGHTS],
            *[out_m[n] for n in WEIGHTS], *[out_v[n] for n in WEIGHTS])
```

```python
import functools
import math

import numpy as np
import jax
import jax.numpy as jnp
from jax import lax
from jax.experimental import pallas as pl
from jax.experimental.pallas import tpu as pltpu

f32 = jnp.float32
bf16 = jnp.bfloat16
HI = lax.Precision.HIGHEST
MESH = pl.DeviceIdType.MESH
AXES = ("x", "y", "c")

EPS = 1e-6
SSD_HEADS, SSD_P, SSD_N, SSD_G, SSD_R, SSD_L = 32, 64, 128, 4, 8, 128
SSD_DI = SSD_HEADS * SSD_P
CONV_K = 4
CONV_DIM = SSD_DI + 2 * SSD_G * SSD_N
S5_W, S5_G, S5_I, S5_P = 1024, 64, 16, 64
S5_S = S5_G * S5_P
N_ADA = 9
ADAM_LR, ADAM_B1, ADAM_B2, ADAM_EPS, ADAM_WD, ADAM_STEP = 0.001, 0.9, 0.999, 1e-08, 0.01, 10

LANE = 128
SUBLANE = 8
VMEM_LIMIT = 56 << 20
MM_VMEM_BUDGET = 40 << 20
RW_VMEM_BUDGET = 36 << 20

WEIGHTS = ['w_ada', 'b_ada', 'norm_ffn1', 'w_ffn1_in', 'w_ffn1_out', 'norm_mix', 'w_in', 'conv_w', 'conv_b', 'dt_bias',
           'a_log', 'd_ssd', 'ssd_norm_w', 'w_a_proj', 's5_lambda_re', 's5_lambda_im', 's5_b_re', 's5_b_im', 's5_c_re',
           's5_c_im', 's5_d', 's5_log_dt', 'w_b_glu', 'w_out', 'norm_ffn2', 'w_ffn2_in', 'w_ffn2_out', 'norm_final']
BIG = ['w_ffn1_in', 'w_ffn1_out', 'w_in', 'w_a_proj', 'w_b_glu', 'w_out', 'w_ffn2_in', 'w_ffn2_out']
COL_SHARDED = ('w_ffn1_in', 'w_in', 'w_b_glu', 'w_ffn2_in')
SMALL = [n for n in WEIGHTS if n not in BIG and n != 'w_ada']


def _cp(sem=None):
    return pltpu.CompilerParams(dimension_semantics=sem, vmem_limit_bytes=VMEM_LIMIT)


def _tile(dim, target, align=LANE):
    if dim <= target:
        return dim
    t = (target // align) * align
    while t >= align:
        if dim % t == 0:
            return t
        t -= align
    return dim


def _round_up(n, m):
    return (n + m - 1) // m * m


def _mm(a, b, mode, *, out_dtype, name, a_win=None, b_win=None, add=None, ride=None):
    a0, aw = a_win or (0, a.shape[1])
    b0, bw = b_win or (0, b.shape[1])
    if mode == 'nn':
        m, k, n = a.shape[0], aw, bw
        assert b.shape[0] == k
    elif mode == 'nt':
        m, k, n = a.shape[0], aw, b.shape[0]
        assert bw == k
    else:
        k, m, n = a.shape[0], aw, bw
        assert b.shape[0] == k
    osz = jnp.dtype(out_dtype).itemsize
    tm, tn, tk = 1024, 1152, 3456
    while True:
        bm = _tile(math.gcd(m, a0) if (mode == 'tn' and a0) else m, tm)
        bn = _tile(math.gcd(n, b0) if (mode != 'nt' and b0) else n, tn)
        kk = k
        if mode != 'tn' and a0:
            kk = math.gcd(kk, a0)
        if mode == 'nt' and b0:
            kk = math.gcd(kk, b0)
        bk = _tile(kk, tk)
        need = 2 * (bm * bk * a.dtype.itemsize + bk * bn * b.dtype.itemsize + bm * bn * osz) + bm * bn * 4
        if add is not None:
            need += 2 * bm * bn * add.dtype.itemsize
        if need <= MM_VMEM_BUDGET or (tm <= 256 and tn <= 256 and tk <= 512):
            break
        if tk > 1024:
            tk //= 2
        elif tm >= tn:
            tm //= 2
        else:
            tn //= 2
    nk = k // bk
    assert m % bm == 0 and n % bn == 0 and k % bk == 0, (name, m, n, k, bm, bn, bk)
    if mode == 'nn':
        ao, bo = a0 // bk, b0 // bn
        a_blk, a_map = (bm, bk), lambda i, j, q: (i, q + ao)
        b_blk, b_map = (bk, bn), lambda i, j, q: (q, j + bo)
    elif mode == 'nt':
        ao, bo = a0 // bk, b0 // bk
        a_blk, a_map = (bm, bk), lambda i, j, q: (i, q + ao)
        b_blk, b_map = (bn, bk), lambda i, j, q: (j, q + bo)
    else:
        ao, bo = a0 // bm, b0 // bn
        a_blk, a_map = (bk, bm), lambda i, j, q: (q, i + ao)
        b_blk, b_map = (bk, bn), lambda i, j, q: (q, j + bo)
    return _mm_core(a, b, mode, grid=(m // bm, n // bn, nk), a_blk=a_blk, a_map=a_map, b_blk=b_blk, b_map=b_map,
                    o_blk=(bm, bn), o_map=lambda i, j, q: (i, j), out_shape=(m, n), out_dtype=out_dtype, name=name,
                    add=add, ride=ride)


def _mm_core(a, b, mode, *, grid, a_blk, a_map, b_blk, b_map, o_blk, o_map, out_shape, out_dtype, name,
             add=None, ride=None):
    dims = {'nn': (((1,), (0,)), ((), ())), 'nt': (((1,), (1,)), ((), ())), 'tn': (((0,), (0,)), ((), ()))}[mode]
    nk = grid[-1]
    has_add = add is not None
    nr = ride.n if ride is not None else 0

    def body(*refs):
        a_ref, b_ref = refs[0], refs[1]
        pos = 2
        add_ref = refs[pos] if has_add else None
        pos += int(has_add)
        r_ins = refs[pos:pos + nr]
        o_ref = refs[pos + nr]
        r_outs = refs[pos + nr + 1:pos + 2 * nr + 1]
        acc_ref = refs[pos + 2 * nr + 1]
        r_sems = refs[pos + 2 * nr + 2:]
        ids = [pl.program_id(ax) for ax in range(len(grid))]
        q = ids[-1]
        if nr:
            @pl.when(functools.reduce(lambda u, v: u & v, [i == 0 for i in ids]))
            def _():
                ride.start(r_ins, r_outs, r_sems)

        @pl.when(q == 0)
        def _():
            acc_ref[...] = jnp.zeros_like(acc_ref)

        acc_ref[...] += lax.dot_general(a_ref[...].astype(bf16), b_ref[...].astype(bf16), dims,
                                        preferred_element_type=f32)

        @pl.when(q == nk - 1)
        def _():
            r = acc_ref[...]
            if has_add:
                r = r + add_ref[...].astype(f32)
            o_ref[...] = r.astype(out_dtype)

        if nr:
            @pl.when(functools.reduce(lambda u, v: u & v, [i == g - 1 for i, g in zip(ids, grid)]))
            def _():
                ride.wait(r_ins, r_outs, r_sems)

    in_specs = [pl.BlockSpec(a_blk, a_map), pl.BlockSpec(b_blk, b_map)]
    ops = [a, b]
    if has_add:
        in_specs.append(pl.BlockSpec(o_blk, o_map))
        ops.append(add)
    out_specs = [pl.BlockSpec(o_blk, o_map)]
    out_shapes = [jax.ShapeDtypeStruct(out_shape, out_dtype)]
    scratch = [pltpu.VMEM(o_blk, f32)]
    if nr:
        in_specs += ride.specs
        ops += ride.srcs
        out_specs += ride.specs
        out_shapes += ride.out_shape
        scratch += ride.scratch
    sem = ("arbitrary",) * len(grid) if nr else ("parallel",) * (len(grid) - 1) + ("arbitrary",)
    res = pl.pallas_call(
        body, name=name, grid=grid, in_specs=in_specs, out_specs=out_specs, out_shape=out_shapes,
        scratch_shapes=scratch, compiler_params=_cp(sem),
    )(*ops)
    return (res[0], list(res[1:])) if nr else res[0]


def _swiglu(a, b):
    return jax.nn.silu(a) * b


def _ride_parts(refs, n_in, n_out, ride):
    nr = ride.n if ride is not None else 0
    ins = refs[:n_in]
    r_ins = refs[n_in:n_in + nr]
    outs = refs[n_in + nr:n_in + nr + n_out]
    r_outs = refs[n_in + nr + n_out:n_in + 2 * nr + n_out]
    return ins, r_ins, outs, r_outs, refs[n_in + 2 * nr + n_out:]


def _call_with_ride(body_core, grid, in_specs, ops, out_specs, out_shape, name, ride, scratch=(), sem=None):
    nr = ride.n if ride is not None else 0
    n_in, n_out, n_scr = len(in_specs), len(out_specs), len(scratch)

    def body(*refs):
        ins, r_ins, outs, r_outs, rest = _ride_parts(refs, n_in, n_out, ride)
        ids = [pl.program_id(ax) for ax in range(len(grid))]
        if nr:
            @pl.when(functools.reduce(lambda u, v: u & v, [i == 0 for i in ids]))
            def _():
                ride.start(r_ins, r_outs, rest[n_scr:])
        body_core(ins, outs, *rest[:n_scr])
        if nr:
            @pl.when(functools.reduce(lambda u, v: u & v, [i == g - 1 for i, g in zip(ids, grid)]))
            def _():
                ride.wait(r_ins, r_outs, rest[n_scr:])

    res = pl.pallas_call(
        body, name=name, grid=grid, in_specs=in_specs + (ride.specs if nr else []),
        out_specs=out_specs + (ride.specs if nr else []), out_shape=out_shape + (ride.out_shape if nr else []),
        scratch_shapes=list(scratch) + (ride.scratch if nr else []),
        compiler_params=_cp(("arbitrary",) * len(grid) if nr else (sem or ("parallel",) * len(grid))),
    )(*ops, *(ride.srcs if nr else []))
    return list(res[:n_out]), list(res[n_out:])


def _ffn_in(h, w, name, ride=None):
    t, d = h.shape
    ffp = w.shape[1] // 2
    bm, bn = _tile(t, 1024), _tile(ffp, 512)
    nb = ffp // bn
    nn = (((1,), (0,)), ((), ()))

    def core(ins, outs):
        h_ref, wa_ref, wb_ref = ins
        ab_ref, act_ref = outs
        hv = h_ref[...].astype(bf16)
        a = lax.dot_general(hv, wa_ref[...].astype(bf16), nn, preferred_element_type=f32)
        b = lax.dot_general(hv, wb_ref[...].astype(bf16), nn, preferred_element_type=f32)
        ab_ref[0] = a.astype(bf16)
        ab_ref[1] = b.astype(bf16)
        act_ref[...] = _swiglu(a, b).astype(bf16)

    (ab, act), landed = _call_with_ride(
        core, (t // bm, nb),
        [pl.BlockSpec((bm, d), lambda i, j: (i, 0)), pl.BlockSpec((d, bn), lambda i, j: (0, j)),
         pl.BlockSpec((d, bn), lambda i, j: (0, j + nb))], [h, w, w],
        [pl.BlockSpec((2, bm, bn), lambda i, j: (0, i, j)), pl.BlockSpec((bm, bn), lambda i, j: (i, j))],
        [jax.ShapeDtypeStruct((2, t, ffp), bf16), jax.ShapeDtypeStruct((t, ffp), bf16)], name, ride)
    return ab, act, landed


def _ffn_out_dx(df, wo, ab, name, ride=None):
    t, d = df.shape
    ffp = wo.shape[0]
    bm, bn = _tile(t, 1024), _tile(ffp, 512)
    nt = (((1,), (1,)), ((), ()))

    def core(ins, outs):
        df_ref, wo_ref, ab_ref = ins
        dact = lax.dot_general(df_ref[...].astype(bf16), wo_ref[...].astype(bf16), nt, preferred_element_type=f32)
        _, vjp = jax.vjp(_swiglu, ab_ref[0].astype(f32), ab_ref[1].astype(f32))
        da, db = vjp(dact)
        outs[0][0] = da.astype(bf16)
        outs[0][1] = db.astype(bf16)

    (dab,), landed = _call_with_ride(
        core, (t // bm, ffp // bn),
        [pl.BlockSpec((bm, d), lambda i, j: (i, 0)), pl.BlockSpec((bn, d), lambda i, j: (j, 0)),
         pl.BlockSpec((2, bm, bn), lambda i, j: (0, i, j))], [df, wo, ab],
        [pl.BlockSpec((2, bm, bn), lambda i, j: (0, i, j))], [jax.ShapeDtypeStruct((2, t, ffp), bf16)], name, ride)
    return dab, landed


def _ffn_in_dx(dab, w, name, ride=None):
    _, t, ffp = dab.shape
    d = w.shape[0]
    bm, bn, bk = _tile(t, 1024), _tile(d, 1024), _tile(ffp, 2816)
    nkh = ffp // bk
    return _mm_core(dab, w, 'nt', grid=(t // bm, d // bn, 2 * nkh),
                    a_blk=(None, bm, bk), a_map=lambda i, j, q: (q // nkh, i, q % nkh),
                    b_blk=(bn, bk), b_map=lambda i, j, q: (j, q),
                    o_blk=(bm, bn), o_map=lambda i, j, q: (i, j),
                    out_shape=(t, d), out_dtype=bf16, name=name, ride=ride)


def _ffn_in_dw(h, dab, name):
    _, t, ffp = dab.shape
    d = h.shape[1]
    bm, bn, bk = _tile(d, 1024), _tile(ffp, 1408), _tile(t, 2048)
    nbh = ffp // bn
    return _mm_core(h, dab, 'tn', grid=(d // bm, 2 * nbh, t // bk),
                    a_blk=(bk, bm), a_map=lambda i, j, q: (q, i),
                    b_blk=(None, bk, bn), b_map=lambda i, j, q: (j // nbh, q, j % nbh),
                    o_blk=(bm, bn), o_map=lambda i, j, q: (i, j),
                    out_shape=(d, 2 * ffp), out_dtype=bf16, name=name)


def _piece_blocks(pieces):
    bk = min(1024, functools.reduce(math.gcd, [p.shape[1] for p in pieces]))
    starts, n = [], 0
    for p in pieces:
        starts.append(n)
        n += p.shape[1] // bk
    return bk, starts, n


def _in_proj_dx(pieces, tail, tail_col, w, name, ride=None):
    t, d = pieces[0].shape[0], w.shape[0]
    bk, starts, nq = _piece_blocks(pieces)
    assert tail_col == nq * bk and tail.shape[1] == LANE
    bm, bn = _tile(t, 1024), _tile(d, 1024)
    npc = len(pieces)

    def core(ins, outs, acc_ref):
        a_refs, tail_ref, b_ref, bt_ref = ins[:npc], ins[npc], ins[npc + 1], ins[npc + 2]
        q = pl.program_id(2)

        @pl.when(q == 0)
        def _():
            acc_ref[...] = jnp.zeros_like(acc_ref)

        for a_ref, s0, p in zip(a_refs, starts, pieces):
            @pl.when((q >= s0) & (q < s0 + p.shape[1] // bk))
            def _(a_ref=a_ref):
                acc_ref[...] += lax.dot_general(a_ref[...].astype(bf16), b_ref[...], _NT, preferred_element_type=f32)

        @pl.when(q == nq)
        def _():
            r = acc_ref[...] + lax.dot_general(tail_ref[...].astype(bf16), bt_ref[...], _NT,
                                               preferred_element_type=f32)
            outs[0][...] = r.astype(bf16)

    def a_spec(s0, p):
        last = p.shape[1] // bk - 1
        return pl.BlockSpec((bm, bk), lambda i, j, q: (i, jnp.clip(q - s0, 0, last)))

    in_specs = [a_spec(s0, p) for s0, p in zip(starts, pieces)] + [
        pl.BlockSpec((bm, LANE), lambda i, j, q: (i, 0)),
        pl.BlockSpec((bn, bk), lambda i, j, q: (j, jnp.minimum(q, nq - 1))),
        pl.BlockSpec((bn, LANE), lambda i, j, q: (j, tail_col // LANE))]
    (dh,), landed = _call_with_ride(
        core, (t // bm, d // bn, nq + 1), in_specs, list(pieces) + [tail, w, w],
        [pl.BlockSpec((bm, bn), lambda i, j, q: (i, j))], [jax.ShapeDtypeStruct((t, d), bf16)], name, ride,
        scratch=[pltpu.VMEM((bm, bn), f32)], sem=("parallel", "parallel", "arbitrary"))
    return dh, landed


def _win(r):
    return r if isinstance(r, tuple) else (r, 0, r.shape[1])


def _row_tile(t, widths):
    per_row = 48 * max(widths)
    tm = 512
    while tm > SUBLANE and tm * per_row > RW_VMEM_BUDGET:
        tm //= 2
    return min(tm, t)


def _row_spec(r, tm):
    arr, c0, w = _win(r)
    assert c0 % w == 0, (c0, w)
    cb = c0 // w
    return pl.BlockSpec((tm, w), lambda i: (i, cb))


def _full_spec(p):
    nd = p.ndim
    return pl.BlockSpec(p.shape, lambda i: (0,) * nd)


def _rw(f, rows, params, outs, *, name, accs=(), tm=None):
    t = _win(rows[0])[0].shape[0]
    tm = tm or _row_tile(t, [_win(r)[2] for r in rows] + [w for w, _ in outs])
    nr, npar, no, na = len(rows), len(params), len(outs), len(accs)

    def body(*refs):
        vals = [r[...] for r in refs[:nr + npar]]
        res = f(*vals)
        res = res if isinstance(res, (tuple, list)) else (res,)
        for o_ref, v in zip(refs[nr + npar:nr + npar + no], res[:no]):
            o_ref[...] = v.astype(o_ref.dtype)
        if na:
            first = pl.program_id(0) == 0
            for a_ref, v in zip(refs[nr + npar + no:], res[no:]):
                @pl.when(first)
                def _(a_ref=a_ref):
                    a_ref[...] = jnp.zeros_like(a_ref)
                a_ref[...] += v

    out_shape = [jax.ShapeDtypeStruct((t, w), d) for w, d in outs] + [jax.ShapeDtypeStruct(s, f32) for s in accs]
    out_specs = [pl.BlockSpec((tm, w), lambda i: (i, 0)) for w, _ in outs] + \
                [pl.BlockSpec(s, lambda i: (0, 0)) for s in accs]
    return pl.pallas_call(
        body, name=name, grid=(t // tm,),
        in_specs=[_row_spec(r, tm) for r in rows] + [_full_spec(p) for p in params],
        out_specs=out_specs, out_shape=out_shape,
        compiler_params=_cp(("arbitrary",)),
    )(*[_win(r)[0] for r in rows], *params)


def _rw_vjp(f, rows, params, cots, *, row_grads, param_grads, name, tm=None, total_of=None):
    t = _win(rows[0])[0].shape[0]
    cot_rows = [c for c in cots if c is not None]
    tm = tm or _row_tile(t, [_win(r)[2] for r in rows] + [_win(c)[2] for c in cot_rows])
    nr, npar, ncot = len(rows), len(params), len(cot_rows)
    d_rows = [i for i, d in enumerate(row_grads) if d is not None]
    d_pars = [i for i, d in enumerate(param_grads) if d]

    def body(*refs):
        rv = [r[...] for r in refs[:nr]]
        pv = [r[...] for r in refs[nr:nr + npar]]
        cv = [r[...] for r in refs[nr + npar:nr + npar + ncot]]
        outs_r = refs[nr + npar + ncot:nr + npar + ncot + len(d_rows)]
        outs_p = refs[nr + npar + ncot + len(d_rows):nr + npar + ncot + len(d_rows) + len(d_pars)]

        def g(*diff):
            rr, pp = list(rv), list(pv)
            for i, v in zip(d_rows, diff[:len(d_rows)]):
                rr[i] = v
            for i, v in zip(d_pars, diff[len(d_rows):]):
                pp[i] = v
            res = f(*rr, *pp)
            return tuple(res) if isinstance(res, (tuple, list)) else (res,)

        prim, vjp = jax.vjp(g, *[rv[i] for i in d_rows], *[pv[i] for i in d_pars])
        it = iter(cv)
        cts = tuple(next(it).astype(o.dtype) if c is not None else jnp.zeros_like(o) for o, c in zip(prim, cots))
        grads = vjp(cts)
        for o_ref, v in zip(outs_r, grads[:len(d_rows)]):
            o_ref[...] = v.astype(o_ref.dtype)
        first = pl.program_id(0) == 0
        for o_ref, v in zip(outs_p, grads[len(d_rows):]):
            @pl.when(first)
            def _(o_ref=o_ref):
                o_ref[...] = jnp.zeros_like(o_ref)
            o_ref[...] += v.astype(f32)
        if total_of is not None:
            tot_ref = refs[-1]

            @pl.when(first)
            def _():
                tot_ref[...] = jnp.zeros_like(tot_ref)
            tot_ref[...] += jnp.broadcast_to(jnp.sum(prim[total_of].astype(f32)), tot_ref.shape)

    out_shape = [jax.ShapeDtypeStruct((t, _win(rows[i])[2]), row_grads[i]) for i in d_rows] + \
                [jax.ShapeDtypeStruct(params[i].shape, f32) for i in d_pars]
    out_specs = [pl.BlockSpec((tm, _win(rows[i])[2]), lambda i_: (i_, 0)) for i in d_rows] + \
                [_full_spec(params[i]) for i in d_pars]
    if total_of is not None:
        out_shape.append(jax.ShapeDtypeStruct((1, LANE), f32))
        out_specs.append(pl.BlockSpec((1, LANE), lambda i_: (0, 0)))
    return pl.pallas_call(
        body, name=name, grid=(t // tm,),
        in_specs=[_row_spec(r, tm) for r in rows] + [_full_spec(p) for p in params] + [_row_spec(c, tm) for c in cot_rows],
        out_specs=out_specs, out_shape=out_shape,
        compiler_params=_cp(("arbitrary",)),
    )(*[_win(r)[0] for r in rows], *params, *[_win(c)[0] for c in cot_rows])


def _rms(x, g):
    return x * lax.rsqrt(jnp.mean(x * x, axis=-1, keepdims=True) + EPS) * g


def _f_mod(x, nw, sh, sc):
    return (_rms(x, nw) * (1.0 + sc) + sh).astype(bf16)


def _f_mod_keep(x, nw, sh, sc):
    return _f_mod(x, nw, sh, sc), x


def _f_res_mod(coef, x, o, g, nw, sh, sc):
    x1 = x + coef * g * o.astype(f32)
    return x1, _f_mod(x1, nw, sh, sc)


def _times01(x, e):
    hi = x.astype(bf16)
    r1 = x - hi.astype(f32)
    mid = r1.astype(bf16)
    lo = (r1 - mid.astype(f32)).astype(bf16)
    return (jnp.dot(hi, e, preferred_element_type=f32) + jnp.dot(mid, e, preferred_element_type=f32) +
            jnp.dot(lo, e, preferred_element_type=f32))


@jax.custom_vjp
def _spread_heads(x, e, et):
    return _times01(x, e)


_spread_heads.defvjp(lambda x, e, et: (_times01(x, e), (e, et)),
                     lambda res, g: (_times01(g, res[1]), None, None))


def _f_ssd_pre(pre, dtraw, bias, e, et):
    xc = jax.nn.silu(pre)
    dtx = _spread_heads(jax.nn.softplus(dtraw + bias), e, et)
    return xc[:, :SSD_DI], xc[:, SSD_DI:SSD_DI + SSD_G * SSD_N], xc[:, SSD_DI + SSD_G * SSD_N:], dtx


def _f_ssd_post(y, z, nw):
    yz = y * jax.nn.silu(z)
    w = SSD_DI // SSD_G
    parts = []
    for g in range(SSD_G):
        s = yz[:, g * w:(g + 1) * w]
        parts.append(s * lax.rsqrt(jnp.mean(s * s, axis=-1, keepdims=True) + EPS))
    return (jnp.concatenate(parts, axis=1) * nw).astype(bf16)


def _f_s5_post(yb, u, d):
    return jax.nn.gelu(yb + d * u).astype(bf16)


def _f_merge(pa, glu, ga, gb):
    d = pa.shape[1]
    glu = glu.astype(f32)
    pb = glu[:, :d] * jax.nn.sigmoid(glu[:, d:])
    return (jax.nn.sigmoid(ga) * pa.astype(f32) + jax.nn.sigmoid(gb) * pb).astype(bf16)


def _f_final(x2, o, tgt, g, nw):
    x3 = x2 + 0.5 * g * o.astype(f32)
    y = _rms(x3, nw)
    return 0.5 * jnp.mean(jnp.square(y - tgt), axis=-1, keepdims=True)


def _f_s5_prep(lr, li, ldt, br, bi):
    dt = jnp.exp(ldt)
    lr = jnp.minimum(lr, -1e-4)
    mag = jnp.exp(lr * dt)
    ar = mag * jnp.cos(li * dt)
    ai = mag * jnp.sin(li * dt)
    den = lr * lr + li * li
    nr = ar - 1.0
    kr = (nr * lr + ai * li) / den
    ki = (ai * lr - nr * li) / den
    return ar, ai, kr * br - ki * bi, kr * bi + ki * br


def _shift_down(cur, halo8, j):
    if j == 0:
        return cur
    rolled = pltpu.roll(cur, j, 0)
    row8 = lax.broadcasted_iota(jnp.int32, halo8.shape, 0)
    top = jnp.where(row8 < j, pltpu.roll(halo8, j, 0), rolled[:SUBLANE])
    return jnp.concatenate([top, rolled[SUBLANE:]], axis=0)


def _shift_up(cur, halo8, j):
    if j == 0:
        return cur
    n = cur.shape[0]
    rolled = pltpu.roll(cur, n - j, 0)
    row8 = lax.broadcasted_iota(jnp.int32, halo8.shape, 0)
    bot = jnp.where(row8 >= SUBLANE - j, pltpu.roll(halo8, SUBLANE - j, 0), rolled[n - SUBLANE:])
    return jnp.concatenate([rolled[:n - SUBLANE], bot], axis=0)


def _conv_fwd(proj, c0, w8, b, name):
    t = proj.shape[0]
    cw = 1024
    tm = min(512, t)
    cb0 = c0 // cw
    r8 = tm // SUBLANE

    def body(x_ref, h_ref, w_ref, b_ref, o_ref):
        i = pl.program_id(1)
        x = x_ref[...]
        halo = jnp.where(i > 0, h_ref[...], 0.0)
        acc = b_ref[...] + w_ref[CONV_K - 1:CONV_K, :] * x
        for j in range(1, CONV_K):
            acc = acc + w_ref[CONV_K - 1 - j:CONV_K - j, :] * _shift_down(x, halo, j)
        o_ref[...] = acc

    return pl.pallas_call(
        body, name=name, grid=(CONV_DIM // cw, t // tm),
        in_specs=[pl.BlockSpec((tm, cw), lambda c, i: (i, cb0 + c)),
                  pl.BlockSpec((SUBLANE, cw), lambda c, i: (jnp.maximum(i * r8 - 1, 0), cb0 + c)),
                  pl.BlockSpec((SUBLANE, cw), lambda c, i: (0, c)),
                  pl.BlockSpec((1, cw), lambda c, i: (0, c))],
        out_specs=pl.BlockSpec((tm, cw), lambda c, i: (i, c)),
        out_shape=jax.ShapeDtypeStruct((t, CONV_DIM), f32),
        compiler_params=_cp(("parallel", "arbitrary")),
    )(proj, proj, w8, b)


def _conv_bwd(dpre, proj, c0, w8, name):
    t = proj.shape[0]
    cw = 1024
    tm = min(512, t)
    cb0 = c0 // cw
    r8 = tm // SUBLANE
    nb = t // tm

    def body(d_ref, dn_ref, x_ref, xh_ref, w_ref, dx_ref, dw_ref, db_ref):
        i = pl.program_id(1)
        d = d_ref[...]
        dn = jnp.where(i < nb - 1, dn_ref[...], 0.0)
        x = x_ref[...]
        xh = jnp.where(i > 0, xh_ref[...], 0.0)

        @pl.when(i == 0)
        def _():
            dw_ref[...] = jnp.zeros_like(dw_ref)
            db_ref[...] = jnp.zeros_like(db_ref)

        dx = w_ref[CONV_K - 1:CONV_K, :] * d
        rows = [jnp.sum(d * x, axis=0, keepdims=True)]
        for j in range(1, CONV_K):
            dx = dx + w_ref[CONV_K - 1 - j:CONV_K - j, :] * _shift_up(d, dn, j)
            rows.append(jnp.sum(d * _shift_down(x, xh, j), axis=0, keepdims=True))
        dx_ref[...] = dx.astype(dx_ref.dtype)
        dw = jnp.concatenate([rows[CONV_K - 1 - k] for k in range(CONV_K)] +
                             [jnp.zeros((SUBLANE - CONV_K, cw), f32)], axis=0)
        dw_ref[...] += dw
        db_ref[...] += jnp.sum(d, axis=0, keepdims=True)

    return pl.pallas_call(
        body, name=name, grid=(CONV_DIM // cw, nb),
        in_specs=[pl.BlockSpec((tm, cw), lambda c, i: (i, c)),
                  pl.BlockSpec((SUBLANE, cw), lambda c, i: (jnp.minimum((i + 1) * r8, nb * r8 - 1), c)),
                  pl.BlockSpec((tm, cw), lambda c, i: (i, cb0 + c)),
                  pl.BlockSpec((SUBLANE, cw), lambda c, i: (jnp.maximum(i * r8 - 1, 0), cb0 + c)),
                  pl.BlockSpec((SUBLANE, cw), lambda c, i: (0, c))],
        out_specs=[pl.BlockSpec((tm, cw), lambda c, i: (i, c)),
                   pl.BlockSpec((SUBLANE, cw), lambda c, i: (0, c)),
                   pl.BlockSpec((1, cw), lambda c, i: (0, c))],
        out_shape=[jax.ShapeDtypeStruct((t, CONV_DIM), bf16), jax.ShapeDtypeStruct((SUBLANE, CONV_DIM), f32),
                   jax.ShapeDtypeStruct((1, CONV_DIM), f32)],
        compiler_params=_cp(("parallel", "arbitrary")),
    )(dpre, dpre, proj, proj, w8)


def _cumsum_rows_impl(x):
    n = x.shape[0]
    row = lax.broadcasted_iota(jnp.int32, x.shape, 0)
    s = 1
    while s < n:
        x = x + jnp.where(row >= s, pltpu.roll(x, s, 0), 0.0)
        s *= 2
    return x


@jax.custom_vjp
def _cumsum_rows(x):
    return _cumsum_rows_impl(x)


def _cumsum_rows_bwd(_, g):
    c = _cumsum_rows_impl(g)
    return (c[c.shape[0] - 1:, :] - c + g,)


_cumsum_rows.defvjp(lambda x: (_cumsum_rows_impl(x), None), _cumsum_rows_bwd)


@jax.custom_vjp
def _swap_halves(t):
    return pltpu.roll(t, LANE // 2, 1)


_swap_halves.defvjp(lambda t: (pltpu.roll(t, LANE // 2, 1), None), lambda _, g: (pltpu.roll(g, LANE // 2, 1),))


def _ssd_chunk(xs, bm, cm, dtx, ax, dskx, ht):
    n = SSD_L
    assert n == LANE and SSD_P * 2 == LANE
    row = lax.broadcasted_iota(jnp.int32, (n, n), 0)
    col = lax.broadcasted_iota(jnp.int32, (n, n), 1)
    causal = row >= col
    lo = col < SSD_P
    cs = _cumsum_rows(dtx * ax)
    xdt = xs * dtx
    last = cs[n - 1:n, :]
    cb = lax.dot_general(cm.astype(bf16), bm.astype(bf16), (((1,), (1,)), ((), ())), preferred_element_type=f32)
    y_off = jnp.dot(cm.astype(bf16), ht.astype(bf16), preferred_element_type=f32) * jnp.exp(cs)
    st = lax.dot_general(bm.astype(bf16), (xdt * jnp.exp(last - cs)).astype(bf16), (((0,), (0,)), ((), ())),
                         preferred_element_type=f32)
    ht_new = jnp.exp(last) * ht + st
    ys = []
    for q in range(SSD_R // 2):
        tq = cs[:, q * LANE:(q + 1) * LANE]
        sw = _swap_halves(tq)
        tqt = tq.T
        xq = xdt[:, q * LANE:(q + 1) * LANE].astype(bf16)
        pair = []
        for c_col, r_row in ((jnp.where(lo, tq, sw), tqt[0:1, :]), (jnp.where(lo, sw, tq), tqt[SSD_P:SSD_P + 1, :])):
            decay = jnp.exp(jnp.where(causal, c_col - r_row, -1e30))
            pair.append(jnp.dot((cb * decay).astype(bf16), xq, preferred_element_type=f32))
        ys.append(jnp.where(lo, pair[0], pair[1]))
    return jnp.concatenate(ys, axis=1) + y_off + dskx * xs, ht_new


SSD_GB = 1


def _ssd_specs(nc, rev):
    ch = (lambda c: nc - 1 - c) if rev else (lambda c: c)
    gw = SSD_GB * SSD_R * SSD_P
    return [pl.BlockSpec((SSD_L, gw), lambda g, c: (ch(c), g)),
            pl.BlockSpec((SSD_L, SSD_GB * SSD_N), lambda g, c: (ch(c), g)),
            pl.BlockSpec((SSD_L, SSD_GB * SSD_N), lambda g, c: (ch(c), g)),
            pl.BlockSpec((SSD_L, gw), lambda g, c: (ch(c), g)),
            pl.BlockSpec((1, gw), lambda g, c: (0, g)),
            pl.BlockSpec((1, gw), lambda g, c: (0, g))]


def _ssd_group(refs, q):
    gw = SSD_R * SSD_P
    xs_ref, bm_ref, cm_ref, dt_ref, a_ref, dsk_ref = refs
    ln = slice(q * LANE, (q + 1) * LANE)
    wd = slice(q * gw, (q + 1) * gw)
    return (xs_ref[:, wd], bm_ref[:, ln], cm_ref[:, ln], dt_ref[:, wd], a_ref[:, wd], dsk_ref[:, wd])


def _ssd_fwd(xs, bm, cm, dt4, a4, dsk4, name, ride=None):
    t = xs.shape[0]
    nc = t // SSD_L
    gw = SSD_R * SSD_P

    nr = ride.n if ride is not None else 0
    ng = SSD_G // SSD_GB

    def body(*refs):
        xs_ref, bm_ref, cm_ref, dt_ref, a_ref, dsk_ref = refs[:6]
        r_ins = refs[6:6 + nr]
        y_ref, hs_ref = refs[6 + nr:8 + nr]
        r_outs = refs[8 + nr:8 + 2 * nr]
        h_ref = refs[8 + 2 * nr]
        r_sems = refs[9 + 2 * nr:]
        g, c = pl.program_id(0), pl.program_id(1)
        if nr:
            @pl.when((g == 0) & (c == 0))
            def _():
                ride.start(r_ins, r_outs, r_sems)

        @pl.when(c == 0)
        def _():
            h_ref[...] = jnp.zeros_like(h_ref)

        hs_ref[...] = h_ref[...]
        grp = (xs_ref, bm_ref, cm_ref, dt_ref, a_ref, dsk_ref)
        ops = [_ssd_group(grp, q) + (h_ref[:, q * gw:(q + 1) * gw],) for q in range(SSD_GB)]
        res = [_ssd_chunk(*o) for o in ops]
        for q, (y, hn) in enumerate(res):
            y_ref[:, q * gw:(q + 1) * gw] = y
            h_ref[:, q * gw:(q + 1) * gw] = hn

        if nr:
            @pl.when((g == ng - 1) & (c == nc - 1))
            def _():
                ride.wait(r_ins, r_outs, r_sems)

    res = pl.pallas_call(
        body, name=name, grid=(ng, nc), in_specs=_ssd_specs(nc, False) + (ride.specs if nr else []),
        out_specs=[pl.BlockSpec((SSD_L, SSD_GB * gw), lambda g, c: (c, g)),
                   pl.BlockSpec((None, None, SSD_N, SSD_GB * gw), lambda g, c: (g, c, 0, 0))] +
                  (ride.specs if nr else []),
        out_shape=[jax.ShapeDtypeStruct((t, SSD_DI), f32),
                   jax.ShapeDtypeStruct((ng, nc, SSD_N, SSD_GB * gw), f32)] + (ride.out_shape if nr else []),
        scratch_shapes=[pltpu.VMEM((SSD_N, SSD_GB * gw), f32)] + (ride.scratch if nr else []),
        compiler_params=_cp(("arbitrary", "arbitrary")),
    )(xs, bm, cm, dt4, a4, dsk4, *(ride.srcs if nr else []))
    return res[0], res[1], list(res[2:])


def _ssd_bwd(xs, bm, cm, dt4, a4, dsk4, hs, dy, name, ride=None):
    t = xs.shape[0]
    nc = t // SSD_L
    gw = SSD_R * SSD_P
    rc = lambda c: nc - 1 - c
    nr = ride.n if ride is not None else 0
    ng = SSD_G // SSD_GB

    def body(*refs):
        xs_ref, bm_ref, cm_ref, dt_ref, a_ref, dsk_ref, hs_ref, dy_ref = refs[:8]
        r_ins = refs[8:8 + nr]
        dxs_ref, dbm_ref, dcm_ref, ddt_ref, da_ref, ddsk_ref = refs[8 + nr:14 + nr]
        r_outs = refs[14 + nr:14 + 2 * nr]
        dh_ref = refs[14 + 2 * nr]
        r_sems = refs[15 + 2 * nr:]
        if nr:
            @pl.when((pl.program_id(0) == 0) & (pl.program_id(1) == 0))
            def _():
                ride.start(r_ins, r_outs, r_sems)

        @pl.when(pl.program_id(1) == 0)
        def _():
            dh_ref[...] = jnp.zeros_like(dh_ref)
            da_ref[...] = jnp.zeros_like(da_ref)
            ddsk_ref[...] = jnp.zeros_like(ddsk_ref)

        grp = (xs_ref, bm_ref, cm_ref, dt_ref, a_ref, dsk_ref)
        ops = [_ssd_group(grp, q) + (hs_ref[:, q * gw:(q + 1) * gw],) for q in range(SSD_GB)]
        cts = [(dy_ref[:, q * gw:(q + 1) * gw], dh_ref[:, q * gw:(q + 1) * gw]) for q in range(SSD_GB)]
        grads = [jax.vjp(_ssd_chunk, *o)[1](ct) for o, ct in zip(ops, cts)]
        for q, (dxs, dbm, dcm, ddt, da, ddsk, dh) in enumerate(grads):
            wd = slice(q * gw, (q + 1) * gw)
            ln = slice(q * LANE, (q + 1) * LANE)
            dxs_ref[:, wd] = dxs
            dbm_ref[:, ln] = dbm
            dcm_ref[:, ln] = dcm
            ddt_ref[:, wd] = ddt
            da_ref[:, wd] += da
            ddsk_ref[:, wd] += ddsk
            dh_ref[:, wd] = dh

        if nr:
            @pl.when((pl.program_id(0) == ng - 1) & (pl.program_id(1) == nc - 1))
            def _():
                ride.wait(r_ins, r_outs, r_sems)

    res = pl.pallas_call(
        body, name=name, grid=(ng, nc),
        in_specs=_ssd_specs(nc, True) + [
            pl.BlockSpec((None, None, SSD_N, SSD_GB * gw), lambda g, c: (g, rc(c), 0, 0)),
            pl.BlockSpec((SSD_L, SSD_GB * gw), lambda g, c: (rc(c), g))] + (ride.specs if nr else []),
        out_specs=[pl.BlockSpec((SSD_L, SSD_GB * gw), lambda g, c: (rc(c), g)),
                   pl.BlockSpec((SSD_L, SSD_GB * SSD_N), lambda g, c: (rc(c), g)),
                   pl.BlockSpec((SSD_L, SSD_GB * SSD_N), lambda g, c: (rc(c), g)),
                   pl.BlockSpec((SSD_L, SSD_GB * gw), lambda g, c: (rc(c), g)),
                   pl.BlockSpec((1, SSD_GB * gw), lambda g, c: (0, g)),
                   pl.BlockSpec((1, SSD_GB * gw), lambda g, c: (0, g))] + (ride.specs if nr else []),
        out_shape=[jax.ShapeDtypeStruct((t, SSD_DI), f32), jax.ShapeDtypeStruct((t, SSD_G * SSD_N), f32),
                   jax.ShapeDtypeStruct((t, SSD_G * SSD_N), f32), jax.ShapeDtypeStruct((t, SSD_DI), f32),
                   jax.ShapeDtypeStruct((1, SSD_DI), f32), jax.ShapeDtypeStruct((1, SSD_DI), f32)] +
                  (ride.out_shape if nr else []),
        scratch_shapes=[pltpu.VMEM((SSD_N, SSD_GB * gw), f32)] + (ride.scratch if nr else []),
        compiler_params=_cp(("arbitrary", "arbitrary")),
    )(xs, bm, cm, dt4, a4, dsk4, hs, dy, *(ride.srcs if nr else []))
    return list(res[:6]), list(res[6:])


S5_CH = 1024


S5_NB = 8
S5_UB = 128
S5_SB = 512
_NT = (((1,), (1,)), ((), ()))
_TN = (((0,), (0,)), ((), ()))


def _s5_fwd(proj, u0, bd_c, c_c, ar, ai, name):
    t = proj.shape[0]
    tb = min(128, t)
    ub = u0 // S5_W

    def body(u_ref, bd_ref, cc_ref, ar_ref, ai_ref, s_ref, yb_ref, bu_ref, carry):
        @pl.when(pl.program_id(0) == 0)
        def _():
            carry[...] = jnp.zeros_like(carry)

        u = u_ref[...].astype(bf16)
        for j in range(S5_NB):
            uj = u[:, j * S5_UB:(j + 1) * S5_UB]
            for half in range(2):
                bu_ref[:, half * S5_S + j * S5_SB:half * S5_S + (j + 1) * S5_SB] = jnp.dot(
                    uj, bd_ref[j * S5_UB:(j + 1) * S5_UB, half * S5_SB:(half + 1) * S5_SB], preferred_element_type=f32)

        for c0 in range(0, S5_S, S5_CH):
            re = pl.ds(c0, S5_CH)
            im = pl.ds(S5_S + c0, S5_CH)
            a_r = ar_ref[:, re]
            a_i = ai_ref[:, re]

            def step(k, st, re=re, im=im, a_r=a_r, a_i=a_i):
                sr, si = st
                row = pl.ds(k, 1)
                nr = a_r * sr - a_i * si + bu_ref[row, re]
                ni = a_r * si + a_i * sr + bu_ref[row, im]
                s_ref[row, re] = nr
                s_ref[row, im] = ni
                return nr, ni

            sr, si = lax.fori_loop(0, tb, step, (carry[:, re], carry[:, im]))
            carry[:, re] = sr
            carry[:, im] = si

        for j in range(S5_NB):
            lo, hi = j * S5_SB, (j + 1) * S5_SB
            yb_ref[:, j * S5_UB:(j + 1) * S5_UB] = (
                jnp.dot(s_ref[:, lo:hi].astype(bf16), cc_ref[lo:hi, :], preferred_element_type=f32) +
                jnp.dot(s_ref[:, S5_S + lo:S5_S + hi].astype(bf16), cc_ref[S5_S + lo:S5_S + hi, :],
                        preferred_element_type=f32))

    return pl.pallas_call(
        body, name=name, grid=(t // tb,),
        in_specs=[pl.BlockSpec((tb, S5_W), lambda i: (i, ub)), _full_spec(bd_c), _full_spec(c_c),
                  pl.BlockSpec((1, S5_S), lambda i: (0, 0)), pl.BlockSpec((1, S5_S), lambda i: (0, 0))],
        out_specs=[pl.BlockSpec((tb, 2 * S5_S), lambda i: (i, 0)), pl.BlockSpec((tb, S5_W), lambda i: (i, 0))],
        out_shape=[jax.ShapeDtypeStruct((t, 2 * S5_S), f32), jax.ShapeDtypeStruct((t, S5_W), f32)],
        scratch_shapes=[pltpu.VMEM((tb, 2 * S5_S), f32), pltpu.VMEM((1, 2 * S5_S), f32)],
        compiler_params=_cp(("arbitrary",)),
    )(proj, bd_c, c_c, ar, ai)


def _s5_bwd(dyb, s, proj, u0, bd_c, c_c, ar, ai, du_skip, name, ride=None):
    t = dyb.shape[0]
    tb = min(128, t)
    nb = t // tb
    r8 = tb // SUBLANE
    rb = lambda i: nb - 1 - i
    ub = u0 // S5_W

    def body(ins, outs, g_ref, carry):
        dyb_ref, s_ref, sh_ref, u_ref, skip_ref, bd_ref, cc_ref, ar_ref, ai_ref = ins
        du_ref, dar_ref, dai_ref, dbd_ref, dcc_ref = outs
        ds_ref = g_ref
        i = pl.program_id(0)

        @pl.when(i == 0)
        def _():
            carry[...] = jnp.zeros_like(carry)
            dar_ref[...] = jnp.zeros_like(dar_ref)
            dai_ref[...] = jnp.zeros_like(dai_ref)
            dbd_ref[...] = jnp.zeros_like(dbd_ref)
            dcc_ref[...] = jnp.zeros_like(dcc_ref)

        dyb = dyb_ref[...].astype(bf16)
        for jj in range(2 * S5_NB):
            blk = jj % S5_NB
            g_ref[:, jj * S5_SB:(jj + 1) * S5_SB] = lax.dot_general(
                dyb[:, blk * S5_UB:(blk + 1) * S5_UB], cc_ref[jj * S5_SB:(jj + 1) * S5_SB, :], _NT,
                preferred_element_type=f32)

        has_prev = (i < nb - 1).astype(f32)
        for c0 in range(0, S5_S, S5_CH):
            re = pl.ds(c0, S5_CH)
            im = pl.ds(S5_S + c0, S5_CH)
            a_r = ar_ref[:, re]
            a_i = ai_ref[:, re]

            def upd(st, row, sp_r, sp_i, re=re, im=im, a_r=a_r, a_i=a_i):
                gr, gi, acr, aci = st
                ngr = ds_ref[row, re] + a_r * gr + a_i * gi
                ngi = ds_ref[row, im] + a_r * gi - a_i * gr
                g_ref[row, re] = ngr
                g_ref[row, im] = ngi
                return ngr, ngi, acr + ngr * sp_r + ngi * sp_i, aci + ngi * sp_r - ngr * sp_i

            def step(k, st, re=re, im=im, upd=upd):
                tt = tb - 1 - k
                prev = pl.ds(tt - 1, 1)
                return upd(st, pl.ds(tt, 1), s_ref[prev, re], s_ref[prev, im])

            zero = jnp.zeros((1, S5_CH), f32)
            st = lax.fori_loop(0, tb - 1, step, (carry[:, re], carry[:, im], zero, zero))
            last = pl.ds(SUBLANE - 1, 1)
            gr, gi, acr, aci = upd(st, pl.ds(0, 1), sh_ref[last, re] * has_prev, sh_ref[last, im] * has_prev)
            carry[:, re] = gr
            carry[:, im] = gi
            dar_ref[:, re] += acr
            dai_ref[:, re] += aci

        u = u_ref[...].astype(bf16)
        for j in range(S5_NB):
            lo, hi = j * S5_SB, (j + 1) * S5_SB
            blk = slice(j * S5_UB, (j + 1) * S5_UB)
            g_re = g_ref[:, lo:hi].astype(bf16)
            g_im = g_ref[:, S5_S + lo:S5_S + hi].astype(bf16)
            du = (lax.dot_general(g_re, bd_ref[blk, :S5_SB], _NT, preferred_element_type=f32) +
                  lax.dot_general(g_im, bd_ref[blk, S5_SB:], _NT, preferred_element_type=f32) + skip_ref[:, blk])
            du_ref[:, blk] = du.astype(du_ref.dtype)
            dbd_ref[blk, :S5_SB] += lax.dot_general(u[:, blk], g_re, _TN, preferred_element_type=f32)
            dbd_ref[blk, S5_SB:] += lax.dot_general(u[:, blk], g_im, _TN, preferred_element_type=f32)
            dcc_ref[lo:hi, :] += lax.dot_general(s_ref[:, lo:hi].astype(bf16), dyb[:, blk], _TN,
                                                 preferred_element_type=f32)
            dcc_ref[S5_S + lo:S5_S + hi, :] += lax.dot_general(s_ref[:, S5_S + lo:S5_S + hi].astype(bf16), dyb[:, blk],
                                                               _TN, preferred_element_type=f32)

    row_blk = lambda w: pl.BlockSpec((tb, w), lambda i: (rb(i), 0))
    const = lambda shape: pl.BlockSpec(shape, lambda i: (0, 0))
    return _call_with_ride(
        body, (nb,),
        [row_blk(S5_W), row_blk(2 * S5_S),
         pl.BlockSpec((SUBLANE, 2 * S5_S), lambda i: (jnp.maximum(rb(i) * r8 - 1, 0), 0)),
         pl.BlockSpec((tb, S5_W), lambda i: (rb(i), ub)), row_blk(S5_W), const(bd_c.shape), const(c_c.shape),
         const((1, S5_S)), const((1, S5_S))],
        [dyb, s, s, proj, du_skip, bd_c, c_c, ar, ai],
        [row_blk(S5_W), const((1, S5_S)), const((1, S5_S)), const(bd_c.shape), const(c_c.shape)],
        [jax.ShapeDtypeStruct((t, S5_W), bf16), jax.ShapeDtypeStruct((1, S5_S), f32),
         jax.ShapeDtypeStruct((1, S5_S), f32), jax.ShapeDtypeStruct(bd_c.shape, f32),
         jax.ShapeDtypeStruct(c_c.shape, f32)],
        name, ride, scratch=[pltpu.VMEM((tb, 2 * S5_S), f32), pltpu.VMEM((1, 2 * S5_S), f32)], sem=("arbitrary",))


def _ada_fwd(c_all, w, b, name):
    d, n = w.shape
    tn = _tile(n, 1536)

    def body(c_ref, w_ref, b_ref, o_ref):
        a = jax.nn.silu(c_ref[...]).astype(bf16)
        o_ref[...] = jnp.dot(a, w_ref[...].astype(bf16), preferred_element_type=f32) + b_ref[...]

    return pl.pallas_call(
        body, name=name, grid=(n // tn,),
        in_specs=[pl.BlockSpec(c_all.shape, lambda j: (0, 0)), pl.BlockSpec((d, tn), lambda j: (0, j)),
                  pl.BlockSpec((1, tn), lambda j: (0, j))],
        out_specs=pl.BlockSpec((c_all.shape[0], tn), lambda j: (0, j)),
        out_shape=jax.ShapeDtypeStruct((c_all.shape[0], n), f32),
        compiler_params=_cp(("parallel",)),
    )(c_all, w, b)


def _ada_bwd(c_all, dm, name):
    d = c_all.shape[1]
    n = dm.shape[1]
    tn = _tile(n, 1536)

    def body(c_ref, dm_ref, o_ref):
        a = jax.nn.silu(c_ref[...]).astype(bf16)
        o_ref[...] = lax.dot_general(a, dm_ref[...].astype(bf16), (((0,), (0,)), ((), ())), preferred_element_type=f32)

    return pl.pallas_call(
        body, name=name, grid=(n // tn,),
        in_specs=[pl.BlockSpec(c_all.shape, lambda j: (0, 0)), pl.BlockSpec((dm.shape[0], tn), lambda j: (0, j))],
        out_specs=pl.BlockSpec((d, tn), lambda j: (0, j)),
        out_shape=jax.ShapeDtypeStruct((d, n), f32),
        compiler_params=_cp(("parallel",)),
    )(c_all, dm)


def _blk_rows(r, c, nbuf, itemsize=4):
    tr = _tile(r, max(SUBLANE, (RW_VMEM_BUDGET // (2 * nbuf * c * itemsize)) // 16 * 16), 16)
    return tr if r % tr == 0 else r


def _cast_bf16(w, name):
    r, c = w.shape
    tr = _blk_rows(r, c, 2)

    def body(w_ref, o_ref):
        o_ref[...] = w_ref[...].astype(bf16)

    return pl.pallas_call(
        body, name=name, grid=(r // tr,), in_specs=[pl.BlockSpec((tr, c), lambda i: (i, 0))],
        out_specs=pl.BlockSpec((tr, c), lambda i: (i, 0)), out_shape=jax.ShapeDtypeStruct((r, c), bf16),
        compiler_params=_cp(("parallel",)),
    )(w)


def _sum_lead(parts, name):
    n, r, c = parts.shape
    tr = _blk_rows(r, c, n + 2)

    def body(p_ref, o_ref):
        acc = p_ref[0].astype(f32)
        for q in range(1, n):
            acc = acc + p_ref[q].astype(f32)
        o_ref[...] = acc

    return pl.pallas_call(
        body, name=name, grid=(r // tr,), in_specs=[pl.BlockSpec((n, tr, c), lambda i: (0, i, 0))],
        out_specs=pl.BlockSpec((tr, c), lambda i: (i, 0)), out_shape=jax.ShapeDtypeStruct((r, c), f32),
        compiler_params=_cp(("parallel",)),
    )(parts)


def _adamw(w, m, v, parts, name):
    r, c = w.shape
    npart = len(parts)
    tr = _blk_rows(r, c, 7 + npart)
    c1 = 1.0 - ADAM_B1 ** ADAM_STEP
    c2 = 1.0 - ADAM_B2 ** ADAM_STEP

    def body(*refs):
        w_ref, m_ref, v_ref = refs[:3]
        g_ref, d_ref, nm_ref, nv_ref = refs[3 + npart:]
        g = refs[3][...].astype(f32)
        for p in refs[4:3 + npart]:
            g = g + p[...].astype(f32)
        nm = ADAM_B1 * m_ref[...] + (1.0 - ADAM_B1) * g
        nv = ADAM_B2 * v_ref[...] + (1.0 - ADAM_B2) * jnp.square(g)
        g_ref[...] = g
        nm_ref[...] = nm
        nv_ref[...] = nv
        d_ref[...] = -ADAM_LR * ((nm / c1) / (jnp.sqrt(nv / c2) + ADAM_EPS) + ADAM_WD * w_ref[...])

    spec = pl.BlockSpec((tr, c), lambda i: (i, 0))
    return pl.pallas_call(
        body, name=name, grid=(r // tr,), in_specs=[spec] * (3 + npart), out_specs=[spec] * 4,
        out_shape=[jax.ShapeDtypeStruct((r, c), f32)] * 4, compiler_params=_cp(("parallel",)),
    )(w, m, v, *parts)


def _ag_small(x_shard, name):
    m_per, n = x_shard.shape

    def body(x_ref, out_ref, send_sems, recv_sems, local_sem):
        x, y, c = lax.axis_index("x"), lax.axis_index("y"), lax.axis_index("c")
        me, sibling = (x, y, c), (x, y, 1 - c)
        chips = [(1 - x, y), (x, 1 - y), (1 - x, 1 - y)]

        def rows(px, py, pc):
            return out_ref.at[pl.ds((4 * px + 2 * py + pc) * m_per, m_per), :]

        def copy(k, block, to, src=None):
            return pltpu.make_async_remote_copy(
                src_ref=rows(*block) if src is None else src, dst_ref=rows(*block),
                send_sem=send_sems.at[k], recv_sem=recv_sems.at[k], device_id=to, device_id_type=MESH)

        mine = pltpu.make_async_copy(x_ref, rows(*me), local_sem)
        mine.start()
        first = [copy(0, me, sibling, src=x_ref)]
        first += [copy(1 + j, me, (*chip, c), src=x_ref) for j, chip in enumerate(chips)]
        for cp in first:
            cp.start()
        passed = [copy(4 + j, (*chip, c), sibling) for j, chip in enumerate(chips)]
        for j, chip in enumerate(chips):
            copy(1 + j, (*chip, c), me).wait_recv()
            passed[j].start()
        copy(0, sibling, me).wait_recv()
        for j, chip in enumerate(chips):
            copy(4 + j, (*chip, 1 - c), me).wait_recv()
        for cp in first + passed:
            cp.wait_send()
        mine.wait()

    return pl.pallas_call(
        body, name=name, out_shape=jax.ShapeDtypeStruct((8 * m_per, n), x_shard.dtype),
        in_specs=[pl.BlockSpec(memory_space=pltpu.VMEM)], out_specs=pl.BlockSpec(memory_space=pltpu.VMEM),
        scratch_shapes=[pltpu.SemaphoreType.DMA((7,)), pltpu.SemaphoreType.DMA((7,)), pltpu.SemaphoreType.DMA],
        compiler_params=pltpu.CompilerParams(vmem_limit_bytes=VMEM_LIMIT),
    )(x_shard)


def _run_ride(ride, name):
    n = ride.n

    def body(*refs):
        ride.start(refs[:n], refs[n:2 * n], refs[2 * n:])
        ride.wait(refs[:n], refs[n:2 * n], refs[2 * n:])

    return pl.pallas_call(
        body, name=name, out_shape=ride.out_shape, in_specs=ride.specs, out_specs=ride.specs,
        scratch_shapes=ride.scratch,
    )(*ride.srcs)


class _Ride:
    def __init__(self, srcs, scatter):
        self.srcs, self.scatter, self.n = list(srcs), scatter, len(srcs)
        n = self.n
        self.out_shape = [jax.ShapeDtypeStruct(s.shape if scatter else (4,) + s.shape, s.dtype) for s in srcs]
        self.specs = [pl.BlockSpec(memory_space=pl.ANY)] * n
        self.scratch = [pltpu.SemaphoreType.DMA((3 * n,)), pltpu.SemaphoreType.DMA((3 * n,)),
                        pltpu.SemaphoreType.DMA((n,))]

    def _copies(self, ins, outs, sems):
        send_sems, recv_sems, local_sems = sems
        x, y, c = lax.axis_index("x"), lax.axis_index("y"), lax.axis_index("c")
        my_k = 2 * x + y
        peers = [(1 - x, y), (x, 1 - y), (1 - x, 1 - y)]
        local, sends, recvs = [], [], []
        for a in range(self.n):
            own = ins[a].at[my_k] if self.scatter else ins[a]
            local.append(pltpu.make_async_copy(own, outs[a].at[my_k], local_sems.at[a]))
            for j, (px, py) in enumerate(peers):
                sems_j = dict(send_sem=send_sems.at[3 * a + j], recv_sem=recv_sems.at[3 * a + j],
                              device_id=(px, py, c), device_id_type=MESH)
                src = ins[a].at[2 * px + py] if self.scatter else ins[a]
                sends.append(pltpu.make_async_remote_copy(src_ref=src, dst_ref=outs[a].at[my_k], **sems_j))
                landed = outs[a].at[2 * px + py]
                recvs.append(pltpu.make_async_remote_copy(src_ref=landed, dst_ref=landed, **sems_j))
        return local, sends, recvs

    def start(self, ins, outs, sems):
        local, sends, _ = self._copies(ins, outs, sems)
        for cp in local + sends:
            cp.start()

    def wait(self, ins, outs, sems):
        local, sends, recvs = self._copies(ins, outs, sems)
        for cp in recvs:
            cp.wait_recv()
        for cp in sends:
            cp.wait_send()
        for cp in local:
            cp.wait()


class _RideGather:
    def __init__(self, srcs):
        self.srcs, self.n = list(srcs), len(srcs)
        n = self.n
        assert all(s.shape[0] % 32 == 0 for s in srcs)
        self.out_shape = [jax.ShapeDtypeStruct((4,) + s.shape, s.dtype) for s in srcs]
        self.specs = [pl.BlockSpec(memory_space=pl.ANY)] * n
        dma = pltpu.SemaphoreType.DMA
        self.scratch = [dma((3 * n,)), dma((3 * n,)), dma((3 * n,)), dma((3 * n,)), dma((n,))]

    def _copies(self, ins, outs, sems):
        send_sems, recv_sems, pass_send, pass_recv, local_sems = sems
        x, y, c = lax.axis_index("x"), lax.axis_index("y"), lax.axis_index("c")
        my_k = 2 * x + y
        peers = [(1 - x, y), (x, 1 - y), (1 - x, 1 - y)]
        local, sends, recvs, passes, pass_recvs = [], [], [], [], []
        for a in range(self.n):
            half = self.srcs[a].shape[0] // 2
            mine = pl.ds(pl.multiple_of(c * half, 16), half)
            other = pl.ds(pl.multiple_of((1 - c) * half, 16), half)
            local.append(pltpu.make_async_copy(ins[a], outs[a].at[my_k], local_sems.at[a]))
            for j, (px, py) in enumerate(peers):
                q = 3 * a + j
                over_ici = dict(send_sem=send_sems.at[q], recv_sem=recv_sems.at[q], device_id=(px, py, c),
                                device_id_type=MESH)
                to_sibling = dict(send_sem=pass_send.at[q], recv_sem=pass_recv.at[q], device_id=(x, y, 1 - c),
                                  device_id_type=MESH)
                sends.append(pltpu.make_async_remote_copy(src_ref=ins[a].at[mine], dst_ref=outs[a].at[my_k, mine],
                                                          **over_ici))
                landed = outs[a].at[2 * px + py, mine]
                recvs.append(pltpu.make_async_remote_copy(src_ref=landed, dst_ref=landed, **over_ici))
                passes.append(pltpu.make_async_remote_copy(src_ref=landed, dst_ref=landed, **to_sibling))
                from_sibling = outs[a].at[2 * px + py, other]
                pass_recvs.append(pltpu.make_async_remote_copy(src_ref=from_sibling, dst_ref=from_sibling, **to_sibling))
        return local, sends, recvs, passes, pass_recvs

    def start(self, ins, outs, sems):
        local, sends = self._copies(ins, outs, sems)[:2]
        for cp in local + sends:
            cp.start()

    def wait(self, ins, outs, sems):
        local, sends, recvs, passes, pass_recvs = self._copies(ins, outs, sems)
        for rc, ps in zip(recvs, passes):
            rc.wait_recv()
            ps.start()
        for cp in pass_recvs:
            cp.wait_recv()
        for cp in sends + passes:
            cp.wait_send()
        for cp in local:
            cp.wait()


class _RideSwap:
    def __init__(self, srcs):
        self.srcs, self.n = list(srcs), len(srcs)
        self.out_shape = [jax.ShapeDtypeStruct(s.shape, s.dtype) for s in srcs]
        self.specs = [pl.BlockSpec(memory_space=pl.ANY)] * self.n
        self.scratch = [pltpu.SemaphoreType.DMA((self.n,)), pltpu.SemaphoreType.DMA((self.n,))]

    def _copies(self, ins, outs, sems):
        send_sems, recv_sems = sems
        sib = (lax.axis_index("x"), lax.axis_index("y"), 1 - lax.axis_index("c"))
        return [pltpu.make_async_remote_copy(src_ref=ins[a], dst_ref=outs[a], send_sem=send_sems.at[a],
                                             recv_sem=recv_sems.at[a], device_id=sib, device_id_type=MESH)
                for a in range(self.n)]

    def start(self, ins, outs, sems):
        for cp in self._copies(ins, outs, sems):
            cp.start()

    def wait(self, ins, outs, sems):
        cps = self._copies(ins, outs, sems)
        for cp in cps:
            cp.wait_recv()
        for cp in cps:
            cp.wait_send()


class _Rides:
    def __init__(self, rides):
        self.rides = list(rides)
        self.n = sum(r.n for r in self.rides)
        self.srcs = [s for r in self.rides for s in r.srcs]
        self.out_shape = [s for r in self.rides for s in r.out_shape]
        self.specs = [s for r in self.rides for s in r.specs]
        self.scratch = [s for r in self.rides for s in r.scratch]

    def _each(self, ins, outs, sems):
        i = k = 0
        for r in self.rides:
            yield r, ins[i:i + r.n], outs[i:i + r.n], sems[k:k + len(r.scratch)]
            i += r.n
            k += len(r.scratch)

    def start(self, ins, outs, sems):
        for r, a, b, c in self._each(ins, outs, sems):
            r.start(a, b, c)

    def wait(self, ins, outs, sems):
        for r, a, b, c in self._each(ins, outs, sems):
            r.wait(a, b, c)

    def split(self, results):
        out, i = [], 0
        for r in self.rides:
            out.append(results[i:i + r.n])
            i += r.n
        return out


def _gather8(vec, name):
    size = vec.shape[0]
    n = _round_up(size, SUBLANE * LANE)
    blk = jnp.concatenate([vec, jnp.zeros((n - size,), f32)]).reshape(SUBLANE, n // SUBLANE)
    out = _ag_small(blk, name)
    return out.reshape(8, n)[:, :size]


def _head_spread_matrices():
    e = np.zeros((LANE, SSD_DI), np.float32)
    for h in range(SSD_HEADS):
        e[h, h * SSD_P:(h + 1) * SSD_P] = 1.0
    return jnp.asarray(e, bf16), jnp.asarray(e.T, bf16)


def _heads_to_lanes(v):
    return jnp.repeat(v, SSD_P).reshape(1, SSD_DI)


def _block_diag8(blocks):
    g, r, c = blocks.shape
    b = blocks.reshape(g // S5_NB, S5_NB, r, c)
    eye = jnp.eye(S5_NB, dtype=bool)[None, :, None, :, None]
    return jnp.where(eye, b[:, :, :, None, :], jnp.zeros((), blocks.dtype)).reshape(g * r, S5_NB * c)


def _diag8(mat, r, c):
    g = mat.shape[0] // r
    m = mat.reshape(g // S5_NB, S5_NB, r, S5_NB, c)
    eye = jnp.eye(S5_NB, dtype=bool)[None, :, None, :, None]
    return jnp.where(eye, m, 0.0).sum(axis=3).reshape(g, r, c)


class _Layout:
    def __init__(self, d):
        self.d = d
        self.z, self.xbc, self.u = 0, SSD_DI, SSD_DI + CONV_DIM
        self.ga = self.u + S5_W
        self.gb = self.ga + d
        self.dt = self.gb + d
        self.np_ = self.dt + LANE
        self.in_cols = SSD_DI + CONV_DIM + SSD_HEADS + S5_W + 2 * d
        off_dt = SSD_DI + CONV_DIM
        off_u = off_dt + SSD_HEADS
        off_g = off_u + S5_W
        self.src = [(0, off_dt), (off_u, off_u + S5_W + 2 * d), (off_dt, off_u)]

    def arrange_slabs(self, g):
        pieces = [p for lo, hi in self.src for p in _cols_from_slabs(g, lo, hi)]
        pieces.append(jnp.zeros((g.shape[1], LANE - SSD_HEADS), g.dtype))
        return jnp.concatenate(pieces, axis=1)

    def restore_slabs(self, chunks):
        (a0, a1), (b0, b1), (c0, c1) = self.src
        n_a, n_b = a1 - a0, b1 - b0
        segs = [(a0, a1, 0), (c0, c1, n_a + n_b), (b0, b1, n_a)]
        firsts = np.cumsum([0] + [c.shape[1] for c in chunks])

        def take(lo, hi):
            return [c[:, max(lo, f) - f:min(hi, f + c.shape[1]) - f] for c, f in zip(chunks, firsts)
                    if max(lo, f) < min(hi, f + c.shape[1])]

        cs = self.in_cols // 4
        slabs = []
        for k in range(4):
            lo, hi = k * cs, (k + 1) * cs
            parts = [p for s0, s1, pos in segs if max(lo, s0) < min(hi, s1)
                     for p in take(pos + max(lo, s0) - s0, pos + min(hi, s1) - s0)]
            slabs.append(jnp.concatenate(parts, axis=1))
        return jnp.stack(slabs)


def _cols_from_slabs(g, start, stop):
    c = g.shape[2]
    return [g[k][:, max(start, k * c) - k * c:min(stop, (k + 1) * c) - k * c] for k in range(4)
            if max(start, k * c) < min(stop, (k + 1) * c)]


def _unshard_cols(g):
    return jnp.concatenate([g[k] for k in range(4)], axis=1)


def _shard_cols(w):
    r, c4 = w.shape
    return w.reshape(r, 4, c4 // 4).transpose(1, 0, 2)


def kernel(x, c, w_ada, b_ada, norm_ffn1, w_ffn1_in, w_ffn1_out, norm_mix, w_in, conv_w, conv_b, dt_bias, a_log, d_ssd, ssd_norm_w, w_a_proj, s5_lambda_re, s5_lambda_im, s5_b_re, s5_b_im, s5_c_re, s5_c_im, s5_d, s5_log_dt, w_b_glu, w_out, norm_ffn2, w_ffn2_in, w_ffn2_out, norm_final, loss_target, m_w_ada, m_b_ada, m_norm_ffn1, m_w_ffn1_in, m_w_ffn1_out, m_norm_mix, m_w_in, m_conv_w, m_conv_b, m_dt_bias, m_a_log, m_d_ssd, m_ssd_norm_w, m_w_a_proj, m_s5_lambda_re, m_s5_lambda_im, m_s5_b_re, m_s5_b_im, m_s5_c_re, m_s5_c_im, m_s5_d, m_s5_log_dt, m_w_b_glu, m_w_out, m_norm_ffn2, m_w_ffn2_in, m_w_ffn2_out, m_norm_final, v_w_ada, v_b_ada, v_norm_ffn1, v_w_ffn1_in, v_w_ffn1_out, v_norm_mix, v_w_in, v_conv_w, v_conv_b, v_dt_bias, v_a_log, v_d_ssd, v_ssd_norm_w, v_w_a_proj, v_s5_lambda_re, v_s5_lambda_im, v_s5_b_re, v_s5_b_im, v_s5_c_re, v_s5_c_im, v_s5_d, v_s5_log_dt, v_w_b_glu, v_w_out, v_norm_ffn2, v_w_ffn2_in, v_w_ffn2_out, v_norm_final):
    W = dict(w_ada=w_ada, b_ada=b_ada, norm_ffn1=norm_ffn1, w_ffn1_in=w_ffn1_in, w_ffn1_out=w_ffn1_out, norm_mix=norm_mix, w_in=w_in, conv_w=conv_w, conv_b=conv_b, dt_bias=dt_bias, a_log=a_log, d_ssd=d_ssd, ssd_norm_w=ssd_norm_w, w_a_proj=w_a_proj, s5_lambda_re=s5_lambda_re, s5_lambda_im=s5_lambda_im, s5_b_re=s5_b_re, s5_b_im=s5_b_im, s5_c_re=s5_c_re, s5_c_im=s5_c_im, s5_d=s5_d, s5_log_dt=s5_log_dt, w_b_glu=w_b_glu, w_out=w_out, norm_ffn2=norm_ffn2, w_ffn2_in=w_ffn2_in, w_ffn2_out=w_ffn2_out, norm_final=norm_final)
    Mo = dict(w_ada=m_w_ada, b_ada=m_b_ada, norm_ffn1=m_norm_ffn1, w_ffn1_in=m_w_ffn1_in, w_ffn1_out=m_w_ffn1_out, norm_mix=m_norm_mix, w_in=m_w_in, conv_w=m_conv_w, conv_b=m_conv_b, dt_bias=m_dt_bias, a_log=m_a_log, d_ssd=m_d_ssd, ssd_norm_w=m_ssd_norm_w, w_a_proj=m_w_a_proj, s5_lambda_re=m_s5_lambda_re, s5_lambda_im=m_s5_lambda_im, s5_b_re=m_s5_b_re, s5_b_im=m_s5_b_im, s5_c_re=m_s5_c_re, s5_c_im=m_s5_c_im, s5_d=m_s5_d, s5_log_dt=m_s5_log_dt, w_b_glu=m_w_b_glu, w_out=m_w_out, norm_ffn2=m_norm_ffn2, w_ffn2_in=m_w_ffn2_in, w_ffn2_out=m_w_ffn2_out, norm_final=m_norm_final)
    Vo = dict(w_ada=v_w_ada, b_ada=v_b_ada, norm_ffn1=v_norm_ffn1, w_ffn1_in=v_w_ffn1_in, w_ffn1_out=v_w_ffn1_out, norm_mix=v_norm_mix, w_in=v_w_in, conv_w=v_conv_w, conv_b=v_conv_b, dt_bias=v_dt_bias, a_log=v_a_log, d_ssd=v_d_ssd, ssd_norm_w=v_ssd_norm_w, w_a_proj=v_w_a_proj, s5_lambda_re=v_s5_lambda_re, s5_lambda_im=v_s5_lambda_im, s5_b_re=v_s5_b_re, s5_b_im=v_s5_b_im, s5_c_re=v_s5_c_re, s5_c_im=v_s5_c_im, s5_d=v_s5_d, s5_log_dt=v_s5_log_dt, w_b_glu=v_w_b_glu, w_out=v_w_out, norm_ffn2=v_norm_ffn2, w_ffn2_in=v_w_ffn2_in, w_ffn2_out=v_w_ffn2_out, norm_final=v_norm_final)

    t, d = x.shape[1], x.shape[2]
    ff = 4 * w_ffn1_out.shape[1]
    ffp = _round_up(ff, 512)
    lay = _Layout(d)
    xi, yi, ci = lax.axis_index("x"), lax.axis_index("y"), lax.axis_index("c")
    k_me = 2 * xi + yi
    e_me = 4 * xi + 2 * yi + ci
    x2d = x[0]
    tgt = loss_target[0]

    cw_cols = conv_w.shape[2]
    g1 = _gather8(jnp.concatenate([c[0], conv_w[0].reshape(-1)]), "gather_c_convw")
    c_all = g1[:, :d]
    conv_full = g1[::2, d:].reshape(4, CONV_K, cw_cols).transpose(1, 0, 2).reshape(CONV_K, CONV_DIM)
    conv_w8 = jnp.zeros((SUBLANE, CONV_DIM), f32).at[:CONV_K].set(conv_full)

    n_ada_loc = w_ada.shape[2]
    b_loc = lax.dynamic_slice(b_ada, (0, k_me * n_ada_loc), (1, n_ada_loc))
    mods_part = _ada_fwd(c_all, w_ada[0], b_loc, "ada_fwd")
    g2 = _gather8(mods_part.reshape(-1), "gather_mods").reshape(8, 8, n_ada_loc)
    mods = lax.dynamic_index_in_dim(g2[::2], e_me, axis=1, keepdims=False).reshape(N_ADA, d)
    sh1, sc1, gt1, sh2, sc2, gt2, sh3, sc3, gt3 = [mods[i:i + 1] for i in range(N_ADA)]

    cast = {n: _cast_bf16(W[n][0], "cast_" + n) for n in BIG}

    def gather_of(names):
        return _RideGather([cast[n] for n in names])

    def rows_of(g):
        return g.reshape(4 * g.shape[1], g.shape[2])

    def ffn_in(g):
        z = jnp.zeros((g.shape[1], ffp - ff), g.dtype)
        return jnp.concatenate([g[0], g[1], z, g[2], g[3], z], axis=1)

    def ffn_out(g):
        return jnp.concatenate([rows_of(g), jnp.zeros((ffp - ff, g.shape[2]), g.dtype)], axis=0)

    nf1, nmx, nf2 = norm_ffn1, norm_mix, norm_ffn2
    nfin = norm_final.reshape(1, d)

    (g_w1i,) = _run_ride(gather_of(['w_ffn1_in']), "gather_w_ffn1_in")
    w1i = ffn_in(g_w1i)
    (h1,) = _rw(_f_mod, [x2d], [nf1, sh1, sc1], [(d, bf16)], name="mod1")
    ab1, act1, (g_w1o, g_win) = _ffn_in(h1, w1i, "ffn1_in", ride=gather_of(['w_ffn1_out', 'w_in']))
    w1o = ffn_out(g_w1o)
    w_inr = lay.arrange_slabs(g_win)
    f1, (g_wa, g_wglu, g_wo) = _mm(act1, w1o, 'nn', out_dtype=bf16, name="ffn1_out",
                                   ride=gather_of(['w_a_proj', 'w_b_glu', 'w_out']))
    w_a = rows_of(g_wa)
    w_glu, w_o = _unshard_cols(g_wglu), rows_of(g_wo)
    res1 = functools.partial(_f_res_mod, 0.5)
    x1, h2 = _rw(res1, [x2d, f1], [gt1, nmx, sh2, sc2], [(d, f32), (d, bf16)], name="res1_mod2")
    proj, (g_w2i,) = _mm(h2, w_inr, 'nn', out_dtype=f32, name="in_proj", ride=gather_of(['w_ffn2_in']))
    w2i = ffn_in(g_w2i)

    pre = _conv_fwd(proj, lay.xbc, conv_w8, conv_b, "conv_fwd")
    spread, spread_t = _head_spread_matrices()
    bias128 = jnp.zeros((1, LANE), f32).at[:, :SSD_HEADS].set(dt_bias)
    xs, bm, cm, dt4 = _rw(_f_ssd_pre, [pre, (proj, lay.dt, LANE)], [bias128, spread, spread_t],
                          [(SSD_DI, f32), (SSD_G * SSD_N, f32), (SSD_G * SSD_N, f32), (SSD_DI, f32)],
                          name="ssd_pre")

    def head_params(a_log_, d_ssd_):
        return _heads_to_lanes(-jnp.exp(a_log_[0])), _heads_to_lanes(d_ssd_[0])

    (a4, dsk4), head_vjp = jax.vjp(head_params, a_log, d_ssd)
    y_ssd, hs, (g_w2o,) = _ssd_fwd(xs, bm, cm, dt4, a4, dsk4, "ssd_fwd", ride=gather_of(['w_ffn2_out']))
    w2o = ffn_out(g_w2o)
    (y_a,) = _rw(_f_ssd_post, [y_ssd, (proj, lay.z, SSD_DI)], [ssd_norm_w], [(SSD_DI, bf16)], name="ssd_post")
    p_a = _mm(y_a, w_a, 'nn', out_dtype=bf16, name="a_proj")

    col = lambda v: v.reshape(S5_S, 1)
    ldt_col = jnp.repeat(s5_log_dt[0], S5_P).reshape(S5_S, 1)
    prep_rows = [col(s5_lambda_re[0]), col(s5_lambda_im[0]), ldt_col,
                 s5_b_re[0].reshape(S5_S, S5_I), s5_b_im[0].reshape(S5_S, S5_I)]
    ar, ai, bbr, bbi = _rw(_f_s5_prep, prep_rows, [], [(1, f32), (1, f32), (S5_I, f32), (S5_I, f32)],
                           name="s5_prep", tm=512)
    to_bd = lambda bb: _block_diag8(bb.reshape(S5_G, S5_P, S5_I).transpose(0, 2, 1).astype(bf16))
    bd_c = jnp.concatenate([to_bd(bbr), to_bd(bbi)], axis=1)
    c_c = jnp.concatenate([_block_diag8(s5_c_re[0].transpose(0, 2, 1).astype(bf16)),
                           _block_diag8((-s5_c_im[0]).transpose(0, 2, 1).astype(bf16))], axis=0)
    ar_row, ai_row = ar.reshape(1, S5_S), ai.reshape(1, S5_S)
    s5s, yb = _s5_fwd(proj, lay.u, bd_c, c_c, ar_row, ai_row, "s5_fwd")
    d_row = s5_d[0].reshape(1, S5_W)
    (gl,) = _rw(_f_s5_post, [yb, (proj, lay.u, S5_W)], [d_row], [(S5_W, bf16)], name="s5_post")
    glu = _mm(gl, w_glu, 'nn', out_dtype=bf16, name="glu_proj")

    merge_rows = [p_a, glu, (proj, lay.ga, d), (proj, lay.gb, d)]
    (merged,) = _rw(_f_merge, merge_rows, [], [(d, bf16)], name="merge")
    o_mix = _mm(merged, w_o, 'nn', out_dtype=bf16, name="out_proj")
    res2 = functools.partial(_f_res_mod, 1.0)
    x2, h3 = _rw(res2, [x1, o_mix], [gt2, nf2, sh3, sc3], [(d, f32), (d, bf16)], name="res2_mod3")
    ab2, act2, _ = _ffn_in(h3, w2i, "ffn2_in")
    f2 = _mm(act2, w2o, 'nn', out_dtype=bf16, name="ffn2_out")

    ones = jnp.ones((t, 1), f32)
    dx2, df2, dgt3, dnfin, loss_acc = _rw_vjp(_f_final, [x2, f2, tgt], [gt3, nfin], [ones],
                                              row_grads=[f32, bf16, None], param_grads=[True, True],
                                              name="loss_and_bwd", total_of=0)
    loss = lax.psum(loss_acc[0, 0], AXES)
    def ffn_in_back(g):
        hf = ff // 2
        return jnp.stack([g[:, :hf], g[:, hf:ff], g[:, ffp:ffp + hf], g[:, ffp + hf:ffp + ff]])

    def rows_back(g, rows):
        return g[:rows].reshape(4, rows // 4, g.shape[1])

    def scatter_of(pairs):
        return _Ride([g for _, g in pairs], True)

    terms = {}
    dab2, _ = _ffn_out_dx(df2, w2o, ab2, "ffn2_out_dx")
    dw2o = _mm(act2, df2, 'tn', out_dtype=bf16, name="ffn2_out_dw")
    dh3, (terms['w_ffn2_out'],) = _ffn_in_dx(dab2, w2i, "ffn2_in_dx",
                                             ride=scatter_of([('w_ffn2_out', rows_back(dw2o, ff))]))
    dw2i = _ffn_in_dw(h3, dab2, "ffn2_in_dw")
    dx1, do_mix, dgt2, dnf2, dsh3, dsc3 = _rw_vjp(
        res2, [x1, o_mix], [gt2, nf2, sh3, sc3], [dx2, dh3], row_grads=[f32, bf16], param_grads=[True] * 4,
        name="res2_mod3_bwd")
    dmerged = _mm(do_mix, w_o, 'nt', out_dtype=bf16, name="out_proj_dx")
    dw_o = _mm(merged, do_mix, 'tn', out_dtype=bf16, name="out_proj_dw")
    dp_a, dglu, dga, dgb = _rw_vjp(_f_merge, merge_rows, [], [dmerged], row_grads=[bf16] * 4,
                                   param_grads=[], name="merge_bwd")

    dgl = _mm(dglu, w_glu, 'nt', out_dtype=bf16, name="glu_proj_dx")
    dw_glu = _mm(gl, dglu, 'tn', out_dtype=bf16, name="glu_proj_dw")
    dyb, du_skip, dd_row = _rw_vjp(_f_s5_post, [yb, (proj, lay.u, S5_W)], [d_row], [dgl],
                                   row_grads=[bf16, f32], param_grads=[True], name="s5_post_bwd")
    (du, dar, dai, dbd_c, dc_c), (terms['w_ffn2_in'],) = _s5_bwd(
        dyb, s5s, proj, lay.u, bd_c, c_c, ar_row, ai_row, du_skip, "s5_bwd",
        ride=scatter_of([('w_ffn2_in', ffn_in_back(dw2i))]))
    from_bd = lambda m_: _diag8(m_, S5_I, S5_P).transpose(0, 2, 1).reshape(S5_S, S5_I)
    dprep = _rw_vjp(_f_s5_prep, prep_rows, [], [dar.reshape(S5_S, 1), dai.reshape(S5_S, 1),
                                                from_bd(dbd_c[:, :S5_SB]), from_bd(dbd_c[:, S5_SB:])],
                    row_grads=[f32] * 5, param_grads=[], name="s5_prep_bwd", tm=512)
    dlr, dli, dldt, dbr, dbi = dprep
    g_s5 = dict(
        s5_lambda_re=dlr.reshape(S5_G, S5_P), s5_lambda_im=dli.reshape(S5_G, S5_P),
        s5_log_dt=dldt.reshape(S5_G, S5_P).sum(axis=1),
        s5_b_re=dbr.reshape(S5_G, S5_P, S5_I), s5_b_im=dbi.reshape(S5_G, S5_P, S5_I),
        s5_c_re=_diag8(dc_c[:S5_S], S5_P, S5_I).transpose(0, 2, 1),
        s5_c_im=-_diag8(dc_c[S5_S:], S5_P, S5_I).transpose(0, 2, 1),
        s5_d=dd_row.reshape(S5_G, S5_I))

    dy_a = _mm(dp_a, w_a, 'nt', out_dtype=bf16, name="a_proj_dx")
    dw_a = _mm(y_a, dp_a, 'tn', out_dtype=bf16, name="a_proj_dw")
    dy_ssd, dz, dssd_nw = _rw_vjp(_f_ssd_post, [y_ssd, (proj, lay.z, SSD_DI)], [ssd_norm_w], [dy_a],
                                  row_grads=[f32, bf16], param_grads=[True], name="ssd_post_bwd")
    early = [('w_out', rows_back(dw_o, d)), ('w_b_glu', _shard_cols(dw_glu)), ('w_a_proj', rows_back(dw_a, SSD_DI))]
    (dxs, dbm, dcm, ddt4, da4, ddsk4), landed = _ssd_bwd(xs, bm, cm, dt4, a4, dsk4, hs, dy_ssd, "ssd_bwd",
                                                         ride=scatter_of(early))
    terms.update({n: p for (n, _), p in zip(early, landed)})
    da_log, dd_ssd = head_vjp((da4, ddsk4))
    dpre, ddt_raw, dbias128 = _rw_vjp(_f_ssd_pre, [pre, (proj, lay.dt, LANE)], [bias128, spread, spread_t],
                                      [dxs, dbm, dcm, ddt4], row_grads=[f32, bf16],
                                      param_grads=[True, False, False], name="ssd_pre_bwd")
    dxbc, dconv_w8, dconv_b = _conv_bwd(dpre, proj, lay.xbc, conv_w8, "conv_bwd")

    dproj = [dz, dxbc, du, dga, dgb]
    dw_in = [_mm(h2, p, 'tn', out_dtype=bf16, name="in_proj_dw_%d" % i) for i, p in enumerate(dproj + [ddt_raw])]
    dh2, (terms['w_in'],) = _in_proj_dx(dproj, ddt_raw, lay.dt, w_inr, "in_proj_dx",
                                        ride=scatter_of([('w_in', lay.restore_slabs(dw_in))]))
    dx0, df1, dgt1, dnmx, dsh2, dsc2 = _rw_vjp(
        res1, [x2d, f1], [gt1, nmx, sh2, sc2], [dx1, dh2], row_grads=[f32, bf16], param_grads=[True] * 4,
        name="res1_mod2_bwd")
    dw1o = _mm(act1, df1, 'tn', out_dtype=bf16, name="ffn1_out_dw")
    dab1, (terms['w_ffn1_out'],) = _ffn_out_dx(df1, w1o, ab1, "ffn1_out_dx",
                                               ride=scatter_of([('w_ffn1_out', rows_back(dw1o, ff))]))
    dw1i = _ffn_in_dw(h1, dab1, "ffn1_in_dw")

    last = 'w_ffn1_in'
    sums = {n: _sum_lead(terms[n], "sum_" + n) for n in BIG if n != last}
    swap = _RideSwap([sums[n] for n in BIG if n != last])
    rides = _Rides([scatter_of([(last, ffn_in_back(dw1i))]), swap])
    dh1, landed = _ffn_in_dx(dab1, w1i, "ffn1_in_dx", ride=rides)
    (terms[last],), swapped = rides.split(landed)
    others = dict(zip([n for n in BIG if n != last], swapped))
    grad_x, dnf1, dsh1, dsc1 = _rw_vjp(_f_mod_keep, [x2d], [nf1, sh1, sc1], [dh1, dx0],
                                       row_grads=[f32], param_grads=[True] * 3, name="mod1_bwd")
    d_mods = jnp.concatenate([dsh1, dsc1, dgt1, dsh2, dsc2, dgt2, dsh3, dsc3, dgt3], axis=1).reshape(-1)
    sums[last] = _sum_lead(terms[last], "sum_" + last)
    (others[last],) = _run_ride(_RideSwap([sums[last]]), "swap_sum_" + last)

    out_g, out_d, out_m, out_v = {}, {}, {}, {}
    for n in BIG:
        r = _adamw(W[n][0], Mo[n][0], Vo[n][0], [sums[n], others[n]], "adamw_" + n)
        out_g[n], out_d[n], out_m[n], out_v[n] = [o[None] for o in r]

    local = dict(
        b_ada=d_mods, norm_ffn1=dnf1, norm_mix=dnmx, conv_w=dconv_w8[:CONV_K], conv_b=dconv_b,
        dt_bias=dbias128[:, :SSD_HEADS], a_log=da_log, d_ssd=dd_ssd, ssd_norm_w=dssd_nw,
        norm_ffn2=dnf2, norm_final=dnfin, **g_s5)
    flat = jnp.concatenate([local[n].reshape(-1) for n in SMALL])
    g3 = _gather8(flat, "gather_small_grads")
    n_small = flat.shape[0]
    npad = _round_up(n_small, SUBLANE * LANE)
    g3p = jnp.zeros((8, npad), f32).at[:, :n_small].set(g3).reshape(8, npad // LANE, LANE)
    gsum = _sum_lead(g3p, "sum_small").reshape(-1)

    def local_shard(n, a):
        if n == 'conv_w':
            return lax.dynamic_slice(a.reshape(CONV_K, CONV_DIM), (0, k_me * cw_cols), (CONV_K, cw_cols))
        return a

    pieces, off = {}, 0
    for n in SMALL:
        sz = local[n].size
        pieces[n] = local_shard(n, gsum[off:off + sz]).reshape(W[n].shape)
        off += sz

    def pack(dct):
        v_ = jnp.concatenate([dct[n].reshape(-1) for n in SMALL])
        pad = _round_up(v_.shape[0], SUBLANE * LANE) - v_.shape[0]
        return jnp.concatenate([v_, jnp.ones((pad,), f32)]).reshape(-1, LANE)

    rs = _adamw(pack(W), pack(Mo), pack(Vo), [pack(pieces)], "adamw_small")
    off = 0
    for n in SMALL:
        sz = W[n].size
        out_g[n], out_d[n], out_m[n], out_v[n] = [o.reshape(-1)[off:off + sz].reshape(W[n].shape) for o in rs]
        off += sz

    dm_loc = lax.dynamic_slice(g3[:, :N_ADA * d], (0, k_me * n_ada_loc), (SUBLANE, n_ada_loc))
    g_ada = _ada_bwd(c_all, dm_loc, "ada_bwd")
    r = _adamw(w_ada[0], m_w_ada[0], v_w_ada[0], [g_ada], "adamw_w_ada")
    out_g['w_ada'], out_d['w_ada'], out_m['w_ada'], out_v['w_ada'] = [o[None] for o in r]

    return (loss, grad_x[None], *[out_g[n] for n in WEIGHTS], *[out_d[n] for n in WEIGHTS],
            *[out_m[n] for n in WEIGHTS], *[out_v[n] for n in WEIGHTS])
```

```python
import functools
import math

import numpy as np
import jax
import jax.numpy as jnp
from jax import lax
from jax.experimental import pallas as pl
from jax.experimental.pallas import tpu as pltpu

f32 = jnp.float32
bf16 = jnp.bfloat16
HI = lax.Precision.HIGHEST
MESH = pl.DeviceIdType.MESH
AXES = ("x", "y", "c")

EPS = 1e-6
SSD_HEADS, SSD_P, SSD_N, SSD_G, SSD_R, SSD_L = 32, 64, 128, 4, 8, 128
SSD_DI = SSD_HEADS * SSD_P
CONV_K = 4
CONV_DIM = SSD_DI + 2 * SSD_G * SSD_N
S5_W, S5_G, S5_I, S5_P = 1024, 64, 16, 64
S5_S = S5_G * S5_P
N_ADA = 9
ADAM_LR, ADAM_B1, ADAM_B2, ADAM_EPS, ADAM_WD, ADAM_STEP = 0.001, 0.9, 0.999, 1e-08, 0.01, 10

LANE = 128
SUBLANE = 8
VMEM_LIMIT = 56 << 20
MM_VMEM_BUDGET = 40 << 20
RW_VMEM_BUDGET = 36 << 20

WEIGHTS = ['w_ada', 'b_ada', 'norm_ffn1', 'w_ffn1_in', 'w_ffn1_out', 'norm_mix', 'w_in', 'conv_w', 'conv_b', 'dt_bias',
           'a_log', 'd_ssd', 'ssd_norm_w', 'w_a_proj', 's5_lambda_re', 's5_lambda_im', 's5_b_re', 's5_b_im', 's5_c_re',
           's5_c_im', 's5_d', 's5_log_dt', 'w_b_glu', 'w_out', 'norm_ffn2', 'w_ffn2_in', 'w_ffn2_out', 'norm_final']
BIG = ['w_ffn1_in', 'w_ffn1_out', 'w_in', 'w_a_proj', 'w_b_glu', 'w_out', 'w_ffn2_in', 'w_ffn2_out']
COL_SHARDED = ('w_ffn1_in', 'w_in', 'w_b_glu', 'w_ffn2_in')
SMALL = [n for n in WEIGHTS if n not in BIG and n != 'w_ada']


def _cp(sem=None):
    return pltpu.CompilerParams(dimension_semantics=sem, vmem_limit_bytes=VMEM_LIMIT)


def _tile(dim, target, align=LANE):
    if dim <= target:
        return dim
    t = (target // align) * align
    while t >= align:
        if dim % t == 0:
            return t
        t -= align
    return dim


def _round_up(n, m):
    return (n + m - 1) // m * m


def _mm(a, b, mode, *, out_dtype, name, a_win=None, b_win=None, add=None, ride=None):
    a0, aw = a_win or (0, a.shape[1])
    b0, bw = b_win or (0, b.shape[1])
    if mode == 'nn':
        m, k, n = a.shape[0], aw, bw
        assert b.shape[0] == k
    elif mode == 'nt':
        m, k, n = a.shape[0], aw, b.shape[0]
        assert bw == k
    else:
        k, m, n = a.shape[0], aw, bw
        assert b.shape[0] == k
    osz = jnp.dtype(out_dtype).itemsize
    tm, tn, tk = 1024, 1152, 3456
    while True:
        bm = _tile(math.gcd(m, a0) if (mode == 'tn' and a0) else m, tm)
        bn = _tile(math.gcd(n, b0) if (mode != 'nt' and b0) else n, tn)
        kk = k
        if mode != 'tn' and a0:
            kk = math.gcd(kk, a0)
        if mode == 'nt' and b0:
            kk = math.gcd(kk, b0)
        bk = _tile(kk, tk)
        need = 2 * (bm * bk * a.dtype.itemsize + bk * bn * b.dtype.itemsize + bm * bn * osz) + bm * bn * 4
        if add is not None:
            need += 2 * bm * bn * add.dtype.itemsize
        if need <= MM_VMEM_BUDGET or (tm <= 256 and tn <= 256 and tk <= 512):
            break
        if tk > 1024:
            tk //= 2
        elif tm >= tn:
            tm //= 2
        else:
            tn //= 2
    nk = k // bk
    assert m % bm == 0 and n % bn == 0 and k % bk == 0, (name, m, n, k, bm, bn, bk)
    if mode == 'nn':
        ao, bo = a0 // bk, b0 // bn
        a_blk, a_map = (bm, bk), lambda i, j, q: (i, q + ao)
        b_blk, b_map = (bk, bn), lambda i, j, q: (q, j + bo)
    elif mode == 'nt':
        ao, bo = a0 // bk, b0 // bk
        a_blk, a_map = (bm, bk), lambda i, j, q: (i, q + ao)
        b_blk, b_map = (bn, bk), lambda i, j, q: (j, q + bo)
    else:
        ao, bo = a0 // bm, b0 // bn
        a_blk, a_map = (bk, bm), lambda i, j, q: (q, i + ao)
        b_blk, b_map = (bk, bn), lambda i, j, q: (q, j + bo)
    return _mm_core(a, b, mode, grid=(m // bm, n // bn, nk), a_blk=a_blk, a_map=a_map, b_blk=b_blk, b_map=b_map,
                    o_blk=(bm, bn), o_map=lambda i, j, q: (i, j), out_shape=(m, n), out_dtype=out_dtype, name=name,
                    add=add, ride=ride)


def _mm_core(a, b, mode, *, grid, a_blk, a_map, b_blk, b_map, o_blk, o_map, out_shape, out_dtype, name,
             add=None, ride=None):
    dims = {'nn': (((1,), (0,)), ((), ())), 'nt': (((1,), (1,)), ((), ())), 'tn': (((0,), (0,)), ((), ()))}[mode]
    nk = grid[-1]
    has_add = add is not None
    nr = ride.n if ride is not None else 0

    def body(*refs):
        a_ref, b_ref = refs[0], refs[1]
        pos = 2
        add_ref = refs[pos] if has_add else None
        pos += int(has_add)
        r_ins = refs[pos:pos + nr]
        o_ref = refs[pos + nr]
        r_outs = refs[pos + nr + 1:pos + 2 * nr + 1]
        acc_ref = refs[pos + 2 * nr + 1]
        r_sems = refs[pos + 2 * nr + 2:]
        ids = [pl.program_id(ax) for ax in range(len(grid))]
        q = ids[-1]
        if nr:
            @pl.when(functools.reduce(lambda u, v: u & v, [i == 0 for i in ids]))
            def _():
                ride.start(r_ins, r_outs, r_sems)

        @pl.when(q == 0)
        def _():
            acc_ref[...] = jnp.zeros_like(acc_ref)

        acc_ref[...] += lax.dot_general(a_ref[...].astype(bf16), b_ref[...].astype(bf16), dims,
                                        preferred_element_type=f32)

        @pl.when(q == nk - 1)
        def _():
            r = acc_ref[...]
            if has_add:
                r = r + add_ref[...].astype(f32)
            o_ref[...] = r.astype(out_dtype)

        if nr:
            @pl.when(functools.reduce(lambda u, v: u & v, [i == g - 1 for i, g in zip(ids, grid)]))
            def _():
                ride.wait(r_ins, r_outs, r_sems)

    in_specs = [pl.BlockSpec(a_blk, a_map), pl.BlockSpec(b_blk, b_map)]
    ops = [a, b]
    if has_add:
        in_specs.append(pl.BlockSpec(o_blk, o_map))
        ops.append(add)
    out_specs = [pl.BlockSpec(o_blk, o_map)]
    out_shapes = [jax.ShapeDtypeStruct(out_shape, out_dtype)]
    scratch = [pltpu.VMEM(tuple(b for b in o_blk if b is not None), f32)]
    if nr:
        in_specs += ride.specs
        ops += ride.srcs
        out_specs += ride.specs
        out_shapes += ride.out_shape
        scratch += ride.scratch
    sem = ("arbitrary",) * len(grid) if nr else ("parallel",) * (len(grid) - 1) + ("arbitrary",)
    res = pl.pallas_call(
        body, name=name, grid=grid, in_specs=in_specs, out_specs=out_specs, out_shape=out_shapes,
        scratch_shapes=scratch, compiler_params=_cp(sem),
    )(*ops)
    return (res[0], list(res[1:])) if nr else res[0]


def _swiglu(a, b):
    return jax.nn.silu(a) * b


def _ride_parts(refs, n_in, n_out, ride):
    nr = ride.n if ride is not None else 0
    ins = refs[:n_in]
    r_ins = refs[n_in:n_in + nr]
    outs = refs[n_in + nr:n_in + nr + n_out]
    r_outs = refs[n_in + nr + n_out:n_in + 2 * nr + n_out]
    return ins, r_ins, outs, r_outs, refs[n_in + 2 * nr + n_out:]


def _call_with_ride(body_core, grid, in_specs, ops, out_specs, out_shape, name, ride, scratch=(), sem=None):
    nr = ride.n if ride is not None else 0
    n_in, n_out, n_scr = len(in_specs), len(out_specs), len(scratch)

    def body(*refs):
        ins, r_ins, outs, r_outs, rest = _ride_parts(refs, n_in, n_out, ride)
        ids = [pl.program_id(ax) for ax in range(len(grid))]
        if nr:
            @pl.when(functools.reduce(lambda u, v: u & v, [i == 0 for i in ids]))
            def _():
                ride.start(r_ins, r_outs, rest[n_scr:])
        body_core(ins, outs, *rest[:n_scr])
        if nr:
            @pl.when(functools.reduce(lambda u, v: u & v, [i == g - 1 for i, g in zip(ids, grid)]))
            def _():
                ride.wait(r_ins, r_outs, rest[n_scr:])

    res = pl.pallas_call(
        body, name=name, grid=grid, in_specs=in_specs + (ride.specs if nr else []),
        out_specs=out_specs + (ride.specs if nr else []), out_shape=out_shape + (ride.out_shape if nr else []),
        scratch_shapes=list(scratch) + (ride.scratch if nr else []),
        compiler_params=_cp(("arbitrary",) * len(grid) if nr else (sem or ("parallel",) * len(grid))),
    )(*ops, *(ride.srcs if nr else []))
    return list(res[:n_out]), list(res[n_out:])


def _ffn_in(h, w, name, ride=None):
    t, d = h.shape
    cp = w.shape[2]
    bm, bn = _tile(t, 512), _tile(cp, 1408)
    nbs = cp // bn
    nn = (((1,), (0,)), ((), ()))

    def core(ins, outs):
        h_ref, wa_ref, wb_ref = ins
        ab_ref, act_ref = outs
        hv = h_ref[...].astype(bf16)
        a = lax.dot_general(hv, wa_ref[...].astype(bf16), nn, preferred_element_type=f32)
        b = lax.dot_general(hv, wb_ref[...].astype(bf16), nn, preferred_element_type=f32)
        ab_ref[0] = a.astype(bf16)
        ab_ref[1] = b.astype(bf16)
        act_ref[...] = _swiglu(a, b).astype(bf16)

    (ab, act), landed = _call_with_ride(
        core, (t // bm, 2 * nbs),
        [pl.BlockSpec((bm, d), lambda i, j: (i, 0)),
         pl.BlockSpec((None, d, bn), lambda i, j: (j // nbs, 0, j % nbs)),
         pl.BlockSpec((None, d, bn), lambda i, j: (2 + j // nbs, 0, j % nbs))], [h, w, w],
        [pl.BlockSpec((2, bm, bn), lambda i, j: (0, i, j)), pl.BlockSpec((bm, bn), lambda i, j: (i, j))],
        [jax.ShapeDtypeStruct((2, t, 2 * cp), bf16), jax.ShapeDtypeStruct((t, 2 * cp), bf16)], name, ride)
    return ab, act, landed


def _ffn_out_dx(df, wo, ab, name, ride=None):
    t, d = df.shape
    ffp = wo.shape[0]
    bm, bn = _tile(t, 1024), _tile(ffp, 512)
    nt = (((1,), (1,)), ((), ()))

    def core(ins, outs):
        df_ref, wo_ref, ab_ref = ins
        dact = lax.dot_general(df_ref[...].astype(bf16), wo_ref[...].astype(bf16), nt, preferred_element_type=f32)
        _, vjp = jax.vjp(_swiglu, ab_ref[0].astype(f32), ab_ref[1].astype(f32))
        da, db = vjp(dact)
        outs[0][0] = da.astype(bf16)
        outs[0][1] = db.astype(bf16)

    (dab,), landed = _call_with_ride(
        core, (t // bm, ffp // bn),
        [pl.BlockSpec((bm, d), lambda i, j: (i, 0)), pl.BlockSpec((bn, d), lambda i, j: (j, 0)),
         pl.BlockSpec((2, bm, bn), lambda i, j: (0, i, j))], [df, wo, ab],
        [pl.BlockSpec((2, bm, bn), lambda i, j: (0, i, j))], [jax.ShapeDtypeStruct((2, t, ffp), bf16)], name, ride)
    return dab, landed


def _ffn_in_dx(dab, w, name, ride=None):
    _, t, cp2 = dab.shape
    d, cp = w.shape[1], w.shape[2]
    bm, bn, bk = _tile(t, 1024), _tile(d, 1024), _tile(cp, 2816)
    nkb = cp // bk
    return _mm_core(dab, w, 'nt', grid=(t // bm, d // bn, 4 * nkb),
                    a_blk=(None, bm, bk), a_map=lambda i, j, q: (q // (2 * nkb), i, q % (2 * nkb)),
                    b_blk=(None, bn, bk), b_map=lambda i, j, q: (q // nkb, j, q % nkb),
                    o_blk=(bm, bn), o_map=lambda i, j, q: (i, j),
                    out_shape=(t, d), out_dtype=bf16, name=name, ride=ride)


def _ffn_in_dw(h, dab, name):
    _, t, cp2 = dab.shape
    cp = cp2 // 2
    d = h.shape[1]
    bm, bn, bk = _tile(d, 1024), _tile(cp, 1408), _tile(t, 2048)
    nbs = cp // bn
    return _mm_core(h, dab, 'tn', grid=(d // bm, 4 * nbs, t // bk),
                    a_blk=(bk, bm), a_map=lambda i, j, q: (q, i),
                    b_blk=(None, bk, bn), b_map=lambda i, j, q: (j // (2 * nbs), q, j % (2 * nbs)),
                    o_blk=(None, bm, bn), o_map=lambda i, j, q: (j // nbs, i, j % nbs),
                    out_shape=(4, d, cp), out_dtype=bf16, name=name)


def _piece_blocks(pieces):
    bk = min(1024, functools.reduce(math.gcd, [p.shape[1] for p in pieces]))
    starts, n = [], 0
    for p in pieces:
        starts.append(n)
        n += p.shape[1] // bk
    return bk, starts, n


def _in_proj_dx(pieces, tail, tail_col, w, name, ride=None):
    t, d = pieces[0].shape[0], w.shape[0]
    bk, starts, nq = _piece_blocks(pieces)
    assert tail_col == nq * bk and tail.shape[1] == LANE
    bm, bn = _tile(t, 1024), _tile(d, 1024)
    npc = len(pieces)

    def core(ins, outs, acc_ref):
        a_refs, tail_ref, b_ref, bt_ref = ins[:npc], ins[npc], ins[npc + 1], ins[npc + 2]
        q = pl.program_id(2)

        @pl.when(q == 0)
        def _():
            acc_ref[...] = jnp.zeros_like(acc_ref)

        for a_ref, s0, p in zip(a_refs, starts, pieces):
            @pl.when((q >= s0) & (q < s0 + p.shape[1] // bk))
            def _(a_ref=a_ref):
                acc_ref[...] += lax.dot_general(a_ref[...].astype(bf16), b_ref[...], _NT, preferred_element_type=f32)

        @pl.when(q == nq)
        def _():
            r = acc_ref[...] + lax.dot_general(tail_ref[...].astype(bf16), bt_ref[...], _NT,
                                               preferred_element_type=f32)
            outs[0][...] = r.astype(bf16)

    def a_spec(s0, p):
        last = p.shape[1] // bk - 1
        return pl.BlockSpec((bm, bk), lambda i, j, q: (i, jnp.clip(q - s0, 0, last)))

    in_specs = [a_spec(s0, p) for s0, p in zip(starts, pieces)] + [
        pl.BlockSpec((bm, LANE), lambda i, j, q: (i, 0)),
        pl.BlockSpec((bn, bk), lambda i, j, q: (j, jnp.minimum(q, nq - 1))),
        pl.BlockSpec((bn, LANE), lambda i, j, q: (j, tail_col // LANE))]
    (dh,), landed = _call_with_ride(
        core, (t // bm, d // bn, nq + 1), in_specs, list(pieces) + [tail, w, w],
        [pl.BlockSpec((bm, bn), lambda i, j, q: (i, j))], [jax.ShapeDtypeStruct((t, d), bf16)], name, ride,
        scratch=[pltpu.VMEM((bm, bn), f32)], sem=("parallel", "parallel", "arbitrary"))
    return dh, landed


def _win(r):
    return r if isinstance(r, tuple) else (r, 0, r.shape[1])


def _row_tile(t, widths):
    per_row = 48 * max(widths)
    tm = 512
    while tm > SUBLANE and tm * per_row > RW_VMEM_BUDGET:
        tm //= 2
    return min(tm, t)


def _row_spec(r, tm):
    arr, c0, w = _win(r)
    assert c0 % w == 0, (c0, w)
    cb = c0 // w
    return pl.BlockSpec((tm, w), lambda i: (i, cb))


def _full_spec(p):
    nd = p.ndim
    return pl.BlockSpec(p.shape, lambda i: (0,) * nd)


def _rw(f, rows, params, outs, *, name, accs=(), tm=None):
    t = _win(rows[0])[0].shape[0]
    tm = tm or _row_tile(t, [_win(r)[2] for r in rows] + [w for w, _ in outs])
    nr, npar, no, na = len(rows), len(params), len(outs), len(accs)

    def body(*refs):
        vals = [r[...] for r in refs[:nr + npar]]
        res = f(*vals)
        res = res if isinstance(res, (tuple, list)) else (res,)
        for o_ref, v in zip(refs[nr + npar:nr + npar + no], res[:no]):
            o_ref[...] = v.astype(o_ref.dtype)
        if na:
            first = pl.program_id(0) == 0
            for a_ref, v in zip(refs[nr + npar + no:], res[no:]):
                @pl.when(first)
                def _(a_ref=a_ref):
                    a_ref[...] = jnp.zeros_like(a_ref)
                a_ref[...] += v

    out_shape = [jax.ShapeDtypeStruct((t, w), d) for w, d in outs] + [jax.ShapeDtypeStruct(s, f32) for s in accs]
    out_specs = [pl.BlockSpec((tm, w), lambda i: (i, 0)) for w, _ in outs] + \
                [pl.BlockSpec(s, lambda i: (0, 0)) for s in accs]
    return pl.pallas_call(
        body, name=name, grid=(t // tm,),
        in_specs=[_row_spec(r, tm) for r in rows] + [_full_spec(p) for p in params],
        out_specs=out_specs, out_shape=out_shape,
        compiler_params=_cp(("arbitrary",)),
    )(*[_win(r)[0] for r in rows], *params)


def _rw_vjp(f, rows, params, cots, *, row_grads, param_grads, name, tm=None, total_of=None):
    t = _win(rows[0])[0].shape[0]
    cot_rows = [c for c in cots if c is not None]
    tm = tm or _row_tile(t, [_win(r)[2] for r in rows] + [_win(c)[2] for c in cot_rows])
    nr, npar, ncot = len(rows), len(params), len(cot_rows)
    d_rows = [i for i, d in enumerate(row_grads) if d is not None]
    d_pars = [i for i, d in enumerate(param_grads) if d]

    def body(*refs):
        rv = [r[...] for r in refs[:nr]]
        pv = [r[...] for r in refs[nr:nr + npar]]
        cv = [r[...] for r in refs[nr + npar:nr + npar + ncot]]
        outs_r = refs[nr + npar + ncot:nr + npar + ncot + len(d_rows)]
        outs_p = refs[nr + npar + ncot + len(d_rows):nr + npar + ncot + len(d_rows) + len(d_pars)]

        def g(*diff):
            rr, pp = list(rv), list(pv)
            for i, v in zip(d_rows, diff[:len(d_rows)]):
                rr[i] = v
            for i, v in zip(d_pars, diff[len(d_rows):]):
                pp[i] = v
            res = f(*rr, *pp)
            return tuple(res) if isinstance(res, (tuple, list)) else (res,)

        prim, vjp = jax.vjp(g, *[rv[i] for i in d_rows], *[pv[i] for i in d_pars])
        it = iter(cv)
        cts = tuple(next(it).astype(o.dtype) if c is not None else jnp.zeros_like(o) for o, c in zip(prim, cots))
        grads = vjp(cts)
        for o_ref, v in zip(outs_r, grads[:len(d_rows)]):
            o_ref[...] = v.astype(o_ref.dtype)
        first = pl.program_id(0) == 0
        for o_ref, v in zip(outs_p, grads[len(d_rows):]):
            @pl.when(first)
            def _(o_ref=o_ref):
                o_ref[...] = jnp.zeros_like(o_ref)
            o_ref[...] += v.astype(f32)
        if total_of is not None:
            tot_ref = refs[-1]

            @pl.when(first)
            def _():
                tot_ref[...] = jnp.zeros_like(tot_ref)
            tot_ref[...] += jnp.broadcast_to(jnp.sum(prim[total_of].astype(f32)), tot_ref.shape)

    out_shape = [jax.ShapeDtypeStruct((t, _win(rows[i])[2]), row_grads[i]) for i in d_rows] + \
                [jax.ShapeDtypeStruct(params[i].shape, f32) for i in d_pars]
    out_specs = [pl.BlockSpec((tm, _win(rows[i])[2]), lambda i_: (i_, 0)) for i in d_rows] + \
                [_full_spec(params[i]) for i in d_pars]
    if total_of is not None:
        out_shape.append(jax.ShapeDtypeStruct((1, LANE), f32))
        out_specs.append(pl.BlockSpec((1, LANE), lambda i_: (0, 0)))
    return pl.pallas_call(
        body, name=name, grid=(t // tm,),
        in_specs=[_row_spec(r, tm) for r in rows] + [_full_spec(p) for p in params] + [_row_spec(c, tm) for c in cot_rows],
        out_specs=out_specs, out_shape=out_shape,
        compiler_params=_cp(("arbitrary",)),
    )(*[_win(r)[0] for r in rows], *params, *[_win(c)[0] for c in cot_rows])


def _rms(x, g):
    return x * lax.rsqrt(jnp.mean(x * x, axis=-1, keepdims=True) + EPS) * g


def _f_mod(x, nw, sh, sc):
    return (_rms(x, nw) * (1.0 + sc) + sh).astype(bf16)


def _f_mod_keep(x, nw, sh, sc):
    return _f_mod(x, nw, sh, sc), x


def _f_res_mod(coef, x, o, g, nw, sh, sc):
    x1 = x + coef * g * o.astype(f32)
    return x1, _f_mod(x1, nw, sh, sc)


def _times01(x, e):
    hi = x.astype(bf16)
    r1 = x - hi.astype(f32)
    mid = r1.astype(bf16)
    lo = (r1 - mid.astype(f32)).astype(bf16)
    return (jnp.dot(hi, e, preferred_element_type=f32) + jnp.dot(mid, e, preferred_element_type=f32) +
            jnp.dot(lo, e, preferred_element_type=f32))


@jax.custom_vjp
def _spread_heads(x, e, et):
    return _times01(x, e)


_spread_heads.defvjp(lambda x, e, et: (_times01(x, e), (e, et)),
                     lambda res, g: (_times01(g, res[1]), None, None))


def _f_ssd_pre(pre, dtraw, bias, e, et):
    xc = jax.nn.silu(pre)
    dtx = _spread_heads(jax.nn.softplus(dtraw + bias), e, et)
    return xc[:, :SSD_DI], xc[:, SSD_DI:SSD_DI + SSD_G * SSD_N], xc[:, SSD_DI + SSD_G * SSD_N:], dtx


def _f_ssd_post(y, z, nw):
    yz = y * jax.nn.silu(z)
    w = SSD_DI // SSD_G
    parts = []
    for g in range(SSD_G):
        s = yz[:, g * w:(g + 1) * w]
        parts.append(s * lax.rsqrt(jnp.mean(s * s, axis=-1, keepdims=True) + EPS))
    return (jnp.concatenate(parts, axis=1) * nw).astype(bf16)


def _f_s5_post(yb, u, d):
    return jax.nn.gelu(yb + d * u).astype(bf16)


def _f_merge(pa, glu, ga, gb):
    d = pa.shape[1]
    glu = glu.astype(f32)
    pb = glu[:, :d] * jax.nn.sigmoid(glu[:, d:])
    return (jax.nn.sigmoid(ga) * pa.astype(f32) + jax.nn.sigmoid(gb) * pb).astype(bf16)


def _f_final(x2, o, tgt, g, nw):
    x3 = x2 + 0.5 * g * o.astype(f32)
    y = _rms(x3, nw)
    return 0.5 * jnp.mean(jnp.square(y - tgt), axis=-1, keepdims=True)


def _f_s5_prep(lr, li, ldt, br, bi):
    dt = jnp.exp(ldt)
    lr = jnp.minimum(lr, -1e-4)
    mag = jnp.exp(lr * dt)
    ar = mag * jnp.cos(li * dt)
    ai = mag * jnp.sin(li * dt)
    den = lr * lr + li * li
    nr = ar - 1.0
    kr = (nr * lr + ai * li) / den
    ki = (ai * lr - nr * li) / den
    return ar, ai, kr * br - ki * bi, kr * bi + ki * br


def _shift_down(cur, halo8, j):
    if j == 0:
        return cur
    rolled = pltpu.roll(cur, j, 0)
    row8 = lax.broadcasted_iota(jnp.int32, halo8.shape, 0)
    top = jnp.where(row8 < j, pltpu.roll(halo8, j, 0), rolled[:SUBLANE])
    return jnp.concatenate([top, rolled[SUBLANE:]], axis=0)


def _shift_up(cur, halo8, j):
    if j == 0:
        return cur
    n = cur.shape[0]
    rolled = pltpu.roll(cur, n - j, 0)
    row8 = lax.broadcasted_iota(jnp.int32, halo8.shape, 0)
    bot = jnp.where(row8 >= SUBLANE - j, pltpu.roll(halo8, SUBLANE - j, 0), rolled[n - SUBLANE:])
    return jnp.concatenate([rolled[:n - SUBLANE], bot], axis=0)


def _conv_fwd(proj, c0, w8, b, name):
    t = proj.shape[0]
    cw = 1024
    tm = min(512, t)
    cb0 = c0 // cw
    r8 = tm // SUBLANE

    def body(x_ref, h_ref, w_ref, b_ref, o_ref):
        i = pl.program_id(1)
        x = x_ref[...]
        halo = jnp.where(i > 0, h_ref[...], 0.0)
        acc = b_ref[...] + w_ref[CONV_K - 1:CONV_K, :] * x
        for j in range(1, CONV_K):
            acc = acc + w_ref[CONV_K - 1 - j:CONV_K - j, :] * _shift_down(x, halo, j)
        o_ref[...] = acc

    return pl.pallas_call(
        body, name=name, grid=(CONV_DIM // cw, t // tm),
        in_specs=[pl.BlockSpec((tm, cw), lambda c, i: (i, cb0 + c)),
                  pl.BlockSpec((SUBLANE, cw), lambda c, i: (jnp.maximum(i * r8 - 1, 0), cb0 + c)),
                  pl.BlockSpec((SUBLANE, cw), lambda c, i: (0, c)),
                  pl.BlockSpec((1, cw), lambda c, i: (0, c))],
        out_specs=pl.BlockSpec((tm, cw), lambda c, i: (i, c)),
        out_shape=jax.ShapeDtypeStruct((t, CONV_DIM), f32),
        compiler_params=_cp(("parallel", "arbitrary")),
    )(proj, proj, w8, b)


def _conv_bwd(dpre, proj, c0, w8, name):
    t = proj.shape[0]
    cw = 1024
    tm = min(512, t)
    cb0 = c0 // cw
    r8 = tm // SUBLANE
    nb = t // tm

    def body(d_ref, dn_ref, x_ref, xh_ref, w_ref, dx_ref, dw_ref, db_ref):
        i = pl.program_id(1)
        d = d_ref[...]
        dn = jnp.where(i < nb - 1, dn_ref[...], 0.0)
        x = x_ref[...]
        xh = jnp.where(i > 0, xh_ref[...], 0.0)

        @pl.when(i == 0)
        def _():
            dw_ref[...] = jnp.zeros_like(dw_ref)
            db_ref[...] = jnp.zeros_like(db_ref)

        dx = w_ref[CONV_K - 1:CONV_K, :] * d
        rows = [jnp.sum(d * x, axis=0, keepdims=True)]
        for j in range(1, CONV_K):
            dx = dx + w_ref[CONV_K - 1 - j:CONV_K - j, :] * _shift_up(d, dn, j)
            rows.append(jnp.sum(d * _shift_down(x, xh, j), axis=0, keepdims=True))
        dx_ref[...] = dx.astype(dx_ref.dtype)
        dw = jnp.concatenate([rows[CONV_K - 1 - k] for k in range(CONV_K)] +
                             [jnp.zeros((SUBLANE - CONV_K, cw), f32)], axis=0)
        dw_ref[...] += dw
        db_ref[...] += jnp.sum(d, axis=0, keepdims=True)

    return pl.pallas_call(
        body, name=name, grid=(CONV_DIM // cw, nb),
        in_specs=[pl.BlockSpec((tm, cw), lambda c, i: (i, c)),
                  pl.BlockSpec((SUBLANE, cw), lambda c, i: (jnp.minimum((i + 1) * r8, nb * r8 - 1), c)),
                  pl.BlockSpec((tm, cw), lambda c, i: (i, cb0 + c)),
                  pl.BlockSpec((SUBLANE, cw), lambda c, i: (jnp.maximum(i * r8 - 1, 0), cb0 + c)),
                  pl.BlockSpec((SUBLANE, cw), lambda c, i: (0, c))],
        out_specs=[pl.BlockSpec((tm, cw), lambda c, i: (i, c)),
                   pl.BlockSpec((SUBLANE, cw), lambda c, i: (0, c)),
                   pl.BlockSpec((1, cw), lambda c, i: (0, c))],
        out_shape=[jax.ShapeDtypeStruct((t, CONV_DIM), bf16), jax.ShapeDtypeStruct((SUBLANE, CONV_DIM), f32),
                   jax.ShapeDtypeStruct((1, CONV_DIM), f32)],
        compiler_params=_cp(("parallel", "arbitrary")),
    )(dpre, dpre, proj, proj, w8)


def _cumsum_rows_impl(x):
    n = x.shape[0]
    row = lax.broadcasted_iota(jnp.int32, x.shape, 0)
    s = 1
    while s < n:
        x = x + jnp.where(row >= s, pltpu.roll(x, s, 0), 0.0)
        s *= 2
    return x


@jax.custom_vjp
def _cumsum_rows(x):
    return _cumsum_rows_impl(x)


def _cumsum_rows_bwd(_, g):
    c = _cumsum_rows_impl(g)
    return (c[c.shape[0] - 1:, :] - c + g,)


_cumsum_rows.defvjp(lambda x: (_cumsum_rows_impl(x), None), _cumsum_rows_bwd)


@jax.custom_vjp
def _swap_halves(t):
    return pltpu.roll(t, LANE // 2, 1)


_swap_halves.defvjp(lambda t: (pltpu.roll(t, LANE // 2, 1), None), lambda _, g: (pltpu.roll(g, LANE // 2, 1),))


def _ssd_chunk(xs, bm, cm, dtx, ax, dskx, ht):
    n = SSD_L
    assert n == LANE and SSD_P * 2 == LANE
    row = lax.broadcasted_iota(jnp.int32, (n, n), 0)
    col = lax.broadcasted_iota(jnp.int32, (n, n), 1)
    causal = row >= col
    lo = col < SSD_P
    cs = _cumsum_rows(dtx * ax)
    xdt = xs * dtx
    last = cs[n - 1:n, :]
    cb = lax.dot_general(cm.astype(bf16), bm.astype(bf16), (((1,), (1,)), ((), ())), preferred_element_type=f32)
    y_off = jnp.dot(cm.astype(bf16), ht.astype(bf16), preferred_element_type=f32) * jnp.exp(cs)
    st = lax.dot_general(bm.astype(bf16), (xdt * jnp.exp(last - cs)).astype(bf16), (((0,), (0,)), ((), ())),
                         preferred_element_type=f32)
    ht_new = jnp.exp(last) * ht + st
    ys = []
    for q in range(SSD_R // 2):
        tq = cs[:, q * LANE:(q + 1) * LANE]
        sw = _swap_halves(tq)
        tqt = tq.T
        xq = xdt[:, q * LANE:(q + 1) * LANE].astype(bf16)
        pair = []
        for c_col, r_row in ((jnp.where(lo, tq, sw), tqt[0:1, :]), (jnp.where(lo, sw, tq), tqt[SSD_P:SSD_P + 1, :])):
            decay = jnp.exp(jnp.where(causal, c_col - r_row, -1e30))
            pair.append(jnp.dot((cb * decay).astype(bf16), xq, preferred_element_type=f32))
        ys.append(jnp.where(lo, pair[0], pair[1]))
    return jnp.concatenate(ys, axis=1) + y_off + dskx * xs, ht_new


SSD_GB = 1


def _ssd_specs(nc, rev):
    ch = (lambda c: nc - 1 - c) if rev else (lambda c: c)
    gw = SSD_GB * SSD_R * SSD_P
    return [pl.BlockSpec((SSD_L, gw), lambda g, c: (ch(c), g)),
            pl.BlockSpec((SSD_L, SSD_GB * SSD_N), lambda g, c: (ch(c), g)),
            pl.BlockSpec((SSD_L, SSD_GB * SSD_N), lambda g, c: (ch(c), g)),
            pl.BlockSpec((SSD_L, gw), lambda g, c: (ch(c), g)),
            pl.BlockSpec((1, gw), lambda g, c: (0, g)),
            pl.BlockSpec((1, gw), lambda g, c: (0, g))]


def _ssd_group(refs, q):
    gw = SSD_R * SSD_P
    xs_ref, bm_ref, cm_ref, dt_ref, a_ref, dsk_ref = refs
    ln = slice(q * LANE, (q + 1) * LANE)
    wd = slice(q * gw, (q + 1) * gw)
    return (xs_ref[:, wd], bm_ref[:, ln], cm_ref[:, ln], dt_ref[:, wd], a_ref[:, wd], dsk_ref[:, wd])


def _ssd_fwd(xs, bm, cm, dt4, a4, dsk4, name, ride=None):
    t = xs.shape[0]
    nc = t // SSD_L
    gw = SSD_R * SSD_P

    nr = ride.n if ride is not None else 0
    ng = SSD_G // SSD_GB

    def body(*refs):
        xs_ref, bm_ref, cm_ref, dt_ref, a_ref, dsk_ref = refs[:6]
        r_ins = refs[6:6 + nr]
        y_ref, hs_ref = refs[6 + nr:8 + nr]
        r_outs = refs[8 + nr:8 + 2 * nr]
        h_ref = refs[8 + 2 * nr]
        r_sems = refs[9 + 2 * nr:]
        g, c = pl.program_id(0), pl.program_id(1)
        if nr:
            @pl.when((g == 0) & (c == 0))
            def _():
                ride.start(r_ins, r_outs, r_sems)

        @pl.when(c == 0)
        def _():
            h_ref[...] = jnp.zeros_like(h_ref)

        hs_ref[...] = h_ref[...]
        grp = (xs_ref, bm_ref, cm_ref, dt_ref, a_ref, dsk_ref)
        ops = [_ssd_group(grp, q) + (h_ref[:, q * gw:(q + 1) * gw],) for q in range(SSD_GB)]
        res = [_ssd_chunk(*o) for o in ops]
        for q, (y, hn) in enumerate(res):
            y_ref[:, q * gw:(q + 1) * gw] = y
            h_ref[:, q * gw:(q + 1) * gw] = hn

        if nr:
            @pl.when((g == ng - 1) & (c == nc - 1))
            def _():
                ride.wait(r_ins, r_outs, r_sems)

    res = pl.pallas_call(
        body, name=name, grid=(ng, nc), in_specs=_ssd_specs(nc, False) + (ride.specs if nr else []),
        out_specs=[pl.BlockSpec((SSD_L, SSD_GB * gw), lambda g, c: (c, g)),
                   pl.BlockSpec((None, None, SSD_N, SSD_GB * gw), lambda g, c: (g, c, 0, 0))] +
                  (ride.specs if nr else []),
        out_shape=[jax.ShapeDtypeStruct((t, SSD_DI), f32),
                   jax.ShapeDtypeStruct((ng, nc, SSD_N, SSD_GB * gw), f32)] + (ride.out_shape if nr else []),
        scratch_shapes=[pltpu.VMEM((SSD_N, SSD_GB * gw), f32)] + (ride.scratch if nr else []),
        compiler_params=_cp(("arbitrary", "arbitrary")),
    )(xs, bm, cm, dt4, a4, dsk4, *(ride.srcs if nr else []))
    return res[0], res[1], list(res[2:])


def _ssd_bwd(xs, bm, cm, dt4, a4, dsk4, hs, dy, name, ride=None):
    t = xs.shape[0]
    nc = t // SSD_L
    gw = SSD_R * SSD_P
    rc = lambda c: nc - 1 - c
    nr = ride.n if ride is not None else 0
    ng = SSD_G // SSD_GB

    def body(*refs):
        xs_ref, bm_ref, cm_ref, dt_ref, a_ref, dsk_ref, hs_ref, dy_ref = refs[:8]
        r_ins = refs[8:8 + nr]
        dxs_ref, dbm_ref, dcm_ref, ddt_ref, da_ref, ddsk_ref = refs[8 + nr:14 + nr]
        r_outs = refs[14 + nr:14 + 2 * nr]
        dh_ref = refs[14 + 2 * nr]
        r_sems = refs[15 + 2 * nr:]
        if nr:
            @pl.when((pl.program_id(0) == 0) & (pl.program_id(1) == 0))
            def _():
                ride.start(r_ins, r_outs, r_sems)

        @pl.when(pl.program_id(1) == 0)
        def _():
            dh_ref[...] = jnp.zeros_like(dh_ref)
            da_ref[...] = jnp.zeros_like(da_ref)
            ddsk_ref[...] = jnp.zeros_like(ddsk_ref)

        grp = (xs_ref, bm_ref, cm_ref, dt_ref, a_ref, dsk_ref)
        ops = [_ssd_group(grp, q) + (hs_ref[:, q * gw:(q + 1) * gw],) for q in range(SSD_GB)]
        cts = [(dy_ref[:, q * gw:(q + 1) * gw], dh_ref[:, q * gw:(q + 1) * gw]) for q in range(SSD_GB)]
        grads = [jax.vjp(_ssd_chunk, *o)[1](ct) for o, ct in zip(ops, cts)]
        for q, (dxs, dbm, dcm, ddt, da, ddsk, dh) in enumerate(grads):
            wd = slice(q * gw, (q + 1) * gw)
            ln = slice(q * LANE, (q + 1) * LANE)
            dxs_ref[:, wd] = dxs
            dbm_ref[:, ln] = dbm
            dcm_ref[:, ln] = dcm
            ddt_ref[:, wd] = ddt
            da_ref[:, wd] += da
            ddsk_ref[:, wd] += ddsk
            dh_ref[:, wd] = dh

        if nr:
            @pl.when((pl.program_id(0) == ng - 1) & (pl.program_id(1) == nc - 1))
            def _():
                ride.wait(r_ins, r_outs, r_sems)

    res = pl.pallas_call(
        body, name=name, grid=(ng, nc),
        in_specs=_ssd_specs(nc, True) + [
            pl.BlockSpec((None, None, SSD_N, SSD_GB * gw), lambda g, c: (g, rc(c), 0, 0)),
            pl.BlockSpec((SSD_L, SSD_GB * gw), lambda g, c: (rc(c), g))] + (ride.specs if nr else []),
        out_specs=[pl.BlockSpec((SSD_L, SSD_GB * gw), lambda g, c: (rc(c), g)),
                   pl.BlockSpec((SSD_L, SSD_GB * SSD_N), lambda g, c: (rc(c), g)),
                   pl.BlockSpec((SSD_L, SSD_GB * SSD_N), lambda g, c: (rc(c), g)),
                   pl.BlockSpec((SSD_L, SSD_GB * gw), lambda g, c: (rc(c), g)),
                   pl.BlockSpec((1, SSD_GB * gw), lambda g, c: (0, g)),
                   pl.BlockSpec((1, SSD_GB * gw), lambda g, c: (0, g))] + (ride.specs if nr else []),
        out_shape=[jax.ShapeDtypeStruct((t, SSD_DI), f32), jax.ShapeDtypeStruct((t, SSD_G * SSD_N), f32),
                   jax.ShapeDtypeStruct((t, SSD_G * SSD_N), f32), jax.ShapeDtypeStruct((t, SSD_DI), f32),
                   jax.ShapeDtypeStruct((1, SSD_DI), f32), jax.ShapeDtypeStruct((1, SSD_DI), f32)] +
                  (ride.out_shape if nr else []),
        scratch_shapes=[pltpu.VMEM((SSD_N, SSD_GB * gw), f32)] + (ride.scratch if nr else []),
        compiler_params=_cp(("arbitrary", "arbitrary")),
    )(xs, bm, cm, dt4, a4, dsk4, hs, dy, *(ride.srcs if nr else []))
    return list(res[:6]), list(res[6:])


S5_CH = 1024


S5_NB = 8
S5_UB = 128
S5_SB = 512
_NT = (((1,), (1,)), ((), ()))
_TN = (((0,), (0,)), ((), ()))


def _s5_fwd(proj, u0, bd_c, c_c, ar, ai, name):
    t = proj.shape[0]
    tb = min(128, t)
    ub = u0 // S5_W

    def body(u_ref, bd_ref, cc_ref, ar_ref, ai_ref, s_ref, yb_ref, bu_ref, carry):
        @pl.when(pl.program_id(0) == 0)
        def _():
            carry[...] = jnp.zeros_like(carry)

        u = u_ref[...].astype(bf16)
        for j in range(S5_NB):
            uj = u[:, j * S5_UB:(j + 1) * S5_UB]
            for half in range(2):
                bu_ref[:, half * S5_S + j * S5_SB:half * S5_S + (j + 1) * S5_SB] = jnp.dot(
                    uj, bd_ref[j * S5_UB:(j + 1) * S5_UB, half * S5_SB:(half + 1) * S5_SB], preferred_element_type=f32)

        for c0 in range(0, S5_S, S5_CH):
            re = pl.ds(c0, S5_CH)
            im = pl.ds(S5_S + c0, S5_CH)
            a_r = ar_ref[:, re]
            a_i = ai_ref[:, re]

            def step(k, st, re=re, im=im, a_r=a_r, a_i=a_i):
                sr, si = st
                row = pl.ds(k, 1)
                nr = a_r * sr - a_i * si + bu_ref[row, re]
                ni = a_r * si + a_i * sr + bu_ref[row, im]
                s_ref[row, re] = nr
                s_ref[row, im] = ni
                return nr, ni

            sr, si = lax.fori_loop(0, tb, step, (carry[:, re], carry[:, im]))
            carry[:, re] = sr
            carry[:, im] = si

        for j in range(S5_NB):
            lo, hi = j * S5_SB, (j + 1) * S5_SB
            yb_ref[:, j * S5_UB:(j + 1) * S5_UB] = (
                jnp.dot(s_ref[:, lo:hi].astype(bf16), cc_ref[lo:hi, :], preferred_element_type=f32) +
                jnp.dot(s_ref[:, S5_S + lo:S5_S + hi].astype(bf16), cc_ref[S5_S + lo:S5_S + hi, :],
                        preferred_element_type=f32))

    return pl.pallas_call(
        body, name=name, grid=(t // tb,),
        in_specs=[pl.BlockSpec((tb, S5_W), lambda i: (i, ub)), _full_spec(bd_c), _full_spec(c_c),
                  pl.BlockSpec((1, S5_S), lambda i: (0, 0)), pl.BlockSpec((1, S5_S), lambda i: (0, 0))],
        out_specs=[pl.BlockSpec((tb, 2 * S5_S), lambda i: (i, 0)), pl.BlockSpec((tb, S5_W), lambda i: (i, 0))],
        out_shape=[jax.ShapeDtypeStruct((t, 2 * S5_S), f32), jax.ShapeDtypeStruct((t, S5_W), f32)],
        scratch_shapes=[pltpu.VMEM((tb, 2 * S5_S), f32), pltpu.VMEM((1, 2 * S5_S), f32)],
        compiler_params=_cp(("arbitrary",)),
    )(proj, bd_c, c_c, ar, ai)


def _s5_bwd(dyb, s, proj, u0, bd_c, c_c, ar, ai, du_skip, name, ride=None):
    t = dyb.shape[0]
    tb = min(128, t)
    nb = t // tb
    r8 = tb // SUBLANE
    rb = lambda i: nb - 1 - i
    ub = u0 // S5_W

    def body(ins, outs, g_ref, carry):
        dyb_ref, s_ref, sh_ref, u_ref, skip_ref, bd_ref, cc_ref, ar_ref, ai_ref = ins
        du_ref, dar_ref, dai_ref, dbd_ref, dcc_ref = outs
        ds_ref = g_ref
        i = pl.program_id(0)

        @pl.when(i == 0)
        def _():
            carry[...] = jnp.zeros_like(carry)
            dar_ref[...] = jnp.zeros_like(dar_ref)
            dai_ref[...] = jnp.zeros_like(dai_ref)
            dbd_ref[...] = jnp.zeros_like(dbd_ref)
            dcc_ref[...] = jnp.zeros_like(dcc_ref)

        dyb = dyb_ref[...].astype(bf16)
        for jj in range(2 * S5_NB):
            blk = jj % S5_NB
            g_ref[:, jj * S5_SB:(jj + 1) * S5_SB] = lax.dot_general(
                dyb[:, blk * S5_UB:(blk + 1) * S5_UB], cc_ref[jj * S5_SB:(jj + 1) * S5_SB, :], _NT,
                preferred_element_type=f32)

        has_prev = (i < nb - 1).astype(f32)
        for c0 in range(0, S5_S, S5_CH):
            re = pl.ds(c0, S5_CH)
            im = pl.ds(S5_S + c0, S5_CH)
            a_r = ar_ref[:, re]
            a_i = ai_ref[:, re]

            def upd(st, row, sp_r, sp_i, re=re, im=im, a_r=a_r, a_i=a_i):
                gr, gi, acr, aci = st
                ngr = ds_ref[row, re] + a_r * gr + a_i * gi
                ngi = ds_ref[row, im] + a_r * gi - a_i * gr
                g_ref[row, re] = ngr
                g_ref[row, im] = ngi
                return ngr, ngi, acr + ngr * sp_r + ngi * sp_i, aci + ngi * sp_r - ngr * sp_i

            def step(k, st, re=re, im=im, upd=upd):
                tt = tb - 1 - k
                prev = pl.ds(tt - 1, 1)
                return upd(st, pl.ds(tt, 1), s_ref[prev, re], s_ref[prev, im])

            zero = jnp.zeros((1, S5_CH), f32)
            st = lax.fori_loop(0, tb - 1, step, (carry[:, re], carry[:, im], zero, zero))
            last = pl.ds(SUBLANE - 1, 1)
            gr, gi, acr, aci = upd(st, pl.ds(0, 1), sh_ref[last, re] * has_prev, sh_ref[last, im] * has_prev)
            carry[:, re] = gr
            carry[:, im] = gi
            dar_ref[:, re] += acr
            dai_ref[:, re] += aci

        u = u_ref[...].astype(bf16)
        for j in range(S5_NB):
            lo, hi = j * S5_SB, (j + 1) * S5_SB
            blk = slice(j * S5_UB, (j + 1) * S5_UB)
            g_re = g_ref[:, lo:hi].astype(bf16)
            g_im = g_ref[:, S5_S + lo:S5_S + hi].astype(bf16)
            du = (lax.dot_general(g_re, bd_ref[blk, :S5_SB], _NT, preferred_element_type=f32) +
                  lax.dot_general(g_im, bd_ref[blk, S5_SB:], _NT, preferred_element_type=f32) + skip_ref[:, blk])
            du_ref[:, blk] = du.astype(du_ref.dtype)
            dbd_ref[blk, :S5_SB] += lax.dot_general(u[:, blk], g_re, _TN, preferred_element_type=f32)
            dbd_ref[blk, S5_SB:] += lax.dot_general(u[:, blk], g_im, _TN, preferred_element_type=f32)
            dcc_ref[lo:hi, :] += lax.dot_general(s_ref[:, lo:hi].astype(bf16), dyb[:, blk], _TN,
                                                 preferred_element_type=f32)
            dcc_ref[S5_S + lo:S5_S + hi, :] += lax.dot_general(s_ref[:, S5_S + lo:S5_S + hi].astype(bf16), dyb[:, blk],
                                                               _TN, preferred_element_type=f32)

    row_blk = lambda w: pl.BlockSpec((tb, w), lambda i: (rb(i), 0))
    const = lambda shape: pl.BlockSpec(shape, lambda i: (0, 0))
    return _call_with_ride(
        body, (nb,),
        [row_blk(S5_W), row_blk(2 * S5_S),
         pl.BlockSpec((SUBLANE, 2 * S5_S), lambda i: (jnp.maximum(rb(i) * r8 - 1, 0), 0)),
         pl.BlockSpec((tb, S5_W), lambda i: (rb(i), ub)), row_blk(S5_W), const(bd_c.shape), const(c_c.shape),
         const((1, S5_S)), const((1, S5_S))],
        [dyb, s, s, proj, du_skip, bd_c, c_c, ar, ai],
        [row_blk(S5_W), const((1, S5_S)), const((1, S5_S)), const(bd_c.shape), const(c_c.shape)],
        [jax.ShapeDtypeStruct((t, S5_W), bf16), jax.ShapeDtypeStruct((1, S5_S), f32),
         jax.ShapeDtypeStruct((1, S5_S), f32), jax.ShapeDtypeStruct(bd_c.shape, f32),
         jax.ShapeDtypeStruct(c_c.shape, f32)],
        name, ride, scratch=[pltpu.VMEM((tb, 2 * S5_S), f32), pltpu.VMEM((1, 2 * S5_S), f32)], sem=("arbitrary",))


def _ada_fwd(c_all, w, b, name):
    d, n = w.shape
    tn = _tile(n, 1536)

    def body(c_ref, w_ref, b_ref, o_ref):
        a = jax.nn.silu(c_ref[...]).astype(bf16)
        o_ref[...] = jnp.dot(a, w_ref[...].astype(bf16), preferred_element_type=f32) + b_ref[...]

    return pl.pallas_call(
        body, name=name, grid=(n // tn,),
        in_specs=[pl.BlockSpec(c_all.shape, lambda j: (0, 0)), pl.BlockSpec((d, tn), lambda j: (0, j)),
                  pl.BlockSpec((1, tn), lambda j: (0, j))],
        out_specs=pl.BlockSpec((c_all.shape[0], tn), lambda j: (0, j)),
        out_shape=jax.ShapeDtypeStruct((c_all.shape[0], n), f32),
        compiler_params=_cp(("parallel",)),
    )(c_all, w, b)


def _ada_bwd(c_all, dm, name):
    d = c_all.shape[1]
    n = dm.shape[1]
    tn = _tile(n, 1536)

    def body(c_ref, dm_ref, o_ref):
        a = jax.nn.silu(c_ref[...]).astype(bf16)
        o_ref[...] = lax.dot_general(a, dm_ref[...].astype(bf16), (((0,), (0,)), ((), ())), preferred_element_type=f32)

    return pl.pallas_call(
        body, name=name, grid=(n // tn,),
        in_specs=[pl.BlockSpec(c_all.shape, lambda j: (0, 0)), pl.BlockSpec((dm.shape[0], tn), lambda j: (0, j))],
        out_specs=pl.BlockSpec((d, tn), lambda j: (0, j)),
        out_shape=jax.ShapeDtypeStruct((d, n), f32),
        compiler_params=_cp(("parallel",)),
    )(c_all, dm)


def _blk_rows(r, c, nbuf, itemsize=4):
    tr = _tile(r, max(SUBLANE, (RW_VMEM_BUDGET // (2 * nbuf * c * itemsize)) // 16 * 16), 16)
    return tr if r % tr == 0 else r


def _cast_bf16(w, name, cols=None):
    r, c = w.shape
    cols = cols or c
    tr = _blk_rows(r, cols, 2)

    def body(w_ref, o_ref):
        o_ref[:, :c] = w_ref[...].astype(bf16)
        if cols > c:
            o_ref[:, c:] = jnp.zeros((tr, cols - c), bf16)

    return pl.pallas_call(
        body, name=name, grid=(r // tr,), in_specs=[pl.BlockSpec((tr, c), lambda i: (i, 0))],
        out_specs=pl.BlockSpec((tr, cols), lambda i: (i, 0)), out_shape=jax.ShapeDtypeStruct((r, cols), bf16),
        compiler_params=_cp(("parallel",)),
    )(w)


def _sum_lead(parts, name, cols=None):
    n, r, c = parts.shape
    cols = cols or c
    tr = _blk_rows(r, c, n + 2)

    def body(p_ref, o_ref):
        acc = p_ref[0].astype(f32)
        for q in range(1, n):
            acc = acc + p_ref[q].astype(f32)
        o_ref[...] = acc[:, :cols]

    return pl.pallas_call(
        body, name=name, grid=(r // tr,), in_specs=[pl.BlockSpec((n, tr, c), lambda i: (0, i, 0))],
        out_specs=pl.BlockSpec((tr, cols), lambda i: (i, 0)), out_shape=jax.ShapeDtypeStruct((r, cols), f32),
        compiler_params=_cp(("parallel",)),
    )(parts)


def _adamw(w, m, v, parts, name):
    r, c = w.shape
    npart = len(parts)
    tr = _blk_rows(r, c, 7 + npart)
    c1 = 1.0 - ADAM_B1 ** ADAM_STEP
    c2 = 1.0 - ADAM_B2 ** ADAM_STEP

    def body(*refs):
        w_ref, m_ref, v_ref = refs[:3]
        g_ref, d_ref, nm_ref, nv_ref = refs[3 + npart:]
        g = refs[3][...].astype(f32)
        for p in refs[4:3 + npart]:
            g = g + p[...].astype(f32)
        nm = ADAM_B1 * m_ref[...] + (1.0 - ADAM_B1) * g
        nv = ADAM_B2 * v_ref[...] + (1.0 - ADAM_B2) * jnp.square(g)
        g_ref[...] = g
        nm_ref[...] = nm
        nv_ref[...] = nv
        d_ref[...] = -ADAM_LR * ((nm / c1) / (jnp.sqrt(nv / c2) + ADAM_EPS) + ADAM_WD * w_ref[...])

    spec = pl.BlockSpec((tr, c), lambda i: (i, 0))
    return pl.pallas_call(
        body, name=name, grid=(r // tr,), in_specs=[spec] * (3 + npart), out_specs=[spec] * 4,
        out_shape=[jax.ShapeDtypeStruct((r, c), f32)] * 4, compiler_params=_cp(("parallel",)),
    )(w, m, v, *parts)


def _ag_small(x_shard, name):
    m_per, n = x_shard.shape

    def body(x_ref, out_ref, send_sems, recv_sems, local_sem):
        x, y, c = lax.axis_index("x"), lax.axis_index("y"), lax.axis_index("c")
        me, sibling = (x, y, c), (x, y, 1 - c)
        chips = [(1 - x, y), (x, 1 - y), (1 - x, 1 - y)]

        def rows(px, py, pc):
            return out_ref.at[pl.ds((4 * px + 2 * py + pc) * m_per, m_per), :]

        def copy(k, block, to, src=None):
            return pltpu.make_async_remote_copy(
                src_ref=rows(*block) if src is None else src, dst_ref=rows(*block),
                send_sem=send_sems.at[k], recv_sem=recv_sems.at[k], device_id=to, device_id_type=MESH)

        mine = pltpu.make_async_copy(x_ref, rows(*me), local_sem)
        mine.start()
        first = [copy(0, me, sibling, src=x_ref)]
        first += [copy(1 + j, me, (*chip, c), src=x_ref) for j, chip in enumerate(chips)]
        for cp in first:
            cp.start()
        passed = [copy(4 + j, (*chip, c), sibling) for j, chip in enumerate(chips)]
        for j, chip in enumerate(chips):
            copy(1 + j, (*chip, c), me).wait_recv()
            passed[j].start()
        copy(0, sibling, me).wait_recv()
        for j, chip in enumerate(chips):
            copy(4 + j, (*chip, 1 - c), me).wait_recv()
        for cp in first + passed:
            cp.wait_send()
        mine.wait()

    return pl.pallas_call(
        body, name=name, out_shape=jax.ShapeDtypeStruct((8 * m_per, n), x_shard.dtype),
        in_specs=[pl.BlockSpec(memory_space=pltpu.VMEM)], out_specs=pl.BlockSpec(memory_space=pltpu.VMEM),
        scratch_shapes=[pltpu.SemaphoreType.DMA((7,)), pltpu.SemaphoreType.DMA((7,)), pltpu.SemaphoreType.DMA],
        compiler_params=pltpu.CompilerParams(vmem_limit_bytes=VMEM_LIMIT),
    )(x_shard)


def _run_ride(ride, name):
    n = ride.n

    def body(*refs):
        ride.start(refs[:n], refs[n:2 * n], refs[2 * n:])
        ride.wait(refs[:n], refs[n:2 * n], refs[2 * n:])

    return pl.pallas_call(
        body, name=name, out_shape=ride.out_shape, in_specs=ride.specs, out_specs=ride.specs,
        scratch_shapes=ride.scratch,
    )(*ride.srcs)


class _Ride:
    def __init__(self, srcs, scatter):
        self.srcs, self.scatter, self.n = list(srcs), scatter, len(srcs)
        n = self.n
        self.out_shape = [jax.ShapeDtypeStruct(s.shape if scatter else (4,) + s.shape, s.dtype) for s in srcs]
        self.specs = [pl.BlockSpec(memory_space=pl.ANY)] * n
        self.scratch = [pltpu.SemaphoreType.DMA((3 * n,)), pltpu.SemaphoreType.DMA((3 * n,)),
                        pltpu.SemaphoreType.DMA((n,))]

    def _copies(self, ins, outs, sems):
        send_sems, recv_sems, local_sems = sems
        x, y, c = lax.axis_index("x"), lax.axis_index("y"), lax.axis_index("c")
        my_k = 2 * x + y
        peers = [(1 - x, y), (x, 1 - y), (1 - x, 1 - y)]
        local, sends, recvs = [], [], []
        for a in range(self.n):
            own = ins[a].at[my_k] if self.scatter else ins[a]
            local.append(pltpu.make_async_copy(own, outs[a].at[my_k], local_sems.at[a]))
            for j, (px, py) in enumerate(peers):
                sems_j = dict(send_sem=send_sems.at[3 * a + j], recv_sem=recv_sems.at[3 * a + j],
                              device_id=(px, py, c), device_id_type=MESH)
                src = ins[a].at[2 * px + py] if self.scatter else ins[a]
                sends.append(pltpu.make_async_remote_copy(src_ref=src, dst_ref=outs[a].at[my_k], **sems_j))
                landed = outs[a].at[2 * px + py]
                recvs.append(pltpu.make_async_remote_copy(src_ref=landed, dst_ref=landed, **sems_j))
        return local, sends, recvs

    def start(self, ins, outs, sems):
        local, sends, _ = self._copies(ins, outs, sems)
        for cp in local + sends:
            cp.start()

    def wait(self, ins, outs, sems):
        local, sends, recvs = self._copies(ins, outs, sems)
        for cp in recvs:
            cp.wait_recv()
        for cp in sends:
            cp.wait_send()
        for cp in local:
            cp.wait()


class _RideGather:
    def __init__(self, srcs):
        self.srcs, self.n = list(srcs), len(srcs)
        n = self.n
        assert all(s.shape[0] % 32 == 0 for s in srcs)
        self.out_shape = [jax.ShapeDtypeStruct((4,) + s.shape, s.dtype) for s in srcs]
        self.specs = [pl.BlockSpec(memory_space=pl.ANY)] * n
        dma = pltpu.SemaphoreType.DMA
        self.scratch = [dma((3 * n,)), dma((3 * n,)), dma((3 * n,)), dma((3 * n,)), dma((n,))]

    def _copies(self, ins, outs, sems):
        send_sems, recv_sems, pass_send, pass_recv, local_sems = sems
        x, y, c = lax.axis_index("x"), lax.axis_index("y"), lax.axis_index("c")
        my_k = 2 * x + y
        peers = [(1 - x, y), (x, 1 - y), (1 - x, 1 - y)]
        local, sends, recvs, passes, pass_recvs = [], [], [], [], []
        for a in range(self.n):
            half = self.srcs[a].shape[0] // 2
            mine = pl.ds(pl.multiple_of(c * half, 16), half)
            other = pl.ds(pl.multiple_of((1 - c) * half, 16), half)
            local.append(pltpu.make_async_copy(ins[a], outs[a].at[my_k], local_sems.at[a]))
            for j, (px, py) in enumerate(peers):
                q = 3 * a + j
                over_ici = dict(send_sem=send_sems.at[q], recv_sem=recv_sems.at[q], device_id=(px, py, c),
                                device_id_type=MESH)
                to_sibling = dict(send_sem=pass_send.at[q], recv_sem=pass_recv.at[q], device_id=(x, y, 1 - c),
                                  device_id_type=MESH)
                sends.append(pltpu.make_async_remote_copy(src_ref=ins[a].at[mine], dst_ref=outs[a].at[my_k, mine],
                                                          **over_ici))
                landed = outs[a].at[2 * px + py, mine]
                recvs.append(pltpu.make_async_remote_copy(src_ref=landed, dst_ref=landed, **over_ici))
                passes.append(pltpu.make_async_remote_copy(src_ref=landed, dst_ref=landed, **to_sibling))
                from_sibling = outs[a].at[2 * px + py, other]
                pass_recvs.append(pltpu.make_async_remote_copy(src_ref=from_sibling, dst_ref=from_sibling, **to_sibling))
        return local, sends, recvs, passes, pass_recvs

    def start(self, ins, outs, sems):
        local, sends = self._copies(ins, outs, sems)[:2]
        for cp in local + sends:
            cp.start()

    def wait(self, ins, outs, sems):
        local, sends, recvs, passes, pass_recvs = self._copies(ins, outs, sems)
        for rc, ps in zip(recvs, passes):
            rc.wait_recv()
            ps.start()
        for cp in pass_recvs:
            cp.wait_recv()
        for cp in sends + passes:
            cp.wait_send()
        for cp in local:
            cp.wait()


class _RideSwap:
    def __init__(self, srcs):
        self.srcs, self.n = list(srcs), len(srcs)
        self.out_shape = [jax.ShapeDtypeStruct(s.shape, s.dtype) for s in srcs]
        self.specs = [pl.BlockSpec(memory_space=pl.ANY)] * self.n
        self.scratch = [pltpu.SemaphoreType.DMA((self.n,)), pltpu.SemaphoreType.DMA((self.n,))]

    def _copies(self, ins, outs, sems):
        send_sems, recv_sems = sems
        sib = (lax.axis_index("x"), lax.axis_index("y"), 1 - lax.axis_index("c"))
        return [pltpu.make_async_remote_copy(src_ref=ins[a], dst_ref=outs[a], send_sem=send_sems.at[a],
                                             recv_sem=recv_sems.at[a], device_id=sib, device_id_type=MESH)
                for a in range(self.n)]

    def start(self, ins, outs, sems):
        for cp in self._copies(ins, outs, sems):
            cp.start()

    def wait(self, ins, outs, sems):
        cps = self._copies(ins, outs, sems)
        for cp in cps:
            cp.wait_recv()
        for cp in cps:
            cp.wait_send()


class _Rides:
    def __init__(self, rides):
        self.rides = list(rides)
        self.n = sum(r.n for r in self.rides)
        self.srcs = [s for r in self.rides for s in r.srcs]
        self.out_shape = [s for r in self.rides for s in r.out_shape]
        self.specs = [s for r in self.rides for s in r.specs]
        self.scratch = [s for r in self.rides for s in r.scratch]

    def _each(self, ins, outs, sems):
        i = k = 0
        for r in self.rides:
            yield r, ins[i:i + r.n], outs[i:i + r.n], sems[k:k + len(r.scratch)]
            i += r.n
            k += len(r.scratch)

    def start(self, ins, outs, sems):
        for r, a, b, c in self._each(ins, outs, sems):
            r.start(a, b, c)

    def wait(self, ins, outs, sems):
        for r, a, b, c in self._each(ins, outs, sems):
            r.wait(a, b, c)

    def split(self, results):
        out, i = [], 0
        for r in self.rides:
            out.append(results[i:i + r.n])
            i += r.n
        return out


def _gather8(vec, name):
    size = vec.shape[0]
    n = _round_up(size, SUBLANE * LANE)
    blk = jnp.concatenate([vec, jnp.zeros((n - size,), f32)]).reshape(SUBLANE, n // SUBLANE)
    out = _ag_small(blk, name)
    return out.reshape(8, n)[:, :size]


def _head_spread_matrices():
    e = np.zeros((LANE, SSD_DI), np.float32)
    for h in range(SSD_HEADS):
        e[h, h * SSD_P:(h + 1) * SSD_P] = 1.0
    return jnp.asarray(e, bf16), jnp.asarray(e.T, bf16)


def _heads_to_lanes(v):
    return jnp.repeat(v, SSD_P).reshape(1, SSD_DI)


def _block_diag8(blocks):
    g, r, c = blocks.shape
    b = blocks.reshape(g // S5_NB, S5_NB, r, c)
    eye = jnp.eye(S5_NB, dtype=bool)[None, :, None, :, None]
    return jnp.where(eye, b[:, :, :, None, :], jnp.zeros((), blocks.dtype)).reshape(g * r, S5_NB * c)


def _diag8(mat, r, c):
    g = mat.shape[0] // r
    m = mat.reshape(g // S5_NB, S5_NB, r, S5_NB, c)
    eye = jnp.eye(S5_NB, dtype=bool)[None, :, None, :, None]
    return jnp.where(eye, m, 0.0).sum(axis=3).reshape(g, r, c)


class _Layout:
    def __init__(self, d):
        self.d = d
        self.z, self.xbc, self.u = 0, SSD_DI, SSD_DI + CONV_DIM
        self.ga = self.u + S5_W
        self.gb = self.ga + d
        self.dt = self.gb + d
        self.np_ = self.dt + LANE
        self.in_cols = SSD_DI + CONV_DIM + SSD_HEADS + S5_W + 2 * d
        off_dt = SSD_DI + CONV_DIM
        off_u = off_dt + SSD_HEADS
        off_g = off_u + S5_W
        self.src = [(0, off_dt), (off_u, off_u + S5_W + 2 * d), (off_dt, off_u)]

    def arrange_slabs(self, g):
        pieces = [p for lo, hi in self.src for p in _cols_from_slabs(g, lo, hi)]
        pieces.append(jnp.zeros((g.shape[1], LANE - SSD_HEADS), g.dtype))
        return jnp.concatenate(pieces, axis=1)

    def restore_slabs(self, chunks):
        (a0, a1), (b0, b1), (c0, c1) = self.src
        n_a, n_b = a1 - a0, b1 - b0
        segs = [(a0, a1, 0), (c0, c1, n_a + n_b), (b0, b1, n_a)]
        firsts = np.cumsum([0] + [c.shape[1] for c in chunks])

        def take(lo, hi):
            return [c[:, max(lo, f) - f:min(hi, f + c.shape[1]) - f] for c, f in zip(chunks, firsts)
                    if max(lo, f) < min(hi, f + c.shape[1])]

        cs = self.in_cols // 4
        slabs = []
        for k in range(4):
            lo, hi = k * cs, (k + 1) * cs
            parts = [p for s0, s1, pos in segs if max(lo, s0) < min(hi, s1)
                     for p in take(pos + max(lo, s0) - s0, pos + min(hi, s1) - s0)]
            slabs.append(jnp.concatenate(parts, axis=1))
        return jnp.stack(slabs)


def _cols_from_slabs(g, start, stop):
    c = g.shape[2]
    return [g[k][:, max(start, k * c) - k * c:min(stop, (k + 1) * c) - k * c] for k in range(4)
            if max(start, k * c) < min(stop, (k + 1) * c)]


def _unshard_cols(g):
    return jnp.concatenate([g[k] for k in range(4)], axis=1)


def _shard_cols(w):
    r, c4 = w.shape
    return w.reshape(r, 4, c4 // 4).transpose(1, 0, 2)


def kernel(x, c, w_ada, b_ada, norm_ffn1, w_ffn1_in, w_ffn1_out, norm_mix, w_in, conv_w, conv_b, dt_bias, a_log, d_ssd, ssd_norm_w, w_a_proj, s5_lambda_re, s5_lambda_im, s5_b_re, s5_b_im, s5_c_re, s5_c_im, s5_d, s5_log_dt, w_b_glu, w_out, norm_ffn2, w_ffn2_in, w_ffn2_out, norm_final, loss_target, m_w_ada, m_b_ada, m_norm_ffn1, m_w_ffn1_in, m_w_ffn1_out, m_norm_mix, m_w_in, m_conv_w, m_conv_b, m_dt_bias, m_a_log, m_d_ssd, m_ssd_norm_w, m_w_a_proj, m_s5_lambda_re, m_s5_lambda_im, m_s5_b_re, m_s5_b_im, m_s5_c_re, m_s5_c_im, m_s5_d, m_s5_log_dt, m_w_b_glu, m_w_out, m_norm_ffn2, m_w_ffn2_in, m_w_ffn2_out, m_norm_final, v_w_ada, v_b_ada, v_norm_ffn1, v_w_ffn1_in, v_w_ffn1_out, v_norm_mix, v_w_in, v_conv_w, v_conv_b, v_dt_bias, v_a_log, v_d_ssd, v_ssd_norm_w, v_w_a_proj, v_s5_lambda_re, v_s5_lambda_im, v_s5_b_re, v_s5_b_im, v_s5_c_re, v_s5_c_im, v_s5_d, v_s5_log_dt, v_w_b_glu, v_w_out, v_norm_ffn2, v_w_ffn2_in, v_w_ffn2_out, v_norm_final):
    W = dict(w_ada=w_ada, b_ada=b_ada, norm_ffn1=norm_ffn1, w_ffn1_in=w_ffn1_in, w_ffn1_out=w_ffn1_out, norm_mix=norm_mix, w_in=w_in, conv_w=conv_w, conv_b=conv_b, dt_bias=dt_bias, a_log=a_log, d_ssd=d_ssd, ssd_norm_w=ssd_norm_w, w_a_proj=w_a_proj, s5_lambda_re=s5_lambda_re, s5_lambda_im=s5_lambda_im, s5_b_re=s5_b_re, s5_b_im=s5_b_im, s5_c_re=s5_c_re, s5_c_im=s5_c_im, s5_d=s5_d, s5_log_dt=s5_log_dt, w_b_glu=w_b_glu, w_out=w_out, norm_ffn2=norm_ffn2, w_ffn2_in=w_ffn2_in, w_ffn2_out=w_ffn2_out, norm_final=norm_final)
    Mo = dict(w_ada=m_w_ada, b_ada=m_b_ada, norm_ffn1=m_norm_ffn1, w_ffn1_in=m_w_ffn1_in, w_ffn1_out=m_w_ffn1_out, norm_mix=m_norm_mix, w_in=m_w_in, conv_w=m_conv_w, conv_b=m_conv_b, dt_bias=m_dt_bias, a_log=m_a_log, d_ssd=m_d_ssd, ssd_norm_w=m_ssd_norm_w, w_a_proj=m_w_a_proj, s5_lambda_re=m_s5_lambda_re, s5_lambda_im=m_s5_lambda_im, s5_b_re=m_s5_b_re, s5_b_im=m_s5_b_im, s5_c_re=m_s5_c_re, s5_c_im=m_s5_c_im, s5_d=m_s5_d, s5_log_dt=m_s5_log_dt, w_b_glu=m_w_b_glu, w_out=m_w_out, norm_ffn2=m_norm_ffn2, w_ffn2_in=m_w_ffn2_in, w_ffn2_out=m_w_ffn2_out, norm_final=m_norm_final)
    Vo = dict(w_ada=v_w_ada, b_ada=v_b_ada, norm_ffn1=v_norm_ffn1, w_ffn1_in=v_w_ffn1_in, w_ffn1_out=v_w_ffn1_out, norm_mix=v_norm_mix, w_in=v_w_in, conv_w=v_conv_w, conv_b=v_conv_b, dt_bias=v_dt_bias, a_log=v_a_log, d_ssd=v_d_ssd, ssd_norm_w=v_ssd_norm_w, w_a_proj=v_w_a_proj, s5_lambda_re=v_s5_lambda_re, s5_lambda_im=v_s5_lambda_im, s5_b_re=v_s5_b_re, s5_b_im=v_s5_b_im, s5_c_re=v_s5_c_re, s5_c_im=v_s5_c_im, s5_d=v_s5_d, s5_log_dt=v_s5_log_dt, w_b_glu=v_w_b_glu, w_out=v_w_out, norm_ffn2=v_norm_ffn2, w_ffn2_in=v_w_ffn2_in, w_ffn2_out=v_w_ffn2_out, norm_final=v_norm_final)

    t, d = x.shape[1], x.shape[2]
    ff = 4 * w_ffn1_out.shape[1]
    hf = ff // 2
    hfp = _round_up(hf, LANE)
    lay = _Layout(d)
    xi, yi, ci = lax.axis_index("x"), lax.axis_index("y"), lax.axis_index("c")
    k_me = 2 * xi + yi
    e_me = 4 * xi + 2 * yi + ci
    x2d = x[0]
    tgt = loss_target[0]

    cw_cols = conv_w.shape[2]
    g1 = _gather8(jnp.concatenate([c[0], conv_w[0].reshape(-1)]), "gather_c_convw")
    c_all = g1[:, :d]
    conv_full = g1[::2, d:].reshape(4, CONV_K, cw_cols).transpose(1, 0, 2).reshape(CONV_K, CONV_DIM)
    conv_w8 = jnp.zeros((SUBLANE, CONV_DIM), f32).at[:CONV_K].set(conv_full)

    n_ada_loc = w_ada.shape[2]
    b_loc = lax.dynamic_slice(b_ada, (0, k_me * n_ada_loc), (1, n_ada_loc))
    mods_part = _ada_fwd(c_all, w_ada[0], b_loc, "ada_fwd")
    g2 = _gather8(mods_part.reshape(-1), "gather_mods").reshape(8, 8, n_ada_loc)
    mods = lax.dynamic_index_in_dim(g2[::2], e_me, axis=1, keepdims=False).reshape(N_ADA, d)
    sh1, sc1, gt1, sh2, sc2, gt2, sh3, sc3, gt3 = [mods[i:i + 1] for i in range(N_ADA)]

    cast = {n: _cast_bf16(W[n][0], "cast_" + n, cols=hfp if n in ('w_ffn1_in', 'w_ffn2_in') else None)
            for n in BIG}

    def gather_of(names):
        return _RideGather([cast[n] for n in names])

    def rows_of(g):
        return g.reshape(4 * g.shape[1], g.shape[2])

    def ffn_out(g):
        z = jnp.zeros((hfp - hf, g.shape[2]), g.dtype)
        return jnp.concatenate([g[0], g[1], z, g[2], g[3], z], axis=0)

    nf1, nmx, nf2 = norm_ffn1, norm_mix, norm_ffn2
    nfin = norm_final.reshape(1, d)

    (g_w1i,) = _run_ride(gather_of(['w_ffn1_in']), "gather_w_ffn1_in")
    w1i = g_w1i
    (h1,) = _rw(_f_mod, [x2d], [nf1, sh1, sc1], [(d, bf16)], name="mod1")
    ab1, act1, (g_w1o, g_win) = _ffn_in(h1, w1i, "ffn1_in", ride=gather_of(['w_ffn1_out', 'w_in']))
    w1o = ffn_out(g_w1o)
    w_inr = lay.arrange_slabs(g_win)
    f1, (g_wa, g_wglu, g_wo) = _mm(act1, w1o, 'nn', out_dtype=bf16, name="ffn1_out",
                                   ride=gather_of(['w_a_proj', 'w_b_glu', 'w_out']))
    w_a = rows_of(g_wa)
    w_glu, w_o = _unshard_cols(g_wglu), rows_of(g_wo)
    res1 = functools.partial(_f_res_mod, 0.5)
    x1, h2 = _rw(res1, [x2d, f1], [gt1, nmx, sh2, sc2], [(d, f32), (d, bf16)], name="res1_mod2")
    proj, (g_w2i,) = _mm(h2, w_inr, 'nn', out_dtype=f32, name="in_proj", ride=gather_of(['w_ffn2_in']))
    w2i = g_w2i

    pre = _conv_fwd(proj, lay.xbc, conv_w8, conv_b, "conv_fwd")
    spread, spread_t = _head_spread_matrices()
    bias128 = jnp.zeros((1, LANE), f32).at[:, :SSD_HEADS].set(dt_bias)
    xs, bm, cm, dt4 = _rw(_f_ssd_pre, [pre, (proj, lay.dt, LANE)], [bias128, spread, spread_t],
                          [(SSD_DI, f32), (SSD_G * SSD_N, f32), (SSD_G * SSD_N, f32), (SSD_DI, f32)],
                          name="ssd_pre")

    def head_params(a_log_, d_ssd_):
        return _heads_to_lanes(-jnp.exp(a_log_[0])), _heads_to_lanes(d_ssd_[0])

    (a4, dsk4), head_vjp = jax.vjp(head_params, a_log, d_ssd)
    y_ssd, hs, (g_w2o,) = _ssd_fwd(xs, bm, cm, dt4, a4, dsk4, "ssd_fwd", ride=gather_of(['w_ffn2_out']))
    w2o = ffn_out(g_w2o)
    (y_a,) = _rw(_f_ssd_post, [y_ssd, (proj, lay.z, SSD_DI)], [ssd_norm_w], [(SSD_DI, bf16)], name="ssd_post")
    p_a = _mm(y_a, w_a, 'nn', out_dtype=bf16, name="a_proj")

    col = lambda v: v.reshape(S5_S, 1)
    ldt_col = jnp.repeat(s5_log_dt[0], S5_P).reshape(S5_S, 1)
    prep_rows = [col(s5_lambda_re[0]), col(s5_lambda_im[0]), ldt_col,
                 s5_b_re[0].reshape(S5_S, S5_I), s5_b_im[0].reshape(S5_S, S5_I)]
    ar, ai, bbr, bbi = _rw(_f_s5_prep, prep_rows, [], [(1, f32), (1, f32), (S5_I, f32), (S5_I, f32)],
                           name="s5_prep", tm=512)
    to_bd = lambda bb: _block_diag8(bb.reshape(S5_G, S5_P, S5_I).transpose(0, 2, 1).astype(bf16))
    bd_c = jnp.concatenate([to_bd(bbr), to_bd(bbi)], axis=1)
    c_c = jnp.concatenate([_block_diag8(s5_c_re[0].transpose(0, 2, 1).astype(bf16)),
                           _block_diag8((-s5_c_im[0]).transpose(0, 2, 1).astype(bf16))], axis=0)
    ar_row, ai_row = ar.reshape(1, S5_S), ai.reshape(1, S5_S)
    s5s, yb = _s5_fwd(proj, lay.u, bd_c, c_c, ar_row, ai_row, "s5_fwd")
    d_row = s5_d[0].reshape(1, S5_W)
    (gl,) = _rw(_f_s5_post, [yb, (proj, lay.u, S5_W)], [d_row], [(S5_W, bf16)], name="s5_post")
    glu = _mm(gl, w_glu, 'nn', out_dtype=bf16, name="glu_proj")

    merge_rows = [p_a, glu, (proj, lay.ga, d), (proj, lay.gb, d)]
    (merged,) = _rw(_f_merge, merge_rows, [], [(d, bf16)], name="merge")
    o_mix = _mm(merged, w_o, 'nn', out_dtype=bf16, name="out_proj")
    res2 = functools.partial(_f_res_mod, 1.0)
    x2, h3 = _rw(res2, [x1, o_mix], [gt2, nf2, sh3, sc3], [(d, f32), (d, bf16)], name="res2_mod3")
    ab2, act2, _ = _ffn_in(h3, w2i, "ffn2_in")
    f2 = _mm(act2, w2o, 'nn', out_dtype=bf16, name="ffn2_out")

    ones = jnp.ones((t, 1), f32)
    dx2, df2, dgt3, dnfin, loss_acc = _rw_vjp(_f_final, [x2, f2, tgt], [gt3, nfin], [ones],
                                              row_grads=[f32, bf16, None], param_grads=[True, True],
                                              name="loss_and_bwd", total_of=0)
    loss = lax.psum(loss_acc[0, 0], AXES)
    def ffn_out_back(g):
        return jnp.concatenate([g[:hf], g[hfp:hfp + hf]], axis=0).reshape(4, ff // 4, g.shape[1])

    def rows_back(g, rows):
        return g.reshape(4, rows // 4, g.shape[1])

    def scatter_of(pairs):
        return _Ride([g for _, g in pairs], True)

    terms = {}
    dab2, _ = _ffn_out_dx(df2, w2o, ab2, "ffn2_out_dx")
    dw2o = _mm(act2, df2, 'tn', out_dtype=bf16, name="ffn2_out_dw")
    dh3, (terms['w_ffn2_out'],) = _ffn_in_dx(dab2, w2i, "ffn2_in_dx",
                                             ride=scatter_of([('w_ffn2_out', ffn_out_back(dw2o))]))
    dw2i = _ffn_in_dw(h3, dab2, "ffn2_in_dw")
    dx1, do_mix, dgt2, dnf2, dsh3, dsc3 = _rw_vjp(
        res2, [x1, o_mix], [gt2, nf2, sh3, sc3], [dx2, dh3], row_grads=[f32, bf16], param_grads=[True] * 4,
        name="res2_mod3_bwd")
    dmerged = _mm(do_mix, w_o, 'nt', out_dtype=bf16, name="out_proj_dx")
    dw_o = _mm(merged, do_mix, 'tn', out_dtype=bf16, name="out_proj_dw")
    dp_a, dglu, dga, dgb = _rw_vjp(_f_merge, merge_rows, [], [dmerged], row_grads=[bf16] * 4,
                                   param_grads=[], name="merge_bwd")

    dgl = _mm(dglu, w_glu, 'nt', out_dtype=bf16, name="glu_proj_dx")
    dw_glu = _mm(gl, dglu, 'tn', out_dtype=bf16, name="glu_proj_dw")
    dyb, du_skip, dd_row = _rw_vjp(_f_s5_post, [yb, (proj, lay.u, S5_W)], [d_row], [dgl],
                                   row_grads=[bf16, f32], param_grads=[True], name="s5_post_bwd")
    (du, dar, dai, dbd_c, dc_c), (terms['w_ffn2_in'],) = _s5_bwd(
        dyb, s5s, proj, lay.u, bd_c, c_c, ar_row, ai_row, du_skip, "s5_bwd",
        ride=scatter_of([('w_ffn2_in', dw2i)]))
    from_bd = lambda m_: _diag8(m_, S5_I, S5_P).transpose(0, 2, 1).reshape(S5_S, S5_I)
    dprep = _rw_vjp(_f_s5_prep, prep_rows, [], [dar.reshape(S5_S, 1), dai.reshape(S5_S, 1),
                                                from_bd(dbd_c[:, :S5_SB]), from_bd(dbd_c[:, S5_SB:])],
                    row_grads=[f32] * 5, param_grads=[], name="s5_prep_bwd", tm=512)
    dlr, dli, dldt, dbr, dbi = dprep
    g_s5 = dict(
        s5_lambda_re=dlr.reshape(S5_G, S5_P), s5_lambda_im=dli.reshape(S5_G, S5_P),
        s5_log_dt=dldt.reshape(S5_G, S5_P).sum(axis=1),
        s5_b_re=dbr.reshape(S5_G, S5_P, S5_I), s5_b_im=dbi.reshape(S5_G, S5_P, S5_I),
        s5_c_re=_diag8(dc_c[:S5_S], S5_P, S5_I).transpose(0, 2, 1),
        s5_c_im=-_diag8(dc_c[S5_S:], S5_P, S5_I).transpose(0, 2, 1),
        s5_d=dd_row.reshape(S5_G, S5_I))

    dy_a = _mm(dp_a, w_a, 'nt', out_dtype=bf16, name="a_proj_dx")
    dw_a = _mm(y_a, dp_a, 'tn', out_dtype=bf16, name="a_proj_dw")
    dy_ssd, dz, dssd_nw = _rw_vjp(_f_ssd_post, [y_ssd, (proj, lay.z, SSD_DI)], [ssd_norm_w], [dy_a],
                                  row_grads=[f32, bf16], param_grads=[True], name="ssd_post_bwd")
    early = [('w_out', rows_back(dw_o, d)), ('w_b_glu', _shard_cols(dw_glu)), ('w_a_proj', rows_back(dw_a, SSD_DI))]
    (dxs, dbm, dcm, ddt4, da4, ddsk4), landed = _ssd_bwd(xs, bm, cm, dt4, a4, dsk4, hs, dy_ssd, "ssd_bwd",
                                                         ride=scatter_of(early))
    terms.update({n: p for (n, _), p in zip(early, landed)})
    da_log, dd_ssd = head_vjp((da4, ddsk4))
    dpre, ddt_raw, dbias128 = _rw_vjp(_f_ssd_pre, [pre, (proj, lay.dt, LANE)], [bias128, spread, spread_t],
                                      [dxs, dbm, dcm, ddt4], row_grads=[f32, bf16],
                                      param_grads=[True, False, False], name="ssd_pre_bwd")
    dxbc, dconv_w8, dconv_b = _conv_bwd(dpre, proj, lay.xbc, conv_w8, "conv_bwd")

    dproj = [dz, dxbc, du, dga, dgb]
    dw_in = [_mm(h2, p, 'tn', out_dtype=bf16, name="in_proj_dw_%d" % i) for i, p in enumerate(dproj + [ddt_raw])]
    dh2, (terms['w_in'],) = _in_proj_dx(dproj, ddt_raw, lay.dt, w_inr, "in_proj_dx",
                                        ride=scatter_of([('w_in', lay.restore_slabs(dw_in))]))
    dx0, df1, dgt1, dnmx, dsh2, dsc2 = _rw_vjp(
        res1, [x2d, f1], [gt1, nmx, sh2, sc2], [dx1, dh2], row_grads=[f32, bf16], param_grads=[True] * 4,
        name="res1_mod2_bwd")
    dw1o = _mm(act1, df1, 'tn', out_dtype=bf16, name="ffn1_out_dw")
    dab1, (terms['w_ffn1_out'],) = _ffn_out_dx(df1, w1o, ab1, "ffn1_out_dx",
                                               ride=scatter_of([('w_ffn1_out', ffn_out_back(dw1o))]))
    dw1i = _ffn_in_dw(h1, dab1, "ffn1_in_dw")

    last = 'w_ffn1_in'
    keep = {'w_ffn1_in': hf, 'w_ffn2_in': hf}
    sums = {n: _sum_lead(terms[n], "sum_" + n, cols=keep.get(n)) for n in BIG if n != last}
    swap = _RideSwap([sums[n] for n in BIG if n != last])
    rides = _Rides([scatter_of([(last, dw1i)]), swap])
    dh1, landed = _ffn_in_dx(dab1, w1i, "ffn1_in_dx", ride=rides)
    (terms[last],), swapped = rides.split(landed)
    others = dict(zip([n for n in BIG if n != last], swapped))
    grad_x, dnf1, dsh1, dsc1 = _rw_vjp(_f_mod_keep, [x2d], [nf1, sh1, sc1], [dh1, dx0],
                                       row_grads=[f32], param_grads=[True] * 3, name="mod1_bwd")
    d_mods = jnp.concatenate([dsh1, dsc1, dgt1, dsh2, dsc2, dgt2, dsh3, dsc3, dgt3], axis=1).reshape(-1)
    sums[last] = _sum_lead(terms[last], "sum_" + last, cols=keep.get(last))
    (others[last],) = _run_ride(_RideSwap([sums[last]]), "swap_sum_" + last)

    out_g, out_d, out_m, out_v = {}, {}, {}, {}
    for n in BIG:
        r = _adamw(W[n][0], Mo[n][0], Vo[n][0], [sums[n], others[n]], "adamw_" + n)
        out_g[n], out_d[n], out_m[n], out_v[n] = [o[None] for o in r]

    local = dict(
        b_ada=d_mods, norm_ffn1=dnf1, norm_mix=dnmx, conv_w=dconv_w8[:CONV_K], conv_b=dconv_b,
        dt_bias=dbias128[:, :SSD_HEADS], a_log=da_log, d_ssd=dd_ssd, ssd_norm_w=dssd_nw,
        norm_ffn2=dnf2, norm_final=dnfin, **g_s5)
    flat = jnp.concatenate([local[n].reshape(-1) for n in SMALL])
    g3 = _gather8(flat, "gather_small_grads")
    n_small = flat.shape[0]
    npad = _round_up(n_small, SUBLANE * LANE)
    g3p = jnp.zeros((8, npad), f32).at[:, :n_small].set(g3).reshape(8, npad // LANE, LANE)
    gsum = _sum_lead(g3p, "sum_small").reshape(-1)

    def local_shard(n, a):
        if n == 'conv_w':
            return lax.dynamic_slice(a.reshape(CONV_K, CONV_DIM), (0, k_me * cw_cols), (CONV_K, cw_cols))
        return a

    pieces, off = {}, 0
    for n in SMALL:
        sz = local[n].size
        pieces[n] = local_shard(n, gsum[off:off + sz]).reshape(W[n].shape)
        off += sz

    def pack(dct):
        v_ = jnp.concatenate([dct[n].reshape(-1) for n in SMALL])
        pad = _round_up(v_.shape[0], SUBLANE * LANE) - v_.shape[0]
        return jnp.concatenate([v_, jnp.ones((pad,), f32)]).reshape(-1, LANE)

    rs = _adamw(pack(W), pack(Mo), pack(Vo), [pack(pieces)], "adamw_small")
    off = 0
    for n in SMALL:
        sz = W[n].size
        out_g[n], out_d[n], out_m[n], out_v[n] = [o.reshape(-1)[off:off + sz].reshape(W[n].shape) for o in rs]
        off += sz

    dm_loc = lax.dynamic_slice(g3[:, :N_ADA * d], (0, k_me * n_ada_loc), (SUBLANE, n_ada_loc))
    g_ada = _ada_bwd(c_all, dm_loc, "ada_bwd")
    r = _adamw(w_ada[0], m_w_ada[0], v_w_ada[0], [g_ada], "adamw_w_ada")
    out_g['w_ada'], out_d['w_ada'], out_m['w_ada'], out_v['w_ada'] = [o[None] for o in r]

    return (loss, grad_x[None], *[out_g[n] for n in WEIGHTS], *[out_d[n] for n in WEIGHTS],
            *[out_m[n] for n in WEIGHTS], *[out_v[n] for n in WEIGHTS])
```

```python
import functools
import math

import numpy as np
import jax
import jax.numpy as jnp
from jax import lax
from jax.experimental import pallas as pl
from jax.experimental.pallas import tpu as pltpu

f32 = jnp.float32
bf16 = jnp.bfloat16
MESH = pl.DeviceIdType.MESH
AXES = ("x", "y", "c")

EPS = 1e-6
SSD_HEADS, SSD_P, SSD_N, SSD_G, SSD_R, SSD_L = 32, 64, 128, 4, 8, 128
SSD_DI = SSD_HEADS * SSD_P
CONV_K = 4
CONV_DIM = SSD_DI + 2 * SSD_G * SSD_N
S5_W, S5_G, S5_I, S5_P = 1024, 64, 16, 64
S5_S = S5_G * S5_P
N_ADA = 9
ADAM_LR, ADAM_B1, ADAM_B2, ADAM_EPS, ADAM_WD, ADAM_STEP = 0.001, 0.9, 0.999, 1e-08, 0.01, 10

LANE = 128
SUBLANE = 8
VMEM_LIMIT = 56 << 20
MM_VMEM_BUDGET = 40 << 20
RW_VMEM_BUDGET = 36 << 20

WEIGHTS = ['w_ada', 'b_ada', 'norm_ffn1', 'w_ffn1_in', 'w_ffn1_out', 'norm_mix', 'w_in', 'conv_w', 'conv_b', 'dt_bias',
           'a_log', 'd_ssd', 'ssd_norm_w', 'w_a_proj', 's5_lambda_re', 's5_lambda_im', 's5_b_re', 's5_b_im', 's5_c_re',
           's5_c_im', 's5_d', 's5_log_dt', 'w_b_glu', 'w_out', 'norm_ffn2', 'w_ffn2_in', 'w_ffn2_out', 'norm_final']
BIG = ['w_ffn1_in', 'w_ffn1_out', 'w_in', 'w_a_proj', 'w_b_glu', 'w_out', 'w_ffn2_in', 'w_ffn2_out']
COL_SHARDED = ('w_ffn1_in', 'w_in', 'w_b_glu', 'w_ffn2_in')
SMALL = [n for n in WEIGHTS if n not in BIG and n != 'w_ada']


def _cp(sem=None):
    return pltpu.CompilerParams(dimension_semantics=sem, vmem_limit_bytes=VMEM_LIMIT)


def _tile(dim, target, align=LANE):
    if dim <= target:
        return dim
    t = (target // align) * align
    while t >= align:
        if dim % t == 0:
            return t
        t -= align
    return dim


def _round_up(n, m):
    return (n + m - 1) // m * m


def _mm(a, b, mode, *, out_dtype, name, a_win=None, b_win=None, add=None, ride=None):
    a0, aw = a_win or (0, a.shape[1])
    b0, bw = b_win or (0, b.shape[1])
    if mode == 'nn':
        m, k, n = a.shape[0], aw, bw
        assert b.shape[0] == k
    elif mode == 'nt':
        m, k, n = a.shape[0], aw, b.shape[0]
        assert bw == k
    else:
        k, m, n = a.shape[0], aw, bw
        assert b.shape[0] == k
    osz = jnp.dtype(out_dtype).itemsize
    tm, tn, tk = 1024, 1152, 3456
    while True:
        bm = _tile(math.gcd(m, a0) if (mode == 'tn' and a0) else m, tm)
        bn = _tile(math.gcd(n, b0) if (mode != 'nt' and b0) else n, tn)
        kk = k
        if mode != 'tn' and a0:
            kk = math.gcd(kk, a0)
        if mode == 'nt' and b0:
            kk = math.gcd(kk, b0)
        bk = _tile(kk, tk)
        need = 2 * (bm * bk * a.dtype.itemsize + bk * bn * b.dtype.itemsize + bm * bn * osz) + bm * bn * 4
        if add is not None:
            need += 2 * bm * bn * add.dtype.itemsize
        if need <= MM_VMEM_BUDGET or (tm <= 256 and tn <= 256 and tk <= 512):
            break
        if tk > 1024:
            tk //= 2
        elif tm >= tn:
            tm //= 2
        else:
            tn //= 2
    nk = k // bk
    assert m % bm == 0 and n % bn == 0 and k % bk == 0, (name, m, n, k, bm, bn, bk)
    if mode == 'nn':
        ao, bo = a0 // bk, b0 // bn
        a_blk, a_map = (bm, bk), lambda i, j, q: (i, q + ao)
        b_blk, b_map = (bk, bn), lambda i, j, q: (q, j + bo)
    elif mode == 'nt':
        ao, bo = a0 // bk, b0 // bk
        a_blk, a_map = (bm, bk), lambda i, j, q: (i, q + ao)
        b_blk, b_map = (bn, bk), lambda i, j, q: (j, q + bo)
    else:
        ao, bo = a0 // bm, b0 // bn
        a_blk, a_map = (bk, bm), lambda i, j, q: (q, i + ao)
        b_blk, b_map = (bk, bn), lambda i, j, q: (q, j + bo)
    return _mm_core(a, b, mode, grid=(m // bm, n // bn, nk), a_blk=a_blk, a_map=a_map, b_blk=b_blk, b_map=b_map,
                    o_blk=(bm, bn), o_map=lambda i, j, q: (i, j), out_shape=(m, n), out_dtype=out_dtype, name=name,
                    add=add, ride=ride)


def _mm_core(a, b, mode, *, grid, a_blk, a_map, b_blk, b_map, o_blk, o_map, out_shape, out_dtype, name,
             add=None, ride=None):
    dims = {'nn': (((1,), (0,)), ((), ())), 'nt': (((1,), (1,)), ((), ())), 'tn': (((0,), (0,)), ((), ()))}[mode]
    nk = grid[-1]
    has_add = add is not None
    nr = ride.n if ride is not None else 0

    def body(*refs):
        a_ref, b_ref = refs[0], refs[1]
        pos = 2
        add_ref = refs[pos] if has_add else None
        pos += int(has_add)
        r_ins = refs[pos:pos + nr]
        o_ref = refs[pos + nr]
        r_outs = refs[pos + nr + 1:pos + 2 * nr + 1]
        acc_ref = refs[pos + 2 * nr + 1]
        r_sems = refs[pos + 2 * nr + 2:]
        ids = [pl.program_id(ax) for ax in range(len(grid))]
        q = ids[-1]
        if nr:
            @pl.when(functools.reduce(lambda u, v: u & v, [i == 0 for i in ids]))
            def _():
                ride.start(r_ins, r_outs, r_sems)

        @pl.when(q == 0)
        def _():
            acc_ref[...] = jnp.zeros_like(acc_ref)

        acc_ref[...] += lax.dot_general(a_ref[...].astype(bf16), b_ref[...].astype(bf16), dims,
                                        preferred_element_type=f32)

        @pl.when(q == nk - 1)
        def _():
            r = acc_ref[...]
            if has_add:
                r = r + add_ref[...].astype(f32)
            o_ref[...] = r.astype(out_dtype)

        if nr:
            @pl.when(functools.reduce(lambda u, v: u & v, [i == g - 1 for i, g in zip(ids, grid)]))
            def _():
                ride.wait(r_ins, r_outs, r_sems)

    in_specs = [pl.BlockSpec(a_blk, a_map), pl.BlockSpec(b_blk, b_map)]
    ops = [a, b]
    if has_add:
        in_specs.append(pl.BlockSpec(o_blk, o_map))
        ops.append(add)
    out_specs = [pl.BlockSpec(o_blk, o_map)]
    out_shapes = [jax.ShapeDtypeStruct(out_shape, out_dtype)]
    scratch = [pltpu.VMEM(tuple(b for b in o_blk if b is not None), f32)]
    if nr:
        in_specs += ride.specs
        ops += ride.srcs
        out_specs += ride.specs
        out_shapes += ride.out_shape
        scratch += ride.scratch
    sem = ("arbitrary",) * len(grid) if nr else ("parallel",) * (len(grid) - 1) + ("arbitrary",)
    res = pl.pallas_call(
        body, name=name, grid=grid, in_specs=in_specs, out_specs=out_specs, out_shape=out_shapes,
        scratch_shapes=scratch, compiler_params=_cp(sem),
    )(*ops)
    return (res[0], list(res[1:])) if nr else res[0]


def _swiglu(a, b):
    return jax.nn.silu(a) * b


def _ride_parts(refs, n_in, n_out, ride):
    nr = ride.n if ride is not None else 0
    ins = refs[:n_in]
    r_ins = refs[n_in:n_in + nr]
    outs = refs[n_in + nr:n_in + nr + n_out]
    r_outs = refs[n_in + nr + n_out:n_in + 2 * nr + n_out]
    return ins, r_ins, outs, r_outs, refs[n_in + 2 * nr + n_out:]


def _call_with_ride(body_core, grid, in_specs, ops, out_specs, out_shape, name, ride, scratch=(), sem=None):
    nr = ride.n if ride is not None else 0
    n_in, n_out, n_scr = len(in_specs), len(out_specs), len(scratch)

    def body(*refs):
        ins, r_ins, outs, r_outs, rest = _ride_parts(refs, n_in, n_out, ride)
        ids = [pl.program_id(ax) for ax in range(len(grid))]
        if nr:
            @pl.when(functools.reduce(lambda u, v: u & v, [i == 0 for i in ids]))
            def _():
                ride.start(r_ins, r_outs, rest[n_scr:])
        body_core(ins, outs, *rest[:n_scr])
        if nr:
            @pl.when(functools.reduce(lambda u, v: u & v, [i == g - 1 for i, g in zip(ids, grid)]))
            def _():
                ride.wait(r_ins, r_outs, rest[n_scr:])

    res = pl.pallas_call(
        body, name=name, grid=grid, in_specs=in_specs + (ride.specs if nr else []),
        out_specs=out_specs + (ride.specs if nr else []), out_shape=out_shape + (ride.out_shape if nr else []),
        scratch_shapes=list(scratch) + (ride.scratch if nr else []),
        compiler_params=_cp(("arbitrary",) * len(grid) if nr else (sem or ("parallel",) * len(grid))),
    )(*ops, *(ride.srcs if nr else []))
    return list(res[:n_out]), list(res[n_out:])


def _ffn_in(h, w, name, ride=None):
    t, d = h.shape
    cp = w.shape[2]
    bm, bn = _tile(t, 512), _tile(cp, 1408)
    nbs = cp // bn
    nn = (((1,), (0,)), ((), ()))

    def core(ins, outs):
        h_ref, wa_ref, wb_ref = ins
        ab_ref, act_ref = outs
        hv = h_ref[...].astype(bf16)
        a = lax.dot_general(hv, wa_ref[...].astype(bf16), nn, preferred_element_type=f32)
        b = lax.dot_general(hv, wb_ref[...].astype(bf16), nn, preferred_element_type=f32)
        ab_ref[0] = a.astype(bf16)
        ab_ref[1] = b.astype(bf16)
        act_ref[...] = _swiglu(a, b).astype(bf16)

    (ab, act), landed = _call_with_ride(
        core, (2 * nbs, t // bm),
        [pl.BlockSpec((bm, d), lambda j, i: (i, 0)),
         pl.BlockSpec((None, d, bn), lambda j, i: (j // nbs, 0, j % nbs)),
         pl.BlockSpec((None, d, bn), lambda j, i: (2 + j // nbs, 0, j % nbs))], [h, w, w],
        [pl.BlockSpec((2, bm, bn), lambda j, i: (0, i, j)), pl.BlockSpec((bm, bn), lambda j, i: (i, j))],
        [jax.ShapeDtypeStruct((2, t, 2 * cp), bf16), jax.ShapeDtypeStruct((t, 2 * cp), bf16)], name, ride)
    return ab, act, landed


def _ffn_out_dx(df, wo, ab, name, ride=None):
    t, d = df.shape
    ffp = wo.shape[0]
    bm, bn = _tile(t, 1024), _tile(ffp, 512)
    nt = (((1,), (1,)), ((), ()))

    def core(ins, outs):
        df_ref, wo_ref, ab_ref = ins
        dact = lax.dot_general(df_ref[...].astype(bf16), wo_ref[...].astype(bf16), nt, preferred_element_type=f32)
        _, vjp = jax.vjp(_swiglu, ab_ref[0].astype(f32), ab_ref[1].astype(f32))
        da, db = vjp(dact)
        outs[0][0] = da.astype(bf16)
        outs[0][1] = db.astype(bf16)

    (dab,), landed = _call_with_ride(
        core, (t // bm, ffp // bn),
        [pl.BlockSpec((bm, d), lambda i, j: (i, 0)), pl.BlockSpec((bn, d), lambda i, j: (j, 0)),
         pl.BlockSpec((2, bm, bn), lambda i, j: (0, i, j))], [df, wo, ab],
        [pl.BlockSpec((2, bm, bn), lambda i, j: (0, i, j))], [jax.ShapeDtypeStruct((2, t, ffp), bf16)], name, ride)
    return dab, landed


def _ffn_in_dx(dab, w, name, ride=None):
    _, t, cp2 = dab.shape
    d, cp = w.shape[1], w.shape[2]
    bm, bn, bk = _tile(t, 1024), _tile(d, 1024), _tile(cp, 2816)
    nkb = cp // bk
    return _mm_core(dab, w, 'nt', grid=(t // bm, d // bn, 4 * nkb),
                    a_blk=(None, bm, bk), a_map=lambda i, j, q: (q // (2 * nkb), i, q % (2 * nkb)),
                    b_blk=(None, bn, bk), b_map=lambda i, j, q: (q // nkb, j, q % nkb),
                    o_blk=(bm, bn), o_map=lambda i, j, q: (i, j),
                    out_shape=(t, d), out_dtype=bf16, name=name, ride=ride)


def _ffn_in_dw(h, dab, name):
    _, t, cp2 = dab.shape
    cp = cp2 // 2
    d = h.shape[1]
    bm, bn, bk = _tile(d, 1024), _tile(cp, 1408), _tile(t, 2048)
    nbs = cp // bn
    return _mm_core(h, dab, 'tn', grid=(d // bm, 4 * nbs, t // bk),
                    a_blk=(bk, bm), a_map=lambda i, j, q: (q, i),
                    b_blk=(None, bk, bn), b_map=lambda i, j, q: (j // (2 * nbs), q, j % (2 * nbs)),
                    o_blk=(None, bm, bn), o_map=lambda i, j, q: (j // nbs, i, j % nbs),
                    out_shape=(4, d, cp), out_dtype=bf16, name=name)


def _piece_blocks(pieces):
    bk = min(1024, functools.reduce(math.gcd, [p.shape[1] for p in pieces]))
    starts, n = [], 0
    for p in pieces:
        starts.append(n)
        n += p.shape[1] // bk
    return bk, starts, n


def _in_proj_dx(pieces, tail, tail_col, w, name, ride=None):
    t, d = pieces[0].shape[0], w.shape[0]
    bk, starts, nq = _piece_blocks(pieces)
    assert tail_col == nq * bk and tail.shape[1] == LANE
    bm, bn = _tile(t, 1024), _tile(d, 1024)
    npc = len(pieces)

    def core(ins, outs, acc_ref):
        a_refs, tail_ref, b_ref, bt_ref = ins[:npc], ins[npc], ins[npc + 1], ins[npc + 2]
        q = pl.program_id(2)

        @pl.when(q == 0)
        def _():
            acc_ref[...] = jnp.zeros_like(acc_ref)

        for a_ref, s0, p in zip(a_refs, starts, pieces):
            @pl.when((q >= s0) & (q < s0 + p.shape[1] // bk))
            def _(a_ref=a_ref):
                acc_ref[...] += lax.dot_general(a_ref[...].astype(bf16), b_ref[...], _NT, preferred_element_type=f32)

        @pl.when(q == nq)
        def _():
            r = acc_ref[...] + lax.dot_general(tail_ref[...].astype(bf16), bt_ref[...], _NT,
                                               preferred_element_type=f32)
            outs[0][...] = r.astype(bf16)

    def a_spec(s0, p):
        last = p.shape[1] // bk - 1
        return pl.BlockSpec((bm, bk), lambda i, j, q: (i, jnp.clip(q - s0, 0, last)))

    in_specs = [a_spec(s0, p) for s0, p in zip(starts, pieces)] + [
        pl.BlockSpec((bm, LANE), lambda i, j, q: (i, 0)),
        pl.BlockSpec((bn, bk), lambda i, j, q: (j, jnp.minimum(q, nq - 1))),
        pl.BlockSpec((bn, LANE), lambda i, j, q: (j, tail_col // LANE))]
    (dh,), landed = _call_with_ride(
        core, (t // bm, d // bn, nq + 1), in_specs, list(pieces) + [tail, w, w],
        [pl.BlockSpec((bm, bn), lambda i, j, q: (i, j))], [jax.ShapeDtypeStruct((t, d), bf16)], name, ride,
        scratch=[pltpu.VMEM((bm, bn), f32)], sem=("parallel", "parallel", "arbitrary"))
    return dh, landed


def _win(r):
    return r if isinstance(r, tuple) else (r, 0, r.shape[1])


def _row_tile(t, widths):
    per_row = 48 * max(widths)
    tm = 512
    while tm > SUBLANE and tm * per_row > RW_VMEM_BUDGET:
        tm //= 2
    return min(tm, t)


def _row_spec(r, tm):
    arr, c0, w = _win(r)
    assert c0 % w == 0, (c0, w)
    cb = c0 // w
    return pl.BlockSpec((tm, w), lambda i: (i, cb))


def _full_spec(p):
    nd = p.ndim
    return pl.BlockSpec(p.shape, lambda i: (0,) * nd)


def _rw(f, rows, params, outs, *, name, accs=(), tm=None):
    t = _win(rows[0])[0].shape[0]
    tm = tm or _row_tile(t, [_win(r)[2] for r in rows] + [w for w, _ in outs])
    nr, npar, no, na = len(rows), len(params), len(outs), len(accs)

    def body(*refs):
        vals = [r[...] for r in refs[:nr + npar]]
        res = f(*vals)
        res = res if isinstance(res, (tuple, list)) else (res,)
        for o_ref, v in zip(refs[nr + npar:nr + npar + no], res[:no]):
            o_ref[...] = v.astype(o_ref.dtype)
        if na:
            first = pl.program_id(0) == 0
            for a_ref, v in zip(refs[nr + npar + no:], res[no:]):
                @pl.when(first)
                def _(a_ref=a_ref):
                    a_ref[...] = jnp.zeros_like(a_ref)
                a_ref[...] += v

    out_shape = [jax.ShapeDtypeStruct((t, w), d) for w, d in outs] + [jax.ShapeDtypeStruct(s, f32) for s in accs]
    out_specs = [pl.BlockSpec((tm, w), lambda i: (i, 0)) for w, _ in outs] + \
                [pl.BlockSpec(s, lambda i: (0, 0)) for s in accs]
    return pl.pallas_call(
        body, name=name, grid=(t // tm,),
        in_specs=[_row_spec(r, tm) for r in rows] + [_full_spec(p) for p in params],
        out_specs=out_specs, out_shape=out_shape,
        compiler_params=_cp(("arbitrary",)),
    )(*[_win(r)[0] for r in rows], *params)


def _rw_vjp(f, rows, params, cots, *, row_grads, param_grads, name, tm=None, total_of=None):
    t = _win(rows[0])[0].shape[0]
    cot_rows = [c for c in cots if c is not None]
    tm = tm or _row_tile(t, [_win(r)[2] for r in rows] + [_win(c)[2] for c in cot_rows])
    nr, npar, ncot = len(rows), len(params), len(cot_rows)
    d_rows = [i for i, d in enumerate(row_grads) if d is not None]
    d_pars = [i for i, d in enumerate(param_grads) if d]

    def body(*refs):
        rv = [r[...] for r in refs[:nr]]
        pv = [r[...] for r in refs[nr:nr + npar]]
        cv = [r[...] for r in refs[nr + npar:nr + npar + ncot]]
        outs_r = refs[nr + npar + ncot:nr + npar + ncot + len(d_rows)]
        outs_p = refs[nr + npar + ncot + len(d_rows):nr + npar + ncot + len(d_rows) + len(d_pars)]

        def g(*diff):
            rr, pp = list(rv), list(pv)
            for i, v in zip(d_rows, diff[:len(d_rows)]):
                rr[i] = v
            for i, v in zip(d_pars, diff[len(d_rows):]):
                pp[i] = v
            res = f(*rr, *pp)
            return tuple(res) if isinstance(res, (tuple, list)) else (res,)

        prim, vjp = jax.vjp(g, *[rv[i] for i in d_rows], *[pv[i] for i in d_pars])
        it = iter(cv)
        cts = tuple(next(it).astype(o.dtype) if c is not None else jnp.zeros_like(o) for o, c in zip(prim, cots))
        grads = vjp(cts)
        for o_ref, v in zip(outs_r, grads[:len(d_rows)]):
            o_ref[...] = v.astype(o_ref.dtype)
        first = pl.program_id(0) == 0
        for o_ref, v in zip(outs_p, grads[len(d_rows):]):
            @pl.when(first)
            def _(o_ref=o_ref):
                o_ref[...] = jnp.zeros_like(o_ref)
            o_ref[...] += v.astype(f32)
        if total_of is not None:
            tot_ref = refs[-1]

            @pl.when(first)
            def _():
                tot_ref[...] = jnp.zeros_like(tot_ref)
            tot_ref[...] += jnp.broadcast_to(jnp.sum(prim[total_of].astype(f32)), tot_ref.shape)

    out_shape = [jax.ShapeDtypeStruct((t, _win(rows[i])[2]), row_grads[i]) for i in d_rows] + \
                [jax.ShapeDtypeStruct(params[i].shape, f32) for i in d_pars]
    out_specs = [pl.BlockSpec((tm, _win(rows[i])[2]), lambda i_: (i_, 0)) for i in d_rows] + \
                [_full_spec(params[i]) for i in d_pars]
    if total_of is not None:
        out_shape.append(jax.ShapeDtypeStruct((1, LANE), f32))
        out_specs.append(pl.BlockSpec((1, LANE), lambda i_: (0, 0)))
    return pl.pallas_call(
        body, name=name, grid=(t // tm,),
        in_specs=[_row_spec(r, tm) for r in rows] + [_full_spec(p) for p in params] + [_row_spec(c, tm) for c in cot_rows],
        out_specs=out_specs, out_shape=out_shape,
        compiler_params=_cp(("arbitrary",)),
    )(*[_win(r)[0] for r in rows], *params, *[_win(c)[0] for c in cot_rows])


def _rms(x, g):
    return x * lax.rsqrt(jnp.mean(x * x, axis=-1, keepdims=True) + EPS) * g


def _f_mod(x, nw, sh, sc):
    return (_rms(x, nw) * (1.0 + sc) + sh).astype(bf16)


def _f_mod_keep(x, nw, sh, sc):
    return _f_mod(x, nw, sh, sc), x


def _f_res_mod(coef, x, o, g, nw, sh, sc):
    x1 = x + coef * g * o.astype(f32)
    return x1, _f_mod(x1, nw, sh, sc)


def _times01(x, e):
    hi = x.astype(bf16)
    r1 = x - hi.astype(f32)
    mid = r1.astype(bf16)
    lo = (r1 - mid.astype(f32)).astype(bf16)
    return (jnp.dot(hi, e, preferred_element_type=f32) + jnp.dot(mid, e, preferred_element_type=f32) +
            jnp.dot(lo, e, preferred_element_type=f32))


@jax.custom_vjp
def _spread_heads(x, e, et):
    return _times01(x, e)


_spread_heads.defvjp(lambda x, e, et: (_times01(x, e), (e, et)),
                     lambda res, g: (_times01(g, res[1]), None, None))


def _f_ssd_pre(pre, dtraw, bias, e, et):
    xc = jax.nn.silu(pre)
    dtx = _spread_heads(jax.nn.softplus(dtraw + bias), e, et)
    return xc[:, :SSD_DI], xc[:, SSD_DI:SSD_DI + SSD_G * SSD_N], xc[:, SSD_DI + SSD_G * SSD_N:], dtx


def _f_ssd_post(y, z, nw):
    yz = y * jax.nn.silu(z)
    w = SSD_DI // SSD_G
    parts = []
    for g in range(SSD_G):
        s = yz[:, g * w:(g + 1) * w]
        parts.append(s * lax.rsqrt(jnp.mean(s * s, axis=-1, keepdims=True) + EPS))
    return (jnp.concatenate(parts, axis=1) * nw).astype(bf16)


def _f_s5_post(yb, u, d):
    return jax.nn.gelu(yb + d * u).astype(bf16)


def _f_merge(pa, glu, ga, gb):
    d = pa.shape[1]
    glu = glu.astype(f32)
    pb = glu[:, :d] * jax.nn.sigmoid(glu[:, d:])
    return (jax.nn.sigmoid(ga) * pa.astype(f32) + jax.nn.sigmoid(gb) * pb).astype(bf16)


def _f_final(x2, o, tgt, g, nw):
    x3 = x2 + 0.5 * g * o.astype(f32)
    y = _rms(x3, nw)
    return 0.5 * jnp.mean(jnp.square(y - tgt), axis=-1, keepdims=True)


def _f_s5_prep(lr, li, ldt, br, bi):
    dt = jnp.exp(ldt)
    lr = jnp.minimum(lr, -1e-4)
    mag = jnp.exp(lr * dt)
    ar = mag * jnp.cos(li * dt)
    ai = mag * jnp.sin(li * dt)
    den = lr * lr + li * li
    nr = ar - 1.0
    kr = (nr * lr + ai * li) / den
    ki = (ai * lr - nr * li) / den
    return ar, ai, kr * br - ki * bi, kr * bi + ki * br


def _shift_down(cur, halo8, j):
    if j == 0:
        return cur
    rolled = pltpu.roll(cur, j, 0)
    row8 = lax.broadcasted_iota(jnp.int32, halo8.shape, 0)
    top = jnp.where(row8 < j, pltpu.roll(halo8, j, 0), rolled[:SUBLANE])
    return jnp.concatenate([top, rolled[SUBLANE:]], axis=0)


def _shift_up(cur, halo8, j):
    if j == 0:
        return cur
    n = cur.shape[0]
    rolled = pltpu.roll(cur, n - j, 0)
    row8 = lax.broadcasted_iota(jnp.int32, halo8.shape, 0)
    bot = jnp.where(row8 >= SUBLANE - j, pltpu.roll(halo8, SUBLANE - j, 0), rolled[n - SUBLANE:])
    return jnp.concatenate([rolled[:n - SUBLANE], bot], axis=0)


def _conv_fwd(proj, c0, w8, b, name):
    t = proj.shape[0]
    cw = 1024
    tm = min(512, t)
    cb0 = c0 // cw
    r8 = tm // SUBLANE

    def body(x_ref, h_ref, w_ref, b_ref, o_ref):
        i = pl.program_id(1)
        x = x_ref[...]
        halo = jnp.where(i > 0, h_ref[...], 0.0)
        acc = b_ref[...] + w_ref[CONV_K - 1:CONV_K, :] * x
        for j in range(1, CONV_K):
            acc = acc + w_ref[CONV_K - 1 - j:CONV_K - j, :] * _shift_down(x, halo, j)
        o_ref[...] = acc

    return pl.pallas_call(
        body, name=name, grid=(CONV_DIM // cw, t // tm),
        in_specs=[pl.BlockSpec((tm, cw), lambda c, i: (i, cb0 + c)),
                  pl.BlockSpec((SUBLANE, cw), lambda c, i: (jnp.maximum(i * r8 - 1, 0), cb0 + c)),
                  pl.BlockSpec((SUBLANE, cw), lambda c, i: (0, c)),
                  pl.BlockSpec((1, cw), lambda c, i: (0, c))],
        out_specs=pl.BlockSpec((tm, cw), lambda c, i: (i, c)),
        out_shape=jax.ShapeDtypeStruct((t, CONV_DIM), f32),
        compiler_params=_cp(("parallel", "arbitrary")),
    )(proj, proj, w8, b)


def _conv_bwd(dpre, proj, c0, w8, name):
    t = proj.shape[0]
    cw = 1024
    tm = min(512, t)
    cb0 = c0 // cw
    r8 = tm // SUBLANE
    nb = t // tm

    def body(d_ref, dn_ref, x_ref, xh_ref, w_ref, dx_ref, dw_ref, db_ref):
        i = pl.program_id(1)
        d = d_ref[...]
        dn = jnp.where(i < nb - 1, dn_ref[...], 0.0)
        x = x_ref[...]
        xh = jnp.where(i > 0, xh_ref[...], 0.0)

        @pl.when(i == 0)
        def _():
            dw_ref[...] = jnp.zeros_like(dw_ref)
            db_ref[...] = jnp.zeros_like(db_ref)

        dx = w_ref[CONV_K - 1:CONV_K, :] * d
        rows = [jnp.sum(d * x, axis=0, keepdims=True)]
        for j in range(1, CONV_K):
            dx = dx + w_ref[CONV_K - 1 - j:CONV_K - j, :] * _shift_up(d, dn, j)
            rows.append(jnp.sum(d * _shift_down(x, xh, j), axis=0, keepdims=True))
        dx_ref[...] = dx.astype(dx_ref.dtype)
        dw = jnp.concatenate([rows[CONV_K - 1 - k] for k in range(CONV_K)] +
                             [jnp.zeros((SUBLANE - CONV_K, cw), f32)], axis=0)
        dw_ref[...] += dw
        db_ref[...] += jnp.sum(d, axis=0, keepdims=True)

    return pl.pallas_call(
        body, name=name, grid=(CONV_DIM // cw, nb),
        in_specs=[pl.BlockSpec((tm, cw), lambda c, i: (i, c)),
                  pl.BlockSpec((SUBLANE, cw), lambda c, i: (jnp.minimum((i + 1) * r8, nb * r8 - 1), c)),
                  pl.BlockSpec((tm, cw), lambda c, i: (i, cb0 + c)),
                  pl.BlockSpec((SUBLANE, cw), lambda c, i: (jnp.maximum(i * r8 - 1, 0), cb0 + c)),
                  pl.BlockSpec((SUBLANE, cw), lambda c, i: (0, c))],
        out_specs=[pl.BlockSpec((tm, cw), lambda c, i: (i, c)),
                   pl.BlockSpec((SUBLANE, cw), lambda c, i: (0, c)),
                   pl.BlockSpec((1, cw), lambda c, i: (0, c))],
        out_shape=[jax.ShapeDtypeStruct((t, CONV_DIM), bf16), jax.ShapeDtypeStruct((SUBLANE, CONV_DIM), f32),
                   jax.ShapeDtypeStruct((1, CONV_DIM), f32)],
        compiler_params=_cp(("parallel", "arbitrary")),
    )(dpre, dpre, proj, proj, w8)


def _cumsum_rows_impl(x):
    n = x.shape[0]
    row = lax.broadcasted_iota(jnp.int32, x.shape, 0)
    s = 1
    while s < n:
        x = x + jnp.where(row >= s, pltpu.roll(x, s, 0), 0.0)
        s *= 2
    return x


@jax.custom_vjp
def _cumsum_rows(x):
    return _cumsum_rows_impl(x)


def _cumsum_rows_bwd(_, g):
    c = _cumsum_rows_impl(g)
    return (c[c.shape[0] - 1:, :] - c + g,)


_cumsum_rows.defvjp(lambda x: (_cumsum_rows_impl(x), None), _cumsum_rows_bwd)


@jax.custom_vjp
def _swap_halves(t):
    return pltpu.roll(t, LANE // 2, 1)


_swap_halves.defvjp(lambda t: (pltpu.roll(t, LANE // 2, 1), None), lambda _, g: (pltpu.roll(g, LANE // 2, 1),))


def _ssd_chunk(xs, bm, cm, dtx, ax, dskx, ht):
    n = SSD_L
    assert n == LANE and SSD_P * 2 == LANE
    row = lax.broadcasted_iota(jnp.int32, (n, n), 0)
    col = lax.broadcasted_iota(jnp.int32, (n, n), 1)
    causal = row >= col
    lo = col < SSD_P
    cs = _cumsum_rows(dtx * ax)
    xdt = xs * dtx
    last = cs[n - 1:n, :]
    cb = lax.dot_general(cm.astype(bf16), bm.astype(bf16), (((1,), (1,)), ((), ())), preferred_element_type=f32)
    y_off = jnp.dot(cm.astype(bf16), ht.astype(bf16), preferred_element_type=f32) * jnp.exp(cs)
    st = lax.dot_general(bm.astype(bf16), (xdt * jnp.exp(last - cs)).astype(bf16), (((0,), (0,)), ((), ())),
                         preferred_element_type=f32)
    ht_new = jnp.exp(last) * ht + st
    ys = []
    for q in range(SSD_R // 2):
        tq = cs[:, q * LANE:(q + 1) * LANE]
        sw = _swap_halves(tq)
        tqt = tq.T
        xq = xdt[:, q * LANE:(q + 1) * LANE].astype(bf16)
        pair = []
        for c_col, r_row in ((jnp.where(lo, tq, sw), tqt[0:1, :]), (jnp.where(lo, sw, tq), tqt[SSD_P:SSD_P + 1, :])):
            decay = jnp.exp(jnp.where(causal, c_col - r_row, -1e30))
            pair.append(jnp.dot((cb * decay).astype(bf16), xq, preferred_element_type=f32))
        ys.append(jnp.where(lo, pair[0], pair[1]))
    return jnp.concatenate(ys, axis=1) + y_off + dskx * xs, ht_new


SSD_GB = 1


def _ssd_specs(nc, rev):
    ch = (lambda c: nc - 1 - c) if rev else (lambda c: c)
    gw = SSD_GB * SSD_R * SSD_P
    return [pl.BlockSpec((SSD_L, gw), lambda g, c: (ch(c), g)),
            pl.BlockSpec((SSD_L, SSD_GB * SSD_N), lambda g, c: (ch(c), g)),
            pl.BlockSpec((SSD_L, SSD_GB * SSD_N), lambda g, c: (ch(c), g)),
            pl.BlockSpec((SSD_L, gw), lambda g, c: (ch(c), g)),
            pl.BlockSpec((1, gw), lambda g, c: (0, g)),
            pl.BlockSpec((1, gw), lambda g, c: (0, g))]


def _ssd_group(refs, q):
    gw = SSD_R * SSD_P
    xs_ref, bm_ref, cm_ref, dt_ref, a_ref, dsk_ref = refs
    ln = slice(q * LANE, (q + 1) * LANE)
    wd = slice(q * gw, (q + 1) * gw)
    return (xs_ref[:, wd], bm_ref[:, ln], cm_ref[:, ln], dt_ref[:, wd], a_ref[:, wd], dsk_ref[:, wd])


def _ssd_fwd(xs, bm, cm, dt4, a4, dsk4, name, ride=None):
    t = xs.shape[0]
    nc = t // SSD_L
    gw = SSD_R * SSD_P

    nr = ride.n if ride is not None else 0
    ng = SSD_G // SSD_GB

    def body(*refs):
        xs_ref, bm_ref, cm_ref, dt_ref, a_ref, dsk_ref = refs[:6]
        r_ins = refs[6:6 + nr]
        y_ref, hs_ref = refs[6 + nr:8 + nr]
        r_outs = refs[8 + nr:8 + 2 * nr]
        h_ref = refs[8 + 2 * nr]
        r_sems = refs[9 + 2 * nr:]
        g, c = pl.program_id(0), pl.program_id(1)
        if nr:
            @pl.when((g == 0) & (c == 0))
            def _():
                ride.start(r_ins, r_outs, r_sems)

        @pl.when(c == 0)
        def _():
            h_ref[...] = jnp.zeros_like(h_ref)

        hs_ref[...] = h_ref[...]
        grp = (xs_ref, bm_ref, cm_ref, dt_ref, a_ref, dsk_ref)
        ops = [_ssd_group(grp, q) + (h_ref[:, q * gw:(q + 1) * gw],) for q in range(SSD_GB)]
        res = [_ssd_chunk(*o) for o in ops]
        for q, (y, hn) in enumerate(res):
            y_ref[:, q * gw:(q + 1) * gw] = y
            h_ref[:, q * gw:(q + 1) * gw] = hn

        if nr:
            @pl.when((g == ng - 1) & (c == nc - 1))
            def _():
                ride.wait(r_ins, r_outs, r_sems)

    res = pl.pallas_call(
        body, name=name, grid=(ng, nc), in_specs=_ssd_specs(nc, False) + (ride.specs if nr else []),
        out_specs=[pl.BlockSpec((SSD_L, SSD_GB * gw), lambda g, c: (c, g)),
                   pl.BlockSpec((None, None, SSD_N, SSD_GB * gw), lambda g, c: (g, c, 0, 0))] +
                  (ride.specs if nr else []),
        out_shape=[jax.ShapeDtypeStruct((t, SSD_DI), f32),
                   jax.ShapeDtypeStruct((ng, nc, SSD_N, SSD_GB * gw), f32)] + (ride.out_shape if nr else []),
        scratch_shapes=[pltpu.VMEM((SSD_N, SSD_GB * gw), f32)] + (ride.scratch if nr else []),
        compiler_params=_cp(("arbitrary", "arbitrary")),
    )(xs, bm, cm, dt4, a4, dsk4, *(ride.srcs if nr else []))
    return res[0], res[1], list(res[2:])


def _ssd_bwd(xs, bm, cm, dt4, a4, dsk4, hs, dy, name, ride=None):
    t = xs.shape[0]
    nc = t // SSD_L
    gw = SSD_R * SSD_P
    rc = lambda c: nc - 1 - c
    nr = ride.n if ride is not None else 0
    ng = SSD_G // SSD_GB

    def body(*refs):
        xs_ref, bm_ref, cm_ref, dt_ref, a_ref, dsk_ref, hs_ref, dy_ref = refs[:8]
        r_ins = refs[8:8 + nr]
        dxs_ref, dbm_ref, dcm_ref, ddt_ref, da_ref, ddsk_ref = refs[8 + nr:14 + nr]
        r_outs = refs[14 + nr:14 + 2 * nr]
        dh_ref = refs[14 + 2 * nr]
        r_sems = refs[15 + 2 * nr:]
        if nr:
            @pl.when((pl.program_id(0) == 0) & (pl.program_id(1) == 0))
            def _():
                ride.start(r_ins, r_outs, r_sems)

        @pl.when(pl.program_id(1) == 0)
        def _():
            dh_ref[...] = jnp.zeros_like(dh_ref)
            da_ref[...] = jnp.zeros_like(da_ref)
            ddsk_ref[...] = jnp.zeros_like(ddsk_ref)

        grp = (xs_ref, bm_ref, cm_ref, dt_ref, a_ref, dsk_ref)
        ops = [_ssd_group(grp, q) + (hs_ref[:, q * gw:(q + 1) * gw],) for q in range(SSD_GB)]
        cts = [(dy_ref[:, q * gw:(q + 1) * gw], dh_ref[:, q * gw:(q + 1) * gw]) for q in range(SSD_GB)]
        grads = [jax.vjp(_ssd_chunk, *o)[1](ct) for o, ct in zip(ops, cts)]
        for q, (dxs, dbm, dcm, ddt, da, ddsk, dh) in enumerate(grads):
            wd = slice(q * gw, (q + 1) * gw)
            ln = slice(q * LANE, (q + 1) * LANE)
            dxs_ref[:, wd] = dxs
            dbm_ref[:, ln] = dbm
            dcm_ref[:, ln] = dcm
            ddt_ref[:, wd] = ddt
            da_ref[:, wd] += da
            ddsk_ref[:, wd] += ddsk
            dh_ref[:, wd] = dh

        if nr:
            @pl.when((pl.program_id(0) == ng - 1) & (pl.program_id(1) == nc - 1))
            def _():
                ride.wait(r_ins, r_outs, r_sems)

    res = pl.pallas_call(
        body, name=name, grid=(ng, nc),
        in_specs=_ssd_specs(nc, True) + [
            pl.BlockSpec((None, None, SSD_N, SSD_GB * gw), lambda g, c: (g, rc(c), 0, 0)),
            pl.BlockSpec((SSD_L, SSD_GB * gw), lambda g, c: (rc(c), g))] + (ride.specs if nr else []),
        out_specs=[pl.BlockSpec((SSD_L, SSD_GB * gw), lambda g, c: (rc(c), g)),
                   pl.BlockSpec((SSD_L, SSD_GB * SSD_N), lambda g, c: (rc(c), g)),
                   pl.BlockSpec((SSD_L, SSD_GB * SSD_N), lambda g, c: (rc(c), g)),
                   pl.BlockSpec((SSD_L, SSD_GB * gw), lambda g, c: (rc(c), g)),
                   pl.BlockSpec((1, SSD_GB * gw), lambda g, c: (0, g)),
                   pl.BlockSpec((1, SSD_GB * gw), lambda g, c: (0, g))] + (ride.specs if nr else []),
        out_shape=[jax.ShapeDtypeStruct((t, SSD_DI), f32), jax.ShapeDtypeStruct((t, SSD_G * SSD_N), f32),
                   jax.ShapeDtypeStruct((t, SSD_G * SSD_N), f32), jax.ShapeDtypeStruct((t, SSD_DI), f32),
                   jax.ShapeDtypeStruct((1, SSD_DI), f32), jax.ShapeDtypeStruct((1, SSD_DI), f32)] +
                  (ride.out_shape if nr else []),
        scratch_shapes=[pltpu.VMEM((SSD_N, SSD_GB * gw), f32)] + (ride.scratch if nr else []),
        compiler_params=_cp(("arbitrary", "arbitrary")),
    )(xs, bm, cm, dt4, a4, dsk4, hs, dy, *(ride.srcs if nr else []))
    return list(res[:6]), list(res[6:])


S5_CH = 1024


S5_NB = 8
S5_UB = 128
S5_SB = 512
_NT = (((1,), (1,)), ((), ()))
_TN = (((0,), (0,)), ((), ()))


def _s5_fwd(proj, u0, bd_c, c_c, ar, ai, name):
    t = proj.shape[0]
    tb = min(128, t)
    ub = u0 // S5_W

    def body(u_ref, bd_ref, cc_ref, ar_ref, ai_ref, s_ref, yb_ref, bu_ref, carry):
        @pl.when(pl.program_id(0) == 0)
        def _():
            carry[...] = jnp.zeros_like(carry)

        u = u_ref[...].astype(bf16)
        for j in range(S5_NB):
            uj = u[:, j * S5_UB:(j + 1) * S5_UB]
            for half in range(2):
                bu_ref[:, half * S5_S + j * S5_SB:half * S5_S + (j + 1) * S5_SB] = jnp.dot(
                    uj, bd_ref[j * S5_UB:(j + 1) * S5_UB, half * S5_SB:(half + 1) * S5_SB], preferred_element_type=f32)

        for c0 in range(0, S5_S, S5_CH):
            re = pl.ds(c0, S5_CH)
            im = pl.ds(S5_S + c0, S5_CH)
            a_r = ar_ref[:, re]
            a_i = ai_ref[:, re]

            def step(k, st, re=re, im=im, a_r=a_r, a_i=a_i):
                sr, si = st
                row = pl.ds(k, 1)
                nr = a_r * sr - a_i * si + bu_ref[row, re]
                ni = a_r * si + a_i * sr + bu_ref[row, im]
                s_ref[row, re] = nr
                s_ref[row, im] = ni
                return nr, ni

            sr, si = lax.fori_loop(0, tb, step, (carry[:, re], carry[:, im]))
            carry[:, re] = sr
            carry[:, im] = si

        for j in range(S5_NB):
            lo, hi = j * S5_SB, (j + 1) * S5_SB
            yb_ref[:, j * S5_UB:(j + 1) * S5_UB] = (
                jnp.dot(s_ref[:, lo:hi].astype(bf16), cc_ref[lo:hi, :], preferred_element_type=f32) +
                jnp.dot(s_ref[:, S5_S + lo:S5_S + hi].astype(bf16), cc_ref[S5_S + lo:S5_S + hi, :],
                        preferred_element_type=f32))

    return pl.pallas_call(
        body, name=name, grid=(t // tb,),
        in_specs=[pl.BlockSpec((tb, S5_W), lambda i: (i, ub)), _full_spec(bd_c), _full_spec(c_c),
                  pl.BlockSpec((1, S5_S), lambda i: (0, 0)), pl.BlockSpec((1, S5_S), lambda i: (0, 0))],
        out_specs=[pl.BlockSpec((tb, 2 * S5_S), lambda i: (i, 0)), pl.BlockSpec((tb, S5_W), lambda i: (i, 0))],
        out_shape=[jax.ShapeDtypeStruct((t, 2 * S5_S), f32), jax.ShapeDtypeStruct((t, S5_W), f32)],
        scratch_shapes=[pltpu.VMEM((tb, 2 * S5_S), f32), pltpu.VMEM((1, 2 * S5_S), f32)],
        compiler_params=_cp(("arbitrary",)),
    )(proj, bd_c, c_c, ar, ai)


def _s5_bwd(dyb, s, proj, u0, bd_c, c_c, ar, ai, du_skip, name, ride=None):
    t = dyb.shape[0]
    tb = min(128, t)
    nb = t // tb
    r8 = tb // SUBLANE
    rb = lambda i: nb - 1 - i
    ub = u0 // S5_W

    def body(ins, outs, g_ref, carry):
        dyb_ref, s_ref, sh_ref, u_ref, skip_ref, bd_ref, cc_ref, ar_ref, ai_ref = ins
        du_ref, dar_ref, dai_ref, dbd_ref, dcc_ref = outs
        ds_ref = g_ref
        i = pl.program_id(0)

        @pl.when(i == 0)
        def _():
            carry[...] = jnp.zeros_like(carry)
            dar_ref[...] = jnp.zeros_like(dar_ref)
            dai_ref[...] = jnp.zeros_like(dai_ref)
            dbd_ref[...] = jnp.zeros_like(dbd_ref)
            dcc_ref[...] = jnp.zeros_like(dcc_ref)

        dyb = dyb_ref[...].astype(bf16)
        for jj in range(2 * S5_NB):
            blk = jj % S5_NB
            g_ref[:, jj * S5_SB:(jj + 1) * S5_SB] = lax.dot_general(
                dyb[:, blk * S5_UB:(blk + 1) * S5_UB], cc_ref[jj * S5_SB:(jj + 1) * S5_SB, :], _NT,
                preferred_element_type=f32)

        has_prev = (i < nb - 1).astype(f32)
        for c0 in range(0, S5_S, S5_CH):
            re = pl.ds(c0, S5_CH)
            im = pl.ds(S5_S + c0, S5_CH)
            a_r = ar_ref[:, re]
            a_i = ai_ref[:, re]

            def upd(st, row, sp_r, sp_i, re=re, im=im, a_r=a_r, a_i=a_i):
                gr, gi, acr, aci = st
                ngr = ds_ref[row, re] + a_r * gr + a_i * gi
                ngi = ds_ref[row, im] + a_r * gi - a_i * gr
                g_ref[row, re] = ngr
                g_ref[row, im] = ngi
                return ngr, ngi, acr + ngr * sp_r + ngi * sp_i, aci + ngi * sp_r - ngr * sp_i

            def step(k, st, re=re, im=im, upd=upd):
                tt = tb - 1 - k
                prev = pl.ds(tt - 1, 1)
                return upd(st, pl.ds(tt, 1), s_ref[prev, re], s_ref[prev, im])

            zero = jnp.zeros((1, S5_CH), f32)
            st = lax.fori_loop(0, tb - 1, step, (carry[:, re], carry[:, im], zero, zero))
            last = pl.ds(SUBLANE - 1, 1)
            gr, gi, acr, aci = upd(st, pl.ds(0, 1), sh_ref[last, re] * has_prev, sh_ref[last, im] * has_prev)
            carry[:, re] = gr
            carry[:, im] = gi
            dar_ref[:, re] += acr
            dai_ref[:, re] += aci

        u = u_ref[...].astype(bf16)
        for j in range(S5_NB):
            lo, hi = j * S5_SB, (j + 1) * S5_SB
            blk = slice(j * S5_UB, (j + 1) * S5_UB)
            g_re = g_ref[:, lo:hi].astype(bf16)
            g_im = g_ref[:, S5_S + lo:S5_S + hi].astype(bf16)
            du = (lax.dot_general(g_re, bd_ref[blk, :S5_SB], _NT, preferred_element_type=f32) +
                  lax.dot_general(g_im, bd_ref[blk, S5_SB:], _NT, preferred_element_type=f32) + skip_ref[:, blk])
            du_ref[:, blk] = du.astype(du_ref.dtype)
            dbd_ref[blk, :S5_SB] += lax.dot_general(u[:, blk], g_re, _TN, preferred_element_type=f32)
            dbd_ref[blk, S5_SB:] += lax.dot_general(u[:, blk], g_im, _TN, preferred_element_type=f32)
            dcc_ref[lo:hi, :] += lax.dot_general(s_ref[:, lo:hi].astype(bf16), dyb[:, blk], _TN,
                                                 preferred_element_type=f32)
            dcc_ref[S5_S + lo:S5_S + hi, :] += lax.dot_general(s_ref[:, S5_S + lo:S5_S + hi].astype(bf16), dyb[:, blk],
                                                               _TN, preferred_element_type=f32)

    row_blk = lambda w: pl.BlockSpec((tb, w), lambda i: (rb(i), 0))
    const = lambda shape: pl.BlockSpec(shape, lambda i: (0, 0))
    return _call_with_ride(
        body, (nb,),
        [row_blk(S5_W), row_blk(2 * S5_S),
         pl.BlockSpec((SUBLANE, 2 * S5_S), lambda i: (jnp.maximum(rb(i) * r8 - 1, 0), 0)),
         pl.BlockSpec((tb, S5_W), lambda i: (rb(i), ub)), row_blk(S5_W), const(bd_c.shape), const(c_c.shape),
         const((1, S5_S)), const((1, S5_S))],
        [dyb, s, s, proj, du_skip, bd_c, c_c, ar, ai],
        [row_blk(S5_W), const((1, S5_S)), const((1, S5_S)), const(bd_c.shape), const(c_c.shape)],
        [jax.ShapeDtypeStruct((t, S5_W), bf16), jax.ShapeDtypeStruct((1, S5_S), f32),
         jax.ShapeDtypeStruct((1, S5_S), f32), jax.ShapeDtypeStruct(bd_c.shape, f32),
         jax.ShapeDtypeStruct(c_c.shape, f32)],
        name, ride, scratch=[pltpu.VMEM((tb, 2 * S5_S), f32), pltpu.VMEM((1, 2 * S5_S), f32)], sem=("arbitrary",))


def _ada_fwd(c_all, w, b, name):
    d, n = w.shape
    tn = _tile(n, 1536)

    def body(c_ref, w_ref, b_ref, o_ref):
        a = jax.nn.silu(c_ref[...]).astype(bf16)
        o_ref[...] = jnp.dot(a, w_ref[...].astype(bf16), preferred_element_type=f32) + b_ref[...]

    return pl.pallas_call(
        body, name=name, grid=(n // tn,),
        in_specs=[pl.BlockSpec(c_all.shape, lambda j: (0, 0)), pl.BlockSpec((d, tn), lambda j: (0, j)),
                  pl.BlockSpec((1, tn), lambda j: (0, j))],
        out_specs=pl.BlockSpec((c_all.shape[0], tn), lambda j: (0, j)),
        out_shape=jax.ShapeDtypeStruct((c_all.shape[0], n), f32),
        compiler_params=_cp(("parallel",)),
    )(c_all, w, b)


def _ada_bwd(c_all, dm, name):
    d = c_all.shape[1]
    n = dm.shape[1]
    tn = _tile(n, 1536)

    def body(c_ref, dm_ref, o_ref):
        a = jax.nn.silu(c_ref[...]).astype(bf16)
        o_ref[...] = lax.dot_general(a, dm_ref[...].astype(bf16), (((0,), (0,)), ((), ())), preferred_element_type=f32)

    return pl.pallas_call(
        body, name=name, grid=(n // tn,),
        in_specs=[pl.BlockSpec(c_all.shape, lambda j: (0, 0)), pl.BlockSpec((dm.shape[0], tn), lambda j: (0, j))],
        out_specs=pl.BlockSpec((d, tn), lambda j: (0, j)),
        out_shape=jax.ShapeDtypeStruct((d, n), f32),
        compiler_params=_cp(("parallel",)),
    )(c_all, dm)


def _blk_rows(r, c, nbuf, itemsize=4):
    tr = _tile(r, max(SUBLANE, (RW_VMEM_BUDGET // (2 * nbuf * c * itemsize)) // 16 * 16), 16)
    return tr if r % tr == 0 else r


def _cast_bf16(w, name, cols=None):
    r, c = w.shape
    cols = cols or c
    tr = _blk_rows(r, cols, 2)

    def body(w_ref, o_ref):
        o_ref[:, :c] = w_ref[...].astype(bf16)
        if cols > c:
            o_ref[:, c:] = jnp.zeros((tr, cols - c), bf16)

    return pl.pallas_call(
        body, name=name, grid=(r // tr,), in_specs=[pl.BlockSpec((tr, c), lambda i: (i, 0))],
        out_specs=pl.BlockSpec((tr, cols), lambda i: (i, 0)), out_shape=jax.ShapeDtypeStruct((r, cols), bf16),
        compiler_params=_cp(("parallel",)),
    )(w)


def _sum_lead(parts, name, cols=None):
    n, r, c = parts.shape
    cols = cols or c
    tr = _blk_rows(r, c, n + 2)

    def body(p_ref, o_ref):
        acc = p_ref[0].astype(f32)
        for q in range(1, n):
            acc = acc + p_ref[q].astype(f32)
        o_ref[...] = acc[:, :cols]

    return pl.pallas_call(
        body, name=name, grid=(r // tr,), in_specs=[pl.BlockSpec((n, tr, c), lambda i: (0, i, 0))],
        out_specs=pl.BlockSpec((tr, cols), lambda i: (i, 0)), out_shape=jax.ShapeDtypeStruct((r, cols), f32),
        compiler_params=_cp(("parallel",)),
    )(parts)


def _adamw(w, m, v, parts, name):
    r, c = w.shape
    npart = len(parts)
    tr = _blk_rows(r, c, 7 + npart)
    c1 = 1.0 - ADAM_B1 ** ADAM_STEP
    c2 = 1.0 - ADAM_B2 ** ADAM_STEP

    def body(*refs):
        w_ref, m_ref, v_ref = refs[:3]
        g_ref, d_ref, nm_ref, nv_ref = refs[3 + npart:]
        g = refs[3][...].astype(f32)
        for p in refs[4:3 + npart]:
            g = g + p[...].astype(f32)
        nm = ADAM_B1 * m_ref[...] + (1.0 - ADAM_B1) * g
        nv = ADAM_B2 * v_ref[...] + (1.0 - ADAM_B2) * jnp.square(g)
        g_ref[...] = g
        nm_ref[...] = nm
        nv_ref[...] = nv
        d_ref[...] = -ADAM_LR * ((nm / c1) / (jnp.sqrt(nv / c2) + ADAM_EPS) + ADAM_WD * w_ref[...])

    spec = pl.BlockSpec((tr, c), lambda i: (i, 0))
    return pl.pallas_call(
        body, name=name, grid=(r // tr,), in_specs=[spec] * (3 + npart), out_specs=[spec] * 4,
        out_shape=[jax.ShapeDtypeStruct((r, c), f32)] * 4, compiler_params=_cp(("parallel",)),
    )(w, m, v, *parts)


def _ag_small(x_shard, name):
    m_per, n = x_shard.shape

    def body(x_ref, out_ref, send_sems, recv_sems, local_sem):
        x, y, c = lax.axis_index("x"), lax.axis_index("y"), lax.axis_index("c")
        me, sibling = (x, y, c), (x, y, 1 - c)
        chips = [(1 - x, y), (x, 1 - y), (1 - x, 1 - y)]

        def rows(px, py, pc):
            return out_ref.at[pl.ds((4 * px + 2 * py + pc) * m_per, m_per), :]

        def copy(k, block, to, src=None):
            return pltpu.make_async_remote_copy(
                src_ref=rows(*block) if src is None else src, dst_ref=rows(*block),
                send_sem=send_sems.at[k], recv_sem=recv_sems.at[k], device_id=to, device_id_type=MESH)

        mine = pltpu.make_async_copy(x_ref, rows(*me), local_sem)
        mine.start()
        first = [copy(0, me, sibling, src=x_ref)]
        first += [copy(1 + j, me, (*chip, c), src=x_ref) for j, chip in enumerate(chips)]
        for cp in first:
            cp.start()
        passed = [copy(4 + j, (*chip, c), sibling) for j, chip in enumerate(chips)]
        for j, chip in enumerate(chips):
            copy(1 + j, (*chip, c), me).wait_recv()
            passed[j].start()
        copy(0, sibling, me).wait_recv()
        for j, chip in enumerate(chips):
            copy(4 + j, (*chip, 1 - c), me).wait_recv()
        for cp in first + passed:
            cp.wait_send()
        mine.wait()

    return pl.pallas_call(
        body, name=name, out_shape=jax.ShapeDtypeStruct((8 * m_per, n), x_shard.dtype),
        in_specs=[pl.BlockSpec(memory_space=pltpu.VMEM)], out_specs=pl.BlockSpec(memory_space=pltpu.VMEM),
        scratch_shapes=[pltpu.SemaphoreType.DMA((7,)), pltpu.SemaphoreType.DMA((7,)), pltpu.SemaphoreType.DMA],
        compiler_params=pltpu.CompilerParams(vmem_limit_bytes=VMEM_LIMIT),
    )(x_shard)


def _run_ride(ride, name):
    n = ride.n

    def body(*refs):
        ride.start(refs[:n], refs[n:2 * n], refs[2 * n:])
        ride.wait(refs[:n], refs[n:2 * n], refs[2 * n:])

    return pl.pallas_call(
        body, name=name, out_shape=ride.out_shape, in_specs=ride.specs, out_specs=ride.specs,
        scratch_shapes=ride.scratch,
    )(*ride.srcs)


class _Ride:
    def __init__(self, srcs, scatter):
        self.srcs, self.scatter, self.n = list(srcs), scatter, len(srcs)
        n = self.n
        self.out_shape = [jax.ShapeDtypeStruct(s.shape if scatter else (4,) + s.shape, s.dtype) for s in srcs]
        self.specs = [pl.BlockSpec(memory_space=pl.ANY)] * n
        self.scratch = [pltpu.SemaphoreType.DMA((3 * n,)), pltpu.SemaphoreType.DMA((3 * n,)),
                        pltpu.SemaphoreType.DMA((n,))]

    def _copies(self, ins, outs, sems):
        send_sems, recv_sems, local_sems = sems
        x, y, c = lax.axis_index("x"), lax.axis_index("y"), lax.axis_index("c")
        my_k = 2 * x + y
        peers = [(1 - x, y), (x, 1 - y), (1 - x, 1 - y)]
        local, sends, recvs = [], [], []
        for a in range(self.n):
            own = ins[a].at[my_k] if self.scatter else ins[a]
            local.append(pltpu.make_async_copy(own, outs[a].at[my_k], local_sems.at[a]))
            for j, (px, py) in enumerate(peers):
                sems_j = dict(send_sem=send_sems.at[3 * a + j], recv_sem=recv_sems.at[3 * a + j],
                              device_id=(px, py, c), device_id_type=MESH)
                src = ins[a].at[2 * px + py] if self.scatter else ins[a]
                sends.append(pltpu.make_async_remote_copy(src_ref=src, dst_ref=outs[a].at[my_k], **sems_j))
                landed = outs[a].at[2 * px + py]
                recvs.append(pltpu.make_async_remote_copy(src_ref=landed, dst_ref=landed, **sems_j))
        return local, sends, recvs

    def start(self, ins, outs, sems):
        local, sends, _ = self._copies(ins, outs, sems)
        for cp in local + sends:
            cp.start()

    def wait(self, ins, outs, sems):
        local, sends, recvs = self._copies(ins, outs, sems)
        for cp in recvs:
            cp.wait_recv()
        for cp in sends:
            cp.wait_send()
        for cp in local:
            cp.wait()


class _RideGather:
    def __init__(self, srcs):
        self.srcs, self.n = list(srcs), len(srcs)
        n = self.n
        assert all(s.shape[0] % 32 == 0 for s in srcs)
        self.out_shape = [jax.ShapeDtypeStruct((4,) + s.shape, s.dtype) for s in srcs]
        self.specs = [pl.BlockSpec(memory_space=pl.ANY)] * n
        dma = pltpu.SemaphoreType.DMA
        self.scratch = [dma((3 * n,)), dma((3 * n,)), dma((3 * n,)), dma((3 * n,)), dma((n,))]

    def _copies(self, ins, outs, sems):
        send_sems, recv_sems, pass_send, pass_recv, local_sems = sems
        x, y, c = lax.axis_index("x"), lax.axis_index("y"), lax.axis_index("c")
        my_k = 2 * x + y
        peers = [(1 - x, y), (x, 1 - y), (1 - x, 1 - y)]
        local, sends, recvs, passes, pass_recvs = [], [], [], [], []
        for a in range(self.n):
            half = self.srcs[a].shape[0] // 2
            mine = pl.ds(pl.multiple_of(c * half, 16), half)
            other = pl.ds(pl.multiple_of((1 - c) * half, 16), half)
            local.append(pltpu.make_async_copy(ins[a], outs[a].at[my_k], local_sems.at[a]))
            for j, (px, py) in enumerate(peers):
                q = 3 * a + j
                over_ici = dict(send_sem=send_sems.at[q], recv_sem=recv_sems.at[q], device_id=(px, py, c),
                                device_id_type=MESH)
                to_sibling = dict(send_sem=pass_send.at[q], recv_sem=pass_recv.at[q], device_id=(x, y, 1 - c),
                                  device_id_type=MESH)
                sends.append(pltpu.make_async_remote_copy(src_ref=ins[a].at[mine], dst_ref=outs[a].at[my_k, mine],
                                                          **over_ici))
                landed = outs[a].at[2 * px + py, mine]
                recvs.append(pltpu.make_async_remote_copy(src_ref=landed, dst_ref=landed, **over_ici))
                passes.append(pltpu.make_async_remote_copy(src_ref=landed, dst_ref=landed, **to_sibling))
                from_sibling = outs[a].at[2 * px + py, other]
                pass_recvs.append(pltpu.make_async_remote_copy(src_ref=from_sibling, dst_ref=from_sibling, **to_sibling))
        return local, sends, recvs, passes, pass_recvs

    def start(self, ins, outs, sems):
        local, sends = self._copies(ins, outs, sems)[:2]
        for cp in local + sends:
            cp.start()

    def wait(self, ins, outs, sems):
        local, sends, recvs, passes, pass_recvs = self._copies(ins, outs, sems)
        for rc, ps in zip(recvs, passes):
            rc.wait_recv()
            ps.start()
        for cp in pass_recvs:
            cp.wait_recv()
        for cp in sends + passes:
            cp.wait_send()
        for cp in local:
            cp.wait()


class _RideSwap:
    def __init__(self, srcs):
        self.srcs, self.n = list(srcs), len(srcs)
        self.out_shape = [jax.ShapeDtypeStruct(s.shape, s.dtype) for s in srcs]
        self.specs = [pl.BlockSpec(memory_space=pl.ANY)] * self.n
        self.scratch = [pltpu.SemaphoreType.DMA((self.n,)), pltpu.SemaphoreType.DMA((self.n,))]

    def _copies(self, ins, outs, sems):
        send_sems, recv_sems = sems
        sib = (lax.axis_index("x"), lax.axis_index("y"), 1 - lax.axis_index("c"))
        return [pltpu.make_async_remote_copy(src_ref=ins[a], dst_ref=outs[a], send_sem=send_sems.at[a],
                                             recv_sem=recv_sems.at[a], device_id=sib, device_id_type=MESH)
                for a in range(self.n)]

    def start(self, ins, outs, sems):
        for cp in self._copies(ins, outs, sems):
            cp.start()

    def wait(self, ins, outs, sems):
        cps = self._copies(ins, outs, sems)
        for cp in cps:
            cp.wait_recv()
        for cp in cps:
            cp.wait_send()


class _Rides:
    def __init__(self, rides):
        self.rides = list(rides)
        self.n = sum(r.n for r in self.rides)
        self.srcs = [s for r in self.rides for s in r.srcs]
        self.out_shape = [s for r in self.rides for s in r.out_shape]
        self.specs = [s for r in self.rides for s in r.specs]
        self.scratch = [s for r in self.rides for s in r.scratch]

    def _each(self, ins, outs, sems):
        i = k = 0
        for r in self.rides:
            yield r, ins[i:i + r.n], outs[i:i + r.n], sems[k:k + len(r.scratch)]
            i += r.n
            k += len(r.scratch)

    def start(self, ins, outs, sems):
        for r, a, b, c in self._each(ins, outs, sems):
            r.start(a, b, c)

    def wait(self, ins, outs, sems):
        for r, a, b, c in self._each(ins, outs, sems):
            r.wait(a, b, c)

    def split(self, results):
        out, i = [], 0
        for r in self.rides:
            out.append(results[i:i + r.n])
            i += r.n
        return out


def _gather8(vec, name):
    size = vec.shape[0]
    n = _round_up(size, SUBLANE * LANE)
    blk = jnp.concatenate([vec, jnp.zeros((n - size,), f32)]).reshape(SUBLANE, n // SUBLANE)
    out = _ag_small(blk, name)
    return out.reshape(8, n)[:, :size]


def _head_spread_matrices():
    e = np.zeros((LANE, SSD_DI), np.float32)
    for h in range(SSD_HEADS):
        e[h, h * SSD_P:(h + 1) * SSD_P] = 1.0
    return jnp.asarray(e, bf16), jnp.asarray(e.T, bf16)


def _heads_to_lanes(v):
    return jnp.repeat(v, SSD_P).reshape(1, SSD_DI)


def _block_diag8(blocks):
    g, r, c = blocks.shape
    b = blocks.reshape(g // S5_NB, S5_NB, r, c)
    eye = jnp.eye(S5_NB, dtype=bool)[None, :, None, :, None]
    return jnp.where(eye, b[:, :, :, None, :], jnp.zeros((), blocks.dtype)).reshape(g * r, S5_NB * c)


def _diag8(mat, r, c):
    g = mat.shape[0] // r
    m = mat.reshape(g // S5_NB, S5_NB, r, S5_NB, c)
    eye = jnp.eye(S5_NB, dtype=bool)[None, :, None, :, None]
    return jnp.where(eye, m, 0.0).sum(axis=3).reshape(g, r, c)


class _Layout:
    def __init__(self, d):
        self.d = d
        self.z, self.xbc, self.u = 0, SSD_DI, SSD_DI + CONV_DIM
        self.ga = self.u + S5_W
        self.gb = self.ga + d
        self.dt = self.gb + d
        self.np_ = self.dt + LANE
        self.in_cols = SSD_DI + CONV_DIM + SSD_HEADS + S5_W + 2 * d
        off_dt = SSD_DI + CONV_DIM
        off_u = off_dt + SSD_HEADS
        self.src = [(0, off_dt), (off_u, off_u + S5_W + 2 * d), (off_dt, off_u)]

    def arrange_slabs(self, g):
        pieces = [p for lo, hi in self.src for p in _cols_from_slabs(g, lo, hi)]
        pieces.append(jnp.zeros((g.shape[1], LANE - SSD_HEADS), g.dtype))
        return jnp.concatenate(pieces, axis=1)

    def restore_slabs(self, chunks):
        (a0, a1), (b0, b1), (c0, c1) = self.src
        n_a, n_b = a1 - a0, b1 - b0
        segs = [(a0, a1, 0), (c0, c1, n_a + n_b), (b0, b1, n_a)]
        firsts = np.cumsum([0] + [c.shape[1] for c in chunks])

        def take(lo, hi):
            return [c[:, max(lo, f) - f:min(hi, f + c.shape[1]) - f] for c, f in zip(chunks, firsts)
                    if max(lo, f) < min(hi, f + c.shape[1])]

        cs = self.in_cols // 4
        slabs = []
        for k in range(4):
            lo, hi = k * cs, (k + 1) * cs
            parts = [p for s0, s1, pos in segs if max(lo, s0) < min(hi, s1)
                     for p in take(pos + max(lo, s0) - s0, pos + min(hi, s1) - s0)]
            slabs.append(jnp.concatenate(parts, axis=1))
        return jnp.stack(slabs)


def _cols_from_slabs(g, start, stop):
    c = g.shape[2]
    return [g[k][:, max(start, k * c) - k * c:min(stop, (k + 1) * c) - k * c] for k in range(4)
            if max(start, k * c) < min(stop, (k + 1) * c)]


def _unshard_cols(g):
    return jnp.concatenate([g[k] for k in range(4)], axis=1)


def _shard_cols(w):
    r, c4 = w.shape
    return w.reshape(r, 4, c4 // 4).transpose(1, 0, 2)


def kernel(x, c, w_ada, b_ada, norm_ffn1, w_ffn1_in, w_ffn1_out, norm_mix, w_in, conv_w, conv_b, dt_bias, a_log, d_ssd, ssd_norm_w, w_a_proj, s5_lambda_re, s5_lambda_im, s5_b_re, s5_b_im, s5_c_re, s5_c_im, s5_d, s5_log_dt, w_b_glu, w_out, norm_ffn2, w_ffn2_in, w_ffn2_out, norm_final, loss_target, m_w_ada, m_b_ada, m_norm_ffn1, m_w_ffn1_in, m_w_ffn1_out, m_norm_mix, m_w_in, m_conv_w, m_conv_b, m_dt_bias, m_a_log, m_d_ssd, m_ssd_norm_w, m_w_a_proj, m_s5_lambda_re, m_s5_lambda_im, m_s5_b_re, m_s5_b_im, m_s5_c_re, m_s5_c_im, m_s5_d, m_s5_log_dt, m_w_b_glu, m_w_out, m_norm_ffn2, m_w_ffn2_in, m_w_ffn2_out, m_norm_final, v_w_ada, v_b_ada, v_norm_ffn1, v_w_ffn1_in, v_w_ffn1_out, v_norm_mix, v_w_in, v_conv_w, v_conv_b, v_dt_bias, v_a_log, v_d_ssd, v_ssd_norm_w, v_w_a_proj, v_s5_lambda_re, v_s5_lambda_im, v_s5_b_re, v_s5_b_im, v_s5_c_re, v_s5_c_im, v_s5_d, v_s5_log_dt, v_w_b_glu, v_w_out, v_norm_ffn2, v_w_ffn2_in, v_w_ffn2_out, v_norm_final):
    W = dict(w_ada=w_ada, b_ada=b_ada, norm_ffn1=norm_ffn1, w_ffn1_in=w_ffn1_in, w_ffn1_out=w_ffn1_out, norm_mix=norm_mix, w_in=w_in, conv_w=conv_w, conv_b=conv_b, dt_bias=dt_bias, a_log=a_log, d_ssd=d_ssd, ssd_norm_w=ssd_norm_w, w_a_proj=w_a_proj, s5_lambda_re=s5_lambda_re, s5_lambda_im=s5_lambda_im, s5_b_re=s5_b_re, s5_b_im=s5_b_im, s5_c_re=s5_c_re, s5_c_im=s5_c_im, s5_d=s5_d, s5_log_dt=s5_log_dt, w_b_glu=w_b_glu, w_out=w_out, norm_ffn2=norm_ffn2, w_ffn2_in=w_ffn2_in, w_ffn2_out=w_ffn2_out, norm_final=norm_final)
    Mo = dict(w_ada=m_w_ada, b_ada=m_b_ada, norm_ffn1=m_norm_ffn1, w_ffn1_in=m_w_ffn1_in, w_ffn1_out=m_w_ffn1_out, norm_mix=m_norm_mix, w_in=m_w_in, conv_w=m_conv_w, conv_b=m_conv_b, dt_bias=m_dt_bias, a_log=m_a_log, d_ssd=m_d_ssd, ssd_norm_w=m_ssd_norm_w, w_a_proj=m_w_a_proj, s5_lambda_re=m_s5_lambda_re, s5_lambda_im=m_s5_lambda_im, s5_b_re=m_s5_b_re, s5_b_im=m_s5_b_im, s5_c_re=m_s5_c_re, s5_c_im=m_s5_c_im, s5_d=m_s5_d, s5_log_dt=m_s5_log_dt, w_b_glu=m_w_b_glu, w_out=m_w_out, norm_ffn2=m_norm_ffn2, w_ffn2_in=m_w_ffn2_in, w_ffn2_out=m_w_ffn2_out, norm_final=m_norm_final)
    Vo = dict(w_ada=v_w_ada, b_ada=v_b_ada, norm_ffn1=v_norm_ffn1, w_ffn1_in=v_w_ffn1_in, w_ffn1_out=v_w_ffn1_out, norm_mix=v_norm_mix, w_in=v_w_in, conv_w=v_conv_w, conv_b=v_conv_b, dt_bias=v_dt_bias, a_log=v_a_log, d_ssd=v_d_ssd, ssd_norm_w=v_ssd_norm_w, w_a_proj=v_w_a_proj, s5_lambda_re=v_s5_lambda_re, s5_lambda_im=v_s5_lambda_im, s5_b_re=v_s5_b_re, s5_b_im=v_s5_b_im, s5_c_re=v_s5_c_re, s5_c_im=v_s5_c_im, s5_d=v_s5_d, s5_log_dt=v_s5_log_dt, w_b_glu=v_w_b_glu, w_out=v_w_out, norm_ffn2=v_norm_ffn2, w_ffn2_in=v_w_ffn2_in, w_ffn2_out=v_w_ffn2_out, norm_final=v_norm_final)

    t, d = x.shape[1], x.shape[2]
    ff = 4 * w_ffn1_out.shape[1]
    hf = ff // 2
    hfp = _round_up(hf, LANE)
    lay = _Layout(d)
    xi, yi, ci = lax.axis_index("x"), lax.axis_index("y"), lax.axis_index("c")
    k_me = 2 * xi + yi
    e_me = 4 * xi + 2 * yi + ci
    x2d = x[0]
    tgt = loss_target[0]

    cw_cols = conv_w.shape[2]
    g1 = _gather8(jnp.concatenate([c[0], conv_w[0].reshape(-1)]), "gather_c_convw")
    c_all = g1[:, :d]
    conv_full = g1[::2, d:].reshape(4, CONV_K, cw_cols).transpose(1, 0, 2).reshape(CONV_K, CONV_DIM)
    conv_w8 = jnp.zeros((SUBLANE, CONV_DIM), f32).at[:CONV_K].set(conv_full)

    n_ada_loc = w_ada.shape[2]
    b_loc = lax.dynamic_slice(b_ada, (0, k_me * n_ada_loc), (1, n_ada_loc))
    mods_part = _ada_fwd(c_all, w_ada[0], b_loc, "ada_fwd")
    g2 = _gather8(mods_part.reshape(-1), "gather_mods").reshape(8, 8, n_ada_loc)
    mods = lax.dynamic_index_in_dim(g2[::2], e_me, axis=1, keepdims=False).reshape(N_ADA, d)
    sh1, sc1, gt1, sh2, sc2, gt2, sh3, sc3, gt3 = [mods[i:i + 1] for i in range(N_ADA)]

    cast = {n: _cast_bf16(W[n][0], "cast_" + n, cols=hfp if n in ('w_ffn1_in', 'w_ffn2_in') else None)
            for n in BIG}

    def gather_of(names):
        return _RideGather([cast[n] for n in names])

    def rows_of(g):
        return g.reshape(4 * g.shape[1], g.shape[2])

    def ffn_out(g):
        z = jnp.zeros((hfp - hf, g.shape[2]), g.dtype)
        return jnp.concatenate([g[0], g[1], z, g[2], g[3], z], axis=0)

    nf1, nmx, nf2 = norm_ffn1, norm_mix, norm_ffn2
    nfin = norm_final.reshape(1, d)

    (g_w1i,) = _run_ride(gather_of(['w_ffn1_in']), "gather_w_ffn1_in")
    w1i = g_w1i
    (h1,) = _rw(_f_mod, [x2d], [nf1, sh1, sc1], [(d, bf16)], name="mod1")
    ab1, act1, (g_w1o, g_win) = _ffn_in(h1, w1i, "ffn1_in", ride=gather_of(['w_ffn1_out', 'w_in']))
    w1o = ffn_out(g_w1o)
    w_inr = lay.arrange_slabs(g_win)
    f1, (g_wa, g_wglu, g_wo) = _mm(act1, w1o, 'nn', out_dtype=bf16, name="ffn1_out",
                                   ride=gather_of(['w_a_proj', 'w_b_glu', 'w_out']))
    w_a = rows_of(g_wa)
    w_glu, w_o = _unshard_cols(g_wglu), rows_of(g_wo)
    res1 = functools.partial(_f_res_mod, 0.5)
    x1, h2 = _rw(res1, [x2d, f1], [gt1, nmx, sh2, sc2], [(d, f32), (d, bf16)], name="res1_mod2")
    proj, (g_w2i,) = _mm(h2, w_inr, 'nn', out_dtype=f32, name="in_proj", ride=gather_of(['w_ffn2_in']))
    w2i = g_w2i

    pre = _conv_fwd(proj, lay.xbc, conv_w8, conv_b, "conv_fwd")
    spread, spread_t = _head_spread_matrices()
    bias128 = jnp.zeros((1, LANE), f32).at[:, :SSD_HEADS].set(dt_bias)
    xs, bm, cm, dt4 = _rw(_f_ssd_pre, [pre, (proj, lay.dt, LANE)], [bias128, spread, spread_t],
                          [(SSD_DI, f32), (SSD_G * SSD_N, f32), (SSD_G * SSD_N, f32), (SSD_DI, f32)],
                          name="ssd_pre")

    def head_params(a_log_, d_ssd_):
        return _heads_to_lanes(-jnp.exp(a_log_[0])), _heads_to_lanes(d_ssd_[0])

    (a4, dsk4), head_vjp = jax.vjp(head_params, a_log, d_ssd)
    y_ssd, hs, (g_w2o,) = _ssd_fwd(xs, bm, cm, dt4, a4, dsk4, "ssd_fwd", ride=gather_of(['w_ffn2_out']))
    w2o = ffn_out(g_w2o)
    (y_a,) = _rw(_f_ssd_post, [y_ssd, (proj, lay.z, SSD_DI)], [ssd_norm_w], [(SSD_DI, bf16)], name="ssd_post")
    p_a = _mm(y_a, w_a, 'nn', out_dtype=bf16, name="a_proj")

    col = lambda v: v.reshape(S5_S, 1)
    ldt_col = jnp.repeat(s5_log_dt[0], S5_P).reshape(S5_S, 1)
    prep_rows = [col(s5_lambda_re[0]), col(s5_lambda_im[0]), ldt_col,
                 s5_b_re[0].reshape(S5_S, S5_I), s5_b_im[0].reshape(S5_S, S5_I)]
    ar, ai, bbr, bbi = _rw(_f_s5_prep, prep_rows, [], [(1, f32), (1, f32), (S5_I, f32), (S5_I, f32)],
                           name="s5_prep", tm=512)
    to_bd = lambda bb: _block_diag8(bb.reshape(S5_G, S5_P, S5_I).transpose(0, 2, 1).astype(bf16))
    bd_c = jnp.concatenate([to_bd(bbr), to_bd(bbi)], axis=1)
    c_c = jnp.concatenate([_block_diag8(s5_c_re[0].transpose(0, 2, 1).astype(bf16)),
                           _block_diag8((-s5_c_im[0]).transpose(0, 2, 1).astype(bf16))], axis=0)
    ar_row, ai_row = ar.reshape(1, S5_S), ai.reshape(1, S5_S)
    s5s, yb = _s5_fwd(proj, lay.u, bd_c, c_c, ar_row, ai_row, "s5_fwd")
    d_row = s5_d[0].reshape(1, S5_W)
    (gl,) = _rw(_f_s5_post, [yb, (proj, lay.u, S5_W)], [d_row], [(S5_W, bf16)], name="s5_post")
    glu = _mm(gl, w_glu, 'nn', out_dtype=bf16, name="glu_proj")

    merge_rows = [p_a, glu, (proj, lay.ga, d), (proj, lay.gb, d)]
    (merged,) = _rw(_f_merge, merge_rows, [], [(d, bf16)], name="merge")
    o_mix = _mm(merged, w_o, 'nn', out_dtype=bf16, name="out_proj")
    res2 = functools.partial(_f_res_mod, 1.0)
    x2, h3 = _rw(res2, [x1, o_mix], [gt2, nf2, sh3, sc3], [(d, f32), (d, bf16)], name="res2_mod3")
    ab2, act2, _ = _ffn_in(h3, w2i, "ffn2_in")
    f2 = _mm(act2, w2o, 'nn', out_dtype=bf16, name="ffn2_out")

    ones = jnp.ones((t, 1), f32)
    dx2, df2, dgt3, dnfin, loss_acc = _rw_vjp(_f_final, [x2, f2, tgt], [gt3, nfin], [ones],
                                              row_grads=[f32, bf16, None], param_grads=[True, True],
                                              name="loss_and_bwd", total_of=0)
    loss = lax.psum(loss_acc[0, 0], AXES)
    def ffn_out_back(g):
        return jnp.concatenate([g[:hf], g[hfp:hfp + hf]], axis=0).reshape(4, ff // 4, g.shape[1])

    def rows_back(g, rows):
        return g.reshape(4, rows // 4, g.shape[1])

    def scatter_of(pairs):
        return _Ride([g for _, g in pairs], True)

    terms = {}
    dab2, _ = _ffn_out_dx(df2, w2o, ab2, "ffn2_out_dx")
    dw2o = _mm(act2, df2, 'tn', out_dtype=bf16, name="ffn2_out_dw")
    dh3, (terms['w_ffn2_out'],) = _ffn_in_dx(dab2, w2i, "ffn2_in_dx",
                                             ride=scatter_of([('w_ffn2_out', ffn_out_back(dw2o))]))
    dw2i = _ffn_in_dw(h3, dab2, "ffn2_in_dw")
    dx1, do_mix, dgt2, dnf2, dsh3, dsc3 = _rw_vjp(
        res2, [x1, o_mix], [gt2, nf2, sh3, sc3], [dx2, dh3], row_grads=[f32, bf16], param_grads=[True] * 4,
        name="res2_mod3_bwd")
    dmerged = _mm(do_mix, w_o, 'nt', out_dtype=bf16, name="out_proj_dx")
    dw_o = _mm(merged, do_mix, 'tn', out_dtype=bf16, name="out_proj_dw")
    dp_a, dglu, dga, dgb = _rw_vjp(_f_merge, merge_rows, [], [dmerged], row_grads=[bf16] * 4,
                                   param_grads=[], name="merge_bwd")

    dgl = _mm(dglu, w_glu, 'nt', out_dtype=bf16, name="glu_proj_dx")
    dw_glu = _mm(gl, dglu, 'tn', out_dtype=bf16, name="glu_proj_dw")
    dyb, du_skip, dd_row = _rw_vjp(_f_s5_post, [yb, (proj, lay.u, S5_W)], [d_row], [dgl],
                                   row_grads=[bf16, f32], param_grads=[True], name="s5_post_bwd")
    (du, dar, dai, dbd_c, dc_c), (terms['w_ffn2_in'],) = _s5_bwd(
        dyb, s5s, proj, lay.u, bd_c, c_c, ar_row, ai_row, du_skip, "s5_bwd",
        ride=scatter_of([('w_ffn2_in', dw2i)]))
    from_bd = lambda m_: _diag8(m_, S5_I, S5_P).transpose(0, 2, 1).reshape(S5_S, S5_I)
    dprep = _rw_vjp(_f_s5_prep, prep_rows, [], [dar.reshape(S5_S, 1), dai.reshape(S5_S, 1),
                                                from_bd(dbd_c[:, :S5_SB]), from_bd(dbd_c[:, S5_SB:])],
                    row_grads=[f32] * 5, param_grads=[], name="s5_prep_bwd", tm=512)
    dlr, dli, dldt, dbr, dbi = dprep
    g_s5 = dict(
        s5_lambda_re=dlr.reshape(S5_G, S5_P), s5_lambda_im=dli.reshape(S5_G, S5_P),
        s5_log_dt=dldt.reshape(S5_G, S5_P).sum(axis=1),
        s5_b_re=dbr.reshape(S5_G, S5_P, S5_I), s5_b_im=dbi.reshape(S5_G, S5_P, S5_I),
        s5_c_re=_diag8(dc_c[:S5_S], S5_P, S5_I).transpose(0, 2, 1),
        s5_c_im=-_diag8(dc_c[S5_S:], S5_P, S5_I).transpose(0, 2, 1),
        s5_d=dd_row.reshape(S5_G, S5_I))

    dy_a = _mm(dp_a, w_a, 'nt', out_dtype=bf16, name="a_proj_dx")
    dw_a = _mm(y_a, dp_a, 'tn', out_dtype=bf16, name="a_proj_dw")
    dy_ssd, dz, dssd_nw = _rw_vjp(_f_ssd_post, [y_ssd, (proj, lay.z, SSD_DI)], [ssd_norm_w], [dy_a],
                                  row_grads=[f32, bf16], param_grads=[True], name="ssd_post_bwd")
    early = [('w_out', rows_back(dw_o, d)), ('w_b_glu', _shard_cols(dw_glu)), ('w_a_proj', rows_back(dw_a, SSD_DI))]
    (dxs, dbm, dcm, ddt4, da4, ddsk4), landed = _ssd_bwd(xs, bm, cm, dt4, a4, dsk4, hs, dy_ssd, "ssd_bwd",
                                                         ride=scatter_of(early))
    terms.update({n: p for (n, _), p in zip(early, landed)})
    da_log, dd_ssd = head_vjp((da4, ddsk4))
    dpre, ddt_raw, dbias128 = _rw_vjp(_f_ssd_pre, [pre, (proj, lay.dt, LANE)], [bias128, spread, spread_t],
                                      [dxs, dbm, dcm, ddt4], row_grads=[f32, bf16],
                                      param_grads=[True, False, False], name="ssd_pre_bwd")
    dxbc, dconv_w8, dconv_b = _conv_bwd(dpre, proj, lay.xbc, conv_w8, "conv_bwd")

    dproj = [dz, dxbc, du, dga, dgb]
    dw_in = [_mm(h2, p, 'tn', out_dtype=bf16, name="in_proj_dw_%d" % i) for i, p in enumerate(dproj + [ddt_raw])]
    dh2, (terms['w_in'],) = _in_proj_dx(dproj, ddt_raw, lay.dt, w_inr, "in_proj_dx",
                                        ride=scatter_of([('w_in', lay.restore_slabs(dw_in))]))
    dx0, df1, dgt1, dnmx, dsh2, dsc2 = _rw_vjp(
        res1, [x2d, f1], [gt1, nmx, sh2, sc2], [dx1, dh2], row_grads=[f32, bf16], param_grads=[True] * 4,
        name="res1_mod2_bwd")
    dw1o = _mm(act1, df1, 'tn', out_dtype=bf16, name="ffn1_out_dw")
    dab1, (terms['w_ffn1_out'],) = _ffn_out_dx(df1, w1o, ab1, "ffn1_out_dx",
                                               ride=scatter_of([('w_ffn1_out', ffn_out_back(dw1o))]))
    dw1i = _ffn_in_dw(h1, dab1, "ffn1_in_dw")

    last = 'w_ffn1_in'
    keep = {'w_ffn1_in': hf, 'w_ffn2_in': hf}
    sums = {n: _sum_lead(terms[n], "sum_" + n, cols=keep.get(n)) for n in BIG if n != last}
    swap = _RideSwap([sums[n] for n in BIG if n != last])
    rides = _Rides([scatter_of([(last, dw1i)]), swap])
    dh1, landed = _ffn_in_dx(dab1, w1i, "ffn1_in_dx", ride=rides)
    (terms[last],), swapped = rides.split(landed)
    others = dict(zip([n for n in BIG if n != last], swapped))
    grad_x, dnf1, dsh1, dsc1 = _rw_vjp(_f_mod_keep, [x2d], [nf1, sh1, sc1], [dh1, dx0],
                                       row_grads=[f32], param_grads=[True] * 3, name="mod1_bwd")
    d_mods = jnp.concatenate([dsh1, dsc1, dgt1, dsh2, dsc2, dgt2, dsh3, dsc3, dgt3], axis=1).reshape(-1)
    sums[last] = _sum_lead(terms[last], "sum_" + last, cols=keep.get(last))
    (others[last],) = _run_ride(_RideSwap([sums[last]]), "swap_sum_" + last)

    out_g, out_d, out_m, out_v = {}, {}, {}, {}
    for n in BIG:
        r = _adamw(W[n][0], Mo[n][0], Vo[n][0], [sums[n], others[n]], "adamw_" + n)
        out_g[n], out_d[n], out_m[n], out_v[n] = [o[None] for o in r]

    local = dict(
        b_ada=d_mods, norm_ffn1=dnf1, norm_mix=dnmx, conv_w=dconv_w8[:CONV_K], conv_b=dconv_b,
        dt_bias=dbias128[:, :SSD_HEADS], a_log=da_log, d_ssd=dd_ssd, ssd_norm_w=dssd_nw,
        norm_ffn2=dnf2, norm_final=dnfin, **g_s5)
    flat = jnp.concatenate([local[n].reshape(-1) for n in SMALL])
    g3 = _gather8(flat, "gather_small_grads")
    n_small = flat.shape[0]
    npad = _round_up(n_small, SUBLANE * LANE)
    g3p = jnp.zeros((8, npad), f32).at[:, :n_small].set(g3).reshape(8, npad // LANE, LANE)
    gsum = _sum_lead(g3p, "sum_small").reshape(-1)

    def local_shard(n, a):
        if n == 'conv_w':
            return lax.dynamic_slice(a.reshape(CONV_K, CONV_DIM), (0, k_me * cw_cols), (CONV_K, cw_cols))
        return a

    pieces, off = {}, 0
    for n in SMALL:
        sz = local[n].size
        pieces[n] = local_shard(n, gsum[off:off + sz]).reshape(W[n].shape)
        off += sz

    def pack(dct):
        v_ = jnp.concatenate([dct[n].reshape(-1) for n in SMALL])
        pad = _round_up(v_.shape[0], SUBLANE * LANE) - v_.shape[0]
        return jnp.concatenate([v_, jnp.ones((pad,), f32)]).reshape(-1, LANE)

    rs = _adamw(pack(W), pack(Mo), pack(Vo), [pack(pieces)], "adamw_small")
    off = 0
    for n in SMALL:
        sz = W[n].size
        out_g[n], out_d[n], out_m[n], out_v[n] = [o.reshape(-1)[off:off + sz].reshape(W[n].shape) for o in rs]
        off += sz

    dm_loc = lax.dynamic_slice(g3[:, :N_ADA * d], (0, k_me * n_ada_loc), (SUBLANE, n_ada_loc))
    g_ada = _ada_bwd(c_all, dm_loc, "ada_bwd")
    r = _adamw(w_ada[0], m_w_ada[0], v_w_ada[0], [g_ada], "adamw_w_ada")
    out_g['w_ada'], out_d['w_ada'], out_m['w_ada'], out_v['w_ada'] = [o[None] for o in r]

    return (loss, grad_x[None], *[out_g[n] for n in WEIGHTS], *[out_d[n] for n in WEIGHTS],
            *[out_m[n] for n in WEIGHTS], *[out_v[n] for n in WEIGHTS])
```

```python
import functools
import math

import numpy as np
import jax
import jax.numpy as jnp
from jax import lax
from jax.experimental import pallas as pl
from jax.experimental.pallas import tpu as pltpu

f32 = jnp.float32
bf16 = jnp.bfloat16
MESH = pl.DeviceIdType.MESH
AXES = ("x", "y", "c")

EPS = 1e-6
SSD_HEADS, SSD_P, SSD_N, SSD_G, SSD_R, SSD_L = 32, 64, 128, 4, 8, 128
SSD_DI = SSD_HEADS * SSD_P
CONV_K = 4
CONV_DIM = SSD_DI + 2 * SSD_G * SSD_N
S5_W, S5_G, S5_I, S5_P = 1024, 64, 16, 64
S5_S = S5_G * S5_P
N_ADA = 9
ADAM_LR, ADAM_B1, ADAM_B2, ADAM_EPS, ADAM_WD, ADAM_STEP = 0.001, 0.9, 0.999, 1e-08, 0.01, 10

LANE = 128
SUBLANE = 8
VMEM_LIMIT = 56 << 20
MM_VMEM_BUDGET = 40 << 20
RW_VMEM_BUDGET = 36 << 20

WEIGHTS = ['w_ada', 'b_ada', 'norm_ffn1', 'w_ffn1_in', 'w_ffn1_out', 'norm_mix', 'w_in', 'conv_w', 'conv_b', 'dt_bias',
           'a_log', 'd_ssd', 'ssd_norm_w', 'w_a_proj', 's5_lambda_re', 's5_lambda_im', 's5_b_re', 's5_b_im', 's5_c_re',
           's5_c_im', 's5_d', 's5_log_dt', 'w_b_glu', 'w_out', 'norm_ffn2', 'w_ffn2_in', 'w_ffn2_out', 'norm_final']
BIG = ['w_ffn1_in', 'w_ffn1_out', 'w_in', 'w_a_proj', 'w_b_glu', 'w_out', 'w_ffn2_in', 'w_ffn2_out']
COL_SHARDED = ('w_ffn1_in', 'w_in', 'w_b_glu', 'w_ffn2_in')
SMALL = [n for n in WEIGHTS if n not in BIG and n != 'w_ada']


def _cp(sem=None):
    return pltpu.CompilerParams(dimension_semantics=sem, vmem_limit_bytes=VMEM_LIMIT)


def _tile(dim, target, align=LANE):
    if dim <= target:
        return dim
    t = (target // align) * align
    while t >= align:
        if dim % t == 0:
            return t
        t -= align
    return dim


def _round_up(n, m):
    return (n + m - 1) // m * m


def _mm(a, b, mode, *, out_dtype, name, a_win=None, b_win=None, add=None, ride=None):
    a0, aw = a_win or (0, a.shape[1])
    b0, bw = b_win or (0, b.shape[1])
    if mode == 'nn':
        m, k, n = a.shape[0], aw, bw
        assert b.shape[0] == k
    elif mode == 'nt':
        m, k, n = a.shape[0], aw, b.shape[0]
        assert bw == k
    else:
        k, m, n = a.shape[0], aw, bw
        assert b.shape[0] == k
    osz = jnp.dtype(out_dtype).itemsize
    tm, tn, tk = 1024, 1152, 3456
    while True:
        bm = _tile(math.gcd(m, a0) if (mode == 'tn' and a0) else m, tm)
        bn = _tile(math.gcd(n, b0) if (mode != 'nt' and b0) else n, tn)
        kk = k
        if mode != 'tn' and a0:
            kk = math.gcd(kk, a0)
        if mode == 'nt' and b0:
            kk = math.gcd(kk, b0)
        bk = _tile(kk, tk)
        need = 2 * (bm * bk * a.dtype.itemsize + bk * bn * b.dtype.itemsize + bm * bn * osz) + bm * bn * 4
        if add is not None:
            need += 2 * bm * bn * add.dtype.itemsize
        if need <= MM_VMEM_BUDGET or (tm <= 256 and tn <= 256 and tk <= 512):
            break
        if tk > 1024:
            tk //= 2
        elif tm >= tn:
            tm //= 2
        else:
            tn //= 2
    nk = k // bk
    assert m % bm == 0 and n % bn == 0 and k % bk == 0, (name, m, n, k, bm, bn, bk)
    if mode == 'nn':
        ao, bo = a0 // bk, b0 // bn
        a_blk, a_map = (bm, bk), lambda i, j, q: (i, q + ao)
        b_blk, b_map = (bk, bn), lambda i, j, q: (q, j + bo)
    elif mode == 'nt':
        ao, bo = a0 // bk, b0 // bk
        a_blk, a_map = (bm, bk), lambda i, j, q: (i, q + ao)
        b_blk, b_map = (bn, bk), lambda i, j, q: (j, q + bo)
    else:
        ao, bo = a0 // bm, b0 // bn
        a_blk, a_map = (bk, bm), lambda i, j, q: (q, i + ao)
        b_blk, b_map = (bk, bn), lambda i, j, q: (q, j + bo)
    return _mm_core(a, b, mode, grid=(m // bm, n // bn, nk), a_blk=a_blk, a_map=a_map, b_blk=b_blk, b_map=b_map,
                    o_blk=(bm, bn), o_map=lambda i, j, q: (i, j), out_shape=(m, n), out_dtype=out_dtype, name=name,
                    add=add, ride=ride)


def _mm_core(a, b, mode, *, grid, a_blk, a_map, b_blk, b_map, o_blk, o_map, out_shape, out_dtype, name,
             add=None, ride=None):
    dims = {'nn': (((1,), (0,)), ((), ())), 'nt': (((1,), (1,)), ((), ())), 'tn': (((0,), (0,)), ((), ()))}[mode]
    nk = grid[-1]
    has_add = add is not None
    nr = ride.n if ride is not None else 0

    def body(*refs):
        a_ref, b_ref = refs[0], refs[1]
        pos = 2
        add_ref = refs[pos] if has_add else None
        pos += int(has_add)
        r_ins = refs[pos:pos + nr]
        o_ref = refs[pos + nr]
        r_outs = refs[pos + nr + 1:pos + 2 * nr + 1]
        acc_ref = refs[pos + 2 * nr + 1]
        r_sems = refs[pos + 2 * nr + 2:]
        ids = [pl.program_id(ax) for ax in range(len(grid))]
        q = ids[-1]
        if nr:
            @pl.when(functools.reduce(lambda u, v: u & v, [i == 0 for i in ids]))
            def _():
                ride.start(r_ins, r_outs, r_sems)

        def prod():
            return lax.dot_general(a_ref[...].astype(bf16), b_ref[...].astype(bf16), dims, preferred_element_type=f32)

        def emit(r):
            if has_add:
                r = r + add_ref[...].astype(f32)
            o_ref[...] = r.astype(out_dtype)

        if nk == 1:
            emit(prod())
        else:
            @pl.when(q == 0)
            def _():
                acc_ref[...] = prod()

            @pl.when((q > 0) & (q < nk - 1))
            def _():
                acc_ref[...] += prod()

            @pl.when(q == nk - 1)
            def _():
                emit(acc_ref[...] + prod())

        if nr:
            @pl.when(functools.reduce(lambda u, v: u & v, [i == g - 1 for i, g in zip(ids, grid)]))
            def _():
                ride.wait(r_ins, r_outs, r_sems)

    in_specs = [pl.BlockSpec(a_blk, a_map), pl.BlockSpec(b_blk, b_map)]
    ops = [a, b]
    if has_add:
        in_specs.append(pl.BlockSpec(o_blk, o_map))
        ops.append(add)
    out_specs = [pl.BlockSpec(o_blk, o_map)]
    out_shapes = [jax.ShapeDtypeStruct(out_shape, out_dtype)]
    scratch = [pltpu.VMEM(tuple(b for b in o_blk if b is not None), f32)]
    if nr:
        in_specs += ride.specs
        ops += ride.srcs
        out_specs += ride.specs
        out_shapes += ride.out_shape
        scratch += ride.scratch
    sem = ("arbitrary",) * len(grid) if nr else ("parallel",) * (len(grid) - 1) + ("arbitrary",)
    res = pl.pallas_call(
        body, name=name, grid=grid, in_specs=in_specs, out_specs=out_specs, out_shape=out_shapes,
        scratch_shapes=scratch, compiler_params=_cp(sem),
    )(*ops)
    return (res[0], list(res[1:])) if nr else res[0]


def _swiglu(a, b):
    return jax.nn.silu(a) * b


def _ride_parts(refs, n_in, n_out, ride):
    nr = ride.n if ride is not None else 0
    ins = refs[:n_in]
    r_ins = refs[n_in:n_in + nr]
    outs = refs[n_in + nr:n_in + nr + n_out]
    r_outs = refs[n_in + nr + n_out:n_in + 2 * nr + n_out]
    return ins, r_ins, outs, r_outs, refs[n_in + 2 * nr + n_out:]


def _call_with_ride(body_core, grid, in_specs, ops, out_specs, out_shape, name, ride, scratch=(), sem=None):
    nr = ride.n if ride is not None else 0
    n_in, n_out, n_scr = len(in_specs), len(out_specs), len(scratch)

    def body(*refs):
        ins, r_ins, outs, r_outs, rest = _ride_parts(refs, n_in, n_out, ride)
        ids = [pl.program_id(ax) for ax in range(len(grid))]
        if nr:
            @pl.when(functools.reduce(lambda u, v: u & v, [i == 0 for i in ids]))
            def _():
                ride.start(r_ins, r_outs, rest[n_scr:])
        body_core(ins, outs, *rest[:n_scr])
        if nr:
            @pl.when(functools.reduce(lambda u, v: u & v, [i == g - 1 for i, g in zip(ids, grid)]))
            def _():
                ride.wait(r_ins, r_outs, rest[n_scr:])

    res = pl.pallas_call(
        body, name=name, grid=grid, in_specs=in_specs + (ride.specs if nr else []),
        out_specs=out_specs + (ride.specs if nr else []), out_shape=out_shape + (ride.out_shape if nr else []),
        scratch_shapes=list(scratch) + (ride.scratch if nr else []),
        compiler_params=_cp(("arbitrary",) * len(grid) if nr else (sem or ("parallel",) * len(grid))),
    )(*ops, *(ride.srcs if nr else []))
    return list(res[:n_out]), list(res[n_out:])


def _ffn_in(h, w, name, ride=None):
    t, d = h.shape
    cp = w.shape[2]
    bm, bn = _tile(t, 512), _tile(cp, 1408)
    nbs = cp // bn
    nn = (((1,), (0,)), ((), ()))

    def core(ins, outs):
        h_ref, wa_ref, wb_ref = ins
        ab_ref, act_ref = outs
        hv = h_ref[...].astype(bf16)
        a = lax.dot_general(hv, wa_ref[...].astype(bf16), nn, preferred_element_type=f32)
        b = lax.dot_general(hv, wb_ref[...].astype(bf16), nn, preferred_element_type=f32)
        ab_ref[0] = a.astype(bf16)
        ab_ref[1] = b.astype(bf16)
        act_ref[...] = _swiglu(a, b).astype(bf16)

    (ab, act), landed = _call_with_ride(
        core, (2 * nbs, t // bm),
        [pl.BlockSpec((bm, d), lambda j, i: (i, 0)),
         pl.BlockSpec((None, d, bn), lambda j, i: (j // nbs, 0, j % nbs)),
         pl.BlockSpec((None, d, bn), lambda j, i: (2 + j // nbs, 0, j % nbs))], [h, w, w],
        [pl.BlockSpec((2, bm, bn), lambda j, i: (0, i, j)), pl.BlockSpec((bm, bn), lambda j, i: (i, j))],
        [jax.ShapeDtypeStruct((2, t, 2 * cp), bf16), jax.ShapeDtypeStruct((t, 2 * cp), bf16)], name, ride)
    return ab, act, landed


def _ffn_out_dx(df, wo, ab, name, ride=None):
    t, d = df.shape
    ffp = wo.shape[0]
    bm, bn = _tile(t, 1024), _tile(ffp, 512)
    nt = (((1,), (1,)), ((), ()))

    def core(ins, outs):
        df_ref, wo_ref, ab_ref = ins
        dact = lax.dot_general(df_ref[...].astype(bf16), wo_ref[...].astype(bf16), nt, preferred_element_type=f32)
        _, vjp = jax.vjp(_swiglu, ab_ref[0].astype(f32), ab_ref[1].astype(f32))
        da, db = vjp(dact)
        outs[0][0] = da.astype(bf16)
        outs[0][1] = db.astype(bf16)

    (dab,), landed = _call_with_ride(
        core, (t // bm, ffp // bn),
        [pl.BlockSpec((bm, d), lambda i, j: (i, 0)), pl.BlockSpec((bn, d), lambda i, j: (j, 0)),
         pl.BlockSpec((2, bm, bn), lambda i, j: (0, i, j))], [df, wo, ab],
        [pl.BlockSpec((2, bm, bn), lambda i, j: (0, i, j))], [jax.ShapeDtypeStruct((2, t, ffp), bf16)], name, ride)
    return dab, landed


def _ffn_in_dx(dab, w, name, ride=None):
    _, t, cp2 = dab.shape
    d, cp = w.shape[1], w.shape[2]
    bm, bn, bk = _tile(t, 1024), _tile(d, 1024), _tile(cp, 2816)
    nkb = cp // bk
    return _mm_core(dab, w, 'nt', grid=(t // bm, d // bn, 4 * nkb),
                    a_blk=(None, bm, bk), a_map=lambda i, j, q: (q // (2 * nkb), i, q % (2 * nkb)),
                    b_blk=(None, bn, bk), b_map=lambda i, j, q: (q // nkb, j, q % nkb),
                    o_blk=(bm, bn), o_map=lambda i, j, q: (i, j),
                    out_shape=(t, d), out_dtype=bf16, name=name, ride=ride)


def _ffn_in_dw(h, dab, name):
    _, t, cp2 = dab.shape
    cp = cp2 // 2
    d = h.shape[1]
    bm, bn, bk = _tile(d, 1024), _tile(cp, 1408), _tile(t, 2048)
    nbs = cp // bn
    return _mm_core(h, dab, 'tn', grid=(d // bm, 4 * nbs, t // bk),
                    a_blk=(bk, bm), a_map=lambda i, j, q: (q, i),
                    b_blk=(None, bk, bn), b_map=lambda i, j, q: (j // (2 * nbs), q, j % (2 * nbs)),
                    o_blk=(None, bm, bn), o_map=lambda i, j, q: (j // nbs, i, j % nbs),
                    out_shape=(4, d, cp), out_dtype=bf16, name=name)


def _piece_blocks(pieces):
    bk = min(1024, functools.reduce(math.gcd, [p.shape[1] for p in pieces]))
    starts, n = [], 0
    for p in pieces:
        starts.append(n)
        n += p.shape[1] // bk
    return bk, starts, n


def _in_proj_dx(pieces, tail, tail_col, w, name, ride=None):
    t, d = pieces[0].shape[0], w.shape[0]
    bk, starts, nq = _piece_blocks(pieces)
    assert tail_col == nq * bk and tail.shape[1] == LANE
    bm, bn = _tile(t, 1024), _tile(d, 1024)
    npc = len(pieces)

    def core(ins, outs, acc_ref):
        a_refs, tail_ref, b_ref, bt_ref = ins[:npc], ins[npc], ins[npc + 1], ins[npc + 2]
        q = pl.program_id(2)

        @pl.when(q == 0)
        def _():
            acc_ref[...] = jnp.zeros_like(acc_ref)

        for a_ref, s0, p in zip(a_refs, starts, pieces):
            @pl.when((q >= s0) & (q < s0 + p.shape[1] // bk))
            def _(a_ref=a_ref):
                acc_ref[...] += lax.dot_general(a_ref[...].astype(bf16), b_ref[...], _NT, preferred_element_type=f32)

        @pl.when(q == nq)
        def _():
            r = acc_ref[...] + lax.dot_general(tail_ref[...].astype(bf16), bt_ref[...], _NT,
                                               preferred_element_type=f32)
            outs[0][...] = r.astype(bf16)

    def a_spec(s0, p):
        last = p.shape[1] // bk - 1
        return pl.BlockSpec((bm, bk), lambda i, j, q: (i, jnp.clip(q - s0, 0, last)))

    in_specs = [a_spec(s0, p) for s0, p in zip(starts, pieces)] + [
        pl.BlockSpec((bm, LANE), lambda i, j, q: (i, 0)),
        pl.BlockSpec((bn, bk), lambda i, j, q: (j, jnp.minimum(q, nq - 1))),
        pl.BlockSpec((bn, LANE), lambda i, j, q: (j, tail_col // LANE))]
    (dh,), landed = _call_with_ride(
        core, (t // bm, d // bn, nq + 1), in_specs, list(pieces) + [tail, w, w],
        [pl.BlockSpec((bm, bn), lambda i, j, q: (i, j))], [jax.ShapeDtypeStruct((t, d), bf16)], name, ride,
        scratch=[pltpu.VMEM((bm, bn), f32)], sem=("parallel", "parallel", "arbitrary"))
    return dh, landed


def _win(r):
    return r if isinstance(r, tuple) else (r, 0, r.shape[1])


def _row_tile(t, widths):
    per_row = 48 * max(widths)
    tm = 512
    while tm > SUBLANE and tm * per_row > RW_VMEM_BUDGET:
        tm //= 2
    return min(tm, t)


def _row_spec(r, tm):
    arr, c0, w = _win(r)
    assert c0 % w == 0, (c0, w)
    cb = c0 // w
    return pl.BlockSpec((tm, w), lambda i: (i, cb))


def _full_spec(p):
    nd = p.ndim
    return pl.BlockSpec(p.shape, lambda i: (0,) * nd)


def _rw(f, rows, params, outs, *, name, accs=(), tm=None):
    t = _win(rows[0])[0].shape[0]
    tm = tm or _row_tile(t, [_win(r)[2] for r in rows] + [w for w, _ in outs])
    nr, npar, no, na = len(rows), len(params), len(outs), len(accs)

    def body(*refs):
        vals = [r[...] for r in refs[:nr + npar]]
        res = f(*vals)
        res = res if isinstance(res, (tuple, list)) else (res,)
        for o_ref, v in zip(refs[nr + npar:nr + npar + no], res[:no]):
            o_ref[...] = v.astype(o_ref.dtype)
        if na:
            first = pl.program_id(0) == 0
            for a_ref, v in zip(refs[nr + npar + no:], res[no:]):
                @pl.when(first)
                def _(a_ref=a_ref):
                    a_ref[...] = jnp.zeros_like(a_ref)
                a_ref[...] += v

    out_shape = [jax.ShapeDtypeStruct((t, w), d) for w, d in outs] + [jax.ShapeDtypeStruct(s, f32) for s in accs]
    out_specs = [pl.BlockSpec((tm, w), lambda i: (i, 0)) for w, _ in outs] + \
                [pl.BlockSpec(s, lambda i: (0, 0)) for s in accs]
    return pl.pallas_call(
        body, name=name, grid=(t // tm,),
        in_specs=[_row_spec(r, tm) for r in rows] + [_full_spec(p) for p in params],
        out_specs=out_specs, out_shape=out_shape,
        compiler_params=_cp(("arbitrary",)),
    )(*[_win(r)[0] for r in rows], *params)


def _rw_vjp(f, rows, params, cots, *, row_grads, param_grads, name, tm=None, total_of=None):
    t = _win(rows[0])[0].shape[0]
    cot_rows = [c for c in cots if c is not None]
    tm = tm or _row_tile(t, [_win(r)[2] for r in rows] + [_win(c)[2] for c in cot_rows])
    nr, npar, ncot = len(rows), len(params), len(cot_rows)
    d_rows = [i for i, d in enumerate(row_grads) if d is not None]
    d_pars = [i for i, d in enumerate(param_grads) if d]

    def body(*refs):
        rv = [r[...] for r in refs[:nr]]
        pv = [r[...] for r in refs[nr:nr + npar]]
        cv = [r[...] for r in refs[nr + npar:nr + npar + ncot]]
        outs_r = refs[nr + npar + ncot:nr + npar + ncot + len(d_rows)]
        outs_p = refs[nr + npar + ncot + len(d_rows):nr + npar + ncot + len(d_rows) + len(d_pars)]

        def g(*diff):
            rr, pp = list(rv), list(pv)
            for i, v in zip(d_rows, diff[:len(d_rows)]):
                rr[i] = v
            for i, v in zip(d_pars, diff[len(d_rows):]):
                pp[i] = v
            res = f(*rr, *pp)
            return tuple(res) if isinstance(res, (tuple, list)) else (res,)

        prim, vjp = jax.vjp(g, *[rv[i] for i in d_rows], *[pv[i] for i in d_pars])
        it = iter(cv)
        cts = tuple(next(it).astype(o.dtype) if c is not None else jnp.zeros_like(o) for o, c in zip(prim, cots))
        grads = vjp(cts)
        for o_ref, v in zip(outs_r, grads[:len(d_rows)]):
            o_ref[...] = v.astype(o_ref.dtype)
        first = pl.program_id(0) == 0
        for o_ref, v in zip(outs_p, grads[len(d_rows):]):
            @pl.when(first)
            def _(o_ref=o_ref):
                o_ref[...] = jnp.zeros_like(o_ref)
            o_ref[...] += v.astype(f32)
        if total_of is not None:
            tot_ref = refs[-1]

            @pl.when(first)
            def _():
                tot_ref[...] = jnp.zeros_like(tot_ref)
            tot_ref[...] += jnp.broadcast_to(jnp.sum(prim[total_of].astype(f32)), tot_ref.shape)

    out_shape = [jax.ShapeDtypeStruct((t, _win(rows[i])[2]), row_grads[i]) for i in d_rows] + \
                [jax.ShapeDtypeStruct(params[i].shape, f32) for i in d_pars]
    out_specs = [pl.BlockSpec((tm, _win(rows[i])[2]), lambda i_: (i_, 0)) for i in d_rows] + \
                [_full_spec(params[i]) for i in d_pars]
    if total_of is not None:
        out_shape.append(jax.ShapeDtypeStruct((1, LANE), f32))
        out_specs.append(pl.BlockSpec((1, LANE), lambda i_: (0, 0)))
    return pl.pallas_call(
        body, name=name, grid=(t // tm,),
        in_specs=[_row_spec(r, tm) for r in rows] + [_full_spec(p) for p in params] + [_row_spec(c, tm) for c in cot_rows],
        out_specs=out_specs, out_shape=out_shape,
        compiler_params=_cp(("arbitrary",)),
    )(*[_win(r)[0] for r in rows], *params, *[_win(c)[0] for c in cot_rows])


def _rms(x, g):
    return x * lax.rsqrt(jnp.mean(x * x, axis=-1, keepdims=True) + EPS) * g


def _f_mod(x, nw, sh, sc):
    return (_rms(x, nw) * (1.0 + sc) + sh).astype(bf16)


def _f_mod_keep(x, nw, sh, sc):
    return _f_mod(x, nw, sh, sc), x


def _f_res_mod(coef, x, o, g, nw, sh, sc):
    x1 = x + coef * g * o.astype(f32)
    return x1, _f_mod(x1, nw, sh, sc)


def _times01(x, e):
    hi = x.astype(bf16)
    r1 = x - hi.astype(f32)
    mid = r1.astype(bf16)
    lo = (r1 - mid.astype(f32)).astype(bf16)
    return (jnp.dot(hi, e, preferred_element_type=f32) + jnp.dot(mid, e, preferred_element_type=f32) +
            jnp.dot(lo, e, preferred_element_type=f32))


@jax.custom_vjp
def _spread_heads(x, e, et):
    return _times01(x, e)


_spread_heads.defvjp(lambda x, e, et: (_times01(x, e), (e, et)),
                     lambda res, g: (_times01(g, res[1]), None, None))


def _f_ssd_pre(pre, dtraw, bias, e, et):
    xc = jax.nn.silu(pre)
    dtx = _spread_heads(jax.nn.softplus(dtraw + bias), e, et)
    return xc[:, :SSD_DI], xc[:, SSD_DI:SSD_DI + SSD_G * SSD_N], xc[:, SSD_DI + SSD_G * SSD_N:], dtx


def _f_ssd_post(y, z, nw):
    yz = y * jax.nn.silu(z)
    w = SSD_DI // SSD_G
    parts = []
    for g in range(SSD_G):
        s = yz[:, g * w:(g + 1) * w]
        parts.append(s * lax.rsqrt(jnp.mean(s * s, axis=-1, keepdims=True) + EPS))
    return (jnp.concatenate(parts, axis=1) * nw).astype(bf16)


def _f_s5_post(yb, u, d):
    return jax.nn.gelu(yb + d * u).astype(bf16)


def _f_merge(pa, glu, ga, gb):
    d = pa.shape[1]
    glu = glu.astype(f32)
    pb = glu[:, :d] * jax.nn.sigmoid(glu[:, d:])
    return (jax.nn.sigmoid(ga) * pa.astype(f32) + jax.nn.sigmoid(gb) * pb).astype(bf16)


def _f_final(x2, o, tgt, g, nw):
    x3 = x2 + 0.5 * g * o.astype(f32)
    y = _rms(x3, nw)
    return 0.5 * jnp.mean(jnp.square(y - tgt), axis=-1, keepdims=True)


def _f_s5_prep(lr, li, ldt, br, bi):
    dt = jnp.exp(ldt)
    lr = jnp.minimum(lr, -1e-4)
    mag = jnp.exp(lr * dt)
    ar = mag * jnp.cos(li * dt)
    ai = mag * jnp.sin(li * dt)
    den = lr * lr + li * li
    nr = ar - 1.0
    kr = (nr * lr + ai * li) / den
    ki = (ai * lr - nr * li) / den
    return ar, ai, kr * br - ki * bi, kr * bi + ki * br


def _shift_down(cur, halo8, j):
    if j == 0:
        return cur
    rolled = pltpu.roll(cur, j, 0)
    row8 = lax.broadcasted_iota(jnp.int32, halo8.shape, 0)
    top = jnp.where(row8 < j, pltpu.roll(halo8, j, 0), rolled[:SUBLANE])
    return jnp.concatenate([top, rolled[SUBLANE:]], axis=0)


def _shift_up(cur, halo8, j):
    if j == 0:
        return cur
    n = cur.shape[0]
    rolled = pltpu.roll(cur, n - j, 0)
    row8 = lax.broadcasted_iota(jnp.int32, halo8.shape, 0)
    bot = jnp.where(row8 >= SUBLANE - j, pltpu.roll(halo8, SUBLANE - j, 0), rolled[n - SUBLANE:])
    return jnp.concatenate([rolled[:n - SUBLANE], bot], axis=0)


def _conv_fwd(proj, c0, w8, b, name):
    t = proj.shape[0]
    cw = 1024
    tm = min(512, t)
    cb0 = c0 // cw
    r8 = tm // SUBLANE

    def body(x_ref, h_ref, w_ref, b_ref, o_ref):
        i = pl.program_id(1)
        x = x_ref[...]
        halo = jnp.where(i > 0, h_ref[...], 0.0)
        acc = b_ref[...] + w_ref[CONV_K - 1:CONV_K, :] * x
        for j in range(1, CONV_K):
            acc = acc + w_ref[CONV_K - 1 - j:CONV_K - j, :] * _shift_down(x, halo, j)
        o_ref[...] = acc

    return pl.pallas_call(
        body, name=name, grid=(CONV_DIM // cw, t // tm),
        in_specs=[pl.BlockSpec((tm, cw), lambda c, i: (i, cb0 + c)),
                  pl.BlockSpec((SUBLANE, cw), lambda c, i: (jnp.maximum(i * r8 - 1, 0), cb0 + c)),
                  pl.BlockSpec((SUBLANE, cw), lambda c, i: (0, c)),
                  pl.BlockSpec((1, cw), lambda c, i: (0, c))],
        out_specs=pl.BlockSpec((tm, cw), lambda c, i: (i, c)),
        out_shape=jax.ShapeDtypeStruct((t, CONV_DIM), f32),
        compiler_params=_cp(("parallel", "arbitrary")),
    )(proj, proj, w8, b)


def _conv_bwd(dpre, proj, c0, w8, name):
    t = proj.shape[0]
    cw = 1024
    tm = min(512, t)
    cb0 = c0 // cw
    r8 = tm // SUBLANE
    nb = t // tm

    def body(d_ref, dn_ref, x_ref, xh_ref, w_ref, dx_ref, dw_ref, db_ref):
        i = pl.program_id(1)
        d = d_ref[...]
        dn = jnp.where(i < nb - 1, dn_ref[...], 0.0)
        x = x_ref[...]
        xh = jnp.where(i > 0, xh_ref[...], 0.0)

        @pl.when(i == 0)
        def _():
            dw_ref[...] = jnp.zeros_like(dw_ref)
            db_ref[...] = jnp.zeros_like(db_ref)

        dx = w_ref[CONV_K - 1:CONV_K, :] * d
        rows = [jnp.sum(d * x, axis=0, keepdims=True)]
        for j in range(1, CONV_K):
            dx = dx + w_ref[CONV_K - 1 - j:CONV_K - j, :] * _shift_up(d, dn, j)
            rows.append(jnp.sum(d * _shift_down(x, xh, j), axis=0, keepdims=True))
        dx_ref[...] = dx.astype(dx_ref.dtype)
        dw = jnp.concatenate([rows[CONV_K - 1 - k] for k in range(CONV_K)] +
                             [jnp.zeros((SUBLANE - CONV_K, cw), f32)], axis=0)
        dw_ref[...] += dw
        db_ref[...] += jnp.sum(d, axis=0, keepdims=True)

    return pl.pallas_call(
        body, name=name, grid=(CONV_DIM // cw, nb),
        in_specs=[pl.BlockSpec((tm, cw), lambda c, i: (i, c)),
                  pl.BlockSpec((SUBLANE, cw), lambda c, i: (jnp.minimum((i + 1) * r8, nb * r8 - 1), c)),
                  pl.BlockSpec((tm, cw), lambda c, i: (i, cb0 + c)),
                  pl.BlockSpec((SUBLANE, cw), lambda c, i: (jnp.maximum(i * r8 - 1, 0), cb0 + c)),
                  pl.BlockSpec((SUBLANE, cw), lambda c, i: (0, c))],
        out_specs=[pl.BlockSpec((tm, cw), lambda c, i: (i, c)),
                   pl.BlockSpec((SUBLANE, cw), lambda c, i: (0, c)),
                   pl.BlockSpec((1, cw), lambda c, i: (0, c))],
        out_shape=[jax.ShapeDtypeStruct((t, CONV_DIM), bf16), jax.ShapeDtypeStruct((SUBLANE, CONV_DIM), f32),
                   jax.ShapeDtypeStruct((1, CONV_DIM), f32)],
        compiler_params=_cp(("parallel", "arbitrary")),
    )(dpre, dpre, proj, proj, w8)


def _cumsum_rows_impl(x):
    n = x.shape[0]
    row = lax.broadcasted_iota(jnp.int32, x.shape, 0)
    s = 1
    while s < n:
        x = x + jnp.where(row >= s, pltpu.roll(x, s, 0), 0.0)
        s *= 2
    return x


@jax.custom_vjp
def _cumsum_rows(x):
    return _cumsum_rows_impl(x)


def _cumsum_rows_bwd(_, g):
    c = _cumsum_rows_impl(g)
    return (c[c.shape[0] - 1:, :] - c + g,)


_cumsum_rows.defvjp(lambda x: (_cumsum_rows_impl(x), None), _cumsum_rows_bwd)


@jax.custom_vjp
def _swap_halves(t):
    return pltpu.roll(t, LANE // 2, 1)


_swap_halves.defvjp(lambda t: (pltpu.roll(t, LANE // 2, 1), None), lambda _, g: (pltpu.roll(g, LANE // 2, 1),))


def _ssd_chunk(xs, bm, cm, dtx, ax, dskx, ht):
    n = SSD_L
    assert n == LANE and SSD_P * 2 == LANE
    row = lax.broadcasted_iota(jnp.int32, (n, n), 0)
    col = lax.broadcasted_iota(jnp.int32, (n, n), 1)
    causal = row >= col
    lo = col < SSD_P
    cs = _cumsum_rows(dtx * ax)
    xdt = xs * dtx
    last = cs[n - 1:n, :]
    cb = lax.dot_general(cm.astype(bf16), bm.astype(bf16), (((1,), (1,)), ((), ())), preferred_element_type=f32)
    y_off = jnp.dot(cm.astype(bf16), ht.astype(bf16), preferred_element_type=f32) * jnp.exp(cs)
    st = lax.dot_general(bm.astype(bf16), (xdt * jnp.exp(last - cs)).astype(bf16), (((0,), (0,)), ((), ())),
                         preferred_element_type=f32)
    ht_new = jnp.exp(last) * ht + st
    ys = []
    for q in range(SSD_R // 2):
        tq = cs[:, q * LANE:(q + 1) * LANE]
        sw = _swap_halves(tq)
        tqt = tq.T
        xq = xdt[:, q * LANE:(q + 1) * LANE].astype(bf16)
        pair = []
        for c_col, r_row in ((jnp.where(lo, tq, sw), tqt[0:1, :]), (jnp.where(lo, sw, tq), tqt[SSD_P:SSD_P + 1, :])):
            decay = jnp.exp(jnp.where(causal, c_col - r_row, -1e30))
            pair.append(jnp.dot((cb * decay).astype(bf16), xq, preferred_element_type=f32))
        ys.append(jnp.where(lo, pair[0], pair[1]))
    return jnp.concatenate(ys, axis=1) + y_off + dskx * xs, ht_new


SSD_GB = 1


def _ssd_specs(nc, rev):
    ch = (lambda c: nc - 1 - c) if rev else (lambda c: c)
    gw = SSD_GB * SSD_R * SSD_P
    return [pl.BlockSpec((SSD_L, gw), lambda g, c: (ch(c), g)),
            pl.BlockSpec((SSD_L, SSD_GB * SSD_N), lambda g, c: (ch(c), g)),
            pl.BlockSpec((SSD_L, SSD_GB * SSD_N), lambda g, c: (ch(c), g)),
            pl.BlockSpec((SSD_L, gw), lambda g, c: (ch(c), g)),
            pl.BlockSpec((1, gw), lambda g, c: (0, g)),
            pl.BlockSpec((1, gw), lambda g, c: (0, g))]


def _ssd_group(refs, q):
    gw = SSD_R * SSD_P
    xs_ref, bm_ref, cm_ref, dt_ref, a_ref, dsk_ref = refs
    ln = slice(q * LANE, (q + 1) * LANE)
    wd = slice(q * gw, (q + 1) * gw)
    return (xs_ref[:, wd], bm_ref[:, ln], cm_ref[:, ln], dt_ref[:, wd], a_ref[:, wd], dsk_ref[:, wd])


def _ssd_fwd(xs, bm, cm, dt4, a4, dsk4, name, ride=None):
    t = xs.shape[0]
    nc = t // SSD_L
    gw = SSD_R * SSD_P

    nr = ride.n if ride is not None else 0
    ng = SSD_G // SSD_GB

    def body(*refs):
        xs_ref, bm_ref, cm_ref, dt_ref, a_ref, dsk_ref = refs[:6]
        r_ins = refs[6:6 + nr]
        y_ref, hs_ref = refs[6 + nr:8 + nr]
        r_outs = refs[8 + nr:8 + 2 * nr]
        h_ref = refs[8 + 2 * nr]
        r_sems = refs[9 + 2 * nr:]
        g, c = pl.program_id(0), pl.program_id(1)
        if nr:
            @pl.when((g == 0) & (c == 0))
            def _():
                ride.start(r_ins, r_outs, r_sems)

        @pl.when(c == 0)
        def _():
            h_ref[...] = jnp.zeros_like(h_ref)

        hs_ref[...] = h_ref[...]
        grp = (xs_ref, bm_ref, cm_ref, dt_ref, a_ref, dsk_ref)
        ops = [_ssd_group(grp, q) + (h_ref[:, q * gw:(q + 1) * gw],) for q in range(SSD_GB)]
        res = [_ssd_chunk(*o) for o in ops]
        for q, (y, hn) in enumerate(res):
            y_ref[:, q * gw:(q + 1) * gw] = y
            h_ref[:, q * gw:(q + 1) * gw] = hn

        if nr:
            @pl.when((g == ng - 1) & (c == nc - 1))
            def _():
                ride.wait(r_ins, r_outs, r_sems)

    res = pl.pallas_call(
        body, name=name, grid=(ng, nc), in_specs=_ssd_specs(nc, False) + (ride.specs if nr else []),
        out_specs=[pl.BlockSpec((SSD_L, SSD_GB * gw), lambda g, c: (c, g)),
                   pl.BlockSpec((None, None, SSD_N, SSD_GB * gw), lambda g, c: (g, c, 0, 0))] +
                  (ride.specs if nr else []),
        out_shape=[jax.ShapeDtypeStruct((t, SSD_DI), f32),
                   jax.ShapeDtypeStruct((ng, nc, SSD_N, SSD_GB * gw), f32)] + (ride.out_shape if nr else []),
        scratch_shapes=[pltpu.VMEM((SSD_N, SSD_GB * gw), f32)] + (ride.scratch if nr else []),
        compiler_params=_cp(("arbitrary", "arbitrary")),
    )(xs, bm, cm, dt4, a4, dsk4, *(ride.srcs if nr else []))
    return res[0], res[1], list(res[2:])


def _ssd_bwd(xs, bm, cm, dt4, a4, dsk4, hs, dy, name, ride=None):
    t = xs.shape[0]
    nc = t // SSD_L
    gw = SSD_R * SSD_P
    rc = lambda c: nc - 1 - c
    nr = ride.n if ride is not None else 0
    ng = SSD_G // SSD_GB

    def body(*refs):
        xs_ref, bm_ref, cm_ref, dt_ref, a_ref, dsk_ref, hs_ref, dy_ref = refs[:8]
        r_ins = refs[8:8 + nr]
        dxs_ref, dbm_ref, dcm_ref, ddt_ref, da_ref, ddsk_ref = refs[8 + nr:14 + nr]
        r_outs = refs[14 + nr:14 + 2 * nr]
        dh_ref = refs[14 + 2 * nr]
        r_sems = refs[15 + 2 * nr:]
        if nr:
            @pl.when((pl.program_id(0) == 0) & (pl.program_id(1) == 0))
            def _():
                ride.start(r_ins, r_outs, r_sems)

        @pl.when(pl.program_id(1) == 0)
        def _():
            dh_ref[...] = jnp.zeros_like(dh_ref)
            da_ref[...] = jnp.zeros_like(da_ref)
            ddsk_ref[...] = jnp.zeros_like(ddsk_ref)

        grp = (xs_ref, bm_ref, cm_ref, dt_ref, a_ref, dsk_ref)
        ops = [_ssd_group(grp, q) + (hs_ref[:, q * gw:(q + 1) * gw],) for q in range(SSD_GB)]
        cts = [(dy_ref[:, q * gw:(q + 1) * gw], dh_ref[:, q * gw:(q + 1) * gw]) for q in range(SSD_GB)]
        grads = [jax.vjp(_ssd_chunk, *o)[1](ct) for o, ct in zip(ops, cts)]
        for q, (dxs, dbm, dcm, ddt, da, ddsk, dh) in enumerate(grads):
            wd = slice(q * gw, (q + 1) * gw)
            ln = slice(q * LANE, (q + 1) * LANE)
            dxs_ref[:, wd] = dxs
            dbm_ref[:, ln] = dbm
            dcm_ref[:, ln] = dcm
            ddt_ref[:, wd] = ddt
            da_ref[:, wd] += da
            ddsk_ref[:, wd] += ddsk
            dh_ref[:, wd] = dh

        if nr:
            @pl.when((pl.program_id(0) == ng - 1) & (pl.program_id(1) == nc - 1))
            def _():
                ride.wait(r_ins, r_outs, r_sems)

    res = pl.pallas_call(
        body, name=name, grid=(ng, nc),
        in_specs=_ssd_specs(nc, True) + [
            pl.BlockSpec((None, None, SSD_N, SSD_GB * gw), lambda g, c: (g, rc(c), 0, 0)),
            pl.BlockSpec((SSD_L, SSD_GB * gw), lambda g, c: (rc(c), g))] + (ride.specs if nr else []),
        out_specs=[pl.BlockSpec((SSD_L, SSD_GB * gw), lambda g, c: (rc(c), g)),
                   pl.BlockSpec((SSD_L, SSD_GB * SSD_N), lambda g, c: (rc(c), g)),
                   pl.BlockSpec((SSD_L, SSD_GB * SSD_N), lambda g, c: (rc(c), g)),
                   pl.BlockSpec((SSD_L, SSD_GB * gw), lambda g, c: (rc(c), g)),
                   pl.BlockSpec((1, SSD_GB * gw), lambda g, c: (0, g)),
                   pl.BlockSpec((1, SSD_GB * gw), lambda g, c: (0, g))] + (ride.specs if nr else []),
        out_shape=[jax.ShapeDtypeStruct((t, SSD_DI), f32), jax.ShapeDtypeStruct((t, SSD_G * SSD_N), f32),
                   jax.ShapeDtypeStruct((t, SSD_G * SSD_N), f32), jax.ShapeDtypeStruct((t, SSD_DI), f32),
                   jax.ShapeDtypeStruct((1, SSD_DI), f32), jax.ShapeDtypeStruct((1, SSD_DI), f32)] +
                  (ride.out_shape if nr else []),
        scratch_shapes=[pltpu.VMEM((SSD_N, SSD_GB * gw), f32)] + (ride.scratch if nr else []),
        compiler_params=_cp(("arbitrary", "arbitrary")),
    )(xs, bm, cm, dt4, a4, dsk4, hs, dy, *(ride.srcs if nr else []))
    return list(res[:6]), list(res[6:])


S5_CH = 1024


S5_NB = 8
S5_UB = 128
S5_SB = 512
_NT = (((1,), (1,)), ((), ()))
_TN = (((0,), (0,)), ((), ()))


def _s5_fwd(proj, u0, bd_c, c_c, ar, ai, name):
    t = proj.shape[0]
    tb = min(128, t)
    ub = u0 // S5_W

    def body(u_ref, bd_ref, cc_ref, ar_ref, ai_ref, s_ref, yb_ref, bu_ref, carry):
        @pl.when(pl.program_id(0) == 0)
        def _():
            carry[...] = jnp.zeros_like(carry)

        u = u_ref[...].astype(bf16)
        for j in range(S5_NB):
            uj = u[:, j * S5_UB:(j + 1) * S5_UB]
            for half in range(2):
                bu_ref[:, half * S5_S + j * S5_SB:half * S5_S + (j + 1) * S5_SB] = jnp.dot(
                    uj, bd_ref[j * S5_UB:(j + 1) * S5_UB, half * S5_SB:(half + 1) * S5_SB], preferred_element_type=f32)

        for c0 in range(0, S5_S, S5_CH):
            re = pl.ds(c0, S5_CH)
            im = pl.ds(S5_S + c0, S5_CH)
            a_r = ar_ref[:, re]
            a_i = ai_ref[:, re]

            def step(k, st, re=re, im=im, a_r=a_r, a_i=a_i):
                sr, si = st
                row = pl.ds(k, 1)
                nr = a_r * sr - a_i * si + bu_ref[row, re]
                ni = a_r * si + a_i * sr + bu_ref[row, im]
                s_ref[row, re] = nr
                s_ref[row, im] = ni
                return nr, ni

            sr, si = lax.fori_loop(0, tb, step, (carry[:, re], carry[:, im]))
            carry[:, re] = sr
            carry[:, im] = si

        for j in range(S5_NB):
            lo, hi = j * S5_SB, (j + 1) * S5_SB
            yb_ref[:, j * S5_UB:(j + 1) * S5_UB] = (
                jnp.dot(s_ref[:, lo:hi].astype(bf16), cc_ref[lo:hi, :], preferred_element_type=f32) +
                jnp.dot(s_ref[:, S5_S + lo:S5_S + hi].astype(bf16), cc_ref[S5_S + lo:S5_S + hi, :],
                        preferred_element_type=f32))

    return pl.pallas_call(
        body, name=name, grid=(t // tb,),
        in_specs=[pl.BlockSpec((tb, S5_W), lambda i: (i, ub)), _full_spec(bd_c), _full_spec(c_c),
                  pl.BlockSpec((1, S5_S), lambda i: (0, 0)), pl.BlockSpec((1, S5_S), lambda i: (0, 0))],
        out_specs=[pl.BlockSpec((tb, 2 * S5_S), lambda i: (i, 0)), pl.BlockSpec((tb, S5_W), lambda i: (i, 0))],
        out_shape=[jax.ShapeDtypeStruct((t, 2 * S5_S), f32), jax.ShapeDtypeStruct((t, S5_W), f32)],
        scratch_shapes=[pltpu.VMEM((tb, 2 * S5_S), f32), pltpu.VMEM((1, 2 * S5_S), f32)],
        compiler_params=_cp(("arbitrary",)),
    )(proj, bd_c, c_c, ar, ai)


def _s5_bwd(dyb, s, proj, u0, bd_c, c_c, ar, ai, du_skip, name, ride=None):
    t = dyb.shape[0]
    tb = min(128, t)
    nb = t // tb
    r8 = tb // SUBLANE
    rb = lambda i: nb - 1 - i
    ub = u0 // S5_W

    def body(ins, outs, g_ref, carry):
        dyb_ref, s_ref, sh_ref, u_ref, skip_ref, bd_ref, cc_ref, ar_ref, ai_ref = ins
        du_ref, dar_ref, dai_ref, dbd_ref, dcc_ref = outs
        ds_ref = g_ref
        i = pl.program_id(0)

        @pl.when(i == 0)
        def _():
            carry[...] = jnp.zeros_like(carry)
            dar_ref[...] = jnp.zeros_like(dar_ref)
            dai_ref[...] = jnp.zeros_like(dai_ref)
            dbd_ref[...] = jnp.zeros_like(dbd_ref)
            dcc_ref[...] = jnp.zeros_like(dcc_ref)

        dyb = dyb_ref[...].astype(bf16)
        for jj in range(2 * S5_NB):
            blk = jj % S5_NB
            g_ref[:, jj * S5_SB:(jj + 1) * S5_SB] = lax.dot_general(
                dyb[:, blk * S5_UB:(blk + 1) * S5_UB], cc_ref[jj * S5_SB:(jj + 1) * S5_SB, :], _NT,
                preferred_element_type=f32)

        has_prev = (i < nb - 1).astype(f32)
        for c0 in range(0, S5_S, S5_CH):
            re = pl.ds(c0, S5_CH)
            im = pl.ds(S5_S + c0, S5_CH)
            a_r = ar_ref[:, re]
            a_i = ai_ref[:, re]

            def upd(st, row, sp_r, sp_i, re=re, im=im, a_r=a_r, a_i=a_i):
                gr, gi, acr, aci = st
                ngr = ds_ref[row, re] + a_r * gr + a_i * gi
                ngi = ds_ref[row, im] + a_r * gi - a_i * gr
                g_ref[row, re] = ngr
                g_ref[row, im] = ngi
                return ngr, ngi, acr + ngr * sp_r + ngi * sp_i, aci + ngi * sp_r - ngr * sp_i

            def step(k, st, re=re, im=im, upd=upd):
                tt = tb - 1 - k
                prev = pl.ds(tt - 1, 1)
                return upd(st, pl.ds(tt, 1), s_ref[prev, re], s_ref[prev, im])

            zero = jnp.zeros((1, S5_CH), f32)
            st = lax.fori_loop(0, tb - 1, step, (carry[:, re], carry[:, im], zero, zero))
            last = pl.ds(SUBLANE - 1, 1)
            gr, gi, acr, aci = upd(st, pl.ds(0, 1), sh_ref[last, re] * has_prev, sh_ref[last, im] * has_prev)
            carry[:, re] = gr
            carry[:, im] = gi
            dar_ref[:, re] += acr
            dai_ref[:, re] += aci

        u = u_ref[...].astype(bf16)
        for j in range(S5_NB):
            lo, hi = j * S5_SB, (j + 1) * S5_SB
            blk = slice(j * S5_UB, (j + 1) * S5_UB)
            g_re = g_ref[:, lo:hi].astype(bf16)
            g_im = g_ref[:, S5_S + lo:S5_S + hi].astype(bf16)
            du = (lax.dot_general(g_re, bd_ref[blk, :S5_SB], _NT, preferred_element_type=f32) +
                  lax.dot_general(g_im, bd_ref[blk, S5_SB:], _NT, preferred_element_type=f32) + skip_ref[:, blk])
            du_ref[:, blk] = du.astype(du_ref.dtype)
            dbd_ref[blk, :S5_SB] += lax.dot_general(u[:, blk], g_re, _TN, preferred_element_type=f32)
            dbd_ref[blk, S5_SB:] += lax.dot_general(u[:, blk], g_im, _TN, preferred_element_type=f32)
            dcc_ref[lo:hi, :] += lax.dot_general(s_ref[:, lo:hi].astype(bf16), dyb[:, blk], _TN,
                                                 preferred_element_type=f32)
            dcc_ref[S5_S + lo:S5_S + hi, :] += lax.dot_general(s_ref[:, S5_S + lo:S5_S + hi].astype(bf16), dyb[:, blk],
                                                               _TN, preferred_element_type=f32)

    row_blk = lambda w: pl.BlockSpec((tb, w), lambda i: (rb(i), 0))
    const = lambda shape: pl.BlockSpec(shape, lambda i: (0, 0))
    return _call_with_ride(
        body, (nb,),
        [row_blk(S5_W), row_blk(2 * S5_S),
         pl.BlockSpec((SUBLANE, 2 * S5_S), lambda i: (jnp.maximum(rb(i) * r8 - 1, 0), 0)),
         pl.BlockSpec((tb, S5_W), lambda i: (rb(i), ub)), row_blk(S5_W), const(bd_c.shape), const(c_c.shape),
         const((1, S5_S)), const((1, S5_S))],
        [dyb, s, s, proj, du_skip, bd_c, c_c, ar, ai],
        [row_blk(S5_W), const((1, S5_S)), const((1, S5_S)), const(bd_c.shape), const(c_c.shape)],
        [jax.ShapeDtypeStruct((t, S5_W), bf16), jax.ShapeDtypeStruct((1, S5_S), f32),
         jax.ShapeDtypeStruct((1, S5_S), f32), jax.ShapeDtypeStruct(bd_c.shape, f32),
         jax.ShapeDtypeStruct(c_c.shape, f32)],
        name, ride, scratch=[pltpu.VMEM((tb, 2 * S5_S), f32), pltpu.VMEM((1, 2 * S5_S), f32)], sem=("arbitrary",))


def _ada_fwd(c_all, w, b, name):
    d, n = w.shape
    tn = _tile(n, 1536)

    def body(c_ref, w_ref, b_ref, o_ref):
        a = jax.nn.silu(c_ref[...]).astype(bf16)
        o_ref[...] = jnp.dot(a, w_ref[...].astype(bf16), preferred_element_type=f32) + b_ref[...]

    return pl.pallas_call(
        body, name=name, grid=(n // tn,),
        in_specs=[pl.BlockSpec(c_all.shape, lambda j: (0, 0)), pl.BlockSpec((d, tn), lambda j: (0, j)),
                  pl.BlockSpec((1, tn), lambda j: (0, j))],
        out_specs=pl.BlockSpec((c_all.shape[0], tn), lambda j: (0, j)),
        out_shape=jax.ShapeDtypeStruct((c_all.shape[0], n), f32),
        compiler_params=_cp(("parallel",)),
    )(c_all, w, b)


def _ada_bwd(c_all, dm, name):
    d = c_all.shape[1]
    n = dm.shape[1]
    tn = _tile(n, 1536)

    def body(c_ref, dm_ref, o_ref):
        a = jax.nn.silu(c_ref[...]).astype(bf16)
        o_ref[...] = lax.dot_general(a, dm_ref[...].astype(bf16), (((0,), (0,)), ((), ())), preferred_element_type=f32)

    return pl.pallas_call(
        body, name=name, grid=(n // tn,),
        in_specs=[pl.BlockSpec(c_all.shape, lambda j: (0, 0)), pl.BlockSpec((dm.shape[0], tn), lambda j: (0, j))],
        out_specs=pl.BlockSpec((d, tn), lambda j: (0, j)),
        out_shape=jax.ShapeDtypeStruct((d, n), f32),
        compiler_params=_cp(("parallel",)),
    )(c_all, dm)


def _blk_rows(r, c, nbuf, itemsize=4):
    tr = _tile(r, max(SUBLANE, (RW_VMEM_BUDGET // (2 * nbuf * c * itemsize)) // 16 * 16), 16)
    return tr if r % tr == 0 else r


def _cast_bf16(w, name, cols=None):
    r, c = w.shape
    cols = cols or c
    tr = _blk_rows(r, cols, 2)

    def body(w_ref, o_ref):
        o_ref[:, :c] = w_ref[...].astype(bf16)
        if cols > c:
            o_ref[:, c:] = jnp.zeros((tr, cols - c), bf16)

    return pl.pallas_call(
        body, name=name, grid=(r // tr,), in_specs=[pl.BlockSpec((tr, c), lambda i: (i, 0))],
        out_specs=pl.BlockSpec((tr, cols), lambda i: (i, 0)), out_shape=jax.ShapeDtypeStruct((r, cols), bf16),
        compiler_params=_cp(("parallel",)),
    )(w)


def _sum_lead(parts, name, cols=None):
    n, r, c = parts.shape
    cols = cols or c
    tr = _blk_rows(r, c, n + 2)

    def body(p_ref, o_ref):
        acc = p_ref[0].astype(f32)
        for q in range(1, n):
            acc = acc + p_ref[q].astype(f32)
        o_ref[...] = acc[:, :cols]

    return pl.pallas_call(
        body, name=name, grid=(r // tr,), in_specs=[pl.BlockSpec((n, tr, c), lambda i: (0, i, 0))],
        out_specs=pl.BlockSpec((tr, cols), lambda i: (i, 0)), out_shape=jax.ShapeDtypeStruct((r, cols), f32),
        compiler_params=_cp(("parallel",)),
    )(parts)


def _adamw(w, m, v, parts, name):
    r, c = w.shape
    npart = len(parts)
    tr = _blk_rows(r, c, 7 + npart)
    c1 = 1.0 - ADAM_B1 ** ADAM_STEP
    c2 = 1.0 - ADAM_B2 ** ADAM_STEP

    def body(*refs):
        w_ref, m_ref, v_ref = refs[:3]
        g_ref, d_ref, nm_ref, nv_ref = refs[3 + npart:]
        g = refs[3][...].astype(f32)
        for p in refs[4:3 + npart]:
            g = g + p[...].astype(f32)
        nm = ADAM_B1 * m_ref[...] + (1.0 - ADAM_B1) * g
        nv = ADAM_B2 * v_ref[...] + (1.0 - ADAM_B2) * jnp.square(g)
        g_ref[...] = g
        nm_ref[...] = nm
        nv_ref[...] = nv
        d_ref[...] = -ADAM_LR * ((nm / c1) / (jnp.sqrt(nv / c2) + ADAM_EPS) + ADAM_WD * w_ref[...])

    spec = pl.BlockSpec((tr, c), lambda i: (i, 0))
    return pl.pallas_call(
        body, name=name, grid=(r // tr,), in_specs=[spec] * (3 + npart), out_specs=[spec] * 4,
        out_shape=[jax.ShapeDtypeStruct((r, c), f32)] * 4, compiler_params=_cp(("parallel",)),
    )(w, m, v, *parts)


def _ag_small(x_shard, name):
    m_per, n = x_shard.shape

    def body(x_ref, out_ref, send_sems, recv_sems, local_sem):
        x, y, c = lax.axis_index("x"), lax.axis_index("y"), lax.axis_index("c")
        me, sibling = (x, y, c), (x, y, 1 - c)
        chips = [(1 - x, y), (x, 1 - y), (1 - x, 1 - y)]

        def rows(px, py, pc):
            return out_ref.at[pl.ds((4 * px + 2 * py + pc) * m_per, m_per), :]

        def copy(k, block, to, src=None):
            return pltpu.make_async_remote_copy(
                src_ref=rows(*block) if src is None else src, dst_ref=rows(*block),
                send_sem=send_sems.at[k], recv_sem=recv_sems.at[k], device_id=to, device_id_type=MESH)

        mine = pltpu.make_async_copy(x_ref, rows(*me), local_sem)
        mine.start()
        first = [copy(0, me, sibling, src=x_ref)]
        first += [copy(1 + j, me, (*chip, c), src=x_ref) for j, chip in enumerate(chips)]
        for cp in first:
            cp.start()
        passed = [copy(4 + j, (*chip, c), sibling) for j, chip in enumerate(chips)]
        for j, chip in enumerate(chips):
            copy(1 + j, (*chip, c), me).wait_recv()
            passed[j].start()
        copy(0, sibling, me).wait_recv()
        for j, chip in enumerate(chips):
            copy(4 + j, (*chip, 1 - c), me).wait_recv()
        for cp in first + passed:
            cp.wait_send()
        mine.wait()

    return pl.pallas_call(
        body, name=name, out_shape=jax.ShapeDtypeStruct((8 * m_per, n), x_shard.dtype),
        in_specs=[pl.BlockSpec(memory_space=pltpu.VMEM)], out_specs=pl.BlockSpec(memory_space=pltpu.VMEM),
        scratch_shapes=[pltpu.SemaphoreType.DMA((7,)), pltpu.SemaphoreType.DMA((7,)), pltpu.SemaphoreType.DMA],
        compiler_params=pltpu.CompilerParams(vmem_limit_bytes=VMEM_LIMIT),
    )(x_shard)


def _run_ride(ride, name):
    n = ride.n

    def body(*refs):
        ride.start(refs[:n], refs[n:2 * n], refs[2 * n:])
        ride.wait(refs[:n], refs[n:2 * n], refs[2 * n:])

    return pl.pallas_call(
        body, name=name, out_shape=ride.out_shape, in_specs=ride.specs, out_specs=ride.specs,
        scratch_shapes=ride.scratch,
    )(*ride.srcs)


class _Ride:
    def __init__(self, srcs, scatter):
        self.srcs, self.scatter, self.n = list(srcs), scatter, len(srcs)
        n = self.n
        self.out_shape = [jax.ShapeDtypeStruct(s.shape if scatter else (4,) + s.shape, s.dtype) for s in srcs]
        self.specs = [pl.BlockSpec(memory_space=pl.ANY)] * n
        self.scratch = [pltpu.SemaphoreType.DMA((3 * n,)), pltpu.SemaphoreType.DMA((3 * n,)),
                        pltpu.SemaphoreType.DMA((n,))]

    def _copies(self, ins, outs, sems):
        send_sems, recv_sems, local_sems = sems
        x, y, c = lax.axis_index("x"), lax.axis_index("y"), lax.axis_index("c")
        my_k = 2 * x + y
        peers = [(1 - x, y), (x, 1 - y), (1 - x, 1 - y)]
        local, sends, recvs = [], [], []
        for a in range(self.n):
            own = ins[a].at[my_k] if self.scatter else ins[a]
            local.append(pltpu.make_async_copy(own, outs[a].at[my_k], local_sems.at[a]))
            for j, (px, py) in enumerate(peers):
                sems_j = dict(send_sem=send_sems.at[3 * a + j], recv_sem=recv_sems.at[3 * a + j],
                              device_id=(px, py, c), device_id_type=MESH)
                src = ins[a].at[2 * px + py] if self.scatter else ins[a]
                sends.append(pltpu.make_async_remote_copy(src_ref=src, dst_ref=outs[a].at[my_k], **sems_j))
                landed = outs[a].at[2 * px + py]
                recvs.append(pltpu.make_async_remote_copy(src_ref=landed, dst_ref=landed, **sems_j))
        return local, sends, recvs

    def start(self, ins, outs, sems):
        local, sends, _ = self._copies(ins, outs, sems)
        for cp in local + sends:
            cp.start()

    def wait(self, ins, outs, sems):
        local, sends, recvs = self._copies(ins, outs, sems)
        for cp in recvs:
            cp.wait_recv()
        for cp in sends:
            cp.wait_send()
        for cp in local:
            cp.wait()


class _RideGather:
    def __init__(self, srcs):
        self.srcs, self.n = list(srcs), len(srcs)
        n = self.n
        assert all(s.shape[0] % 32 == 0 for s in srcs)
        self.out_shape = [jax.ShapeDtypeStruct((4,) + s.shape, s.dtype) for s in srcs]
        self.specs = [pl.BlockSpec(memory_space=pl.ANY)] * n
        dma = pltpu.SemaphoreType.DMA
        self.scratch = [dma((3 * n,)), dma((3 * n,)), dma((3 * n,)), dma((3 * n,)), dma((n,))]

    def _copies(self, ins, outs, sems):
        send_sems, recv_sems, pass_send, pass_recv, local_sems = sems
        x, y, c = lax.axis_index("x"), lax.axis_index("y"), lax.axis_index("c")
        my_k = 2 * x + y
        peers = [(1 - x, y), (x, 1 - y), (1 - x, 1 - y)]
        local, sends, recvs, passes, pass_recvs = [], [], [], [], []
        for a in range(self.n):
            half = self.srcs[a].shape[0] // 2
            mine = pl.ds(pl.multiple_of(c * half, 16), half)
            other = pl.ds(pl.multiple_of((1 - c) * half, 16), half)
            local.append(pltpu.make_async_copy(ins[a], outs[a].at[my_k], local_sems.at[a]))
            for j, (px, py) in enumerate(peers):
                q = 3 * a + j
                over_ici = dict(send_sem=send_sems.at[q], recv_sem=recv_sems.at[q], device_id=(px, py, c),
                                device_id_type=MESH)
                to_sibling = dict(send_sem=pass_send.at[q], recv_sem=pass_recv.at[q], device_id=(x, y, 1 - c),
                                  device_id_type=MESH)
                sends.append(pltpu.make_async_remote_copy(src_ref=ins[a].at[mine], dst_ref=outs[a].at[my_k, mine],
                                                          **over_ici))
                landed = outs[a].at[2 * px + py, mine]
                recvs.append(pltpu.make_async_remote_copy(src_ref=landed, dst_ref=landed, **over_ici))
                passes.append(pltpu.make_async_remote_copy(src_ref=landed, dst_ref=landed, **to_sibling))
                from_sibling = outs[a].at[2 * px + py, other]
                pass_recvs.append(pltpu.make_async_remote_copy(src_ref=from_sibling, dst_ref=from_sibling, **to_sibling))
        return local, sends, recvs, passes, pass_recvs

    def start(self, ins, outs, sems):
        local, sends = self._copies(ins, outs, sems)[:2]
        for cp in local + sends:
            cp.start()

    def wait(self, ins, outs, sems):
        local, sends, recvs, passes, pass_recvs = self._copies(ins, outs, sems)
        for rc, ps in zip(recvs, passes):
            rc.wait_recv()
            ps.start()
        for cp in pass_recvs:
            cp.wait_recv()
        for cp in sends + passes:
            cp.wait_send()
        for cp in local:
            cp.wait()


class _RideSwap:
    def __init__(self, srcs):
        self.srcs, self.n = list(srcs), len(srcs)
        self.out_shape = [jax.ShapeDtypeStruct(s.shape, s.dtype) for s in srcs]
        self.specs = [pl.BlockSpec(memory_space=pl.ANY)] * self.n
        self.scratch = [pltpu.SemaphoreType.DMA((self.n,)), pltpu.SemaphoreType.DMA((self.n,))]

    def _copies(self, ins, outs, sems):
        send_sems, recv_sems = sems
        sib = (lax.axis_index("x"), lax.axis_index("y"), 1 - lax.axis_index("c"))
        return [pltpu.make_async_remote_copy(src_ref=ins[a], dst_ref=outs[a], send_sem=send_sems.at[a],
                                             recv_sem=recv_sems.at[a], device_id=sib, device_id_type=MESH)
                for a in range(self.n)]

    def start(self, ins, outs, sems):
        for cp in self._copies(ins, outs, sems):
            cp.start()

    def wait(self, ins, outs, sems):
        cps = self._copies(ins, outs, sems)
        for cp in cps:
            cp.wait_recv()
        for cp in cps:
            cp.wait_send()


class _Rides:
    def __init__(self, rides):
        self.rides = list(rides)
        self.n = sum(r.n for r in self.rides)
        self.srcs = [s for r in self.rides for s in r.srcs]
        self.out_shape = [s for r in self.rides for s in r.out_shape]
        self.specs = [s for r in self.rides for s in r.specs]
        self.scratch = [s for r in self.rides for s in r.scratch]

    def _each(self, ins, outs, sems):
        i = k = 0
        for r in self.rides:
            yield r, ins[i:i + r.n], outs[i:i + r.n], sems[k:k + len(r.scratch)]
            i += r.n
            k += len(r.scratch)

    def start(self, ins, outs, sems):
        for r, a, b, c in self._each(ins, outs, sems):
            r.start(a, b, c)

    def wait(self, ins, outs, sems):
        for r, a, b, c in self._each(ins, outs, sems):
            r.wait(a, b, c)

    def split(self, results):
        out, i = [], 0
        for r in self.rides:
            out.append(results[i:i + r.n])
            i += r.n
        return out


def _gather8(vec, name):
    size = vec.shape[0]
    n = _round_up(size, SUBLANE * LANE)
    blk = jnp.concatenate([vec, jnp.zeros((n - size,), f32)]).reshape(SUBLANE, n // SUBLANE)
    out = _ag_small(blk, name)
    return out.reshape(8, n)[:, :size]


def _head_spread_matrices():
    e = np.zeros((LANE, SSD_DI), np.float32)
    for h in range(SSD_HEADS):
        e[h, h * SSD_P:(h + 1) * SSD_P] = 1.0
    return jnp.asarray(e, bf16), jnp.asarray(e.T, bf16)


def _heads_to_lanes(v):
    return jnp.repeat(v, SSD_P).reshape(1, SSD_DI)


def _block_diag8(blocks):
    g, r, c = blocks.shape
    b = blocks.reshape(g // S5_NB, S5_NB, r, c)
    eye = jnp.eye(S5_NB, dtype=bool)[None, :, None, :, None]
    return jnp.where(eye, b[:, :, :, None, :], jnp.zeros((), blocks.dtype)).reshape(g * r, S5_NB * c)


def _diag8(mat, r, c):
    g = mat.shape[0] // r
    m = mat.reshape(g // S5_NB, S5_NB, r, S5_NB, c)
    eye = jnp.eye(S5_NB, dtype=bool)[None, :, None, :, None]
    return jnp.where(eye, m, 0.0).sum(axis=3).reshape(g, r, c)


class _Layout:
    def __init__(self, d):
        self.d = d
        self.z, self.xbc, self.u = 0, SSD_DI, SSD_DI + CONV_DIM
        self.ga = self.u + S5_W
        self.gb = self.ga + d
        self.dt = self.gb + d
        self.np_ = self.dt + LANE
        self.in_cols = SSD_DI + CONV_DIM + SSD_HEADS + S5_W + 2 * d
        off_dt = SSD_DI + CONV_DIM
        off_u = off_dt + SSD_HEADS
        self.src = [(0, off_dt), (off_u, off_u + S5_W + 2 * d), (off_dt, off_u)]

    def arrange_slabs(self, g):
        pieces = [p for lo, hi in self.src for p in _cols_from_slabs(g, lo, hi)]
        pieces.append(jnp.zeros((g.shape[1], LANE - SSD_HEADS), g.dtype))
        return jnp.concatenate(pieces, axis=1)

    def restore_slabs(self, chunks):
        (a0, a1), (b0, b1), (c0, c1) = self.src
        n_a, n_b = a1 - a0, b1 - b0
        segs = [(a0, a1, 0), (c0, c1, n_a + n_b), (b0, b1, n_a)]
        firsts = np.cumsum([0] + [c.shape[1] for c in chunks])

        def take(lo, hi):
            return [c[:, max(lo, f) - f:min(hi, f + c.shape[1]) - f] for c, f in zip(chunks, firsts)
                    if max(lo, f) < min(hi, f + c.shape[1])]

        cs = self.in_cols // 4
        slabs = []
        for k in range(4):
            lo, hi = k * cs, (k + 1) * cs
            parts = [p for s0, s1, pos in segs if max(lo, s0) < min(hi, s1)
                     for p in take(pos + max(lo, s0) - s0, pos + min(hi, s1) - s0)]
            slabs.append(jnp.concatenate(parts, axis=1))
        return jnp.stack(slabs)


def _cols_from_slabs(g, start, stop):
    c = g.shape[2]
    return [g[k][:, max(start, k * c) - k * c:min(stop, (k + 1) * c) - k * c] for k in range(4)
            if max(start, k * c) < min(stop, (k + 1) * c)]


def _unshard_cols(g):
    return jnp.concatenate([g[k] for k in range(4)], axis=1)


def _shard_cols(w):
    r, c4 = w.shape
    return w.reshape(r, 4, c4 // 4).transpose(1, 0, 2)


def kernel(x, c, w_ada, b_ada, norm_ffn1, w_ffn1_in, w_ffn1_out, norm_mix, w_in, conv_w, conv_b, dt_bias, a_log, d_ssd, ssd_norm_w, w_a_proj, s5_lambda_re, s5_lambda_im, s5_b_re, s5_b_im, s5_c_re, s5_c_im, s5_d, s5_log_dt, w_b_glu, w_out, norm_ffn2, w_ffn2_in, w_ffn2_out, norm_final, loss_target, m_w_ada, m_b_ada, m_norm_ffn1, m_w_ffn1_in, m_w_ffn1_out, m_norm_mix, m_w_in, m_conv_w, m_conv_b, m_dt_bias, m_a_log, m_d_ssd, m_ssd_norm_w, m_w_a_proj, m_s5_lambda_re, m_s5_lambda_im, m_s5_b_re, m_s5_b_im, m_s5_c_re, m_s5_c_im, m_s5_d, m_s5_log_dt, m_w_b_glu, m_w_out, m_norm_ffn2, m_w_ffn2_in, m_w_ffn2_out, m_norm_final, v_w_ada, v_b_ada, v_norm_ffn1, v_w_ffn1_in, v_w_ffn1_out, v_norm_mix, v_w_in, v_conv_w, v_conv_b, v_dt_bias, v_a_log, v_d_ssd, v_ssd_norm_w, v_w_a_proj, v_s5_lambda_re, v_s5_lambda_im, v_s5_b_re, v_s5_b_im, v_s5_c_re, v_s5_c_im, v_s5_d, v_s5_log_dt, v_w_b_glu, v_w_out, v_norm_ffn2, v_w_ffn2_in, v_w_ffn2_out, v_norm_final):
    W = dict(w_ada=w_ada, b_ada=b_ada, norm_ffn1=norm_ffn1, w_ffn1_in=w_ffn1_in, w_ffn1_out=w_ffn1_out, norm_mix=norm_mix, w_in=w_in, conv_w=conv_w, conv_b=conv_b, dt_bias=dt_bias, a_log=a_log, d_ssd=d_ssd, ssd_norm_w=ssd_norm_w, w_a_proj=w_a_proj, s5_lambda_re=s5_lambda_re, s5_lambda_im=s5_lambda_im, s5_b_re=s5_b_re, s5_b_im=s5_b_im, s5_c_re=s5_c_re, s5_c_im=s5_c_im, s5_d=s5_d, s5_log_dt=s5_log_dt, w_b_glu=w_b_glu, w_out=w_out, norm_ffn2=norm_ffn2, w_ffn2_in=w_ffn2_in, w_ffn2_out=w_ffn2_out, norm_final=norm_final)
    Mo = dict(w_ada=m_w_ada, b_ada=m_b_ada, norm_ffn1=m_norm_ffn1, w_ffn1_in=m_w_ffn1_in, w_ffn1_out=m_w_ffn1_out, norm_mix=m_norm_mix, w_in=m_w_in, conv_w=m_conv_w, conv_b=m_conv_b, dt_bias=m_dt_bias, a_log=m_a_log, d_ssd=m_d_ssd, ssd_norm_w=m_ssd_norm_w, w_a_proj=m_w_a_proj, s5_lambda_re=m_s5_lambda_re, s5_lambda_im=m_s5_lambda_im, s5_b_re=m_s5_b_re, s5_b_im=m_s5_b_im, s5_c_re=m_s5_c_re, s5_c_im=m_s5_c_im, s5_d=m_s5_d, s5_log_dt=m_s5_log_dt, w_b_glu=m_w_b_glu, w_out=m_w_out, norm_ffn2=m_norm_ffn2, w_ffn2_in=m_w_ffn2_in, w_ffn2_out=m_w_ffn2_out, norm_final=m_norm_final)
    Vo = dict(w_ada=v_w_ada, b_ada=v_b_ada, norm_ffn1=v_norm_ffn1, w_ffn1_in=v_w_ffn1_in, w_ffn1_out=v_w_ffn1_out, norm_mix=v_norm_mix, w_in=v_w_in, conv_w=v_conv_w, conv_b=v_conv_b, dt_bias=v_dt_bias, a_log=v_a_log, d_ssd=v_d_ssd, ssd_norm_w=v_ssd_norm_w, w_a_proj=v_w_a_proj, s5_lambda_re=v_s5_lambda_re, s5_lambda_im=v_s5_lambda_im, s5_b_re=v_s5_b_re, s5_b_im=v_s5_b_im, s5_c_re=v_s5_c_re, s5_c_im=v_s5_c_im, s5_d=v_s5_d, s5_log_dt=v_s5_log_dt, w_b_glu=v_w_b_glu, w_out=v_w_out, norm_ffn2=v_norm_ffn2, w_ffn2_in=v_w_ffn2_in, w_ffn2_out=v_w_ffn2_out, norm_final=v_norm_final)

    t, d = x.shape[1], x.shape[2]
    ff = 4 * w_ffn1_out.shape[1]
    hf = ff // 2
    hfp = _round_up(hf, LANE)
    lay = _Layout(d)
    xi, yi, ci = lax.axis_index("x"), lax.axis_index("y"), lax.axis_index("c")
    k_me = 2 * xi + yi
    e_me = 4 * xi + 2 * yi + ci
    x2d = x[0]
    tgt = loss_target[0]

    cw_cols = conv_w.shape[2]
    g1 = _gather8(jnp.concatenate([c[0], conv_w[0].reshape(-1)]), "gather_c_convw")
    c_all = g1[:, :d]
    conv_full = g1[::2, d:].reshape(4, CONV_K, cw_cols).transpose(1, 0, 2).reshape(CONV_K, CONV_DIM)
    conv_w8 = jnp.zeros((SUBLANE, CONV_DIM), f32).at[:CONV_K].set(conv_full)

    n_ada_loc = w_ada.shape[2]
    b_loc = lax.dynamic_slice(b_ada, (0, k_me * n_ada_loc), (1, n_ada_loc))
    mods_part = _ada_fwd(c_all, w_ada[0], b_loc, "ada_fwd")
    g2 = _gather8(mods_part.reshape(-1), "gather_mods").reshape(8, 8, n_ada_loc)
    mods = lax.dynamic_index_in_dim(g2[::2], e_me, axis=1, keepdims=False).reshape(N_ADA, d)
    sh1, sc1, gt1, sh2, sc2, gt2, sh3, sc3, gt3 = [mods[i:i + 1] for i in range(N_ADA)]

    cast = {n: _cast_bf16(W[n][0], "cast_" + n, cols=hfp if n in ('w_ffn1_in', 'w_ffn2_in') else None)
            for n in BIG}

    def gather_of(names):
        return _RideGather([cast[n] for n in names])

    def rows_of(g):
        return g.reshape(4 * g.shape[1], g.shape[2])

    def ffn_out(g):
        z = jnp.zeros((hfp - hf, g.shape[2]), g.dtype)
        return jnp.concatenate([g[0], g[1], z, g[2], g[3], z], axis=0)

    nf1, nmx, nf2 = norm_ffn1, norm_mix, norm_ffn2
    nfin = norm_final.reshape(1, d)

    (g_w1i,) = _run_ride(gather_of(['w_ffn1_in']), "gather_w_ffn1_in")
    w1i = g_w1i
    (h1,) = _rw(_f_mod, [x2d], [nf1, sh1, sc1], [(d, bf16)], name="mod1")
    ab1, act1, (g_w1o, g_win) = _ffn_in(h1, w1i, "ffn1_in", ride=gather_of(['w_ffn1_out', 'w_in']))
    w1o = ffn_out(g_w1o)
    w_inr = lay.arrange_slabs(g_win)
    f1, (g_wa, g_wglu, g_wo) = _mm(act1, w1o, 'nn', out_dtype=bf16, name="ffn1_out",
                                   ride=gather_of(['w_a_proj', 'w_b_glu', 'w_out']))
    w_a = rows_of(g_wa)
    w_glu, w_o = _unshard_cols(g_wglu), rows_of(g_wo)
    res1 = functools.partial(_f_res_mod, 0.5)
    x1, h2 = _rw(res1, [x2d, f1], [gt1, nmx, sh2, sc2], [(d, f32), (d, bf16)], name="res1_mod2")
    proj, (g_w2i,) = _mm(h2, w_inr, 'nn', out_dtype=f32, name="in_proj", ride=gather_of(['w_ffn2_in']))
    w2i = g_w2i

    pre = _conv_fwd(proj, lay.xbc, conv_w8, conv_b, "conv_fwd")
    spread, spread_t = _head_spread_matrices()
    bias128 = jnp.zeros((1, LANE), f32).at[:, :SSD_HEADS].set(dt_bias)
    xs, bm, cm, dt4 = _rw(_f_ssd_pre, [pre, (proj, lay.dt, LANE)], [bias128, spread, spread_t],
                          [(SSD_DI, f32), (SSD_G * SSD_N, f32), (SSD_G * SSD_N, f32), (SSD_DI, f32)],
                          name="ssd_pre")

    def head_params(a_log_, d_ssd_):
        return _heads_to_lanes(-jnp.exp(a_log_[0])), _heads_to_lanes(d_ssd_[0])

    (a4, dsk4), head_vjp = jax.vjp(head_params, a_log, d_ssd)
    y_ssd, hs, (g_w2o,) = _ssd_fwd(xs, bm, cm, dt4, a4, dsk4, "ssd_fwd", ride=gather_of(['w_ffn2_out']))
    w2o = ffn_out(g_w2o)
    (y_a,) = _rw(_f_ssd_post, [y_ssd, (proj, lay.z, SSD_DI)], [ssd_norm_w], [(SSD_DI, bf16)], name="ssd_post")
    p_a = _mm(y_a, w_a, 'nn', out_dtype=bf16, name="a_proj")

    col = lambda v: v.reshape(S5_S, 1)
    ldt_col = jnp.repeat(s5_log_dt[0], S5_P).reshape(S5_S, 1)
    prep_rows = [col(s5_lambda_re[0]), col(s5_lambda_im[0]), ldt_col,
                 s5_b_re[0].reshape(S5_S, S5_I), s5_b_im[0].reshape(S5_S, S5_I)]
    ar, ai, bbr, bbi = _rw(_f_s5_prep, prep_rows, [], [(1, f32), (1, f32), (S5_I, f32), (S5_I, f32)],
                           name="s5_prep", tm=512)
    to_bd = lambda bb: _block_diag8(bb.reshape(S5_G, S5_P, S5_I).transpose(0, 2, 1).astype(bf16))
    bd_c = jnp.concatenate([to_bd(bbr), to_bd(bbi)], axis=1)
    c_c = jnp.concatenate([_block_diag8(s5_c_re[0].transpose(0, 2, 1).astype(bf16)),
                           _block_diag8((-s5_c_im[0]).transpose(0, 2, 1).astype(bf16))], axis=0)
    ar_row, ai_row = ar.reshape(1, S5_S), ai.reshape(1, S5_S)
    s5s, yb = _s5_fwd(proj, lay.u, bd_c, c_c, ar_row, ai_row, "s5_fwd")
    d_row = s5_d[0].reshape(1, S5_W)
    (gl,) = _rw(_f_s5_post, [yb, (proj, lay.u, S5_W)], [d_row], [(S5_W, bf16)], name="s5_post")
    glu = _mm(gl, w_glu, 'nn', out_dtype=bf16, name="glu_proj")

    merge_rows = [p_a, glu, (proj, lay.ga, d), (proj, lay.gb, d)]
    (merged,) = _rw(_f_merge, merge_rows, [], [(d, bf16)], name="merge")
    o_mix = _mm(merged, w_o, 'nn', out_dtype=bf16, name="out_proj")
    res2 = functools.partial(_f_res_mod, 1.0)
    x2, h3 = _rw(res2, [x1, o_mix], [gt2, nf2, sh3, sc3], [(d, f32), (d, bf16)], name="res2_mod3")
    ab2, act2, _ = _ffn_in(h3, w2i, "ffn2_in")
    f2 = _mm(act2, w2o, 'nn', out_dtype=bf16, name="ffn2_out")

    ones = jnp.ones((t, 1), f32)
    dx2, df2, dgt3, dnfin, loss_acc = _rw_vjp(_f_final, [x2, f2, tgt], [gt3, nfin], [ones],
                                              row_grads=[f32, bf16, None], param_grads=[True, True],
                                              name="loss_and_bwd", total_of=0)
    loss = lax.psum(loss_acc[0, 0], AXES)
    def ffn_out_back(g):
        return jnp.concatenate([g[:hf], g[hfp:hfp + hf]], axis=0).reshape(4, ff // 4, g.shape[1])

    def rows_back(g, rows):
        return g.reshape(4, rows // 4, g.shape[1])

    def scatter_of(pairs):
        return _Ride([g for _, g in pairs], True)

    terms = {}
    dab2, _ = _ffn_out_dx(df2, w2o, ab2, "ffn2_out_dx")
    dw2o = _mm(act2, df2, 'tn', out_dtype=bf16, name="ffn2_out_dw")
    dh3, (terms['w_ffn2_out'],) = _ffn_in_dx(dab2, w2i, "ffn2_in_dx",
                                             ride=scatter_of([('w_ffn2_out', ffn_out_back(dw2o))]))
    dw2i = _ffn_in_dw(h3, dab2, "ffn2_in_dw")
    dx1, do_mix, dgt2, dnf2, dsh3, dsc3 = _rw_vjp(
        res2, [x1, o_mix], [gt2, nf2, sh3, sc3], [dx2, dh3], row_grads=[f32, bf16], param_grads=[True] * 4,
        name="res2_mod3_bwd")
    dmerged = _mm(do_mix, w_o, 'nt', out_dtype=bf16, name="out_proj_dx")
    dw_o = _mm(merged, do_mix, 'tn', out_dtype=bf16, name="out_proj_dw")
    dp_a, dglu, dga, dgb = _rw_vjp(_f_merge, merge_rows, [], [dmerged], row_grads=[bf16] * 4,
                                   param_grads=[], name="merge_bwd")

    dgl = _mm(dglu, w_glu, 'nt', out_dtype=bf16, name="glu_proj_dx")
    dw_glu = _mm(gl, dglu, 'tn', out_dtype=bf16, name="glu_proj_dw")
    dyb, du_skip, dd_row = _rw_vjp(_f_s5_post, [yb, (proj, lay.u, S5_W)], [d_row], [dgl],
                                   row_grads=[bf16, f32], param_grads=[True], name="s5_post_bwd")
    (du, dar, dai, dbd_c, dc_c), (terms['w_ffn2_in'],) = _s5_bwd(
        dyb, s5s, proj, lay.u, bd_c, c_c, ar_row, ai_row, du_skip, "s5_bwd",
        ride=scatter_of([('w_ffn2_in', dw2i)]))
    from_bd = lambda m_: _diag8(m_, S5_I, S5_P).transpose(0, 2, 1).reshape(S5_S, S5_I)
    dprep = _rw_vjp(_f_s5_prep, prep_rows, [], [dar.reshape(S5_S, 1), dai.reshape(S5_S, 1),
                                                from_bd(dbd_c[:, :S5_SB]), from_bd(dbd_c[:, S5_SB:])],
                    row_grads=[f32] * 5, param_grads=[], name="s5_prep_bwd", tm=512)
    dlr, dli, dldt, dbr, dbi = dprep
    g_s5 = dict(
        s5_lambda_re=dlr.reshape(S5_G, S5_P), s5_lambda_im=dli.reshape(S5_G, S5_P),
        s5_log_dt=dldt.reshape(S5_G, S5_P).sum(axis=1),
        s5_b_re=dbr.reshape(S5_G, S5_P, S5_I), s5_b_im=dbi.reshape(S5_G, S5_P, S5_I),
        s5_c_re=_diag8(dc_c[:S5_S], S5_P, S5_I).transpose(0, 2, 1),
        s5_c_im=-_diag8(dc_c[S5_S:], S5_P, S5_I).transpose(0, 2, 1),
        s5_d=dd_row.reshape(S5_G, S5_I))

    dy_a = _mm(dp_a, w_a, 'nt', out_dtype=bf16, name="a_proj_dx")
    dw_a = _mm(y_a, dp_a, 'tn', out_dtype=bf16, name="a_proj_dw")
    dy_ssd, dz, dssd_nw = _rw_vjp(_f_ssd_post, [y_ssd, (proj, lay.z, SSD_DI)], [ssd_norm_w], [dy_a],
                                  row_grads=[f32, bf16], param_grads=[True], name="ssd_post_bwd")
    early = [('w_out', rows_back(dw_o, d)), ('w_b_glu', _shard_cols(dw_glu)), ('w_a_proj', rows_back(dw_a, SSD_DI))]
    (dxs, dbm, dcm, ddt4, da4, ddsk4), landed = _ssd_bwd(xs, bm, cm, dt4, a4, dsk4, hs, dy_ssd, "ssd_bwd",
                                                         ride=scatter_of(early))
    terms.update({n: p for (n, _), p in zip(early, landed)})
    da_log, dd_ssd = head_vjp((da4, ddsk4))
    dpre, ddt_raw, dbias128 = _rw_vjp(_f_ssd_pre, [pre, (proj, lay.dt, LANE)], [bias128, spread, spread_t],
                                      [dxs, dbm, dcm, ddt4], row_grads=[f32, bf16],
                                      param_grads=[True, False, False], name="ssd_pre_bwd")
    dxbc, dconv_w8, dconv_b = _conv_bwd(dpre, proj, lay.xbc, conv_w8, "conv_bwd")

    dproj = [dz, dxbc, du, dga, dgb]
    dw_in = [_mm(h2, p, 'tn', out_dtype=bf16, name="in_proj_dw_%d" % i) for i, p in enumerate(dproj + [ddt_raw])]
    dh2, (terms['w_in'],) = _in_proj_dx(dproj, ddt_raw, lay.dt, w_inr, "in_proj_dx",
                                        ride=scatter_of([('w_in', lay.restore_slabs(dw_in))]))
    dx0, df1, dgt1, dnmx, dsh2, dsc2 = _rw_vjp(
        res1, [x2d, f1], [gt1, nmx, sh2, sc2], [dx1, dh2], row_grads=[f32, bf16], param_grads=[True] * 4,
        name="res1_mod2_bwd")
    dw1o = _mm(act1, df1, 'tn', out_dtype=bf16, name="ffn1_out_dw")
    dab1, (terms['w_ffn1_out'],) = _ffn_out_dx(df1, w1o, ab1, "ffn1_out_dx",
                                               ride=scatter_of([('w_ffn1_out', ffn_out_back(dw1o))]))
    dw1i = _ffn_in_dw(h1, dab1, "ffn1_in_dw")

    last = 'w_ffn1_in'
    keep = {'w_ffn1_in': hf, 'w_ffn2_in': hf}
    sums = {n: _sum_lead(terms[n], "sum_" + n, cols=keep.get(n)) for n in BIG if n != last}
    swap = _RideSwap([sums[n] for n in BIG if n != last])
    rides = _Rides([scatter_of([(last, dw1i)]), swap])
    dh1, landed = _ffn_in_dx(dab1, w1i, "ffn1_in_dx", ride=rides)
    (terms[last],), swapped = rides.split(landed)
    others = dict(zip([n for n in BIG if n != last], swapped))
    grad_x, dnf1, dsh1, dsc1 = _rw_vjp(_f_mod_keep, [x2d], [nf1, sh1, sc1], [dh1, dx0],
                                       row_grads=[f32], param_grads=[True] * 3, name="mod1_bwd")
    d_mods = jnp.concatenate([dsh1, dsc1, dgt1, dsh2, dsc2, dgt2, dsh3, dsc3, dgt3], axis=1).reshape(-1)
    sums[last] = _sum_lead(terms[last], "sum_" + last, cols=keep.get(last))
    (others[last],) = _run_ride(_RideSwap([sums[last]]), "swap_sum_" + last)

    out_g, out_d, out_m, out_v = {}, {}, {}, {}
    for n in BIG:
        r = _adamw(W[n][0], Mo[n][0], Vo[n][0], [sums[n], others[n]], "adamw_" + n)
        out_g[n], out_d[n], out_m[n], out_v[n] = [o[None] for o in r]

    local = dict(
        b_ada=d_mods, norm_ffn1=dnf1, norm_mix=dnmx, conv_w=dconv_w8[:CONV_K], conv_b=dconv_b,
        dt_bias=dbias128[:, :SSD_HEADS], a_log=da_log, d_ssd=dd_ssd, ssd_norm_w=dssd_nw,
        norm_ffn2=dnf2, norm_final=dnfin, **g_s5)
    flat = jnp.concatenate([local[n].reshape(-1) for n in SMALL])
    g3 = _gather8(flat, "gather_small_grads")
    n_small = flat.shape[0]
    npad = _round_up(n_small, SUBLANE * LANE)
    g3p = jnp.zeros((8, npad), f32).at[:, :n_small].set(g3).reshape(8, npad // LANE, LANE)
    gsum = _sum_lead(g3p, "sum_small").reshape(-1)

    def local_shard(n, a):
        if n == 'conv_w':
            return lax.dynamic_slice(a.reshape(CONV_K, CONV_DIM), (0, k_me * cw_cols), (CONV_K, cw_cols))
        return a

    pieces, off = {}, 0
    for n in SMALL:
        sz = local[n].size
        pieces[n] = local_shard(n, gsum[off:off + sz]).reshape(W[n].shape)
        off += sz

    def pack(dct):
        v_ = jnp.concatenate([dct[n].reshape(-1) for n in SMALL])
        pad = _round_up(v_.shape[0], SUBLANE * LANE) - v_.shape[0]
        return jnp.concatenate([v_, jnp.ones((pad,), f32)]).reshape(-1, LANE)

    rs = _adamw(pack(W), pack(Mo), pack(Vo), [pack(pieces)], "adamw_small")
    off = 0
    for n in SMALL:
        sz = W[n].size
        out_g[n], out_d[n], out_m[n], out_v[n] = [o.reshape(-1)[off:off + sz].reshape(W[n].shape) for o in rs]
        off += sz

    dm_loc = lax.dynamic_slice(g3[:, :N_ADA * d], (0, k_me * n_ada_loc), (SUBLANE, n_ada_loc))
    g_ada = _ada_bwd(c_all, dm_loc, "ada_bwd")
    r = _adamw(w_ada[0], m_w_ada[0], v_w_ada[0], [g_ada], "adamw_w_ada")
    out_g['w_ada'], out_d['w_ada'], out_m['w_ada'], out_v['w_ada'] = [o[None] for o in r]

    return (loss, grad_x[None], *[out_g[n] for n in WEIGHTS], *[out_d[n] for n in WEIGHTS],
            *[out_m[n] for n in WEIGHTS], *[out_v[n] for n in WEIGHTS])
```

```python
import functools
import math

import numpy as np
import jax
import jax.numpy as jnp
from jax import lax
from jax.experimental import pallas as pl
from jax.experimental.pallas import tpu as pltpu

f32 = jnp.float32
bf16 = jnp.bfloat16
MESH = pl.DeviceIdType.MESH
AXES = ("x", "y", "c")

EPS = 1e-6
SSD_HEADS, SSD_P, SSD_N, SSD_G, SSD_R, SSD_L = 32, 64, 128, 4, 8, 128
SSD_DI = SSD_HEADS * SSD_P
CONV_K = 4
CONV_DIM = SSD_DI + 2 * SSD_G * SSD_N
S5_W, S5_G, S5_I, S5_P = 1024, 64, 16, 64
S5_S = S5_G * S5_P
N_ADA = 9
ADAM_LR, ADAM_B1, ADAM_B2, ADAM_EPS, ADAM_WD, ADAM_STEP = 0.001, 0.9, 0.999, 1e-08, 0.01, 10

LANE = 128
SUBLANE = 8
VMEM_LIMIT = 56 << 20
MM_VMEM_BUDGET = 40 << 20
RW_VMEM_BUDGET = 36 << 20

WEIGHTS = ['w_ada', 'b_ada', 'norm_ffn1', 'w_ffn1_in', 'w_ffn1_out', 'norm_mix', 'w_in', 'conv_w', 'conv_b', 'dt_bias',
           'a_log', 'd_ssd', 'ssd_norm_w', 'w_a_proj', 's5_lambda_re', 's5_lambda_im', 's5_b_re', 's5_b_im', 's5_c_re',
           's5_c_im', 's5_d', 's5_log_dt', 'w_b_glu', 'w_out', 'norm_ffn2', 'w_ffn2_in', 'w_ffn2_out', 'norm_final']
BIG = ['w_ffn1_in', 'w_ffn1_out', 'w_in', 'w_a_proj', 'w_b_glu', 'w_out', 'w_ffn2_in', 'w_ffn2_out']
COL_SHARDED = ('w_ffn1_in', 'w_in', 'w_b_glu', 'w_ffn2_in')
SMALL = [n for n in WEIGHTS if n not in BIG and n != 'w_ada']


def _cp(sem=None):
    return pltpu.CompilerParams(dimension_semantics=sem, vmem_limit_bytes=VMEM_LIMIT)


def _tile(dim, target, align=LANE):
    if dim <= target:
        return dim
    t = (target // align) * align
    while t >= align:
        if dim % t == 0:
            return t
        t -= align
    return dim


def _round_up(n, m):
    return (n + m - 1) // m * m


def _mm(a, b, mode, *, out_dtype, name, a_win=None, b_win=None, add=None, ride=None):
    a0, aw = a_win or (0, a.shape[1])
    b0, bw = b_win or (0, b.shape[1])
    if mode == 'nn':
        m, k, n = a.shape[0], aw, bw
        assert b.shape[0] == k
    elif mode == 'nt':
        m, k, n = a.shape[0], aw, b.shape[0]
        assert bw == k
    else:
        k, m, n = a.shape[0], aw, bw
        assert b.shape[0] == k
    osz = jnp.dtype(out_dtype).itemsize
    tm, tn, tk = 1024, 1152, 3456
    while True:
        bm = _tile(math.gcd(m, a0) if (mode == 'tn' and a0) else m, tm)
        bn = _tile(math.gcd(n, b0) if (mode != 'nt' and b0) else n, tn)
        kk = k
        if mode != 'tn' and a0:
            kk = math.gcd(kk, a0)
        if mode == 'nt' and b0:
            kk = math.gcd(kk, b0)
        bk = _tile(kk, tk)
        need = 2 * (bm * bk * a.dtype.itemsize + bk * bn * b.dtype.itemsize + bm * bn * osz) + bm * bn * 4
        if add is not None:
            need += 2 * bm * bn * add.dtype.itemsize
        if need <= MM_VMEM_BUDGET or (tm <= 256 and tn <= 256 and tk <= 512):
            break
        if tk > 1024:
            tk //= 2
        elif tm >= tn:
            tm //= 2
        else:
            tn //= 2
    nk = k // bk
    assert m % bm == 0 and n % bn == 0 and k % bk == 0, (name, m, n, k, bm, bn, bk)
    if mode == 'nn':
        ao, bo = a0 // bk, b0 // bn
        a_blk, a_map = (bm, bk), lambda i, j, q: (i, q + ao)
        b_blk, b_map = (bk, bn), lambda i, j, q: (q, j + bo)
    elif mode == 'nt':
        ao, bo = a0 // bk, b0 // bk
        a_blk, a_map = (bm, bk), lambda i, j, q: (i, q + ao)
        b_blk, b_map = (bn, bk), lambda i, j, q: (j, q + bo)
    else:
        ao, bo = a0 // bm, b0 // bn
        a_blk, a_map = (bk, bm), lambda i, j, q: (q, i + ao)
        b_blk, b_map = (bk, bn), lambda i, j, q: (q, j + bo)
    return _mm_core(a, b, mode, grid=(m // bm, n // bn, nk), a_blk=a_blk, a_map=a_map, b_blk=b_blk, b_map=b_map,
                    o_blk=(bm, bn), o_map=lambda i, j, q: (i, j), out_shape=(m, n), out_dtype=out_dtype, name=name,
                    add=add, ride=ride)


def _mm_core(a, b, mode, *, grid, a_blk, a_map, b_blk, b_map, o_blk, o_map, out_shape, out_dtype, name,
             add=None, ride=None):
    dims = {'nn': (((1,), (0,)), ((), ())), 'nt': (((1,), (1,)), ((), ())), 'tn': (((0,), (0,)), ((), ()))}[mode]
    nk = grid[-1]
    has_add = add is not None
    nr = ride.n if ride is not None else 0

    def body(*refs):
        a_ref, b_ref = refs[0], refs[1]
        pos = 2
        add_ref = refs[pos] if has_add else None
        pos += int(has_add)
        r_ins = refs[pos:pos + nr]
        o_ref = refs[pos + nr]
        r_outs = refs[pos + nr + 1:pos + 2 * nr + 1]
        acc_ref = refs[pos + 2 * nr + 1]
        r_sems = refs[pos + 2 * nr + 2:]
        ids = [pl.program_id(ax) for ax in range(len(grid))]
        q = ids[-1]
        if nr:
            @pl.when(functools.reduce(lambda u, v: u & v, [i == 0 for i in ids]))
            def _():
                ride.start(r_ins, r_outs, r_sems)

        def prod():
            return lax.dot_general(a_ref[...].astype(bf16), b_ref[...].astype(bf16), dims, preferred_element_type=f32)

        def emit(r):
            if has_add:
                r = r + add_ref[...].astype(f32)
            o_ref[...] = r.astype(out_dtype)

        if nk == 1:
            emit(prod())
        else:
            @pl.when(q == 0)
            def _():
                acc_ref[...] = prod()

            @pl.when((q > 0) & (q < nk - 1))
            def _():
                acc_ref[...] += prod()

            @pl.when(q == nk - 1)
            def _():
                emit(acc_ref[...] + prod())

        if nr:
            @pl.when(functools.reduce(lambda u, v: u & v, [i == g - 1 for i, g in zip(ids, grid)]))
            def _():
                ride.wait(r_ins, r_outs, r_sems)

    in_specs = [pl.BlockSpec(a_blk, a_map), pl.BlockSpec(b_blk, b_map)]
    ops = [a, b]
    if has_add:
        in_specs.append(pl.BlockSpec(o_blk, o_map))
        ops.append(add)
    out_specs = [pl.BlockSpec(o_blk, o_map)]
    out_shapes = [jax.ShapeDtypeStruct(out_shape, out_dtype)]
    scratch = [pltpu.VMEM(tuple(b for b in o_blk if b is not None), f32)]
    if nr:
        in_specs += ride.specs
        ops += ride.srcs
        out_specs += ride.specs
        out_shapes += ride.out_shape
        scratch += ride.scratch
    sem = ("arbitrary",) * len(grid) if nr else ("parallel",) * (len(grid) - 1) + ("arbitrary",)
    res = pl.pallas_call(
        body, name=name, grid=grid, in_specs=in_specs, out_specs=out_specs, out_shape=out_shapes,
        scratch_shapes=scratch, compiler_params=_cp(sem),
    )(*ops)
    return (res[0], list(res[1:])) if nr else res[0]


def _swiglu(a, b):
    return jax.nn.silu(a) * b


def _ride_parts(refs, n_in, n_out, ride):
    nr = ride.n if ride is not None else 0
    ins = refs[:n_in]
    r_ins = refs[n_in:n_in + nr]
    outs = refs[n_in + nr:n_in + nr + n_out]
    r_outs = refs[n_in + nr + n_out:n_in + 2 * nr + n_out]
    return ins, r_ins, outs, r_outs, refs[n_in + 2 * nr + n_out:]


def _call_with_ride(body_core, grid, in_specs, ops, out_specs, out_shape, name, ride, scratch=(), sem=None):
    nr = ride.n if ride is not None else 0
    n_in, n_out, n_scr = len(in_specs), len(out_specs), len(scratch)

    def body(*refs):
        ins, r_ins, outs, r_outs, rest = _ride_parts(refs, n_in, n_out, ride)
        ids = [pl.program_id(ax) for ax in range(len(grid))]
        if nr:
            @pl.when(functools.reduce(lambda u, v: u & v, [i == 0 for i in ids]))
            def _():
                ride.start(r_ins, r_outs, rest[n_scr:])
        body_core(ins, outs, *rest[:n_scr])
        if nr:
            @pl.when(functools.reduce(lambda u, v: u & v, [i == g - 1 for i, g in zip(ids, grid)]))
            def _():
                ride.wait(r_ins, r_outs, rest[n_scr:])

    res = pl.pallas_call(
        body, name=name, grid=grid, in_specs=in_specs + (ride.specs if nr else []),
        out_specs=out_specs + (ride.specs if nr else []), out_shape=out_shape + (ride.out_shape if nr else []),
        scratch_shapes=list(scratch) + (ride.scratch if nr else []),
        compiler_params=_cp(("arbitrary",) * len(grid) if nr else (sem or ("parallel",) * len(grid))),
    )(*ops, *(ride.srcs if nr else []))
    return list(res[:n_out]), list(res[n_out:])


def _ffn_in(h, w, name, ride=None):
    t, d = h.shape
    cp = w.shape[2]
    bm, bn = _tile(t, 512), _tile(cp, 1408)
    nbs = cp // bn
    nn = (((1,), (0,)), ((), ()))

    def core(ins, outs):
        h_ref, wa_ref, wb_ref = ins
        ab_ref, act_ref = outs
        hv = h_ref[...].astype(bf16)
        a = lax.dot_general(hv, wa_ref[...].astype(bf16), nn, preferred_element_type=f32)
        b = lax.dot_general(hv, wb_ref[...].astype(bf16), nn, preferred_element_type=f32)
        ab_ref[0] = a.astype(bf16)
        ab_ref[1] = b.astype(bf16)
        act_ref[...] = _swiglu(a, b).astype(bf16)

    (ab, act), landed = _call_with_ride(
        core, (2 * nbs, t // bm),
        [pl.BlockSpec((bm, d), lambda j, i: (i, 0)),
         pl.BlockSpec((None, d, bn), lambda j, i: (j // nbs, 0, j % nbs)),
         pl.BlockSpec((None, d, bn), lambda j, i: (2 + j // nbs, 0, j % nbs))], [h, w, w],
        [pl.BlockSpec((2, bm, bn), lambda j, i: (0, i, j)), pl.BlockSpec((bm, bn), lambda j, i: (i, j))],
        [jax.ShapeDtypeStruct((2, t, 2 * cp), bf16), jax.ShapeDtypeStruct((t, 2 * cp), bf16)], name, ride)
    return ab, act, landed


def _ffn_out_dx(df, wo, ab, name, ride=None):
    t, d = df.shape
    ffp = wo.shape[0]
    bm, bn = _tile(t, 1024), _tile(ffp, 512)
    nt = (((1,), (1,)), ((), ()))

    def core(ins, outs):
        df_ref, wo_ref, ab_ref = ins
        dact = lax.dot_general(df_ref[...].astype(bf16), wo_ref[...].astype(bf16), nt, preferred_element_type=f32)
        _, vjp = jax.vjp(_swiglu, ab_ref[0].astype(f32), ab_ref[1].astype(f32))
        da, db = vjp(dact)
        outs[0][0] = da.astype(bf16)
        outs[0][1] = db.astype(bf16)

    (dab,), landed = _call_with_ride(
        core, (t // bm, ffp // bn),
        [pl.BlockSpec((bm, d), lambda i, j: (i, 0)), pl.BlockSpec((bn, d), lambda i, j: (j, 0)),
         pl.BlockSpec((2, bm, bn), lambda i, j: (0, i, j))], [df, wo, ab],
        [pl.BlockSpec((2, bm, bn), lambda i, j: (0, i, j))], [jax.ShapeDtypeStruct((2, t, ffp), bf16)], name, ride)
    return dab, landed


def _ffn_in_dx(dab, w, name, ride=None):
    _, t, cp2 = dab.shape
    d, cp = w.shape[1], w.shape[2]
    bm, bn, bk = _tile(t, 1024), _tile(d, 1024), _tile(cp, 2816)
    nkb = cp // bk
    return _mm_core(dab, w, 'nt', grid=(t // bm, d // bn, 4 * nkb),
                    a_blk=(None, bm, bk), a_map=lambda i, j, q: (q // (2 * nkb), i, q % (2 * nkb)),
                    b_blk=(None, bn, bk), b_map=lambda i, j, q: (q // nkb, j, q % nkb),
                    o_blk=(bm, bn), o_map=lambda i, j, q: (i, j),
                    out_shape=(t, d), out_dtype=bf16, name=name, ride=ride)


def _ffn_in_dw(h, dab, name):
    _, t, cp2 = dab.shape
    cp = cp2 // 2
    d = h.shape[1]
    bm, bn, bk = _tile(d, 1024), _tile(cp, 1408), _tile(t, 2048)
    nbs = cp // bn
    return _mm_core(h, dab, 'tn', grid=(d // bm, 4 * nbs, t // bk),
                    a_blk=(bk, bm), a_map=lambda i, j, q: (q, i),
                    b_blk=(None, bk, bn), b_map=lambda i, j, q: (j // (2 * nbs), q, j % (2 * nbs)),
                    o_blk=(None, bm, bn), o_map=lambda i, j, q: (j // nbs, i, j % nbs),
                    out_shape=(4, d, cp), out_dtype=bf16, name=name)


def _piece_blocks(pieces):
    bk = min(1024, functools.reduce(math.gcd, [p.shape[1] for p in pieces]))
    starts, n = [], 0
    for p in pieces:
        starts.append(n)
        n += p.shape[1] // bk
    return bk, starts, n


def _in_proj_dx(pieces, tail, tail_col, w, name, ride=None):
    t, d = pieces[0].shape[0], w.shape[0]
    bk, starts, nq = _piece_blocks(pieces)
    assert tail_col == nq * bk and tail.shape[1] == LANE
    bm, bn = _tile(t, 1024), _tile(d, 1024)
    npc = len(pieces)

    def core(ins, outs, acc_ref):
        a_refs, tail_ref, b_ref, bt_ref = ins[:npc], ins[npc], ins[npc + 1], ins[npc + 2]
        q = pl.program_id(2)

        @pl.when(q == 0)
        def _():
            acc_ref[...] = jnp.zeros_like(acc_ref)

        for a_ref, s0, p in zip(a_refs, starts, pieces):
            @pl.when((q >= s0) & (q < s0 + p.shape[1] // bk))
            def _(a_ref=a_ref):
                acc_ref[...] += lax.dot_general(a_ref[...].astype(bf16), b_ref[...], _NT, preferred_element_type=f32)

        @pl.when(q == nq)
        def _():
            r = acc_ref[...] + lax.dot_general(tail_ref[...].astype(bf16), bt_ref[...], _NT,
                                               preferred_element_type=f32)
            outs[0][...] = r.astype(bf16)

    def a_spec(s0, p):
        last = p.shape[1] // bk - 1
        return pl.BlockSpec((bm, bk), lambda i, j, q: (i, jnp.clip(q - s0, 0, last)))

    in_specs = [a_spec(s0, p) for s0, p in zip(starts, pieces)] + [
        pl.BlockSpec((bm, LANE), lambda i, j, q: (i, 0)),
        pl.BlockSpec((bn, bk), lambda i, j, q: (j, jnp.minimum(q, nq - 1))),
        pl.BlockSpec((bn, LANE), lambda i, j, q: (j, tail_col // LANE))]
    (dh,), landed = _call_with_ride(
        core, (t // bm, d // bn, nq + 1), in_specs, list(pieces) + [tail, w, w],
        [pl.BlockSpec((bm, bn), lambda i, j, q: (i, j))], [jax.ShapeDtypeStruct((t, d), bf16)], name, ride,
        scratch=[pltpu.VMEM((bm, bn), f32)], sem=("parallel", "parallel", "arbitrary"))
    return dh, landed


def _win(r):
    return r if isinstance(r, tuple) else (r, 0, r.shape[1])


def _row_tile(t, widths):
    per_row = 48 * max(widths)
    tm = 512
    while tm > SUBLANE and tm * per_row > RW_VMEM_BUDGET:
        tm //= 2
    return min(tm, t)


def _row_spec(r, tm):
    arr, c0, w = _win(r)
    assert c0 % w == 0, (c0, w)
    cb = c0 // w
    return pl.BlockSpec((tm, w), lambda i: (i, cb))


def _full_spec(p):
    nd = p.ndim
    return pl.BlockSpec(p.shape, lambda i: (0,) * nd)


def _rw(f, rows, params, outs, *, name, accs=(), tm=None):
    t = _win(rows[0])[0].shape[0]
    tm = tm or _row_tile(t, [_win(r)[2] for r in rows] + [w for w, _ in outs])
    nr, npar, no, na = len(rows), len(params), len(outs), len(accs)

    def body(*refs):
        vals = [r[...] for r in refs[:nr + npar]]
        res = f(*vals)
        res = res if isinstance(res, (tuple, list)) else (res,)
        for o_ref, v in zip(refs[nr + npar:nr + npar + no], res[:no]):
            o_ref[...] = v.astype(o_ref.dtype)
        if na:
            first = pl.program_id(0) == 0
            for a_ref, v in zip(refs[nr + npar + no:], res[no:]):
                @pl.when(first)
                def _(a_ref=a_ref):
                    a_ref[...] = jnp.zeros_like(a_ref)
                a_ref[...] += v

    out_shape = [jax.ShapeDtypeStruct((t, w), d) for w, d in outs] + [jax.ShapeDtypeStruct(s, f32) for s in accs]
    out_specs = [pl.BlockSpec((tm, w), lambda i: (i, 0)) for w, _ in outs] + \
                [pl.BlockSpec(s, lambda i: (0, 0)) for s in accs]
    return pl.pallas_call(
        body, name=name, grid=(t // tm,),
        in_specs=[_row_spec(r, tm) for r in rows] + [_full_spec(p) for p in params],
        out_specs=out_specs, out_shape=out_shape,
        compiler_params=_cp(("arbitrary",)),
    )(*[_win(r)[0] for r in rows], *params)


def _rw_vjp(f, rows, params, cots, *, row_grads, param_grads, name, tm=None, total_of=None):
    t = _win(rows[0])[0].shape[0]
    cot_rows = [c for c in cots if c is not None]
    tm = tm or _row_tile(t, [_win(r)[2] for r in rows] + [_win(c)[2] for c in cot_rows])
    nr, npar, ncot = len(rows), len(params), len(cot_rows)
    d_rows = [i for i, d in enumerate(row_grads) if d is not None]
    d_pars = [i for i, d in enumerate(param_grads) if d]

    def body(*refs):
        rv = [r[...] for r in refs[:nr]]
        pv = [r[...] for r in refs[nr:nr + npar]]
        cv = [r[...] for r in refs[nr + npar:nr + npar + ncot]]
        outs_r = refs[nr + npar + ncot:nr + npar + ncot + len(d_rows)]
        outs_p = refs[nr + npar + ncot + len(d_rows):nr + npar + ncot + len(d_rows) + len(d_pars)]

        def g(*diff):
            rr, pp = list(rv), list(pv)
            for i, v in zip(d_rows, diff[:len(d_rows)]):
                rr[i] = v
            for i, v in zip(d_pars, diff[len(d_rows):]):
                pp[i] = v
            res = f(*rr, *pp)
            return tuple(res) if isinstance(res, (tuple, list)) else (res,)

        prim, vjp = jax.vjp(g, *[rv[i] for i in d_rows], *[pv[i] for i in d_pars])
        it = iter(cv)
        cts = tuple(next(it).astype(o.dtype) if c is not None else jnp.zeros_like(o) for o, c in zip(prim, cots))
        grads = vjp(cts)
        for o_ref, v in zip(outs_r, grads[:len(d_rows)]):
            o_ref[...] = v.astype(o_ref.dtype)
        first = pl.program_id(0) == 0
        for o_ref, v in zip(outs_p, grads[len(d_rows):]):
            @pl.when(first)
            def _(o_ref=o_ref):
                o_ref[...] = jnp.zeros_like(o_ref)
            o_ref[...] += v.astype(f32)
        if total_of is not None:
            tot_ref = refs[-1]

            @pl.when(first)
            def _():
                tot_ref[...] = jnp.zeros_like(tot_ref)
            tot_ref[...] += jnp.broadcast_to(jnp.sum(prim[total_of].astype(f32)), tot_ref.shape)

    out_shape = [jax.ShapeDtypeStruct((t, _win(rows[i])[2]), row_grads[i]) for i in d_rows] + \
                [jax.ShapeDtypeStruct(params[i].shape, f32) for i in d_pars]
    out_specs = [pl.BlockSpec((tm, _win(rows[i])[2]), lambda i_: (i_, 0)) for i in d_rows] + \
                [_full_spec(params[i]) for i in d_pars]
    if total_of is not None:
        out_shape.append(jax.ShapeDtypeStruct((1, LANE), f32))
        out_specs.append(pl.BlockSpec((1, LANE), lambda i_: (0, 0)))
    return pl.pallas_call(
        body, name=name, grid=(t // tm,),
        in_specs=[_row_spec(r, tm) for r in rows] + [_full_spec(p) for p in params] + [_row_spec(c, tm) for c in cot_rows],
        out_specs=out_specs, out_shape=out_shape,
        compiler_params=_cp(("arbitrary",)),
    )(*[_win(r)[0] for r in rows], *params, *[_win(c)[0] for c in cot_rows])


def _rms(x, g):
    return x * lax.rsqrt(jnp.mean(x * x, axis=-1, keepdims=True) + EPS) * g


def _f_mod(x, nw, sh, sc):
    return (_rms(x, nw) * (1.0 + sc) + sh).astype(bf16)


def _f_mod_keep(x, nw, sh, sc):
    return _f_mod(x, nw, sh, sc), x


def _f_res_mod(coef, x, o, g, nw, sh, sc):
    x1 = x + coef * g * o.astype(f32)
    return x1, _f_mod(x1, nw, sh, sc)


def _times01(x, e):
    hi = x.astype(bf16)
    r1 = x - hi.astype(f32)
    mid = r1.astype(bf16)
    lo = (r1 - mid.astype(f32)).astype(bf16)
    return (jnp.dot(hi, e, preferred_element_type=f32) + jnp.dot(mid, e, preferred_element_type=f32) +
            jnp.dot(lo, e, preferred_element_type=f32))


@jax.custom_vjp
def _spread_heads(x, e, et):
    return _times01(x, e)


_spread_heads.defvjp(lambda x, e, et: (_times01(x, e), (e, et)),
                     lambda res, g: (_times01(g, res[1]), None, None))


def _f_ssd_pre(pre, dtraw, bias, e, et):
    xc = jax.nn.silu(pre)
    dtx = _spread_heads(jax.nn.softplus(dtraw + bias), e, et)
    return xc[:, :SSD_DI], xc[:, SSD_DI:SSD_DI + SSD_G * SSD_N], xc[:, SSD_DI + SSD_G * SSD_N:], dtx


def _f_ssd_post(y, z, nw):
    yz = y * jax.nn.silu(z)
    w = SSD_DI // SSD_G
    parts = []
    for g in range(SSD_G):
        s = yz[:, g * w:(g + 1) * w]
        parts.append(s * lax.rsqrt(jnp.mean(s * s, axis=-1, keepdims=True) + EPS))
    return (jnp.concatenate(parts, axis=1) * nw).astype(bf16)


def _f_s5_post(yb, u, d):
    return jax.nn.gelu(yb + d * u).astype(bf16)


def _f_merge(pa, glu, ga, gb):
    d = pa.shape[1]
    glu = glu.astype(f32)
    pb = glu[:, :d] * jax.nn.sigmoid(glu[:, d:])
    return (jax.nn.sigmoid(ga) * pa.astype(f32) + jax.nn.sigmoid(gb) * pb).astype(bf16)


def _f_final(x2, o, tgt, g, nw):
    x3 = x2 + 0.5 * g * o.astype(f32)
    y = _rms(x3, nw)
    return 0.5 * jnp.mean(jnp.square(y - tgt), axis=-1, keepdims=True)


def _f_s5_prep(lr, li, ldt, br, bi):
    dt = jnp.exp(ldt)
    lr = jnp.minimum(lr, -1e-4)
    mag = jnp.exp(lr * dt)
    ar = mag * jnp.cos(li * dt)
    ai = mag * jnp.sin(li * dt)
    den = lr * lr + li * li
    nr = ar - 1.0
    kr = (nr * lr + ai * li) / den
    ki = (ai * lr - nr * li) / den
    return ar, ai, kr * br - ki * bi, kr * bi + ki * br


def _shift_down(cur, halo8, j):
    if j == 0:
        return cur
    rolled = pltpu.roll(cur, j, 0)
    row8 = lax.broadcasted_iota(jnp.int32, halo8.shape, 0)
    top = jnp.where(row8 < j, pltpu.roll(halo8, j, 0), rolled[:SUBLANE])
    return jnp.concatenate([top, rolled[SUBLANE:]], axis=0)


def _shift_up(cur, halo8, j):
    if j == 0:
        return cur
    n = cur.shape[0]
    rolled = pltpu.roll(cur, n - j, 0)
    row8 = lax.broadcasted_iota(jnp.int32, halo8.shape, 0)
    bot = jnp.where(row8 >= SUBLANE - j, pltpu.roll(halo8, SUBLANE - j, 0), rolled[n - SUBLANE:])
    return jnp.concatenate([rolled[:n - SUBLANE], bot], axis=0)


def _conv_fwd(proj, c0, w8, b, name):
    t = proj.shape[0]
    cw = 1024
    tm = min(512, t)
    cb0 = c0 // cw
    r8 = tm // SUBLANE

    def body(x_ref, h_ref, w_ref, b_ref, o_ref):
        i = pl.program_id(1)
        x = x_ref[...]
        halo = jnp.where(i > 0, h_ref[...], 0.0)
        acc = b_ref[...] + w_ref[CONV_K - 1:CONV_K, :] * x
        for j in range(1, CONV_K):
            acc = acc + w_ref[CONV_K - 1 - j:CONV_K - j, :] * _shift_down(x, halo, j)
        o_ref[...] = acc

    return pl.pallas_call(
        body, name=name, grid=(CONV_DIM // cw, t // tm),
        in_specs=[pl.BlockSpec((tm, cw), lambda c, i: (i, cb0 + c)),
                  pl.BlockSpec((SUBLANE, cw), lambda c, i: (jnp.maximum(i * r8 - 1, 0), cb0 + c)),
                  pl.BlockSpec((SUBLANE, cw), lambda c, i: (0, c)),
                  pl.BlockSpec((1, cw), lambda c, i: (0, c))],
        out_specs=pl.BlockSpec((tm, cw), lambda c, i: (i, c)),
        out_shape=jax.ShapeDtypeStruct((t, CONV_DIM), f32),
        compiler_params=_cp(("parallel", "arbitrary")),
    )(proj, proj, w8, b)


def _conv_bwd(dpre, proj, c0, w8, name):
    t = proj.shape[0]
    cw = 1024
    tm = min(512, t)
    cb0 = c0 // cw
    r8 = tm // SUBLANE
    nb = t // tm

    def body(d_ref, dn_ref, x_ref, xh_ref, w_ref, dx_ref, dw_ref, db_ref):
        i = pl.program_id(1)
        d = d_ref[...]
        dn = jnp.where(i < nb - 1, dn_ref[...], 0.0)
        x = x_ref[...]
        xh = jnp.where(i > 0, xh_ref[...], 0.0)

        @pl.when(i == 0)
        def _():
            dw_ref[...] = jnp.zeros_like(dw_ref)
            db_ref[...] = jnp.zeros_like(db_ref)

        dx = w_ref[CONV_K - 1:CONV_K, :] * d
        rows = [jnp.sum(d * x, axis=0, keepdims=True)]
        for j in range(1, CONV_K):
            dx = dx + w_ref[CONV_K - 1 - j:CONV_K - j, :] * _shift_up(d, dn, j)
            rows.append(jnp.sum(d * _shift_down(x, xh, j), axis=0, keepdims=True))
        dx_ref[...] = dx.astype(dx_ref.dtype)
        dw = jnp.concatenate([rows[CONV_K - 1 - k] for k in range(CONV_K)] +
                             [jnp.zeros((SUBLANE - CONV_K, cw), f32)], axis=0)
        dw_ref[...] += dw
        db_ref[...] += jnp.sum(d, axis=0, keepdims=True)

    return pl.pallas_call(
        body, name=name, grid=(CONV_DIM // cw, nb),
        in_specs=[pl.BlockSpec((tm, cw), lambda c, i: (i, c)),
                  pl.BlockSpec((SUBLANE, cw), lambda c, i: (jnp.minimum((i + 1) * r8, nb * r8 - 1), c)),
                  pl.BlockSpec((tm, cw), lambda c, i: (i, cb0 + c)),
                  pl.BlockSpec((SUBLANE, cw), lambda c, i: (jnp.maximum(i * r8 - 1, 0), cb0 + c)),
                  pl.BlockSpec((SUBLANE, cw), lambda c, i: (0, c))],
        out_specs=[pl.BlockSpec((tm, cw), lambda c, i: (i, c)),
                   pl.BlockSpec((SUBLANE, cw), lambda c, i: (0, c)),
                   pl.BlockSpec((1, cw), lambda c, i: (0, c))],
        out_shape=[jax.ShapeDtypeStruct((t, CONV_DIM), bf16), jax.ShapeDtypeStruct((SUBLANE, CONV_DIM), f32),
                   jax.ShapeDtypeStruct((1, CONV_DIM), f32)],
        compiler_params=_cp(("parallel", "arbitrary")),
    )(dpre, dpre, proj, proj, w8)


def _cumsum_rows_impl(x):
    n = x.shape[0]
    row = lax.broadcasted_iota(jnp.int32, x.shape, 0)
    s = 1
    while s < n:
        x = x + jnp.where(row >= s, pltpu.roll(x, s, 0), 0.0)
        s *= 2
    return x


@jax.custom_vjp
def _cumsum_rows(x):
    return _cumsum_rows_impl(x)


def _cumsum_rows_bwd(_, g):
    c = _cumsum_rows_impl(g)
    return (c[c.shape[0] - 1:, :] - c + g,)


_cumsum_rows.defvjp(lambda x: (_cumsum_rows_impl(x), None), _cumsum_rows_bwd)


@jax.custom_vjp
def _swap_halves(t):
    return pltpu.roll(t, LANE // 2, 1)


_swap_halves.defvjp(lambda t: (pltpu.roll(t, LANE // 2, 1), None), lambda _, g: (pltpu.roll(g, LANE // 2, 1),))


def _ssd_chunk(xs, bm, cm, dtx, ax, dskx, ht):
    n = SSD_L
    assert n == LANE and SSD_P * 2 == LANE
    row = lax.broadcasted_iota(jnp.int32, (n, n), 0)
    col = lax.broadcasted_iota(jnp.int32, (n, n), 1)
    causal = row >= col
    lo = col < SSD_P
    cs = _cumsum_rows(dtx * ax)
    xdt = xs * dtx
    last = cs[n - 1:n, :]
    cb = lax.dot_general(cm.astype(bf16), bm.astype(bf16), (((1,), (1,)), ((), ())), preferred_element_type=f32)
    y_off = jnp.dot(cm.astype(bf16), ht.astype(bf16), preferred_element_type=f32) * jnp.exp(cs)
    st = lax.dot_general(bm.astype(bf16), (xdt * jnp.exp(last - cs)).astype(bf16), (((0,), (0,)), ((), ())),
                         preferred_element_type=f32)
    ht_new = jnp.exp(last) * ht + st
    ys = []
    for q in range(SSD_R // 2):
        tq = cs[:, q * LANE:(q + 1) * LANE]
        sw = _swap_halves(tq)
        tqt = tq.T
        xq = xdt[:, q * LANE:(q + 1) * LANE].astype(bf16)
        pair = []
        for c_col, r_row in ((jnp.where(lo, tq, sw), tqt[0:1, :]), (jnp.where(lo, sw, tq), tqt[SSD_P:SSD_P + 1, :])):
            decay = jnp.exp(jnp.where(causal, c_col - r_row, -1e30))
            pair.append(jnp.dot((cb * decay).astype(bf16), xq, preferred_element_type=f32))
        ys.append(jnp.where(lo, pair[0], pair[1]))
    return jnp.concatenate(ys, axis=1) + y_off + dskx * xs, ht_new


SSD_GB = 1


def _ssd_specs(nc, rev):
    ch = (lambda c: nc - 1 - c) if rev else (lambda c: c)
    gw = SSD_GB * SSD_R * SSD_P
    return [pl.BlockSpec((SSD_L, gw), lambda g, c: (ch(c), g)),
            pl.BlockSpec((SSD_L, SSD_GB * SSD_N), lambda g, c: (ch(c), g)),
            pl.BlockSpec((SSD_L, SSD_GB * SSD_N), lambda g, c: (ch(c), g)),
            pl.BlockSpec((SSD_L, gw), lambda g, c: (ch(c), g)),
            pl.BlockSpec((1, gw), lambda g, c: (0, g)),
            pl.BlockSpec((1, gw), lambda g, c: (0, g))]


def _ssd_group(refs, q):
    gw = SSD_R * SSD_P
    xs_ref, bm_ref, cm_ref, dt_ref, a_ref, dsk_ref = refs
    ln = slice(q * LANE, (q + 1) * LANE)
    wd = slice(q * gw, (q + 1) * gw)
    return (xs_ref[:, wd], bm_ref[:, ln], cm_ref[:, ln], dt_ref[:, wd], a_ref[:, wd], dsk_ref[:, wd])


def _ssd_fwd(xs, bm, cm, dt4, a4, dsk4, name, ride=None):
    t = xs.shape[0]
    nc = t // SSD_L
    gw = SSD_R * SSD_P

    nr = ride.n if ride is not None else 0
    ng = SSD_G // SSD_GB

    def body(*refs):
        xs_ref, bm_ref, cm_ref, dt_ref, a_ref, dsk_ref = refs[:6]
        r_ins = refs[6:6 + nr]
        y_ref, hs_ref = refs[6 + nr:8 + nr]
        r_outs = refs[8 + nr:8 + 2 * nr]
        h_ref = refs[8 + 2 * nr]
        r_sems = refs[9 + 2 * nr:]
        g, c = pl.program_id(0), pl.program_id(1)
        if nr:
            @pl.when((g == 0) & (c == 0))
            def _():
                ride.start(r_ins, r_outs, r_sems)

        @pl.when(c == 0)
        def _():
            h_ref[...] = jnp.zeros_like(h_ref)

        hs_ref[...] = h_ref[...]
        grp = (xs_ref, bm_ref, cm_ref, dt_ref, a_ref, dsk_ref)
        ops = [_ssd_group(grp, q) + (h_ref[:, q * gw:(q + 1) * gw],) for q in range(SSD_GB)]
        res = [_ssd_chunk(*o) for o in ops]
        for q, (y, hn) in enumerate(res):
            y_ref[:, q * gw:(q + 1) * gw] = y
            h_ref[:, q * gw:(q + 1) * gw] = hn

        if nr:
            @pl.when((g == ng - 1) & (c == nc - 1))
            def _():
                ride.wait(r_ins, r_outs, r_sems)

    res = pl.pallas_call(
        body, name=name, grid=(ng, nc), in_specs=_ssd_specs(nc, False) + (ride.specs if nr else []),
        out_specs=[pl.BlockSpec((SSD_L, SSD_GB * gw), lambda g, c: (c, g)),
                   pl.BlockSpec((None, None, SSD_N, SSD_GB * gw), lambda g, c: (g, c, 0, 0))] +
                  (ride.specs if nr else []),
        out_shape=[jax.ShapeDtypeStruct((t, SSD_DI), f32),
                   jax.ShapeDtypeStruct((ng, nc, SSD_N, SSD_GB * gw), f32)] + (ride.out_shape if nr else []),
        scratch_shapes=[pltpu.VMEM((SSD_N, SSD_GB * gw), f32)] + (ride.scratch if nr else []),
        compiler_params=_cp(("arbitrary", "arbitrary")),
    )(xs, bm, cm, dt4, a4, dsk4, *(ride.srcs if nr else []))
    return res[0], res[1], list(res[2:])


def _ssd_bwd(xs, bm, cm, dt4, a4, dsk4, hs, dy, name, ride=None):
    t = xs.shape[0]
    nc = t // SSD_L
    gw = SSD_R * SSD_P
    rc = lambda c: nc - 1 - c
    nr = ride.n if ride is not None else 0
    ng = SSD_G // SSD_GB

    def body(*refs):
        xs_ref, bm_ref, cm_ref, dt_ref, a_ref, dsk_ref, hs_ref, dy_ref = refs[:8]
        r_ins = refs[8:8 + nr]
        dxs_ref, dbm_ref, dcm_ref, ddt_ref, da_ref, ddsk_ref = refs[8 + nr:14 + nr]
        r_outs = refs[14 + nr:14 + 2 * nr]
        dh_ref = refs[14 + 2 * nr]
        r_sems = refs[15 + 2 * nr:]
        if nr:
            @pl.when((pl.program_id(0) == 0) & (pl.program_id(1) == 0))
            def _():
                ride.start(r_ins, r_outs, r_sems)

        @pl.when(pl.program_id(1) == 0)
        def _():
            dh_ref[...] = jnp.zeros_like(dh_ref)
            da_ref[...] = jnp.zeros_like(da_ref)
            ddsk_ref[...] = jnp.zeros_like(ddsk_ref)

        grp = (xs_ref, bm_ref, cm_ref, dt_ref, a_ref, dsk_ref)
        ops = [_ssd_group(grp, q) + (hs_ref[:, q * gw:(q + 1) * gw],) for q in range(SSD_GB)]
        cts = [(dy_ref[:, q * gw:(q + 1) * gw], dh_ref[:, q * gw:(q + 1) * gw]) for q in range(SSD_GB)]
        grads = [jax.vjp(_ssd_chunk, *o)[1](ct) for o, ct in zip(ops, cts)]
        for q, (dxs, dbm, dcm, ddt, da, ddsk, dh) in enumerate(grads):
            wd = slice(q * gw, (q + 1) * gw)
            ln = slice(q * LANE, (q + 1) * LANE)
            dxs_ref[:, wd] = dxs
            dbm_ref[:, ln] = dbm
            dcm_ref[:, ln] = dcm
            ddt_ref[:, wd] = ddt
            da_ref[:, wd] += da
            ddsk_ref[:, wd] += ddsk
            dh_ref[:, wd] = dh

        if nr:
            @pl.when((pl.program_id(0) == ng - 1) & (pl.program_id(1) == nc - 1))
            def _():
                ride.wait(r_ins, r_outs, r_sems)

    res = pl.pallas_call(
        body, name=name, grid=(ng, nc),
        in_specs=_ssd_specs(nc, True) + [
            pl.BlockSpec((None, None, SSD_N, SSD_GB * gw), lambda g, c: (g, rc(c), 0, 0)),
            pl.BlockSpec((SSD_L, SSD_GB * gw), lambda g, c: (rc(c), g))] + (ride.specs if nr else []),
        out_specs=[pl.BlockSpec((SSD_L, SSD_GB * gw), lambda g, c: (rc(c), g)),
                   pl.BlockSpec((SSD_L, SSD_GB * SSD_N), lambda g, c: (rc(c), g)),
                   pl.BlockSpec((SSD_L, SSD_GB * SSD_N), lambda g, c: (rc(c), g)),
                   pl.BlockSpec((SSD_L, SSD_GB * gw), lambda g, c: (rc(c), g)),
                   pl.BlockSpec((1, SSD_GB * gw), lambda g, c: (0, g)),
                   pl.BlockSpec((1, SSD_GB * gw), lambda g, c: (0, g))] + (ride.specs if nr else []),
        out_shape=[jax.ShapeDtypeStruct((t, SSD_DI), f32), jax.ShapeDtypeStruct((t, SSD_G * SSD_N), f32),
                   jax.ShapeDtypeStruct((t, SSD_G * SSD_N), f32), jax.ShapeDtypeStruct((t, SSD_DI), f32),
                   jax.ShapeDtypeStruct((1, SSD_DI), f32), jax.ShapeDtypeStruct((1, SSD_DI), f32)] +
                  (ride.out_shape if nr else []),
        scratch_shapes=[pltpu.VMEM((SSD_N, SSD_GB * gw), f32)] + (ride.scratch if nr else []),
        compiler_params=_cp(("arbitrary", "arbitrary")),
    )(xs, bm, cm, dt4, a4, dsk4, hs, dy, *(ride.srcs if nr else []))
    return list(res[:6]), list(res[6:])


S5_CH = 4096


S5_NB = 8
S5_UB = 128
S5_SB = 512
_NT = (((1,), (1,)), ((), ()))
_TN = (((0,), (0,)), ((), ()))


def _s5_fwd(proj, u0, bd_c, c_c, ar, ai, name):
    t = proj.shape[0]
    tb = min(128, t)
    ub = u0 // S5_W

    def body(u_ref, bd_ref, cc_ref, ar_ref, ai_ref, s_ref, yb_ref, bu_ref, carry):
        @pl.when(pl.program_id(0) == 0)
        def _():
            carry[...] = jnp.zeros_like(carry)

        u = u_ref[...].astype(bf16)
        for j in range(S5_NB):
            uj = u[:, j * S5_UB:(j + 1) * S5_UB]
            for half in range(2):
                bu_ref[:, half * S5_S + j * S5_SB:half * S5_S + (j + 1) * S5_SB] = jnp.dot(
                    uj, bd_ref[j * S5_UB:(j + 1) * S5_UB, half * S5_SB:(half + 1) * S5_SB], preferred_element_type=f32)

        for c0 in range(0, S5_S, S5_CH):
            re = pl.ds(c0, S5_CH)
            im = pl.ds(S5_S + c0, S5_CH)
            a_r = ar_ref[:, re]
            a_i = ai_ref[:, re]

            def step(k, st, re=re, im=im, a_r=a_r, a_i=a_i):
                sr, si = st
                row = pl.ds(k, 1)
                nr = a_r * sr - a_i * si + bu_ref[row, re]
                ni = a_r * si + a_i * sr + bu_ref[row, im]
                s_ref[row, re] = nr
                s_ref[row, im] = ni
                return nr, ni

            sr, si = lax.fori_loop(0, tb, step, (carry[:, re], carry[:, im]))
            carry[:, re] = sr
            carry[:, im] = si

        for j in range(S5_NB):
            lo, hi = j * S5_SB, (j + 1) * S5_SB
            yb_ref[:, j * S5_UB:(j + 1) * S5_UB] = (
                jnp.dot(s_ref[:, lo:hi].astype(bf16), cc_ref[lo:hi, :], preferred_element_type=f32) +
                jnp.dot(s_ref[:, S5_S + lo:S5_S + hi].astype(bf16), cc_ref[S5_S + lo:S5_S + hi, :],
                        preferred_element_type=f32))

    return pl.pallas_call(
        body, name=name, grid=(t // tb,),
        in_specs=[pl.BlockSpec((tb, S5_W), lambda i: (i, ub)), _full_spec(bd_c), _full_spec(c_c),
                  pl.BlockSpec((1, S5_S), lambda i: (0, 0)), pl.BlockSpec((1, S5_S), lambda i: (0, 0))],
        out_specs=[pl.BlockSpec((tb, 2 * S5_S), lambda i: (i, 0)), pl.BlockSpec((tb, S5_W), lambda i: (i, 0))],
        out_shape=[jax.ShapeDtypeStruct((t, 2 * S5_S), f32), jax.ShapeDtypeStruct((t, S5_W), f32)],
        scratch_shapes=[pltpu.VMEM((tb, 2 * S5_S), f32), pltpu.VMEM((1, 2 * S5_S), f32)],
        compiler_params=_cp(("arbitrary",)),
    )(proj, bd_c, c_c, ar, ai)


def _s5_bwd(dyb, s, proj, u0, bd_c, c_c, ar, ai, du_skip, name, ride=None):
    t = dyb.shape[0]
    tb = min(128, t)
    nb = t // tb
    r8 = tb // SUBLANE
    rb = lambda i: nb - 1 - i
    ub = u0 // S5_W

    def body(ins, outs, g_ref, carry):
        dyb_ref, s_ref, sh_ref, u_ref, skip_ref, bd_ref, cc_ref, ar_ref, ai_ref = ins
        du_ref, dar_ref, dai_ref, dbd_ref, dcc_ref = outs
        ds_ref = g_ref
        i = pl.program_id(0)

        @pl.when(i == 0)
        def _():
            carry[...] = jnp.zeros_like(carry)
            dar_ref[...] = jnp.zeros_like(dar_ref)
            dai_ref[...] = jnp.zeros_like(dai_ref)
            dbd_ref[...] = jnp.zeros_like(dbd_ref)
            dcc_ref[...] = jnp.zeros_like(dcc_ref)

        dyb = dyb_ref[...].astype(bf16)
        for jj in range(2 * S5_NB):
            blk = jj % S5_NB
            g_ref[:, jj * S5_SB:(jj + 1) * S5_SB] = lax.dot_general(
                dyb[:, blk * S5_UB:(blk + 1) * S5_UB], cc_ref[jj * S5_SB:(jj + 1) * S5_SB, :], _NT,
                preferred_element_type=f32)

        has_prev = (i < nb - 1).astype(f32)
        for c0 in range(0, S5_S, S5_CH):
            re = pl.ds(c0, S5_CH)
            im = pl.ds(S5_S + c0, S5_CH)
            a_r = ar_ref[:, re]
            a_i = ai_ref[:, re]

            def upd(st, row, sp_r, sp_i, re=re, im=im, a_r=a_r, a_i=a_i):
                gr, gi, acr, aci = st
                ngr = ds_ref[row, re] + a_r * gr + a_i * gi
                ngi = ds_ref[row, im] + a_r * gi - a_i * gr
                g_ref[row, re] = ngr
                g_ref[row, im] = ngi
                return ngr, ngi, acr + ngr * sp_r + ngi * sp_i, aci + ngi * sp_r - ngr * sp_i

            def step(k, st, re=re, im=im, upd=upd):
                tt = tb - 1 - k
                prev = pl.ds(tt - 1, 1)
                return upd(st, pl.ds(tt, 1), s_ref[prev, re], s_ref[prev, im])

            zero = jnp.zeros((1, S5_CH), f32)
            st = lax.fori_loop(0, tb - 1, step, (carry[:, re], carry[:, im], zero, zero))
            last = pl.ds(SUBLANE - 1, 1)
            gr, gi, acr, aci = upd(st, pl.ds(0, 1), sh_ref[last, re] * has_prev, sh_ref[last, im] * has_prev)
            carry[:, re] = gr
            carry[:, im] = gi
            dar_ref[:, re] += acr
            dai_ref[:, re] += aci

        u = u_ref[...].astype(bf16)
        for j in range(S5_NB):
            lo, hi = j * S5_SB, (j + 1) * S5_SB
            blk = slice(j * S5_UB, (j + 1) * S5_UB)
            g_re = g_ref[:, lo:hi].astype(bf16)
            g_im = g_ref[:, S5_S + lo:S5_S + hi].astype(bf16)
            du = (lax.dot_general(g_re, bd_ref[blk, :S5_SB], _NT, preferred_element_type=f32) +
                  lax.dot_general(g_im, bd_ref[blk, S5_SB:], _NT, preferred_element_type=f32) + skip_ref[:, blk])
            du_ref[:, blk] = du.astype(du_ref.dtype)
            dbd_ref[blk, :S5_SB] += lax.dot_general(u[:, blk], g_re, _TN, preferred_element_type=f32)
            dbd_ref[blk, S5_SB:] += lax.dot_general(u[:, blk], g_im, _TN, preferred_element_type=f32)
            dcc_ref[lo:hi, :] += lax.dot_general(s_ref[:, lo:hi].astype(bf16), dyb[:, blk], _TN,
                                                 preferred_element_type=f32)
            dcc_ref[S5_S + lo:S5_S + hi, :] += lax.dot_general(s_ref[:, S5_S + lo:S5_S + hi].astype(bf16), dyb[:, blk],
                                                               _TN, preferred_element_type=f32)

    row_blk = lambda w: pl.BlockSpec((tb, w), lambda i: (rb(i), 0))
    const = lambda shape: pl.BlockSpec(shape, lambda i: (0, 0))
    return _call_with_ride(
        body, (nb,),
        [row_blk(S5_W), row_blk(2 * S5_S),
         pl.BlockSpec((SUBLANE, 2 * S5_S), lambda i: (jnp.maximum(rb(i) * r8 - 1, 0), 0)),
         pl.BlockSpec((tb, S5_W), lambda i: (rb(i), ub)), row_blk(S5_W), const(bd_c.shape), const(c_c.shape),
         const((1, S5_S)), const((1, S5_S))],
        [dyb, s, s, proj, du_skip, bd_c, c_c, ar, ai],
        [row_blk(S5_W), const((1, S5_S)), const((1, S5_S)), const(bd_c.shape), const(c_c.shape)],
        [jax.ShapeDtypeStruct((t, S5_W), bf16), jax.ShapeDtypeStruct((1, S5_S), f32),
         jax.ShapeDtypeStruct((1, S5_S), f32), jax.ShapeDtypeStruct(bd_c.shape, f32),
         jax.ShapeDtypeStruct(c_c.shape, f32)],
        name, ride, scratch=[pltpu.VMEM((tb, 2 * S5_S), f32), pltpu.VMEM((1, 2 * S5_S), f32)], sem=("arbitrary",))


def _ada_fwd(c_all, w, b, name):
    d, n = w.shape
    tn = _tile(n, 1536)

    def body(c_ref, w_ref, b_ref, o_ref):
        a = jax.nn.silu(c_ref[...]).astype(bf16)
        o_ref[...] = jnp.dot(a, w_ref[...].astype(bf16), preferred_element_type=f32) + b_ref[...]

    return pl.pallas_call(
        body, name=name, grid=(n // tn,),
        in_specs=[pl.BlockSpec(c_all.shape, lambda j: (0, 0)), pl.BlockSpec((d, tn), lambda j: (0, j)),
                  pl.BlockSpec((1, tn), lambda j: (0, j))],
        out_specs=pl.BlockSpec((c_all.shape[0], tn), lambda j: (0, j)),
        out_shape=jax.ShapeDtypeStruct((c_all.shape[0], n), f32),
        compiler_params=_cp(("parallel",)),
    )(c_all, w, b)


def _ada_bwd(c_all, dm, name):
    d = c_all.shape[1]
    n = dm.shape[1]
    tn = _tile(n, 1536)

    def body(c_ref, dm_ref, o_ref):
        a = jax.nn.silu(c_ref[...]).astype(bf16)
        o_ref[...] = lax.dot_general(a, dm_ref[...].astype(bf16), (((0,), (0,)), ((), ())), preferred_element_type=f32)

    return pl.pallas_call(
        body, name=name, grid=(n // tn,),
        in_specs=[pl.BlockSpec(c_all.shape, lambda j: (0, 0)), pl.BlockSpec((dm.shape[0], tn), lambda j: (0, j))],
        out_specs=pl.BlockSpec((d, tn), lambda j: (0, j)),
        out_shape=jax.ShapeDtypeStruct((d, n), f32),
        compiler_params=_cp(("parallel",)),
    )(c_all, dm)


def _blk_rows(r, c, nbuf, itemsize=4):
    tr = _tile(r, max(SUBLANE, (RW_VMEM_BUDGET // (2 * nbuf * c * itemsize)) // 16 * 16), 16)
    return tr if r % tr == 0 else r


def _cast_bf16(w, name, cols=None):
    r, c = w.shape
    cols = cols or c
    tr = _blk_rows(r, cols, 2)

    def body(w_ref, o_ref):
        o_ref[:, :c] = w_ref[...].astype(bf16)
        if cols > c:
            o_ref[:, c:] = jnp.zeros((tr, cols - c), bf16)

    return pl.pallas_call(
        body, name=name, grid=(r // tr,), in_specs=[pl.BlockSpec((tr, c), lambda i: (i, 0))],
        out_specs=pl.BlockSpec((tr, cols), lambda i: (i, 0)), out_shape=jax.ShapeDtypeStruct((r, cols), bf16),
        compiler_params=_cp(("parallel",)),
    )(w)


def _sum_lead(parts, name, cols=None):
    n, r, c = parts.shape
    cols = cols or c
    tr = _blk_rows(r, c, n + 2)

    def body(p_ref, o_ref):
        acc = p_ref[0].astype(f32)
        for q in range(1, n):
            acc = acc + p_ref[q].astype(f32)
        o_ref[...] = acc[:, :cols]

    return pl.pallas_call(
        body, name=name, grid=(r // tr,), in_specs=[pl.BlockSpec((n, tr, c), lambda i: (0, i, 0))],
        out_specs=pl.BlockSpec((tr, cols), lambda i: (i, 0)), out_shape=jax.ShapeDtypeStruct((r, cols), f32),
        compiler_params=_cp(("parallel",)),
    )(parts)


def _adamw(w, m, v, parts, name):
    r, c = w.shape
    npart = len(parts)
    tr = _blk_rows(r, c, 7 + npart)
    c1 = 1.0 - ADAM_B1 ** ADAM_STEP
    c2 = 1.0 - ADAM_B2 ** ADAM_STEP

    def body(*refs):
        w_ref, m_ref, v_ref = refs[:3]
        g_ref, d_ref, nm_ref, nv_ref = refs[3 + npart:]
        g = refs[3][...].astype(f32)
        for p in refs[4:3 + npart]:
            g = g + p[...].astype(f32)
        nm = ADAM_B1 * m_ref[...] + (1.0 - ADAM_B1) * g
        nv = ADAM_B2 * v_ref[...] + (1.0 - ADAM_B2) * jnp.square(g)
        g_ref[...] = g
        nm_ref[...] = nm
        nv_ref[...] = nv
        d_ref[...] = -ADAM_LR * ((nm / c1) / (jnp.sqrt(nv / c2) + ADAM_EPS) + ADAM_WD * w_ref[...])

    spec = pl.BlockSpec((tr, c), lambda i: (i, 0))
    return pl.pallas_call(
        body, name=name, grid=(r // tr,), in_specs=[spec] * (3 + npart), out_specs=[spec] * 4,
        out_shape=[jax.ShapeDtypeStruct((r, c), f32)] * 4, compiler_params=_cp(("parallel",)),
    )(w, m, v, *parts)


def _ag_small(x_shard, name):
    m_per, n = x_shard.shape

    def body(x_ref, out_ref, send_sems, recv_sems, local_sem):
        x, y, c = lax.axis_index("x"), lax.axis_index("y"), lax.axis_index("c")
        me, sibling = (x, y, c), (x, y, 1 - c)
        chips = [(1 - x, y), (x, 1 - y), (1 - x, 1 - y)]

        def rows(px, py, pc):
            return out_ref.at[pl.ds((4 * px + 2 * py + pc) * m_per, m_per), :]

        def copy(k, block, to, src=None):
            return pltpu.make_async_remote_copy(
                src_ref=rows(*block) if src is None else src, dst_ref=rows(*block),
                send_sem=send_sems.at[k], recv_sem=recv_sems.at[k], device_id=to, device_id_type=MESH)

        mine = pltpu.make_async_copy(x_ref, rows(*me), local_sem)
        mine.start()
        first = [copy(0, me, sibling, src=x_ref)]
        first += [copy(1 + j, me, (*chip, c), src=x_ref) for j, chip in enumerate(chips)]
        for cp in first:
            cp.start()
        passed = [copy(4 + j, (*chip, c), sibling) for j, chip in enumerate(chips)]
        for j, chip in enumerate(chips):
            copy(1 + j, (*chip, c), me).wait_recv()
            passed[j].start()
        copy(0, sibling, me).wait_recv()
        for j, chip in enumerate(chips):
            copy(4 + j, (*chip, 1 - c), me).wait_recv()
        for cp in first + passed:
            cp.wait_send()
        mine.wait()

    return pl.pallas_call(
        body, name=name, out_shape=jax.ShapeDtypeStruct((8 * m_per, n), x_shard.dtype),
        in_specs=[pl.BlockSpec(memory_space=pltpu.VMEM)], out_specs=pl.BlockSpec(memory_space=pltpu.VMEM),
        scratch_shapes=[pltpu.SemaphoreType.DMA((7,)), pltpu.SemaphoreType.DMA((7,)), pltpu.SemaphoreType.DMA],
        compiler_params=pltpu.CompilerParams(vmem_limit_bytes=VMEM_LIMIT),
    )(x_shard)


def _run_ride(ride, name):
    n = ride.n

    def body(*refs):
        ride.start(refs[:n], refs[n:2 * n], refs[2 * n:])
        ride.wait(refs[:n], refs[n:2 * n], refs[2 * n:])

    return pl.pallas_call(
        body, name=name, out_shape=ride.out_shape, in_specs=ride.specs, out_specs=ride.specs,
        scratch_shapes=ride.scratch,
    )(*ride.srcs)


class _Ride:
    def __init__(self, srcs, scatter):
        self.srcs, self.scatter, self.n = list(srcs), scatter, len(srcs)
        n = self.n
        self.out_shape = [jax.ShapeDtypeStruct(s.shape if scatter else (4,) + s.shape, s.dtype) for s in srcs]
        self.specs = [pl.BlockSpec(memory_space=pl.ANY)] * n
        self.scratch = [pltpu.SemaphoreType.DMA((3 * n,)), pltpu.SemaphoreType.DMA((3 * n,)),
                        pltpu.SemaphoreType.DMA((n,))]

    def _copies(self, ins, outs, sems):
        send_sems, recv_sems, local_sems = sems
        x, y, c = lax.axis_index("x"), lax.axis_index("y"), lax.axis_index("c")
        my_k = 2 * x + y
        peers = [(1 - x, y), (x, 1 - y), (1 - x, 1 - y)]
        local, sends, recvs = [], [], []
        for a in range(self.n):
            own = ins[a].at[my_k] if self.scatter else ins[a]
            local.append(pltpu.make_async_copy(own, outs[a].at[my_k], local_sems.at[a]))
            for j, (px, py) in enumerate(peers):
                sems_j = dict(send_sem=send_sems.at[3 * a + j], recv_sem=recv_sems.at[3 * a + j],
                              device_id=(px, py, c), device_id_type=MESH)
                src = ins[a].at[2 * px + py] if self.scatter else ins[a]
                sends.append(pltpu.make_async_remote_copy(src_ref=src, dst_ref=outs[a].at[my_k], **sems_j))
                landed = outs[a].at[2 * px + py]
                recvs.append(pltpu.make_async_remote_copy(src_ref=landed, dst_ref=landed, **sems_j))
        return local, sends, recvs

    def start(self, ins, outs, sems):
        local, sends, _ = self._copies(ins, outs, sems)
        for cp in local + sends:
            cp.start()

    def wait(self, ins, outs, sems):
        local, sends, recvs = self._copies(ins, outs, sems)
        for cp in recvs:
            cp.wait_recv()
        for cp in sends:
            cp.wait_send()
        for cp in local:
            cp.wait()


class _RideGather:
    def __init__(self, srcs):
        self.srcs, self.n = list(srcs), len(srcs)
        n = self.n
        assert all(s.shape[0] % 32 == 0 for s in srcs)
        self.out_shape = [jax.ShapeDtypeStruct((4,) + s.shape, s.dtype) for s in srcs]
        self.specs = [pl.BlockSpec(memory_space=pl.ANY)] * n
        dma = pltpu.SemaphoreType.DMA
        self.scratch = [dma((3 * n,)), dma((3 * n,)), dma((3 * n,)), dma((3 * n,)), dma((n,))]

    def _copies(self, ins, outs, sems):
        send_sems, recv_sems, pass_send, pass_recv, local_sems = sems
        x, y, c = lax.axis_index("x"), lax.axis_index("y"), lax.axis_index("c")
        my_k = 2 * x + y
        peers = [(1 - x, y), (x, 1 - y), (1 - x, 1 - y)]
        local, sends, recvs, passes, pass_recvs = [], [], [], [], []
        for a in range(self.n):
            half = self.srcs[a].shape[0] // 2
            mine = pl.ds(pl.multiple_of(c * half, 16), half)
            other = pl.ds(pl.multiple_of((1 - c) * half, 16), half)
            local.append(pltpu.make_async_copy(ins[a], outs[a].at[my_k], local_sems.at[a]))
            for j, (px, py) in enumerate(peers):
                q = 3 * a + j
                over_ici = dict(send_sem=send_sems.at[q], recv_sem=recv_sems.at[q], device_id=(px, py, c),
                                device_id_type=MESH)
                to_sibling = dict(send_sem=pass_send.at[q], recv_sem=pass_recv.at[q], device_id=(x, y, 1 - c),
                                  device_id_type=MESH)
                sends.append(pltpu.make_async_remote_copy(src_ref=ins[a].at[mine], dst_ref=outs[a].at[my_k, mine],
                                                          **over_ici))
                landed = outs[a].at[2 * px + py, mine]
                recvs.append(pltpu.make_async_remote_copy(src_ref=landed, dst_ref=landed, **over_ici))
                passes.append(pltpu.make_async_remote_copy(src_ref=landed, dst_ref=landed, **to_sibling))
                from_sibling = outs[a].at[2 * px + py, other]
                pass_recvs.append(pltpu.make_async_remote_copy(src_ref=from_sibling, dst_ref=from_sibling, **to_sibling))
        return local, sends, recvs, passes, pass_recvs

    def start(self, ins, outs, sems):
        local, sends = self._copies(ins, outs, sems)[:2]
        for cp in local + sends:
            cp.start()

    def wait(self, ins, outs, sems):
        local, sends, recvs, passes, pass_recvs = self._copies(ins, outs, sems)
        for rc, ps in zip(recvs, passes):
            rc.wait_recv()
            ps.start()
        for cp in pass_recvs:
            cp.wait_recv()
        for cp in sends + passes:
            cp.wait_send()
        for cp in local:
            cp.wait()


class _RideSwap:
    def __init__(self, srcs):
        self.srcs, self.n = list(srcs), len(srcs)
        self.out_shape = [jax.ShapeDtypeStruct(s.shape, s.dtype) for s in srcs]
        self.specs = [pl.BlockSpec(memory_space=pl.ANY)] * self.n
        self.scratch = [pltpu.SemaphoreType.DMA((self.n,)), pltpu.SemaphoreType.DMA((self.n,))]

    def _copies(self, ins, outs, sems):
        send_sems, recv_sems = sems
        sib = (lax.axis_index("x"), lax.axis_index("y"), 1 - lax.axis_index("c"))
        return [pltpu.make_async_remote_copy(src_ref=ins[a], dst_ref=outs[a], send_sem=send_sems.at[a],
                                             recv_sem=recv_sems.at[a], device_id=sib, device_id_type=MESH)
                for a in range(self.n)]

    def start(self, ins, outs, sems):
        for cp in self._copies(ins, outs, sems):
            cp.start()

    def wait(self, ins, outs, sems):
        cps = self._copies(ins, outs, sems)
        for cp in cps:
            cp.wait_recv()
        for cp in cps:
            cp.wait_send()


class _Rides:
    def __init__(self, rides):
        self.rides = list(rides)
        self.n = sum(r.n for r in self.rides)
        self.srcs = [s for r in self.rides for s in r.srcs]
        self.out_shape = [s for r in self.rides for s in r.out_shape]
        self.specs = [s for r in self.rides for s in r.specs]
        self.scratch = [s for r in self.rides for s in r.scratch]

    def _each(self, ins, outs, sems):
        i = k = 0
        for r in self.rides:
            yield r, ins[i:i + r.n], outs[i:i + r.n], sems[k:k + len(r.scratch)]
            i += r.n
            k += len(r.scratch)

    def start(self, ins, outs, sems):
        for r, a, b, c in self._each(ins, outs, sems):
            r.start(a, b, c)

    def wait(self, ins, outs, sems):
        for r, a, b, c in self._each(ins, outs, sems):
            r.wait(a, b, c)

    def split(self, results):
        out, i = [], 0
        for r in self.rides:
            out.append(results[i:i + r.n])
            i += r.n
        return out


def _gather8(vec, name):
    size = vec.shape[0]
    n = _round_up(size, SUBLANE * LANE)
    blk = jnp.concatenate([vec, jnp.zeros((n - size,), f32)]).reshape(SUBLANE, n // SUBLANE)
    out = _ag_small(blk, name)
    return out.reshape(8, n)[:, :size]


def _head_spread_matrices():
    e = np.zeros((LANE, SSD_DI), np.float32)
    for h in range(SSD_HEADS):
        e[h, h * SSD_P:(h + 1) * SSD_P] = 1.0
    return jnp.asarray(e, bf16), jnp.asarray(e.T, bf16)


def _heads_to_lanes(v):
    return jnp.repeat(v, SSD_P).reshape(1, SSD_DI)


def _block_diag8(blocks):
    g, r, c = blocks.shape
    b = blocks.reshape(g // S5_NB, S5_NB, r, c)
    eye = jnp.eye(S5_NB, dtype=bool)[None, :, None, :, None]
    return jnp.where(eye, b[:, :, :, None, :], jnp.zeros((), blocks.dtype)).reshape(g * r, S5_NB * c)


def _diag8(mat, r, c):
    g = mat.shape[0] // r
    m = mat.reshape(g // S5_NB, S5_NB, r, S5_NB, c)
    eye = jnp.eye(S5_NB, dtype=bool)[None, :, None, :, None]
    return jnp.where(eye, m, 0.0).sum(axis=3).reshape(g, r, c)


class _Layout:
    def __init__(self, d):
        self.d = d
        self.z, self.xbc, self.u = 0, SSD_DI, SSD_DI + CONV_DIM
        self.ga = self.u + S5_W
        self.gb = self.ga + d
        self.dt = self.gb + d
        self.np_ = self.dt + LANE
        self.in_cols = SSD_DI + CONV_DIM + SSD_HEADS + S5_W + 2 * d
        off_dt = SSD_DI + CONV_DIM
        off_u = off_dt + SSD_HEADS
        self.src = [(0, off_dt), (off_u, off_u + S5_W + 2 * d), (off_dt, off_u)]

    def arrange_slabs(self, g):
        pieces = [p for lo, hi in self.src for p in _cols_from_slabs(g, lo, hi)]
        pieces.append(jnp.zeros((g.shape[1], LANE - SSD_HEADS), g.dtype))
        return jnp.concatenate(pieces, axis=1)

    def restore_slabs(self, chunks):
        (a0, a1), (b0, b1), (c0, c1) = self.src
        n_a, n_b = a1 - a0, b1 - b0
        segs = [(a0, a1, 0), (c0, c1, n_a + n_b), (b0, b1, n_a)]
        firsts = np.cumsum([0] + [c.shape[1] for c in chunks])

        def take(lo, hi):
            return [c[:, max(lo, f) - f:min(hi, f + c.shape[1]) - f] for c, f in zip(chunks, firsts)
                    if max(lo, f) < min(hi, f + c.shape[1])]

        cs = self.in_cols // 4
        slabs = []
        for k in range(4):
            lo, hi = k * cs, (k + 1) * cs
            parts = [p for s0, s1, pos in segs if max(lo, s0) < min(hi, s1)
                     for p in take(pos + max(lo, s0) - s0, pos + min(hi, s1) - s0)]
            slabs.append(jnp.concatenate(parts, axis=1))
        return jnp.stack(slabs)


def _cols_from_slabs(g, start, stop):
    c = g.shape[2]
    return [g[k][:, max(start, k * c) - k * c:min(stop, (k + 1) * c) - k * c] for k in range(4)
            if max(start, k * c) < min(stop, (k + 1) * c)]


def _unshard_cols(g):
    return jnp.concatenate([g[k] for k in range(4)], axis=1)


def _shard_cols(w):
    r, c4 = w.shape
    return w.reshape(r, 4, c4 // 4).transpose(1, 0, 2)


def kernel(x, c, w_ada, b_ada, norm_ffn1, w_ffn1_in, w_ffn1_out, norm_mix, w_in, conv_w, conv_b, dt_bias, a_log, d_ssd, ssd_norm_w, w_a_proj, s5_lambda_re, s5_lambda_im, s5_b_re, s5_b_im, s5_c_re, s5_c_im, s5_d, s5_log_dt, w_b_glu, w_out, norm_ffn2, w_ffn2_in, w_ffn2_out, norm_final, loss_target, m_w_ada, m_b_ada, m_norm_ffn1, m_w_ffn1_in, m_w_ffn1_out, m_norm_mix, m_w_in, m_conv_w, m_conv_b, m_dt_bias, m_a_log, m_d_ssd, m_ssd_norm_w, m_w_a_proj, m_s5_lambda_re, m_s5_lambda_im, m_s5_b_re, m_s5_b_im, m_s5_c_re, m_s5_c_im, m_s5_d, m_s5_log_dt, m_w_b_glu, m_w_out, m_norm_ffn2, m_w_ffn2_in, m_w_ffn2_out, m_norm_final, v_w_ada, v_b_ada, v_norm_ffn1, v_w_ffn1_in, v_w_ffn1_out, v_norm_mix, v_w_in, v_conv_w, v_conv_b, v_dt_bias, v_a_log, v_d_ssd, v_ssd_norm_w, v_w_a_proj, v_s5_lambda_re, v_s5_lambda_im, v_s5_b_re, v_s5_b_im, v_s5_c_re, v_s5_c_im, v_s5_d, v_s5_log_dt, v_w_b_glu, v_w_out, v_norm_ffn2, v_w_ffn2_in, v_w_ffn2_out, v_norm_final):
    W = dict(w_ada=w_ada, b_ada=b_ada, norm_ffn1=norm_ffn1, w_ffn1_in=w_ffn1_in, w_ffn1_out=w_ffn1_out, norm_mix=norm_mix, w_in=w_in, conv_w=conv_w, conv_b=conv_b, dt_bias=dt_bias, a_log=a_log, d_ssd=d_ssd, ssd_norm_w=ssd_norm_w, w_a_proj=w_a_proj, s5_lambda_re=s5_lambda_re, s5_lambda_im=s5_lambda_im, s5_b_re=s5_b_re, s5_b_im=s5_b_im, s5_c_re=s5_c_re, s5_c_im=s5_c_im, s5_d=s5_d, s5_log_dt=s5_log_dt, w_b_glu=w_b_glu, w_out=w_out, norm_ffn2=norm_ffn2, w_ffn2_in=w_ffn2_in, w_ffn2_out=w_ffn2_out, norm_final=norm_final)
    Mo = dict(w_ada=m_w_ada, b_ada=m_b_ada, norm_ffn1=m_norm_ffn1, w_ffn1_in=m_w_ffn1_in, w_ffn1_out=m_w_ffn1_out, norm_mix=m_norm_mix, w_in=m_w_in, conv_w=m_conv_w, conv_b=m_conv_b, dt_bias=m_dt_bias, a_log=m_a_log, d_ssd=m_d_ssd, ssd_norm_w=m_ssd_norm_w, w_a_proj=m_w_a_proj, s5_lambda_re=m_s5_lambda_re, s5_lambda_im=m_s5_lambda_im, s5_b_re=m_s5_b_re, s5_b_im=m_s5_b_im, s5_c_re=m_s5_c_re, s5_c_im=m_s5_c_im, s5_d=m_s5_d, s5_log_dt=m_s5_log_dt, w_b_glu=m_w_b_glu, w_out=m_w_out, norm_ffn2=m_norm_ffn2, w_ffn2_in=m_w_ffn2_in, w_ffn2_out=m_w_ffn2_out, norm_final=m_norm_final)
    Vo = dict(w_ada=v_w_ada, b_ada=v_b_ada, norm_ffn1=v_norm_ffn1, w_ffn1_in=v_w_ffn1_in, w_ffn1_out=v_w_ffn1_out, norm_mix=v_norm_mix, w_in=v_w_in, conv_w=v_conv_w, conv_b=v_conv_b, dt_bias=v_dt_bias, a_log=v_a_log, d_ssd=v_d_ssd, ssd_norm_w=v_ssd_norm_w, w_a_proj=v_w_a_proj, s5_lambda_re=v_s5_lambda_re, s5_lambda_im=v_s5_lambda_im, s5_b_re=v_s5_b_re, s5_b_im=v_s5_b_im, s5_c_re=v_s5_c_re, s5_c_im=v_s5_c_im, s5_d=v_s5_d, s5_log_dt=v_s5_log_dt, w_b_glu=v_w_b_glu, w_out=v_w_out, norm_ffn2=v_norm_ffn2, w_ffn2_in=v_w_ffn2_in, w_ffn2_out=v_w_ffn2_out, norm_final=v_norm_final)

    t, d = x.shape[1], x.shape[2]
    ff = 4 * w_ffn1_out.shape[1]
    hf = ff // 2
    hfp = _round_up(hf, LANE)
    lay = _Layout(d)
    xi, yi, ci = lax.axis_index("x"), lax.axis_index("y"), lax.axis_index("c")
    k_me = 2 * xi + yi
    e_me = 4 * xi + 2 * yi + ci
    x2d = x[0]
    tgt = loss_target[0]

    cw_cols = conv_w.shape[2]
    g1 = _gather8(jnp.concatenate([c[0], conv_w[0].reshape(-1)]), "gather_c_convw")
    c_all = g1[:, :d]
    conv_full = g1[::2, d:].reshape(4, CONV_K, cw_cols).transpose(1, 0, 2).reshape(CONV_K, CONV_DIM)
    conv_w8 = jnp.zeros((SUBLANE, CONV_DIM), f32).at[:CONV_K].set(conv_full)

    n_ada_loc = w_ada.shape[2]
    b_loc = lax.dynamic_slice(b_ada, (0, k_me * n_ada_loc), (1, n_ada_loc))
    mods_part = _ada_fwd(c_all, w_ada[0], b_loc, "ada_fwd")
    g2 = _gather8(mods_part.reshape(-1), "gather_mods").reshape(8, 8, n_ada_loc)
    mods = lax.dynamic_index_in_dim(g2[::2], e_me, axis=1, keepdims=False).reshape(N_ADA, d)
    sh1, sc1, gt1, sh2, sc2, gt2, sh3, sc3, gt3 = [mods[i:i + 1] for i in range(N_ADA)]

    cast = {n: _cast_bf16(W[n][0], "cast_" + n, cols=hfp if n in ('w_ffn1_in', 'w_ffn2_in') else None)
            for n in BIG}

    def gather_of(names):
        return _RideGather([cast[n] for n in names])

    def rows_of(g):
        return g.reshape(4 * g.shape[1], g.shape[2])

    def ffn_out(g):
        z = jnp.zeros((hfp - hf, g.shape[2]), g.dtype)
        return jnp.concatenate([g[0], g[1], z, g[2], g[3], z], axis=0)

    nf1, nmx, nf2 = norm_ffn1, norm_mix, norm_ffn2
    nfin = norm_final.reshape(1, d)

    (g_w1i,) = _run_ride(gather_of(['w_ffn1_in']), "gather_w_ffn1_in")
    w1i = g_w1i
    (h1,) = _rw(_f_mod, [x2d], [nf1, sh1, sc1], [(d, bf16)], name="mod1")
    ab1, act1, (g_w1o, g_win) = _ffn_in(h1, w1i, "ffn1_in", ride=gather_of(['w_ffn1_out', 'w_in']))
    w1o = ffn_out(g_w1o)
    w_inr = lay.arrange_slabs(g_win)
    f1, (g_wa, g_wglu, g_wo) = _mm(act1, w1o, 'nn', out_dtype=bf16, name="ffn1_out",
                                   ride=gather_of(['w_a_proj', 'w_b_glu', 'w_out']))
    w_a = rows_of(g_wa)
    w_glu, w_o = _unshard_cols(g_wglu), rows_of(g_wo)
    res1 = functools.partial(_f_res_mod, 0.5)
    x1, h2 = _rw(res1, [x2d, f1], [gt1, nmx, sh2, sc2], [(d, f32), (d, bf16)], name="res1_mod2")
    proj, (g_w2i,) = _mm(h2, w_inr, 'nn', out_dtype=f32, name="in_proj", ride=gather_of(['w_ffn2_in']))
    w2i = g_w2i

    pre = _conv_fwd(proj, lay.xbc, conv_w8, conv_b, "conv_fwd")
    spread, spread_t = _head_spread_matrices()
    bias128 = jnp.zeros((1, LANE), f32).at[:, :SSD_HEADS].set(dt_bias)
    xs, bm, cm, dt4 = _rw(_f_ssd_pre, [pre, (proj, lay.dt, LANE)], [bias128, spread, spread_t],
                          [(SSD_DI, f32), (SSD_G * SSD_N, f32), (SSD_G * SSD_N, f32), (SSD_DI, f32)],
                          name="ssd_pre")

    def head_params(a_log_, d_ssd_):
        return _heads_to_lanes(-jnp.exp(a_log_[0])), _heads_to_lanes(d_ssd_[0])

    (a4, dsk4), head_vjp = jax.vjp(head_params, a_log, d_ssd)
    y_ssd, hs, (g_w2o,) = _ssd_fwd(xs, bm, cm, dt4, a4, dsk4, "ssd_fwd", ride=gather_of(['w_ffn2_out']))
    w2o = ffn_out(g_w2o)
    (y_a,) = _rw(_f_ssd_post, [y_ssd, (proj, lay.z, SSD_DI)], [ssd_norm_w], [(SSD_DI, bf16)], name="ssd_post")
    p_a = _mm(y_a, w_a, 'nn', out_dtype=bf16, name="a_proj")

    col = lambda v: v.reshape(S5_S, 1)
    ldt_col = jnp.repeat(s5_log_dt[0], S5_P).reshape(S5_S, 1)
    prep_rows = [col(s5_lambda_re[0]), col(s5_lambda_im[0]), ldt_col,
                 s5_b_re[0].reshape(S5_S, S5_I), s5_b_im[0].reshape(S5_S, S5_I)]
    ar, ai, bbr, bbi = _rw(_f_s5_prep, prep_rows, [], [(1, f32), (1, f32), (S5_I, f32), (S5_I, f32)],
                           name="s5_prep", tm=512)
    to_bd = lambda bb: _block_diag8(bb.reshape(S5_G, S5_P, S5_I).transpose(0, 2, 1).astype(bf16))
    bd_c = jnp.concatenate([to_bd(bbr), to_bd(bbi)], axis=1)
    c_c = jnp.concatenate([_block_diag8(s5_c_re[0].transpose(0, 2, 1).astype(bf16)),
                           _block_diag8((-s5_c_im[0]).transpose(0, 2, 1).astype(bf16))], axis=0)
    ar_row, ai_row = ar.reshape(1, S5_S), ai.reshape(1, S5_S)
    s5s, yb = _s5_fwd(proj, lay.u, bd_c, c_c, ar_row, ai_row, "s5_fwd")
    d_row = s5_d[0].reshape(1, S5_W)
    (gl,) = _rw(_f_s5_post, [yb, (proj, lay.u, S5_W)], [d_row], [(S5_W, bf16)], name="s5_post")
    glu = _mm(gl, w_glu, 'nn', out_dtype=bf16, name="glu_proj")

    merge_rows = [p_a, glu, (proj, lay.ga, d), (proj, lay.gb, d)]
    (merged,) = _rw(_f_merge, merge_rows, [], [(d, bf16)], name="merge")
    o_mix = _mm(merged, w_o, 'nn', out_dtype=bf16, name="out_proj")
    res2 = functools.partial(_f_res_mod, 1.0)
    x2, h3 = _rw(res2, [x1, o_mix], [gt2, nf2, sh3, sc3], [(d, f32), (d, bf16)], name="res2_mod3")
    ab2, act2, _ = _ffn_in(h3, w2i, "ffn2_in")
    f2 = _mm(act2, w2o, 'nn', out_dtype=bf16, name="ffn2_out")

    ones = jnp.ones((t, 1), f32)
    dx2, df2, dgt3, dnfin, loss_acc = _rw_vjp(_f_final, [x2, f2, tgt], [gt3, nfin], [ones],
                                              row_grads=[f32, bf16, None], param_grads=[True, True],
                                              name="loss_and_bwd", total_of=0)
    loss = lax.psum(loss_acc[0, 0], AXES)
    def ffn_out_back(g):
        return jnp.concatenate([g[:hf], g[hfp:hfp + hf]], axis=0).reshape(4, ff // 4, g.shape[1])

    def rows_back(g, rows):
        return g.reshape(4, rows // 4, g.shape[1])

    def scatter_of(pairs):
        return _Ride([g for _, g in pairs], True)

    terms = {}
    dab2, _ = _ffn_out_dx(df2, w2o, ab2, "ffn2_out_dx")
    dw2o = _mm(act2, df2, 'tn', out_dtype=bf16, name="ffn2_out_dw")
    dh3, (terms['w_ffn2_out'],) = _ffn_in_dx(dab2, w2i, "ffn2_in_dx",
                                             ride=scatter_of([('w_ffn2_out', ffn_out_back(dw2o))]))
    dw2i = _ffn_in_dw(h3, dab2, "ffn2_in_dw")
    dx1, do_mix, dgt2, dnf2, dsh3, dsc3 = _rw_vjp(
        res2, [x1, o_mix], [gt2, nf2, sh3, sc3], [dx2, dh3], row_grads=[f32, bf16], param_grads=[True] * 4,
        name="res2_mod3_bwd")
    dmerged = _mm(do_mix, w_o, 'nt', out_dtype=bf16, name="out_proj_dx")
    dw_o = _mm(merged, do_mix, 'tn', out_dtype=bf16, name="out_proj_dw")
    dp_a, dglu, dga, dgb = _rw_vjp(_f_merge, merge_rows, [], [dmerged], row_grads=[bf16] * 4,
                                   param_grads=[], name="merge_bwd")

    dgl = _mm(dglu, w_glu, 'nt', out_dtype=bf16, name="glu_proj_dx")
    dw_glu = _mm(gl, dglu, 'tn', out_dtype=bf16, name="glu_proj_dw")
    dyb, du_skip, dd_row = _rw_vjp(_f_s5_post, [yb, (proj, lay.u, S5_W)], [d_row], [dgl],
                                   row_grads=[bf16, f32], param_grads=[True], name="s5_post_bwd")
    (du, dar, dai, dbd_c, dc_c), (terms['w_ffn2_in'],) = _s5_bwd(
        dyb, s5s, proj, lay.u, bd_c, c_c, ar_row, ai_row, du_skip, "s5_bwd",
        ride=scatter_of([('w_ffn2_in', dw2i)]))
    from_bd = lambda m_: _diag8(m_, S5_I, S5_P).transpose(0, 2, 1).reshape(S5_S, S5_I)
    dprep = _rw_vjp(_f_s5_prep, prep_rows, [], [dar.reshape(S5_S, 1), dai.reshape(S5_S, 1),
                                                from_bd(dbd_c[:, :S5_SB]), from_bd(dbd_c[:, S5_SB:])],
                    row_grads=[f32] * 5, param_grads=[], name="s5_prep_bwd", tm=512)
    dlr, dli, dldt, dbr, dbi = dprep
    g_s5 = dict(
        s5_lambda_re=dlr.reshape(S5_G, S5_P), s5_lambda_im=dli.reshape(S5_G, S5_P),
        s5_log_dt=dldt.reshape(S5_G, S5_P).sum(axis=1),
        s5_b_re=dbr.reshape(S5_G, S5_P, S5_I), s5_b_im=dbi.reshape(S5_G, S5_P, S5_I),
        s5_c_re=_diag8(dc_c[:S5_S], S5_P, S5_I).transpose(0, 2, 1),
        s5_c_im=-_diag8(dc_c[S5_S:], S5_P, S5_I).transpose(0, 2, 1),
        s5_d=dd_row.reshape(S5_G, S5_I))

    dy_a = _mm(dp_a, w_a, 'nt', out_dtype=bf16, name="a_proj_dx")
    dw_a = _mm(y_a, dp_a, 'tn', out_dtype=bf16, name="a_proj_dw")
    dy_ssd, dz, dssd_nw = _rw_vjp(_f_ssd_post, [y_ssd, (proj, lay.z, SSD_DI)], [ssd_norm_w], [dy_a],
                                  row_grads=[f32, bf16], param_grads=[True], name="ssd_post_bwd")
    early = [('w_out', rows_back(dw_o, d)), ('w_b_glu', _shard_cols(dw_glu)), ('w_a_proj', rows_back(dw_a, SSD_DI))]
    (dxs, dbm, dcm, ddt4, da4, ddsk4), landed = _ssd_bwd(xs, bm, cm, dt4, a4, dsk4, hs, dy_ssd, "ssd_bwd",
                                                         ride=scatter_of(early))
    terms.update({n: p for (n, _), p in zip(early, landed)})
    da_log, dd_ssd = head_vjp((da4, ddsk4))
    dpre, ddt_raw, dbias128 = _rw_vjp(_f_ssd_pre, [pre, (proj, lay.dt, LANE)], [bias128, spread, spread_t],
                                      [dxs, dbm, dcm, ddt4], row_grads=[f32, bf16],
                                      param_grads=[True, False, False], name="ssd_pre_bwd")
    dxbc, dconv_w8, dconv_b = _conv_bwd(dpre, proj, lay.xbc, conv_w8, "conv_bwd")

    dproj = [dz, dxbc, du, dga, dgb]
    dw_in = [_mm(h2, p, 'tn', out_dtype=bf16, name="in_proj_dw_%d" % i) for i, p in enumerate(dproj + [ddt_raw])]
    dh2, (terms['w_in'],) = _in_proj_dx(dproj, ddt_raw, lay.dt, w_inr, "in_proj_dx",
                                        ride=scatter_of([('w_in', lay.restore_slabs(dw_in))]))
    dx0, df1, dgt1, dnmx, dsh2, dsc2 = _rw_vjp(
        res1, [x2d, f1], [gt1, nmx, sh2, sc2], [dx1, dh2], row_grads=[f32, bf16], param_grads=[True] * 4,
        name="res1_mod2_bwd")
    dw1o = _mm(act1, df1, 'tn', out_dtype=bf16, name="ffn1_out_dw")
    dab1, (terms['w_ffn1_out'],) = _ffn_out_dx(df1, w1o, ab1, "ffn1_out_dx",
                                               ride=scatter_of([('w_ffn1_out', ffn_out_back(dw1o))]))
    dw1i = _ffn_in_dw(h1, dab1, "ffn1_in_dw")

    last = 'w_ffn1_in'
    keep = {'w_ffn1_in': hf, 'w_ffn2_in': hf}
    sums = {n: _sum_lead(terms[n], "sum_" + n, cols=keep.get(n)) for n in BIG if n != last}
    swap = _RideSwap([sums[n] for n in BIG if n != last])
    rides = _Rides([scatter_of([(last, dw1i)]), swap])
    dh1, landed = _ffn_in_dx(dab1, w1i, "ffn1_in_dx", ride=rides)
    (terms[last],), swapped = rides.split(landed)
    others = dict(zip([n for n in BIG if n != last], swapped))
    grad_x, dnf1, dsh1, dsc1 = _rw_vjp(_f_mod_keep, [x2d], [nf1, sh1, sc1], [dh1, dx0],
                                       row_grads=[f32], param_grads=[True] * 3, name="mod1_bwd")
    d_mods = jnp.concatenate([dsh1, dsc1, dgt1, dsh2, dsc2, dgt2, dsh3, dsc3, dgt3], axis=1).reshape(-1)
    sums[last] = _sum_lead(terms[last], "sum_" + last, cols=keep.get(last))
    (others[last],) = _run_ride(_RideSwap([sums[last]]), "swap_sum_" + last)

    out_g, out_d, out_m, out_v = {}, {}, {}, {}
    for n in BIG:
        r = _adamw(W[n][0], Mo[n][0], Vo[n][0], [sums[n], others[n]], "adamw_" + n)
        out_g[n], out_d[n], out_m[n], out_v[n] = [o[None] for o in r]

    local = dict(
        b_ada=d_mods, norm_ffn1=dnf1, norm_mix=dnmx, conv_w=dconv_w8[:CONV_K], conv_b=dconv_b,
        dt_bias=dbias128[:, :SSD_HEADS], a_log=da_log, d_ssd=dd_ssd, ssd_norm_w=dssd_nw,
        norm_ffn2=dnf2, norm_final=dnfin, **g_s5)
    flat = jnp.concatenate([local[n].reshape(-1) for n in SMALL])
    g3 = _gather8(flat, "gather_small_grads")
    n_small = flat.shape[0]
    npad = _round_up(n_small, SUBLANE * LANE)
    g3p = jnp.zeros((8, npad), f32).at[:, :n_small].set(g3).reshape(8, npad // LANE, LANE)
    gsum = _sum_lead(g3p, "sum_small").reshape(-1)

    def local_shard(n, a):
        if n == 'conv_w':
            return lax.dynamic_slice(a.reshape(CONV_K, CONV_DIM), (0, k_me * cw_cols), (CONV_K, cw_cols))
        return a

    pieces, off = {}, 0
    for n in SMALL:
        sz = local[n].size
        pieces[n] = local_shard(n, gsum[off:off + sz]).reshape(W[n].shape)
        off += sz

    def pack(dct):
        v_ = jnp.concatenate([dct[n].reshape(-1) for n in SMALL])
        pad = _round_up(v_.shape[0], SUBLANE * LANE) - v_.shape[0]
        return jnp.concatenate([v_, jnp.ones((pad,), f32)]).reshape(-1, LANE)

    rs = _adamw(pack(W), pack(Mo), pack(Vo), [pack(pieces)], "adamw_small")
    off = 0
    for n in SMALL:
        sz = W[n].size
        out_g[n], out_d[n], out_m[n], out_v[n] = [o.reshape(-1)[off:off + sz].reshape(W[n].shape) for o in rs]
        off += sz

    dm_loc = lax.dynamic_slice(g3[:, :N_ADA * d], (0, k_me * n_ada_loc), (SUBLANE, n_ada_loc))
    g_ada = _ada_bwd(c_all, dm_loc, "ada_bwd")
    r = _adamw(w_ada[0], m_w_ada[0], v_w_ada[0], [g_ada], "adamw_w_ada")
    out_g['w_ada'], out_d['w_ada'], out_m['w_ada'], out_v['w_ada'] = [o[None] for o in r]

    return (loss, grad_x[None], *[out_g[n] for n in WEIGHTS], *[out_d[n] for n in WEIGHTS],
            *[out_m[n] for n in WEIGHTS], *[out_v[n] for n in WEIGHTS])
```
